```python
import functools
import jax, jax.numpy as jnp
from jax import lax
import numpy as np

D_MODEL = 1024
BATCH = 32
SEQ = 256
DEPTH = 2
DEC_BATCH = 2
DEC_SEQ = 1024
PAST_LEN = 256

GRID_W = 64
DH = 64
H_A = 8
KV_A = 2
H_B = 8
H_C = 8
H_D = 8
KV_D = 2
D_MIX_EVEN = (H_A + H_B) * DH
D_MIX_ODD = (H_C + H_D) * DH
D_FF = 2816
ADA_CHUNKS = 9
QBLOCK = 128
MLSTM_CHUNK = 64
NA_KH = 8
NA_KW = 16
NA_SLAB = 2 * NA_KW
SWA_WIN = 128
ROPE_THETA = 10000.0
ROPE_FREQS = DH // 4
ATTN_SCALE = DH ** -0.5
NEG_INF = -1e30
EPS = 1e-6
EVEN_SIZES = (H_A * DH, KV_A * DH, KV_A * DH, H_B * DH, H_B * DH, H_B * DH, 4 * H_B, H_B * DH)
ODD_SIZES = (H_C * DH, H_C * DH, H_C * DH, H_D * DH, KV_D * DH, KV_D * DH)

kernel_name = "hybrid_diffusion_prefix_trunk_step"


def rms_norm(x, g):
    xf = x.astype(jnp.float32)
    y = xf * lax.rsqrt(jnp.mean(xf * xf, axis=-1, keepdims=True) + EPS)
    return (y * g.astype(jnp.float32)).astype(x.dtype)


def modulate(h, shift, scale):
    return h * (1 + scale) + shift


def adaln(cond, w, b):
    m = jax.nn.silu(cond) @ w + b
    m = m.reshape(-1, 1, ADA_CHUNKS, D_MODEL)
    return [m[:, :, i] for i in range(ADA_CHUNKS)]


def swiglu(h, w_in, w_out):
    g, u = jnp.split(h @ w_in, 2, axis=-1)
    return (jax.nn.silu(g) * u) @ w_out


def split_cols(p, sizes):
    return jnp.split(p, np.cumsum(sizes)[:-1].tolist(), axis=-1)


def axial_rope(L, dtype):
    t = jnp.arange(L)
    pos = jnp.stack([t // GRID_W, t % GRID_W], axis=-1).astype(jnp.float32)
    freqs = ROPE_THETA ** (-jnp.arange(ROPE_FREQS, dtype=jnp.float32) / ROPE_FREQS)
    ang = (pos[:, :, None] * freqs).reshape(L, 2 * ROPE_FREQS)
    return jnp.cos(ang)[:, None, :].astype(dtype), jnp.sin(ang)[:, None, :].astype(dtype)


def apply_rope(x, cos, sin):
    x1, x2 = jnp.split(x, 2, axis=-1)
    return jnp.concatenate([x1 * cos - x2 * sin, x2 * cos + x1 * sin], axis=-1)


def attend_dense(q, k, v, sink=None):
    B, L, H, dh = q.shape
    KV = k.shape[2]
    G = H // KV
    nb = L // QBLOCK
    qb = q.reshape(B, nb, QBLOCK, KV, G, dh).swapaxes(0, 1)

    def block(qblk):
        s = jnp.einsum('bqkgd,bskd->bkgqs', qblk, k).astype(jnp.float32) * ATTN_SCALE
        if sink is None:
            p = jax.nn.softmax(s, axis=-1)
        else:
            sk = jnp.broadcast_to(sink.astype(jnp.float32).reshape(KV, G, 1, 1), s.shape[:-1] + (1,))
            p = jax.nn.softmax(jnp.concatenate([sk, s], axis=-1), axis=-1)[..., 1:]
        return jnp.einsum('bkgqs,bskd->bqkgd', p.astype(v.dtype), v)

    o = lax.map(block, qb)
    return o.swapaxes(0, 1).reshape(B, L, H * dh)


def swa_latent(q, k, v, k_ctx, v_ctx, sink):
    B, L, H, dh = q.shape
    KV = k.shape[2]
    G = H // KV
    nb = L // QBLOCK
    span = QBLOCK + 2 * SWA_WIN
    pad = ((0, 0), (SWA_WIN, SWA_WIN), (0, 0), (0, 0))
    kidx = jnp.arange(nb)[:, None] * QBLOCK + jnp.arange(span)[None, :]
    kb = jnp.pad(k, pad)[:, kidx]
    vb = jnp.pad(v, pad)[:, kidx]
    qb = q.reshape(B, nb, QBLOCK, KV, G, dh)
    s_loc = jnp.einsum('bnqkgd,bnskd->bnkgqs', qb, kb).astype(jnp.float32) * ATTN_SCALE
    qpos = jnp.arange(L).reshape(nb, QBLOCK)
    kpos = kidx - SWA_WIN
    ok = ((kpos[:, None, :] >= 0) & (kpos[:, None, :] < L)
          & (jnp.abs(kpos[:, None, :] - qpos[:, :, None]) <= SWA_WIN))
    s_loc = jnp.where(ok[None, :, None, None], s_loc, NEG_INF)
    s_ctx = jnp.einsum('bnqkgd,bskd->bnkgqs', qb, k_ctx).astype(jnp.float32) * ATTN_SCALE
    sk = jnp.broadcast_to(sink.astype(jnp.float32).reshape(1, 1, KV, G, 1, 1), s_ctx.shape[:-1] + (1,))
    p = jax.nn.softmax(jnp.concatenate([sk, s_ctx, s_loc], axis=-1), axis=-1).astype(v.dtype)
    P = k_ctx.shape[1]
    o = (jnp.einsum('bnkgqs,bskd->bnqkgd', p[..., 1:1 + P], v_ctx)
         + jnp.einsum('bnkgqs,bnskd->bnqkgd', p[..., 1 + P:], vb))
    return o.reshape(B, L, H * dh)


def na_latent(q, k, v, k_ctx, v_ctx, rpb):
    B, L, H, dh = q.shape
    rows = L // GRID_W
    kh = min(NA_KH, rows)
    ncb = GRID_W // NA_KW
    r = jnp.arange(rows)
    row_idx = jnp.clip(r - kh // 2, 0, rows - kh)[:, None] + jnp.arange(kh)[None, :]
    qcol = jnp.arange(GRID_W).reshape(ncb, NA_KW)
    col_idx = (jnp.clip(jnp.arange(ncb) * NA_KW - NA_KW // 2, 0, GRID_W - NA_SLAB)[:, None]
               + jnp.arange(NA_SLAB)[None, :])
    win_start = jnp.clip(qcol - NA_KW // 2, 0, GRID_W - NA_KW)
    col_ok = ((col_idx[:, None, :] >= win_start[..., None])
              & (col_idx[:, None, :] < win_start[..., None] + NA_KW))
    mask = jnp.broadcast_to(col_ok[:, :, None, :], (ncb, NA_KW, kh, NA_SLAB)).reshape(ncb, NA_KW, kh * NA_SLAB)
    gi = row_idx[:, None, :, None]
    gj = col_idx[None, :, None, :]
    kg = k.reshape(B, rows, GRID_W, H, dh)[:, gi, gj].reshape(B, rows, ncb, kh * NA_SLAB, H, dh)
    vg = v.reshape(B, rows, GRID_W, H, dh)[:, gi, gj].reshape(B, rows, ncb, kh * NA_SLAB, H, dh)
    qg = q.reshape(B, rows, ncb, NA_KW, H, dh)
    s_win = jnp.einsum('brnqhd,brnshd->brnhqs', qg, kg).astype(jnp.float32) * ATTN_SCALE
    dy = row_idx - r[:, None] + (NA_KH - 1)
    dx = jnp.clip(col_idx[:, None, :] - qcol[..., None] + (NA_KW - 1), 0, 2 * NA_KW - 2)
    bias = rpb.astype(jnp.float32)[:, dy[:, None, None, :, None], dx[None, :, :, None, :]]
    bias = bias.transpose(1, 2, 0, 3, 4, 5).reshape(rows, ncb, H, NA_KW, kh * NA_SLAB)
    s_win = jnp.where(mask[None, None, :, None], s_win + bias[None], NEG_INF)
    s_ctx = jnp.einsum('brnqhd,bshd->brnhqs', qg, k_ctx).astype(jnp.float32) * ATTN_SCALE
    p = jax.nn.softmax(jnp.concatenate([s_ctx, s_win], axis=-1), axis=-1).astype(v.dtype)
    P = k_ctx.shape[1]
    o = (jnp.einsum('brnhqs,bshd->brnqhd', p[..., :P], v_ctx)
         + jnp.einsum('brnhqs,brnshd->brnqhd', p[..., P:], vg))
    return o.reshape(B, L, H * dh)


def mlstm_scan(q, k, v, i_pre, f_pre, C0, n0, m0):
    B, L, H, dh = q.shape
    nc = L // MLSTM_CHUNK
    f32 = jnp.float32

    def chunks(a):
        return a.astype(f32).reshape((B, nc, MLSTM_CHUNK) + a.shape[2:]).swapaxes(0, 1)

    xs = (chunks(q), chunks(k) * (dh ** -0.5), chunks(v), chunks(i_pre),
          chunks(jax.nn.log_sigmoid(f_pre.astype(f32))))
    tri = jnp.tril(jnp.ones((MLSTM_CHUNK, MLSTM_CHUNK), dtype=bool))

    def step(carry, xc):
        C, n, m = carry
        qc, kc, vc, li, lf = xc
        li = li.swapaxes(1, 2)
        b = jnp.cumsum(lf.swapaxes(1, 2), axis=-1)
        d = jnp.where(tri, b[..., :, None] - b[..., None, :] + li[..., None, :], -jnp.inf)
        inter = b + m[..., None]
        mt = jnp.maximum(inter, d.max(axis=-1))
        w = jnp.exp(d - mt[..., None]) * jnp.einsum('bthd,bshd->bhts', qc, kc)
        w_inter = jnp.exp(inter - mt)
        num = (jnp.einsum('bhts,bshd->bhtd', w, vc)
               + w_inter[..., None] * jnp.einsum('bthk,bhkv->bhtv', qc, C))
        den = w.sum(axis=-1) + w_inter * jnp.einsum('bthk,bhk->bht', qc, n)
        h = num / jnp.maximum(jnp.abs(den), jnp.exp(-mt))[..., None]
        b_last = b[..., -1]
        dec = b_last[..., None] - b + li
        m_new = jnp.maximum(b_last + m, dec.max(axis=-1))
        ws = jnp.exp(dec - m_new[..., None])
        wc = jnp.exp(b_last + m - m_new)
        C_new = wc[..., None, None] * C + jnp.einsum('bhs,bshk,bshv->bhkv', ws, kc, vc)
        n_new = wc[..., None] * n + jnp.einsum('bhs,bshk->bhk', ws, kc)
        return (C_new, n_new, m_new), h.swapaxes(1, 2)

    (C, n, m), hs = lax.scan(step, (C0.astype(f32), n0.astype(f32), m0.astype(f32)), xs)
    return hs.swapaxes(0, 1).reshape(B, L, H, dh), C, n, m


def mlstm_bidir(qb, kb, vb, gates, ob, head_g, C0, n0, m0):
    B, L = qb.shape[:2]
    flip = lambda a: a[:, ::-1]
    hf, Cf, nf, mf = mlstm_scan(qb, kb, vb, gates[:, :, 0], gates[:, :, 1], C0[:, 0], n0[:, 0], m0[:, 0])
    hb, Cb, nb_, mb = mlstm_scan(flip(qb), flip(kb), flip(vb), flip(gates[:, :, 2]), flip(gates[:, :, 3]),
                                 C0[:, 1], n0[:, 1], m0[:, 1])
    h = (hf + flip(hb)).astype(qb.dtype)
    y = jax.nn.sigmoid(ob) * rms_norm(h, head_g.reshape(H_B, DH)).reshape(B, L, H_B * DH)
    return y, jnp.stack([Cf, Cb], axis=1), jnp.stack([nf, nb_], axis=1), jnp.stack([mf, mb], axis=1)


def even_project(h, w_in, qk_g, gate_b):
    B, L, _ = h.shape
    qa, ka, va, qb, kb, vb, gates, ob = split_cols(h @ w_in, EVEN_SIZES)
    qa = rms_norm(qa.reshape(B, L, H_A, DH), qk_g[0])
    ka = rms_norm(ka.reshape(B, L, KV_A, DH), qk_g[1])
    va = va.reshape(B, L, KV_A, DH)
    heads = lambda a: a.reshape(B, L, H_B, DH)
    gates = (gates + gate_b).reshape(B, L, 4, H_B)
    return qa, ka, va, heads(qb), heads(kb), heads(vb), gates, ob


def even_mixer_context(h, params):
    w_in, w_out, qk_g, gate_b, head_g = params
    B = h.shape[0]
    qa, ka, va, qb, kb, vb, gates, ob = even_project(h, w_in, qk_g, gate_b)
    ya = attend_dense(qa, ka, va)
    C0 = jnp.zeros((B, 2, H_B, DH, DH), jnp.float32)
    n0 = jnp.zeros((B, 2, H_B, DH), jnp.float32)
    m0 = jnp.zeros((B, 2, H_B), jnp.float32)
    yb, C, n, m = mlstm_bidir(qb, kb, vb, gates, ob, head_g, C0, n0, m0)
    return jnp.concatenate([ya, yb], axis=-1) @ w_out, (ka, va, C, n, m)


def even_mixer_latent(h, params, cache):
    w_in, w_out, qk_g, gate_b, head_g = params
    k_ctx, v_ctx, C0, n0, m0 = cache
    L = h.shape[1]
    qa, ka, va, qb, kb, vb, gates, ob = even_project(h, w_in, qk_g, gate_b)
    cos, sin = axial_rope(L, qa.dtype)
    qa, ka = apply_rope(qa, cos, sin), apply_rope(ka, cos, sin)
    ya = attend_dense(qa, jnp.concatenate([k_ctx.astype(ka.dtype), ka], axis=1),
                      jnp.concatenate([v_ctx.astype(va.dtype), va], axis=1))
    yb = mlstm_bidir(qb, kb, vb, gates, ob, head_g, C0, n0, m0)[0]
    return jnp.concatenate([ya, yb], axis=-1) @ w_out, ()


def odd_project(h, w_in):
    B, L, _ = h.shape
    qc, kc, vc, qd, kd, vd = split_cols(h @ w_in, ODD_SIZES)
    c_heads = lambda a: a.reshape(B, L, H_C, DH)
    return (c_heads(qc), c_heads(kc), c_heads(vc), qd.reshape(B, L, H_D, DH),
            kd.reshape(B, L, KV_D, DH), vd.reshape(B, L, KV_D, DH))


def odd_mixer_context(h, params):
    w_in, w_out, rpb, sink = params
    qc, kc, vc, qd, kd, vd = odd_project(h, w_in)
    yc = attend_dense(qc, kc, vc)
    yd = attend_dense(qd, kd, vd, sink)
    return jnp.concatenate([yc, yd], axis=-1) @ w_out, (kc, vc, kd, vd)


def odd_mixer_latent(h, params, cache):
    w_in, w_out, rpb, sink = params
    kc_ctx, vc_ctx, kd_ctx, vd_ctx = cache
    L = h.shape[1]
    qc, kc, vc, qd, kd, vd = odd_project(h, w_in)
    yc = na_latent(qc, kc, vc, kc_ctx.astype(kc.dtype), vc_ctx.astype(vc.dtype), rpb)
    cos, sin = axial_rope(L, qd.dtype)
    yd = swa_latent(apply_rope(qd, cos, sin), apply_rope(kd, cos, sin), vd,
                    kd_ctx.astype(kd.dtype), vd_ctx.astype(vd.dtype), sink)
    return jnp.concatenate([yc, yd], axis=-1) @ w_out, ()


def trunk_layer(x, mod, norm_g, f1_in, f1_out, f2_in, f2_out, mixer):
    sh1, sc1, g1, sh2, sc2, g2, sh3, sc3, g3 = mod
    x = x + 0.5 * g1 * swiglu(modulate(rms_norm(x, norm_g[0]), sh1, sc1), f1_in, f1_out)
    mo, st = mixer(modulate(rms_norm(x, norm_g[1]), sh2, sc2))
    x = x + g2 * mo
    x = x + 0.5 * g3 * swiglu(modulate(rms_norm(x, norm_g[2]), sh3, sc3), f2_in, f2_out)
    return x, st


def setup_inputs(seed: int = 0) -> dict:
    key = jax.random.key(seed)
    ks = iter(list(jax.random.split(key, 80)))

    def rnd(shape, scale, offset=0.0):
        return offset + scale * jax.random.normal(next(ks), shape, jnp.float32)

    D = D_MODEL
    inp = {}
    inp['x_prompt'] = rnd((BATCH, SEQ, D), 1.0)
    inp['x_sample'] = rnd((DEC_BATCH, DEC_SEQ, D), 1.0)
    inp['cache_l0_attn_k'] = rnd((DEC_BATCH, PAST_LEN, KV_A, DH), 1.0)
    inp['cache_l0_attn_v'] = rnd((DEC_BATCH, PAST_LEN, KV_A, DH), 1.0)
    inp['state_l0_mlstm_C'] = rnd((DEC_BATCH, 2, H_B, DH, DH), 0.3)
    inp['state_l0_mlstm_n'] = rnd((DEC_BATCH, 2, H_B, DH), 0.3)
    inp['state_l0_mlstm_m'] = rnd((DEC_BATCH, 2, H_B), 1.0)
    inp['cache_l1_na_k'] = rnd((DEC_BATCH, PAST_LEN, H_C, DH), 1.0)
    inp['cache_l1_na_v'] = rnd((DEC_BATCH, PAST_LEN, H_C, DH), 1.0)
    inp['cache_l1_swa_k'] = rnd((DEC_BATCH, PAST_LEN, KV_D, DH), 1.0)
    inp['cache_l1_swa_v'] = rnd((DEC_BATCH, PAST_LEN, KV_D, DH), 1.0)
    inp['c'] = rnd((DEC_BATCH, D), 1.0)
    inp['c_ctx'] = rnd((D,), 1.0)
    inp['norm_final'] = rnd((D,), 0.05, 1.0)
    for l, (p_in, d_mix) in enumerate(((sum(EVEN_SIZES), D_MIX_EVEN), (sum(ODD_SIZES), D_MIX_ODD))):
        inp[f'ada_w_l{l}'] = rnd((D, ADA_CHUNKS * D), 0.5 * D ** -0.5)
        inp[f'ada_b_l{l}'] = rnd((ADA_CHUNKS * D,), 0.1)
        inp[f'norm_l{l}'] = rnd((3, D), 0.05, 1.0)
        inp[f'ffn1_in_l{l}'] = rnd((D, 2 * D_FF), D ** -0.5)
        inp[f'ffn1_out_l{l}'] = rnd((D_FF, D), D_FF ** -0.5)
        inp[f'ffn2_in_l{l}'] = rnd((D, 2 * D_FF), D ** -0.5)
        inp[f'ffn2_out_l{l}'] = rnd((D_FF, D), D_FF ** -0.5)
        inp[f'mix_in_l{l}'] = rnd((D, p_in), D ** -0.5)
        inp[f'mix_out_l{l}'] = rnd((d_mix, D), d_mix ** -0.5)
        if l == 0:
            inp['qk_norm_l0'] = rnd((2, DH), 0.05, 1.0)
            inp['gate_bias_l0'] = jnp.concatenate([rnd((H_B,), 0.1, -1.0), rnd((H_B,), 0.1, 3.0),
                                                   rnd((H_B,), 0.1, -1.0), rnd((H_B,), 0.1, 3.0)])
            inp['head_norm_l0'] = rnd((H_B * DH,), 0.05, 1.0)
        else:
            inp['rpb_l1'] = rnd((H_C, 2 * NA_KH - 1, 2 * NA_KW - 1), 0.1)
            inp['sink_l1'] = rnd((H_D,), 0.5)
    return inp


def reference(x_prompt, x_sample, cache_l0_attn_k, cache_l0_attn_v, state_l0_mlstm_C, state_l0_mlstm_n,
              state_l0_mlstm_m, cache_l1_na_k, cache_l1_na_v, cache_l1_swa_k, cache_l1_swa_v, c, c_ctx,
              norm_final,
              ada_w_l0, ada_b_l0, norm_l0, ffn1_in_l0, ffn1_out_l0, ffn2_in_l0, ffn2_out_l0,
              mix_in_l0, mix_out_l0, qk_norm_l0, gate_bias_l0, head_norm_l0,
              ada_w_l1, ada_b_l1, norm_l1, ffn1_in_l1, ffn1_out_l1, ffn2_in_l1, ffn2_out_l1,
              mix_in_l1, mix_out_l1, rpb_l1, sink_l1):
    common = ((ada_w_l0, ada_b_l0, norm_l0, ffn1_in_l0, ffn1_out_l0, ffn2_in_l0, ffn2_out_l0),
              (ada_w_l1, ada_b_l1, norm_l1, ffn1_in_l1, ffn1_out_l1, ffn2_in_l1, ffn2_out_l1))
    mixers = ((mix_in_l0, mix_out_l0, qk_norm_l0, gate_bias_l0, head_norm_l0),
              (mix_in_l1, mix_out_l1, rpb_l1, sink_l1))
    caches = ((cache_l0_attn_k, cache_l0_attn_v, state_l0_mlstm_C, state_l0_mlstm_n, state_l0_mlstm_m),
              (cache_l1_na_k, cache_l1_na_v, cache_l1_swa_k, cache_l1_swa_v))
    xp, xs = x_prompt, x_sample
    ctx_states = []
    for l in range(DEPTH):
        ada_w, ada_b, norm_g, f1_in, f1_out, f2_in, f2_out = common[l]
        even = l % 2 == 0
        ctx_mixer = functools.partial(even_mixer_context if even else odd_mixer_context, params=mixers[l])
        lat_mixer = functools.partial(even_mixer_latent if even else odd_mixer_latent,
                                      params=mixers[l], cache=caches[l])
        xp, st = trunk_layer(xp, adaln(c_ctx, ada_w, ada_b), norm_g, f1_in, f1_out, f2_in, f2_out, ctx_mixer)
        xs, _ = trunk_layer(xs, adaln(c, ada_w, ada_b), norm_g, f1_in, f1_out, f2_in, f2_out, lat_mixer)
        ctx_states.append(st)
    y_prompt = rms_norm(xp, norm_final)
    y_sample = rms_norm(xs, norm_final)
    (k0, v0, C0, n0, m0), (kc1, vc1, kd1, vd1) = ctx_states
    return (y_prompt, y_sample, k0, v0, C0, n0, m0, kc1, vc1, kd1, vd1)
```

```python
import functools

import jax
import jax.numpy as jnp
import numpy as np
from jax import lax
from jax.experimental import pallas as pl
from jax.experimental.pallas import tpu as pltpu

F32 = jnp.float32
BF16 = jnp.bfloat16

D_MODEL = 1024
DH = 64
D_FF = 2816
ADA_CHUNKS = 9
GRID_W = 64
LAT_LEN = 1024
CTX_LEN = 256
N_HEADS = 8
NA_KH = 8
NA_KW = 16
SWA_WIN = 128
ROPE_THETA = 10000.0
ROPE_FREQS = DH // 4
ATTN_SCALE = DH ** -0.5
NEG_INF = -1e30
EPS = 1e-6

TOKEN_TILE = 512
MLSTM_T = 256
FFN_CHUNKS = ((0, 1024), (1024, 2048), (2048, D_FF))
VMEM_LIMIT = 56 * 1024 * 1024


def _params(sem, vmem=VMEM_LIMIT):
    return pltpu.CompilerParams(dimension_semantics=sem, vmem_limit_bytes=vmem)


def _dot(a, b):
    return jnp.dot(a.astype(BF16), b.astype(BF16), preferred_element_type=F32)


def _dot_nt(a, b):
    return lax.dot_general(a.astype(BF16), b.astype(BF16), (((1,), (1,)), ((), ())),
                           preferred_element_type=F32)


def _dot_tn(a, b):
    return lax.dot_general(a.astype(BF16), b.astype(BF16), (((0,), (0,)), ((), ())),
                           preferred_element_type=F32)


def _split3(x):
    hi = x.astype(BF16)
    r1 = x - hi.astype(F32)
    mid = r1.astype(BF16)
    lo = (r1 - mid.astype(F32)).astype(BF16)
    return hi, mid, lo


def _dot_exact_rhs(a_bf16, x):
    hi, mid, lo = _split3(x)
    f = lambda p: jnp.dot(a_bf16, p, preferred_element_type=F32)
    return f(hi) + f(mid) + f(lo)


def _dot_exact_lhs(x, b_bf16):
    hi, mid, lo = _split3(x)
    f = lambda p: jnp.dot(p, b_bf16, preferred_element_type=F32)
    return f(hi) + f(mid) + f(lo)


def _silu(x):
    return x * jax.nn.sigmoid(x)


def _log_sigmoid(x):
    return jnp.minimum(x, 0.0) - jnp.log1p(jnp.exp(-jnp.abs(x)))


def _rms(x, g):
    return x * lax.rsqrt(jnp.mean(x * x, axis=-1, keepdims=True) + EPS) * g


def _head_mean_sq(x, bd_ref):
    n = x.shape[-1]
    return _dot_exact_lhs(x * x, bd_ref[:n, :n])


def _rope(x, cos, sin_signed):
    n = x.shape[-1]
    lane = lax.broadcasted_iota(jnp.int32, x.shape, 1)
    first_half = (lane % DH) < (DH // 2)
    partner = jnp.where(first_half, pltpu.roll(x, n - DH // 2, 1), pltpu.roll(x, DH // 2, 1))
    return x * cos[:, :n] + partner * sin_signed[:, :n]


def _adaln_kernel(c_ref, w_ref, b_ref, o_ref):
    s = _silu(c_ref[...])
    o_ref[...] = _dot(s, w_ref[...]) + b_ref[...]


def _adaln(cond8, w, b):
    n = w.shape[1]
    tn = 1536
    return pl.pallas_call(
        _adaln_kernel,
        out_shape=jax.ShapeDtypeStruct((8, n), F32),
        grid=(n // tn,),
        in_specs=[pl.BlockSpec((8, D_MODEL), lambda j: (0, 0)),
                  pl.BlockSpec((D_MODEL, tn), lambda j: (0, j)),
                  pl.BlockSpec((1, tn), lambda j: (0, j))],
        out_specs=pl.BlockSpec((8, tn), lambda j: (0, j)),
        compiler_params=_params(("arbitrary",)),
        name="adaln",
    )(cond8, w, b.reshape(1, n))


def _const_spec(shape):
    nd = len(shape)
    return pl.BlockSpec(shape, lambda i, _n=nd: (0,) * _n, pipeline_mode=pl.Buffered(1))


def _mod_spec(chunk, rowfn):
    return pl.BlockSpec((1, 1, D_MODEL), lambda i: (rowfn(i) * ADA_CHUNKS + chunk, 0, 0))


def _rowfn(latent):
    if latent:
        return lambda i: 1 + (i * TOKEN_TILE) // LAT_LEN
    return lambda i: 0


def _tok_spec(width):
    return pl.BlockSpec((TOKEN_TILE, width), lambda i: (i, 0))


def _modnorm(x, ng_ref, sh_ref, sc_ref):
    return _rms(x, ng_ref[...]) * (1.0 + sc_ref[0]) + sh_ref[0]


def _ffn_kernel(*refs, has_mix, final_norm):
    it = iter(refs)
    x_ref = next(it)
    if has_mix:
        ya_ref, yb_ref, woa_ref, wob_ref, g2_ref = (next(it) for _ in range(5))
    ng_ref, sh_ref, sc_ref, g_ref, win_ref, wout_ref = (next(it) for _ in range(6))
    if final_norm:
        nf_ref = next(it)
    o_ref = next(it)

    x = x_ref[...]
    if has_mix:
        mo = _dot(ya_ref[...], woa_ref[...]) + _dot(yb_ref[...], wob_ref[...])
        x = x + g2_ref[0] * mo
    h = _modnorm(x, ng_ref, sh_ref, sc_ref).astype(BF16)
    acc = None
    for lo, hi in FFN_CHUNKS:
        g = _dot(h, win_ref[:, lo:hi])
        u = _dot(h, win_ref[:, D_FF + lo:D_FF + hi])
        part = _dot(_silu(g) * u, wout_ref[lo:hi, :])
        acc = part if acc is None else acc + part
    y = x + (0.5 * g_ref[0]) * acc
    if final_norm:
        y = _rms(y, nf_ref[...])
    o_ref[...] = y


def _ffn(x, mods, mod_base, latent, ng, w_in, w_out, mix=None, final_g=None):
    t = x.shape[0]
    rowfn = _rowfn(latent)
    args, specs = [x], [_tok_spec(D_MODEL)]
    if mix is not None:
        ya, yb, woa, wob = mix
        args += [ya, yb, woa, wob, mods]
        specs += [_tok_spec(ya.shape[1]), _tok_spec(yb.shape[1]), _const_spec(woa.shape), _const_spec(wob.shape),
                  _mod_spec(mod_base - 1, rowfn)]
    args += [ng.reshape(1, D_MODEL), mods, mods, mods, w_in, w_out]
    specs += [_const_spec((1, D_MODEL)), _mod_spec(mod_base, rowfn), _mod_spec(mod_base + 1, rowfn),
              _mod_spec(mod_base + 2, rowfn), _const_spec(w_in.shape), _const_spec(w_out.shape)]
    if final_g is not None:
        args.append(final_g.reshape(1, D_MODEL))
        specs.append(_const_spec((1, D_MODEL)))
    return pl.pallas_call(
        functools.partial(_ffn_kernel, has_mix=mix is not None, final_norm=final_g is not None),
        out_shape=jax.ShapeDtypeStruct((t, D_MODEL), F32),
        grid=(t // TOKEN_TILE,),
        in_specs=specs,
        out_specs=_tok_spec(D_MODEL),
        compiler_params=_params(("arbitrary",)),
        name="ffn",
    )(*args)


_EV_QA, _EV_KA, _EV_VA, _EV_QB, _EV_KB, _EV_VB, _EV_OB, _EV_G, _EV_END = 0, 512, 640, 768, 1280, 1792, 2304, 2816, 2944
N_GATES = 4 * N_HEADS


def _proj_even_kernel(*refs, rope):
    it = iter(refs)
    x_ref, ng_ref, sh_ref, sc_ref, w_ref, wgt_ref, gb_ref, gbt_ref, qg_ref, kg_ref, bd_ref = (next(it) for _ in range(11))
    if rope:
        cos_ref, sin_ref = next(it), next(it)
    qa_ref, ka_ref, va_ref, qb_ref, kb_ref, vb_ref, ob_ref, g_ref, gt_ref = (next(it) for _ in range(9))

    h = _modnorm(x_ref[...], ng_ref, sh_ref, sc_ref).astype(BF16)
    proj = lambda lo, hi: _dot(h, w_ref[:, lo:hi])

    qa = proj(_EV_QA, _EV_KA)
    qa = qa * lax.rsqrt(_head_mean_sq(qa, bd_ref) + EPS) * qg_ref[...]
    ka = proj(_EV_KA, _EV_VA)
    ka = ka * lax.rsqrt(_head_mean_sq(ka, bd_ref) + EPS) * kg_ref[...]
    if rope:
        qa = _rope(qa, cos_ref[...], sin_ref[...])
        ka = _rope(ka, cos_ref[...], sin_ref[...])
    qa_ref[...] = qa * ATTN_SCALE
    ka_ref[...] = ka
    va_ref[...] = proj(_EV_VA, _EV_QB)
    qb_ref[...] = proj(_EV_QB, _EV_KB)
    kb_ref[...] = proj(_EV_KB, _EV_VB) * ATTN_SCALE
    vb_ref[...] = proj(_EV_VB, _EV_OB)
    ob_ref[...] = proj(_EV_OB, _EV_G)
    g_ref[...] = proj(_EV_G, _EV_END) + gb_ref[...]
    for j in range(TOKEN_TILE // MLSTM_T):
        gt_ref[j] = _dot_nt(wgt_ref[...], h[j * MLSTM_T:(j + 1) * MLSTM_T]) + gbt_ref[...]


def _proj_even(x, mods, latent, ng, w, wgt, gb, gbt, qg, kg, bd, rope_tabs):
    t = x.shape[0]
    rowfn = _rowfn(latent)
    args = [x, ng.reshape(1, D_MODEL), mods, mods, w, wgt, gb, gbt, qg, kg, bd]
    specs = [_tok_spec(D_MODEL), _const_spec((1, D_MODEL)), _mod_spec(3, rowfn), _mod_spec(4, rowfn),
             _const_spec(w.shape), _const_spec(wgt.shape), _const_spec(gb.shape), _const_spec(gbt.shape),
             _const_spec(qg.shape), _const_spec(kg.shape), _const_spec(bd.shape)]
    if latent:
        nblk = LAT_LEN // TOKEN_TILE
        args += list(rope_tabs)
        specs += [pl.BlockSpec((TOKEN_TILE, 512), lambda i: (i % nblk, 0))] * 2
    widths = (512, 128, 128, 512, 512, 512, 512, 128)
    out_shape = [jax.ShapeDtypeStruct((t, wd), F32) for wd in widths]
    out_specs = [_tok_spec(wd) for wd in widths]
    nj = TOKEN_TILE // MLSTM_T
    out_shape.append(jax.ShapeDtypeStruct((t // MLSTM_T, N_GATES, MLSTM_T), F32))
    out_specs.append(pl.BlockSpec((nj, N_GATES, MLSTM_T), lambda i: (i, 0, 0)))
    return pl.pallas_call(
        functools.partial(_proj_even_kernel, rope=latent),
        out_shape=out_shape, grid=(t // TOKEN_TILE,), in_specs=specs, out_specs=out_specs,
        compiler_params=_params(("arbitrary",)), name="proj_even",
    )(*args)


def _proj_odd_kernel(*refs, rope):
    it = iter(refs)
    x_ref, ng_ref, sh_ref, sc_ref, w_ref = (next(it) for _ in range(5))
    if rope:
        cos_ref, sin_ref = next(it), next(it)
    qc_ref, kc_ref, vc_ref, qd_ref, kd_ref, vd_ref = (next(it) for _ in range(6))

    h = _modnorm(x_ref[...], ng_ref, sh_ref, sc_ref).astype(BF16)
    proj = lambda lo, hi: _dot(h, w_ref[:, lo:hi])
    qc_ref[...] = proj(0, 512) * ATTN_SCALE
    kc_ref[...] = proj(512, 1024)
    vc_ref[...] = proj(1024, 1536)
    qd = proj(1536, 2048)
    kd = proj(2048, 2176)
    if rope:
        qd = _rope(qd, cos_ref[...], sin_ref[...])
        kd = _rope(kd, cos_ref[...], sin_ref[...])
    qd_ref[...] = qd * ATTN_SCALE
    kd_ref[...] = kd
    vd_ref[...] = proj(2176, 2304)


def _proj_odd(x, mods, latent, ng, w, rope_tabs):
    t = x.shape[0]
    rowfn = _rowfn(latent)
    args = [x, ng.reshape(1, D_MODEL), mods, mods, w]
    specs = [_tok_spec(D_MODEL), _const_spec((1, D_MODEL)), _mod_spec(3, rowfn), _mod_spec(4, rowfn), _const_spec(w.shape)]
    if latent:
        nblk = LAT_LEN // TOKEN_TILE
        args += list(rope_tabs)
        specs += [pl.BlockSpec((TOKEN_TILE, 512), lambda i: (i % nblk, 0))] * 2
    widths = (512, 512, 512, 512, 128, 128)
    return pl.pallas_call(
        functools.partial(_proj_odd_kernel, rope=latent),
        out_shape=[jax.ShapeDtypeStruct((t, wd), F32) for wd in widths],
        grid=(t // TOKEN_TILE,), in_specs=specs, out_specs=[_tok_spec(wd) for wd in widths],
        compiler_params=_params(("arbitrary",)), name="proj_odd",
    )(*args)


def _attn_kernel(*refs, group, has_extra, has_sink):
    it = iter(refs)
    q_ref, k_ref, v_ref = next(it), next(it), next(it)
    if has_extra:
        ke_ref, ve_ref = next(it), next(it)
    if has_sink:
        sink_ref = next(it)
    o_ref = next(it)
    for hd in range(N_HEADS):
        kv = hd // group
        hs = slice(hd * DH, (hd + 1) * DH)
        ks = slice(kv * DH, (kv + 1) * DH)
        q = q_ref[:, hs].astype(BF16)
        s = _dot_nt(q, k_ref[:, ks])
        m = jnp.max(s, axis=-1, keepdims=True)
        if has_extra:
            se = _dot_nt(q, ke_ref[:, ks])
            m = jnp.maximum(m, jnp.max(se, axis=-1, keepdims=True))
        if has_sink:
            m = jnp.maximum(m, sink_ref[hd])
        p = jnp.exp(s - m)
        den = jnp.sum(p, axis=-1, keepdims=True)
        o = _dot(p, v_ref[:, ks])
        if has_extra:
            pe = jnp.exp(se - m)
            den = den + jnp.sum(pe, axis=-1, keepdims=True)
            o = o + _dot(pe, ve_ref[:, ks])
        if has_sink:
            den = den + jnp.exp(sink_ref[hd] - m)
        o_ref[:, hs] = o / den


def _attention(q, k, v, *, n_batch, q_len, kv_len, q_tile, group, extra=None, sink=None):
    nq = q_len // q_tile
    kvw = k.shape[1]
    args = [q, k, v]
    specs = [pl.BlockSpec((q_tile, 512), lambda b, j: (b * nq + j, 0)),
             pl.BlockSpec((kv_len, kvw), lambda b, j: (b, 0)),
             pl.BlockSpec((kv_len, kvw), lambda b, j: (b, 0))]
    if extra is not None:
        e_len = extra[0].shape[0] // n_batch
        args += list(extra)
        specs += [pl.BlockSpec((e_len, kvw), lambda b, j: (b, 0))] * 2
    if sink is not None:
        args.append(sink)
        specs.append(pl.BlockSpec(memory_space=pltpu.SMEM))
    return pl.pallas_call(
        functools.partial(_attn_kernel, group=group, has_extra=extra is not None, has_sink=sink is not None),
        out_shape=jax.ShapeDtypeStruct(q.shape, F32),
        grid=(n_batch, nq), in_specs=specs,
        out_specs=pl.BlockSpec((q_tile, 512), lambda b, j: (b * nq + j, 0)),
        compiler_params=_params(("arbitrary", "arbitrary")), name="attn",
    )(*args)


def _mlstm_kernel(*refs, n_chunks, zero_init):
    it = iter(refs)
    q_ref, k_ref, v_ref, ob_ref, g_ref, gt_ref, hg_ref, bd_ref = (next(it) for _ in range(8))
    if not zero_init:
        c0_ref, n0_ref, m0_ref = next(it), next(it), next(it)
    y_ref, c_ref, n_ref, m_ref, hf_s, hb_s = (next(it) for _ in range(6))
    T = MLSTM_T
    step = pl.program_id(1)
    carry = not (zero_init and n_chunks == 1)

    if not zero_init:
        @pl.when(step == 0)
        def _():
            c_ref[...] = c0_ref[...]
            n_ref[...] = n0_ref[...]
            m_ref[...] = m0_ref[...]
    elif carry:
        @pl.when(step == 0)
        def _():
            c_ref[...] = jnp.zeros_like(c_ref)
            n_ref[...] = jnp.zeros_like(n_ref)
            m_ref[...] = jnp.zeros_like(m_ref)

    row = lax.broadcasted_iota(jnp.int32, (T, T), 0)
    col = lax.broadcasted_iota(jnp.int32, (T, T), 1)
    lower = row >= col
    upper = row <= col
    lower_m = jnp.where(lower, 1.0, 0.0).astype(BF16)
    upper_m = jnp.where(upper, 1.0, 0.0).astype(BF16)

    for d in range(2):
        chunk = step if d == 0 else n_chunks - 1 - step
        t0 = pl.multiple_of(chunk * T, T)
        rows = pl.ds(t0, T)
        gates = g_ref[rows, :]
        gates_t = gt_ref[chunk]
        lf, lf_t = _log_sigmoid(gates), _log_sigmoid(gates_t)
        if d == 0:
            b_cols = _dot_exact_rhs(lower_m, lf)
            b_rows = _dot_exact_lhs(lf_t, upper_m)
            mask = lower
        else:
            b_cols = _dot_exact_rhs(upper_m, lf)
            b_rows = _dot_exact_lhs(lf_t, lower_m)
            mask = upper
        h_s = hf_s if d == 0 else hb_s
        for hd in range(N_HEADS):
            hs = slice(hd * DH, (hd + 1) * DH)
            ic, fc = 16 * d + hd, 16 * d + 8 + hd
            r = d * N_HEADS + hd
            b_col, b_row = b_cols[:, fc:fc + 1], b_rows[fc:fc + 1, :]
            li_col, li_row = gates[:, ic:ic + 1], gates_t[ic:ic + 1, :]
            q, k, v = q_ref[rows, hs], k_ref[rows, hs], v_ref[rows, hs]
            qk = _dot_nt(q, k)
            dmat = jnp.where(mask, b_col - b_row + li_row, -jnp.inf)
            dmax = jnp.max(dmat, axis=-1, keepdims=True)
            if carry:
                m_prev = m_ref[0, r:r + 1, 0:1]
                c_prev = c_ref[0, d, hd]
                n_prev = n_ref[0, d, hd:hd + 1, :]
                inter = b_col + m_prev
            else:
                inter = b_col
            mt = jnp.maximum(inter, dmax)
            w = jnp.exp(dmat - mt) * qk
            num = _dot(w, v)
            den = jnp.sum(w, axis=-1, keepdims=True)
            if carry:
                w_inter = jnp.exp(inter - mt)
                num = num + w_inter * _dot(q, c_prev)
                den = den + w_inter * jnp.sum(q * n_prev, axis=-1, keepdims=True)
            h_s[rows, hs] = num / jnp.maximum(jnp.abs(den), jnp.exp(-mt))

            b_last = b_row[:, T - 1:T] if d == 0 else b_row[:, 0:1]
            m_new = jnp.max(b_last - b_row + li_row, axis=-1, keepdims=True)
            m_new = jnp.maximum(m_new, b_last + m_prev if carry else b_last)
            kw = k * jnp.exp(b_last - b_col + li_col - m_new)
            c_new = _dot_tn(kw, v)
            n_new = jnp.sum(kw, axis=0, keepdims=True)
            if carry:
                wc = jnp.exp(b_last + m_prev - m_new)
                c_new = c_new + wc * c_prev
                n_new = n_new + wc * n_prev
            c_ref[0, d, hd] = c_new
            n_ref[0, d, hd:hd + 1, :] = n_new
            m_ref[0, r:r + 1, :] = jnp.broadcast_to(m_new, (1, 128))

    @pl.when(step == n_chunks - 1)
    def _():
        hsum = hf_s[...] + hb_s[...]
        y = hsum * lax.rsqrt(_head_mean_sq(hsum, bd_ref) + EPS) * hg_ref[...]
        y_ref[...] = jax.nn.sigmoid(ob_ref[...]) * y


def _mlstm(qb, kb, vb, ob, gates, gates_t, head_g, bd, *, n_batch, seq_len, state=None):
    n_chunks = seq_len // MLSTM_T
    seq = lambda wd: pl.BlockSpec((seq_len, wd), lambda b, c: (b, 0))
    args = [qb, kb, vb, ob, gates, gates_t, head_g, bd]
    specs = [seq(512), seq(512), seq(512), seq(512), seq(128),
             pl.BlockSpec((n_chunks, N_GATES, MLSTM_T), lambda b, c: (b, 0, 0)),
             pl.BlockSpec((1, 512), lambda b, c: (0, 0)), pl.BlockSpec(bd.shape, lambda b, c: (0, 0))]
    c_spec = pl.BlockSpec((1, 2, N_HEADS, DH, DH), lambda b, c: (b, 0, 0, 0, 0))
    n_spec = pl.BlockSpec((1, 2, N_HEADS, DH), lambda b, c: (b, 0, 0, 0))
    m_spec = pl.BlockSpec((1, 2 * N_HEADS, 128), lambda b, c: (b, 0, 0))
    if state is not None:
        args += list(state)
        specs += [c_spec, n_spec, m_spec]
    return pl.pallas_call(
        functools.partial(_mlstm_kernel, n_chunks=n_chunks, zero_init=state is None),
        out_shape=[jax.ShapeDtypeStruct((n_batch * seq_len, 512), F32),
                   jax.ShapeDtypeStruct((n_batch, 2, N_HEADS, DH, DH), F32),
                   jax.ShapeDtypeStruct((n_batch, 2, N_HEADS, DH), F32),
                   jax.ShapeDtypeStruct((n_batch, 2 * N_HEADS, 128), F32)],
        grid=(n_batch, n_chunks), in_specs=specs,
        out_specs=[seq(512), c_spec, n_spec, m_spec],
        scratch_shapes=[pltpu.VMEM((seq_len, 512), F32), pltpu.VMEM((seq_len, 512), F32)],
        compiler_params=_params(("arbitrary", "arbitrary")), name="mlstm",
    )(*args)


N_ROWS = LAT_LEN // GRID_W
N_DY = 2 * NA_KH - 1
N_DX = 2 * NA_KW - 1


def _na_kernel(q_ref, k_ref, v_ref, ke_ref, ve_ref, rpb_ref, o_ref, tile_s, slab_s):
    @pl.when(pl.program_id(0) == 0)
    def _():
        qc = lax.broadcasted_iota(jnp.int32, (GRID_W, GRID_W), 0)
        kc = lax.broadcasted_iota(jnp.int32, (GRID_W, GRID_W), 1)
        start = jnp.clip(qc - NA_KW // 2, 0, GRID_W - NA_KW)
        in_win = (kc >= start) & (kc < start + NA_KW)
        dx = kc - qc + (NA_KW - 1)

        def build_tile(idx, carry):
            tile = jnp.full((GRID_W, GRID_W), NEG_INF, F32)
            for j in range(N_DX):
                tile = jnp.where(dx == j, rpb_ref[idx * N_DX + j], tile)
            tile_s[idx] = jnp.where(in_win, tile, NEG_INF)
            return carry

        lax.fori_loop(0, N_HEADS * N_DY, build_tile, 0)

        def build_slab(idx, carry):
            first = (idx // NA_KH) * N_DY + idx % NA_KH
            for j in range(NA_KH):
                slab_s[idx, :, j * GRID_W:(j + 1) * GRID_W] = tile_s[first + j]
            return carry

        lax.fori_loop(0, N_HEADS * NA_KH, build_slab, 0)

    def row_body(r, carry):
        r0 = jnp.clip(r - NA_KH // 2, 0, N_ROWS - NA_KH)
        dy0 = r0 - r + NA_KH - 1
        qrows = pl.ds(pl.multiple_of(r * GRID_W, GRID_W), GRID_W)
        keys = pl.ds(pl.multiple_of(r0 * GRID_W, GRID_W), NA_KH * GRID_W)
        for hd in range(N_HEADS):
            hs = slice(hd * DH, (hd + 1) * DH)
            q = q_ref[qrows, hs].astype(BF16)
            s = _dot_nt(q, k_ref[keys, hs]) + slab_s[hd * NA_KH + dy0]
            se = _dot_nt(q, ke_ref[:, hs])
            m = jnp.maximum(jnp.max(s, axis=-1, keepdims=True), jnp.max(se, axis=-1, keepdims=True))
            p, pe = jnp.exp(s - m), jnp.exp(se - m)
            den = jnp.sum(p, axis=-1, keepdims=True) + jnp.sum(pe, axis=-1, keepdims=True)
            o_ref[qrows, hs] = (_dot(p, v_ref[keys, hs]) + _dot(pe, ve_ref[:, hs])) / den
        return carry

    lax.fori_loop(0, N_ROWS, row_body, 0)


def _na_latent(q, k, v, k_ctx, v_ctx, rpb_flat, n_batch):
    seq = pl.BlockSpec((LAT_LEN, 512), lambda b: (b, 0))
    ctx = pl.BlockSpec((CTX_LEN, 512), lambda b: (b, 0))
    return pl.pallas_call(
        _na_kernel,
        out_shape=jax.ShapeDtypeStruct(q.shape, F32),
        grid=(n_batch,),
        in_specs=[seq, seq, seq, ctx, ctx, pl.BlockSpec(memory_space=pltpu.SMEM)],
        out_specs=seq,
        scratch_shapes=[pltpu.VMEM((N_HEADS * N_DY, GRID_W, GRID_W), F32),
                        pltpu.VMEM((N_HEADS * NA_KH, GRID_W, NA_KH * GRID_W), F32)],
        compiler_params=_params(("arbitrary",)), name="na_latent",
    )(q, k, v, k_ctx, v_ctx, rpb_flat)


SWA_QB = 128


def _swa_kernel(q_ref, k_ref, v_ref, ke_ref, ve_ref, sink_ref, o_ref, *, group):
    span = SWA_QB + 2 * SWA_WIN

    def block_body(n, carry):
        q0 = n * SWA_QB
        lo = pl.multiple_of(jnp.clip(q0 - SWA_WIN, 0, LAT_LEN - span), SWA_QB)
        qrows = pl.ds(pl.multiple_of(q0, SWA_QB), SWA_QB)
        keys = pl.ds(lo, span)
        qpos = q0 + lax.broadcasted_iota(jnp.int32, (SWA_QB, span), 0)
        kpos = lo + lax.broadcasted_iota(jnp.int32, (SWA_QB, span), 1)
        near = jnp.abs(kpos - qpos) <= SWA_WIN
        for hd in range(N_HEADS):
            kv = hd // group
            hs = slice(hd * DH, (hd + 1) * DH)
            ks = slice(kv * DH, (kv + 1) * DH)
            sink = sink_ref[hd]
            q = q_ref[qrows, hs].astype(BF16)
            s = jnp.where(near, _dot_nt(q, k_ref[keys, ks]), NEG_INF)
            se = _dot_nt(q, ke_ref[:, ks])
            m = jnp.maximum(jnp.max(s, axis=-1, keepdims=True), jnp.max(se, axis=-1, keepdims=True))
            m = jnp.maximum(m, sink)
            p, pe = jnp.exp(s - m), jnp.exp(se - m)
            den = jnp.sum(p, axis=-1, keepdims=True) + jnp.sum(pe, axis=-1, keepdims=True) + jnp.exp(sink - m)
            o_ref[qrows, hs] = (_dot(p, v_ref[keys, ks]) + _dot(pe, ve_ref[:, ks])) / den
        return carry

    lax.fori_loop(0, LAT_LEN // SWA_QB, block_body, 0)


def _swa_latent(q, k, v, k_ctx, v_ctx, sink, n_batch):
    kvw = k.shape[1]
    return pl.pallas_call(
        functools.partial(_swa_kernel, group=N_HEADS * DH // kvw),
        out_shape=jax.ShapeDtypeStruct(q.shape, F32),
        grid=(n_batch,),
        in_specs=[pl.BlockSpec((LAT_LEN, 512), lambda b: (b, 0)),
                  pl.BlockSpec((LAT_LEN, kvw), lambda b: (b, 0)), pl.BlockSpec((LAT_LEN, kvw), lambda b: (b, 0)),
                  pl.BlockSpec((CTX_LEN, kvw), lambda b: (b, 0)), pl.BlockSpec((CTX_LEN, kvw), lambda b: (b, 0)),
                  pl.BlockSpec(memory_space=pltpu.SMEM)],
        out_specs=pl.BlockSpec((LAT_LEN, 512), lambda b: (b, 0)),
        compiler_params=_params(("arbitrary",)), name="swa_latent",
    )(q, k, v, k_ctx, v_ctx, sink)


def _rope_tables():
    t = jnp.arange(LAT_LEN)
    pos = jnp.stack([t // GRID_W, t % GRID_W], axis=-1).astype(F32)
    freqs = ROPE_THETA ** (-jnp.arange(ROPE_FREQS, dtype=F32) / ROPE_FREQS)
    ang = (pos[:, :, None] * freqs).reshape(LAT_LEN, 2 * ROPE_FREQS)
    cos, sin = jnp.cos(ang), jnp.sin(ang)
    cos_t = jnp.tile(jnp.concatenate([cos, cos], axis=-1), (1, N_HEADS))
    sin_t = jnp.tile(jnp.concatenate([-sin, sin], axis=-1), (1, N_HEADS))
    return cos_t, sin_t


def kernel(x_prompt, x_sample, cache_l0_attn_k, cache_l0_attn_v, state_l0_mlstm_C, state_l0_mlstm_n, state_l0_mlstm_m, cache_l1_na_k, cache_l1_na_v, cache_l1_swa_k, cache_l1_swa_v, c, c_ctx, norm_final, ada_w_l0, ada_b_l0, norm_l0, ffn1_in_l0, ffn1_out_l0, ffn2_in_l0, ffn2_out_l0, mix_in_l0, mix_out_l0, qk_norm_l0, gate_bias_l0, head_norm_l0, ada_w_l1, ada_b_l1, norm_l1, ffn1_in_l1, ffn1_out_l1, ffn2_in_l1, ffn2_out_l1, mix_in_l1, mix_out_l1, rpb_l1, sink_l1):
    nb, nl = x_prompt.shape[0], x_sample.shape[0]
    bf = lambda a: a.astype(BF16)

    cond8 = jnp.zeros((8, D_MODEL), F32).at[0].set(c_ctx).at[1:1 + nl].set(c)
    mods0 = _adaln(cond8, ada_w_l0, ada_b_l0).reshape(8 * ADA_CHUNKS, 1, D_MODEL)
    mods1 = _adaln(cond8, ada_w_l1, ada_b_l1).reshape(8 * ADA_CHUNKS, 1, D_MODEL)

    w_even = jnp.concatenate([mix_in_l0[:, :2304], mix_in_l0[:, 2336:], mix_in_l0[:, 2304:2336],
                              jnp.zeros((D_MODEL, _EV_END - _EV_G - N_GATES), F32)], axis=1)
    w_even = bf(w_even)
    wgt = bf(mix_in_l0[:, 2304:2336].T)
    gb = jnp.zeros((1, 128), F32).at[0, :N_GATES].set(gate_bias_l0)
    gbt = gate_bias_l0.reshape(N_GATES, 1)
    qg = jnp.tile(qk_norm_l0[0], N_HEADS).reshape(1, 512)
    kg = jnp.tile(qk_norm_l0[1], 2).reshape(1, 128)
    head_g = head_norm_l0.reshape(1, 512)
    grp = np.arange(512) // DH
    bd = jnp.asarray((grp[:, None] == grp[None, :]).astype(np.float32) / DH, dtype=BF16)
    rope_tabs = _rope_tables()
    w_odd = bf(mix_in_l1)
    wo0, wo1 = bf(mix_out_l0), bf(mix_out_l1)
    rpb_flat = rpb_l1.reshape(-1)

    xp = x_prompt.reshape(nb * CTX_LEN, D_MODEL)
    xs = x_sample.reshape(nl * LAT_LEN, D_MODEL)

    f1i, f1o, f2i, f2o = bf(ffn1_in_l0), bf(ffn1_out_l0), bf(ffn2_in_l0), bf(ffn2_out_l0)
    outs = {}
    for latent, x in ((False, xp), (True, xs)):
        n_batch, seq_len = (nl, LAT_LEN) if latent else (nb, CTX_LEN)
        x = _ffn(x, mods0, 0, latent, norm_l0[0], f1i, f1o)
        qa, ka, va, qb, kb, vb, ob, gates, gates_t = _proj_even(
            x, mods0, latent, norm_l0[1], w_even, wgt, gb, gbt, qg, kg, bd, rope_tabs)
        if latent:
            extra = (cache_l0_attn_k.reshape(nl * CTX_LEN, 128), cache_l0_attn_v.reshape(nl * CTX_LEN, 128))
            m0 = jnp.broadcast_to(state_l0_mlstm_m.reshape(nl, 2 * N_HEADS, 1), (nl, 2 * N_HEADS, 128))
            state = (state_l0_mlstm_C, state_l0_mlstm_n, m0)
        else:
            extra, state = None, None
        ya = _attention(qa, ka, va, n_batch=n_batch, q_len=seq_len, kv_len=seq_len, q_tile=256, group=4, extra=extra)
        yb, c_fin, n_fin, m_fin = _mlstm(qb, kb, vb, ob, gates, gates_t, head_g, bd,
                                         n_batch=n_batch, seq_len=seq_len, state=state)
        x = _ffn(x, mods0, 6, latent, norm_l0[2], f2i, f2o, mix=(ya, yb, wo0[:512], wo0[512:]))
        outs[latent] = (x, ka, va, c_fin, n_fin, m_fin)
    xp, k0, v0, c0, n0, m0_pad = outs[False]
    xs = outs[True][0]

    f1i, f1o, f2i, f2o = bf(ffn1_in_l1), bf(ffn1_out_l1), bf(ffn2_in_l1), bf(ffn2_out_l1)
    for latent, x in ((False, xp), (True, xs)):
        n_batch, seq_len = (nl, LAT_LEN) if latent else (nb, CTX_LEN)
        x = _ffn(x, mods1, 0, latent, norm_l1[0], f1i, f1o)
        qc, kc, vc, qd, kd, vd = _proj_odd(x, mods1, latent, norm_l1[1], w_odd, rope_tabs)
        if latent:
            yc = _na_latent(qc, kc, vc, cache_l1_na_k.reshape(nl * CTX_LEN, 512),
                            cache_l1_na_v.reshape(nl * CTX_LEN, 512), rpb_flat, nl)
            yd = _swa_latent(qd, kd, vd, cache_l1_swa_k.reshape(nl * CTX_LEN, 128),
                             cache_l1_swa_v.reshape(nl * CTX_LEN, 128), sink_l1, nl)
        else:
            yc = _attention(qc, kc, vc, n_batch=n_batch, q_len=seq_len, kv_len=seq_len, q_tile=256, group=1)
            yd = _attention(qd, kd, vd, n_batch=n_batch, q_len=seq_len, kv_len=seq_len, q_tile=256, group=4, sink=sink_l1)
        x = _ffn(x, mods1, 6, latent, norm_l1[2], f2i, f2o, mix=(yc, yd, wo1[:512], wo1[512:]), final_g=norm_final)
        outs[latent] = (x, kc, vc, kd, vd)
    y_prompt, kc1, vc1, kd1, vd1 = outs[False]
    y_sample = outs[True][0]

    return (y_prompt.reshape(nb, CTX_LEN, D_MODEL), y_sample.reshape(nl, LAT_LEN, D_MODEL),
            k0.reshape(nb, CTX_LEN, 2, DH), v0.reshape(nb, CTX_LEN, 2, DH),
            c0, n0, m0_pad[:, :, 0].reshape(nb, 2, N_HEADS),
            kc1.reshape(nb, CTX_LEN, N_HEADS, DH), vc1.reshape(nb, CTX_LEN, N_HEADS, DH),
            kd1.reshape(nb, CTX_LEN, 2, DH), vd1.reshape(nb, CTX_LEN, 2, DH))
```

```python
import functools

import jax
import jax.numpy as jnp
import numpy as np
from jax import lax
from jax.experimental import pallas as pl
from jax.experimental.pallas import tpu as pltpu

F32 = jnp.float32
BF16 = jnp.bfloat16

D_MODEL = 1024
DH = 64
D_FF = 2816
ADA_CHUNKS = 9
GRID_W = 64
LAT_LEN = 1024
CTX_LEN = 256
N_HEADS = 8
NA_KH = 8
NA_KW = 16
SWA_WIN = 128
ROPE_THETA = 10000.0
ROPE_FREQS = DH // 4
ATTN_SCALE = DH ** -0.5
NEG_INF = -1e30
EPS = 1e-6

TOKEN_TILE = 512
MLSTM_T = 256
FFN_CHUNKS = ((0, 1024), (1024, 2048), (2048, D_FF))
VMEM_LIMIT = 56 * 1024 * 1024


def _params(sem, vmem=VMEM_LIMIT):
    return pltpu.CompilerParams(dimension_semantics=sem, vmem_limit_bytes=vmem)


def _dot(a, b):
    return jnp.dot(a.astype(BF16), b.astype(BF16), preferred_element_type=F32)


def _dot_nt(a, b):
    return lax.dot_general(a.astype(BF16), b.astype(BF16), (((1,), (1,)), ((), ())),
                           preferred_element_type=F32)


def _split3(x):
    hi = x.astype(BF16)
    r1 = x - hi.astype(F32)
    mid = r1.astype(BF16)
    lo = (r1 - mid.astype(F32)).astype(BF16)
    return hi, mid, lo


def _dot_exact_rhs(a_bf16, x):
    hi, mid, lo = _split3(x)
    f = lambda p: jnp.dot(a_bf16, p, preferred_element_type=F32)
    return f(hi) + f(mid) + f(lo)


def _dot_exact_lhs(x, b_bf16):
    hi, mid, lo = _split3(x)
    f = lambda p: jnp.dot(p, b_bf16, preferred_element_type=F32)
    return f(hi) + f(mid) + f(lo)


def _silu(x):
    return x * jax.nn.sigmoid(x)


def _log_sigmoid(x):
    return jnp.minimum(x, 0.0) - jnp.log1p(jnp.exp(-jnp.abs(x)))


def _rms(x, g):
    return x * lax.rsqrt(jnp.mean(x * x, axis=-1, keepdims=True) + EPS) * g


def _head_mean_sq(x, bd_ref):
    n = x.shape[-1]
    return _dot_exact_lhs(x * x, bd_ref[:n, :n])


def _rope(x, cos, sin_signed):
    n = x.shape[-1]
    lane = lax.broadcasted_iota(jnp.int32, x.shape, 1)
    first_half = (lane % DH) < (DH // 2)
    partner = jnp.where(first_half, pltpu.roll(x, n - DH // 2, 1), pltpu.roll(x, DH // 2, 1))
    return x * cos[:, :n] + partner * sin_signed[:, :n]


def _adaln_kernel(c_ref, w_ref, b_ref, o_ref):
    s = _silu(c_ref[...])
    o_ref[...] = _dot(s, w_ref[...]) + b_ref[...]


def _adaln(cond8, w, b):
    n = w.shape[1]
    tn = 1536
    return pl.pallas_call(
        _adaln_kernel,
        out_shape=jax.ShapeDtypeStruct((8, n), F32),
        grid=(n // tn,),
        in_specs=[pl.BlockSpec((8, D_MODEL), lambda j: (0, 0)),
                  pl.BlockSpec((D_MODEL, tn), lambda j: (0, j)),
                  pl.BlockSpec((1, tn), lambda j: (0, j))],
        out_specs=pl.BlockSpec((8, tn), lambda j: (0, j)),
        compiler_params=_params(("arbitrary",)),
        name="adaln",
    )(cond8, w, b.reshape(1, n))


def _const_spec(shape):
    nd = len(shape)
    return pl.BlockSpec(shape, lambda i, _n=nd: (0,) * _n, pipeline_mode=pl.Buffered(1))


def _mod_spec(chunk, rowfn):
    return pl.BlockSpec((1, 1, D_MODEL), lambda i: (rowfn(i) * ADA_CHUNKS + chunk, 0, 0))


def _rowfn(latent):
    if latent:
        return lambda i: 1 + (i * TOKEN_TILE) // LAT_LEN
    return lambda i: 0


def _tok_spec(width):
    return pl.BlockSpec((TOKEN_TILE, width), lambda i: (i, 0))


def _modnorm(x, ng_ref, sh_ref, sc_ref):
    return _rms(x, ng_ref[...]) * (1.0 + sc_ref[0]) + sh_ref[0]


def _ffn_kernel(*refs, has_mix, final_norm):
    it = iter(refs)
    x_ref = next(it)
    if has_mix:
        ya_ref, yb_ref, woa_ref, wob_ref, g2_ref = (next(it) for _ in range(5))
    ng_ref, sh_ref, sc_ref, g_ref, win_ref, wout_ref = (next(it) for _ in range(6))
    if final_norm:
        nf_ref = next(it)
    o_ref = next(it)

    x = x_ref[...]
    if has_mix:
        mo = _dot(ya_ref[...], woa_ref[...]) + _dot(yb_ref[...], wob_ref[...])
        x = x + g2_ref[0] * mo
    h = _modnorm(x, ng_ref, sh_ref, sc_ref).astype(BF16)
    acc = None
    for lo, hi in FFN_CHUNKS:
        g = _dot(h, win_ref[:, lo:hi])
        u = _dot(h, win_ref[:, D_FF + lo:D_FF + hi])
        part = _dot(_silu(g) * u, wout_ref[lo:hi, :])
        acc = part if acc is None else acc + part
    y = x + (0.5 * g_ref[0]) * acc
    if final_norm:
        y = _rms(y, nf_ref[...])
    o_ref[...] = y


def _ffn(x, mods, mod_base, latent, ng, w_in, w_out, mix=None, final_g=None):
    t = x.shape[0]
    rowfn = _rowfn(latent)
    args, specs = [x], [_tok_spec(D_MODEL)]
    if mix is not None:
        ya, yb, woa, wob = mix
        args += [ya, yb, woa, wob, mods]
        specs += [_tok_spec(ya.shape[1]), _tok_spec(yb.shape[1]), _const_spec(woa.shape), _const_spec(wob.shape),
                  _mod_spec(mod_base - 1, rowfn)]
    args += [ng.reshape(1, D_MODEL), mods, mods, mods, w_in, w_out]
    specs += [_const_spec((1, D_MODEL)), _mod_spec(mod_base, rowfn), _mod_spec(mod_base + 1, rowfn),
              _mod_spec(mod_base + 2, rowfn), _const_spec(w_in.shape), _const_spec(w_out.shape)]
    if final_g is not None:
        args.append(final_g.reshape(1, D_MODEL))
        specs.append(_const_spec((1, D_MODEL)))
    return pl.pallas_call(
        functools.partial(_ffn_kernel, has_mix=mix is not None, final_norm=final_g is not None),
        out_shape=jax.ShapeDtypeStruct((t, D_MODEL), F32),
        grid=(t // TOKEN_TILE,),
        in_specs=specs,
        out_specs=_tok_spec(D_MODEL),
        compiler_params=_params(("arbitrary",)),
        name="ffn",
    )(*args)


_EV_QA, _EV_KA, _EV_VA, _EV_QB, _EV_KB, _EV_VB, _EV_OB, _EV_GI, _EV_GF, _EV_END = (
    0, 512, 640, 768, 1280, 1792, 2304, 2816, 2944, 3072)
N_DIRHEAD = 2 * N_HEADS


def _proj_even_kernel(*refs, rope):
    it = iter(refs)
    x_ref, ng_ref, sh_ref, sc_ref, w_ref, wt_ref, gbi_ref, gbf_ref, gbt_ref, qg_ref, kg_ref, bd_ref = (
        next(it) for _ in range(12))
    if rope:
        cos_ref, sin_ref = next(it), next(it)
    qa_ref, ka_ref, va_ref, qb_ref, kb_ref, vb_ref, ob_ref, gi_ref, gf_ref, gt_ref, kbt_ref = (next(it) for _ in range(11))

    h = _modnorm(x_ref[...], ng_ref, sh_ref, sc_ref).astype(BF16)
    proj = lambda lo, hi: _dot(h, w_ref[:, lo:hi])

    qa = proj(_EV_QA, _EV_KA)
    qa = qa * lax.rsqrt(_head_mean_sq(qa, bd_ref) + EPS) * qg_ref[...]
    ka = proj(_EV_KA, _EV_VA)
    ka = ka * lax.rsqrt(_head_mean_sq(ka, bd_ref) + EPS) * kg_ref[...]
    if rope:
        qa = _rope(qa, cos_ref[...], sin_ref[...])
        ka = _rope(ka, cos_ref[...], sin_ref[...])
    qa_ref[...] = qa * ATTN_SCALE
    ka_ref[...] = ka
    va_ref[...] = proj(_EV_VA, _EV_QB)
    qb_ref[...] = proj(_EV_QB, _EV_KB)
    kb_ref[...] = proj(_EV_KB, _EV_VB) * ATTN_SCALE
    vb_ref[...] = proj(_EV_VB, _EV_OB)
    ob_ref[...] = proj(_EV_OB, _EV_GI)
    gi_ref[...] = proj(_EV_GI, _EV_GF) + gbi_ref[...]
    gf_ref[...] = proj(_EV_GF, _EV_END) + gbf_ref[...]
    for j in range(TOKEN_TILE // MLSTM_T):
        ht = h[j * MLSTM_T:(j + 1) * MLSTM_T]
        gt_ref[j] = _dot_nt(wt_ref[:2 * N_DIRHEAD, :], ht) + gbt_ref[...]
        kbt_ref[j] = _dot_nt(wt_ref[2 * N_DIRHEAD:, :], ht) * ATTN_SCALE


def _proj_even(x, mods, latent, ng, w, wt, gbi, gbf, gbt, qg, kg, bd, rope_tabs):
    t = x.shape[0]
    rowfn = _rowfn(latent)
    args = [x, ng.reshape(1, D_MODEL), mods, mods, w, wt, gbi, gbf, gbt, qg, kg, bd]
    specs = [_tok_spec(D_MODEL), _const_spec((1, D_MODEL)), _mod_spec(3, rowfn), _mod_spec(4, rowfn)]
    specs += [_const_spec(a.shape) for a in args[4:]]
    if latent:
        nblk = LAT_LEN // TOKEN_TILE
        args += list(rope_tabs)
        specs += [pl.BlockSpec((TOKEN_TILE, 512), lambda i: (i % nblk, 0))] * 2
    widths = (512, 128, 128, 512, 512, 512, 512, 128, 128)
    out_shape = [jax.ShapeDtypeStruct((t, wd), F32) for wd in widths]
    out_specs = [_tok_spec(wd) for wd in widths]
    nj = TOKEN_TILE // MLSTM_T
    for rows in (2 * N_DIRHEAD, 512):
        out_shape.append(jax.ShapeDtypeStruct((t // MLSTM_T, rows, MLSTM_T), F32))
        out_specs.append(pl.BlockSpec((nj, rows, MLSTM_T), lambda i: (i, 0, 0)))
    return pl.pallas_call(
        functools.partial(_proj_even_kernel, rope=latent),
        out_shape=out_shape, grid=(t // TOKEN_TILE,), in_specs=specs, out_specs=out_specs,
        compiler_params=_params(("arbitrary",)), name="proj_even",
    )(*args)


def _proj_odd_kernel(*refs, rope):
    it = iter(refs)
    x_ref, ng_ref, sh_ref, sc_ref, w_ref = (next(it) for _ in range(5))
    if rope:
        cos_ref, sin_ref = next(it), next(it)
    qc_ref, kc_ref, vc_ref, qd_ref, kd_ref, vd_ref = (next(it) for _ in range(6))

    h = _modnorm(x_ref[...], ng_ref, sh_ref, sc_ref).astype(BF16)
    proj = lambda lo, hi: _dot(h, w_ref[:, lo:hi])
    qc_ref[...] = proj(0, 512) * ATTN_SCALE
    kc_ref[...] = proj(512, 1024)
    vc_ref[...] = proj(1024, 1536)
    qd = proj(1536, 2048)
    kd = proj(2048, 2176)
    if rope:
        qd = _rope(qd, cos_ref[...], sin_ref[...])
        kd = _rope(kd, cos_ref[...], sin_ref[...])
    qd_ref[...] = qd * ATTN_SCALE
    kd_ref[...] = kd
    vd_ref[...] = proj(2176, 2304)


def _proj_odd(x, mods, latent, ng, w, rope_tabs):
    t = x.shape[0]
    rowfn = _rowfn(latent)
    args = [x, ng.reshape(1, D_MODEL), mods, mods, w]
    specs = [_tok_spec(D_MODEL), _const_spec((1, D_MODEL)), _mod_spec(3, rowfn), _mod_spec(4, rowfn), _const_spec(w.shape)]
    if latent:
        nblk = LAT_LEN // TOKEN_TILE
        args += list(rope_tabs)
        specs += [pl.BlockSpec((TOKEN_TILE, 512), lambda i: (i % nblk, 0))] * 2
    widths = (512, 512, 512, 512, 128, 128)
    return pl.pallas_call(
        functools.partial(_proj_odd_kernel, rope=latent),
        out_shape=[jax.ShapeDtypeStruct((t, wd), F32) for wd in widths],
        grid=(t // TOKEN_TILE,), in_specs=specs, out_specs=[_tok_spec(wd) for wd in widths],
        compiler_params=_params(("arbitrary",)), name="proj_odd",
    )(*args)


def _attn_kernel(*refs, group, has_extra, has_sink):
    it = iter(refs)
    q_ref, k_ref, v_ref = next(it), next(it), next(it)
    if has_extra:
        ke_ref, ve_ref = next(it), next(it)
    if has_sink:
        sink_ref = next(it)
    o_ref = next(it)
    for hd in range(N_HEADS):
        kv = hd // group
        hs = slice(hd * DH, (hd + 1) * DH)
        ks = slice(kv * DH, (kv + 1) * DH)
        q = q_ref[:, hs].astype(BF16)
        s = _dot_nt(q, k_ref[:, ks])
        m = jnp.max(s, axis=-1, keepdims=True)
        if has_extra:
            se = _dot_nt(q, ke_ref[:, ks])
            m = jnp.maximum(m, jnp.max(se, axis=-1, keepdims=True))
        if has_sink:
            m = jnp.maximum(m, sink_ref[hd])
        p = jnp.exp(s - m)
        den = jnp.sum(p, axis=-1, keepdims=True)
        o = _dot(p, v_ref[:, ks])
        if has_extra:
            pe = jnp.exp(se - m)
            den = den + jnp.sum(pe, axis=-1, keepdims=True)
            o = o + _dot(pe, ve_ref[:, ks])
        if has_sink:
            den = den + jnp.exp(sink_ref[hd] - m)
        o_ref[:, hs] = o / den


def _attention(q, k, v, *, n_batch, q_len, kv_len, q_tile, group, extra=None, sink=None):
    nq = q_len // q_tile
    kvw = k.shape[1]
    args = [q, k, v]
    specs = [pl.BlockSpec((q_tile, 512), lambda b, j: (b * nq + j, 0)),
             pl.BlockSpec((kv_len, kvw), lambda b, j: (b, 0)),
             pl.BlockSpec((kv_len, kvw), lambda b, j: (b, 0))]
    if extra is not None:
        e_len = extra[0].shape[0] // n_batch
        args += list(extra)
        specs += [pl.BlockSpec((e_len, kvw), lambda b, j: (b, 0))] * 2
    if sink is not None:
        args.append(sink)
        specs.append(pl.BlockSpec(memory_space=pltpu.SMEM))
    return pl.pallas_call(
        functools.partial(_attn_kernel, group=group, has_extra=extra is not None, has_sink=sink is not None),
        out_shape=jax.ShapeDtypeStruct(q.shape, F32),
        grid=(n_batch, nq), in_specs=specs,
        out_specs=pl.BlockSpec((q_tile, 512), lambda b, j: (b * nq + j, 0)),
        compiler_params=_params(("arbitrary", "arbitrary")), name="attn",
    )(*args)


LOG2E = 1.4426950408889634


def _cummax_time(x, reverse):
    n = x.shape[0]
    row = lax.broadcasted_iota(jnp.int32, x.shape, 0)
    k = 1
    while k < n:
        if reverse:
            shifted = jnp.where(row < n - k, pltpu.roll(x, n - k, 0), -jnp.inf)
        else:
            shifted = jnp.where(row >= k, pltpu.roll(x, k, 0), -jnp.inf)
        x = jnp.maximum(x, shifted)
        k *= 2
    return x


def _mlstm_kernel(*refs, n_chunks, carry):
    it = iter(refs)
    q_ref, k_ref, kt_ref, v_ref, ob_ref, gi_ref, gf_ref, gt_ref, hg_ref, bd_ref = (next(it) for _ in range(10))
    if carry:
        c0_ref, n0_ref, m0r_ref, m0c_ref = (next(it) for _ in range(4))
    y_ref = next(it)
    if not carry:
        c_ref, n_ref, m_ref = next(it), next(it), next(it)
    hf_s, hb_s = next(it), next(it)
    if carry:
        st_s, mrow_s, mcol_s = next(it), next(it), next(it)
    T = MLSTM_T
    step = pl.program_id(1)

    if carry:
        @pl.when(step == 0)
        def _():
            z = jnp.zeros((DH, DH), F32)
            for d in range(2):
                for j in range(N_HEADS // 2):
                    ca, cb = c0_ref[0, d, 2 * j], c0_ref[0, d, 2 * j + 1]
                    na, nb = n0_ref[0, d, 2 * j][:, :DH], n0_ref[0, d, 2 * j + 1][:, :DH]
                    st_s[d, j, 0:DH, :] = jnp.concatenate([ca, z, na, z], axis=1)
                    st_s[d, j, DH:2 * DH, :] = jnp.concatenate([z, cb, z, nb], axis=1)
            mrow_s[...] = m0r_ref[0]
            mcol_s[...] = m0c_ref[0]

    row = lax.broadcasted_iota(jnp.int32, (T, T), 0)
    col = lax.broadcasted_iota(jnp.int32, (T, T), 1)
    lower = row >= col
    upper = row <= col
    lower_m = jnp.where(lower, 1.0, 0.0).astype(BF16)
    upper_m = jnp.where(upper, 1.0, 0.0).astype(BF16)
    lane = lax.broadcasted_iota(jnp.int32, (T, 128), 1)
    first = lane < DH
    lane_c = lax.broadcasted_iota(jnp.int32, (DH, 256), 1) % 128
    zeros_kt = jnp.zeros((DH, T), BF16)
    ones_v = jnp.ones((T, 128), BF16)

    for d in range(2):
        chunk = step if d == 0 else n_chunks - 1 - step
        rows = pl.ds(pl.multiple_of(chunk * T, T), T)
        hd0 = N_HEADS * d
        mask = lower if d == 0 else upper

        gi, gf = gi_ref[rows, :], gf_ref[rows, :]
        b_cols = _dot_exact_rhs(lower_m if d == 0 else upper_m, _log_sigmoid(gf))
        a_cols = gi - b_cols
        dmax = b_cols + _cummax_time(a_cols, reverse=d == 1)
        if carry:
            inter = b_cols + mrow_s[0:1, :]
            mt = jnp.maximum(inter, dmax)
            w_inter = jnp.exp(inter - mt)
        else:
            mt = jnp.maximum(b_cols, dmax)
        c2 = (b_cols - mt) * LOG2E
        einv = jnp.exp(-mt)

        gt = gt_ref[chunk]
        gi_t = gt[hd0:hd0 + N_HEADS, :]
        gf_t = gt[N_DIRHEAD + hd0:N_DIRHEAD + hd0 + N_HEADS, :]
        b_rows = _dot_exact_lhs(_log_sigmoid(gf_t), upper_m if d == 0 else lower_m)
        a_rows = gi_t - b_rows
        a2 = a_rows * LOG2E
        b_last = b_rows[:, T - 1:T] if d == 0 else b_rows[:, 0:1]
        dec_rows = b_last + a_rows
        m_new = jnp.max(dec_rows, axis=-1, keepdims=True)
        if carry:
            m_prev = mcol_s[hd0:hd0 + N_HEADS, 0:1]
            m_new = jnp.maximum(m_new, b_last + m_prev)
            wc = jnp.exp(b_last + m_prev - m_new)
        else:
            m_new = jnp.maximum(m_new, b_last)
        ws_rows = jnp.exp(dec_rows - m_new)

        h_s = hf_s if d == 0 else hb_s
        for j in range(N_HEADS // 2):
            ps = slice(128 * j, 128 * (j + 1))
            q_pair = q_ref[rows, ps].astype(BF16)
            v_aug = jnp.concatenate([v_ref[rows, ps].astype(BF16), ones_v], axis=1)
            kt = (kt_ref[chunk, 128 * j:128 * j + DH, :], kt_ref[chunk, 128 * j + DH:128 * (j + 1), :])
            res, cu = [], []
            for half in range(2):
                hd = 2 * j + half
                kt0 = jnp.concatenate([kt[0].astype(BF16), zeros_kt] if half == 0 else [zeros_kt, kt[1].astype(BF16)], axis=0)
                qk = jnp.dot(q_pair, kt0, preferred_element_type=F32)
                e = c2[:, hd0 + hd:hd0 + hd + 1] + a2[hd:hd + 1, :]
                w = jnp.exp2(jnp.where(mask, e, -jnp.inf)) * qk
                res.append(_dot(w, v_aug))
                cu.append(_dot(kt[half] * ws_rows[hd:hd + 1, :], v_aug))
            ra, rb = hd0 + 2 * j, hd0 + 2 * j + 1
            num = jnp.where(first, res[0][:, :128], res[1][:, :128])
            den = jnp.where(first, res[0][:, 128:], res[1][:, 128:])
            if carry:
                old = st_s[d, j]
                qs = _dot(q_pair, old)
                wi = jnp.where(first, w_inter[:, ra:ra + 1], w_inter[:, rb:rb + 1])
                num = num + wi * qs[:, :128]
                den = den + wi * qs[:, 128:]
            floor = jnp.where(first, einv[:, ra:ra + 1], einv[:, rb:rb + 1])
            h_s[rows, ps] = num / jnp.maximum(jnp.abs(den), floor)
            if carry:
                st_s[d, j, 0:DH, :] = jnp.where(lane_c < DH, cu[0], 0.0) + wc[2 * j:2 * j + 1, :] * old[0:DH]
                st_s[d, j, DH:2 * DH, :] = jnp.where(lane_c >= DH, cu[1], 0.0) + wc[2 * j + 1:2 * j + 2, :] * old[DH:2 * DH]
            else:
                c_ref[0, d, 2 * j] = cu[0][:, 0:DH]
                c_ref[0, d, 2 * j + 1] = cu[1][:, DH:2 * DH]

        if carry:
            b_last_l = b_cols[T - 1:T, :] if d == 0 else b_cols[0:1, :]
            m_prev_l = mrow_s[0:1, :]
            m_new_l = jnp.maximum(jnp.max(b_last_l + a_cols, axis=0, keepdims=True), b_last_l + m_prev_l)
            lane_r = lax.broadcasted_iota(jnp.int32, (1, 128), 1)
            mine = (lane_r >= hd0) & (lane_r < hd0 + N_HEADS)
            mrow_s[0:1, :] = jnp.where(mine, m_new_l, m_prev_l)
            mcol_s[hd0:hd0 + N_HEADS, :] = jnp.broadcast_to(m_new, (N_HEADS, 128))
        else:
            n_all = _dot(ws_rows, k_ref[rows, :])
            for hd in range(N_HEADS):
                n_ref[0, d, hd:hd + 1, :] = n_all[hd:hd + 1, hd * DH:(hd + 1) * DH]
            m_ref[0, hd0:hd0 + N_HEADS, :] = jnp.broadcast_to(m_new, (N_HEADS, 128))

    @pl.when(step == n_chunks - 1)
    def _():
        hsum = hf_s[...] + hb_s[...]
        y = hsum * lax.rsqrt(_head_mean_sq(hsum, bd_ref) + EPS) * hg_ref[...]
        y_ref[...] = jax.nn.sigmoid(ob_ref[...]) * y


def _mlstm(qb, kb, kbt, vb, ob, gi, gf, gt, head_g, bd, *, n_batch, seq_len, state=None):
    n_chunks = seq_len // MLSTM_T
    carry = state is not None
    assert carry or n_chunks == 1
    seq = lambda wd: pl.BlockSpec((seq_len, wd), lambda b, c: (b, 0))
    per_chunk = lambda r: pl.BlockSpec((n_chunks, r, MLSTM_T), lambda b, c: (b, 0, 0))
    args = [qb, kb, kbt, vb, ob, gi, gf, gt, head_g, bd]
    specs = [seq(512), seq(512), per_chunk(512), seq(512), seq(512), seq(128), seq(128), per_chunk(2 * N_DIRHEAD),
             pl.BlockSpec((1, 512), lambda b, c: (0, 0)), pl.BlockSpec(bd.shape, lambda b, c: (0, 0))]
    out_shape = [jax.ShapeDtypeStruct((n_batch * seq_len, 512), F32)]
    out_specs = [seq(512)]
    scratch = [pltpu.VMEM((seq_len, 512), F32), pltpu.VMEM((seq_len, 512), F32)]
    if carry:
        args += list(state)
        specs += [pl.BlockSpec((1, 2, N_HEADS, DH, DH), lambda b, c: (b, 0, 0, 0, 0)),
                  pl.BlockSpec((1, 2, N_HEADS, DH, 128), lambda b, c: (b, 0, 0, 0, 0)),
                  pl.BlockSpec((1, 8, 128), lambda b, c: (b, 0, 0)),
                  pl.BlockSpec((1, N_DIRHEAD, 128), lambda b, c: (b, 0, 0))]
        scratch += [pltpu.VMEM((2, N_HEADS // 2, 128, 256), F32), pltpu.VMEM((8, 128), F32),
                    pltpu.VMEM((N_DIRHEAD, 128), F32)]
    else:
        out_shape += [jax.ShapeDtypeStruct((n_batch, 2, N_HEADS, DH, DH), F32),
                      jax.ShapeDtypeStruct((n_batch, 2, N_HEADS, DH), F32),
                      jax.ShapeDtypeStruct((n_batch, N_DIRHEAD, 128), F32)]
        out_specs += [pl.BlockSpec((1, 2, N_HEADS, DH, DH), lambda b, c: (b, 0, 0, 0, 0)),
                      pl.BlockSpec((1, 2, N_HEADS, DH), lambda b, c: (b, 0, 0, 0)),
                      pl.BlockSpec((1, N_DIRHEAD, 128), lambda b, c: (b, 0, 0))]
    return pl.pallas_call(
        functools.partial(_mlstm_kernel, n_chunks=n_chunks, carry=carry),
        out_shape=out_shape, grid=(n_batch, n_chunks), in_specs=specs, out_specs=out_specs,
        scratch_shapes=scratch,
        compiler_params=_params(("arbitrary", "arbitrary")), name="mlstm",
    )(*args)


N_ROWS = LAT_LEN // GRID_W
N_DY = 2 * NA_KH - 1
N_DX = 2 * NA_KW - 1


def _na_kernel(q_ref, k_ref, v_ref, ke_ref, ve_ref, rpb_ref, o_ref, tile_s, slab_s):
    @pl.when(pl.program_id(0) == 0)
    def _():
        qc = lax.broadcasted_iota(jnp.int32, (GRID_W, GRID_W), 0)
        kc = lax.broadcasted_iota(jnp.int32, (GRID_W, GRID_W), 1)
        start = jnp.clip(qc - NA_KW // 2, 0, GRID_W - NA_KW)
        in_win = (kc >= start) & (kc < start + NA_KW)
        dx = kc - qc + (NA_KW - 1)

        def build_tile(idx, carry):
            tile = jnp.full((GRID_W, GRID_W), NEG_INF, F32)
            for j in range(N_DX):
                tile = jnp.where(dx == j, rpb_ref[idx * N_DX + j], tile)
            tile_s[idx] = jnp.where(in_win, tile, NEG_INF)
            return carry

        lax.fori_loop(0, N_HEADS * N_DY, build_tile, 0)

        def build_slab(idx, carry):
            first = (idx // NA_KH) * N_DY + idx % NA_KH
            for j in range(NA_KH):
                slab_s[idx, :, j * GRID_W:(j + 1) * GRID_W] = tile_s[first + j]
            return carry

        lax.fori_loop(0, N_HEADS * NA_KH, build_slab, 0)

    def row_body(r, carry):
        r0 = jnp.clip(r - NA_KH // 2, 0, N_ROWS - NA_KH)
        dy0 = r0 - r + NA_KH - 1
        qrows = pl.ds(pl.multiple_of(r * GRID_W, GRID_W), GRID_W)
        keys = pl.ds(pl.multiple_of(r0 * GRID_W, GRID_W), NA_KH * GRID_W)
        for hd in range(N_HEADS):
            hs = slice(hd * DH, (hd + 1) * DH)
            q = q_ref[qrows, hs].astype(BF16)
            s = _dot_nt(q, k_ref[keys, hs]) + slab_s[hd * NA_KH + dy0]
            se = _dot_nt(q, ke_ref[:, hs])
            m = jnp.maximum(jnp.max(s, axis=-1, keepdims=True), jnp.max(se, axis=-1, keepdims=True))
            p, pe = jnp.exp(s - m), jnp.exp(se - m)
            den = jnp.sum(p, axis=-1, keepdims=True) + jnp.sum(pe, axis=-1, keepdims=True)
            o_ref[qrows, hs] = (_dot(p, v_ref[keys, hs]) + _dot(pe, ve_ref[:, hs])) / den
        return carry

    lax.fori_loop(0, N_ROWS, row_body, 0)


def _na_latent(q, k, v, k_ctx, v_ctx, rpb_flat, n_batch):
    seq = pl.BlockSpec((LAT_LEN, 512), lambda b: (b, 0))
    ctx = pl.BlockSpec((CTX_LEN, 512), lambda b: (b, 0))
    return pl.pallas_call(
        _na_kernel,
        out_shape=jax.ShapeDtypeStruct(q.shape, F32),
        grid=(n_batch,),
        in_specs=[seq, seq, seq, ctx, ctx, pl.BlockSpec(memory_space=pltpu.SMEM)],
        out_specs=seq,
        scratch_shapes=[pltpu.VMEM((N_HEADS * N_DY, GRID_W, GRID_W), F32),
                        pltpu.VMEM((N_HEADS * NA_KH, GRID_W, NA_KH * GRID_W), F32)],
        compiler_params=_params(("arbitrary",)), name="na_latent",
    )(q, k, v, k_ctx, v_ctx, rpb_flat)


SWA_QB = 128


def _swa_kernel(q_ref, k_ref, v_ref, ke_ref, ve_ref, sink_ref, o_ref, *, group):
    span = SWA_QB + 2 * SWA_WIN

    def block_body(n, carry):
        q0 = n * SWA_QB
        lo = pl.multiple_of(jnp.clip(q0 - SWA_WIN, 0, LAT_LEN - span), SWA_QB)
        qrows = pl.ds(pl.multiple_of(q0, SWA_QB), SWA_QB)
        keys = pl.ds(lo, span)
        qpos = q0 + lax.broadcasted_iota(jnp.int32, (SWA_QB, span), 0)
        kpos = lo + lax.broadcasted_iota(jnp.int32, (SWA_QB, span), 1)
        near = jnp.abs(kpos - qpos) <= SWA_WIN
        for hd in range(N_HEADS):
            kv = hd // group
            hs = slice(hd * DH, (hd + 1) * DH)
            ks = slice(kv * DH, (kv + 1) * DH)
            sink = sink_ref[hd]
            q = q_ref[qrows, hs].astype(BF16)
            s = jnp.where(near, _dot_nt(q, k_ref[keys, ks]), NEG_INF)
            se = _dot_nt(q, ke_ref[:, ks])
            m = jnp.maximum(jnp.max(s, axis=-1, keepdims=True), jnp.max(se, axis=-1, keepdims=True))
            m = jnp.maximum(m, sink)
            p, pe = jnp.exp(s - m), jnp.exp(se - m)
            den = jnp.sum(p, axis=-1, keepdims=True) + jnp.sum(pe, axis=-1, keepdims=True) + jnp.exp(sink - m)
            o_ref[qrows, hs] = (_dot(p, v_ref[keys, ks]) + _dot(pe, ve_ref[:, ks])) / den
        return carry

    lax.fori_loop(0, LAT_LEN // SWA_QB, block_body, 0)


def _swa_latent(q, k, v, k_ctx, v_ctx, sink, n_batch):
    kvw = k.shape[1]
    return pl.pallas_call(
        functools.partial(_swa_kernel, group=N_HEADS * DH // kvw),
        out_shape=jax.ShapeDtypeStruct(q.shape, F32),
        grid=(n_batch,),
        in_specs=[pl.BlockSpec((LAT_LEN, 512), lambda b: (b, 0)),
                  pl.BlockSpec((LAT_LEN, kvw), lambda b: (b, 0)), pl.BlockSpec((LAT_LEN, kvw), lambda b: (b, 0)),
                  pl.BlockSpec((CTX_LEN, kvw), lambda b: (b, 0)), pl.BlockSpec((CTX_LEN, kvw), lambda b: (b, 0)),
                  pl.BlockSpec(memory_space=pltpu.SMEM)],
        out_specs=pl.BlockSpec((LAT_LEN, 512), lambda b: (b, 0)),
        compiler_params=_params(("arbitrary",)), name="swa_latent",
    )(q, k, v, k_ctx, v_ctx, sink)


def _rope_tables():
    t = jnp.arange(LAT_LEN)
    pos = jnp.stack([t // GRID_W, t % GRID_W], axis=-1).astype(F32)
    freqs = ROPE_THETA ** (-jnp.arange(ROPE_FREQS, dtype=F32) / ROPE_FREQS)
    ang = (pos[:, :, None] * freqs).reshape(LAT_LEN, 2 * ROPE_FREQS)
    cos, sin = jnp.cos(ang), jnp.sin(ang)
    cos_t = jnp.tile(jnp.concatenate([cos, cos], axis=-1), (1, N_HEADS))
    sin_t = jnp.tile(jnp.concatenate([-sin, sin], axis=-1), (1, N_HEADS))
    return cos_t, sin_t


def kernel(x_prompt, x_sample, cache_l0_attn_k, cache_l0_attn_v, state_l0_mlstm_C, state_l0_mlstm_n, state_l0_mlstm_m, cache_l1_na_k, cache_l1_na_v, cache_l1_swa_k, cache_l1_swa_v, c, c_ctx, norm_final, ada_w_l0, ada_b_l0, norm_l0, ffn1_in_l0, ffn1_out_l0, ffn2_in_l0, ffn2_out_l0, mix_in_l0, mix_out_l0, qk_norm_l0, gate_bias_l0, head_norm_l0, ada_w_l1, ada_b_l1, norm_l1, ffn1_in_l1, ffn1_out_l1, ffn2_in_l1, ffn2_out_l1, mix_in_l1, mix_out_l1, rpb_l1, sink_l1):
    nb, nl = x_prompt.shape[0], x_sample.shape[0]
    bf = lambda a: a.astype(BF16)

    cond8 = jnp.zeros((8, D_MODEL), F32).at[0].set(c_ctx).at[1:1 + nl].set(c)
    mods0 = _adaln(cond8, ada_w_l0, ada_b_l0).reshape(8 * ADA_CHUNKS, 1, D_MODEL)
    mods1 = _adaln(cond8, ada_w_l1, ada_b_l1).reshape(8 * ADA_CHUNKS, 1, D_MODEL)

    g0 = 2304
    gcols = lambda j: mix_in_l0[:, g0 + 8 * j:g0 + 8 * (j + 1)]
    gpad = jnp.zeros((D_MODEL, 128 - N_DIRHEAD), F32)
    w_gi, w_gf = jnp.concatenate([gcols(0), gcols(2)], axis=1), jnp.concatenate([gcols(1), gcols(3)], axis=1)
    w_even = bf(jnp.concatenate([mix_in_l0[:, :g0], mix_in_l0[:, g0 + 32:], w_gi, gpad, w_gf, gpad], axis=1))
    w_even_t = bf(jnp.concatenate([w_gi, w_gf, mix_in_l0[:, 1280:1792]], axis=1).T)
    gb4 = gate_bias_l0.reshape(4, N_HEADS)
    b_gi, b_gf = jnp.concatenate([gb4[0], gb4[2]]), jnp.concatenate([gb4[1], gb4[3]])
    gbi = jnp.zeros((1, 128), F32).at[0, :N_DIRHEAD].set(b_gi)
    gbf = jnp.zeros((1, 128), F32).at[0, :N_DIRHEAD].set(b_gf)
    gbt = jnp.concatenate([b_gi, b_gf]).reshape(2 * N_DIRHEAD, 1)
    qg = jnp.tile(qk_norm_l0[0], N_HEADS).reshape(1, 512)
    kg = jnp.tile(qk_norm_l0[1], 2).reshape(1, 128)
    head_g = head_norm_l0.reshape(1, 512)
    grp = np.arange(512) // DH
    bd = jnp.asarray((grp[:, None] == grp[None, :]).astype(np.float32) / DH, dtype=BF16)
    rope_tabs = _rope_tables()
    w_odd = bf(mix_in_l1)
    wo0, wo1 = bf(mix_out_l0), bf(mix_out_l1)
    rpb_flat = rpb_l1.reshape(-1)

    xp = x_prompt.reshape(nb * CTX_LEN, D_MODEL)
    xs = x_sample.reshape(nl * LAT_LEN, D_MODEL)

    f1i, f1o, f2i, f2o = bf(ffn1_in_l0), bf(ffn1_out_l0), bf(ffn2_in_l0), bf(ffn2_out_l0)
    outs = {}
    for latent, x in ((False, xp), (True, xs)):
        n_batch, seq_len = (nl, LAT_LEN) if latent else (nb, CTX_LEN)
        x = _ffn(x, mods0, 0, latent, norm_l0[0], f1i, f1o)
        qa, ka, va, qb, kb, vb, ob, gi, gf, gt, kbt = _proj_even(
            x, mods0, latent, norm_l0[1], w_even, w_even_t, gbi, gbf, gbt, qg, kg, bd, rope_tabs)
        if latent:
            extra = (cache_l0_attn_k.reshape(nl * CTX_LEN, 128), cache_l0_attn_v.reshape(nl * CTX_LEN, 128))
            m0 = state_l0_mlstm_m.reshape(nl, N_DIRHEAD)
            m0_lanes = jnp.zeros((nl, 8, 128), F32).at[:, 0, :N_DIRHEAD].set(m0)
            m0_rows = jnp.broadcast_to(m0[:, :, None], (nl, N_DIRHEAD, 128))
            n0_cols = jnp.broadcast_to(state_l0_mlstm_n[..., None], (nl, 2, N_HEADS, DH, 128))
            state = (state_l0_mlstm_C, n0_cols, m0_lanes, m0_rows)
        else:
            extra, state = None, None
        ya = _attention(qa, ka, va, n_batch=n_batch, q_len=seq_len, kv_len=seq_len, q_tile=256, group=4, extra=extra)
        ml = _mlstm(qb, kb, kbt, vb, ob, gi, gf, gt, head_g, bd, n_batch=n_batch, seq_len=seq_len, state=state)
        yb = ml[0]
        if not latent:
            c_fin, n_fin, m_fin = ml[1:]
        x = _ffn(x, mods0, 6, latent, norm_l0[2], f2i, f2o, mix=(ya, yb, wo0[:512], wo0[512:]))
        outs[latent] = (x, ka, va)
    xp, k0, v0 = outs[False]
    c0, n0, m0_pad = c_fin, n_fin, m_fin
    xs = outs[True][0]

    f1i, f1o, f2i, f2o = bf(ffn1_in_l1), bf(ffn1_out_l1), bf(ffn2_in_l1), bf(ffn2_out_l1)
    for latent, x in ((False, xp), (True, xs)):
        n_batch, seq_len = (nl, LAT_LEN) if latent else (nb, CTX_LEN)
        x = _ffn(x, mods1, 0, latent, norm_l1[0], f1i, f1o)
        qc, kc, vc, qd, kd, vd = _proj_odd(x, mods1, latent, norm_l1[1], w_odd, rope_tabs)
        if latent:
            yc = _na_latent(qc, kc, vc, cache_l1_na_k.reshape(nl * CTX_LEN, 512),
                            cache_l1_na_v.reshape(nl * CTX_LEN, 512), rpb_flat, nl)
            yd = _swa_latent(qd, kd, vd, cache_l1_swa_k.reshape(nl * CTX_LEN, 128),
                             cache_l1_swa_v.reshape(nl * CTX_LEN, 128), sink_l1, nl)
        else:
            yc = _attention(qc, kc, vc, n_batch=n_batch, q_len=seq_len, kv_len=seq_len, q_tile=256, group=1)
            yd = _attention(qd, kd, vd, n_batch=n_batch, q_len=seq_len, kv_len=seq_len, q_tile=256, group=4, sink=sink_l1)
        x = _ffn(x, mods1, 6, latent, norm_l1[2], f2i, f2o, mix=(yc, yd, wo1[:512], wo1[512:]), final_g=norm_final)
        outs[latent] = (x, kc, vc, kd, vd)
    y_prompt, kc1, vc1, kd1, vd1 = outs[False]
    y_sample = outs[True][0]

    return (y_prompt.reshape(nb, CTX_LEN, D_MODEL), y_sample.reshape(nl, LAT_LEN, D_MODEL),
            k0.reshape(nb, CTX_LEN, 2, DH), v0.reshape(nb, CTX_LEN, 2, DH),
            c0, n0, m0_pad[:, :, 0].reshape(nb, 2, N_HEADS),
            kc1.reshape(nb, CTX_LEN, N_HEADS, DH), vc1.reshape(nb, CTX_LEN, N_HEADS, DH),
            kd1.reshape(nb, CTX_LEN, 2, DH), vd1.reshape(nb, CTX_LEN, 2, DH))
```

```python
import functools

import jax
import jax.numpy as jnp
import numpy as np
from jax import lax
from jax.experimental import pallas as pl
from jax.experimental.pallas import tpu as pltpu

F32 = jnp.float32
BF16 = jnp.bfloat16

D_MODEL = 1024
DH = 64
D_FF = 2816
ADA_CHUNKS = 9
GRID_W = 64
LAT_LEN = 1024
CTX_LEN = 256
N_HEADS = 8
NA_KH = 8
NA_KW = 16
SWA_WIN = 128
ROPE_THETA = 10000.0
ROPE_FREQS = DH // 4
ATTN_SCALE = DH ** -0.5
NEG_INF = -1e30
EPS = 1e-6

TOKEN_TILE = 512
MLSTM_T = 256
FFN_CHUNKS = ((0, 1024), (1024, 2048), (2048, D_FF))
VMEM_LIMIT = 56 * 1024 * 1024


def _params(sem, vmem=VMEM_LIMIT):
    return pltpu.CompilerParams(dimension_semantics=sem, vmem_limit_bytes=vmem)


def _dot(a, b):
    return jnp.dot(a.astype(BF16), b.astype(BF16), preferred_element_type=F32)


def _dot_nt(a, b):
    return lax.dot_general(a.astype(BF16), b.astype(BF16), (((1,), (1,)), ((), ())),
                           preferred_element_type=F32)


def _split3(x):
    hi = x.astype(BF16)
    r1 = x - hi.astype(F32)
    mid = r1.astype(BF16)
    lo = (r1 - mid.astype(F32)).astype(BF16)
    return hi, mid, lo


def _dot_exact_rhs(a_bf16, x):
    hi, mid, lo = _split3(x)
    f = lambda p: jnp.dot(a_bf16, p, preferred_element_type=F32)
    return f(hi) + f(mid) + f(lo)


def _dot_exact_lhs(x, b_bf16):
    hi, mid, lo = _split3(x)
    f = lambda p: jnp.dot(p, b_bf16, preferred_element_type=F32)
    return f(hi) + f(mid) + f(lo)


def _silu(x):
    return x * jax.nn.sigmoid(x)


def _log_sigmoid(x):
    return jnp.minimum(x, 0.0) - jnp.log1p(jnp.exp(-jnp.abs(x)))


def _rms(x, g):
    return x * lax.rsqrt(jnp.mean(x * x, axis=-1, keepdims=True) + EPS) * g


def _head_mean_sq(x, bd_ref):
    n = x.shape[-1]
    return _dot_exact_lhs(x * x, bd_ref[:n, :n])


def _rope(x, cos, sin_signed):
    n = x.shape[-1]
    lane = lax.broadcasted_iota(jnp.int32, x.shape, 1)
    first_half = (lane % DH) < (DH // 2)
    partner = jnp.where(first_half, pltpu.roll(x, n - DH // 2, 1), pltpu.roll(x, DH // 2, 1))
    return x * cos[:, :n] + partner * sin_signed[:, :n]


def _adaln_kernel(c_ref, w_ref, b_ref, o_ref):
    s = _silu(c_ref[...])
    o_ref[...] = _dot(s, w_ref[...]) + b_ref[...]


def _adaln(cond8, w, b):
    n = w.shape[1]
    tn = 1536
    return pl.pallas_call(
        _adaln_kernel,
        out_shape=jax.ShapeDtypeStruct((8, n), F32),
        grid=(n // tn,),
        in_specs=[pl.BlockSpec((8, D_MODEL), lambda j: (0, 0)),
                  pl.BlockSpec((D_MODEL, tn), lambda j: (0, j)),
                  pl.BlockSpec((1, tn), lambda j: (0, j))],
        out_specs=pl.BlockSpec((8, tn), lambda j: (0, j)),
        compiler_params=_params(("arbitrary",)),
        name="adaln",
    )(cond8, w, b.reshape(1, n))


def _const_spec(shape):
    nd = len(shape)
    return pl.BlockSpec(shape, lambda i, _n=nd: (0,) * _n, pipeline_mode=pl.Buffered(1))


def _mod_spec(chunk, rowfn):
    return pl.BlockSpec((1, 1, D_MODEL), lambda i: (rowfn(i) * ADA_CHUNKS + chunk, 0, 0))


def _rowfn(latent):
    if latent:
        return lambda i: 1 + (i * TOKEN_TILE) // LAT_LEN
    return lambda i: 0


def _tok_spec(width):
    return pl.BlockSpec((TOKEN_TILE, width), lambda i: (i, 0))


def _seq_len(latent):
    return LAT_LEN if latent else CTX_LEN


def _transposed_out(t, latent, rows):
    seq = _seq_len(latent)
    shape = jax.ShapeDtypeStruct((t // seq, rows, seq), F32)
    if seq <= TOKEN_TILE:
        per = TOKEN_TILE // seq
        return shape, pl.BlockSpec((per, rows, seq), lambda i: (i, 0, 0))
    per = seq // TOKEN_TILE
    return shape, pl.BlockSpec((1, rows, TOKEN_TILE), lambda i: (i // per, 0, i % per))


def _store_transposed(ref, x):
    n, _, w = ref.shape
    for j in range(n):
        ref[j] = x[j * w:(j + 1) * w].T


def _store_proj_transposed(ref, w_rows, h):
    n, _, w = ref.shape
    for j in range(n):
        ref[j] = _dot_nt(w_rows, h[j * w:(j + 1) * w])


def _modnorm(x, ng_ref, sh_ref, sc_ref):
    return _rms(x, ng_ref[...]) * (1.0 + sc_ref[0]) + sh_ref[0]


def _ffn_kernel(*refs, has_mix, final_norm):
    it = iter(refs)
    x_ref = next(it)
    if has_mix:
        ya_ref, yb_ref, woa_ref, wob_ref, g2_ref = (next(it) for _ in range(5))
    ng_ref, sh_ref, sc_ref, g_ref, win_ref, wout_ref = (next(it) for _ in range(6))
    if final_norm:
        nf_ref = next(it)
    o_ref = next(it)

    x = x_ref[...]
    if has_mix:
        mo = _dot(ya_ref[...], woa_ref[...]) + _dot(yb_ref[...], wob_ref[...])
        x = x + g2_ref[0] * mo
    h = _modnorm(x, ng_ref, sh_ref, sc_ref).astype(BF16)
    acc = None
    for lo, hi in FFN_CHUNKS:
        g = _dot(h, win_ref[:, lo:hi])
        u = _dot(h, win_ref[:, D_FF + lo:D_FF + hi])
        part = _dot(_silu(g) * u, wout_ref[lo:hi, :])
        acc = part if acc is None else acc + part
    y = x + (0.5 * g_ref[0]) * acc
    if final_norm:
        y = _rms(y, nf_ref[...])
    o_ref[...] = y


def _ffn(x, mods, mod_base, latent, ng, w_in, w_out, mix=None, final_g=None):
    t = x.shape[0]
    rowfn = _rowfn(latent)
    args, specs = [x], [_tok_spec(D_MODEL)]
    if mix is not None:
        ya, yb, woa, wob = mix
        args += [ya, yb, woa, wob, mods]
        specs += [_tok_spec(ya.shape[1]), _tok_spec(yb.shape[1]), _const_spec(woa.shape), _const_spec(wob.shape),
                  _mod_spec(mod_base - 1, rowfn)]
    args += [ng.reshape(1, D_MODEL), mods, mods, mods, w_in, w_out]
    specs += [_const_spec((1, D_MODEL)), _mod_spec(mod_base, rowfn), _mod_spec(mod_base + 1, rowfn),
              _mod_spec(mod_base + 2, rowfn), _const_spec(w_in.shape), _const_spec(w_out.shape)]
    if final_g is not None:
        args.append(final_g.reshape(1, D_MODEL))
        specs.append(_const_spec((1, D_MODEL)))
    return pl.pallas_call(
        functools.partial(_ffn_kernel, has_mix=mix is not None, final_norm=final_g is not None),
        out_shape=jax.ShapeDtypeStruct((t, D_MODEL), F32),
        grid=(t // TOKEN_TILE,),
        in_specs=specs,
        out_specs=_tok_spec(D_MODEL),
        compiler_params=_params(("arbitrary",)),
        name="ffn",
    )(*args)


_EV_QA, _EV_KA, _EV_VA, _EV_QB, _EV_KB, _EV_VB, _EV_OB, _EV_GI, _EV_GF, _EV_END = (
    0, 512, 640, 768, 1280, 1792, 2304, 2816, 2944, 3072)
N_DIRHEAD = 2 * N_HEADS
_WT_VA = 2 * N_DIRHEAD + 512


def _proj_even_kernel(*refs, rope):
    it = iter(refs)
    x_ref, ng_ref, sh_ref, sc_ref, w_ref, wt_ref, gbi_ref, gbf_ref, gbt_ref, qg_ref, kg_ref, bd_ref = (
        next(it) for _ in range(12))
    if rope:
        cos_ref, sin_ref = next(it), next(it)
    qa_ref, kat_ref, vat_ref, qb_ref, kb_ref, vb_ref, ob_ref, gi_ref, gf_ref, gt_ref, kbt_ref = (next(it) for _ in range(11))

    h = _modnorm(x_ref[...], ng_ref, sh_ref, sc_ref).astype(BF16)
    proj = lambda lo, hi: _dot(h, w_ref[:, lo:hi])

    qa = proj(_EV_QA, _EV_KA)
    qa = qa * lax.rsqrt(_head_mean_sq(qa, bd_ref) + EPS) * qg_ref[...]
    ka = proj(_EV_KA, _EV_VA)
    ka = ka * lax.rsqrt(_head_mean_sq(ka, bd_ref) + EPS) * kg_ref[...]
    if rope:
        qa = _rope(qa, cos_ref[...], sin_ref[...])
        ka = _rope(ka, cos_ref[...], sin_ref[...])
    qa_ref[...] = qa * ATTN_SCALE
    _store_transposed(kat_ref, ka)
    _store_proj_transposed(vat_ref, wt_ref[_WT_VA:, :], h)
    qb_ref[...] = proj(_EV_QB, _EV_KB)
    kb_ref[...] = proj(_EV_KB, _EV_VB) * ATTN_SCALE
    vb_ref[...] = proj(_EV_VB, _EV_OB)
    ob_ref[...] = proj(_EV_OB, _EV_GI)
    gi_ref[...] = proj(_EV_GI, _EV_GF) + gbi_ref[...]
    gf_ref[...] = proj(_EV_GF, _EV_END) + gbf_ref[...]
    for j in range(TOKEN_TILE // MLSTM_T):
        ht = h[j * MLSTM_T:(j + 1) * MLSTM_T]
        gt_ref[j] = _dot_nt(wt_ref[:2 * N_DIRHEAD, :], ht) + gbt_ref[...]
        kbt_ref[j] = _dot_nt(wt_ref[2 * N_DIRHEAD:_WT_VA, :], ht) * ATTN_SCALE


def _proj_even(x, mods, latent, ng, w, wt, gbi, gbf, gbt, qg, kg, bd, rope_tabs):
    t = x.shape[0]
    rowfn = _rowfn(latent)
    args = [x, ng.reshape(1, D_MODEL), mods, mods, w, wt, gbi, gbf, gbt, qg, kg, bd]
    specs = [_tok_spec(D_MODEL), _const_spec((1, D_MODEL)), _mod_spec(3, rowfn), _mod_spec(4, rowfn)]
    specs += [_const_spec(a.shape) for a in args[4:]]
    if latent:
        nblk = LAT_LEN // TOKEN_TILE
        args += list(rope_tabs)
        specs += [pl.BlockSpec((TOKEN_TILE, 512), lambda i: (i % nblk, 0))] * 2
    widths = (512, 512, 512, 512, 512, 128, 128)
    out_shape = [jax.ShapeDtypeStruct((t, wd), F32) for wd in widths]
    out_specs = [_tok_spec(wd) for wd in widths]
    for pos in (1, 2):
        shape, spec = _transposed_out(t, latent, 128)
        out_shape.insert(pos, shape)
        out_specs.insert(pos, spec)
    nj = TOKEN_TILE // MLSTM_T
    for rows in (2 * N_DIRHEAD, 512):
        out_shape.append(jax.ShapeDtypeStruct((t // MLSTM_T, rows, MLSTM_T), F32))
        out_specs.append(pl.BlockSpec((nj, rows, MLSTM_T), lambda i: (i, 0, 0)))
    return pl.pallas_call(
        functools.partial(_proj_even_kernel, rope=latent),
        out_shape=out_shape, grid=(t // TOKEN_TILE,), in_specs=specs, out_specs=out_specs,
        compiler_params=_params(("arbitrary",)), name="proj_even",
    )(*args)


def _proj_odd_kernel(*refs, rope):
    it = iter(refs)
    x_ref, ng_ref, sh_ref, sc_ref, w_ref = (next(it) for _ in range(5))
    if rope:
        cos_ref, sin_ref = next(it), next(it)
    else:
        wt_ref = next(it)
    qc_ref, kc_ref, vc_ref, qd_ref, kd_ref, vd_ref = (next(it) for _ in range(6))

    h = _modnorm(x_ref[...], ng_ref, sh_ref, sc_ref).astype(BF16)
    proj = lambda lo, hi: _dot(h, w_ref[:, lo:hi])
    qc_ref[...] = proj(0, 512) * ATTN_SCALE
    kc_ref[...] = proj(512, 1024)
    vc_ref[...] = proj(1024, 1536)
    qd = proj(1536, 2048)
    if rope:
        qd = _rope(qd, cos_ref[...], sin_ref[...])
        kd_ref[...] = _rope(proj(2048, 2176), cos_ref[...], sin_ref[...])
        vd_ref[...] = proj(2176, 2304)
    else:
        _store_proj_transposed(kd_ref, wt_ref[:128, :], h)
        _store_proj_transposed(vd_ref, wt_ref[128:, :], h)
    qd_ref[...] = qd * ATTN_SCALE


def _proj_odd(x, mods, latent, ng, w, wt, rope_tabs):
    t = x.shape[0]
    rowfn = _rowfn(latent)
    args = [x, ng.reshape(1, D_MODEL), mods, mods, w]
    specs = [_tok_spec(D_MODEL), _const_spec((1, D_MODEL)), _mod_spec(3, rowfn), _mod_spec(4, rowfn), _const_spec(w.shape)]
    if latent:
        nblk = LAT_LEN // TOKEN_TILE
        args += list(rope_tabs)
        specs += [pl.BlockSpec((TOKEN_TILE, 512), lambda i: (i % nblk, 0))] * 2
    else:
        args.append(wt)
        specs.append(_const_spec(wt.shape))
    widths = (512, 512, 512, 512) + ((128, 128) if latent else ())
    out_shape = [jax.ShapeDtypeStruct((t, wd), F32) for wd in widths]
    out_specs = [_tok_spec(wd) for wd in widths]
    if not latent:
        for _ in range(2):
            shape, spec = _transposed_out(t, latent, 128)
            out_shape.append(shape)
            out_specs.append(spec)
    return pl.pallas_call(
        functools.partial(_proj_odd_kernel, rope=latent),
        out_shape=out_shape,
        grid=(t // TOKEN_TILE,), in_specs=specs, out_specs=out_specs,
        compiler_params=_params(("arbitrary",)), name="proj_odd",
    )(*args)


def _attn_kernel(*refs, group, kv_t, has_extra, has_sink):
    it = iter(refs)
    q_ref, k_ref, v_ref = next(it), next(it), next(it)
    if has_extra:
        ke_ref, ve_ref = next(it), next(it)
    if has_sink:
        sink_ref = next(it)
    o_ref = next(it)
    tq = q_ref.shape[0]
    first = lax.broadcasted_iota(jnp.int32, (tq, 128), 1) < DH

    def kv_operands(kt_ref, vt_ref, kv):
        kt = kt_ref[0, kv * DH:(kv + 1) * DH, :].astype(BF16)
        vt = vt_ref[0, kv * DH:(kv + 1) * DH, :].astype(BF16)
        z = jnp.zeros_like(kt)
        ones = jnp.ones((128, kt.shape[1]), BF16)
        return ((jnp.concatenate([kt, z], axis=0), jnp.concatenate([z, kt], axis=0)),
                (jnp.concatenate([vt, z, ones], axis=0), jnp.concatenate([z, vt, ones], axis=0)))

    cache = {}
    for j in range(N_HEADS // 2):
        ps = slice(128 * j, 128 * (j + 1))
        q_pair = q_ref[:, ps]
        if kv_t:
            kv = 2 * j // group
            if kv not in cache:
                cache[kv] = (kv_operands(k_ref, v_ref, kv), kv_operands(ke_ref, ve_ref, kv) if has_extra else None)
            (k_ops, v_ops), extra = cache[kv]
            qb = q_pair.astype(BF16)
        else:
            k_pair = k_ref[:, ps].astype(BF16)
            v_aug = jnp.concatenate([v_ref[:, ps].astype(BF16), jnp.ones((k_pair.shape[0], 128), BF16)], axis=1)
        res, sink_den = [], []
        for half in range(2):
            if kv_t:
                s = jnp.dot(qb, k_ops[half], preferred_element_type=F32)
            else:
                s = _dot_nt(jnp.where(first, q_pair, 0.0) if half == 0 else jnp.where(first, 0.0, q_pair), k_pair)
            m = jnp.max(s, axis=-1, keepdims=True)
            if has_extra:
                se = jnp.dot(qb, extra[0][half], preferred_element_type=F32)
                m = jnp.maximum(m, jnp.max(se, axis=-1, keepdims=True))
            if has_sink:
                m = jnp.maximum(m, sink_ref[2 * j + half])
                sink_den.append(jnp.exp(sink_ref[2 * j + half] - m))
            p = jnp.exp(s - m)
            r = _dot_nt(p, v_ops[half]) if kv_t else _dot(p, v_aug)
            if has_extra:
                r = r + _dot_nt(jnp.exp(se - m), extra[1][half])
            res.append(r)
        num = jnp.where(first, res[0][:, :128], res[1][:, :128])
        den = jnp.where(first, res[0][:, 128:], res[1][:, 128:])
        if has_sink:
            den = den + jnp.where(first, sink_den[0], sink_den[1])
        o_ref[:, ps] = num / den


def _attention(q, k, v, *, n_batch, q_len, q_tile, group, kv_t, extra=None, sink=None):
    nq = q_len // q_tile
    args = [q, k, v]
    specs = [pl.BlockSpec((q_tile, 512), lambda b, j: (b * nq + j, 0))]
    if kv_t:
        kv_spec = lambda a: pl.BlockSpec((1,) + a.shape[1:], lambda b, j: (b, 0, 0))
        specs += [kv_spec(k), kv_spec(v)]
        if extra is not None:
            args += list(extra)
            specs += [kv_spec(extra[0]), kv_spec(extra[1])]
    else:
        assert extra is None
        keys = k.shape[0] // n_batch
        specs += [pl.BlockSpec((keys, 512), lambda b, j: (b, 0))] * 2
    if sink is not None:
        args.append(sink)
        specs.append(pl.BlockSpec(memory_space=pltpu.SMEM))
    return pl.pallas_call(
        functools.partial(_attn_kernel, group=group, kv_t=kv_t, has_extra=extra is not None, has_sink=sink is not None),
        out_shape=jax.ShapeDtypeStruct(q.shape, F32),
        grid=(n_batch, nq), in_specs=specs,
        out_specs=pl.BlockSpec((q_tile, 512), lambda b, j: (b * nq + j, 0)),
        compiler_params=_params(("arbitrary", "arbitrary")), name="attn",
    )(*args)


LOG2E = 1.4426950408889634


def _cummax_time(x, reverse):
    n = x.shape[0]
    row = lax.broadcasted_iota(jnp.int32, x.shape, 0)
    k = 1
    while k < n:
        if reverse:
            shifted = jnp.where(row < n - k, pltpu.roll(x, n - k, 0), -jnp.inf)
        else:
            shifted = jnp.where(row >= k, pltpu.roll(x, k, 0), -jnp.inf)
        x = jnp.maximum(x, shifted)
        k *= 2
    return x


def _mlstm_kernel(*refs, n_chunks, carry):
    it = iter(refs)
    q_ref, k_ref, kt_ref, v_ref, ob_ref, gi_ref, gf_ref, gt_ref, hg_ref, bd_ref = (next(it) for _ in range(10))
    if carry:
        c0_ref, n0_ref, m0r_ref, m0c_ref = (next(it) for _ in range(4))
    y_ref = next(it)
    if not carry:
        c_ref, n_ref, m_ref = next(it), next(it), next(it)
    hf_s, hb_s = next(it), next(it)
    if carry:
        st_s, mrow_s, mcol_s = next(it), next(it), next(it)
    T = MLSTM_T
    step = pl.program_id(1)

    if carry:
        @pl.when(step == 0)
        def _():
            z = jnp.zeros((DH, DH), F32)
            for d in range(2):
                for j in range(N_HEADS // 2):
                    ca, cb = c0_ref[0, d, 2 * j], c0_ref[0, d, 2 * j + 1]
                    na, nb = n0_ref[0, d, 2 * j][:, :DH], n0_ref[0, d, 2 * j + 1][:, :DH]
                    st_s[d, j, 0:DH, :] = jnp.concatenate([ca, z, na, z], axis=1)
                    st_s[d, j, DH:2 * DH, :] = jnp.concatenate([z, cb, z, nb], axis=1)
            mrow_s[...] = m0r_ref[0]
            mcol_s[...] = m0c_ref[0]

    row = lax.broadcasted_iota(jnp.int32, (T, T), 0)
    col = lax.broadcasted_iota(jnp.int32, (T, T), 1)
    lower = row >= col
    upper = row <= col
    lower_m = jnp.where(lower, 1.0, 0.0).astype(BF16)
    upper_m = jnp.where(upper, 1.0, 0.0).astype(BF16)
    lane = lax.broadcasted_iota(jnp.int32, (T, 128), 1)
    first = lane < DH
    lane_c = lax.broadcasted_iota(jnp.int32, (DH, 256), 1) % 128
    zeros_kt = jnp.zeros((DH, T), BF16)
    ones_v = jnp.ones((T, 128), BF16)

    for d in range(2):
        chunk = step if d == 0 else n_chunks - 1 - step
        rows = pl.ds(pl.multiple_of(chunk * T, T), T)
        hd0 = N_HEADS * d
        mask = lower if d == 0 else upper

        gi, gf = gi_ref[rows, :], gf_ref[rows, :]
        b_cols = _dot_exact_rhs(lower_m if d == 0 else upper_m, _log_sigmoid(gf))
        a_cols = gi - b_cols
        dmax = b_cols + _cummax_time(a_cols, reverse=d == 1)
        if carry:
            inter = b_cols + mrow_s[0:1, :]
            mt = jnp.maximum(inter, dmax)
            w_inter = jnp.exp(inter - mt)
        else:
            mt = jnp.maximum(b_cols, dmax)
        c2 = (b_cols - mt) * LOG2E
        einv = jnp.exp(-mt)

        gt = gt_ref[chunk]
        gi_t = gt[hd0:hd0 + N_HEADS, :]
        gf_t = gt[N_DIRHEAD + hd0:N_DIRHEAD + hd0 + N_HEADS, :]
        b_rows = _dot_exact_lhs(_log_sigmoid(gf_t), upper_m if d == 0 else lower_m)
        a_rows = gi_t - b_rows
        a2 = a_rows * LOG2E
        b_last = b_rows[:, T - 1:T] if d == 0 else b_rows[:, 0:1]
        dec_rows = b_last + a_rows
        m_new = jnp.max(dec_rows, axis=-1, keepdims=True)
        if carry:
            m_prev = mcol_s[hd0:hd0 + N_HEADS, 0:1]
            m_new = jnp.maximum(m_new, b_last + m_prev)
            wc = jnp.exp(b_last + m_prev - m_new)
        else:
            m_new = jnp.maximum(m_new, b_last)
        ws_rows = jnp.exp(dec_rows - m_new)

        h_s = hf_s if d == 0 else hb_s
        for j in range(N_HEADS // 2):
            ps = slice(128 * j, 128 * (j + 1))
            q_pair = q_ref[rows, ps].astype(BF16)
            v_aug = jnp.concatenate([v_ref[rows, ps].astype(BF16), ones_v], axis=1)
            kt = (kt_ref[chunk, 128 * j:128 * j + DH, :], kt_ref[chunk, 128 * j + DH:128 * (j + 1), :])
            res, cu = [], []
            for half in range(2):
                hd = 2 * j + half
                kt0 = jnp.concatenate([kt[0].astype(BF16), zeros_kt] if half == 0 else [zeros_kt, kt[1].astype(BF16)], axis=0)
                qk = jnp.dot(q_pair, kt0, preferred_element_type=F32)
                e = c2[:, hd0 + hd:hd0 + hd + 1] + a2[hd:hd + 1, :]
                w = jnp.exp2(jnp.where(mask, e, -jnp.inf)) * qk
                res.append(_dot(w, v_aug))
                cu.append(_dot(kt[half] * ws_rows[hd:hd + 1, :], v_aug))
            ra, rb = hd0 + 2 * j, hd0 + 2 * j + 1
            num = jnp.where(first, res[0][:, :128], res[1][:, :128])
            den = jnp.where(first, res[0][:, 128:], res[1][:, 128:])
            if carry:
                old = st_s[d, j]
                qs = _dot(q_pair, old)
                wi = jnp.where(first, w_inter[:, ra:ra + 1], w_inter[:, rb:rb + 1])
                num = num + wi * qs[:, :128]
                den = den + wi * qs[:, 128:]
            floor = jnp.where(first, einv[:, ra:ra + 1], einv[:, rb:rb + 1])
            h_s[rows, ps] = num / jnp.maximum(jnp.abs(den), floor)
            if carry:
                st_s[d, j, 0:DH, :] = jnp.where(lane_c < DH, cu[0], 0.0) + wc[2 * j:2 * j + 1, :] * old[0:DH]
                st_s[d, j, DH:2 * DH, :] = jnp.where(lane_c >= DH, cu[1], 0.0) + wc[2 * j + 1:2 * j + 2, :] * old[DH:2 * DH]
            else:
                c_ref[0, d, 2 * j] = cu[0][:, 0:DH]
                c_ref[0, d, 2 * j + 1] = cu[1][:, DH:2 * DH]

        if carry:
            b_last_l = b_cols[T - 1:T, :] if d == 0 else b_cols[0:1, :]
            m_prev_l = mrow_s[0:1, :]
            m_new_l = jnp.maximum(jnp.max(b_last_l + a_cols, axis=0, keepdims=True), b_last_l + m_prev_l)
            lane_r = lax.broadcasted_iota(jnp.int32, (1, 128), 1)
            mine = (lane_r >= hd0) & (lane_r < hd0 + N_HEADS)
            mrow_s[0:1, :] = jnp.where(mine, m_new_l, m_prev_l)
            mcol_s[hd0:hd0 + N_HEADS, :] = jnp.broadcast_to(m_new, (N_HEADS, 128))
        else:
            n_all = _dot(ws_rows, k_ref[rows, :])
            for hd in range(N_HEADS):
                n_ref[0, d, hd:hd + 1, :] = n_all[hd:hd + 1, hd * DH:(hd + 1) * DH]
            m_ref[0, hd0:hd0 + N_HEADS, :] = jnp.broadcast_to(m_new, (N_HEADS, 128))

    @pl.when(step == n_chunks - 1)
    def _():
        hsum = hf_s[...] + hb_s[...]
        y = hsum * lax.rsqrt(_head_mean_sq(hsum, bd_ref) + EPS) * hg_ref[...]
        y_ref[...] = jax.nn.sigmoid(ob_ref[...]) * y


def _mlstm(qb, kb, kbt, vb, ob, gi, gf, gt, head_g, bd, *, n_batch, seq_len, state=None):
    n_chunks = seq_len // MLSTM_T
    carry = state is not None
    assert carry or n_chunks == 1
    seq = lambda wd: pl.BlockSpec((seq_len, wd), lambda b, c: (b, 0))
    per_chunk = lambda r: pl.BlockSpec((n_chunks, r, MLSTM_T), lambda b, c: (b, 0, 0))
    args = [qb, kb, kbt, vb, ob, gi, gf, gt, head_g, bd]
    specs = [seq(512), seq(512), per_chunk(512), seq(512), seq(512), seq(128), seq(128), per_chunk(2 * N_DIRHEAD),
             pl.BlockSpec((1, 512), lambda b, c: (0, 0)), pl.BlockSpec(bd.shape, lambda b, c: (0, 0))]
    out_shape = [jax.ShapeDtypeStruct((n_batch * seq_len, 512), F32)]
    out_specs = [seq(512)]
    scratch = [pltpu.VMEM((seq_len, 512), F32), pltpu.VMEM((seq_len, 512), F32)]
    if carry:
        args += list(state)
        specs += [pl.BlockSpec((1, 2, N_HEADS, DH, DH), lambda b, c: (b, 0, 0, 0, 0)),
                  pl.BlockSpec((1, 2, N_HEADS, DH, 128), lambda b, c: (b, 0, 0, 0, 0)),
                  pl.BlockSpec((1, 8, 128), lambda b, c: (b, 0, 0)),
                  pl.BlockSpec((1, N_DIRHEAD, 128), lambda b, c: (b, 0, 0))]
        scratch += [pltpu.VMEM((2, N_HEADS // 2, 128, 256), F32), pltpu.VMEM((8, 128), F32),
                    pltpu.VMEM((N_DIRHEAD, 128), F32)]
    else:
        out_shape += [jax.ShapeDtypeStruct((n_batch, 2, N_HEADS, DH, DH), F32),
                      jax.ShapeDtypeStruct((n_batch, 2, N_HEADS, DH), F32),
                      jax.ShapeDtypeStruct((n_batch, N_DIRHEAD, 128), F32)]
        out_specs += [pl.BlockSpec((1, 2, N_HEADS, DH, DH), lambda b, c: (b, 0, 0, 0, 0)),
                      pl.BlockSpec((1, 2, N_HEADS, DH), lambda b, c: (b, 0, 0, 0)),
                      pl.BlockSpec((1, N_DIRHEAD, 128), lambda b, c: (b, 0, 0))]
    return pl.pallas_call(
        functools.partial(_mlstm_kernel, n_chunks=n_chunks, carry=carry),
        out_shape=out_shape, grid=(n_batch, n_chunks), in_specs=specs, out_specs=out_specs,
        scratch_shapes=scratch,
        compiler_params=_params(("arbitrary", "arbitrary")), name="mlstm",
    )(*args)


N_ROWS = LAT_LEN // GRID_W
N_DY = 2 * NA_KH - 1
N_DX = 2 * NA_KW - 1


def _na_kernel(q_ref, k_ref, v_ref, ke_ref, ve_ref, rpb_ref, o_ref, tile_s, slab_s):
    @pl.when(pl.program_id(0) == 0)
    def _():
        qc = lax.broadcasted_iota(jnp.int32, (GRID_W, GRID_W), 0)
        kc = lax.broadcasted_iota(jnp.int32, (GRID_W, GRID_W), 1)
        start = jnp.clip(qc - NA_KW // 2, 0, GRID_W - NA_KW)
        in_win = (kc >= start) & (kc < start + NA_KW)
        dx = kc - qc + (NA_KW - 1)

        def build_tile(idx, carry):
            tile = jnp.full((GRID_W, GRID_W), NEG_INF, F32)
            for j in range(N_DX):
                tile = jnp.where(dx == j, rpb_ref[idx * N_DX + j], tile)
            tile_s[idx] = jnp.where(in_win, tile, NEG_INF)
            return carry

        lax.fori_loop(0, N_HEADS * N_DY, build_tile, 0)

        def build_slab(idx, carry):
            first = (idx // NA_KH) * N_DY + idx % NA_KH
            for j in range(NA_KH):
                slab_s[idx, :, j * GRID_W:(j + 1) * GRID_W] = tile_s[first + j]
            return carry

        lax.fori_loop(0, N_HEADS * NA_KH, build_slab, 0)

    def row_body(r, carry):
        r0 = jnp.clip(r - NA_KH // 2, 0, N_ROWS - NA_KH)
        dy0 = r0 - r + NA_KH - 1
        qrows = pl.ds(pl.multiple_of(r * GRID_W, GRID_W), GRID_W)
        keys = pl.ds(pl.multiple_of(r0 * GRID_W, GRID_W), NA_KH * GRID_W)
        for hd in range(N_HEADS):
            hs = slice(hd * DH, (hd + 1) * DH)
            q = q_ref[qrows, hs].astype(BF16)
            s = _dot_nt(q, k_ref[keys, hs]) + slab_s[hd * NA_KH + dy0]
            se = _dot_nt(q, ke_ref[:, hs])
            m = jnp.maximum(jnp.max(s, axis=-1, keepdims=True), jnp.max(se, axis=-1, keepdims=True))
            p, pe = jnp.exp(s - m), jnp.exp(se - m)
            den = jnp.sum(p, axis=-1, keepdims=True) + jnp.sum(pe, axis=-1, keepdims=True)
            o_ref[qrows, hs] = (_dot(p, v_ref[keys, hs]) + _dot(pe, ve_ref[:, hs])) / den
        return carry

    lax.fori_loop(0, N_ROWS, row_body, 0)


def _na_latent(q, k, v, k_ctx, v_ctx, rpb_flat, n_batch):
    seq = pl.BlockSpec((LAT_LEN, 512), lambda b: (b, 0))
    ctx = pl.BlockSpec((CTX_LEN, 512), lambda b: (b, 0))
    return pl.pallas_call(
        _na_kernel,
        out_shape=jax.ShapeDtypeStruct(q.shape, F32),
        grid=(n_batch,),
        in_specs=[seq, seq, seq, ctx, ctx, pl.BlockSpec(memory_space=pltpu.SMEM)],
        out_specs=seq,
        scratch_shapes=[pltpu.VMEM((N_HEADS * N_DY, GRID_W, GRID_W), F32),
                        pltpu.VMEM((N_HEADS * NA_KH, GRID_W, NA_KH * GRID_W), F32)],
        compiler_params=_params(("arbitrary",)), name="na_latent",
    )(q, k, v, k_ctx, v_ctx, rpb_flat)


SWA_QB = 128


def _swa_kernel(q_ref, k_ref, v_ref, ke_ref, ve_ref, sink_ref, o_ref, *, group):
    span = SWA_QB + 2 * SWA_WIN

    def block_body(n, carry):
        q0 = n * SWA_QB
        lo = pl.multiple_of(jnp.clip(q0 - SWA_WIN, 0, LAT_LEN - span), SWA_QB)
        qrows = pl.ds(pl.multiple_of(q0, SWA_QB), SWA_QB)
        keys = pl.ds(lo, span)
        qpos = q0 + lax.broadcasted_iota(jnp.int32, (SWA_QB, span), 0)
        kpos = lo + lax.broadcasted_iota(jnp.int32, (SWA_QB, span), 1)
        near = jnp.abs(kpos - qpos) <= SWA_WIN
        for hd in range(N_HEADS):
            kv = hd // group
            hs = slice(hd * DH, (hd + 1) * DH)
            ks = slice(kv * DH, (kv + 1) * DH)
            sink = sink_ref[hd]
            q = q_ref[qrows, hs].astype(BF16)
            s = jnp.where(near, _dot_nt(q, k_ref[keys, ks]), NEG_INF)
            se = _dot_nt(q, ke_ref[:, ks])
            m = jnp.maximum(jnp.max(s, axis=-1, keepdims=True), jnp.max(se, axis=-1, keepdims=True))
            m = jnp.maximum(m, sink)
            p, pe = jnp.exp(s - m), jnp.exp(se - m)
            den = jnp.sum(p, axis=-1, keepdims=True) + jnp.sum(pe, axis=-1, keepdims=True) + jnp.exp(sink - m)
            o_ref[qrows, hs] = (_dot(p, v_ref[keys, ks]) + _dot(pe, ve_ref[:, ks])) / den
        return carry

    lax.fori_loop(0, LAT_LEN // SWA_QB, block_body, 0)


def _swa_latent(q, k, v, k_ctx, v_ctx, sink, n_batch):
    kvw = k.shape[1]
    return pl.pallas_call(
        functools.partial(_swa_kernel, group=N_HEADS * DH // kvw),
        out_shape=jax.ShapeDtypeStruct(q.shape, F32),
        grid=(n_batch,),
        in_specs=[pl.BlockSpec((LAT_LEN, 512), lambda b: (b, 0)),
                  pl.BlockSpec((LAT_LEN, kvw), lambda b: (b, 0)), pl.BlockSpec((LAT_LEN, kvw), lambda b: (b, 0)),
                  pl.BlockSpec((CTX_LEN, kvw), lambda b: (b, 0)), pl.BlockSpec((CTX_LEN, kvw), lambda b: (b, 0)),
                  pl.BlockSpec(memory_space=pltpu.SMEM)],
        out_specs=pl.BlockSpec((LAT_LEN, 512), lambda b: (b, 0)),
        compiler_params=_params(("arbitrary",)), name="swa_latent",
    )(q, k, v, k_ctx, v_ctx, sink)


def _rope_tables():
    t = jnp.arange(LAT_LEN)
    pos = jnp.stack([t // GRID_W, t % GRID_W], axis=-1).astype(F32)
    freqs = ROPE_THETA ** (-jnp.arange(ROPE_FREQS, dtype=F32) / ROPE_FREQS)
    ang = (pos[:, :, None] * freqs).reshape(LAT_LEN, 2 * ROPE_FREQS)
    cos, sin = jnp.cos(ang), jnp.sin(ang)
    cos_t = jnp.tile(jnp.concatenate([cos, cos], axis=-1), (1, N_HEADS))
    sin_t = jnp.tile(jnp.concatenate([-sin, sin], axis=-1), (1, N_HEADS))
    return cos_t, sin_t


def kernel(x_prompt, x_sample, cache_l0_attn_k, cache_l0_attn_v, state_l0_mlstm_C, state_l0_mlstm_n, state_l0_mlstm_m, cache_l1_na_k, cache_l1_na_v, cache_l1_swa_k, cache_l1_swa_v, c, c_ctx, norm_final, ada_w_l0, ada_b_l0, norm_l0, ffn1_in_l0, ffn1_out_l0, ffn2_in_l0, ffn2_out_l0, mix_in_l0, mix_out_l0, qk_norm_l0, gate_bias_l0, head_norm_l0, ada_w_l1, ada_b_l1, norm_l1, ffn1_in_l1, ffn1_out_l1, ffn2_in_l1, ffn2_out_l1, mix_in_l1, mix_out_l1, rpb_l1, sink_l1):
    nb, nl = x_prompt.shape[0], x_sample.shape[0]
    bf = lambda a: a.astype(BF16)

    cond8 = jnp.zeros((8, D_MODEL), F32).at[0].set(c_ctx).at[1:1 + nl].set(c)
    mods0 = _adaln(cond8, ada_w_l0, ada_b_l0).reshape(8 * ADA_CHUNKS, 1, D_MODEL)
    mods1 = _adaln(cond8, ada_w_l1, ada_b_l1).reshape(8 * ADA_CHUNKS, 1, D_MODEL)

    g0 = 2304
    gcols = lambda j: mix_in_l0[:, g0 + 8 * j:g0 + 8 * (j + 1)]
    gpad = jnp.zeros((D_MODEL, 128 - N_DIRHEAD), F32)
    w_gi, w_gf = jnp.concatenate([gcols(0), gcols(2)], axis=1), jnp.concatenate([gcols(1), gcols(3)], axis=1)
    w_even = bf(jnp.concatenate([mix_in_l0[:, :g0], mix_in_l0[:, g0 + 32:], w_gi, gpad, w_gf, gpad], axis=1))
    w_even_t = bf(jnp.concatenate([w_gi, w_gf, mix_in_l0[:, 1280:1792], mix_in_l0[:, 640:768]], axis=1).T)
    gb4 = gate_bias_l0.reshape(4, N_HEADS)
    b_gi, b_gf = jnp.concatenate([gb4[0], gb4[2]]), jnp.concatenate([gb4[1], gb4[3]])
    gbi = jnp.zeros((1, 128), F32).at[0, :N_DIRHEAD].set(b_gi)
    gbf = jnp.zeros((1, 128), F32).at[0, :N_DIRHEAD].set(b_gf)
    gbt = jnp.concatenate([b_gi, b_gf]).reshape(2 * N_DIRHEAD, 1)
    qg = jnp.tile(qk_norm_l0[0], N_HEADS).reshape(1, 512)
    kg = jnp.tile(qk_norm_l0[1], 2).reshape(1, 128)
    head_g = head_norm_l0.reshape(1, 512)
    grp = np.arange(512) // DH
    bd = jnp.asarray((grp[:, None] == grp[None, :]).astype(np.float32) / DH, dtype=BF16)
    rope_tabs = _rope_tables()
    w_odd = bf(mix_in_l1)
    w_odd_t = bf(mix_in_l1[:, 2048:2304].T)
    wo0, wo1 = bf(mix_out_l0), bf(mix_out_l1)
    rpb_flat = rpb_l1.reshape(-1)
    to_t = lambda a: a.reshape(a.shape[0], a.shape[1], -1).transpose(0, 2, 1)
    from_t = lambda a: a.reshape(a.shape[0], -1, DH, a.shape[2]).transpose(0, 3, 1, 2)

    xp = x_prompt.reshape(nb * CTX_LEN, D_MODEL)
    xs = x_sample.reshape(nl * LAT_LEN, D_MODEL)

    f1i, f1o, f2i, f2o = bf(ffn1_in_l0), bf(ffn1_out_l0), bf(ffn2_in_l0), bf(ffn2_out_l0)
    outs = {}
    for latent, x in ((False, xp), (True, xs)):
        n_batch, seq_len = (nl, LAT_LEN) if latent else (nb, CTX_LEN)
        x = _ffn(x, mods0, 0, latent, norm_l0[0], f1i, f1o)
        qa, kat, vat, qb, kb, vb, ob, gi, gf, gt, kbt = _proj_even(
            x, mods0, latent, norm_l0[1], w_even, w_even_t, gbi, gbf, gbt, qg, kg, bd, rope_tabs)
        if latent:
            extra = (to_t(cache_l0_attn_k), to_t(cache_l0_attn_v))
            m0 = state_l0_mlstm_m.reshape(nl, N_DIRHEAD)
            m0_lanes = jnp.zeros((nl, 8, 128), F32).at[:, 0, :N_DIRHEAD].set(m0)
            m0_rows = jnp.broadcast_to(m0[:, :, None], (nl, N_DIRHEAD, 128))
            n0_cols = jnp.broadcast_to(state_l0_mlstm_n[..., None], (nl, 2, N_HEADS, DH, 128))
            state = (state_l0_mlstm_C, n0_cols, m0_lanes, m0_rows)
        else:
            extra, state = None, None
        ya = _attention(qa, kat, vat, n_batch=n_batch, q_len=seq_len, q_tile=256, group=4, kv_t=True, extra=extra)
        ml = _mlstm(qb, kb, kbt, vb, ob, gi, gf, gt, head_g, bd, n_batch=n_batch, seq_len=seq_len, state=state)
        yb = ml[0]
        if not latent:
            c_fin, n_fin, m_fin = ml[1:]
        x = _ffn(x, mods0, 6, latent, norm_l0[2], f2i, f2o, mix=(ya, yb, wo0[:512], wo0[512:]))
        outs[latent] = (x, kat, vat)
    xp, k0t, v0t = outs[False]
    c0, n0, m0_pad = c_fin, n_fin, m_fin
    xs = outs[True][0]

    f1i, f1o, f2i, f2o = bf(ffn1_in_l1), bf(ffn1_out_l1), bf(ffn2_in_l1), bf(ffn2_out_l1)
    for latent, x in ((False, xp), (True, xs)):
        n_batch, seq_len = (nl, LAT_LEN) if latent else (nb, CTX_LEN)
        x = _ffn(x, mods1, 0, latent, norm_l1[0], f1i, f1o)
        qc, kc, vc, qd, kd, vd = _proj_odd(x, mods1, latent, norm_l1[1], w_odd, w_odd_t, rope_tabs)
        if latent:
            yc = _na_latent(qc, kc, vc, cache_l1_na_k.reshape(nl * CTX_LEN, 512),
                            cache_l1_na_v.reshape(nl * CTX_LEN, 512), rpb_flat, nl)
            yd = _swa_latent(qd, kd, vd, cache_l1_swa_k.reshape(nl * CTX_LEN, 128),
                             cache_l1_swa_v.reshape(nl * CTX_LEN, 128), sink_l1, nl)
        else:
            yc = _attention(qc, kc, vc, n_batch=n_batch, q_len=seq_len, q_tile=256, group=1, kv_t=False)
            yd = _attention(qd, kd, vd, n_batch=n_batch, q_len=seq_len, q_tile=256, group=4, kv_t=True, sink=sink_l1)
        x = _ffn(x, mods1, 6, latent, norm_l1[2], f2i, f2o, mix=(yc, yd, wo1[:512], wo1[512:]), final_g=norm_final)
        outs[latent] = (x, kc, vc, kd, vd)
    y_prompt, kc1, vc1, kd1t, vd1t = outs[False]
    y_sample = outs[True][0]

    return (y_prompt.reshape(nb, CTX_LEN, D_MODEL), y_sample.reshape(nl, LAT_LEN, D_MODEL),
            from_t(k0t), from_t(v0t),
            c0, n0, m0_pad[:, :, 0].reshape(nb, 2, N_HEADS),
            kc1.reshape(nb, CTX_LEN, N_HEADS, DH), vc1.reshape(nb, CTX_LEN, N_HEADS, DH),
            from_t(kd1t), from_t(vd1t))
```

```python
import functools

import jax
import jax.numpy as jnp
import numpy as np
from jax import lax
from jax.experimental import pallas as pl
from jax.experimental.pallas import tpu as pltpu

F32 = jnp.float32
BF16 = jnp.bfloat16

D_MODEL = 1024
DH = 64
D_FF = 2816
ADA_CHUNKS = 9
GRID_W = 64
LAT_LEN = 1024
CTX_LEN = 256
N_HEADS = 8
NA_KH = 8
NA_KW = 16
SWA_WIN = 128
ROPE_THETA = 10000.0
ROPE_FREQS = DH // 4
ATTN_SCALE = DH ** -0.5
NEG_INF = -1e30
EPS = 1e-6

TOKEN_TILE = 512
MLSTM_T = 256
FFN_CHUNKS = ((0, 1024), (1024, 2048), (2048, D_FF))
VMEM_LIMIT = 56 * 1024 * 1024


def _params(sem, vmem=VMEM_LIMIT):
    return pltpu.CompilerParams(dimension_semantics=sem, vmem_limit_bytes=vmem)


def _dot(a, b):
    return jnp.dot(a.astype(BF16), b.astype(BF16), preferred_element_type=F32)


def _dot_nt(a, b):
    return lax.dot_general(a.astype(BF16), b.astype(BF16), (((1,), (1,)), ((), ())),
                           preferred_element_type=F32)


def _split3(x):
    hi = x.astype(BF16)
    r1 = x - hi.astype(F32)
    mid = r1.astype(BF16)
    lo = (r1 - mid.astype(F32)).astype(BF16)
    return hi, mid, lo


def _dot_exact_rhs(a_bf16, x):
    hi, mid, lo = _split3(x)
    f = lambda p: jnp.dot(a_bf16, p, preferred_element_type=F32)
    return f(hi) + f(mid) + f(lo)


def _dot_exact_lhs(x, b_bf16):
    hi, mid, lo = _split3(x)
    f = lambda p: jnp.dot(p, b_bf16, preferred_element_type=F32)
    return f(hi) + f(mid) + f(lo)


def _silu(x):
    return x * jax.nn.sigmoid(x)


def _log_sigmoid(x):
    return jnp.minimum(x, 0.0) - jnp.log1p(jnp.exp(-jnp.abs(x)))


def _rms(x, g):
    return x * lax.rsqrt(jnp.mean(x * x, axis=-1, keepdims=True) + EPS) * g


def _head_mean_sq(x, bd_ref):
    n = x.shape[-1]
    return _dot_exact_lhs(x * x, bd_ref[:n, :n])


def _rope(x, cos, sin_signed):
    n = x.shape[-1]
    lane = lax.broadcasted_iota(jnp.int32, x.shape, 1)
    first_half = (lane % DH) < (DH // 2)
    partner = jnp.where(first_half, pltpu.roll(x, n - DH // 2, 1), pltpu.roll(x, DH // 2, 1))
    return x * cos[:, :n] + partner * sin_signed[:, :n]


def _adaln_kernel(c_ref, w_ref, b_ref, o_ref):
    s = _silu(c_ref[...])
    o_ref[...] = _dot(s, w_ref[...]) + b_ref[...]


def _adaln(cond8, w, b):
    n = w.shape[1]
    tn = 1536
    return pl.pallas_call(
        _adaln_kernel,
        out_shape=jax.ShapeDtypeStruct((8, n), F32),
        grid=(n // tn,),
        in_specs=[pl.BlockSpec((8, D_MODEL), lambda j: (0, 0)),
                  pl.BlockSpec((D_MODEL, tn), lambda j: (0, j)),
                  pl.BlockSpec((1, tn), lambda j: (0, j))],
        out_specs=pl.BlockSpec((8, tn), lambda j: (0, j)),
        compiler_params=_params(("arbitrary",)),
        name="adaln",
    )(cond8, w, b.reshape(1, n))


def _const_spec(shape):
    nd = len(shape)
    return pl.BlockSpec(shape, lambda i, _n=nd: (0,) * _n, pipeline_mode=pl.Buffered(1))


def _mod_spec(chunk, rowfn):
    return pl.BlockSpec((1, 1, D_MODEL), lambda i: (rowfn(i) * ADA_CHUNKS + chunk, 0, 0))


def _rowfn(latent):
    if latent:
        return lambda i: 1 + (i * TOKEN_TILE) // LAT_LEN
    return lambda i: 0


def _tok_spec(width):
    return pl.BlockSpec((TOKEN_TILE, width), lambda i: (i, 0))


def _seq_len(latent):
    return LAT_LEN if latent else CTX_LEN


def _transposed_out(t, latent, rows):
    seq = _seq_len(latent)
    shape = jax.ShapeDtypeStruct((t // seq, rows, seq), F32)
    if seq <= TOKEN_TILE:
        per = TOKEN_TILE // seq
        return shape, pl.BlockSpec((per, rows, seq), lambda i: (i, 0, 0))
    per = seq // TOKEN_TILE
    return shape, pl.BlockSpec((1, rows, TOKEN_TILE), lambda i: (i // per, 0, i % per))


def _store_transposed(ref, x):
    n, _, w = ref.shape
    for j in range(n):
        ref[j] = x[j * w:(j + 1) * w].T


def _store_proj_transposed(ref, w_rows, h):
    n, _, w = ref.shape
    for j in range(n):
        ref[j] = _dot_nt(w_rows, h[j * w:(j + 1) * w])


def _modnorm(x, ng_ref, sh_ref, sc_ref):
    return _rms(x, ng_ref[...]) * (1.0 + sc_ref[0]) + sh_ref[0]


def _ffn_kernel(*refs, has_mix, final_norm):
    it = iter(refs)
    x_ref = next(it)
    if has_mix:
        ya_ref, yb_ref, woa_ref, wob_ref, g2_ref = (next(it) for _ in range(5))
    ng_ref, sh_ref, sc_ref, g_ref, win_ref, wout_ref = (next(it) for _ in range(6))
    if final_norm:
        nf_ref = next(it)
    o_ref = next(it)

    x = x_ref[...]
    if has_mix:
        mo = _dot(ya_ref[...], woa_ref[...]) + _dot(yb_ref[...], wob_ref[...])
        x = x + g2_ref[0] * mo
    h = _modnorm(x, ng_ref, sh_ref, sc_ref).astype(BF16)
    acc = None
    for lo, hi in FFN_CHUNKS:
        g = _dot(h, win_ref[:, lo:hi])
        u = _dot(h, win_ref[:, D_FF + lo:D_FF + hi])
        part = _dot(_silu(g) * u, wout_ref[lo:hi, :])
        acc = part if acc is None else acc + part
    y = x + (0.5 * g_ref[0]) * acc
    if final_norm:
        y = _rms(y, nf_ref[...])
    o_ref[...] = y


def _ffn(x, mods, mod_base, latent, ng, w_in, w_out, mix=None, final_g=None):
    t = x.shape[0]
    rowfn = _rowfn(latent)
    args, specs = [x], [_tok_spec(D_MODEL)]
    if mix is not None:
        ya, yb, woa, wob = mix
        args += [ya, yb, woa, wob, mods]
        specs += [_tok_spec(ya.shape[1]), _tok_spec(yb.shape[1]), _const_spec(woa.shape), _const_spec(wob.shape),
                  _mod_spec(mod_base - 1, rowfn)]
    args += [ng.reshape(1, D_MODEL), mods, mods, mods, w_in, w_out]
    specs += [_const_spec((1, D_MODEL)), _mod_spec(mod_base, rowfn), _mod_spec(mod_base + 1, rowfn),
              _mod_spec(mod_base + 2, rowfn), _const_spec(w_in.shape), _const_spec(w_out.shape)]
    if final_g is not None:
        args.append(final_g.reshape(1, D_MODEL))
        specs.append(_const_spec((1, D_MODEL)))
    return pl.pallas_call(
        functools.partial(_ffn_kernel, has_mix=mix is not None, final_norm=final_g is not None),
        out_shape=jax.ShapeDtypeStruct((t, D_MODEL), F32),
        grid=(t // TOKEN_TILE,),
        in_specs=specs,
        out_specs=_tok_spec(D_MODEL),
        compiler_params=_params(("arbitrary",)),
        name="ffn",
    )(*args)


_EV_QA, _EV_KA, _EV_VA, _EV_QB, _EV_KB, _EV_VB, _EV_OB, _EV_GI, _EV_GF, _EV_END = (
    0, 512, 640, 768, 1280, 1792, 2304, 2816, 2944, 3072)
N_DIRHEAD = 2 * N_HEADS
_WT_VA = 2 * N_DIRHEAD + 512


def _proj_even_kernel(*refs, rope):
    it = iter(refs)
    x_ref, ng_ref, sh_ref, sc_ref, w_ref, wt_ref, gbi_ref, gbf_ref, gbt_ref, qg_ref, kg_ref, bd_ref = (
        next(it) for _ in range(12))
    if rope:
        cos_ref, sin_ref = next(it), next(it)
    qa_ref, kat_ref, vat_ref, qb_ref, kb_ref, vb_ref, ob_ref, gi_ref, gf_ref, gt_ref, kbt_ref = (next(it) for _ in range(11))

    h = _modnorm(x_ref[...], ng_ref, sh_ref, sc_ref).astype(BF16)
    proj = lambda lo, hi: _dot(h, w_ref[:, lo:hi])

    qa = proj(_EV_QA, _EV_KA)
    qa = qa * lax.rsqrt(_head_mean_sq(qa, bd_ref) + EPS) * qg_ref[...]
    ka = proj(_EV_KA, _EV_VA)
    ka = ka * lax.rsqrt(_head_mean_sq(ka, bd_ref) + EPS) * kg_ref[...]
    if rope:
        qa = _rope(qa, cos_ref[...], sin_ref[...])
        ka = _rope(ka, cos_ref[...], sin_ref[...])
    qa_ref[...] = qa * ATTN_SCALE
    _store_transposed(kat_ref, ka)
    _store_proj_transposed(vat_ref, wt_ref[_WT_VA:, :], h)
    qb_ref[...] = proj(_EV_QB, _EV_KB)
    kb_ref[...] = proj(_EV_KB, _EV_VB) * ATTN_SCALE
    vb_ref[...] = proj(_EV_VB, _EV_OB)
    ob_ref[...] = proj(_EV_OB, _EV_GI)
    gi_ref[...] = proj(_EV_GI, _EV_GF) + gbi_ref[...]
    gf_ref[...] = proj(_EV_GF, _EV_END) + gbf_ref[...]
    for j in range(TOKEN_TILE // MLSTM_T):
        ht = h[j * MLSTM_T:(j + 1) * MLSTM_T]
        gt_ref[j] = _dot_nt(wt_ref[:2 * N_DIRHEAD, :], ht) + gbt_ref[...]
        kbt_ref[j] = _dot_nt(wt_ref[2 * N_DIRHEAD:_WT_VA, :], ht) * ATTN_SCALE


def _proj_even(x, mods, latent, ng, w, wt, gbi, gbf, gbt, qg, kg, bd, rope_tabs):
    t = x.shape[0]
    rowfn = _rowfn(latent)
    args = [x, ng.reshape(1, D_MODEL), mods, mods, w, wt, gbi, gbf, gbt, qg, kg, bd]
    specs = [_tok_spec(D_MODEL), _const_spec((1, D_MODEL)), _mod_spec(3, rowfn), _mod_spec(4, rowfn)]
    specs += [_const_spec(a.shape) for a in args[4:]]
    if latent:
        nblk = LAT_LEN // TOKEN_TILE
        args += list(rope_tabs)
        specs += [pl.BlockSpec((TOKEN_TILE, 512), lambda i: (i % nblk, 0))] * 2
    widths = (512, 512, 512, 512, 512, 128, 128)
    out_shape = [jax.ShapeDtypeStruct((t, wd), F32) for wd in widths]
    out_specs = [_tok_spec(wd) for wd in widths]
    for pos in (1, 2):
        shape, spec = _transposed_out(t, latent, 128)
        out_shape.insert(pos, shape)
        out_specs.insert(pos, spec)
    nj = TOKEN_TILE // MLSTM_T
    for rows in (2 * N_DIRHEAD, 512):
        out_shape.append(jax.ShapeDtypeStruct((t // MLSTM_T, rows, MLSTM_T), F32))
        out_specs.append(pl.BlockSpec((nj, rows, MLSTM_T), lambda i: (i, 0, 0)))
    return pl.pallas_call(
        functools.partial(_proj_even_kernel, rope=latent),
        out_shape=out_shape, grid=(t // TOKEN_TILE,), in_specs=specs, out_specs=out_specs,
        compiler_params=_params(("arbitrary",)), name="proj_even",
    )(*args)


def _proj_odd_kernel(*refs, rope):
    it = iter(refs)
    x_ref, ng_ref, sh_ref, sc_ref, w_ref = (next(it) for _ in range(5))
    wt_ref = next(it)
    if rope:
        cos_ref, sin_ref = next(it), next(it)
    qc_ref, kc_ref, vc_ref, qd_ref, kd_ref, vd_ref = (next(it) for _ in range(6))

    h = _modnorm(x_ref[...], ng_ref, sh_ref, sc_ref).astype(BF16)
    proj = lambda lo, hi: _dot(h, w_ref[:, lo:hi])
    qc_ref[...] = proj(0, 512) * ATTN_SCALE
    kc_ref[...] = proj(512, 1024)
    vc_ref[...] = proj(1024, 1536)
    qd = proj(1536, 2048)
    if rope:
        qd = _rope(qd, cos_ref[...], sin_ref[...])
        _store_transposed(kd_ref, _rope(proj(2048, 2176), cos_ref[...], sin_ref[...]))
    else:
        _store_proj_transposed(kd_ref, wt_ref[:128, :], h)
    _store_proj_transposed(vd_ref, wt_ref[128:, :], h)
    qd_ref[...] = qd * ATTN_SCALE


def _proj_odd(x, mods, latent, ng, w, wt, rope_tabs):
    t = x.shape[0]
    rowfn = _rowfn(latent)
    args = [x, ng.reshape(1, D_MODEL), mods, mods, w, wt]
    specs = [_tok_spec(D_MODEL), _const_spec((1, D_MODEL)), _mod_spec(3, rowfn), _mod_spec(4, rowfn),
             _const_spec(w.shape), _const_spec(wt.shape)]
    if latent:
        nblk = LAT_LEN // TOKEN_TILE
        args += list(rope_tabs)
        specs += [pl.BlockSpec((TOKEN_TILE, 512), lambda i: (i % nblk, 0))] * 2
    widths = (512, 512, 512, 512)
    out_shape = [jax.ShapeDtypeStruct((t, wd), F32) for wd in widths]
    out_specs = [_tok_spec(wd) for wd in widths]
    for _ in range(2):
        shape, spec = _transposed_out(t, latent, 128)
        out_shape.append(shape)
        out_specs.append(spec)
    return pl.pallas_call(
        functools.partial(_proj_odd_kernel, rope=latent),
        out_shape=out_shape,
        grid=(t // TOKEN_TILE,), in_specs=specs, out_specs=out_specs,
        compiler_params=_params(("arbitrary",)), name="proj_odd",
    )(*args)


def _attn_kernel(*refs, group, kv_t, has_extra, has_sink):
    it = iter(refs)
    q_ref, k_ref, v_ref = next(it), next(it), next(it)
    if has_extra:
        ke_ref, ve_ref = next(it), next(it)
    if has_sink:
        sink_ref = next(it)
    o_ref = next(it)
    tq = q_ref.shape[0]
    first = lax.broadcasted_iota(jnp.int32, (tq, 128), 1) < DH

    def kv_operands(kt_ref, vt_ref, kv):
        kt = kt_ref[0, kv * DH:(kv + 1) * DH, :].astype(BF16)
        vt = vt_ref[0, kv * DH:(kv + 1) * DH, :].astype(BF16)
        z = jnp.zeros_like(kt)
        ones = jnp.ones((128, kt.shape[1]), BF16)
        return ((jnp.concatenate([kt, z], axis=0), jnp.concatenate([z, kt], axis=0)),
                (jnp.concatenate([vt, z, ones], axis=0), jnp.concatenate([z, vt, ones], axis=0)))

    cache = {}
    for j in range(N_HEADS // 2):
        ps = slice(128 * j, 128 * (j + 1))
        q_pair = q_ref[:, ps]
        if kv_t:
            kv = 2 * j // group
            if kv not in cache:
                cache[kv] = (kv_operands(k_ref, v_ref, kv), kv_operands(ke_ref, ve_ref, kv) if has_extra else None)
            (k_ops, v_ops), extra = cache[kv]
            qb = q_pair.astype(BF16)
        else:
            k_pair = k_ref[:, ps].astype(BF16)
            v_aug = jnp.concatenate([v_ref[:, ps].astype(BF16), jnp.ones((k_pair.shape[0], 128), BF16)], axis=1)
        res, sink_den = [], []
        for half in range(2):
            if kv_t:
                s = jnp.dot(qb, k_ops[half], preferred_element_type=F32)
            else:
                s = _dot_nt(jnp.where(first, q_pair, 0.0) if half == 0 else jnp.where(first, 0.0, q_pair), k_pair)
            m = jnp.max(s, axis=-1, keepdims=True)
            if has_extra:
                se = jnp.dot(qb, extra[0][half], preferred_element_type=F32)
                m = jnp.maximum(m, jnp.max(se, axis=-1, keepdims=True))
            if has_sink:
                m = jnp.maximum(m, sink_ref[2 * j + half])
                sink_den.append(jnp.exp(sink_ref[2 * j + half] - m))
            p = jnp.exp(s - m)
            r = _dot_nt(p, v_ops[half]) if kv_t else _dot(p, v_aug)
            if has_extra:
                r = r + _dot_nt(jnp.exp(se - m), extra[1][half])
            res.append(r)
        num = jnp.where(first, res[0][:, :128], res[1][:, :128])
        den = jnp.where(first, res[0][:, 128:], res[1][:, 128:])
        if has_sink:
            den = den + jnp.where(first, sink_den[0], sink_den[1])
        o_ref[:, ps] = num / den


def _attention(q, k, v, *, n_batch, q_len, q_tile, group, kv_t, extra=None, sink=None):
    nq = q_len // q_tile
    args = [q, k, v]
    specs = [pl.BlockSpec((q_tile, 512), lambda b, j: (b * nq + j, 0))]
    if kv_t:
        kv_spec = lambda a: pl.BlockSpec((1,) + a.shape[1:], lambda b, j: (b, 0, 0))
        specs += [kv_spec(k), kv_spec(v)]
        if extra is not None:
            args += list(extra)
            specs += [kv_spec(extra[0]), kv_spec(extra[1])]
    else:
        assert extra is None
        keys = k.shape[0] // n_batch
        specs += [pl.BlockSpec((keys, 512), lambda b, j: (b, 0))] * 2
    if sink is not None:
        args.append(sink)
        specs.append(pl.BlockSpec(memory_space=pltpu.SMEM))
    return pl.pallas_call(
        functools.partial(_attn_kernel, group=group, kv_t=kv_t, has_extra=extra is not None, has_sink=sink is not None),
        out_shape=jax.ShapeDtypeStruct(q.shape, F32),
        grid=(n_batch, nq), in_specs=specs,
        out_specs=pl.BlockSpec((q_tile, 512), lambda b, j: (b * nq + j, 0)),
        compiler_params=_params(("arbitrary", "arbitrary")), name="attn",
    )(*args)


LOG2E = 1.4426950408889634


def _cummax_time(x, reverse):
    n = x.shape[0]
    row = lax.broadcasted_iota(jnp.int32, x.shape, 0)
    k = 1
    while k < n:
        if reverse:
            shifted = jnp.where(row < n - k, pltpu.roll(x, n - k, 0), -jnp.inf)
        else:
            shifted = jnp.where(row >= k, pltpu.roll(x, k, 0), -jnp.inf)
        x = jnp.maximum(x, shifted)
        k *= 2
    return x


def _mlstm_kernel(*refs, n_chunks, carry):
    it = iter(refs)
    q_ref, k_ref, kt_ref, v_ref, ob_ref, gi_ref, gf_ref, gt_ref, hg_ref, bd_ref = (next(it) for _ in range(10))
    if carry:
        c0_ref, n0_ref, m0r_ref, m0c_ref = (next(it) for _ in range(4))
    y_ref = next(it)
    if not carry:
        c_ref, n_ref, m_ref = next(it), next(it), next(it)
    hf_s, hb_s = next(it), next(it)
    if carry:
        st_s, mrow_s, mcol_s = next(it), next(it), next(it)
    T = MLSTM_T
    step = pl.program_id(1)

    if carry:
        @pl.when(step == 0)
        def _():
            z = jnp.zeros((DH, DH), F32)
            for d in range(2):
                for j in range(N_HEADS // 2):
                    ca, cb = c0_ref[0, d, 2 * j], c0_ref[0, d, 2 * j + 1]
                    na, nb = n0_ref[0, d, 2 * j][:, :DH], n0_ref[0, d, 2 * j + 1][:, :DH]
                    st_s[d, j, 0:DH, :] = jnp.concatenate([ca, z, na, z], axis=1)
                    st_s[d, j, DH:2 * DH, :] = jnp.concatenate([z, cb, z, nb], axis=1)
            mrow_s[...] = m0r_ref[0]
            mcol_s[...] = m0c_ref[0]

    row = lax.broadcasted_iota(jnp.int32, (T, T), 0)
    col = lax.broadcasted_iota(jnp.int32, (T, T), 1)
    lower = row >= col
    upper = row <= col
    lower_m = jnp.where(lower, 1.0, 0.0).astype(BF16)
    upper_m = jnp.where(upper, 1.0, 0.0).astype(BF16)
    lane = lax.broadcasted_iota(jnp.int32, (T, 128), 1)
    first = lane < DH
    lane_c = lax.broadcasted_iota(jnp.int32, (DH, 256), 1) % 128
    zeros_kt = jnp.zeros((DH, T), BF16)
    ones_v = jnp.ones((T, 128), BF16)

    for d in range(2):
        chunk = step if d == 0 else n_chunks - 1 - step
        rows = pl.ds(pl.multiple_of(chunk * T, T), T)
        hd0 = N_HEADS * d
        mask = lower if d == 0 else upper

        gi, gf = gi_ref[rows, :], gf_ref[rows, :]
        b_cols = _dot_exact_rhs(lower_m if d == 0 else upper_m, _log_sigmoid(gf))
        a_cols = gi - b_cols
        dmax = b_cols + _cummax_time(a_cols, reverse=d == 1)
        if carry:
            inter = b_cols + mrow_s[0:1, :]
            mt = jnp.maximum(inter, dmax)
            w_inter = jnp.exp(inter - mt)
        else:
            mt = jnp.maximum(b_cols, dmax)
        c2 = (b_cols - mt) * LOG2E
        einv = jnp.exp(-mt)

        gt = gt_ref[chunk]
        gi_t = gt[hd0:hd0 + N_HEADS, :]
        gf_t = gt[N_DIRHEAD + hd0:N_DIRHEAD + hd0 + N_HEADS, :]
        b_rows = _dot_exact_lhs(_log_sigmoid(gf_t), upper_m if d == 0 else lower_m)
        a_rows = gi_t - b_rows
        a2 = a_rows * LOG2E
        b_last = b_rows[:, T - 1:T] if d == 0 else b_rows[:, 0:1]
        dec_rows = b_last + a_rows
        m_new = jnp.max(dec_rows, axis=-1, keepdims=True)
        if carry:
            m_prev = mcol_s[hd0:hd0 + N_HEADS, 0:1]
            m_new = jnp.maximum(m_new, b_last + m_prev)
            wc = jnp.exp(b_last + m_prev - m_new)
        else:
            m_new = jnp.maximum(m_new, b_last)
        ws_rows = jnp.exp(dec_rows - m_new)

        h_s = hf_s if d == 0 else hb_s
        for j in range(N_HEADS // 2):
            ps = slice(128 * j, 128 * (j + 1))
            q_pair = q_ref[rows, ps].astype(BF16)
            v_aug = jnp.concatenate([v_ref[rows, ps].astype(BF16), ones_v], axis=1)
            kt = (kt_ref[chunk, 128 * j:128 * j + DH, :], kt_ref[chunk, 128 * j + DH:128 * (j + 1), :])
            res, cu = [], []
            for half in range(2):
                hd = 2 * j + half
                kt0 = jnp.concatenate([kt[0].astype(BF16), zeros_kt] if half == 0 else [zeros_kt, kt[1].astype(BF16)], axis=0)
                qk = jnp.dot(q_pair, kt0, preferred_element_type=F32)
                e = c2[:, hd0 + hd:hd0 + hd + 1] + a2[hd:hd + 1, :]
                w = jnp.exp2(jnp.where(mask, e, -jnp.inf)) * qk
                res.append(_dot(w, v_aug))
                cu.append(_dot(kt[half] * ws_rows[hd:hd + 1, :], v_aug))
            ra, rb = hd0 + 2 * j, hd0 + 2 * j + 1
            num = jnp.where(first, res[0][:, :128], res[1][:, :128])
            den = jnp.where(first, res[0][:, 128:], res[1][:, 128:])
            if carry:
                old = st_s[d, j]
                qs = _dot(q_pair, old)
                wi = jnp.where(first, w_inter[:, ra:ra + 1], w_inter[:, rb:rb + 1])
                num = num + wi * qs[:, :128]
                den = den + wi * qs[:, 128:]
            floor = jnp.where(first, einv[:, ra:ra + 1], einv[:, rb:rb + 1])
            h_s[rows, ps] = num / jnp.maximum(jnp.abs(den), floor)
            if carry:
                st_s[d, j, 0:DH, :] = jnp.where(lane_c < DH, cu[0], 0.0) + wc[2 * j:2 * j + 1, :] * old[0:DH]
                st_s[d, j, DH:2 * DH, :] = jnp.where(lane_c >= DH, cu[1], 0.0) + wc[2 * j + 1:2 * j + 2, :] * old[DH:2 * DH]
            else:
                c_ref[0, d, 2 * j] = cu[0][:, 0:DH]
                c_ref[0, d, 2 * j + 1] = cu[1][:, DH:2 * DH]

        if carry:
            b_last_l = b_cols[T - 1:T, :] if d == 0 else b_cols[0:1, :]
            m_prev_l = mrow_s[0:1, :]
            m_new_l = jnp.maximum(jnp.max(b_last_l + a_cols, axis=0, keepdims=True), b_last_l + m_prev_l)
            lane_r = lax.broadcasted_iota(jnp.int32, (1, 128), 1)
            mine = (lane_r >= hd0) & (lane_r < hd0 + N_HEADS)
            mrow_s[0:1, :] = jnp.where(mine, m_new_l, m_prev_l)
            mcol_s[hd0:hd0 + N_HEADS, :] = jnp.broadcast_to(m_new, (N_HEADS, 128))
        else:
            n_all = _dot(ws_rows, k_ref[rows, :])
            for hd in range(N_HEADS):
                n_ref[0, d, hd:hd + 1, :] = n_all[hd:hd + 1, hd * DH:(hd + 1) * DH]
            m_ref[0, hd0:hd0 + N_HEADS, :] = jnp.broadcast_to(m_new, (N_HEADS, 128))

    @pl.when(step == n_chunks - 1)
    def _():
        hsum = hf_s[...] + hb_s[...]
        y = hsum * lax.rsqrt(_head_mean_sq(hsum, bd_ref) + EPS) * hg_ref[...]
        y_ref[...] = jax.nn.sigmoid(ob_ref[...]) * y


def _mlstm(qb, kb, kbt, vb, ob, gi, gf, gt, head_g, bd, *, n_batch, seq_len, state=None):
    n_chunks = seq_len // MLSTM_T
    carry = state is not None
    assert carry or n_chunks == 1
    seq = lambda wd: pl.BlockSpec((seq_len, wd), lambda b, c: (b, 0))
    per_chunk = lambda r: pl.BlockSpec((n_chunks, r, MLSTM_T), lambda b, c: (b, 0, 0))
    args = [qb, kb, kbt, vb, ob, gi, gf, gt, head_g, bd]
    specs = [seq(512), seq(512), per_chunk(512), seq(512), seq(512), seq(128), seq(128), per_chunk(2 * N_DIRHEAD),
             pl.BlockSpec((1, 512), lambda b, c: (0, 0)), pl.BlockSpec(bd.shape, lambda b, c: (0, 0))]
    out_shape = [jax.ShapeDtypeStruct((n_batch * seq_len, 512), F32)]
    out_specs = [seq(512)]
    scratch = [pltpu.VMEM((seq_len, 512), F32), pltpu.VMEM((seq_len, 512), F32)]
    if carry:
        args += list(state)
        specs += [pl.BlockSpec((1, 2, N_HEADS, DH, DH), lambda b, c: (b, 0, 0, 0, 0)),
                  pl.BlockSpec((1, 2, N_HEADS, DH, 128), lambda b, c: (b, 0, 0, 0, 0)),
                  pl.BlockSpec((1, 8, 128), lambda b, c: (b, 0, 0)),
                  pl.BlockSpec((1, N_DIRHEAD, 128), lambda b, c: (b, 0, 0))]
        scratch += [pltpu.VMEM((2, N_HEADS // 2, 128, 256), F32), pltpu.VMEM((8, 128), F32),
                    pltpu.VMEM((N_DIRHEAD, 128), F32)]
    else:
        out_shape += [jax.ShapeDtypeStruct((n_batch, 2, N_HEADS, DH, DH), F32),
                      jax.ShapeDtypeStruct((n_batch, 2, N_HEADS, DH), F32),
                      jax.ShapeDtypeStruct((n_batch, N_DIRHEAD, 128), F32)]
        out_specs += [pl.BlockSpec((1, 2, N_HEADS, DH, DH), lambda b, c: (b, 0, 0, 0, 0)),
                      pl.BlockSpec((1, 2, N_HEADS, DH), lambda b, c: (b, 0, 0, 0)),
                      pl.BlockSpec((1, N_DIRHEAD, 128), lambda b, c: (b, 0, 0))]
    return pl.pallas_call(
        functools.partial(_mlstm_kernel, n_chunks=n_chunks, carry=carry),
        out_shape=out_shape, grid=(n_batch, n_chunks), in_specs=specs, out_specs=out_specs,
        scratch_shapes=scratch,
        compiler_params=_params(("arbitrary", "arbitrary")), name="mlstm",
    )(*args)


N_ROWS = LAT_LEN // GRID_W
N_DY = 2 * NA_KH - 1
N_DX = 2 * NA_KW - 1


NA_QROWS = 4
NA_KROWS = 12
NA_GROUPS = N_ROWS // NA_QROWS


def _na_kernel(q_ref, k_ref, v_ref, ke_ref, ve_ref, rpb_ref, o_ref, tile_s, slab_s):
    pair = pl.program_id(1)
    qn, kn = NA_QROWS * GRID_W, NA_KROWS * GRID_W

    @pl.when(pl.program_id(0) == 0)
    def _():
        qc = lax.broadcasted_iota(jnp.int32, (GRID_W, GRID_W), 0)
        kc = lax.broadcasted_iota(jnp.int32, (GRID_W, GRID_W), 1)
        start = jnp.clip(qc - NA_KW // 2, 0, GRID_W - NA_KW)
        in_win = (kc >= start) & (kc < start + NA_KW)
        dx = kc - qc + (NA_KW - 1)
        blocked = jnp.full((GRID_W, GRID_W), NEG_INF, F32)
        for half in range(2):
            head = 2 * pair + half

            def build_tile(dy, carry):
                tile = blocked
                for j in range(N_DX):
                    tile = jnp.where(dx == j, rpb_ref[(head * N_DY + dy) * N_DX + j], tile)
                tile_s[half * N_DY + dy] = jnp.where(in_win, tile, NEG_INF)
                return carry

            lax.fori_loop(0, N_DY, build_tile, 0)
            for kind in range(3):
                idx = (pair * 2 + half) * 3 + kind
                for i in range(NA_QROWS):
                    for k in range(NA_KROWS):
                        if kind == 0:
                            ok, dy = k < NA_KH, k - i + NA_KH - 1
                        elif kind == 1:
                            ok, dy = i <= k < i + NA_KH, k - i + NA_KH // 2 - 1
                        else:
                            ok, dy = k >= NA_KROWS - NA_KH, k - i - 1
                        slab_s[idx, i * GRID_W:(i + 1) * GRID_W, k * GRID_W:(k + 1) * GRID_W] = (
                            tile_s[half * N_DY + dy] if ok else blocked)

    first = lax.broadcasted_iota(jnp.int32, (qn, 128), 1) < DH
    ones_k = jnp.ones((kn, 128), BF16)
    ke = ke_ref[...].astype(BF16)
    ve_aug = jnp.concatenate([ve_ref[...].astype(BF16), jnp.ones((ke.shape[0], 128), BF16)], axis=1)

    def group_body(g, carry):
        k0 = jnp.where(g < NA_GROUPS // 2, 0, N_ROWS - NA_KROWS)
        kind = jnp.where(g == 0, 0, jnp.where(g == NA_GROUPS - 1, 2, 1))
        qrows = pl.ds(pl.multiple_of(g * qn, qn), qn)
        keys = pl.ds(pl.multiple_of(k0 * GRID_W, GRID_W), kn)
        q_pair = q_ref[qrows, :]
        k_pair = k_ref[keys, :].astype(BF16)
        v_aug = jnp.concatenate([v_ref[keys, :].astype(BF16), ones_k], axis=1)
        res = []
        for half in range(2):
            qm = jnp.where(first, q_pair, 0.0) if half == 0 else jnp.where(first, 0.0, q_pair)
            s = _dot_nt(qm, k_pair) + slab_s[(pair * 2 + half) * 3 + kind]
            se = _dot_nt(qm, ke)
            m = jnp.maximum(jnp.max(s, axis=-1, keepdims=True), jnp.max(se, axis=-1, keepdims=True))
            res.append(_dot(jnp.exp(s - m), v_aug) + _dot(jnp.exp(se - m), ve_aug))
        num = jnp.where(first, res[0][:, :128], res[1][:, :128])
        den = jnp.where(first, res[0][:, 128:], res[1][:, 128:])
        o_ref[qrows, :] = num / den
        return carry

    lax.fori_loop(0, NA_GROUPS, group_body, 0)


def _na_latent(q, k, v, k_ctx, v_ctx, rpb_flat, n_batch):
    seq = pl.BlockSpec((LAT_LEN, 128), lambda b, j: (b, j))
    ctx = pl.BlockSpec((CTX_LEN, 128), lambda b, j: (b, j))
    return pl.pallas_call(
        _na_kernel,
        out_shape=jax.ShapeDtypeStruct(q.shape, F32),
        grid=(n_batch, N_HEADS // 2),
        in_specs=[seq, seq, seq, ctx, ctx, pl.BlockSpec(memory_space=pltpu.SMEM)],
        out_specs=seq,
        scratch_shapes=[pltpu.VMEM((2 * N_DY, GRID_W, GRID_W), F32),
                        pltpu.VMEM((N_HEADS * 3, NA_QROWS * GRID_W, NA_KROWS * GRID_W), F32)],
        compiler_params=_params(("arbitrary", "arbitrary")), name="na_latent",
    )(q, k, v, k_ctx, v_ctx, rpb_flat)


SWA_QT = 256
SWA_SPAN = SWA_QT + 2 * SWA_WIN


def _swa_kernel(q_ref, kt_ref, vt_ref, ket_ref, vet_ref, sink_ref, o_ref):
    pair = pl.program_id(1)
    first = lax.broadcasted_iota(jnp.int32, (SWA_QT, 128), 1) < DH
    row = lax.broadcasted_iota(jnp.int32, (SWA_QT, SWA_SPAN), 0)
    col = lax.broadcasted_iota(jnp.int32, (SWA_QT, SWA_SPAN), 1)
    z = jnp.zeros((DH, SWA_SPAN), BF16)
    ones = jnp.ones((128, SWA_SPAN), BF16)
    ket, vet = ket_ref[0].astype(BF16), vet_ref[0].astype(BF16)
    ze = jnp.zeros_like(ket)
    ones_e = jnp.ones((128, ket.shape[1]), BF16)
    sinks = (sink_ref[2 * pair], sink_ref[2 * pair + 1])
    for g in range(LAT_LEN // SWA_QT):
        q0 = g * SWA_QT
        lo = min(max(q0 - SWA_WIN, 0), LAT_LEN - SWA_SPAN)
        near = jnp.abs((lo + col) - (q0 + row)) <= SWA_WIN
        qb = q_ref[q0:q0 + SWA_QT, :].astype(BF16)
        kt = kt_ref[0, :, lo:lo + SWA_SPAN].astype(BF16)
        vt = vt_ref[0, :, lo:lo + SWA_SPAN].astype(BF16)
        res, sink_den = [], []
        for half in range(2):
            k_op = jnp.concatenate([kt, z] if half == 0 else [z, kt], axis=0)
            v_op = jnp.concatenate([vt, z, ones] if half == 0 else [z, vt, ones], axis=0)
            ke_op = jnp.concatenate([ket, ze] if half == 0 else [ze, ket], axis=0)
            ve_op = jnp.concatenate([vet, ze, ones_e] if half == 0 else [ze, vet, ones_e], axis=0)
            s = jnp.where(near, jnp.dot(qb, k_op, preferred_element_type=F32), NEG_INF)
            se = jnp.dot(qb, ke_op, preferred_element_type=F32)
            m = jnp.maximum(jnp.max(s, axis=-1, keepdims=True), jnp.max(se, axis=-1, keepdims=True))
            m = jnp.maximum(m, sinks[half])
            sink_den.append(jnp.exp(sinks[half] - m))
            res.append(_dot_nt(jnp.exp(s - m), v_op) + _dot_nt(jnp.exp(se - m), ve_op))
        num = jnp.where(first, res[0][:, :128], res[1][:, :128])
        den = jnp.where(first, res[0][:, 128:], res[1][:, 128:]) + jnp.where(first, sink_den[0], sink_den[1])
        o_ref[q0:q0 + SWA_QT, :] = num / den


def _swa_latent(q, kt, vt, kt_ctx, vt_ctx, sink, n_batch, group):
    ppk = group // 2
    kv_spec = lambda a: pl.BlockSpec((1, DH, a.shape[2]), lambda b, j: (b, j // ppk, 0))
    return pl.pallas_call(
        _swa_kernel,
        out_shape=jax.ShapeDtypeStruct(q.shape, F32),
        grid=(n_batch, N_HEADS // 2),
        in_specs=[pl.BlockSpec((LAT_LEN, 128), lambda b, j: (b, j)),
                  kv_spec(kt), kv_spec(vt), kv_spec(kt_ctx), kv_spec(vt_ctx),
                  pl.BlockSpec(memory_space=pltpu.SMEM)],
        out_specs=pl.BlockSpec((LAT_LEN, 128), lambda b, j: (b, j)),
        compiler_params=_params(("arbitrary", "arbitrary")), name="swa_latent",
    )(q, kt, vt, kt_ctx, vt_ctx, sink)


def _rope_tables():
    t = jnp.arange(LAT_LEN)
    pos = jnp.stack([t // GRID_W, t % GRID_W], axis=-1).astype(F32)
    freqs = ROPE_THETA ** (-jnp.arange(ROPE_FREQS, dtype=F32) / ROPE_FREQS)
    ang = (pos[:, :, None] * freqs).reshape(LAT_LEN, 2 * ROPE_FREQS)
    cos, sin = jnp.cos(ang), jnp.sin(ang)
    cos_t = jnp.tile(jnp.concatenate([cos, cos], axis=-1), (1, N_HEADS))
    sin_t = jnp.tile(jnp.concatenate([-sin, sin], axis=-1), (1, N_HEADS))
    return cos_t, sin_t


def kernel(x_prompt, x_sample, cache_l0_attn_k, cache_l0_attn_v, state_l0_mlstm_C, state_l0_mlstm_n, state_l0_mlstm_m, cache_l1_na_k, cache_l1_na_v, cache_l1_swa_k, cache_l1_swa_v, c, c_ctx, norm_final, ada_w_l0, ada_b_l0, norm_l0, ffn1_in_l0, ffn1_out_l0, ffn2_in_l0, ffn2_out_l0, mix_in_l0, mix_out_l0, qk_norm_l0, gate_bias_l0, head_norm_l0, ada_w_l1, ada_b_l1, norm_l1, ffn1_in_l1, ffn1_out_l1, ffn2_in_l1, ffn2_out_l1, mix_in_l1, mix_out_l1, rpb_l1, sink_l1):
    nb, nl = x_prompt.shape[0], x_sample.shape[0]
    bf = lambda a: a.astype(BF16)

    cond8 = jnp.zeros((8, D_MODEL), F32).at[0].set(c_ctx).at[1:1 + nl].set(c)
    mods0 = _adaln(cond8, ada_w_l0, ada_b_l0).reshape(8 * ADA_CHUNKS, 1, D_MODEL)
    mods1 = _adaln(cond8, ada_w_l1, ada_b_l1).reshape(8 * ADA_CHUNKS, 1, D_MODEL)

    g0 = 2304
    gcols = lambda j: mix_in_l0[:, g0 + 8 * j:g0 + 8 * (j + 1)]
    gpad = jnp.zeros((D_MODEL, 128 - N_DIRHEAD), F32)
    w_gi, w_gf = jnp.concatenate([gcols(0), gcols(2)], axis=1), jnp.concatenate([gcols(1), gcols(3)], axis=1)
    w_even = bf(jnp.concatenate([mix_in_l0[:, :g0], mix_in_l0[:, g0 + 32:], w_gi, gpad, w_gf, gpad], axis=1))
    w_even_t = bf(jnp.concatenate([w_gi, w_gf, mix_in_l0[:, 1280:1792], mix_in_l0[:, 640:768]], axis=1).T)
    gb4 = gate_bias_l0.reshape(4, N_HEADS)
    b_gi, b_gf = jnp.concatenate([gb4[0], gb4[2]]), jnp.concatenate([gb4[1], gb4[3]])
    gbi = jnp.zeros((1, 128), F32).at[0, :N_DIRHEAD].set(b_gi)
    gbf = jnp.zeros((1, 128), F32).at[0, :N_DIRHEAD].set(b_gf)
    gbt = jnp.concatenate([b_gi, b_gf]).reshape(2 * N_DIRHEAD, 1)
    qg = jnp.tile(qk_norm_l0[0], N_HEADS).reshape(1, 512)
    kg = jnp.tile(qk_norm_l0[1], 2).reshape(1, 128)
    head_g = head_norm_l0.reshape(1, 512)
    grp = np.arange(512) // DH
    bd = jnp.asarray((grp[:, None] == grp[None, :]).astype(np.float32) / DH, dtype=BF16)
    rope_tabs = _rope_tables()
    w_odd = bf(mix_in_l1)
    w_odd_t = bf(mix_in_l1[:, 2048:2304].T)
    wo0, wo1 = bf(mix_out_l0), bf(mix_out_l1)
    rpb_flat = rpb_l1.reshape(-1)
    to_t = lambda a: a.reshape(a.shape[0], a.shape[1], -1).transpose(0, 2, 1)
    from_t = lambda a: a.reshape(a.shape[0], -1, DH, a.shape[2]).transpose(0, 3, 1, 2)

    xp = x_prompt.reshape(nb * CTX_LEN, D_MODEL)
    xs = x_sample.reshape(nl * LAT_LEN, D_MODEL)

    f1i, f1o, f2i, f2o = bf(ffn1_in_l0), bf(ffn1_out_l0), bf(ffn2_in_l0), bf(ffn2_out_l0)
    outs = {}
    for latent, x in ((False, xp), (True, xs)):
        n_batch, seq_len = (nl, LAT_LEN) if latent else (nb, CTX_LEN)
        x = _ffn(x, mods0, 0, latent, norm_l0[0], f1i, f1o)
        qa, kat, vat, qb, kb, vb, ob, gi, gf, gt, kbt = _proj_even(
            x, mods0, latent, norm_l0[1], w_even, w_even_t, gbi, gbf, gbt, qg, kg, bd, rope_tabs)
        if latent:
            extra = (to_t(cache_l0_attn_k), to_t(cache_l0_attn_v))
            m0 = state_l0_mlstm_m.reshape(nl, N_DIRHEAD)
            m0_lanes = jnp.zeros((nl, 8, 128), F32).at[:, 0, :N_DIRHEAD].set(m0)
            m0_rows = jnp.broadcast_to(m0[:, :, None], (nl, N_DIRHEAD, 128))
            n0_cols = jnp.broadcast_to(state_l0_mlstm_n[..., None], (nl, 2, N_HEADS, DH, 128))
            state = (state_l0_mlstm_C, n0_cols, m0_lanes, m0_rows)
        else:
            extra, state = None, None
        ya = _attention(qa, kat, vat, n_batch=n_batch, q_len=seq_len, q_tile=256, group=4, kv_t=True, extra=extra)
        ml = _mlstm(qb, kb, kbt, vb, ob, gi, gf, gt, head_g, bd, n_batch=n_batch, seq_len=seq_len, state=state)
        yb = ml[0]
        if not latent:
            c_fin, n_fin, m_fin = ml[1:]
        x = _ffn(x, mods0, 6, latent, norm_l0[2], f2i, f2o, mix=(ya, yb, wo0[:512], wo0[512:]))
        outs[latent] = (x, kat, vat)
    xp, k0t, v0t = outs[False]
    c0, n0, m0_pad = c_fin, n_fin, m_fin
    xs = outs[True][0]

    f1i, f1o, f2i, f2o = bf(ffn1_in_l1), bf(ffn1_out_l1), bf(ffn2_in_l1), bf(ffn2_out_l1)
    for latent, x in ((False, xp), (True, xs)):
        n_batch, seq_len = (nl, LAT_LEN) if latent else (nb, CTX_LEN)
        x = _ffn(x, mods1, 0, latent, norm_l1[0], f1i, f1o)
        qc, kc, vc, qd, kd, vd = _proj_odd(x, mods1, latent, norm_l1[1], w_odd, w_odd_t, rope_tabs)
        if latent:
            yc = _na_latent(qc, kc, vc, cache_l1_na_k.reshape(nl * CTX_LEN, 512),
                            cache_l1_na_v.reshape(nl * CTX_LEN, 512), rpb_flat, nl)
            yd = _swa_latent(qd, kd, vd, to_t(cache_l1_swa_k), to_t(cache_l1_swa_v), sink_l1, nl, group=4)
        else:
            yc = _attention(qc, kc, vc, n_batch=n_batch, q_len=seq_len, q_tile=256, group=1, kv_t=False)
            yd = _attention(qd, kd, vd, n_batch=n_batch, q_len=seq_len, q_tile=256, group=4, kv_t=True, sink=sink_l1)
        x = _ffn(x, mods1, 6, latent, norm_l1[2], f2i, f2o, mix=(yc, yd, wo1[:512], wo1[512:]), final_g=norm_final)
        outs[latent] = (x, kc, vc, kd, vd)
    y_prompt, kc1, vc1, kd1t, vd1t = outs[False]
    y_sample = outs[True][0]

    return (y_prompt.reshape(nb, CTX_LEN, D_MODEL), y_sample.reshape(nl, LAT_LEN, D_MODEL),
            from_t(k0t), from_t(v0t),
            c0, n0, m0_pad[:, :, 0].reshape(nb, 2, N_HEADS),
            kc1.reshape(nb, CTX_LEN, N_HEADS, DH), vc1.reshape(nb, CTX_LEN, N_HEADS, DH),
            from_t(kd1t), from_t(vd1t))
```

```python
import functools

import jax
import jax.numpy as jnp
import numpy as np
from jax import lax
from jax.experimental import pallas as pl
from jax.experimental.pallas import tpu as pltpu

F32 = jnp.float32
BF16 = jnp.bfloat16

D_MODEL = 1024
DH = 64
D_FF = 2816
ADA_CHUNKS = 9
GRID_W = 64
LAT_LEN = 1024
CTX_LEN = 256
N_HEADS = 8
NA_KH = 8
NA_KW = 16
SWA_WIN = 128
ROPE_THETA = 10000.0
ROPE_FREQS = DH // 4
ATTN_SCALE = DH ** -0.5
NEG_INF = -1e30
EPS = 1e-6

TOKEN_TILE = 512
MLSTM_T = 256
FFN_CHUNK = 512
VMEM_LIMIT = 56 * 1024 * 1024
FFN_VMEM_LIMIT = 60 * 1024 * 1024
W_TILE = (256, 512)
W_SLOTS = 6


def _params(sem, vmem=VMEM_LIMIT):
    return pltpu.CompilerParams(dimension_semantics=sem, vmem_limit_bytes=vmem)


def _dot(a, b):
    return jnp.dot(a.astype(BF16), b.astype(BF16), preferred_element_type=F32)


def _dot_nt(a, b):
    return lax.dot_general(a.astype(BF16), b.astype(BF16), (((1,), (1,)), ((), ())),
                           preferred_element_type=F32)


def _split3(x):
    hi = x.astype(BF16)
    r1 = x - hi.astype(F32)
    mid = r1.astype(BF16)
    lo = (r1 - mid.astype(F32)).astype(BF16)
    return hi, mid, lo


def _dot_exact_rhs(a_bf16, x):
    hi, mid, lo = _split3(x)
    f = lambda p: jnp.dot(a_bf16, p, preferred_element_type=F32)
    return f(hi) + f(mid) + f(lo)


def _dot_exact_lhs(x, b_bf16):
    hi, mid, lo = _split3(x)
    f = lambda p: jnp.dot(p, b_bf16, preferred_element_type=F32)
    return f(hi) + f(mid) + f(lo)


def _silu(x):
    return x * jax.nn.sigmoid(x)


def _log_sigmoid(x):
    return jnp.minimum(x, 0.0) - jnp.log1p(jnp.exp(-jnp.abs(x)))


def _rms(x, g):
    return x * lax.rsqrt(jnp.mean(x * x, axis=-1, keepdims=True) + EPS) * g


def _head_mean_sq(x, bd_ref):
    n = x.shape[-1]
    sq = x * x
    hi = sq.astype(BF16)
    lo = (sq - hi.astype(F32)).astype(BF16)
    w = min(n, bd_ref.shape[0])
    bd = bd_ref[:w, :w]
    f = lambda p: jnp.dot(p, bd, preferred_element_type=F32)
    parts = [f(hi[:, c:c + w]) + f(lo[:, c:c + w]) for c in range(0, n, w)]
    return parts[0] if len(parts) == 1 else jnp.concatenate(parts, axis=1)


def _rope(x, cos, sin_signed):
    n = x.shape[-1]
    lane = lax.broadcasted_iota(jnp.int32, x.shape, 1)
    first_half = (lane % DH) < (DH // 2)
    partner = jnp.where(first_half, pltpu.roll(x, n - DH // 2, 1), pltpu.roll(x, DH // 2, 1))
    return x * cos[:, :n] + partner * sin_signed[:, :n]


def _adaln_kernel(c_ref, w_ref, b_ref, o_ref):
    s = _silu(c_ref[...])
    o_ref[...] = _dot(s, w_ref[...]) + b_ref[...]


def _adaln(cond8, w, b):
    n = w.shape[1]
    tn = 1536
    return pl.pallas_call(
        _adaln_kernel,
        out_shape=jax.ShapeDtypeStruct((8, n), F32),
        grid=(n // tn,),
        in_specs=[pl.BlockSpec((8, D_MODEL), lambda j: (0, 0)),
                  pl.BlockSpec((D_MODEL, tn), lambda j: (0, j)),
                  pl.BlockSpec((1, tn), lambda j: (0, j))],
        out_specs=pl.BlockSpec((8, tn), lambda j: (0, j)),
        compiler_params=_params(("arbitrary",)),
        name="adaln",
    )(cond8, w, b.reshape(1, n))


def _const_spec(shape):
    nd = len(shape)
    return pl.BlockSpec(shape, lambda i, _n=nd: (0,) * _n, pipeline_mode=pl.Buffered(1))


def _mod_spec(chunk, rowfn):
    return pl.BlockSpec((1, 1, D_MODEL), lambda i: (rowfn(i) * ADA_CHUNKS + chunk, 0, 0))


def _rowfn(latent):
    if latent:
        return lambda i: 1 + (i * TOKEN_TILE) // LAT_LEN
    return lambda i: 0


def _tok_spec(width, blk0=0):
    return pl.BlockSpec((TOKEN_TILE, width), lambda i: (i + blk0, 0))


def _seq_len(latent):
    return LAT_LEN if latent else CTX_LEN


def _transposed_out(t, latent, rows):
    seq = _seq_len(latent)
    shape = jax.ShapeDtypeStruct((t // seq, rows, seq), F32)
    if seq <= TOKEN_TILE:
        per = TOKEN_TILE // seq
        return shape, pl.BlockSpec((per, rows, seq), lambda i: (i, 0, 0))
    per = seq // TOKEN_TILE
    return shape, pl.BlockSpec((1, rows, TOKEN_TILE), lambda i: (i // per, 0, i % per))


def _store_transposed(ref, x):
    n, _, w = ref.shape
    for j in range(n):
        ref[j] = x[j * w:(j + 1) * w].T


def _store_proj_transposed(ref, w_rows, h):
    n, _, w = ref.shape
    for j in range(n):
        ref[j] = _dot_nt(w_rows, h[j * w:(j + 1) * w])


def _modnorm(x, ng_ref, sh_ref, sc_ref):
    return _rms(x, ng_ref[...]) * (1.0 + sc_ref[0]) + sh_ref[0]


def _load_weights_bf16(pairs, stage, sem):
    tr, tc = W_TILE
    tiles = [(src, dst, r, c) for src, dst in pairs
             for r in range(0, src.shape[0], tr) for c in range(0, src.shape[1], tc)]

    def copy(k):
        src, _, r, c = tiles[k]
        return pltpu.make_async_copy(src.at[r:r + tr, c:c + tc], stage.at[k % W_SLOTS], sem.at[k % W_SLOTS])

    for k in range(min(W_SLOTS, len(tiles))):
        copy(k).start()
    for k, (_, dst, r, c) in enumerate(tiles):
        copy(k).wait()
        dst[r:r + tr, c:c + tc] = stage[k % W_SLOTS].astype(BF16)
        if k + W_SLOTS < len(tiles):
            copy(k + W_SLOTS).start()


def _ffn_kernel(*refs, n_ctx, first, has_mix, last):
    it = iter(refs)
    xs = (next(it), next(it)) if first else (next(it),)
    if has_mix:
        yac_ref, ybc_ref, yal_ref, ybl_ref, wo_hbm, g2_ref = (next(it) for _ in range(6))
    ng_ref, sh_ref, sc_ref, g_ref, win_hbm, wout_hbm = (next(it) for _ in range(6))
    if last:
        nf_ref = next(it)
    outs = (next(it), next(it)) if last else (next(it),)
    win_s, wout_s = next(it), next(it)
    if has_mix:
        wo_s = next(it)
    stage, sem = next(it), next(it)

    step = pl.program_id(0)
    is_lat = step >= n_ctx

    @pl.when(step == 0)
    def _():
        pairs = [(win_hbm, win_s), (wout_hbm, wout_s)]
        if has_mix:
            pairs.insert(0, (wo_hbm, wo_s))
        _load_weights_bf16(pairs, stage, sem)

    pick = lambda c_ref, l_ref: jnp.where(is_lat, l_ref[...], c_ref[...])
    x = pick(*xs) if first else xs[0][...]
    if has_mix:
        half = wo_s.shape[0] // 2
        mo = _dot(pick(yac_ref, yal_ref), wo_s[:half, :]) + _dot(pick(ybc_ref, ybl_ref), wo_s[half:, :])
        x = x + g2_ref[0] * mo
    h = _modnorm(x, ng_ref, sh_ref, sc_ref).astype(BF16)
    acc = None
    for lo in range(0, D_FF, FFN_CHUNK):
        hi = min(lo + FFN_CHUNK, D_FF)
        g = _dot(h, win_s[:, lo:hi])
        u = _dot(h, win_s[:, D_FF + lo:D_FF + hi])
        part = _dot(_silu(g) * u, wout_s[lo:hi, :])
        acc = part if acc is None else acc + part
    y = x + (0.5 * g_ref[0]) * acc
    if last:
        y = _rms(y, nf_ref[...])
        oc_ref, ol_ref = outs

        @pl.when(jnp.logical_not(is_lat))
        def _():
            oc_ref[...] = y
            ol_ref[...] = jnp.zeros_like(ol_ref)

        @pl.when(is_lat)
        def _():
            ol_ref[...] = y
    else:
        outs[0][...] = y


def _ffn(x, mods, mod_base, ng, w_in, w_out, *, t_ctx, t_lat, mix=None, final_g=None):
    n_ctx, n_lat = t_ctx // TOKEN_TILE, t_lat // TOKEN_TILE
    first, last = isinstance(x, tuple), final_g is not None
    ctx_map = lambda i: (jnp.minimum(i, n_ctx - 1), 0)
    lat_map = lambda i: (jnp.maximum(i - n_ctx, 0), 0)
    rowfn = lambda i: jnp.where(i < n_ctx, 0, 1 + ((i - n_ctx) * TOKEN_TILE) // LAT_LEN)
    two = lambda wd: [pl.BlockSpec((TOKEN_TILE, wd), ctx_map), pl.BlockSpec((TOKEN_TILE, wd), lat_map)]
    hbm = pl.BlockSpec(memory_space=pl.ANY)
    args = list(x) if first else [x]
    specs = two(D_MODEL) if first else [_tok_spec(D_MODEL)]
    scratch = [pltpu.VMEM(w_in.shape, BF16), pltpu.VMEM(w_out.shape, BF16)]
    if mix is not None:
        yac, ybc, yal, ybl, wo = mix
        args += [yac, ybc, yal, ybl, wo, mods]
        mw = yac.shape[1]
        specs += [pl.BlockSpec((TOKEN_TILE, mw), ctx_map)] * 2 + [pl.BlockSpec((TOKEN_TILE, mw), lat_map)] * 2
        specs += [hbm, _mod_spec(mod_base - 1, rowfn)]
        scratch.append(pltpu.VMEM(wo.shape, BF16))
    args += [ng.reshape(1, D_MODEL), mods, mods, mods, w_in, w_out]
    specs += [_const_spec((1, D_MODEL)), _mod_spec(mod_base, rowfn), _mod_spec(mod_base + 1, rowfn),
              _mod_spec(mod_base + 2, rowfn), hbm, hbm]
    if last:
        args.append(final_g.reshape(1, D_MODEL))
        specs.append(_const_spec((1, D_MODEL)))
        out_shape = [jax.ShapeDtypeStruct((t_ctx, D_MODEL), F32), jax.ShapeDtypeStruct((t_lat, D_MODEL), F32)]
        out_specs = two(D_MODEL)
    else:
        out_shape = jax.ShapeDtypeStruct((t_ctx + t_lat, D_MODEL), F32)
        out_specs = _tok_spec(D_MODEL)
    scratch += [pltpu.VMEM((W_SLOTS,) + W_TILE, F32), pltpu.SemaphoreType.DMA((W_SLOTS,))]
    return pl.pallas_call(
        functools.partial(_ffn_kernel, n_ctx=n_ctx, first=first, has_mix=mix is not None, last=last),
        out_shape=out_shape,
        grid=(n_ctx + n_lat,),
        in_specs=specs,
        out_specs=out_specs,
        scratch_shapes=scratch,
        compiler_params=_params(("arbitrary",), FFN_VMEM_LIMIT),
        name="ffn",
    )(*args)


_EV_QA, _EV_KA, _EV_VA, _EV_QB, _EV_KB, _EV_VB, _EV_OB, _EV_GI, _EV_GF, _EV_END = (
    0, 512, 640, 768, 1280, 1792, 2304, 2816, 2944, 3072)
N_DIRHEAD = 2 * N_HEADS
_WT_VA = 2 * N_DIRHEAD + 512


def _proj_even_kernel(*refs, rope):
    it = iter(refs)
    x_ref, ng_ref, sh_ref, sc_ref, w_ref, wt_ref, gbi_ref, gbf_ref, gbt_ref, qg_ref, kg_ref, bd_ref = (
        next(it) for _ in range(12))
    if rope:
        cos_ref, sin_ref = next(it), next(it)
    qa_ref, kat_ref, vat_ref, qb_ref, kb_ref, vb_ref, ob_ref, gi_ref, gf_ref, gt_ref, kbt_ref = (next(it) for _ in range(11))

    h = _modnorm(x_ref[...], ng_ref, sh_ref, sc_ref).astype(BF16)
    proj = lambda lo, hi: _dot(h, w_ref[:, lo:hi])

    qa = proj(_EV_QA, _EV_KA)
    qa = qa * lax.rsqrt(_head_mean_sq(qa, bd_ref) + EPS) * qg_ref[...]
    ka = proj(_EV_KA, _EV_VA)
    ka = ka * lax.rsqrt(_head_mean_sq(ka, bd_ref) + EPS) * kg_ref[...]
    if rope:
        qa = _rope(qa, cos_ref[...], sin_ref[...])
        ka = _rope(ka, cos_ref[...], sin_ref[...])
    qa_ref[...] = qa * ATTN_SCALE
    _store_transposed(kat_ref, ka)
    _store_proj_transposed(vat_ref, wt_ref[_WT_VA:, :], h)
    qb_ref[...] = proj(_EV_QB, _EV_KB)
    kb_ref[...] = proj(_EV_KB, _EV_VB) * ATTN_SCALE
    vb_ref[...] = proj(_EV_VB, _EV_OB)
    ob_ref[...] = proj(_EV_OB, _EV_GI)
    gi_ref[...] = proj(_EV_GI, _EV_GF) + gbi_ref[...]
    gf_ref[...] = proj(_EV_GF, _EV_END) + gbf_ref[...]
    for j in range(TOKEN_TILE // MLSTM_T):
        ht = h[j * MLSTM_T:(j + 1) * MLSTM_T]
        gt_ref[j] = _dot_nt(wt_ref[:2 * N_DIRHEAD, :], ht) + gbt_ref[...]
        kbt_ref[j] = _dot_nt(wt_ref[2 * N_DIRHEAD:_WT_VA, :], ht) * ATTN_SCALE


def _proj_even(x, t, blk0, mods, latent, ng, w, wt, gbi, gbf, gbt, qg, kg, bd, rope_tabs):
    rowfn = _rowfn(latent)
    args = [x, ng.reshape(1, D_MODEL), mods, mods, w, wt, gbi, gbf, gbt, qg, kg, bd]
    specs = [_tok_spec(D_MODEL, blk0), _const_spec((1, D_MODEL)), _mod_spec(3, rowfn), _mod_spec(4, rowfn)]
    specs += [_const_spec(a.shape) for a in args[4:]]
    if latent:
        nblk = LAT_LEN // TOKEN_TILE
        args += list(rope_tabs)
        specs += [pl.BlockSpec((TOKEN_TILE, 512), lambda i: (i % nblk, 0))] * 2
    widths = (512, 512, 512, 512, 512, 128, 128)
    out_shape = [jax.ShapeDtypeStruct((t, wd), F32) for wd in widths]
    out_specs = [_tok_spec(wd) for wd in widths]
    for pos in (1, 2):
        shape, spec = _transposed_out(t, latent, 128)
        out_shape.insert(pos, shape)
        out_specs.insert(pos, spec)
    nj = TOKEN_TILE // MLSTM_T
    for rows in (2 * N_DIRHEAD, 512):
        out_shape.append(jax.ShapeDtypeStruct((t // MLSTM_T, rows, MLSTM_T), F32))
        out_specs.append(pl.BlockSpec((nj, rows, MLSTM_T), lambda i: (i, 0, 0)))
    return pl.pallas_call(
        functools.partial(_proj_even_kernel, rope=latent),
        out_shape=out_shape, grid=(t // TOKEN_TILE,), in_specs=specs, out_specs=out_specs,
        compiler_params=_params(("arbitrary",)), name="proj_even",
    )(*args)


def _proj_odd_kernel(*refs, rope):
    it = iter(refs)
    x_ref, ng_ref, sh_ref, sc_ref, w_ref = (next(it) for _ in range(5))
    wt_ref = next(it)
    if rope:
        cos_ref, sin_ref = next(it), next(it)
    qc_ref, kc_ref, vc_ref, qd_ref, kd_ref, vd_ref = (next(it) for _ in range(6))

    h = _modnorm(x_ref[...], ng_ref, sh_ref, sc_ref).astype(BF16)
    proj = lambda lo, hi: _dot(h, w_ref[:, lo:hi])
    qc_ref[...] = proj(0, 512) * ATTN_SCALE
    kc_ref[...] = proj(512, 1024)
    vc_ref[...] = proj(1024, 1536)
    qd = proj(1536, 2048)
    if rope:
        qd = _rope(qd, cos_ref[...], sin_ref[...])
        _store_transposed(kd_ref, _rope(proj(2048, 2176), cos_ref[...], sin_ref[...]))
    else:
        _store_proj_transposed(kd_ref, wt_ref[:128, :], h)
    _store_proj_transposed(vd_ref, wt_ref[128:, :], h)
    qd_ref[...] = qd * ATTN_SCALE


def _proj_odd(x, t, blk0, mods, latent, ng, w, wt, rope_tabs):
    rowfn = _rowfn(latent)
    args = [x, ng.reshape(1, D_MODEL), mods, mods, w, wt]
    specs = [_tok_spec(D_MODEL, blk0), _const_spec((1, D_MODEL)), _mod_spec(3, rowfn), _mod_spec(4, rowfn),
             _const_spec(w.shape), _const_spec(wt.shape)]
    if latent:
        nblk = LAT_LEN // TOKEN_TILE
        args += list(rope_tabs)
        specs += [pl.BlockSpec((TOKEN_TILE, 512), lambda i: (i % nblk, 0))] * 2
    widths = (512, 512, 512, 512)
    out_shape = [jax.ShapeDtypeStruct((t, wd), F32) for wd in widths]
    out_specs = [_tok_spec(wd) for wd in widths]
    for _ in range(2):
        shape, spec = _transposed_out(t, latent, 128)
        out_shape.append(shape)
        out_specs.append(spec)
    return pl.pallas_call(
        functools.partial(_proj_odd_kernel, rope=latent),
        out_shape=out_shape,
        grid=(t // TOKEN_TILE,), in_specs=specs, out_specs=out_specs,
        compiler_params=_params(("arbitrary",)), name="proj_odd",
    )(*args)


def _attn_kernel(*refs, group, kv_t, has_extra, has_sink):
    it = iter(refs)
    q_ref, k_ref, v_ref = next(it), next(it), next(it)
    if has_extra:
        ke_ref, ve_ref = next(it), next(it)
    if has_sink:
        sink_ref = next(it)
    o_ref = next(it)
    tq = q_ref.shape[0]
    first = lax.broadcasted_iota(jnp.int32, (tq, 128), 1) < DH

    def kv_operands(kt_ref, vt_ref, kv):
        kt = kt_ref[0, kv * DH:(kv + 1) * DH, :].astype(BF16)
        vt = vt_ref[0, kv * DH:(kv + 1) * DH, :].astype(BF16)
        z = jnp.zeros_like(kt)
        ones = jnp.ones((128, kt.shape[1]), BF16)
        return ((jnp.concatenate([kt, z], axis=0), jnp.concatenate([z, kt], axis=0)),
                (jnp.concatenate([vt, z, ones], axis=0), jnp.concatenate([z, vt, ones], axis=0)))

    cache = {}
    for j in range(N_HEADS // 2):
        ps = slice(128 * j, 128 * (j + 1))
        q_pair = q_ref[:, ps]
        if kv_t:
            kv = 2 * j // group
            if kv not in cache:
                cache[kv] = (kv_operands(k_ref, v_ref, kv), kv_operands(ke_ref, ve_ref, kv) if has_extra else None)
            (k_ops, v_ops), extra = cache[kv]
            qb = q_pair.astype(BF16)
        else:
            k_pair = k_ref[:, ps].astype(BF16)
            v_aug = jnp.concatenate([v_ref[:, ps].astype(BF16), jnp.ones((k_pair.shape[0], 128), BF16)], axis=1)
        res, sink_den = [], []
        for half in range(2):
            if kv_t:
                s = jnp.dot(qb, k_ops[half], preferred_element_type=F32)
            else:
                s = _dot_nt(jnp.where(first, q_pair, 0.0) if half == 0 else jnp.where(first, 0.0, q_pair), k_pair)
            m = jnp.max(s, axis=-1, keepdims=True)
            if has_extra:
                se = jnp.dot(qb, extra[0][half], preferred_element_type=F32)
                m = jnp.maximum(m, jnp.max(se, axis=-1, keepdims=True))
            if has_sink:
                m = jnp.maximum(m, sink_ref[2 * j + half])
                sink_den.append(jnp.exp(sink_ref[2 * j + half] - m))
            p = jnp.exp(s - m)
            r = _dot_nt(p, v_ops[half]) if kv_t else _dot(p, v_aug)
            if has_extra:
                r = r + _dot_nt(jnp.exp(se - m), extra[1][half])
            res.append(r)
        num = jnp.where(first, res[0][:, :128], res[1][:, :128])
        den = jnp.where(first, res[0][:, 128:], res[1][:, 128:])
        if has_sink:
            den = den + jnp.where(first, sink_den[0], sink_den[1])
        o_ref[:, ps] = num / den


def _attention(q, k, v, *, n_batch, q_len, q_tile, group, kv_t, extra=None, sink=None):
    nq = q_len // q_tile
    args = [q, k, v]
    specs = [pl.BlockSpec((q_tile, 512), lambda b, j: (b * nq + j, 0))]
    if kv_t:
        kv_spec = lambda a: pl.BlockSpec((1,) + a.shape[1:], lambda b, j: (b, 0, 0))
        specs += [kv_spec(k), kv_spec(v)]
        if extra is not None:
            args += list(extra)
            specs += [kv_spec(extra[0]), kv_spec(extra[1])]
    else:
        assert extra is None
        keys = k.shape[0] // n_batch
        specs += [pl.BlockSpec((keys, 512), lambda b, j: (b, 0))] * 2
    if sink is not None:
        args.append(sink)
        specs.append(pl.BlockSpec(memory_space=pltpu.SMEM))
    return pl.pallas_call(
        functools.partial(_attn_kernel, group=group, kv_t=kv_t, has_extra=extra is not None, has_sink=sink is not None),
        out_shape=jax.ShapeDtypeStruct(q.shape, F32),
        grid=(n_batch, nq), in_specs=specs,
        out_specs=pl.BlockSpec((q_tile, 512), lambda b, j: (b * nq + j, 0)),
        compiler_params=_params(("arbitrary", "arbitrary")), name="attn",
    )(*args)


LOG2E = 1.4426950408889634


def _cummax_time(x, reverse):
    n = x.shape[0]
    row = lax.broadcasted_iota(jnp.int32, x.shape, 0)
    k = 1
    while k < n:
        if reverse:
            shifted = jnp.where(row < n - k, pltpu.roll(x, n - k, 0), -jnp.inf)
        else:
            shifted = jnp.where(row >= k, pltpu.roll(x, k, 0), -jnp.inf)
        x = jnp.maximum(x, shifted)
        k *= 2
    return x


def _mlstm_kernel(*refs, n_chunks, carry):
    it = iter(refs)
    q_ref, k_ref, kt_ref, v_ref, ob_ref, gi_ref, gf_ref, gt_ref, hg_ref, bd_ref = (next(it) for _ in range(10))
    if carry:
        c0_ref, n0_ref, m0r_ref, m0c_ref = (next(it) for _ in range(4))
    y_ref = next(it)
    if not carry:
        c_ref, n_ref, m_ref = next(it), next(it), next(it)
    hf_s, hb_s = next(it), next(it)
    if carry:
        st_s, mrow_s, mcol_s = next(it), next(it), next(it)
    T = MLSTM_T
    step = pl.program_id(1)

    if carry:
        @pl.when(step == 0)
        def _():
            z = jnp.zeros((DH, DH), F32)
            for d in range(2):
                for j in range(N_HEADS // 2):
                    ca, cb = c0_ref[0, d, 2 * j], c0_ref[0, d, 2 * j + 1]
                    na, nb = n0_ref[0, d, 2 * j][:, :DH], n0_ref[0, d, 2 * j + 1][:, :DH]
                    st_s[d, j, 0:DH, :] = jnp.concatenate([ca, z, na, z], axis=1)
                    st_s[d, j, DH:2 * DH, :] = jnp.concatenate([z, cb, z, nb], axis=1)
            mrow_s[...] = m0r_ref[0]
            mcol_s[...] = m0c_ref[0]

    row = lax.broadcasted_iota(jnp.int32, (T, T), 0)
    col = lax.broadcasted_iota(jnp.int32, (T, T), 1)
    lower = row >= col
    upper = row <= col
    lower_m = jnp.where(lower, 1.0, 0.0).astype(BF16)
    upper_m = jnp.where(upper, 1.0, 0.0).astype(BF16)
    lane = lax.broadcasted_iota(jnp.int32, (T, 128), 1)
    first = lane < DH
    lane_c = lax.broadcasted_iota(jnp.int32, (DH, 256), 1) % 128
    zeros_kt = jnp.zeros((DH, T), BF16)
    ones_v = jnp.ones((T, 128), BF16)

    for d in range(2):
        chunk = step if d == 0 else n_chunks - 1 - step
        rows = pl.ds(pl.multiple_of(chunk * T, T), T)
        hd0 = N_HEADS * d
        mask = lower if d == 0 else upper

        gi, gf = gi_ref[rows, :], gf_ref[rows, :]
        lf = _log_sigmoid(gf)
        b_cols = _dot_exact_rhs(lower_m, lf)
        if d == 1:
            b_cols = b_cols[T - 1:T, :] - b_cols + lf
        a_cols = gi - b_cols
        dmax = b_cols + _cummax_time(a_cols, reverse=d == 1)
        if carry:
            inter = b_cols + mrow_s[0:1, :]
            mt = jnp.maximum(inter, dmax)
            w_inter = jnp.exp(inter - mt)
        else:
            mt = jnp.maximum(b_cols, dmax)
        c2 = (b_cols - mt) * LOG2E
        einv = jnp.exp(-mt)

        gt = gt_ref[chunk]
        gi_t = gt[hd0:hd0 + N_HEADS, :]
        gf_t = gt[N_DIRHEAD + hd0:N_DIRHEAD + hd0 + N_HEADS, :]
        lf_t = _log_sigmoid(gf_t)
        b_rows = _dot_exact_lhs(lf_t, upper_m)
        if d == 1:
            b_rows = b_rows[:, T - 1:T] - b_rows + lf_t
        a_rows = gi_t - b_rows
        a2 = a_rows * LOG2E
        b_last = b_rows[:, T - 1:T] if d == 0 else b_rows[:, 0:1]
        dec_rows = b_last + a_rows
        m_new = jnp.max(dec_rows, axis=-1, keepdims=True)
        if carry:
            m_prev = mcol_s[hd0:hd0 + N_HEADS, 0:1]
            m_new = jnp.maximum(m_new, b_last + m_prev)
            wc = jnp.exp(b_last + m_prev - m_new)
        else:
            m_new = jnp.maximum(m_new, b_last)
        ws_rows = jnp.exp(dec_rows - m_new)

        h_s = hf_s if d == 0 else hb_s
        for j in range(N_HEADS // 2):
            ps = slice(128 * j, 128 * (j + 1))
            q_pair = q_ref[rows, ps].astype(BF16)
            v_aug = jnp.concatenate([v_ref[rows, ps].astype(BF16), ones_v], axis=1)
            kt = (kt_ref[chunk, 128 * j:128 * j + DH, :], kt_ref[chunk, 128 * j + DH:128 * (j + 1), :])
            res, cu = [], []
            for half in range(2):
                hd = 2 * j + half
                kt0 = jnp.concatenate([kt[0].astype(BF16), zeros_kt] if half == 0 else [zeros_kt, kt[1].astype(BF16)], axis=0)
                qk = jnp.dot(q_pair, kt0, preferred_element_type=F32)
                e = c2[:, hd0 + hd:hd0 + hd + 1] + a2[hd:hd + 1, :]
                w = jnp.exp2(jnp.where(mask, e, -jnp.inf)) * qk
                res.append(_dot(w, v_aug))
                cu.append(_dot(kt[half] * ws_rows[hd:hd + 1, :], v_aug))
            ra, rb = hd0 + 2 * j, hd0 + 2 * j + 1
            num = jnp.where(first, res[0][:, :128], res[1][:, :128])
            den = jnp.where(first, res[0][:, 128:], res[1][:, 128:])
            if carry:
                old = st_s[d, j]
                qs = _dot(q_pair, old)
                wi = jnp.where(first, w_inter[:, ra:ra + 1], w_inter[:, rb:rb + 1])
                num = num + wi * qs[:, :128]
                den = den + wi * qs[:, 128:]
            floor = jnp.where(first, einv[:, ra:ra + 1], einv[:, rb:rb + 1])
            h_s[rows, ps] = num / jnp.maximum(jnp.abs(den), floor)
            if carry:
                st_s[d, j, 0:DH, :] = jnp.where(lane_c < DH, cu[0], 0.0) + wc[2 * j:2 * j + 1, :] * old[0:DH]
                st_s[d, j, DH:2 * DH, :] = jnp.where(lane_c >= DH, cu[1], 0.0) + wc[2 * j + 1:2 * j + 2, :] * old[DH:2 * DH]
            else:
                c_ref[0, d, 2 * j] = cu[0][:, 0:DH]
                c_ref[0, d, 2 * j + 1] = cu[1][:, DH:2 * DH]

        if carry:
            b_last_l = b_cols[T - 1:T, :] if d == 0 else b_cols[0:1, :]
            m_prev_l = mrow_s[0:1, :]
            m_new_l = jnp.maximum(jnp.max(b_last_l + a_cols, axis=0, keepdims=True), b_last_l + m_prev_l)
            lane_r = lax.broadcasted_iota(jnp.int32, (1, 128), 1)
            mine = (lane_r >= hd0) & (lane_r < hd0 + N_HEADS)
            mrow_s[0:1, :] = jnp.where(mine, m_new_l, m_prev_l)
            mcol_s[hd0:hd0 + N_HEADS, :] = jnp.broadcast_to(m_new, (N_HEADS, 128))
        else:
            n_all = _dot(ws_rows, k_ref[rows, :])
            for hd in range(N_HEADS):
                n_ref[0, d, hd:hd + 1, :] = n_all[hd:hd + 1, hd * DH:(hd + 1) * DH]
            m_ref[0, hd0:hd0 + N_HEADS, :] = jnp.broadcast_to(m_new, (N_HEADS, 128))

    @pl.when(step == n_chunks - 1)
    def _():
        hsum = hf_s[...] + hb_s[...]
        y = hsum * lax.rsqrt(_head_mean_sq(hsum, bd_ref) + EPS) * hg_ref[...]
        y_ref[...] = jax.nn.sigmoid(ob_ref[...]) * y


def _mlstm(qb, kb, kbt, vb, ob, gi, gf, gt, head_g, bd, *, n_batch, seq_len, state=None):
    n_chunks = seq_len // MLSTM_T
    carry = state is not None
    assert carry or n_chunks == 1
    seq = lambda wd: pl.BlockSpec((seq_len, wd), lambda b, c: (b, 0))
    per_chunk = lambda r: pl.BlockSpec((n_chunks, r, MLSTM_T), lambda b, c: (b, 0, 0))
    args = [qb, kb, kbt, vb, ob, gi, gf, gt, head_g, bd]
    specs = [seq(512), seq(512), per_chunk(512), seq(512), seq(512), seq(128), seq(128), per_chunk(2 * N_DIRHEAD),
             pl.BlockSpec((1, 512), lambda b, c: (0, 0)), pl.BlockSpec(bd.shape, lambda b, c: (0, 0))]
    out_shape = [jax.ShapeDtypeStruct((n_batch * seq_len, 512), F32)]
    out_specs = [seq(512)]
    scratch = [pltpu.VMEM((seq_len, 512), F32), pltpu.VMEM((seq_len, 512), F32)]
    if carry:
        args += list(state)
        specs += [pl.BlockSpec((1, 2, N_HEADS, DH, DH), lambda b, c: (b, 0, 0, 0, 0)),
                  pl.BlockSpec((1, 2, N_HEADS, DH, 128), lambda b, c: (b, 0, 0, 0, 0)),
                  pl.BlockSpec((1, 8, 128), lambda b, c: (b, 0, 0)),
                  pl.BlockSpec((1, N_DIRHEAD, 128), lambda b, c: (b, 0, 0))]
        scratch += [pltpu.VMEM((2, N_HEADS // 2, 128, 256), F32), pltpu.VMEM((8, 128), F32),
                    pltpu.VMEM((N_DIRHEAD, 128), F32)]
    else:
        out_shape += [jax.ShapeDtypeStruct((n_batch, 2, N_HEADS, DH, DH), F32),
                      jax.ShapeDtypeStruct((n_batch, 2, N_HEADS, DH), F32),
                      jax.ShapeDtypeStruct((n_batch, N_DIRHEAD, 128), F32)]
        out_specs += [pl.BlockSpec((1, 2, N_HEADS, DH, DH), lambda b, c: (b, 0, 0, 0, 0)),
                      pl.BlockSpec((1, 2, N_HEADS, DH), lambda b, c: (b, 0, 0, 0)),
                      pl.BlockSpec((1, N_DIRHEAD, 128), lambda b, c: (b, 0, 0))]
    return pl.pallas_call(
        functools.partial(_mlstm_kernel, n_chunks=n_chunks, carry=carry),
        out_shape=out_shape, grid=(n_batch, n_chunks), in_specs=specs, out_specs=out_specs,
        scratch_shapes=scratch,
        compiler_params=_params(("arbitrary", "arbitrary")), name="mlstm",
    )(*args)


N_ROWS = LAT_LEN // GRID_W
N_DY = 2 * NA_KH - 1
N_DX = 2 * NA_KW - 1


NA_QROWS = 4
NA_KROWS = 12
NA_GROUPS = N_ROWS // NA_QROWS


def _na_kernel(q_ref, k_ref, v_ref, ke_ref, ve_ref, rpb_ref, o_ref, tile_s, slab_s):
    pair = pl.program_id(1)
    qn, kn = NA_QROWS * GRID_W, NA_KROWS * GRID_W

    @pl.when(pl.program_id(0) == 0)
    def _():
        qc = lax.broadcasted_iota(jnp.int32, (GRID_W, GRID_W), 0)
        kc = lax.broadcasted_iota(jnp.int32, (GRID_W, GRID_W), 1)
        start = jnp.clip(qc - NA_KW // 2, 0, GRID_W - NA_KW)
        in_win = (kc >= start) & (kc < start + NA_KW)
        dx = kc - qc + (NA_KW - 1)
        blocked = jnp.full((GRID_W, GRID_W), NEG_INF, F32)
        for half in range(2):
            head = 2 * pair + half

            def build_tile(dy, carry):
                tile = blocked
                for j in range(N_DX):
                    tile = jnp.where(dx == j, rpb_ref[(head * N_DY + dy) * N_DX + j], tile)
                tile_s[half * N_DY + dy] = jnp.where(in_win, tile, NEG_INF)
                return carry

            lax.fori_loop(0, N_DY, build_tile, 0)
            for kind in range(3):
                idx = (pair * 2 + half) * 3 + kind
                for i in range(NA_QROWS):
                    for k in range(NA_KROWS):
                        if kind == 0:
                            ok, dy = k < NA_KH, k - i + NA_KH - 1
                        elif kind == 1:
                            ok, dy = i <= k < i + NA_KH, k - i + NA_KH // 2 - 1
                        else:
                            ok, dy = k >= NA_KROWS - NA_KH, k - i - 1
                        slab_s[idx, i * GRID_W:(i + 1) * GRID_W, k * GRID_W:(k + 1) * GRID_W] = (
                            tile_s[half * N_DY + dy] if ok else blocked)

    first = lax.broadcasted_iota(jnp.int32, (qn, 128), 1) < DH
    ones_k = jnp.ones((kn, 128), BF16)
    ke = ke_ref[...].astype(BF16)
    ve_aug = jnp.concatenate([ve_ref[...].astype(BF16), jnp.ones((ke.shape[0], 128), BF16)], axis=1)

    def group_body(g, carry):
        k0 = jnp.where(g < NA_GROUPS // 2, 0, N_ROWS - NA_KROWS)
        kind = jnp.where(g == 0, 0, jnp.where(g == NA_GROUPS - 1, 2, 1))
        qrows = pl.ds(pl.multiple_of(g * qn, qn), qn)
        keys = pl.ds(pl.multiple_of(k0 * GRID_W, GRID_W), kn)
        q_pair = q_ref[qrows, :]
        k_pair = k_ref[keys, :].astype(BF16)
        v_aug = jnp.concatenate([v_ref[keys, :].astype(BF16), ones_k], axis=1)
        res = []
        for half in range(2):
            qm = jnp.where(first, q_pair, 0.0) if half == 0 else jnp.where(first, 0.0, q_pair)
            s = _dot_nt(qm, k_pair) + slab_s[(pair * 2 + half) * 3 + kind]
            se = _dot_nt(qm, ke)
            m = jnp.maximum(jnp.max(s, axis=-1, keepdims=True), jnp.max(se, axis=-1, keepdims=True))
            res.append(_dot(jnp.exp(s - m), v_aug) + _dot(jnp.exp(se - m), ve_aug))
        num = jnp.where(first, res[0][:, :128], res[1][:, :128])
        den = jnp.where(first, res[0][:, 128:], res[1][:, 128:])
        o_ref[qrows, :] = num / den
        return carry

    lax.fori_loop(0, NA_GROUPS, group_body, 0)


def _na_latent(q, k, v, k_ctx, v_ctx, rpb_flat, n_batch):
    seq = pl.BlockSpec((LAT_LEN, 128), lambda b, j: (b, j))
    ctx = pl.BlockSpec((CTX_LEN, 128), lambda b, j: (b, j))
    return pl.pallas_call(
        _na_kernel,
        out_shape=jax.ShapeDtypeStruct(q.shape, F32),
        grid=(n_batch, N_HEADS // 2),
        in_specs=[seq, seq, seq, ctx, ctx, pl.BlockSpec(memory_space=pltpu.SMEM)],
        out_specs=seq,
        scratch_shapes=[pltpu.VMEM((2 * N_DY, GRID_W, GRID_W), F32),
                        pltpu.VMEM((N_HEADS * 3, NA_QROWS * GRID_W, NA_KROWS * GRID_W), F32)],
        compiler_params=_params(("arbitrary", "arbitrary")), name="na_latent",
    )(q, k, v, k_ctx, v_ctx, rpb_flat)


SWA_QT = 256
SWA_SPAN = SWA_QT + 2 * SWA_WIN


def _swa_kernel(q_ref, kt_ref, vt_ref, ket_ref, vet_ref, sink_ref, o_ref):
    pair = pl.program_id(1)
    first = lax.broadcasted_iota(jnp.int32, (SWA_QT, 128), 1) < DH
    row = lax.broadcasted_iota(jnp.int32, (SWA_QT, SWA_SPAN), 0)
    col = lax.broadcasted_iota(jnp.int32, (SWA_QT, SWA_SPAN), 1)
    z = jnp.zeros((DH, SWA_SPAN), BF16)
    ones = jnp.ones((128, SWA_SPAN), BF16)
    ket, vet = ket_ref[0].astype(BF16), vet_ref[0].astype(BF16)
    ze = jnp.zeros_like(ket)
    ones_e = jnp.ones((128, ket.shape[1]), BF16)
    sinks = (sink_ref[2 * pair], sink_ref[2 * pair + 1])
    for g in range(LAT_LEN // SWA_QT):
        q0 = g * SWA_QT
        lo = min(max(q0 - SWA_WIN, 0), LAT_LEN - SWA_SPAN)
        near = jnp.abs((lo + col) - (q0 + row)) <= SWA_WIN
        qb = q_ref[q0:q0 + SWA_QT, :].astype(BF16)
        kt = kt_ref[0, :, lo:lo + SWA_SPAN].astype(BF16)
        vt = vt_ref[0, :, lo:lo + SWA_SPAN].astype(BF16)
        res, sink_den = [], []
        for half in range(2):
            k_op = jnp.concatenate([kt, z] if half == 0 else [z, kt], axis=0)
            v_op = jnp.concatenate([vt, z, ones] if half == 0 else [z, vt, ones], axis=0)
            ke_op = jnp.concatenate([ket, ze] if half == 0 else [ze, ket], axis=0)
            ve_op = jnp.concatenate([vet, ze, ones_e] if half == 0 else [ze, vet, ones_e], axis=0)
            s = jnp.where(near, jnp.dot(qb, k_op, preferred_element_type=F32), NEG_INF)
            se = jnp.dot(qb, ke_op, preferred_element_type=F32)
            m = jnp.maximum(jnp.max(s, axis=-1, keepdims=True), jnp.max(se, axis=-1, keepdims=True))
            m = jnp.maximum(m, sinks[half])
            sink_den.append(jnp.exp(sinks[half] - m))
            res.append(_dot_nt(jnp.exp(s - m), v_op) + _dot_nt(jnp.exp(se - m), ve_op))
        num = jnp.where(first, res[0][:, :128], res[1][:, :128])
        den = jnp.where(first, res[0][:, 128:], res[1][:, 128:]) + jnp.where(first, sink_den[0], sink_den[1])
        o_ref[q0:q0 + SWA_QT, :] = num / den


def _swa_latent(q, kt, vt, kt_ctx, vt_ctx, sink, n_batch, group):
    ppk = group // 2
    kv_spec = lambda a: pl.BlockSpec((1, DH, a.shape[2]), lambda b, j: (b, j // ppk, 0))
    return pl.pallas_call(
        _swa_kernel,
        out_shape=jax.ShapeDtypeStruct(q.shape, F32),
        grid=(n_batch, N_HEADS // 2),
        in_specs=[pl.BlockSpec((LAT_LEN, 128), lambda b, j: (b, j)),
                  kv_spec(kt), kv_spec(vt), kv_spec(kt_ctx), kv_spec(vt_ctx),
                  pl.BlockSpec(memory_space=pltpu.SMEM)],
        out_specs=pl.BlockSpec((LAT_LEN, 128), lambda b, j: (b, j)),
        compiler_params=_params(("arbitrary", "arbitrary")), name="swa_latent",
    )(q, kt, vt, kt_ctx, vt_ctx, sink)


def _rope_tables():
    t = jnp.arange(LAT_LEN)
    pos = jnp.stack([t // GRID_W, t % GRID_W], axis=-1).astype(F32)
    freqs = ROPE_THETA ** (-jnp.arange(ROPE_FREQS, dtype=F32) / ROPE_FREQS)
    ang = (pos[:, :, None] * freqs).reshape(LAT_LEN, 2 * ROPE_FREQS)
    cos, sin = jnp.cos(ang), jnp.sin(ang)
    cos_t = jnp.tile(jnp.concatenate([cos, cos], axis=-1), (1, N_HEADS))
    sin_t = jnp.tile(jnp.concatenate([-sin, sin], axis=-1), (1, N_HEADS))
    return cos_t, sin_t


def kernel(x_prompt, x_sample, cache_l0_attn_k, cache_l0_attn_v, state_l0_mlstm_C, state_l0_mlstm_n, state_l0_mlstm_m, cache_l1_na_k, cache_l1_na_v, cache_l1_swa_k, cache_l1_swa_v, c, c_ctx, norm_final, ada_w_l0, ada_b_l0, norm_l0, ffn1_in_l0, ffn1_out_l0, ffn2_in_l0, ffn2_out_l0, mix_in_l0, mix_out_l0, qk_norm_l0, gate_bias_l0, head_norm_l0, ada_w_l1, ada_b_l1, norm_l1, ffn1_in_l1, ffn1_out_l1, ffn2_in_l1, ffn2_out_l1, mix_in_l1, mix_out_l1, rpb_l1, sink_l1):
    nb, nl = x_prompt.shape[0], x_sample.shape[0]
    bf = lambda a: a.astype(BF16)

    cond8 = jnp.zeros((8, D_MODEL), F32).at[0].set(c_ctx).at[1:1 + nl].set(c)
    mods0 = _adaln(cond8, ada_w_l0, ada_b_l0).reshape(8 * ADA_CHUNKS, 1, D_MODEL)
    mods1 = _adaln(cond8, ada_w_l1, ada_b_l1).reshape(8 * ADA_CHUNKS, 1, D_MODEL)

    g0 = 2304
    gcols = lambda j: mix_in_l0[:, g0 + 8 * j:g0 + 8 * (j + 1)]
    gpad = jnp.zeros((D_MODEL, 128 - N_DIRHEAD), F32)
    w_gi, w_gf = jnp.concatenate([gcols(0), gcols(2)], axis=1), jnp.concatenate([gcols(1), gcols(3)], axis=1)
    w_even = bf(jnp.concatenate([mix_in_l0[:, :g0], mix_in_l0[:, g0 + 32:], w_gi, gpad, w_gf, gpad], axis=1))
    w_even_t = bf(jnp.concatenate([w_gi, w_gf, mix_in_l0[:, 1280:1792], mix_in_l0[:, 640:768]], axis=1).T)
    gb4 = gate_bias_l0.reshape(4, N_HEADS)
    b_gi, b_gf = jnp.concatenate([gb4[0], gb4[2]]), jnp.concatenate([gb4[1], gb4[3]])
    gbi = jnp.zeros((1, 128), F32).at[0, :N_DIRHEAD].set(b_gi)
    gbf = jnp.zeros((1, 128), F32).at[0, :N_DIRHEAD].set(b_gf)
    gbt = jnp.concatenate([b_gi, b_gf]).reshape(2 * N_DIRHEAD, 1)
    qg = jnp.tile(qk_norm_l0[0], N_HEADS).reshape(1, 512)
    kg = jnp.tile(qk_norm_l0[1], 2).reshape(1, 128)
    head_g = head_norm_l0.reshape(1, 512)
    grp = np.arange(256) // DH
    bd = jnp.asarray((grp[:, None] == grp[None, :]).astype(np.float32) / DH, dtype=BF16)
    rope_tabs = _rope_tables()
    w_odd = bf(mix_in_l1)
    w_odd_t = bf(mix_in_l1[:, 2048:2304].T)
    rpb_flat = rpb_l1.reshape(-1)
    to_t = lambda a: a.reshape(a.shape[0], a.shape[1], -1).transpose(0, 2, 1)
    from_t = lambda a: a.reshape(a.shape[0], -1, DH, a.shape[2]).transpose(0, 3, 1, 2)

    xp = x_prompt.reshape(nb * CTX_LEN, D_MODEL)
    xs = x_sample.reshape(nl * LAT_LEN, D_MODEL)
    t_ctx, t_lat = xp.shape[0], xs.shape[0]
    streams = ((False, t_ctx, 0, nb, CTX_LEN), (True, t_lat, t_ctx // TOKEN_TILE, nl, LAT_LEN))
    ffn = functools.partial(_ffn, t_ctx=t_ctx, t_lat=t_lat)

    x = ffn((xp, xs), mods0, 0, norm_l0[0], ffn1_in_l0, ffn1_out_l0)
    ya, yb = {}, {}
    for latent, t, blk0, n_batch, seq_len in streams:
        qa, kat, vat, qb, kb, vb, ob, gi, gf, gt, kbt = _proj_even(
            x, t, blk0, mods0, latent, norm_l0[1], w_even, w_even_t, gbi, gbf, gbt, qg, kg, bd, rope_tabs)
        if latent:
            extra = (to_t(cache_l0_attn_k), to_t(cache_l0_attn_v))
            m0 = state_l0_mlstm_m.reshape(nl, N_DIRHEAD)
            m0_lanes = jnp.zeros((nl, 8, 128), F32).at[:, 0, :N_DIRHEAD].set(m0)
            m0_rows = jnp.broadcast_to(m0[:, :, None], (nl, N_DIRHEAD, 128))
            n0_cols = jnp.broadcast_to(state_l0_mlstm_n[..., None], (nl, 2, N_HEADS, DH, 128))
            state = (state_l0_mlstm_C, n0_cols, m0_lanes, m0_rows)
        else:
            extra, state = None, None
            k0t, v0t = kat, vat
        ya[latent] = _attention(qa, kat, vat, n_batch=n_batch, q_len=seq_len, q_tile=256, group=4, kv_t=True, extra=extra)
        ml = _mlstm(qb, kb, kbt, vb, ob, gi, gf, gt, head_g, bd, n_batch=n_batch, seq_len=seq_len, state=state)
        yb[latent] = ml[0]
        if not latent:
            c0, n0, m0_pad = ml[1:]
    x = ffn(x, mods0, 6, norm_l0[2], ffn2_in_l0, ffn2_out_l0, mix=(ya[False], yb[False], ya[True], yb[True], mix_out_l0))

    x = ffn(x, mods1, 0, norm_l1[0], ffn1_in_l1, ffn1_out_l1)
    yc, yd = {}, {}
    for latent, t, blk0, n_batch, seq_len in streams:
        qc, kc, vc, qd, kd, vd = _proj_odd(x, t, blk0, mods1, latent, norm_l1[1], w_odd, w_odd_t, rope_tabs)
        if latent:
            yc[latent] = _na_latent(qc, kc, vc, cache_l1_na_k.reshape(nl * CTX_LEN, 512),
                                    cache_l1_na_v.reshape(nl * CTX_LEN, 512), rpb_flat, nl)
            yd[latent] = _swa_latent(qd, kd, vd, to_t(cache_l1_swa_k), to_t(cache_l1_swa_v), sink_l1, nl, group=4)
        else:
            yc[latent] = _attention(qc, kc, vc, n_batch=n_batch, q_len=seq_len, q_tile=256, group=1, kv_t=False)
            yd[latent] = _attention(qd, kd, vd, n_batch=n_batch, q_len=seq_len, q_tile=256, group=4, kv_t=True, sink=sink_l1)
            kc1, vc1, kd1t, vd1t = kc, vc, kd, vd
    y_prompt, y_sample = ffn(x, mods1, 6, norm_l1[2], ffn2_in_l1, ffn2_out_l1,
                             mix=(yc[False], yd[False], yc[True], yd[True], mix_out_l1), final_g=norm_final)

    return (y_prompt.reshape(nb, CTX_LEN, D_MODEL), y_sample.reshape(nl, LAT_LEN, D_MODEL),
            from_t(k0t), from_t(v0t),
            c0, n0, m0_pad[:, :, 0].reshape(nb, 2, N_HEADS),
            kc1.reshape(nb, CTX_LEN, N_HEADS, DH), vc1.reshape(nb, CTX_LEN, N_HEADS, DH),
            from_t(kd1t), from_t(vd1t))
```

```python
import functools

import jax
import jax.numpy as jnp
import numpy as np
from jax import lax
from jax.experimental import pallas as pl
from jax.experimental.pallas import tpu as pltpu

F32 = jnp.float32
BF16 = jnp.bfloat16

D_MODEL = 1024
DH = 64
D_FF = 2816
ADA_CHUNKS = 9
GRID_W = 64
LAT_LEN = 1024
CTX_LEN = 256
N_HEADS = 8
NA_KH = 8
NA_KW = 16
SWA_WIN = 128
ROPE_THETA = 10000.0
ROPE_FREQS = DH // 4
ATTN_SCALE = DH ** -0.5
NEG_INF = -1e30
EPS = 1e-6

TOKEN_TILE = 512
MLSTM_T = 256
FFN_CHUNK = 512
VMEM_LIMIT = 56 * 1024 * 1024
FFN_VMEM_LIMIT = 60 * 1024 * 1024
W_TILE = (256, 512)
W_SLOTS = 6


def _params(sem, vmem=VMEM_LIMIT):
    return pltpu.CompilerParams(dimension_semantics=sem, vmem_limit_bytes=vmem)


def _dot(a, b):
    return jnp.dot(a.astype(BF16), b.astype(BF16), preferred_element_type=F32)


def _dot_nt(a, b):
    return lax.dot_general(a.astype(BF16), b.astype(BF16), (((1,), (1,)), ((), ())),
                           preferred_element_type=F32)


def _split3(x):
    hi = x.astype(BF16)
    r1 = x - hi.astype(F32)
    mid = r1.astype(BF16)
    lo = (r1 - mid.astype(F32)).astype(BF16)
    return hi, mid, lo


def _dot_exact_rhs(a_bf16, x):
    hi, mid, lo = _split3(x)
    f = lambda p: jnp.dot(a_bf16, p, preferred_element_type=F32)
    return f(hi) + f(mid) + f(lo)


def _dot_exact_lhs(x, b_bf16):
    hi, mid, lo = _split3(x)
    f = lambda p: jnp.dot(p, b_bf16, preferred_element_type=F32)
    return f(hi) + f(mid) + f(lo)


def _silu(x):
    return x * jax.nn.sigmoid(x)


def _log_sigmoid(x):
    return jnp.minimum(x, 0.0) - jnp.log1p(jnp.exp(-jnp.abs(x)))


def _rms(x, g):
    return x * lax.rsqrt(jnp.mean(x * x, axis=-1, keepdims=True) + EPS) * g


def _head_mean_sq(x, bd_ref):
    n = x.shape[-1]
    sq = x * x
    hi = sq.astype(BF16)
    lo = (sq - hi.astype(F32)).astype(BF16)
    w = min(n, bd_ref.shape[0])
    bd = bd_ref[:w, :w]
    f = lambda p: jnp.dot(p, bd, preferred_element_type=F32)
    parts = [f(hi[:, c:c + w]) + f(lo[:, c:c + w]) for c in range(0, n, w)]
    return parts[0] if len(parts) == 1 else jnp.concatenate(parts, axis=1)


def _rope(x, cos, sin_signed):
    n = x.shape[-1]
    lane = lax.broadcasted_iota(jnp.int32, x.shape, 1)
    first_half = (lane % DH) < (DH // 2)
    partner = jnp.where(first_half, pltpu.roll(x, n - DH // 2, 1), pltpu.roll(x, DH // 2, 1))
    return x * cos[:, :n] + partner * sin_signed[:, :n]


def _adaln_kernel(c_ref, w_ref, b_ref, o_ref):
    s = _silu(c_ref[...])
    o_ref[...] = _dot(s, w_ref[...]) + b_ref[...]


def _adaln(cond8, w, b):
    n = w.shape[1]
    tn = 1536
    return pl.pallas_call(
        _adaln_kernel,
        out_shape=jax.ShapeDtypeStruct((8, n), F32),
        grid=(n // tn,),
        in_specs=[pl.BlockSpec((8, D_MODEL), lambda j: (0, 0)),
                  pl.BlockSpec((D_MODEL, tn), lambda j: (0, j)),
                  pl.BlockSpec((1, tn), lambda j: (0, j))],
        out_specs=pl.BlockSpec((8, tn), lambda j: (0, j)),
        compiler_params=_params(("arbitrary",)),
        name="adaln",
    )(cond8, w, b.reshape(1, n))


def _const_spec(shape):
    nd = len(shape)
    return pl.BlockSpec(shape, lambda i, _n=nd: (0,) * _n, pipeline_mode=pl.Buffered(1))


def _mod_spec(chunk, rowfn):
    return pl.BlockSpec((1, 1, D_MODEL), lambda i: (rowfn(i) * ADA_CHUNKS + chunk, 0, 0))


def _rowfn(latent):
    if latent:
        return lambda i: 1 + (i * TOKEN_TILE) // LAT_LEN
    return lambda i: 0


def _tok_spec(width, blk0=0):
    return pl.BlockSpec((TOKEN_TILE, width), lambda i: (i + blk0, 0))


def _seq_len(latent):
    return LAT_LEN if latent else CTX_LEN


def _transposed_out(t, latent, rows):
    seq = _seq_len(latent)
    shape = jax.ShapeDtypeStruct((t // seq, rows, seq), F32)
    if seq <= TOKEN_TILE:
        per = TOKEN_TILE // seq
        return shape, pl.BlockSpec((per, rows, seq), lambda i: (i, 0, 0))
    per = seq // TOKEN_TILE
    return shape, pl.BlockSpec((1, rows, TOKEN_TILE), lambda i: (i // per, 0, i % per))


def _store_transposed(ref, x):
    n, _, w = ref.shape
    for j in range(n):
        ref[j] = x[j * w:(j + 1) * w].T


def _store_proj_transposed(ref, w_rows, h):
    n, _, w = ref.shape
    for j in range(n):
        ref[j] = _dot_nt(w_rows, h[j * w:(j + 1) * w])


def _modnorm(x, ng_ref, sh_ref, sc_ref):
    return _rms(x, ng_ref[...]) * (1.0 + sc_ref[0]) + sh_ref[0]


def _load_weights_bf16(pairs, stage, sem):
    tr, tc = W_TILE
    tiles = [(src, dst, r, c) for src, dst in pairs
             for r in range(0, src.shape[0], tr) for c in range(0, src.shape[1], tc)]

    def copy(k):
        src, _, r, c = tiles[k]
        return pltpu.make_async_copy(src.at[r:r + tr, c:c + tc], stage.at[k % W_SLOTS], sem.at[k % W_SLOTS])

    for k in range(min(W_SLOTS, len(tiles))):
        copy(k).start()
    for k, (_, dst, r, c) in enumerate(tiles):
        copy(k).wait()
        dst[r:r + tr, c:c + tc] = stage[k % W_SLOTS].astype(BF16)
        if k + W_SLOTS < len(tiles):
            copy(k + W_SLOTS).start()


def _ffn_kernel(*refs, n_ctx, first, has_mix, last):
    it = iter(refs)
    xs = (next(it), next(it)) if first else (next(it),)
    if has_mix:
        yac_ref, ybc_ref, yal_ref, ybl_ref, wo_hbm, g2_ref = (next(it) for _ in range(6))
    ng_ref, sh_ref, sc_ref, g_ref, win_hbm, wout_hbm = (next(it) for _ in range(6))
    if last:
        nf_ref = next(it)
    outs = (next(it), next(it)) if last else (next(it),)
    win_s, wout_s = next(it), next(it)
    if has_mix:
        wo_s = next(it)
    stage, sem = next(it), next(it)

    step = pl.program_id(0)
    is_lat = step >= n_ctx

    @pl.when(step == 0)
    def _():
        pairs = [(win_hbm, win_s), (wout_hbm, wout_s)]
        if has_mix:
            pairs.insert(0, (wo_hbm, wo_s))
        _load_weights_bf16(pairs, stage, sem)

    pick = lambda c_ref, l_ref: jnp.where(is_lat, l_ref[...], c_ref[...])
    x = pick(*xs) if first else xs[0][...]
    if has_mix:
        half = wo_s.shape[0] // 2
        mo = _dot(pick(yac_ref, yal_ref), wo_s[:half, :]) + _dot(pick(ybc_ref, ybl_ref), wo_s[half:, :])
        x = x + g2_ref[0] * mo
    h = _modnorm(x, ng_ref, sh_ref, sc_ref).astype(BF16)
    acc = None
    for lo in range(0, D_FF, FFN_CHUNK):
        hi = min(lo + FFN_CHUNK, D_FF)
        g = _dot(h, win_s[:, lo:hi])
        u = _dot(h, win_s[:, D_FF + lo:D_FF + hi])
        part = _dot(_silu(g) * u, wout_s[lo:hi, :])
        acc = part if acc is None else acc + part
    y = x + (0.5 * g_ref[0]) * acc
    if last:
        y = _rms(y, nf_ref[...])
        oc_ref, ol_ref = outs

        @pl.when(jnp.logical_not(is_lat))
        def _():
            oc_ref[...] = y
            ol_ref[...] = jnp.zeros_like(ol_ref)

        @pl.when(is_lat)
        def _():
            ol_ref[...] = y
    else:
        outs[0][...] = y


def _ffn(x, mods, mod_base, ng, w_in, w_out, *, t_ctx, t_lat, mix=None, final_g=None):
    n_ctx, n_lat = t_ctx // TOKEN_TILE, t_lat // TOKEN_TILE
    first, last = isinstance(x, tuple), final_g is not None
    ctx_map = lambda i: (jnp.minimum(i, n_ctx - 1), 0)
    lat_map = lambda i: (jnp.maximum(i - n_ctx, 0), 0)
    rowfn = lambda i: jnp.where(i < n_ctx, 0, 1 + ((i - n_ctx) * TOKEN_TILE) // LAT_LEN)
    two = lambda wd: [pl.BlockSpec((TOKEN_TILE, wd), ctx_map), pl.BlockSpec((TOKEN_TILE, wd), lat_map)]
    hbm = pl.BlockSpec(memory_space=pl.ANY)
    args = list(x) if first else [x]
    specs = two(D_MODEL) if first else [_tok_spec(D_MODEL)]
    scratch = [pltpu.VMEM(w_in.shape, BF16), pltpu.VMEM(w_out.shape, BF16)]
    if mix is not None:
        yac, ybc, yal, ybl, wo = mix
        args += [yac, ybc, yal, ybl, wo, mods]
        mw = yac.shape[1]
        specs += [pl.BlockSpec((TOKEN_TILE, mw), ctx_map)] * 2 + [pl.BlockSpec((TOKEN_TILE, mw), lat_map)] * 2
        specs += [hbm, _mod_spec(mod_base - 1, rowfn)]
        scratch.append(pltpu.VMEM(wo.shape, BF16))
    args += [ng.reshape(1, D_MODEL), mods, mods, mods, w_in, w_out]
    specs += [_const_spec((1, D_MODEL)), _mod_spec(mod_base, rowfn), _mod_spec(mod_base + 1, rowfn),
              _mod_spec(mod_base + 2, rowfn), hbm, hbm]
    if last:
        args.append(final_g.reshape(1, D_MODEL))
        specs.append(_const_spec((1, D_MODEL)))
        out_shape = [jax.ShapeDtypeStruct((t_ctx, D_MODEL), F32), jax.ShapeDtypeStruct((t_lat, D_MODEL), F32)]
        out_specs = two(D_MODEL)
    else:
        out_shape = jax.ShapeDtypeStruct((t_ctx + t_lat, D_MODEL), F32)
        out_specs = _tok_spec(D_MODEL)
    scratch += [pltpu.VMEM((W_SLOTS,) + W_TILE, F32), pltpu.SemaphoreType.DMA((W_SLOTS,))]
    return pl.pallas_call(
        functools.partial(_ffn_kernel, n_ctx=n_ctx, first=first, has_mix=mix is not None, last=last),
        out_shape=out_shape,
        grid=(n_ctx + n_lat,),
        in_specs=specs,
        out_specs=out_specs,
        scratch_shapes=scratch,
        compiler_params=_params(("arbitrary",), FFN_VMEM_LIMIT),
        name="ffn",
    )(*args)


_EV_QA, _EV_KA, _EV_VA, _EV_QB, _EV_KB, _EV_VB, _EV_OB, _EV_GI, _EV_GF, _EV_END = (
    0, 512, 640, 768, 1280, 1792, 2304, 2816, 2944, 3072)
N_DIRHEAD = 2 * N_HEADS
_WT_VA = 2 * N_DIRHEAD + 512


def _proj_even_kernel(*refs, rope):
    it = iter(refs)
    x_ref, ng_ref, sh_ref, sc_ref, w_ref, wt_ref, gbi_ref, gbf_ref, gbt_ref, qg_ref, kg_ref, bd_ref = (
        next(it) for _ in range(12))
    if rope:
        cos_ref, sin_ref = next(it), next(it)
    qa_ref, kat_ref, vat_ref, qb_ref, kb_ref, vb_ref, ob_ref, gi_ref, gf_ref, gt_ref, kbt_ref = (next(it) for _ in range(11))

    h = _modnorm(x_ref[...], ng_ref, sh_ref, sc_ref).astype(BF16)
    proj = lambda lo, hi: _dot(h, w_ref[:, lo:hi])

    qa = proj(_EV_QA, _EV_KA)
    qa = qa * lax.rsqrt(_head_mean_sq(qa, bd_ref) + EPS) * qg_ref[...]
    ka = proj(_EV_KA, _EV_VA)
    ka = ka * lax.rsqrt(_head_mean_sq(ka, bd_ref) + EPS) * kg_ref[...]
    if rope:
        qa = _rope(qa, cos_ref[...], sin_ref[...])
        ka = _rope(ka, cos_ref[...], sin_ref[...])
    qa_ref[...] = qa * ATTN_SCALE
    _store_transposed(kat_ref, ka)
    _store_proj_transposed(vat_ref, wt_ref[_WT_VA:, :], h)
    qb_ref[...] = proj(_EV_QB, _EV_KB)
    kb_ref[...] = proj(_EV_KB, _EV_VB) * ATTN_SCALE
    vb_ref[...] = proj(_EV_VB, _EV_OB)
    ob_ref[...] = proj(_EV_OB, _EV_GI)
    gi_ref[...] = proj(_EV_GI, _EV_GF) + gbi_ref[...]
    gf_ref[...] = proj(_EV_GF, _EV_END) + gbf_ref[...]
    for j in range(TOKEN_TILE // MLSTM_T):
        ht = h[j * MLSTM_T:(j + 1) * MLSTM_T]
        gt_ref[j] = _dot_nt(wt_ref[:2 * N_DIRHEAD, :], ht) + gbt_ref[...]
        kbt_ref[j] = _dot_nt(wt_ref[2 * N_DIRHEAD:_WT_VA, :], ht) * ATTN_SCALE


def _proj_even(x, t, blk0, mods, latent, ng, w, wt, gbi, gbf, gbt, qg, kg, bd, rope_tabs):
    rowfn = _rowfn(latent)
    args = [x, ng.reshape(1, D_MODEL), mods, mods, w, wt, gbi, gbf, gbt, qg, kg, bd]
    specs = [_tok_spec(D_MODEL, blk0), _const_spec((1, D_MODEL)), _mod_spec(3, rowfn), _mod_spec(4, rowfn)]
    specs += [_const_spec(a.shape) for a in args[4:]]
    if latent:
        nblk = LAT_LEN // TOKEN_TILE
        args += list(rope_tabs)
        specs += [pl.BlockSpec((TOKEN_TILE, 512), lambda i: (i % nblk, 0))] * 2
    widths = (512, 512, 512, 512, 512, 128, 128)
    out_shape = [jax.ShapeDtypeStruct((t, wd), F32) for wd in widths]
    out_specs = [_tok_spec(wd) for wd in widths]
    for pos in (1, 2):
        shape, spec = _transposed_out(t, latent, 128)
        out_shape.insert(pos, shape)
        out_specs.insert(pos, spec)
    nj = TOKEN_TILE // MLSTM_T
    for rows in (2 * N_DIRHEAD, 512):
        out_shape.append(jax.ShapeDtypeStruct((t // MLSTM_T, rows, MLSTM_T), F32))
        out_specs.append(pl.BlockSpec((nj, rows, MLSTM_T), lambda i: (i, 0, 0)))
    return pl.pallas_call(
        functools.partial(_proj_even_kernel, rope=latent),
        out_shape=out_shape, grid=(t // TOKEN_TILE,), in_specs=specs, out_specs=out_specs,
        compiler_params=_params(("arbitrary",)), name="proj_even",
    )(*args)


def _proj_odd_kernel(*refs, rope):
    it = iter(refs)
    x_ref, ng_ref, sh_ref, sc_ref, w_ref = (next(it) for _ in range(5))
    wt_ref = next(it)
    if rope:
        cos_ref, sin_ref = next(it), next(it)
    qc_ref, kc_ref, vc_ref, qd_ref, kd_ref, vd_ref = (next(it) for _ in range(6))

    h = _modnorm(x_ref[...], ng_ref, sh_ref, sc_ref).astype(BF16)
    proj = lambda lo, hi: _dot(h, w_ref[:, lo:hi])
    qc_ref[...] = proj(0, 512) * ATTN_SCALE
    kc_ref[...] = proj(512, 1024)
    vc_ref[...] = proj(1024, 1536)
    qd = proj(1536, 2048)
    if rope:
        qd = _rope(qd, cos_ref[...], sin_ref[...])
        _store_transposed(kd_ref, _rope(proj(2048, 2176), cos_ref[...], sin_ref[...]))
    else:
        _store_proj_transposed(kd_ref, wt_ref[:128, :], h)
    _store_proj_transposed(vd_ref, wt_ref[128:, :], h)
    qd_ref[...] = qd * ATTN_SCALE


def _proj_odd(x, t, blk0, mods, latent, ng, w, wt, rope_tabs):
    rowfn = _rowfn(latent)
    args = [x, ng.reshape(1, D_MODEL), mods, mods, w, wt]
    specs = [_tok_spec(D_MODEL, blk0), _const_spec((1, D_MODEL)), _mod_spec(3, rowfn), _mod_spec(4, rowfn),
             _const_spec(w.shape), _const_spec(wt.shape)]
    if latent:
        nblk = LAT_LEN // TOKEN_TILE
        args += list(rope_tabs)
        specs += [pl.BlockSpec((TOKEN_TILE, 512), lambda i: (i % nblk, 0))] * 2
    widths = (512, 512, 512, 512)
    out_shape = [jax.ShapeDtypeStruct((t, wd), F32) for wd in widths]
    out_specs = [_tok_spec(wd) for wd in widths]
    for _ in range(2):
        shape, spec = _transposed_out(t, latent, 128)
        out_shape.append(shape)
        out_specs.append(spec)
    return pl.pallas_call(
        functools.partial(_proj_odd_kernel, rope=latent),
        out_shape=out_shape,
        grid=(t // TOKEN_TILE,), in_specs=specs, out_specs=out_specs,
        compiler_params=_params(("arbitrary",)), name="proj_odd",
    )(*args)


def _attn_kernel(*refs, group, kv_t, has_extra, has_sink, spb):
    it = iter(refs)
    q_ref, k_ref, v_ref = next(it), next(it), next(it)
    if has_extra:
        ke_ref, ve_ref = next(it), next(it)
    if has_sink:
        sink_ref = next(it)
    o_ref = next(it)
    tq = q_ref.shape[0] // spb
    first = lax.broadcasted_iota(jnp.int32, (tq, 128), 1) < DH

    def kv_operands(kt_ref, vt_ref, smp, kv):
        kt = kt_ref[smp, kv * DH:(kv + 1) * DH, :].astype(BF16)
        vt = vt_ref[smp, kv * DH:(kv + 1) * DH, :].astype(BF16)
        z = jnp.zeros_like(kt)
        ones = jnp.ones((128, kt.shape[1]), BF16)
        return ((jnp.concatenate([kt, z], axis=0), jnp.concatenate([z, kt], axis=0)),
                (jnp.concatenate([vt, z, ones], axis=0), jnp.concatenate([z, vt, ones], axis=0)))

    for smp in range(spb):
        qrows = slice(smp * tq, (smp + 1) * tq)
        cache = {}
        for j in range(N_HEADS // 2):
            ps = slice(128 * j, 128 * (j + 1))
            q_pair = q_ref[qrows, ps]
            if kv_t:
                kv = 2 * j // group
                if kv not in cache:
                    cache[kv] = (kv_operands(k_ref, v_ref, smp, kv),
                                 kv_operands(ke_ref, ve_ref, smp, kv) if has_extra else None)
                (k_ops, v_ops), extra = cache[kv]
                qb = q_pair.astype(BF16)
            else:
                keys = k_ref.shape[0] // spb
                krows = slice(smp * keys, (smp + 1) * keys)
                k_pair = k_ref[krows, ps].astype(BF16)
                v_aug = jnp.concatenate([v_ref[krows, ps].astype(BF16), jnp.ones((keys, 128), BF16)], axis=1)
            res, sink_den = [], []
            for half in range(2):
                if kv_t:
                    s = jnp.dot(qb, k_ops[half], preferred_element_type=F32)
                else:
                    s = _dot_nt(jnp.where(first, q_pair, 0.0) if half == 0 else jnp.where(first, 0.0, q_pair), k_pair)
                m = jnp.max(s, axis=-1, keepdims=True)
                if has_extra:
                    se = jnp.dot(qb, extra[0][half], preferred_element_type=F32)
                    m = jnp.maximum(m, jnp.max(se, axis=-1, keepdims=True))
                if has_sink:
                    m = jnp.maximum(m, sink_ref[2 * j + half])
                    sink_den.append(jnp.exp(sink_ref[2 * j + half] - m))
                p = jnp.exp(s - m)
                r = _dot_nt(p, v_ops[half]) if kv_t else _dot(p, v_aug)
                if has_extra:
                    r = r + _dot_nt(jnp.exp(se - m), extra[1][half])
                res.append(r)
            num = jnp.where(first, res[0][:, :128], res[1][:, :128])
            den = jnp.where(first, res[0][:, 128:], res[1][:, 128:])
            if has_sink:
                den = den + jnp.where(first, sink_den[0], sink_den[1])
            o_ref[qrows, ps] = num / den


def _attention(q, k, v, *, n_batch, q_len, q_tile, group, kv_t, extra=None, sink=None):
    nq = q_len // q_tile
    spb = 2 if nq == 1 and n_batch % 2 == 0 else 1
    args = [q, k, v]
    specs = [pl.BlockSpec((spb * q_tile, 512), lambda b, j: (b * nq + j, 0))]
    if kv_t:
        kv_spec = lambda a: pl.BlockSpec((spb,) + a.shape[1:], lambda b, j: (b, 0, 0))
        specs += [kv_spec(k), kv_spec(v)]
        if extra is not None:
            args += list(extra)
            specs += [kv_spec(extra[0]), kv_spec(extra[1])]
    else:
        assert extra is None
        keys = k.shape[0] // n_batch
        specs += [pl.BlockSpec((spb * keys, 512), lambda b, j: (b, 0))] * 2
    if sink is not None:
        args.append(sink)
        specs.append(pl.BlockSpec(memory_space=pltpu.SMEM))
    return pl.pallas_call(
        functools.partial(_attn_kernel, group=group, kv_t=kv_t, has_extra=extra is not None,
                          has_sink=sink is not None, spb=spb),
        out_shape=jax.ShapeDtypeStruct(q.shape, F32),
        grid=(n_batch // spb, nq), in_specs=specs,
        out_specs=pl.BlockSpec((spb * q_tile, 512), lambda b, j: (b * nq + j, 0)),
        compiler_params=_params(("arbitrary", "arbitrary")), name="attn",
    )(*args)


LOG2E = 1.4426950408889634


def _cummax_time(x, reverse):
    n = x.shape[0]
    row = lax.broadcasted_iota(jnp.int32, x.shape, 0)
    k = 1
    while k < n:
        if reverse:
            shifted = jnp.where(row < n - k, pltpu.roll(x, n - k, 0), -jnp.inf)
        else:
            shifted = jnp.where(row >= k, pltpu.roll(x, k, 0), -jnp.inf)
        x = jnp.maximum(x, shifted)
        k *= 2
    return x


def _mlstm_kernel(*refs, n_chunks, carry):
    it = iter(refs)
    q_ref, k_ref, kt_ref, v_ref, ob_ref, gi_ref, gf_ref, gt_ref, hg_ref, bd_ref = (next(it) for _ in range(10))
    if carry:
        c0_ref, n0_ref, m0r_ref, m0c_ref = (next(it) for _ in range(4))
    y_ref = next(it)
    if not carry:
        c_ref, n_ref, m_ref = next(it), next(it), next(it)
    hf_s, hb_s = next(it), next(it)
    if carry:
        st_s, mrow_s, mcol_s = next(it), next(it), next(it)
    T = MLSTM_T
    step = pl.program_id(1)

    if carry:
        @pl.when(step == 0)
        def _():
            z = jnp.zeros((DH, DH), F32)
            for d in range(2):
                for j in range(N_HEADS // 2):
                    ca, cb = c0_ref[0, d, 2 * j], c0_ref[0, d, 2 * j + 1]
                    na, nb = n0_ref[0, d, 2 * j][:, :DH], n0_ref[0, d, 2 * j + 1][:, :DH]
                    st_s[d, j, 0:DH, :] = jnp.concatenate([ca, z, na, z], axis=1)
                    st_s[d, j, DH:2 * DH, :] = jnp.concatenate([z, cb, z, nb], axis=1)
            mrow_s[...] = m0r_ref[0]
            mcol_s[...] = m0c_ref[0]

    row = lax.broadcasted_iota(jnp.int32, (T, T), 0)
    col = lax.broadcasted_iota(jnp.int32, (T, T), 1)
    lower = row >= col
    upper = row <= col
    lower_m = jnp.where(lower, 1.0, 0.0).astype(BF16)
    upper_m = jnp.where(upper, 1.0, 0.0).astype(BF16)
    lane = lax.broadcasted_iota(jnp.int32, (T, 128), 1)
    first = lane < DH
    lane_c = lax.broadcasted_iota(jnp.int32, (DH, 256), 1) % 128
    zeros_kt = jnp.zeros((DH, T), BF16)
    ones_v = jnp.ones((T, 128), BF16)

    for d in range(2):
        chunk = step if d == 0 else n_chunks - 1 - step
        rows = pl.ds(pl.multiple_of(chunk * T, T), T)
        hd0 = N_HEADS * d
        mask = lower if d == 0 else upper

        gi, gf = gi_ref[rows, :], gf_ref[rows, :]
        lf = _log_sigmoid(gf)
        b_cols = _dot_exact_rhs(lower_m, lf)
        if d == 1:
            b_cols = b_cols[T - 1:T, :] - b_cols + lf
        a_cols = gi - b_cols
        dmax = b_cols + _cummax_time(a_cols, reverse=d == 1)
        if carry:
            inter = b_cols + mrow_s[0:1, :]
            mt = jnp.maximum(inter, dmax)
            w_inter = jnp.exp(inter - mt)
        else:
            mt = jnp.maximum(b_cols, dmax)
        c2 = (b_cols - mt) * LOG2E
        einv = jnp.exp(-mt)

        gt = gt_ref[chunk]
        gi_t = gt[hd0:hd0 + N_HEADS, :]
        gf_t = gt[N_DIRHEAD + hd0:N_DIRHEAD + hd0 + N_HEADS, :]
        lf_t = _log_sigmoid(gf_t)
        b_rows = _dot_exact_lhs(lf_t, upper_m)
        if d == 1:
            b_rows = b_rows[:, T - 1:T] - b_rows + lf_t
        a_rows = gi_t - b_rows
        a2 = a_rows * LOG2E
        b_last = b_rows[:, T - 1:T] if d == 0 else b_rows[:, 0:1]
        dec_rows = b_last + a_rows
        m_new = jnp.max(dec_rows, axis=-1, keepdims=True)
        if carry:
            m_prev = mcol_s[hd0:hd0 + N_HEADS, 0:1]
            m_new = jnp.maximum(m_new, b_last + m_prev)
            wc = jnp.exp(b_last + m_prev - m_new)
        else:
            m_new = jnp.maximum(m_new, b_last)
        ws_rows = jnp.exp(dec_rows - m_new)

        h_s = hf_s if d == 0 else hb_s
        for j in range(N_HEADS // 2):
            ps = slice(128 * j, 128 * (j + 1))
            q_pair = q_ref[rows, ps].astype(BF16)
            v_aug = jnp.concatenate([v_ref[rows, ps].astype(BF16), ones_v], axis=1)
            kt = (kt_ref[chunk, 128 * j:128 * j + DH, :], kt_ref[chunk, 128 * j + DH:128 * (j + 1), :])
            lhs = []
            for half in range(2):
                hd = 2 * j + half
                kt0 = jnp.concatenate([kt[0].astype(BF16), zeros_kt] if half == 0 else [zeros_kt, kt[1].astype(BF16)], axis=0)
                qk = jnp.dot(q_pair, kt0, preferred_element_type=F32)
                e = c2[:, hd0 + hd:hd0 + hd + 1] + a2[hd:hd + 1, :]
                lhs.append((jnp.exp2(jnp.where(mask, e, -jnp.inf)) * qk).astype(BF16))
            for half in range(2):
                lhs.append((kt[half] * ws_rows[2 * j + half:2 * j + half + 1, :]).astype(BF16))
            both = jnp.dot(jnp.concatenate(lhs, axis=0), v_aug, preferred_element_type=F32)
            res = (both[0:T], both[T:2 * T])
            cu = (both[2 * T:2 * T + DH], both[2 * T + DH:2 * T + 2 * DH])
            ra, rb = hd0 + 2 * j, hd0 + 2 * j + 1
            num = jnp.where(first, res[0][:, :128], res[1][:, :128])
            den = jnp.where(first, res[0][:, 128:], res[1][:, 128:])
            if carry:
                old = st_s[d, j]
                qs = _dot(q_pair, old)
                wi = jnp.where(first, w_inter[:, ra:ra + 1], w_inter[:, rb:rb + 1])
                num = num + wi * qs[:, :128]
                den = den + wi * qs[:, 128:]
            floor = jnp.where(first, einv[:, ra:ra + 1], einv[:, rb:rb + 1])
            h_s[rows, ps] = num / jnp.maximum(jnp.abs(den), floor)
            if carry:
                st_s[d, j, 0:DH, :] = jnp.where(lane_c < DH, cu[0], 0.0) + wc[2 * j:2 * j + 1, :] * old[0:DH]
                st_s[d, j, DH:2 * DH, :] = jnp.where(lane_c >= DH, cu[1], 0.0) + wc[2 * j + 1:2 * j + 2, :] * old[DH:2 * DH]
            else:
                c_ref[0, d, 2 * j] = cu[0][:, 0:DH]
                c_ref[0, d, 2 * j + 1] = cu[1][:, DH:2 * DH]

        if carry:
            b_last_l = b_cols[T - 1:T, :] if d == 0 else b_cols[0:1, :]
            m_prev_l = mrow_s[0:1, :]
            m_new_l = jnp.maximum(jnp.max(b_last_l + a_cols, axis=0, keepdims=True), b_last_l + m_prev_l)
            lane_r = lax.broadcasted_iota(jnp.int32, (1, 128), 1)
            mine = (lane_r >= hd0) & (lane_r < hd0 + N_HEADS)
            mrow_s[0:1, :] = jnp.where(mine, m_new_l, m_prev_l)
            mcol_s[hd0:hd0 + N_HEADS, :] = jnp.broadcast_to(m_new, (N_HEADS, 128))
        else:
            n_all = _dot(ws_rows, k_ref[rows, :])
            for hd in range(N_HEADS):
                n_ref[0, d, hd:hd + 1, :] = n_all[hd:hd + 1, hd * DH:(hd + 1) * DH]
            m_ref[0, hd0:hd0 + N_HEADS, :] = jnp.broadcast_to(m_new, (N_HEADS, 128))

    @pl.when(step == n_chunks - 1)
    def _():
        hsum = hf_s[...] + hb_s[...]
        y = hsum * lax.rsqrt(_head_mean_sq(hsum, bd_ref) + EPS) * hg_ref[...]
        y_ref[...] = jax.nn.sigmoid(ob_ref[...]) * y


def _mlstm(qb, kb, kbt, vb, ob, gi, gf, gt, head_g, bd, *, n_batch, seq_len, state=None):
    n_chunks = seq_len // MLSTM_T
    carry = state is not None
    assert carry or n_chunks == 1
    seq = lambda wd: pl.BlockSpec((seq_len, wd), lambda b, c: (b, 0))
    per_chunk = lambda r: pl.BlockSpec((n_chunks, r, MLSTM_T), lambda b, c: (b, 0, 0))
    args = [qb, kb, kbt, vb, ob, gi, gf, gt, head_g, bd]
    specs = [seq(512), seq(512), per_chunk(512), seq(512), seq(512), seq(128), seq(128), per_chunk(2 * N_DIRHEAD),
             pl.BlockSpec((1, 512), lambda b, c: (0, 0)), pl.BlockSpec(bd.shape, lambda b, c: (0, 0))]
    out_shape = [jax.ShapeDtypeStruct((n_batch * seq_len, 512), F32)]
    out_specs = [seq(512)]
    scratch = [pltpu.VMEM((seq_len, 512), F32), pltpu.VMEM((seq_len, 512), F32)]
    if carry:
        args += list(state)
        specs += [pl.BlockSpec((1, 2, N_HEADS, DH, DH), lambda b, c: (b, 0, 0, 0, 0)),
                  pl.BlockSpec((1, 2, N_HEADS, DH, 128), lambda b, c: (b, 0, 0, 0, 0)),
                  pl.BlockSpec((1, 8, 128), lambda b, c: (b, 0, 0)),
                  pl.BlockSpec((1, N_DIRHEAD, 128), lambda b, c: (b, 0, 0))]
        scratch += [pltpu.VMEM((2, N_HEADS // 2, 128, 256), F32), pltpu.VMEM((8, 128), F32),
                    pltpu.VMEM((N_DIRHEAD, 128), F32)]
    else:
        out_shape += [jax.ShapeDtypeStruct((n_batch, 2, N_HEADS, DH, DH), F32),
                      jax.ShapeDtypeStruct((n_batch, 2, N_HEADS, DH), F32),
                      jax.ShapeDtypeStruct((n_batch, N_DIRHEAD, 128), F32)]
        out_specs += [pl.BlockSpec((1, 2, N_HEADS, DH, DH), lambda b, c: (b, 0, 0, 0, 0)),
                      pl.BlockSpec((1, 2, N_HEADS, DH), lambda b, c: (b, 0, 0, 0)),
                      pl.BlockSpec((1, N_DIRHEAD, 128), lambda b, c: (b, 0, 0))]
    return pl.pallas_call(
        functools.partial(_mlstm_kernel, n_chunks=n_chunks, carry=carry),
        out_shape=out_shape, grid=(n_batch, n_chunks), in_specs=specs, out_specs=out_specs,
        scratch_shapes=scratch,
        compiler_params=_params(("arbitrary", "arbitrary")), name="mlstm",
    )(*args)


N_ROWS = LAT_LEN // GRID_W
N_DY = 2 * NA_KH - 1
N_DX = 2 * NA_KW - 1


NA_QROWS = 4
NA_KROWS = 12
NA_GROUPS = N_ROWS // NA_QROWS


def _na_kernel(q_ref, k_ref, v_ref, ke_ref, ve_ref, rpb_ref, o_ref, tile_s, slab_s):
    pair = pl.program_id(1)
    qn, kn = NA_QROWS * GRID_W, NA_KROWS * GRID_W

    @pl.when(pl.program_id(0) == 0)
    def _():
        qc = lax.broadcasted_iota(jnp.int32, (GRID_W, GRID_W), 0)
        kc = lax.broadcasted_iota(jnp.int32, (GRID_W, GRID_W), 1)
        start = jnp.clip(qc - NA_KW // 2, 0, GRID_W - NA_KW)
        in_win = (kc >= start) & (kc < start + NA_KW)
        dx = kc - qc + (NA_KW - 1)
        blocked = jnp.full((GRID_W, GRID_W), NEG_INF, F32)
        for half in range(2):
            head = 2 * pair + half

            def build_tile(dy, carry):
                tile = blocked
                for j in range(N_DX):
                    tile = jnp.where(dx == j, rpb_ref[(head * N_DY + dy) * N_DX + j], tile)
                tile_s[half * N_DY + dy] = jnp.where(in_win, tile, NEG_INF)
                return carry

            lax.fori_loop(0, N_DY, build_tile, 0)
            for kind in range(3):
                idx = (pair * 2 + half) * 3 + kind
                for i in range(NA_QROWS):
                    for k in range(NA_KROWS):
                        if kind == 0:
                            ok, dy = k < NA_KH, k - i + NA_KH - 1
                        elif kind == 1:
                            ok, dy = i <= k < i + NA_KH, k - i + NA_KH // 2 - 1
                        else:
                            ok, dy = k >= NA_KROWS - NA_KH, k - i - 1
                        slab_s[idx, i * GRID_W:(i + 1) * GRID_W, k * GRID_W:(k + 1) * GRID_W] = (
                            tile_s[half * N_DY + dy] if ok else blocked)

    first = lax.broadcasted_iota(jnp.int32, (qn, 128), 1) < DH
    ones_k = jnp.ones((kn, 128), BF16)
    ke = ke_ref[...].astype(BF16)
    ve_aug = jnp.concatenate([ve_ref[...].astype(BF16), jnp.ones((ke.shape[0], 128), BF16)], axis=1)

    def group_body(g, carry):
        k0 = jnp.where(g < NA_GROUPS // 2, 0, N_ROWS - NA_KROWS)
        kind = jnp.where(g == 0, 0, jnp.where(g == NA_GROUPS - 1, 2, 1))
        qrows = pl.ds(pl.multiple_of(g * qn, qn), qn)
        keys = pl.ds(pl.multiple_of(k0 * GRID_W, GRID_W), kn)
        q_pair = q_ref[qrows, :]
        k_pair = k_ref[keys, :].astype(BF16)
        v_aug = jnp.concatenate([v_ref[keys, :].astype(BF16), ones_k], axis=1)
        res = []
        for half in range(2):
            qm = jnp.where(first, q_pair, 0.0) if half == 0 else jnp.where(first, 0.0, q_pair)
            s = _dot_nt(qm, k_pair) + slab_s[(pair * 2 + half) * 3 + kind]
            se = _dot_nt(qm, ke)
            m = jnp.maximum(jnp.max(s, axis=-1, keepdims=True), jnp.max(se, axis=-1, keepdims=True))
            res.append(_dot(jnp.exp(s - m), v_aug) + _dot(jnp.exp(se - m), ve_aug))
        num = jnp.where(first, res[0][:, :128], res[1][:, :128])
        den = jnp.where(first, res[0][:, 128:], res[1][:, 128:])
        o_ref[qrows, :] = num / den
        return carry

    lax.fori_loop(0, NA_GROUPS, group_body, 0)


def _na_latent(q, k, v, k_ctx, v_ctx, rpb_flat, n_batch):
    seq = pl.BlockSpec((LAT_LEN, 128), lambda b, j: (b, j))
    ctx = pl.BlockSpec((CTX_LEN, 128), lambda b, j: (b, j))
    return pl.pallas_call(
        _na_kernel,
        out_shape=jax.ShapeDtypeStruct(q.shape, F32),
        grid=(n_batch, N_HEADS // 2),
        in_specs=[seq, seq, seq, ctx, ctx, pl.BlockSpec(memory_space=pltpu.SMEM)],
        out_specs=seq,
        scratch_shapes=[pltpu.VMEM((2 * N_DY, GRID_W, GRID_W), F32),
                        pltpu.VMEM((N_HEADS * 3, NA_QROWS * GRID_W, NA_KROWS * GRID_W), F32)],
        compiler_params=_params(("arbitrary", "arbitrary")), name="na_latent",
    )(q, k, v, k_ctx, v_ctx, rpb_flat)


SWA_QT = 256
SWA_SPAN = SWA_QT + 2 * SWA_WIN


def _swa_kernel(q_ref, kt_ref, vt_ref, ket_ref, vet_ref, sink_ref, o_ref):
    pair = pl.program_id(1)
    first = lax.broadcasted_iota(jnp.int32, (SWA_QT, 128), 1) < DH
    row = lax.broadcasted_iota(jnp.int32, (SWA_QT, SWA_SPAN), 0)
    col = lax.broadcasted_iota(jnp.int32, (SWA_QT, SWA_SPAN), 1)
    z = jnp.zeros((DH, SWA_SPAN), BF16)
    ones = jnp.ones((128, SWA_SPAN), BF16)
    ket, vet = ket_ref[0].astype(BF16), vet_ref[0].astype(BF16)
    ze = jnp.zeros_like(ket)
    ones_e = jnp.ones((128, ket.shape[1]), BF16)
    sinks = (sink_ref[2 * pair], sink_ref[2 * pair + 1])
    for g in range(LAT_LEN // SWA_QT):
        q0 = g * SWA_QT
        lo = min(max(q0 - SWA_WIN, 0), LAT_LEN - SWA_SPAN)
        near = jnp.abs((lo + col) - (q0 + row)) <= SWA_WIN
        qb = q_ref[q0:q0 + SWA_QT, :].astype(BF16)
        kt = kt_ref[0, :, lo:lo + SWA_SPAN].astype(BF16)
        vt = vt_ref[0, :, lo:lo + SWA_SPAN].astype(BF16)
        res, sink_den = [], []
        for half in range(2):
            k_op = jnp.concatenate([kt, z] if half == 0 else [z, kt], axis=0)
            v_op = jnp.concatenate([vt, z, ones] if half == 0 else [z, vt, ones], axis=0)
            ke_op = jnp.concatenate([ket, ze] if half == 0 else [ze, ket], axis=0)
            ve_op = jnp.concatenate([vet, ze, ones_e] if half == 0 else [ze, vet, ones_e], axis=0)
            s = jnp.where(near, jnp.dot(qb, k_op, preferred_element_type=F32), NEG_INF)
            se = jnp.dot(qb, ke_op, preferred_element_type=F32)
            m = jnp.maximum(jnp.max(s, axis=-1, keepdims=True), jnp.max(se, axis=-1, keepdims=True))
            m = jnp.maximum(m, sinks[half])
            sink_den.append(jnp.exp(sinks[half] - m))
            res.append(_dot_nt(jnp.exp(s - m), v_op) + _dot_nt(jnp.exp(se - m), ve_op))
        num = jnp.where(first, res[0][:, :128], res[1][:, :128])
        den = jnp.where(first, res[0][:, 128:], res[1][:, 128:]) + jnp.where(first, sink_den[0], sink_den[1])
        o_ref[q0:q0 + SWA_QT, :] = num / den


def _swa_latent(q, kt, vt, kt_ctx, vt_ctx, sink, n_batch, group):
    ppk = group // 2
    kv_spec = lambda a: pl.BlockSpec((1, DH, a.shape[2]), lambda b, j: (b, j // ppk, 0))
    return pl.pallas_call(
        _swa_kernel,
        out_shape=jax.ShapeDtypeStruct(q.shape, F32),
        grid=(n_batch, N_HEADS // 2),
        in_specs=[pl.BlockSpec((LAT_LEN, 128), lambda b, j: (b, j)),
                  kv_spec(kt), kv_spec(vt), kv_spec(kt_ctx), kv_spec(vt_ctx),
                  pl.BlockSpec(memory_space=pltpu.SMEM)],
        out_specs=pl.BlockSpec((LAT_LEN, 128), lambda b, j: (b, j)),
        compiler_params=_params(("arbitrary", "arbitrary")), name="swa_latent",
    )(q, kt, vt, kt_ctx, vt_ctx, sink)


def _rope_tables():
    t = jnp.arange(LAT_LEN)
    pos = jnp.stack([t // GRID_W, t % GRID_W], axis=-1).astype(F32)
    freqs = ROPE_THETA ** (-jnp.arange(ROPE_FREQS, dtype=F32) / ROPE_FREQS)
    ang = (pos[:, :, None] * freqs).reshape(LAT_LEN, 2 * ROPE_FREQS)
    cos, sin = jnp.cos(ang), jnp.sin(ang)
    cos_t = jnp.tile(jnp.concatenate([cos, cos], axis=-1), (1, N_HEADS))
    sin_t = jnp.tile(jnp.concatenate([-sin, sin], axis=-1), (1, N_HEADS))
    return cos_t, sin_t


def kernel(x_prompt, x_sample, cache_l0_attn_k, cache_l0_attn_v, state_l0_mlstm_C, state_l0_mlstm_n, state_l0_mlstm_m, cache_l1_na_k, cache_l1_na_v, cache_l1_swa_k, cache_l1_swa_v, c, c_ctx, norm_final, ada_w_l0, ada_b_l0, norm_l0, ffn1_in_l0, ffn1_out_l0, ffn2_in_l0, ffn2_out_l0, mix_in_l0, mix_out_l0, qk_norm_l0, gate_bias_l0, head_norm_l0, ada_w_l1, ada_b_l1, norm_l1, ffn1_in_l1, ffn1_out_l1, ffn2_in_l1, ffn2_out_l1, mix_in_l1, mix_out_l1, rpb_l1, sink_l1):
    nb, nl = x_prompt.shape[0], x_sample.shape[0]
    bf = lambda a: a.astype(BF16)

    cond8 = jnp.zeros((8, D_MODEL), F32).at[0].set(c_ctx).at[1:1 + nl].set(c)
    mods0 = _adaln(cond8, ada_w_l0, ada_b_l0).reshape(8 * ADA_CHUNKS, 1, D_MODEL)
    mods1 = _adaln(cond8, ada_w_l1, ada_b_l1).reshape(8 * ADA_CHUNKS, 1, D_MODEL)

    g0 = 2304
    gcols = lambda j: mix_in_l0[:, g0 + 8 * j:g0 + 8 * (j + 1)]
    gpad = jnp.zeros((D_MODEL, 128 - N_DIRHEAD), F32)
    w_gi, w_gf = jnp.concatenate([gcols(0), gcols(2)], axis=1), jnp.concatenate([gcols(1), gcols(3)], axis=1)
    w_even = bf(jnp.concatenate([mix_in_l0[:, :g0], mix_in_l0[:, g0 + 32:], w_gi, gpad, w_gf, gpad], axis=1))
    w_even_t = bf(jnp.concatenate([w_gi, w_gf, mix_in_l0[:, 1280:1792], mix_in_l0[:, 640:768]], axis=1).T)
    gb4 = gate_bias_l0.reshape(4, N_HEADS)
    b_gi, b_gf = jnp.concatenate([gb4[0], gb4[2]]), jnp.concatenate([gb4[1], gb4[3]])
    gbi = jnp.zeros((1, 128), F32).at[0, :N_DIRHEAD].set(b_gi)
    gbf = jnp.zeros((1, 128), F32).at[0, :N_DIRHEAD].set(b_gf)
    gbt = jnp.concatenate([b_gi, b_gf]).reshape(2 * N_DIRHEAD, 1)
    qg = jnp.tile(qk_norm_l0[0], N_HEADS).reshape(1, 512)
    kg = jnp.tile(qk_norm_l0[1], 2).reshape(1, 128)
    head_g = head_norm_l0.reshape(1, 512)
    grp = np.arange(256) // DH
    bd = jnp.asarray((grp[:, None] == grp[None, :]).astype(np.float32) / DH, dtype=BF16)
    rope_tabs = _rope_tables()
    w_odd = bf(mix_in_l1)
    w_odd_t = bf(mix_in_l1[:, 2048:2304].T)
    rpb_flat = rpb_l1.reshape(-1)
    to_t = lambda a: a.reshape(a.shape[0], a.shape[1], -1).transpose(0, 2, 1)
    from_t = lambda a: a.reshape(a.shape[0], -1, DH, a.shape[2]).transpose(0, 3, 1, 2)

    xp = x_prompt.reshape(nb * CTX_LEN, D_MODEL)
    xs = x_sample.reshape(nl * LAT_LEN, D_MODEL)
    t_ctx, t_lat = xp.shape[0], xs.shape[0]
    streams = ((False, t_ctx, 0, nb, CTX_LEN), (True, t_lat, t_ctx // TOKEN_TILE, nl, LAT_LEN))
    ffn = functools.partial(_ffn, t_ctx=t_ctx, t_lat=t_lat)

    x = ffn((xp, xs), mods0, 0, norm_l0[0], ffn1_in_l0, ffn1_out_l0)
    ya, yb = {}, {}
    for latent, t, blk0, n_batch, seq_len in streams:
        qa, kat, vat, qb, kb, vb, ob, gi, gf, gt, kbt = _proj_even(
            x, t, blk0, mods0, latent, norm_l0[1], w_even, w_even_t, gbi, gbf, gbt, qg, kg, bd, rope_tabs)
        if latent:
            extra = (to_t(cache_l0_attn_k), to_t(cache_l0_attn_v))
            m0 = state_l0_mlstm_m.reshape(nl, N_DIRHEAD)
            m0_lanes = jnp.zeros((nl, 8, 128), F32).at[:, 0, :N_DIRHEAD].set(m0)
            m0_rows = jnp.broadcast_to(m0[:, :, None], (nl, N_DIRHEAD, 128))
            n0_cols = jnp.broadcast_to(state_l0_mlstm_n[..., None], (nl, 2, N_HEADS, DH, 128))
            state = (state_l0_mlstm_C, n0_cols, m0_lanes, m0_rows)
        else:
            extra, state = None, None
            k0t, v0t = kat, vat
        ya[latent] = _attention(qa, kat, vat, n_batch=n_batch, q_len=seq_len, q_tile=256, group=4, kv_t=True, extra=extra)
        ml = _mlstm(qb, kb, kbt, vb, ob, gi, gf, gt, head_g, bd, n_batch=n_batch, seq_len=seq_len, state=state)
        yb[latent] = ml[0]
        if not latent:
            c0, n0, m0_pad = ml[1:]
    x = ffn(x, mods0, 6, norm_l0[2], ffn2_in_l0, ffn2_out_l0, mix=(ya[False], yb[False], ya[True], yb[True], mix_out_l0))

    x = ffn(x, mods1, 0, norm_l1[0], ffn1_in_l1, ffn1_out_l1)
    yc, yd = {}, {}
    for latent, t, blk0, n_batch, seq_len in streams:
        qc, kc, vc, qd, kd, vd = _proj_odd(x, t, blk0, mods1, latent, norm_l1[1], w_odd, w_odd_t, rope_tabs)
        if latent:
            yc[latent] = _na_latent(qc, kc, vc, cache_l1_na_k.reshape(nl * CTX_LEN, 512),
                                    cache_l1_na_v.reshape(nl * CTX_LEN, 512), rpb_flat, nl)
            yd[latent] = _swa_latent(qd, kd, vd, to_t(cache_l1_swa_k), to_t(cache_l1_swa_v), sink_l1, nl, group=4)
        else:
            yc[latent] = _attention(qc, kc, vc, n_batch=n_batch, q_len=seq_len, q_tile=256, group=1, kv_t=False)
            yd[latent] = _attention(qd, kd, vd, n_batch=n_batch, q_len=seq_len, q_tile=256, group=4, kv_t=True, sink=sink_l1)
            kc1, vc1, kd1t, vd1t = kc, vc, kd, vd
    y_prompt, y_sample = ffn(x, mods1, 6, norm_l1[2], ffn2_in_l1, ffn2_out_l1,
                             mix=(yc[False], yd[False], yc[True], yd[True], mix_out_l1), final_g=norm_final)

    return (y_prompt.reshape(nb, CTX_LEN, D_MODEL), y_sample.reshape(nl, LAT_LEN, D_MODEL),
            from_t(k0t), from_t(v0t),
            c0, n0, m0_pad[:, :, 0].reshape(nb, 2, N_HEADS),
            kc1.reshape(nb, CTX_LEN, N_HEADS, DH), vc1.reshape(nb, CTX_LEN, N_HEADS, DH),
            from_t(kd1t), from_t(vd1t))
```

```python
import functools

import jax
import jax.numpy as jnp
import numpy as np
from jax import lax
from jax.experimental import pallas as pl
from jax.experimental.pallas import tpu as pltpu

F32 = jnp.float32
BF16 = jnp.bfloat16

D_MODEL = 1024
DH = 64
D_FF = 2816
ADA_CHUNKS = 9
GRID_W = 64
LAT_LEN = 1024
CTX_LEN = 256
N_HEADS = 8
NA_KH = 8
NA_KW = 16
SWA_WIN = 128
ROPE_THETA = 10000.0
ROPE_FREQS = DH // 4
ATTN_SCALE = DH ** -0.5
NEG_INF = -1e30
EPS = 1e-6

TOKEN_TILE = 512
MLSTM_T = 256
FFN_CHUNK = 512
VMEM_LIMIT = 56 * 1024 * 1024
FFN_VMEM_LIMIT = 60 * 1024 * 1024
W_TILE = (256, 512)
W_SLOTS = 6


def _params(sem, vmem=VMEM_LIMIT):
    return pltpu.CompilerParams(dimension_semantics=sem, vmem_limit_bytes=vmem)


def _dot(a, b):
    return jnp.dot(a.astype(BF16), b.astype(BF16), preferred_element_type=F32)


def _dot_nt(a, b):
    return lax.dot_general(a.astype(BF16), b.astype(BF16), (((1,), (1,)), ((), ())),
                           preferred_element_type=F32)


def _split3(x):
    hi = x.astype(BF16)
    r1 = x - hi.astype(F32)
    mid = r1.astype(BF16)
    lo = (r1 - mid.astype(F32)).astype(BF16)
    return hi, mid, lo


def _dot_exact_rhs(a_bf16, x):
    hi, mid, lo = _split3(x)
    f = lambda p: jnp.dot(a_bf16, p, preferred_element_type=F32)
    return f(hi) + f(mid) + f(lo)


def _dot_exact_lhs(x, b_bf16):
    hi, mid, lo = _split3(x)
    f = lambda p: jnp.dot(p, b_bf16, preferred_element_type=F32)
    return f(hi) + f(mid) + f(lo)


def _silu(x):
    return x * jax.nn.sigmoid(x)


def _log_sigmoid(x):
    return jnp.minimum(x, 0.0) - jnp.log1p(jnp.exp(-jnp.abs(x)))


def _rms(x, g):
    return x * lax.rsqrt(jnp.mean(x * x, axis=-1, keepdims=True) + EPS) * g


def _head_mean_sq(x, bd_ref):
    n = x.shape[-1]
    sq = x * x
    hi = sq.astype(BF16)
    lo = (sq - hi.astype(F32)).astype(BF16)
    w = min(n, bd_ref.shape[0])
    bd = bd_ref[:w, :w]
    f = lambda p: jnp.dot(p, bd, preferred_element_type=F32)
    parts = [f(hi[:, c:c + w]) + f(lo[:, c:c + w]) for c in range(0, n, w)]
    return parts[0] if len(parts) == 1 else jnp.concatenate(parts, axis=1)


def _rope(x, cos, sin_signed):
    n = x.shape[-1]
    lane = lax.broadcasted_iota(jnp.int32, x.shape, 1)
    first_half = (lane % DH) < (DH // 2)
    partner = jnp.where(first_half, pltpu.roll(x, n - DH // 2, 1), pltpu.roll(x, DH // 2, 1))
    return x * cos[:, :n] + partner * sin_signed[:, :n]


def _adaln_kernel(c_ref, w_ref, b_ref, o_ref):
    s = _silu(c_ref[...])
    o_ref[...] = _dot(s, w_ref[...]) + b_ref[...]


def _adaln(cond8, w, b):
    n = w.shape[1]
    tn = 1536
    return pl.pallas_call(
        _adaln_kernel,
        out_shape=jax.ShapeDtypeStruct((8, n), F32),
        grid=(n // tn,),
        in_specs=[pl.BlockSpec((8, D_MODEL), lambda j: (0, 0)),
                  pl.BlockSpec((D_MODEL, tn), lambda j: (0, j)),
                  pl.BlockSpec((1, tn), lambda j: (0, j))],
        out_specs=pl.BlockSpec((8, tn), lambda j: (0, j)),
        compiler_params=_params(("arbitrary",)),
        name="adaln",
    )(cond8, w, b.reshape(1, n))


def _const_spec(shape):
    nd = len(shape)
    return pl.BlockSpec(shape, lambda i, _n=nd: (0,) * _n, pipeline_mode=pl.Buffered(1))


def _mod_spec(chunk, rowfn):
    return pl.BlockSpec((1, 1, D_MODEL), lambda i: (rowfn(i) * ADA_CHUNKS + chunk, 0, 0))


def _rowfn(latent):
    if latent:
        return lambda i: 1 + (i * TOKEN_TILE) // LAT_LEN
    return lambda i: 0


def _tok_spec(width, blk0=0):
    return pl.BlockSpec((TOKEN_TILE, width), lambda i: (i + blk0, 0))


def _seq_len(latent):
    return LAT_LEN if latent else CTX_LEN


def _transposed_out(t, latent, rows):
    seq = _seq_len(latent)
    shape = jax.ShapeDtypeStruct((t // seq, rows, seq), F32)
    if seq <= TOKEN_TILE:
        per = TOKEN_TILE // seq
        return shape, pl.BlockSpec((per, rows, seq), lambda i: (i, 0, 0))
    per = seq // TOKEN_TILE
    return shape, pl.BlockSpec((1, rows, TOKEN_TILE), lambda i: (i // per, 0, i % per))


def _store_transposed(ref, x):
    n, _, w = ref.shape
    for j in range(n):
        ref[j] = x[j * w:(j + 1) * w].T


def _store_proj_transposed(ref, w_rows, h):
    n, _, w = ref.shape
    for j in range(n):
        ref[j] = _dot_nt(w_rows, h[j * w:(j + 1) * w])


def _modnorm(x, ng_ref, sh_ref, sc_ref):
    return _rms(x, ng_ref[...]) * (1.0 + sc_ref[0]) + sh_ref[0]


def _load_weights_bf16(pairs, stage, sem):
    tr, tc = W_TILE
    tiles = [(src, dst, r, c) for src, dst in pairs
             for r in range(0, src.shape[0], tr) for c in range(0, src.shape[1], tc)]

    def copy(k):
        src, _, r, c = tiles[k]
        return pltpu.make_async_copy(src.at[r:r + tr, c:c + tc], stage.at[k % W_SLOTS], sem.at[k % W_SLOTS])

    for k in range(min(W_SLOTS, len(tiles))):
        copy(k).start()
    for k, (_, dst, r, c) in enumerate(tiles):
        copy(k).wait()
        dst[r:r + tr, c:c + tc] = stage[k % W_SLOTS].astype(BF16)
        if k + W_SLOTS < len(tiles):
            copy(k + W_SLOTS).start()


def _ffn_kernel(*refs, n_ctx, first, has_mix, last):
    it = iter(refs)
    xs = (next(it), next(it)) if first else (next(it),)
    if has_mix:
        yac_ref, ybc_ref, yal_ref, ybl_ref, wo_hbm, g2_ref = (next(it) for _ in range(6))
    ng_ref, sh_ref, sc_ref, g_ref, win_hbm, wout_hbm = (next(it) for _ in range(6))
    if last:
        nf_ref = next(it)
    outs = (next(it), next(it)) if last else (next(it),)
    win_s, wout_s = next(it), next(it)
    if has_mix:
        wo_s = next(it)
    stage, sem = next(it), next(it)

    step = pl.program_id(0)
    is_lat = step >= n_ctx

    @pl.when(step == 0)
    def _():
        pairs = [(win_hbm, win_s), (wout_hbm, wout_s)]
        if has_mix:
            pairs.insert(0, (wo_hbm, wo_s))
        _load_weights_bf16(pairs, stage, sem)

    pick = lambda c_ref, l_ref: jnp.where(is_lat, l_ref[...], c_ref[...])
    x = pick(*xs) if first else xs[0][...]
    if has_mix:
        half = wo_s.shape[0] // 2
        mo = _dot(pick(yac_ref, yal_ref), wo_s[:half, :]) + _dot(pick(ybc_ref, ybl_ref), wo_s[half:, :])
        x = x + g2_ref[0] * mo
    h = _modnorm(x, ng_ref, sh_ref, sc_ref).astype(BF16)
    acc = None
    for lo in range(0, D_FF, FFN_CHUNK):
        hi = min(lo + FFN_CHUNK, D_FF)
        g = _dot(h, win_s[:, lo:hi])
        u = _dot(h, win_s[:, D_FF + lo:D_FF + hi])
        part = _dot(_silu(g) * u, wout_s[lo:hi, :])
        acc = part if acc is None else acc + part
    y = x + (0.5 * g_ref[0]) * acc
    if last:
        y = _rms(y, nf_ref[...])
        oc_ref, ol_ref = outs

        @pl.when(jnp.logical_not(is_lat))
        def _():
            oc_ref[...] = y
            ol_ref[...] = jnp.zeros_like(ol_ref)

        @pl.when(is_lat)
        def _():
            ol_ref[...] = y
    else:
        outs[0][...] = y


def _ffn(x, mods, mod_base, ng, w_in, w_out, *, t_ctx, t_lat, mix=None, final_g=None):
    n_ctx, n_lat = t_ctx // TOKEN_TILE, t_lat // TOKEN_TILE
    first, last = isinstance(x, tuple), final_g is not None
    ctx_map = lambda i: (jnp.minimum(i, n_ctx - 1), 0)
    lat_map = lambda i: (jnp.maximum(i - n_ctx, 0), 0)
    rowfn = lambda i: jnp.where(i < n_ctx, 0, 1 + ((i - n_ctx) * TOKEN_TILE) // LAT_LEN)
    two = lambda wd: [pl.BlockSpec((TOKEN_TILE, wd), ctx_map), pl.BlockSpec((TOKEN_TILE, wd), lat_map)]
    hbm = pl.BlockSpec(memory_space=pl.ANY)
    args = list(x) if first else [x]
    specs = two(D_MODEL) if first else [_tok_spec(D_MODEL)]
    scratch = [pltpu.VMEM(w_in.shape, BF16), pltpu.VMEM(w_out.shape, BF16)]
    if mix is not None:
        yac, ybc, yal, ybl, wo = mix
        args += [yac, ybc, yal, ybl, wo, mods]
        mw = yac.shape[1]
        specs += [pl.BlockSpec((TOKEN_TILE, mw), ctx_map)] * 2 + [pl.BlockSpec((TOKEN_TILE, mw), lat_map)] * 2
        specs += [hbm, _mod_spec(mod_base - 1, rowfn)]
        scratch.append(pltpu.VMEM(wo.shape, BF16))
    args += [ng.reshape(1, D_MODEL), mods, mods, mods, w_in, w_out]
    specs += [_const_spec((1, D_MODEL)), _mod_spec(mod_base, rowfn), _mod_spec(mod_base + 1, rowfn),
              _mod_spec(mod_base + 2, rowfn), hbm, hbm]
    if last:
        args.append(final_g.reshape(1, D_MODEL))
        specs.append(_const_spec((1, D_MODEL)))
        out_shape = [jax.ShapeDtypeStruct((t_ctx, D_MODEL), F32), jax.ShapeDtypeStruct((t_lat, D_MODEL), F32)]
        out_specs = two(D_MODEL)
    else:
        out_shape = jax.ShapeDtypeStruct((t_ctx + t_lat, D_MODEL), F32)
        out_specs = _tok_spec(D_MODEL)
    scratch += [pltpu.VMEM((W_SLOTS,) + W_TILE, F32), pltpu.SemaphoreType.DMA((W_SLOTS,))]
    return pl.pallas_call(
        functools.partial(_ffn_kernel, n_ctx=n_ctx, first=first, has_mix=mix is not None, last=last),
        out_shape=out_shape,
        grid=(n_ctx + n_lat,),
        in_specs=specs,
        out_specs=out_specs,
        scratch_shapes=scratch,
        compiler_params=_params(("arbitrary",), FFN_VMEM_LIMIT),
        name="ffn",
    )(*args)


_EV_QA, _EV_KA, _EV_VA, _EV_QB, _EV_KB, _EV_VB, _EV_OB, _EV_GI, _EV_GF, _EV_END = (
    0, 512, 640, 768, 1280, 1792, 2304, 2816, 2944, 3072)
N_DIRHEAD = 2 * N_HEADS
_WT_VA = 2 * N_DIRHEAD + 512


def _proj_even_kernel(*refs, rope):
    it = iter(refs)
    x_ref, ng_ref, sh_ref, sc_ref, w_ref, wt_ref, gbi_ref, gbf_ref, gbt_ref, qg_ref, kg_ref, bd_ref = (
        next(it) for _ in range(12))
    if rope:
        cos_ref, sin_ref = next(it), next(it)
    qa_ref, kat_ref, vat_ref, qb_ref, kb_ref, vb_ref, ob_ref, gi_ref, gf_ref, gt_ref, kbt_ref = (next(it) for _ in range(11))

    h = _modnorm(x_ref[...], ng_ref, sh_ref, sc_ref).astype(BF16)
    proj = lambda lo, hi: _dot(h, w_ref[:, lo:hi])

    qa = proj(_EV_QA, _EV_KA)
    qa = qa * lax.rsqrt(_head_mean_sq(qa, bd_ref) + EPS) * qg_ref[...]
    ka = proj(_EV_KA, _EV_VA)
    ka = ka * lax.rsqrt(_head_mean_sq(ka, bd_ref) + EPS) * kg_ref[...]
    if rope:
        qa = _rope(qa, cos_ref[...], sin_ref[...])
        ka = _rope(ka, cos_ref[...], sin_ref[...])
    qa_ref[...] = qa * ATTN_SCALE
    _store_transposed(kat_ref, ka)
    _store_proj_transposed(vat_ref, wt_ref[_WT_VA:, :], h)
    qb_ref[...] = proj(_EV_QB, _EV_KB)
    kb_ref[...] = proj(_EV_KB, _EV_VB) * ATTN_SCALE
    vb_ref[...] = proj(_EV_VB, _EV_OB)
    ob_ref[...] = proj(_EV_OB, _EV_GI)
    gi_ref[...] = proj(_EV_GI, _EV_GF) + gbi_ref[...]
    gf_ref[...] = proj(_EV_GF, _EV_END) + gbf_ref[...]
    for j in range(TOKEN_TILE // MLSTM_T):
        ht = h[j * MLSTM_T:(j + 1) * MLSTM_T]
        gt_ref[j] = _dot_nt(wt_ref[:2 * N_DIRHEAD, :], ht) + gbt_ref[...]
        kbt_ref[j] = _dot_nt(wt_ref[2 * N_DIRHEAD:_WT_VA, :], ht) * ATTN_SCALE


def _proj_even(x, t, blk0, mods, latent, ng, w, wt, gbi, gbf, gbt, qg, kg, bd, rope_tabs):
    rowfn = _rowfn(latent)
    args = [x, ng.reshape(1, D_MODEL), mods, mods, w, wt, gbi, gbf, gbt, qg, kg, bd]
    specs = [_tok_spec(D_MODEL, blk0), _const_spec((1, D_MODEL)), _mod_spec(3, rowfn), _mod_spec(4, rowfn)]
    specs += [_const_spec(a.shape) for a in args[4:]]
    if latent:
        nblk = LAT_LEN // TOKEN_TILE
        args += list(rope_tabs)
        specs += [pl.BlockSpec((TOKEN_TILE, 512), lambda i: (i % nblk, 0))] * 2
    widths = (512, 512, 512, 512, 512, 128, 128)
    out_shape = [jax.ShapeDtypeStruct((t, wd), F32) for wd in widths]
    out_specs = [_tok_spec(wd) for wd in widths]
    for pos in (1, 2):
        shape, spec = _transposed_out(t, latent, 128)
        out_shape.insert(pos, shape)
        out_specs.insert(pos, spec)
    nj = TOKEN_TILE // MLSTM_T
    for rows in (2 * N_DIRHEAD, 512):
        out_shape.append(jax.ShapeDtypeStruct((t // MLSTM_T, rows, MLSTM_T), F32))
        out_specs.append(pl.BlockSpec((nj, rows, MLSTM_T), lambda i: (i, 0, 0)))
    return pl.pallas_call(
        functools.partial(_proj_even_kernel, rope=latent),
        out_shape=out_shape, grid=(t // TOKEN_TILE,), in_specs=specs, out_specs=out_specs,
        compiler_params=_params(("arbitrary",)), name="proj_even",
    )(*args)


def _proj_odd_kernel(*refs, rope):
    it = iter(refs)
    x_ref, ng_ref, sh_ref, sc_ref, w_ref = (next(it) for _ in range(5))
    wt_ref = next(it)
    if rope:
        cos_ref, sin_ref = next(it), next(it)
    qc_ref, kc_ref, vc_ref, qd_ref, kd_ref, vd_ref = (next(it) for _ in range(6))

    h = _modnorm(x_ref[...], ng_ref, sh_ref, sc_ref).astype(BF16)
    proj = lambda lo, hi: _dot(h, w_ref[:, lo:hi])
    qc_ref[...] = proj(0, 512) * ATTN_SCALE
    if rope:
        kc_ref[...] = proj(512, 1024)
        vc_ref[...] = proj(1024, 1536)
    else:
        _store_proj_transposed(kc_ref, wt_ref[256:768, :], h)
        _store_proj_transposed(vc_ref, wt_ref[768:, :], h)
    qd = proj(1536, 2048)
    if rope:
        qd = _rope(qd, cos_ref[...], sin_ref[...])
        _store_transposed(kd_ref, _rope(proj(2048, 2176), cos_ref[...], sin_ref[...]))
    else:
        _store_proj_transposed(kd_ref, wt_ref[:128, :], h)
    _store_proj_transposed(vd_ref, wt_ref[128:256, :], h)
    qd_ref[...] = qd * ATTN_SCALE


def _proj_odd(x, t, blk0, mods, latent, ng, w, wt, rope_tabs):
    rowfn = _rowfn(latent)
    args = [x, ng.reshape(1, D_MODEL), mods, mods, w, wt]
    specs = [_tok_spec(D_MODEL, blk0), _const_spec((1, D_MODEL)), _mod_spec(3, rowfn), _mod_spec(4, rowfn),
             _const_spec(w.shape), _const_spec(wt.shape)]
    if latent:
        nblk = LAT_LEN // TOKEN_TILE
        args += list(rope_tabs)
        specs += [pl.BlockSpec((TOKEN_TILE, 512), lambda i: (i % nblk, 0))] * 2
    out_shape = [jax.ShapeDtypeStruct((t, 512), F32)] * 4
    out_specs = [_tok_spec(512)] * 4
    if not latent:
        out_shape[1], out_specs[1] = _transposed_out(t, latent, 512)
        out_shape[2], out_specs[2] = out_shape[1], out_specs[1]
    for _ in range(2):
        shape, spec = _transposed_out(t, latent, 128)
        out_shape.append(shape)
        out_specs.append(spec)
    return pl.pallas_call(
        functools.partial(_proj_odd_kernel, rope=latent),
        out_shape=out_shape,
        grid=(t // TOKEN_TILE,), in_specs=specs, out_specs=out_specs,
        compiler_params=_params(("arbitrary",)), name="proj_odd",
    )(*args)


def _attn_kernel(*refs, group, kv_t, has_extra, has_sink, spb):
    it = iter(refs)
    q_ref, k_ref, v_ref = next(it), next(it), next(it)
    if has_extra:
        ke_ref, ve_ref = next(it), next(it)
    if has_sink:
        sink_ref = next(it)
    o_ref = next(it)
    tq = q_ref.shape[0] // spb
    first = lax.broadcasted_iota(jnp.int32, (tq, 128), 1) < DH

    def kv_operands(kt_ref, vt_ref, smp, kv):
        kt = kt_ref[smp, kv * DH:(kv + 1) * DH, :].astype(BF16)
        vt = vt_ref[smp, kv * DH:(kv + 1) * DH, :].astype(BF16)
        z = jnp.zeros_like(kt)
        ones = jnp.ones((128, kt.shape[1]), BF16)
        return ((jnp.concatenate([kt, z], axis=0), jnp.concatenate([z, kt], axis=0)),
                (jnp.concatenate([vt, z, ones], axis=0), jnp.concatenate([z, vt, ones], axis=0)))

    for smp in range(spb):
        qrows = slice(smp * tq, (smp + 1) * tq)
        cache = {}
        for j in range(N_HEADS // 2):
            ps = slice(128 * j, 128 * (j + 1))
            q_pair = q_ref[qrows, ps]
            if kv_t:
                qb = q_pair.astype(BF16)
                ops = []
                for half in range(2):
                    kv = (2 * j + half) // group
                    if kv not in cache:
                        cache[kv] = (kv_operands(k_ref, v_ref, smp, kv),
                                     kv_operands(ke_ref, ve_ref, smp, kv) if has_extra else None)
                    ops.append(cache[kv])
            else:
                keys = k_ref.shape[0] // spb
                krows = slice(smp * keys, (smp + 1) * keys)
                k_pair = k_ref[krows, ps].astype(BF16)
                v_aug = jnp.concatenate([v_ref[krows, ps].astype(BF16), jnp.ones((keys, 128), BF16)], axis=1)
            res, sink_den = [], []
            for half in range(2):
                if kv_t:
                    (k_ops, v_ops), extra = ops[half]
                    s = jnp.dot(qb, k_ops[half], preferred_element_type=F32)
                else:
                    s = _dot_nt(jnp.where(first, q_pair, 0.0) if half == 0 else jnp.where(first, 0.0, q_pair), k_pair)
                m = jnp.max(s, axis=-1, keepdims=True)
                if has_extra:
                    se = jnp.dot(qb, extra[0][half], preferred_element_type=F32)
                    m = jnp.maximum(m, jnp.max(se, axis=-1, keepdims=True))
                if has_sink:
                    m = jnp.maximum(m, sink_ref[2 * j + half])
                    sink_den.append(jnp.exp(sink_ref[2 * j + half] - m))
                p = jnp.exp(s - m)
                r = _dot_nt(p, v_ops[half]) if kv_t else _dot(p, v_aug)
                if has_extra:
                    r = r + _dot_nt(jnp.exp(se - m), extra[1][half])
                res.append(r)
            num = jnp.where(first, res[0][:, :128], res[1][:, :128])
            den = jnp.where(first, res[0][:, 128:], res[1][:, 128:])
            if has_sink:
                den = den + jnp.where(first, sink_den[0], sink_den[1])
            o_ref[qrows, ps] = num / den


def _attention(q, k, v, *, n_batch, q_len, q_tile, group, kv_t, extra=None, sink=None):
    nq = q_len // q_tile
    spb = next(n for n in (4, 2, 1) if n_batch % n == 0) if nq == 1 else 1
    args = [q, k, v]
    specs = [pl.BlockSpec((spb * q_tile, 512), lambda b, j: (b * nq + j, 0))]
    if kv_t:
        kv_spec = lambda a: pl.BlockSpec((spb,) + a.shape[1:], lambda b, j: (b, 0, 0))
        specs += [kv_spec(k), kv_spec(v)]
        if extra is not None:
            args += list(extra)
            specs += [kv_spec(extra[0]), kv_spec(extra[1])]
    else:
        assert extra is None
        keys = k.shape[0] // n_batch
        specs += [pl.BlockSpec((spb * keys, 512), lambda b, j: (b, 0))] * 2
    if sink is not None:
        args.append(sink)
        specs.append(pl.BlockSpec(memory_space=pltpu.SMEM))
    return pl.pallas_call(
        functools.partial(_attn_kernel, group=group, kv_t=kv_t, has_extra=extra is not None,
                          has_sink=sink is not None, spb=spb),
        out_shape=jax.ShapeDtypeStruct(q.shape, F32),
        grid=(n_batch // spb, nq), in_specs=specs,
        out_specs=pl.BlockSpec((spb * q_tile, 512), lambda b, j: (b * nq + j, 0)),
        compiler_params=_params(("arbitrary", "arbitrary")), name="attn",
    )(*args)


LOG2E = 1.4426950408889634


def _cummax_time(x, reverse):
    n = x.shape[0]
    row = lax.broadcasted_iota(jnp.int32, x.shape, 0)
    k = 1
    while k < n:
        if reverse:
            shifted = jnp.where(row < n - k, pltpu.roll(x, n - k, 0), -jnp.inf)
        else:
            shifted = jnp.where(row >= k, pltpu.roll(x, k, 0), -jnp.inf)
        x = jnp.maximum(x, shifted)
        k *= 2
    return x


def _mlstm_kernel(*refs, n_chunks, carry, spb):
    it = iter(refs)
    q_ref, k_ref, kt_ref, v_ref, ob_ref, gi_ref, gf_ref, gt_ref, hg_ref, bd_ref = (next(it) for _ in range(10))
    if carry:
        c0_ref, n0_ref, m0r_ref, m0c_ref = (next(it) for _ in range(4))
    y_ref = next(it)
    if not carry:
        c_ref, n_ref, m_ref = next(it), next(it), next(it)
    hf_s, hb_s = next(it), next(it)
    if carry:
        st_s, mrow_s, mcol_s = next(it), next(it), next(it)
    T = MLSTM_T
    step = pl.program_id(1)

    if carry:
        @pl.when(step == 0)
        def _():
            z = jnp.zeros((DH, DH), F32)
            for d in range(2):
                for j in range(N_HEADS // 2):
                    ca, cb = c0_ref[0, d, 2 * j], c0_ref[0, d, 2 * j + 1]
                    na, nb = n0_ref[0, d, 2 * j][:, :DH], n0_ref[0, d, 2 * j + 1][:, :DH]
                    st_s[d, j, 0:DH, :] = jnp.concatenate([ca, z, na, z], axis=1)
                    st_s[d, j, DH:2 * DH, :] = jnp.concatenate([z, cb, z, nb], axis=1)
            mrow_s[...] = m0r_ref[0]
            mcol_s[...] = m0c_ref[0]

    row = lax.broadcasted_iota(jnp.int32, (T, T), 0)
    col = lax.broadcasted_iota(jnp.int32, (T, T), 1)
    lower = row >= col
    upper = row <= col
    lower_m = jnp.where(lower, 1.0, 0.0).astype(BF16)
    upper_m = jnp.where(upper, 1.0, 0.0).astype(BF16)
    lane = lax.broadcasted_iota(jnp.int32, (T, 128), 1)
    first = lane < DH
    lane_c = lax.broadcasted_iota(jnp.int32, (DH, 256), 1) % 128
    zeros_kt = jnp.zeros((DH, T), BF16)
    ones_v = jnp.ones((T, 128), BF16)

    for smp, d in ((s_, d_) for s_ in range(spb) for d_ in range(2)):
        chunk = (step if d == 0 else n_chunks - 1 - step) if carry else smp
        rows = pl.ds(pl.multiple_of(chunk * T, T), T)
        hd0 = N_HEADS * d
        mask = lower if d == 0 else upper

        gi, gf = gi_ref[rows, :], gf_ref[rows, :]
        lf = _log_sigmoid(gf)
        b_cols = _dot_exact_rhs(lower_m, lf)
        if d == 1:
            b_cols = b_cols[T - 1:T, :] - b_cols + lf
        a_cols = gi - b_cols
        dmax = b_cols + _cummax_time(a_cols, reverse=d == 1)
        if carry:
            inter = b_cols + mrow_s[0:1, :]
            mt = jnp.maximum(inter, dmax)
            w_inter = jnp.exp(inter - mt)
        else:
            mt = jnp.maximum(b_cols, dmax)
        c2 = (b_cols - mt) * LOG2E
        einv = jnp.exp(-mt)

        gt = gt_ref[chunk]
        gi_t = gt[hd0:hd0 + N_HEADS, :]
        gf_t = gt[N_DIRHEAD + hd0:N_DIRHEAD + hd0 + N_HEADS, :]
        lf_t = _log_sigmoid(gf_t)
        b_rows = _dot_exact_lhs(lf_t, upper_m)
        if d == 1:
            b_rows = b_rows[:, T - 1:T] - b_rows + lf_t
        a_rows = gi_t - b_rows
        a2 = a_rows * LOG2E
        b_last = b_rows[:, T - 1:T] if d == 0 else b_rows[:, 0:1]
        dec_rows = b_last + a_rows
        m_new = jnp.max(dec_rows, axis=-1, keepdims=True)
        if carry:
            m_prev = mcol_s[hd0:hd0 + N_HEADS, 0:1]
            m_new = jnp.maximum(m_new, b_last + m_prev)
            wc = jnp.exp(b_last + m_prev - m_new)
        else:
            m_new = jnp.maximum(m_new, b_last)
        ws_rows = jnp.exp(dec_rows - m_new)

        h_s = hf_s if d == 0 else hb_s
        for j in range(N_HEADS // 2):
            ps = slice(128 * j, 128 * (j + 1))
            q_pair = q_ref[rows, ps].astype(BF16)
            v_aug = jnp.concatenate([v_ref[rows, ps].astype(BF16), ones_v], axis=1)
            kt = (kt_ref[chunk, 128 * j:128 * j + DH, :], kt_ref[chunk, 128 * j + DH:128 * (j + 1), :])
            lhs = []
            for half in range(2):
                hd = 2 * j + half
                kt0 = jnp.concatenate([kt[0].astype(BF16), zeros_kt] if half == 0 else [zeros_kt, kt[1].astype(BF16)], axis=0)
                qk = jnp.dot(q_pair, kt0, preferred_element_type=F32)
                e = c2[:, hd0 + hd:hd0 + hd + 1] + a2[hd:hd + 1, :]
                lhs.append((jnp.exp2(jnp.where(mask, e, -jnp.inf)) * qk).astype(BF16))
            for half in range(2):
                lhs.append((kt[half] * ws_rows[2 * j + half:2 * j + half + 1, :]).astype(BF16))
            both = jnp.dot(jnp.concatenate(lhs, axis=0), v_aug, preferred_element_type=F32)
            res = (both[0:T], both[T:2 * T])
            cu = (both[2 * T:2 * T + DH], both[2 * T + DH:2 * T + 2 * DH])
            ra, rb = hd0 + 2 * j, hd0 + 2 * j + 1
            num = jnp.where(first, res[0][:, :128], res[1][:, :128])
            den = jnp.where(first, res[0][:, 128:], res[1][:, 128:])
            if carry:
                old = st_s[d, j]
                qs = _dot(q_pair, old)
                wi = jnp.where(first, w_inter[:, ra:ra + 1], w_inter[:, rb:rb + 1])
                num = num + wi * qs[:, :128]
                den = den + wi * qs[:, 128:]
            floor = jnp.where(first, einv[:, ra:ra + 1], einv[:, rb:rb + 1])
            h_s[rows, ps] = num / jnp.maximum(jnp.abs(den), floor)
            if carry:
                st_s[d, j, 0:DH, :] = jnp.where(lane_c < DH, cu[0], 0.0) + wc[2 * j:2 * j + 1, :] * old[0:DH]
                st_s[d, j, DH:2 * DH, :] = jnp.where(lane_c >= DH, cu[1], 0.0) + wc[2 * j + 1:2 * j + 2, :] * old[DH:2 * DH]
            else:
                c_ref[smp, d, 2 * j] = cu[0][:, 0:DH]
                c_ref[smp, d, 2 * j + 1] = cu[1][:, DH:2 * DH]

        if carry:
            b_last_l = b_cols[T - 1:T, :] if d == 0 else b_cols[0:1, :]
            m_prev_l = mrow_s[0:1, :]
            m_new_l = jnp.maximum(jnp.max(b_last_l + a_cols, axis=0, keepdims=True), b_last_l + m_prev_l)
            lane_r = lax.broadcasted_iota(jnp.int32, (1, 128), 1)
            mine = (lane_r >= hd0) & (lane_r < hd0 + N_HEADS)
            mrow_s[0:1, :] = jnp.where(mine, m_new_l, m_prev_l)
            mcol_s[hd0:hd0 + N_HEADS, :] = jnp.broadcast_to(m_new, (N_HEADS, 128))
        else:
            n_all = _dot(ws_rows, k_ref[rows, :])
            for hd in range(N_HEADS):
                n_ref[smp, d, hd:hd + 1, :] = n_all[hd:hd + 1, hd * DH:(hd + 1) * DH]
            m_ref[smp, hd0:hd0 + N_HEADS, :] = jnp.broadcast_to(m_new, (N_HEADS, 128))

    @pl.when(step == n_chunks - 1)
    def _():
        hsum = hf_s[...] + hb_s[...]
        y = hsum * lax.rsqrt(_head_mean_sq(hsum, bd_ref) + EPS) * hg_ref[...]
        y_ref[...] = jax.nn.sigmoid(ob_ref[...]) * y


def _mlstm(qb, kb, kbt, vb, ob, gi, gf, gt, head_g, bd, *, n_batch, seq_len, state=None):
    n_chunks = seq_len // MLSTM_T
    carry = state is not None
    assert carry or n_chunks == 1
    spb = 1 if carry or n_batch % 2 else 2
    seq = lambda wd: pl.BlockSpec((spb * seq_len, wd), lambda b, c: (b, 0))
    per_chunk = lambda r: pl.BlockSpec((spb * n_chunks, r, MLSTM_T), lambda b, c: (b, 0, 0))
    args = [qb, kb, kbt, vb, ob, gi, gf, gt, head_g, bd]
    specs = [seq(512), seq(512), per_chunk(512), seq(512), seq(512), seq(128), seq(128), per_chunk(2 * N_DIRHEAD),
             pl.BlockSpec((1, 512), lambda b, c: (0, 0)), pl.BlockSpec(bd.shape, lambda b, c: (0, 0))]
    out_shape = [jax.ShapeDtypeStruct((n_batch * seq_len, 512), F32)]
    out_specs = [seq(512)]
    scratch = [pltpu.VMEM((spb * seq_len, 512), F32), pltpu.VMEM((spb * seq_len, 512), F32)]
    if carry:
        args += list(state)
        specs += [pl.BlockSpec((1, 2, N_HEADS, DH, DH), lambda b, c: (b, 0, 0, 0, 0)),
                  pl.BlockSpec((1, 2, N_HEADS, DH, 128), lambda b, c: (b, 0, 0, 0, 0)),
                  pl.BlockSpec((1, 8, 128), lambda b, c: (b, 0, 0)),
                  pl.BlockSpec((1, N_DIRHEAD, 128), lambda b, c: (b, 0, 0))]
        scratch += [pltpu.VMEM((2, N_HEADS // 2, 128, 256), F32), pltpu.VMEM((8, 128), F32),
                    pltpu.VMEM((N_DIRHEAD, 128), F32)]
    else:
        out_shape += [jax.ShapeDtypeStruct((n_batch, 2, N_HEADS, DH, DH), F32),
                      jax.ShapeDtypeStruct((n_batch, 2, N_HEADS, DH), F32),
                      jax.ShapeDtypeStruct((n_batch, N_DIRHEAD, 128), F32)]
        out_specs += [pl.BlockSpec((spb, 2, N_HEADS, DH, DH), lambda b, c: (b, 0, 0, 0, 0)),
                      pl.BlockSpec((spb, 2, N_HEADS, DH), lambda b, c: (b, 0, 0, 0)),
                      pl.BlockSpec((spb, N_DIRHEAD, 128), lambda b, c: (b, 0, 0))]
    return pl.pallas_call(
        functools.partial(_mlstm_kernel, n_chunks=n_chunks, carry=carry, spb=spb),
        out_shape=out_shape, grid=(n_batch // spb, n_chunks), in_specs=specs, out_specs=out_specs,
        scratch_shapes=scratch,
        compiler_params=_params(("arbitrary", "arbitrary")), name="mlstm",
    )(*args)


N_ROWS = LAT_LEN // GRID_W
N_DY = 2 * NA_KH - 1
N_DX = 2 * NA_KW - 1


NA_QROWS = 4
NA_KROWS = 12
NA_GROUPS = N_ROWS // NA_QROWS


def _na_kernel(q_ref, k_ref, v_ref, ke_ref, ve_ref, rpb_ref, o_ref, tile_s, slab_s):
    pair = pl.program_id(1)
    qn, kn = NA_QROWS * GRID_W, NA_KROWS * GRID_W

    @pl.when(pl.program_id(0) == 0)
    def _():
        qc = lax.broadcasted_iota(jnp.int32, (GRID_W, GRID_W), 0)
        kc = lax.broadcasted_iota(jnp.int32, (GRID_W, GRID_W), 1)
        start = jnp.clip(qc - NA_KW // 2, 0, GRID_W - NA_KW)
        in_win = (kc >= start) & (kc < start + NA_KW)
        dx = kc - qc + (NA_KW - 1)
        blocked = jnp.full((GRID_W, GRID_W), NEG_INF, F32)
        for half in range(2):
            head = 2 * pair + half

            def build_tile(dy, carry):
                tile = blocked
                for j in range(N_DX):
                    tile = jnp.where(dx == j, rpb_ref[(head * N_DY + dy) * N_DX + j], tile)
                tile_s[half * N_DY + dy] = jnp.where(in_win, tile, NEG_INF)
                return carry

            lax.fori_loop(0, N_DY, build_tile, 0)
            for kind in range(3):
                idx = (pair * 2 + half) * 3 + kind
                for i in range(NA_QROWS):
                    for k in range(NA_KROWS):
                        if kind == 0:
                            ok, dy = k < NA_KH, k - i + NA_KH - 1
                        elif kind == 1:
                            ok, dy = i <= k < i + NA_KH, k - i + NA_KH // 2 - 1
                        else:
                            ok, dy = k >= NA_KROWS - NA_KH, k - i - 1
                        slab_s[idx, i * GRID_W:(i + 1) * GRID_W, k * GRID_W:(k + 1) * GRID_W] = (
                            tile_s[half * N_DY + dy] if ok else blocked)

    first = lax.broadcasted_iota(jnp.int32, (qn, 128), 1) < DH
    ones_k = jnp.ones((kn, 128), BF16)
    ke = ke_ref[...].astype(BF16)
    ve_aug = jnp.concatenate([ve_ref[...].astype(BF16), jnp.ones((ke.shape[0], 128), BF16)], axis=1)

    def group_body(g, carry):
        k0 = jnp.where(g < NA_GROUPS // 2, 0, N_ROWS - NA_KROWS)
        kind = jnp.where(g == 0, 0, jnp.where(g == NA_GROUPS - 1, 2, 1))
        qrows = pl.ds(pl.multiple_of(g * qn, qn), qn)
        keys = pl.ds(pl.multiple_of(k0 * GRID_W, GRID_W), kn)
        q_pair = q_ref[qrows, :]
        k_pair = k_ref[keys, :].astype(BF16)
        v_aug = jnp.concatenate([v_ref[keys, :].astype(BF16), ones_k], axis=1)
        res = []
        for half in range(2):
            qm = jnp.where(first, q_pair, 0.0) if half == 0 else jnp.where(first, 0.0, q_pair)
            s = _dot_nt(qm, k_pair) + slab_s[(pair * 2 + half) * 3 + kind]
            se = _dot_nt(qm, ke)
            m = jnp.maximum(jnp.max(s, axis=-1, keepdims=True), jnp.max(se, axis=-1, keepdims=True))
            res.append(_dot(jnp.exp(s - m), v_aug) + _dot(jnp.exp(se - m), ve_aug))
        num = jnp.where(first, res[0][:, :128], res[1][:, :128])
        den = jnp.where(first, res[0][:, 128:], res[1][:, 128:])
        o_ref[qrows, :] = num / den
        return carry

    lax.fori_loop(0, NA_GROUPS, group_body, 0)


def _na_latent(q, k, v, k_ctx, v_ctx, rpb_flat, n_batch):
    seq = pl.BlockSpec((LAT_LEN, 128), lambda b, j: (b, j))
    ctx = pl.BlockSpec((CTX_LEN, 128), lambda b, j: (b, j))
    return pl.pallas_call(
        _na_kernel,
        out_shape=jax.ShapeDtypeStruct(q.shape, F32),
        grid=(n_batch, N_HEADS // 2),
        in_specs=[seq, seq, seq, ctx, ctx, pl.BlockSpec(memory_space=pltpu.SMEM)],
        out_specs=seq,
        scratch_shapes=[pltpu.VMEM((2 * N_DY, GRID_W, GRID_W), F32),
                        pltpu.VMEM((N_HEADS * 3, NA_QROWS * GRID_W, NA_KROWS * GRID_W), F32)],
        compiler_params=_params(("arbitrary", "arbitrary")), name="na_latent",
    )(q, k, v, k_ctx, v_ctx, rpb_flat)


SWA_QT = 256
SWA_SPAN = SWA_QT + 2 * SWA_WIN


def _swa_kernel(q_ref, kt_ref, vt_ref, ket_ref, vet_ref, sink_ref, o_ref):
    pair = pl.program_id(1)
    first = lax.broadcasted_iota(jnp.int32, (SWA_QT, 128), 1) < DH
    row = lax.broadcasted_iota(jnp.int32, (SWA_QT, SWA_SPAN), 0)
    col = lax.broadcasted_iota(jnp.int32, (SWA_QT, SWA_SPAN), 1)
    z = jnp.zeros((DH, SWA_SPAN), BF16)
    ones = jnp.ones((128, SWA_SPAN), BF16)
    ket, vet = ket_ref[0].astype(BF16), vet_ref[0].astype(BF16)
    ze = jnp.zeros_like(ket)
    ones_e = jnp.ones((128, ket.shape[1]), BF16)
    sinks = (sink_ref[2 * pair], sink_ref[2 * pair + 1])
    for g in range(LAT_LEN // SWA_QT):
        q0 = g * SWA_QT
        lo = min(max(q0 - SWA_WIN, 0), LAT_LEN - SWA_SPAN)
        near = jnp.abs((lo + col) - (q0 + row)) <= SWA_WIN
        qb = q_ref[q0:q0 + SWA_QT, :].astype(BF16)
        kt = kt_ref[0, :, lo:lo + SWA_SPAN].astype(BF16)
        vt = vt_ref[0, :, lo:lo + SWA_SPAN].astype(BF16)
        res, sink_den = [], []
        for half in range(2):
            k_op = jnp.concatenate([kt, z] if half == 0 else [z, kt], axis=0)
            v_op = jnp.concatenate([vt, z, ones] if half == 0 else [z, vt, ones], axis=0)
            ke_op = jnp.concatenate([ket, ze] if half == 0 else [ze, ket], axis=0)
            ve_op = jnp.concatenate([vet, ze, ones_e] if half == 0 else [ze, vet, ones_e], axis=0)
            s = jnp.where(near, jnp.dot(qb, k_op, preferred_element_type=F32), NEG_INF)
            se = jnp.dot(qb, ke_op, preferred_element_type=F32)
            m = jnp.maximum(jnp.max(s, axis=-1, keepdims=True), jnp.max(se, axis=-1, keepdims=True))
            m = jnp.maximum(m, sinks[half])
            sink_den.append(jnp.exp(sinks[half] - m))
            res.append(_dot_nt(jnp.exp(s - m), v_op) + _dot_nt(jnp.exp(se - m), ve_op))
        num = jnp.where(first, res[0][:, :128], res[1][:, :128])
        den = jnp.where(first, res[0][:, 128:], res[1][:, 128:]) + jnp.where(first, sink_den[0], sink_den[1])
        o_ref[q0:q0 + SWA_QT, :] = num / den


def _swa_latent(q, kt, vt, kt_ctx, vt_ctx, sink, n_batch, group):
    ppk = group // 2
    kv_spec = lambda a: pl.BlockSpec((1, DH, a.shape[2]), lambda b, j: (b, j // ppk, 0))
    return pl.pallas_call(
        _swa_kernel,
        out_shape=jax.ShapeDtypeStruct(q.shape, F32),
        grid=(n_batch, N_HEADS // 2),
        in_specs=[pl.BlockSpec((LAT_LEN, 128), lambda b, j: (b, j)),
                  kv_spec(kt), kv_spec(vt), kv_spec(kt_ctx), kv_spec(vt_ctx),
                  pl.BlockSpec(memory_space=pltpu.SMEM)],
        out_specs=pl.BlockSpec((LAT_LEN, 128), lambda b, j: (b, j)),
        compiler_params=_params(("arbitrary", "arbitrary")), name="swa_latent",
    )(q, kt, vt, kt_ctx, vt_ctx, sink)


def _rope_tables():
    t = jnp.arange(LAT_LEN)
    pos = jnp.stack([t // GRID_W, t % GRID_W], axis=-1).astype(F32)
    freqs = ROPE_THETA ** (-jnp.arange(ROPE_FREQS, dtype=F32) / ROPE_FREQS)
    ang = (pos[:, :, None] * freqs).reshape(LAT_LEN, 2 * ROPE_FREQS)
    cos, sin = jnp.cos(ang), jnp.sin(ang)
    cos_t = jnp.tile(jnp.concatenate([cos, cos], axis=-1), (1, N_HEADS))
    sin_t = jnp.tile(jnp.concatenate([-sin, sin], axis=-1), (1, N_HEADS))
    return cos_t, sin_t


def kernel(x_prompt, x_sample, cache_l0_attn_k, cache_l0_attn_v, state_l0_mlstm_C, state_l0_mlstm_n, state_l0_mlstm_m, cache_l1_na_k, cache_l1_na_v, cache_l1_swa_k, cache_l1_swa_v, c, c_ctx, norm_final, ada_w_l0, ada_b_l0, norm_l0, ffn1_in_l0, ffn1_out_l0, ffn2_in_l0, ffn2_out_l0, mix_in_l0, mix_out_l0, qk_norm_l0, gate_bias_l0, head_norm_l0, ada_w_l1, ada_b_l1, norm_l1, ffn1_in_l1, ffn1_out_l1, ffn2_in_l1, ffn2_out_l1, mix_in_l1, mix_out_l1, rpb_l1, sink_l1):
    nb, nl = x_prompt.shape[0], x_sample.shape[0]
    bf = lambda a: a.astype(BF16)

    cond8 = jnp.zeros((8, D_MODEL), F32).at[0].set(c_ctx).at[1:1 + nl].set(c)
    mods0 = _adaln(cond8, ada_w_l0, ada_b_l0).reshape(8 * ADA_CHUNKS, 1, D_MODEL)
    mods1 = _adaln(cond8, ada_w_l1, ada_b_l1).reshape(8 * ADA_CHUNKS, 1, D_MODEL)

    g0 = 2304
    gcols = lambda j: mix_in_l0[:, g0 + 8 * j:g0 + 8 * (j + 1)]
    gpad = jnp.zeros((D_MODEL, 128 - N_DIRHEAD), F32)
    w_gi, w_gf = jnp.concatenate([gcols(0), gcols(2)], axis=1), jnp.concatenate([gcols(1), gcols(3)], axis=1)
    w_even = bf(jnp.concatenate([mix_in_l0[:, :g0], mix_in_l0[:, g0 + 32:], w_gi, gpad, w_gf, gpad], axis=1))
    w_even_t = bf(jnp.concatenate([w_gi, w_gf, mix_in_l0[:, 1280:1792], mix_in_l0[:, 640:768]], axis=1).T)
    gb4 = gate_bias_l0.reshape(4, N_HEADS)
    b_gi, b_gf = jnp.concatenate([gb4[0], gb4[2]]), jnp.concatenate([gb4[1], gb4[3]])
    gbi = jnp.zeros((1, 128), F32).at[0, :N_DIRHEAD].set(b_gi)
    gbf = jnp.zeros((1, 128), F32).at[0, :N_DIRHEAD].set(b_gf)
    gbt = jnp.concatenate([b_gi, b_gf]).reshape(2 * N_DIRHEAD, 1)
    qg = jnp.tile(qk_norm_l0[0], N_HEADS).reshape(1, 512)
    kg = jnp.tile(qk_norm_l0[1], 2).reshape(1, 128)
    head_g = head_norm_l0.reshape(1, 512)
    grp = np.arange(256) // DH
    bd = jnp.asarray((grp[:, None] == grp[None, :]).astype(np.float32) / DH, dtype=BF16)
    rope_tabs = _rope_tables()
    w_odd = bf(mix_in_l1)
    w_odd_t = bf(jnp.concatenate([mix_in_l1[:, 2048:2304], mix_in_l1[:, 512:1536]], axis=1).T)
    rpb_flat = rpb_l1.reshape(-1)
    to_t = lambda a: a.reshape(a.shape[0], a.shape[1], -1).transpose(0, 2, 1)
    from_t = lambda a: a.reshape(a.shape[0], -1, DH, a.shape[2]).transpose(0, 3, 1, 2)

    xp = x_prompt.reshape(nb * CTX_LEN, D_MODEL)
    xs = x_sample.reshape(nl * LAT_LEN, D_MODEL)
    t_ctx, t_lat = xp.shape[0], xs.shape[0]
    streams = ((False, t_ctx, 0, nb, CTX_LEN), (True, t_lat, t_ctx // TOKEN_TILE, nl, LAT_LEN))
    ffn = functools.partial(_ffn, t_ctx=t_ctx, t_lat=t_lat)

    x = ffn((xp, xs), mods0, 0, norm_l0[0], ffn1_in_l0, ffn1_out_l0)
    ya, yb = {}, {}
    for latent, t, blk0, n_batch, seq_len in streams:
        qa, kat, vat, qb, kb, vb, ob, gi, gf, gt, kbt = _proj_even(
            x, t, blk0, mods0, latent, norm_l0[1], w_even, w_even_t, gbi, gbf, gbt, qg, kg, bd, rope_tabs)
        if latent:
            extra = (to_t(cache_l0_attn_k), to_t(cache_l0_attn_v))
            m0 = state_l0_mlstm_m.reshape(nl, N_DIRHEAD)
            m0_lanes = jnp.zeros((nl, 8, 128), F32).at[:, 0, :N_DIRHEAD].set(m0)
            m0_rows = jnp.broadcast_to(m0[:, :, None], (nl, N_DIRHEAD, 128))
            n0_cols = jnp.broadcast_to(state_l0_mlstm_n[..., None], (nl, 2, N_HEADS, DH, 128))
            state = (state_l0_mlstm_C, n0_cols, m0_lanes, m0_rows)
        else:
            extra, state = None, None
            k0t, v0t = kat, vat
        ya[latent] = _attention(qa, kat, vat, n_batch=n_batch, q_len=seq_len, q_tile=256, group=4, kv_t=True, extra=extra)
        ml = _mlstm(qb, kb, kbt, vb, ob, gi, gf, gt, head_g, bd, n_batch=n_batch, seq_len=seq_len, state=state)
        yb[latent] = ml[0]
        if not latent:
            c0, n0, m0_pad = ml[1:]
    x = ffn(x, mods0, 6, norm_l0[2], ffn2_in_l0, ffn2_out_l0, mix=(ya[False], yb[False], ya[True], yb[True], mix_out_l0))

    x = ffn(x, mods1, 0, norm_l1[0], ffn1_in_l1, ffn1_out_l1)
    yc, yd = {}, {}
    for latent, t, blk0, n_batch, seq_len in streams:
        qc, kc, vc, qd, kd, vd = _proj_odd(x, t, blk0, mods1, latent, norm_l1[1], w_odd, w_odd_t, rope_tabs)
        if latent:
            yc[latent] = _na_latent(qc, kc, vc, cache_l1_na_k.reshape(nl * CTX_LEN, 512),
                                    cache_l1_na_v.reshape(nl * CTX_LEN, 512), rpb_flat, nl)
            yd[latent] = _swa_latent(qd, kd, vd, to_t(cache_l1_swa_k), to_t(cache_l1_swa_v), sink_l1, nl, group=4)
        else:
            yc[latent] = _attention(qc, kc, vc, n_batch=n_batch, q_len=seq_len, q_tile=256, group=1, kv_t=True)
            yd[latent] = _attention(qd, kd, vd, n_batch=n_batch, q_len=seq_len, q_tile=256, group=4, kv_t=True, sink=sink_l1)
            kc1t, vc1t, kd1t, vd1t = kc, vc, kd, vd
    y_prompt, y_sample = ffn(x, mods1, 6, norm_l1[2], ffn2_in_l1, ffn2_out_l1,
                             mix=(yc[False], yd[False], yc[True], yd[True], mix_out_l1), final_g=norm_final)

    return (y_prompt.reshape(nb, CTX_LEN, D_MODEL), y_sample.reshape(nl, LAT_LEN, D_MODEL),
            from_t(k0t), from_t(v0t),
            c0, n0, m0_pad[:, :, 0].reshape(nb, 2, N_HEADS),
            from_t(kc1t), from_t(vc1t),
            from_t(kd1t), from_t(vd1t))
```

```python
import functools

import jax
import jax.numpy as jnp
import numpy as np
from jax import lax
from jax.experimental import pallas as pl
from jax.experimental.pallas import tpu as pltpu

F32 = jnp.float32
BF16 = jnp.bfloat16

D_MODEL = 1024
DH = 64
D_FF = 2816
ADA_CHUNKS = 9
GRID_W = 64
LAT_LEN = 1024
CTX_LEN = 256
N_HEADS = 8
NA_KH = 8
NA_KW = 16
SWA_WIN = 128
ROPE_THETA = 10000.0
ROPE_FREQS = DH // 4
ATTN_SCALE = DH ** -0.5
NEG_INF = -1e30
EPS = 1e-6

TOKEN_TILE = 512
MLSTM_T = 256
FFN_CHUNK = 512
VMEM_LIMIT = 56 * 1024 * 1024
FFN_VMEM_LIMIT = 60 * 1024 * 1024
W_TILE = (256, 512)
W_SLOTS = 6


def _params(sem, vmem=VMEM_LIMIT):
    return pltpu.CompilerParams(dimension_semantics=sem, vmem_limit_bytes=vmem)


def _dot(a, b):
    return jnp.dot(a.astype(BF16), b.astype(BF16), preferred_element_type=F32)


def _dot_nt(a, b):
    return lax.dot_general(a.astype(BF16), b.astype(BF16), (((1,), (1,)), ((), ())),
                           preferred_element_type=F32)


def _split3(x):
    hi = x.astype(BF16)
    r1 = x - hi.astype(F32)
    mid = r1.astype(BF16)
    lo = (r1 - mid.astype(F32)).astype(BF16)
    return hi, mid, lo


def _dot_exact_rhs(a_bf16, x):
    hi, mid, lo = _split3(x)
    f = lambda p: jnp.dot(a_bf16, p, preferred_element_type=F32)
    return f(hi) + f(mid) + f(lo)


def _dot_exact_lhs(x, b_bf16):
    hi, mid, lo = _split3(x)
    f = lambda p: jnp.dot(p, b_bf16, preferred_element_type=F32)
    return f(hi) + f(mid) + f(lo)


def _silu(x):
    return x * jax.nn.sigmoid(x)


def _log_sigmoid(x):
    return jnp.minimum(x, 0.0) - jnp.log1p(jnp.exp(-jnp.abs(x)))


def _rms(x, g):
    return x * lax.rsqrt(jnp.mean(x * x, axis=-1, keepdims=True) + EPS) * g


def _head_mean_sq(x, bd_ref):
    n = x.shape[-1]
    sq = x * x
    hi = sq.astype(BF16)
    lo = (sq - hi.astype(F32)).astype(BF16)
    w = min(n, bd_ref.shape[0])
    bd = bd_ref[:w, :w]
    f = lambda p: jnp.dot(p, bd, preferred_element_type=F32)
    parts = [f(hi[:, c:c + w]) + f(lo[:, c:c + w]) for c in range(0, n, w)]
    return parts[0] if len(parts) == 1 else jnp.concatenate(parts, axis=1)


def _rope(x, cos, sin_signed):
    n = x.shape[-1]
    lane = lax.broadcasted_iota(jnp.int32, x.shape, 1)
    first_half = (lane % DH) < (DH // 2)
    partner = jnp.where(first_half, pltpu.roll(x, n - DH // 2, 1), pltpu.roll(x, DH // 2, 1))
    return x * cos[:, :n] + partner * sin_signed[:, :n]


def _adaln_kernel(c_ref, w_ref, b_ref, o_ref):
    s = _silu(c_ref[...])
    o_ref[...] = _dot(s, w_ref[...]) + b_ref[...]


def _adaln(cond8, w, b):
    n = w.shape[1]
    tn = 1536
    return pl.pallas_call(
        _adaln_kernel,
        out_shape=jax.ShapeDtypeStruct((8, n), F32),
        grid=(n // tn,),
        in_specs=[pl.BlockSpec((8, D_MODEL), lambda j: (0, 0)),
                  pl.BlockSpec((D_MODEL, tn), lambda j: (0, j)),
                  pl.BlockSpec((1, tn), lambda j: (0, j))],
        out_specs=pl.BlockSpec((8, tn), lambda j: (0, j)),
        compiler_params=_params(("arbitrary",)),
        name="adaln",
    )(cond8, w, b.reshape(1, n))


def _const_spec(shape):
    nd = len(shape)
    return pl.BlockSpec(shape, lambda i, _n=nd: (0,) * _n, pipeline_mode=pl.Buffered(1))


def _mod_spec(chunk, rowfn):
    return pl.BlockSpec((1, 1, D_MODEL), lambda i: (rowfn(i) * ADA_CHUNKS + chunk, 0, 0))


def _rowfn(latent):
    if latent:
        return lambda i: 1 + (i * TOKEN_TILE) // LAT_LEN
    return lambda i: 0


def _tok_spec(width, blk0=0):
    return pl.BlockSpec((TOKEN_TILE, width), lambda i: (i + blk0, 0))


def _seq_len(latent):
    return LAT_LEN if latent else CTX_LEN


def _transposed_out(t, latent, rows):
    seq = _seq_len(latent)
    shape = jax.ShapeDtypeStruct((t // seq, rows, seq), F32)
    if seq <= TOKEN_TILE:
        per = TOKEN_TILE // seq
        return shape, pl.BlockSpec((per, rows, seq), lambda i: (i, 0, 0))
    per = seq // TOKEN_TILE
    return shape, pl.BlockSpec((1, rows, TOKEN_TILE), lambda i: (i // per, 0, i % per))


def _store_transposed(ref, x):
    n, _, w = ref.shape
    for j in range(n):
        ref[j] = x[j * w:(j + 1) * w].T


def _dot_colsT(w_cols, h):
    return lax.dot_general(w_cols, h, (((0,), (1,)), ((), ())), preferred_element_type=F32)


def _store_proj_cols_transposed(ref, w_cols, h):
    n, _, w = ref.shape
    for j in range(n):
        ref[j] = _dot_colsT(w_cols, h[j * w:(j + 1) * w])


def _modnorm(x, ng_ref, sh_ref, sc_ref):
    return _rms(x, ng_ref[...]) * (1.0 + sc_ref[0]) + sh_ref[0]


def _load_weights_bf16(pairs, stage, sem):
    tr, tc = W_TILE
    tiles = [(src, dst, r, c) for src, dst in pairs
             for r in range(0, src.shape[0], tr) for c in range(0, src.shape[1], tc)]

    def copy(k):
        src, _, r, c = tiles[k]
        return pltpu.make_async_copy(src.at[r:r + tr, c:c + tc], stage.at[k % W_SLOTS], sem.at[k % W_SLOTS])

    for k in range(min(W_SLOTS, len(tiles))):
        copy(k).start()
    for k, (_, dst, r, c) in enumerate(tiles):
        copy(k).wait()
        dst[r:r + tr, c:c + tc] = stage[k % W_SLOTS].astype(BF16)
        if k + W_SLOTS < len(tiles):
            copy(k + W_SLOTS).start()


def _ffn_kernel(*refs, n_ctx, first, has_mix, last):
    it = iter(refs)
    xs = (next(it), next(it)) if first else (next(it),)
    if has_mix:
        yac_ref, ybc_ref, yal_ref, ybl_ref, wo_hbm, g2_ref = (next(it) for _ in range(6))
    ng_ref, sh_ref, sc_ref, g_ref, win_hbm, wout_hbm = (next(it) for _ in range(6))
    if last:
        nf_ref = next(it)
    outs = (next(it), next(it)) if last else (next(it),)
    win_s, wout_s = next(it), next(it)
    if has_mix:
        wo_s = next(it)
    stage, sem = next(it), next(it)

    step = pl.program_id(0)
    is_lat = step >= n_ctx

    @pl.when(step == 0)
    def _():
        pairs = [(win_hbm, win_s), (wout_hbm, wout_s)]
        if has_mix:
            pairs.insert(0, (wo_hbm, wo_s))
        _load_weights_bf16(pairs, stage, sem)

    pick = lambda c_ref, l_ref: jnp.where(is_lat, l_ref[...], c_ref[...])
    x = pick(*xs) if first else xs[0][...]
    if has_mix:
        half = wo_s.shape[0] // 2
        mo = _dot(pick(yac_ref, yal_ref), wo_s[:half, :]) + _dot(pick(ybc_ref, ybl_ref), wo_s[half:, :])
        x = x + g2_ref[0] * mo
    h = _modnorm(x, ng_ref, sh_ref, sc_ref).astype(BF16)
    acc = None
    for lo in range(0, D_FF, FFN_CHUNK):
        hi = min(lo + FFN_CHUNK, D_FF)
        g = _dot(h, win_s[:, lo:hi])
        u = _dot(h, win_s[:, D_FF + lo:D_FF + hi])
        part = _dot(_silu(g) * u, wout_s[lo:hi, :])
        acc = part if acc is None else acc + part
    y = x + (0.5 * g_ref[0]) * acc
    if last:
        y = _rms(y, nf_ref[...])
        oc_ref, ol_ref = outs

        @pl.when(jnp.logical_not(is_lat))
        def _():
            oc_ref[...] = y
            ol_ref[...] = jnp.zeros_like(ol_ref)

        @pl.when(is_lat)
        def _():
            ol_ref[...] = y
    else:
        outs[0][...] = y


def _ffn(x, mods, mod_base, ng, w_in, w_out, *, t_ctx, t_lat, mix=None, final_g=None):
    n_ctx, n_lat = t_ctx // TOKEN_TILE, t_lat // TOKEN_TILE
    first, last = isinstance(x, tuple), final_g is not None
    ctx_map = lambda i: (jnp.minimum(i, n_ctx - 1), 0)
    lat_map = lambda i: (jnp.maximum(i - n_ctx, 0), 0)
    rowfn = lambda i: jnp.where(i < n_ctx, 0, 1 + ((i - n_ctx) * TOKEN_TILE) // LAT_LEN)
    two = lambda wd: [pl.BlockSpec((TOKEN_TILE, wd), ctx_map), pl.BlockSpec((TOKEN_TILE, wd), lat_map)]
    hbm = pl.BlockSpec(memory_space=pl.ANY)
    args = list(x) if first else [x]
    specs = two(D_MODEL) if first else [_tok_spec(D_MODEL)]
    scratch = [pltpu.VMEM(w_in.shape, BF16), pltpu.VMEM(w_out.shape, BF16)]
    if mix is not None:
        yac, ybc, yal, ybl, wo = mix
        args += [yac, ybc, yal, ybl, wo, mods]
        mw = yac.shape[1]
        specs += [pl.BlockSpec((TOKEN_TILE, mw), ctx_map)] * 2 + [pl.BlockSpec((TOKEN_TILE, mw), lat_map)] * 2
        specs += [hbm, _mod_spec(mod_base - 1, rowfn)]
        scratch.append(pltpu.VMEM(wo.shape, BF16))
    args += [ng.reshape(1, D_MODEL), mods, mods, mods, w_in, w_out]
    specs += [_const_spec((1, D_MODEL)), _mod_spec(mod_base, rowfn), _mod_spec(mod_base + 1, rowfn),
              _mod_spec(mod_base + 2, rowfn), hbm, hbm]
    if last:
        args.append(final_g.reshape(1, D_MODEL))
        specs.append(_const_spec((1, D_MODEL)))
        out_shape = [jax.ShapeDtypeStruct((t_ctx, D_MODEL), F32), jax.ShapeDtypeStruct((t_lat, D_MODEL), F32)]
        out_specs = two(D_MODEL)
    else:
        out_shape = jax.ShapeDtypeStruct((t_ctx + t_lat, D_MODEL), F32)
        out_specs = _tok_spec(D_MODEL)
    scratch += [pltpu.VMEM((W_SLOTS,) + W_TILE, F32), pltpu.SemaphoreType.DMA((W_SLOTS,))]
    return pl.pallas_call(
        functools.partial(_ffn_kernel, n_ctx=n_ctx, first=first, has_mix=mix is not None, last=last),
        out_shape=out_shape,
        grid=(n_ctx + n_lat,),
        in_specs=specs,
        out_specs=out_specs,
        scratch_shapes=scratch,
        compiler_params=_params(("arbitrary",), FFN_VMEM_LIMIT),
        name="ffn",
    )(*args)


_EV_QA, _EV_KA, _EV_VA, _EV_QB, _EV_KB, _EV_VB, _EV_OB, _EV_GI, _EV_GF, _EV_END = (
    0, 512, 640, 768, 1280, 1792, 2304, 2816, 2944, 3072)
N_DIRHEAD = 2 * N_HEADS


def _proj_even_kernel(*refs, rope):
    it = iter(refs)
    x_ref, ng_ref, sh_ref, sc_ref, w_ref, gbi_ref, gbf_ref, gbt_ref, qg_ref, kg_ref, bd_ref = (
        next(it) for _ in range(11))
    if rope:
        cos_ref, sin_ref = next(it), next(it)
    qa_ref, kat_ref, vat_ref, qb_ref, kb_ref, vb_ref, ob_ref, gi_ref, gf_ref, gt_ref, kbt_ref = (next(it) for _ in range(11))

    h = _modnorm(x_ref[...], ng_ref, sh_ref, sc_ref).astype(BF16)
    proj = lambda lo, hi: _dot(h, w_ref[:, lo:hi])

    qa = proj(_EV_QA, _EV_KA)
    qa = qa * lax.rsqrt(_head_mean_sq(qa, bd_ref) + EPS) * qg_ref[...]
    ka = proj(_EV_KA, _EV_VA)
    ka = ka * lax.rsqrt(_head_mean_sq(ka, bd_ref) + EPS) * kg_ref[...]
    if rope:
        qa = _rope(qa, cos_ref[...], sin_ref[...])
        ka = _rope(ka, cos_ref[...], sin_ref[...])
    qa_ref[...] = qa * ATTN_SCALE
    _store_transposed(kat_ref, ka)
    _store_proj_cols_transposed(vat_ref, w_ref[:, _EV_VA:_EV_QB], h)
    qb_ref[...] = proj(_EV_QB, _EV_KB)
    kb_ref[...] = proj(_EV_KB, _EV_VB) * ATTN_SCALE
    vb_ref[...] = proj(_EV_VB, _EV_OB)
    ob_ref[...] = proj(_EV_OB, _EV_GI)
    gi_ref[...] = proj(_EV_GI, _EV_GF) + gbi_ref[...]
    gf_ref[...] = proj(_EV_GF, _EV_END) + gbf_ref[...]
    for j in range(TOKEN_TILE // MLSTM_T):
        ht = h[j * MLSTM_T:(j + 1) * MLSTM_T]
        gates_t = [_dot_colsT(w_ref[:, c0:c0 + N_DIRHEAD], ht) for c0 in (_EV_GI, _EV_GF)]
        gt_ref[j] = jnp.concatenate(gates_t, axis=0) + gbt_ref[...]
        kbt_ref[j] = _dot_colsT(w_ref[:, _EV_KB:_EV_VB], ht) * ATTN_SCALE


def _proj_even(x, t, blk0, mods, latent, ng, w, gbi, gbf, gbt, qg, kg, bd, rope_tabs):
    rowfn = _rowfn(latent)
    args = [x, ng.reshape(1, D_MODEL), mods, mods, w, gbi, gbf, gbt, qg, kg, bd]
    specs = [_tok_spec(D_MODEL, blk0), _const_spec((1, D_MODEL)), _mod_spec(3, rowfn), _mod_spec(4, rowfn)]
    specs += [_const_spec(a.shape) for a in args[4:]]
    if latent:
        nblk = LAT_LEN // TOKEN_TILE
        args += list(rope_tabs)
        specs += [pl.BlockSpec((TOKEN_TILE, 512), lambda i: (i % nblk, 0))] * 2
    widths = (512, 512, 512, 512, 512, 128, 128)
    out_shape = [jax.ShapeDtypeStruct((t, wd), F32) for wd in widths]
    out_specs = [_tok_spec(wd) for wd in widths]
    for pos in (1, 2):
        shape, spec = _transposed_out(t, latent, 128)
        out_shape.insert(pos, shape)
        out_specs.insert(pos, spec)
    nj = TOKEN_TILE // MLSTM_T
    for rows in (2 * N_DIRHEAD, 512):
        out_shape.append(jax.ShapeDtypeStruct((t // MLSTM_T, rows, MLSTM_T), F32))
        out_specs.append(pl.BlockSpec((nj, rows, MLSTM_T), lambda i: (i, 0, 0)))
    return pl.pallas_call(
        functools.partial(_proj_even_kernel, rope=latent),
        out_shape=out_shape, grid=(t // TOKEN_TILE,), in_specs=specs, out_specs=out_specs,
        compiler_params=_params(("arbitrary",)), name="proj_even",
    )(*args)


def _proj_odd_kernel(*refs, rope):
    it = iter(refs)
    x_ref, ng_ref, sh_ref, sc_ref, w_ref = (next(it) for _ in range(5))
    if rope:
        cos_ref, sin_ref = next(it), next(it)
    qc_ref, kc_ref, vc_ref, qd_ref, kd_ref, vd_ref = (next(it) for _ in range(6))

    h = _modnorm(x_ref[...], ng_ref, sh_ref, sc_ref).astype(BF16)
    proj = lambda lo, hi: _dot(h, w_ref[:, lo:hi])
    qc_ref[...] = proj(0, 512) * ATTN_SCALE
    if rope:
        kc_ref[...] = proj(512, 1024)
        vc_ref[...] = proj(1024, 1536)
    else:
        _store_proj_cols_transposed(kc_ref, w_ref[:, 512:1024], h)
        _store_proj_cols_transposed(vc_ref, w_ref[:, 1024:1536], h)
    qd = proj(1536, 2048)
    if rope:
        qd = _rope(qd, cos_ref[...], sin_ref[...])
        _store_transposed(kd_ref, _rope(proj(2048, 2176), cos_ref[...], sin_ref[...]))
    else:
        _store_proj_cols_transposed(kd_ref, w_ref[:, 2048:2176], h)
    _store_proj_cols_transposed(vd_ref, w_ref[:, 2176:2304], h)
    qd_ref[...] = qd * ATTN_SCALE


def _proj_odd(x, t, blk0, mods, latent, ng, w, rope_tabs):
    rowfn = _rowfn(latent)
    args = [x, ng.reshape(1, D_MODEL), mods, mods, w]
    specs = [_tok_spec(D_MODEL, blk0), _const_spec((1, D_MODEL)), _mod_spec(3, rowfn), _mod_spec(4, rowfn),
             _const_spec(w.shape)]
    if latent:
        nblk = LAT_LEN // TOKEN_TILE
        args += list(rope_tabs)
        specs += [pl.BlockSpec((TOKEN_TILE, 512), lambda i: (i % nblk, 0))] * 2
    out_shape = [jax.ShapeDtypeStruct((t, 512), F32)] * 4
    out_specs = [_tok_spec(512)] * 4
    if not latent:
        out_shape[1], out_specs[1] = _transposed_out(t, latent, 512)
        out_shape[2], out_specs[2] = out_shape[1], out_specs[1]
    for _ in range(2):
        shape, spec = _transposed_out(t, latent, 128)
        out_shape.append(shape)
        out_specs.append(spec)
    return pl.pallas_call(
        functools.partial(_proj_odd_kernel, rope=latent),
        out_shape=out_shape,
        grid=(t // TOKEN_TILE,), in_specs=specs, out_specs=out_specs,
        compiler_params=_params(("arbitrary",)), name="proj_odd",
    )(*args)


def _attn_kernel(*refs, group, kv_t, has_extra, has_sink, spb):
    it = iter(refs)
    q_ref, k_ref, v_ref = next(it), next(it), next(it)
    if has_extra:
        ke_ref, ve_ref = next(it), next(it)
    if has_sink:
        sink_ref = next(it)
    o_ref = next(it)
    tq = q_ref.shape[0] // spb
    first = lax.broadcasted_iota(jnp.int32, (tq, 128), 1) < DH

    def kv_operands(kt_ref, vt_ref, smp, kv):
        kt = kt_ref[smp, kv * DH:(kv + 1) * DH, :].astype(BF16)
        vt = vt_ref[smp, kv * DH:(kv + 1) * DH, :].astype(BF16)
        z = jnp.zeros_like(kt)
        ones = jnp.ones((128, kt.shape[1]), BF16)
        return ((jnp.concatenate([kt, z], axis=0), jnp.concatenate([z, kt], axis=0)),
                (jnp.concatenate([vt, z, ones], axis=0), jnp.concatenate([z, vt, ones], axis=0)))

    for smp in range(spb):
        qrows = slice(smp * tq, (smp + 1) * tq)
        cache = {}
        for j in range(N_HEADS // 2):
            ps = slice(128 * j, 128 * (j + 1))
            q_pair = q_ref[qrows, ps]
            if kv_t:
                qb = q_pair.astype(BF16)
                ops = []
                for half in range(2):
                    kv = (2 * j + half) // group
                    if kv not in cache:
                        cache[kv] = (kv_operands(k_ref, v_ref, smp, kv),
                                     kv_operands(ke_ref, ve_ref, smp, kv) if has_extra else None)
                    ops.append(cache[kv])
            else:
                keys = k_ref.shape[0] // spb
                krows = slice(smp * keys, (smp + 1) * keys)
                k_pair = k_ref[krows, ps].astype(BF16)
                v_aug = jnp.concatenate([v_ref[krows, ps].astype(BF16), jnp.ones((keys, 128), BF16)], axis=1)
            res, sink_den = [], []
            for half in range(2):
                if kv_t:
                    (k_ops, v_ops), extra = ops[half]
                    s = jnp.dot(qb, k_ops[half], preferred_element_type=F32)
                else:
                    s = _dot_nt(jnp.where(first, q_pair, 0.0) if half == 0 else jnp.where(first, 0.0, q_pair), k_pair)
                m = jnp.max(s, axis=-1, keepdims=True)
                if has_extra:
                    se = jnp.dot(qb, extra[0][half], preferred_element_type=F32)
                    m = jnp.maximum(m, jnp.max(se, axis=-1, keepdims=True))
                if has_sink:
                    m = jnp.maximum(m, sink_ref[2 * j + half])
                    sink_den.append(jnp.exp(sink_ref[2 * j + half] - m))
                p = jnp.exp(s - m)
                r = _dot_nt(p, v_ops[half]) if kv_t else _dot(p, v_aug)
                if has_extra:
                    r = r + _dot_nt(jnp.exp(se - m), extra[1][half])
                res.append(r)
            num = jnp.where(first, res[0][:, :128], res[1][:, :128])
            den = jnp.where(first, res[0][:, 128:], res[1][:, 128:])
            if has_sink:
                den = den + jnp.where(first, sink_den[0], sink_den[1])
            o_ref[qrows, ps] = num / den


def _attention(q, k, v, *, n_batch, q_len, q_tile, group, kv_t, extra=None, sink=None):
    nq = q_len // q_tile
    spb = next(n for n in (4, 2, 1) if n_batch % n == 0) if nq == 1 else 1
    args = [q, k, v]
    specs = [pl.BlockSpec((spb * q_tile, 512), lambda b, j: (b * nq + j, 0))]
    if kv_t:
        kv_spec = lambda a: pl.BlockSpec((spb,) + a.shape[1:], lambda b, j: (b, 0, 0))
        specs += [kv_spec(k), kv_spec(v)]
        if extra is not None:
            args += list(extra)
            specs += [kv_spec(extra[0]), kv_spec(extra[1])]
    else:
        assert extra is None
        keys = k.shape[0] // n_batch
        specs += [pl.BlockSpec((spb * keys, 512), lambda b, j: (b, 0))] * 2
    if sink is not None:
        args.append(sink)
        specs.append(pl.BlockSpec(memory_space=pltpu.SMEM))
    return pl.pallas_call(
        functools.partial(_attn_kernel, group=group, kv_t=kv_t, has_extra=extra is not None,
                          has_sink=sink is not None, spb=spb),
        out_shape=jax.ShapeDtypeStruct(q.shape, F32),
        grid=(n_batch // spb, nq), in_specs=specs,
        out_specs=pl.BlockSpec((spb * q_tile, 512), lambda b, j: (b * nq + j, 0)),
        compiler_params=_params(("arbitrary", "arbitrary")), name="attn",
    )(*args)


LOG2E = 1.4426950408889634


def _cummax_time(x, reverse):
    n = x.shape[0]
    row = lax.broadcasted_iota(jnp.int32, x.shape, 0)
    k = 1
    while k < n:
        if reverse:
            shifted = jnp.where(row < n - k, pltpu.roll(x, n - k, 0), -jnp.inf)
        else:
            shifted = jnp.where(row >= k, pltpu.roll(x, k, 0), -jnp.inf)
        x = jnp.maximum(x, shifted)
        k *= 2
    return x


def _mlstm_kernel(*refs, n_chunks, carry, spb):
    it = iter(refs)
    q_ref, k_ref, kt_ref, v_ref, ob_ref, gi_ref, gf_ref, gt_ref, hg_ref, bd_ref = (next(it) for _ in range(10))
    if carry:
        c0_ref, n0_ref, m0r_ref, m0c_ref = (next(it) for _ in range(4))
    y_ref = next(it)
    if not carry:
        c_ref, n_ref, m_ref = next(it), next(it), next(it)
    hf_s, hb_s = next(it), next(it)
    if carry:
        st_s, mrow_s, mcol_s = next(it), next(it), next(it)
    T = MLSTM_T
    step = pl.program_id(1)

    if carry:
        @pl.when(step == 0)
        def _():
            z = jnp.zeros((DH, DH), F32)
            for d in range(2):
                for j in range(N_HEADS // 2):
                    ca, cb = c0_ref[0, d, 2 * j], c0_ref[0, d, 2 * j + 1]
                    na, nb = n0_ref[0, d, 2 * j][:, :DH], n0_ref[0, d, 2 * j + 1][:, :DH]
                    st_s[d, j, 0:DH, :] = jnp.concatenate([ca, z, na, z], axis=1)
                    st_s[d, j, DH:2 * DH, :] = jnp.concatenate([z, cb, z, nb], axis=1)
            mrow_s[...] = m0r_ref[0]
            mcol_s[...] = m0c_ref[0]

    row = lax.broadcasted_iota(jnp.int32, (T, T), 0)
    col = lax.broadcasted_iota(jnp.int32, (T, T), 1)
    lower = row >= col
    upper = row <= col
    lower_m = jnp.where(lower, 1.0, 0.0).astype(BF16)
    upper_m = jnp.where(upper, 1.0, 0.0).astype(BF16)
    lane = lax.broadcasted_iota(jnp.int32, (T, 128), 1)
    first = lane < DH
    lane_c = lax.broadcasted_iota(jnp.int32, (DH, 256), 1) % 128
    zeros_kt = jnp.zeros((DH, T), BF16)
    ones_v = jnp.ones((T, 128), BF16)

    for smp, d in ((s_, d_) for s_ in range(spb) for d_ in range(2)):
        chunk = (step if d == 0 else n_chunks - 1 - step) if carry else smp
        rows = pl.ds(pl.multiple_of(chunk * T, T), T)
        hd0 = N_HEADS * d
        mask = lower if d == 0 else upper

        gi, gf = gi_ref[rows, :], gf_ref[rows, :]
        lf = _log_sigmoid(gf)
        b_cols = _dot_exact_rhs(lower_m, lf)
        if d == 1:
            b_cols = b_cols[T - 1:T, :] - b_cols + lf
        a_cols = gi - b_cols
        dmax = b_cols + _cummax_time(a_cols, reverse=d == 1)
        if carry:
            inter = b_cols + mrow_s[0:1, :]
            mt = jnp.maximum(inter, dmax)
            w_inter = jnp.exp(inter - mt)
        else:
            mt = jnp.maximum(b_cols, dmax)
        c2 = (b_cols - mt) * LOG2E
        einv = jnp.exp(-mt)

        gt = gt_ref[chunk]
        gi_t = gt[hd0:hd0 + N_HEADS, :]
        gf_t = gt[N_DIRHEAD + hd0:N_DIRHEAD + hd0 + N_HEADS, :]
        lf_t = _log_sigmoid(gf_t)
        b_rows = _dot_exact_lhs(lf_t, upper_m)
        if d == 1:
            b_rows = b_rows[:, T - 1:T] - b_rows + lf_t
        a_rows = gi_t - b_rows
        a2 = a_rows * LOG2E
        b_last = b_rows[:, T - 1:T] if d == 0 else b_rows[:, 0:1]
        dec_rows = b_last + a_rows
        m_new = jnp.max(dec_rows, axis=-1, keepdims=True)
        if carry:
            m_prev = mcol_s[hd0:hd0 + N_HEADS, 0:1]
            m_new = jnp.maximum(m_new, b_last + m_prev)
            wc = jnp.exp(b_last + m_prev - m_new)
        else:
            m_new = jnp.maximum(m_new, b_last)
        ws_rows = jnp.exp(dec_rows - m_new)

        h_s = hf_s if d == 0 else hb_s
        for j in range(N_HEADS // 2):
            ps = slice(128 * j, 128 * (j + 1))
            q_pair = q_ref[rows, ps].astype(BF16)
            v_aug = jnp.concatenate([v_ref[rows, ps].astype(BF16), ones_v], axis=1)
            kt = (kt_ref[chunk, 128 * j:128 * j + DH, :], kt_ref[chunk, 128 * j + DH:128 * (j + 1), :])
            lhs = []
            for half in range(2):
                hd = 2 * j + half
                kt0 = jnp.concatenate([kt[0].astype(BF16), zeros_kt] if half == 0 else [zeros_kt, kt[1].astype(BF16)], axis=0)
                qk = jnp.dot(q_pair, kt0, preferred_element_type=F32)
                e = c2[:, hd0 + hd:hd0 + hd + 1] + a2[hd:hd + 1, :]
                lhs.append((jnp.exp2(jnp.where(mask, e, -jnp.inf)) * qk).astype(BF16))
            for half in range(2):
                lhs.append((kt[half] * ws_rows[2 * j + half:2 * j + half + 1, :]).astype(BF16))
            both = jnp.dot(jnp.concatenate(lhs, axis=0), v_aug, preferred_element_type=F32)
            res = (both[0:T], both[T:2 * T])
            cu = (both[2 * T:2 * T + DH], both[2 * T + DH:2 * T + 2 * DH])
            ra, rb = hd0 + 2 * j, hd0 + 2 * j + 1
            num = jnp.where(first, res[0][:, :128], res[1][:, :128])
            den = jnp.where(first, res[0][:, 128:], res[1][:, 128:])
            if carry:
                old = st_s[d, j]
                qs = _dot(q_pair, old)
                wi = jnp.where(first, w_inter[:, ra:ra + 1], w_inter[:, rb:rb + 1])
                num = num + wi * qs[:, :128]
                den = den + wi * qs[:, 128:]
            floor = jnp.where(first, einv[:, ra:ra + 1], einv[:, rb:rb + 1])
            h_s[rows, ps] = num / jnp.maximum(jnp.abs(den), floor)
            if carry:
                st_s[d, j, 0:DH, :] = jnp.where(lane_c < DH, cu[0], 0.0) + wc[2 * j:2 * j + 1, :] * old[0:DH]
                st_s[d, j, DH:2 * DH, :] = jnp.where(lane_c >= DH, cu[1], 0.0) + wc[2 * j + 1:2 * j + 2, :] * old[DH:2 * DH]
            else:
                c_ref[smp, d, 2 * j] = cu[0][:, 0:DH]
                c_ref[smp, d, 2 * j + 1] = cu[1][:, DH:2 * DH]

        if carry:
            b_last_l = b_cols[T - 1:T, :] if d == 0 else b_cols[0:1, :]
            m_prev_l = mrow_s[0:1, :]
            m_new_l = jnp.maximum(jnp.max(b_last_l + a_cols, axis=0, keepdims=True), b_last_l + m_prev_l)
            lane_r = lax.broadcasted_iota(jnp.int32, (1, 128), 1)
            mine = (lane_r >= hd0) & (lane_r < hd0 + N_HEADS)
            mrow_s[0:1, :] = jnp.where(mine, m_new_l, m_prev_l)
            mcol_s[hd0:hd0 + N_HEADS, :] = jnp.broadcast_to(m_new, (N_HEADS, 128))
        else:
            n_all = _dot(ws_rows, k_ref[rows, :])
            for hd in range(N_HEADS):
                n_ref[smp, d, hd:hd + 1, :] = n_all[hd:hd + 1, hd * DH:(hd + 1) * DH]
            m_ref[smp, hd0:hd0 + N_HEADS, :] = jnp.broadcast_to(m_new, (N_HEADS, 128))

    @pl.when(step == n_chunks - 1)
    def _():
        hsum = hf_s[...] + hb_s[...]
        y = hsum * lax.rsqrt(_head_mean_sq(hsum, bd_ref) + EPS) * hg_ref[...]
        y_ref[...] = jax.nn.sigmoid(ob_ref[...]) * y


def _mlstm(qb, kb, kbt, vb, ob, gi, gf, gt, head_g, bd, *, n_batch, seq_len, state=None):
    n_chunks = seq_len // MLSTM_T
    carry = state is not None
    assert carry or n_chunks == 1
    spb = 1
    seq = lambda wd: pl.BlockSpec((spb * seq_len, wd), lambda b, c: (b, 0))
    per_chunk = lambda r: pl.BlockSpec((spb * n_chunks, r, MLSTM_T), lambda b, c: (b, 0, 0))
    args = [qb, kb, kbt, vb, ob, gi, gf, gt, head_g, bd]
    specs = [seq(512), seq(512), per_chunk(512), seq(512), seq(512), seq(128), seq(128), per_chunk(2 * N_DIRHEAD),
             pl.BlockSpec((1, 512), lambda b, c: (0, 0)), pl.BlockSpec(bd.shape, lambda b, c: (0, 0))]
    out_shape = [jax.ShapeDtypeStruct((n_batch * seq_len, 512), F32)]
    out_specs = [seq(512)]
    scratch = [pltpu.VMEM((spb * seq_len, 512), F32), pltpu.VMEM((spb * seq_len, 512), F32)]
    if carry:
        args += list(state)
        specs += [pl.BlockSpec((1, 2, N_HEADS, DH, DH), lambda b, c: (b, 0, 0, 0, 0)),
                  pl.BlockSpec((1, 2, N_HEADS, DH, 128), lambda b, c: (b, 0, 0, 0, 0)),
                  pl.BlockSpec((1, 8, 128), lambda b, c: (b, 0, 0)),
                  pl.BlockSpec((1, N_DIRHEAD, 128), lambda b, c: (b, 0, 0))]
        scratch += [pltpu.VMEM((2, N_HEADS // 2, 128, 256), F32), pltpu.VMEM((8, 128), F32),
                    pltpu.VMEM((N_DIRHEAD, 128), F32)]
    else:
        out_shape += [jax.ShapeDtypeStruct((n_batch, 2, N_HEADS, DH, DH), F32),
                      jax.ShapeDtypeStruct((n_batch, 2, N_HEADS, DH), F32),
                      jax.ShapeDtypeStruct((n_batch, N_DIRHEAD, 128), F32)]
        out_specs += [pl.BlockSpec((spb, 2, N_HEADS, DH, DH), lambda b, c: (b, 0, 0, 0, 0)),
                      pl.BlockSpec((spb, 2, N_HEADS, DH), lambda b, c: (b, 0, 0, 0)),
                      pl.BlockSpec((spb, N_DIRHEAD, 128), lambda b, c: (b, 0, 0))]
    return pl.pallas_call(
        functools.partial(_mlstm_kernel, n_chunks=n_chunks, carry=carry, spb=spb),
        out_shape=out_shape, grid=(n_batch // spb, n_chunks), in_specs=specs, out_specs=out_specs,
        scratch_shapes=scratch,
        compiler_params=_params(("arbitrary", "arbitrary")), name="mlstm",
    )(*args)


N_ROWS = LAT_LEN // GRID_W
N_DY = 2 * NA_KH - 1
N_DX = 2 * NA_KW - 1


NA_QROWS = 4
NA_KROWS = 12
NA_GROUPS = N_ROWS // NA_QROWS


def _na_kernel(q_ref, k_ref, v_ref, ke_ref, ve_ref, rpb_ref, o_ref, tile_s, slab_s):
    pair = pl.program_id(1)
    qn, kn = NA_QROWS * GRID_W, NA_KROWS * GRID_W

    @pl.when(pl.program_id(0) == 0)
    def _():
        qc = lax.broadcasted_iota(jnp.int32, (GRID_W, GRID_W), 0)
        kc = lax.broadcasted_iota(jnp.int32, (GRID_W, GRID_W), 1)
        start = jnp.clip(qc - NA_KW // 2, 0, GRID_W - NA_KW)
        in_win = (kc >= start) & (kc < start + NA_KW)
        dx = kc - qc + (NA_KW - 1)
        blocked = jnp.full((GRID_W, GRID_W), NEG_INF, F32)
        for half in range(2):
            head = 2 * pair + half

            def build_tile(dy, carry):
                tile = blocked
                for j in range(N_DX):
                    tile = jnp.where(dx == j, rpb_ref[(head * N_DY + dy) * N_DX + j], tile)
                tile_s[half * N_DY + dy] = jnp.where(in_win, tile, NEG_INF)
                return carry

            lax.fori_loop(0, N_DY, build_tile, 0)
            for kind in range(3):
                idx = (pair * 2 + half) * 3 + kind
                for i in range(NA_QROWS):
                    for k in range(NA_KROWS):
                        if kind == 0:
                            ok, dy = k < NA_KH, k - i + NA_KH - 1
                        elif kind == 1:
                            ok, dy = i <= k < i + NA_KH, k - i + NA_KH // 2 - 1
                        else:
                            ok, dy = k >= NA_KROWS - NA_KH, k - i - 1
                        slab_s[idx, i * GRID_W:(i + 1) * GRID_W, k * GRID_W:(k + 1) * GRID_W] = (
                            tile_s[half * N_DY + dy] if ok else blocked)

    first = lax.broadcasted_iota(jnp.int32, (qn, 128), 1) < DH
    ones_k = jnp.ones((kn, 128), BF16)
    ke = ke_ref[...].astype(BF16)
    ve_aug = jnp.concatenate([ve_ref[...].astype(BF16), jnp.ones((ke.shape[0], 128), BF16)], axis=1)

    def group_body(g, carry):
        k0 = jnp.where(g < NA_GROUPS // 2, 0, N_ROWS - NA_KROWS)
        kind = jnp.where(g == 0, 0, jnp.where(g == NA_GROUPS - 1, 2, 1))
        qrows = pl.ds(pl.multiple_of(g * qn, qn), qn)
        keys = pl.ds(pl.multiple_of(k0 * GRID_W, GRID_W), kn)
        q_pair = q_ref[qrows, :]
        k_pair = k_ref[keys, :].astype(BF16)
        v_aug = jnp.concatenate([v_ref[keys, :].astype(BF16), ones_k], axis=1)
        res = []
        for half in range(2):
            qm = jnp.where(first, q_pair, 0.0) if half == 0 else jnp.where(first, 0.0, q_pair)
            s = _dot_nt(qm, k_pair) + slab_s[(pair * 2 + half) * 3 + kind]
            se = _dot_nt(qm, ke)
            m = jnp.maximum(jnp.max(s, axis=-1, keepdims=True), jnp.max(se, axis=-1, keepdims=True))
            res.append(_dot(jnp.exp(s - m), v_aug) + _dot(jnp.exp(se - m), ve_aug))
        num = jnp.where(first, res[0][:, :128], res[1][:, :128])
        den = jnp.where(first, res[0][:, 128:], res[1][:, 128:])
        o_ref[qrows, :] = num / den
        return carry

    lax.fori_loop(0, NA_GROUPS, group_body, 0)


def _na_latent(q, k, v, k_ctx, v_ctx, rpb_flat, n_batch):
    seq = pl.BlockSpec((LAT_LEN, 128), lambda b, j: (b, j))
    ctx = pl.BlockSpec((CTX_LEN, 128), lambda b, j: (b, j))
    return pl.pallas_call(
        _na_kernel,
        out_shape=jax.ShapeDtypeStruct(q.shape, F32),
        grid=(n_batch, N_HEADS // 2),
        in_specs=[seq, seq, seq, ctx, ctx, pl.BlockSpec(memory_space=pltpu.SMEM)],
        out_specs=seq,
        scratch_shapes=[pltpu.VMEM((2 * N_DY, GRID_W, GRID_W), F32),
                        pltpu.VMEM((N_HEADS * 3, NA_QROWS * GRID_W, NA_KROWS * GRID_W), F32)],
        compiler_params=_params(("arbitrary", "arbitrary")), name="na_latent",
    )(q, k, v, k_ctx, v_ctx, rpb_flat)


SWA_QT = 256
SWA_SPAN = SWA_QT + 2 * SWA_WIN


def _swa_kernel(q_ref, kt_ref, vt_ref, ket_ref, vet_ref, sink_ref, o_ref):
    pair = pl.program_id(1)
    first = lax.broadcasted_iota(jnp.int32, (SWA_QT, 128), 1) < DH
    row = lax.broadcasted_iota(jnp.int32, (SWA_QT, SWA_SPAN), 0)
    col = lax.broadcasted_iota(jnp.int32, (SWA_QT, SWA_SPAN), 1)
    z = jnp.zeros((DH, SWA_SPAN), BF16)
    ones = jnp.ones((128, SWA_SPAN), BF16)
    ket, vet = ket_ref[0].astype(BF16), vet_ref[0].astype(BF16)
    ze = jnp.zeros_like(ket)
    ones_e = jnp.ones((128, ket.shape[1]), BF16)
    sinks = (sink_ref[2 * pair], sink_ref[2 * pair + 1])
    for g in range(LAT_LEN // SWA_QT):
        q0 = g * SWA_QT
        lo = min(max(q0 - SWA_WIN, 0), LAT_LEN - SWA_SPAN)
        near = jnp.abs((lo + col) - (q0 + row)) <= SWA_WIN
        qb = q_ref[q0:q0 + SWA_QT, :].astype(BF16)
        kt = kt_ref[0, :, lo:lo + SWA_SPAN].astype(BF16)
        vt = vt_ref[0, :, lo:lo + SWA_SPAN].astype(BF16)
        res, sink_den = [], []
        for half in range(2):
            k_op = jnp.concatenate([kt, z] if half == 0 else [z, kt], axis=0)
            v_op = jnp.concatenate([vt, z, ones] if half == 0 else [z, vt, ones], axis=0)
            ke_op = jnp.concatenate([ket, ze] if half == 0 else [ze, ket], axis=0)
            ve_op = jnp.concatenate([vet, ze, ones_e] if half == 0 else [ze, vet, ones_e], axis=0)
            s = jnp.where(near, jnp.dot(qb, k_op, preferred_element_type=F32), NEG_INF)
            se = jnp.dot(qb, ke_op, preferred_element_type=F32)
            m = jnp.maximum(jnp.max(s, axis=-1, keepdims=True), jnp.max(se, axis=-1, keepdims=True))
            m = jnp.maximum(m, sinks[half])
            sink_den.append(jnp.exp(sinks[half] - m))
            res.append(_dot_nt(jnp.exp(s - m), v_op) + _dot_nt(jnp.exp(se - m), ve_op))
        num = jnp.where(first, res[0][:, :128], res[1][:, :128])
        den = jnp.where(first, res[0][:, 128:], res[1][:, 128:]) + jnp.where(first, sink_den[0], sink_den[1])
        o_ref[q0:q0 + SWA_QT, :] = num / den


def _swa_latent(q, kt, vt, kt_ctx, vt_ctx, sink, n_batch, group):
    ppk = group // 2
    kv_spec = lambda a: pl.BlockSpec((1, DH, a.shape[2]), lambda b, j: (b, j // ppk, 0))
    return pl.pallas_call(
        _swa_kernel,
        out_shape=jax.ShapeDtypeStruct(q.shape, F32),
        grid=(n_batch, N_HEADS // 2),
        in_specs=[pl.BlockSpec((LAT_LEN, 128), lambda b, j: (b, j)),
                  kv_spec(kt), kv_spec(vt), kv_spec(kt_ctx), kv_spec(vt_ctx),
                  pl.BlockSpec(memory_space=pltpu.SMEM)],
        out_specs=pl.BlockSpec((LAT_LEN, 128), lambda b, j: (b, j)),
        compiler_params=_params(("arbitrary", "arbitrary")), name="swa_latent",
    )(q, kt, vt, kt_ctx, vt_ctx, sink)


def _rope_tables():
    t = jnp.arange(LAT_LEN)
    pos = jnp.stack([t // GRID_W, t % GRID_W], axis=-1).astype(F32)
    freqs = ROPE_THETA ** (-jnp.arange(ROPE_FREQS, dtype=F32) / ROPE_FREQS)
    ang = (pos[:, :, None] * freqs).reshape(LAT_LEN, 2 * ROPE_FREQS)
    cos, sin = jnp.cos(ang), jnp.sin(ang)
    cos_t = jnp.tile(jnp.concatenate([cos, cos], axis=-1), (1, N_HEADS))
    sin_t = jnp.tile(jnp.concatenate([-sin, sin], axis=-1), (1, N_HEADS))
    return cos_t, sin_t


def kernel(x_prompt, x_sample, cache_l0_attn_k, cache_l0_attn_v, state_l0_mlstm_C, state_l0_mlstm_n, state_l0_mlstm_m, cache_l1_na_k, cache_l1_na_v, cache_l1_swa_k, cache_l1_swa_v, c, c_ctx, norm_final, ada_w_l0, ada_b_l0, norm_l0, ffn1_in_l0, ffn1_out_l0, ffn2_in_l0, ffn2_out_l0, mix_in_l0, mix_out_l0, qk_norm_l0, gate_bias_l0, head_norm_l0, ada_w_l1, ada_b_l1, norm_l1, ffn1_in_l1, ffn1_out_l1, ffn2_in_l1, ffn2_out_l1, mix_in_l1, mix_out_l1, rpb_l1, sink_l1):
    nb, nl = x_prompt.shape[0], x_sample.shape[0]
    bf = lambda a: a.astype(BF16)

    cond8 = jnp.zeros((8, D_MODEL), F32).at[0].set(c_ctx).at[1:1 + nl].set(c)
    mods0 = _adaln(cond8, ada_w_l0, ada_b_l0).reshape(8 * ADA_CHUNKS, 1, D_MODEL)
    mods1 = _adaln(cond8, ada_w_l1, ada_b_l1).reshape(8 * ADA_CHUNKS, 1, D_MODEL)

    g0 = 2304
    gcols = lambda j: mix_in_l0[:, g0 + 8 * j:g0 + 8 * (j + 1)]
    gpad = jnp.zeros((D_MODEL, 128 - N_DIRHEAD), F32)
    w_gi, w_gf = jnp.concatenate([gcols(0), gcols(2)], axis=1), jnp.concatenate([gcols(1), gcols(3)], axis=1)
    w_even = bf(jnp.concatenate([mix_in_l0[:, :g0], mix_in_l0[:, g0 + 32:], w_gi, gpad, w_gf, gpad], axis=1))
    gb4 = gate_bias_l0.reshape(4, N_HEADS)
    b_gi, b_gf = jnp.concatenate([gb4[0], gb4[2]]), jnp.concatenate([gb4[1], gb4[3]])
    gbi = jnp.zeros((1, 128), F32).at[0, :N_DIRHEAD].set(b_gi)
    gbf = jnp.zeros((1, 128), F32).at[0, :N_DIRHEAD].set(b_gf)
    gbt = jnp.concatenate([b_gi, b_gf]).reshape(2 * N_DIRHEAD, 1)
    qg = jnp.tile(qk_norm_l0[0], N_HEADS).reshape(1, 512)
    kg = jnp.tile(qk_norm_l0[1], 2).reshape(1, 128)
    head_g = head_norm_l0.reshape(1, 512)
    grp = np.arange(256) // DH
    bd = jnp.asarray((grp[:, None] == grp[None, :]).astype(np.float32) / DH, dtype=BF16)
    rope_tabs = _rope_tables()
    w_odd = bf(mix_in_l1)
    rpb_flat = rpb_l1.reshape(-1)
    to_t = lambda a: a.reshape(a.shape[0], a.shape[1], -1).transpose(0, 2, 1)
    from_t = lambda a: a.reshape(a.shape[0], -1, DH, a.shape[2]).transpose(0, 3, 1, 2)

    xp = x_prompt.reshape(nb * CTX_LEN, D_MODEL)
    xs = x_sample.reshape(nl * LAT_LEN, D_MODEL)
    t_ctx, t_lat = xp.shape[0], xs.shape[0]
    streams = ((False, t_ctx, 0, nb, CTX_LEN), (True, t_lat, t_ctx // TOKEN_TILE, nl, LAT_LEN))
    ffn = functools.partial(_ffn, t_ctx=t_ctx, t_lat=t_lat)

    x = ffn((xp, xs), mods0, 0, norm_l0[0], ffn1_in_l0, ffn1_out_l0)
    ya, yb = {}, {}
    for latent, t, blk0, n_batch, seq_len in streams:
        qa, kat, vat, qb, kb, vb, ob, gi, gf, gt, kbt = _proj_even(
            x, t, blk0, mods0, latent, norm_l0[1], w_even, gbi, gbf, gbt, qg, kg, bd, rope_tabs)
        if latent:
            extra = (to_t(cache_l0_attn_k), to_t(cache_l0_attn_v))
            m0 = state_l0_mlstm_m.reshape(nl, N_DIRHEAD)
            m0_lanes = jnp.zeros((nl, 8, 128), F32).at[:, 0, :N_DIRHEAD].set(m0)
            m0_rows = jnp.broadcast_to(m0[:, :, None], (nl, N_DIRHEAD, 128))
            n0_cols = jnp.broadcast_to(state_l0_mlstm_n[..., None], (nl, 2, N_HEADS, DH, 128))
            state = (state_l0_mlstm_C, n0_cols, m0_lanes, m0_rows)
        else:
            extra, state = None, None
            k0t, v0t = kat, vat
        ya[latent] = _attention(qa, kat, vat, n_batch=n_batch, q_len=seq_len, q_tile=256, group=4, kv_t=True, extra=extra)
        ml = _mlstm(qb, kb, kbt, vb, ob, gi, gf, gt, head_g, bd, n_batch=n_batch, seq_len=seq_len, state=state)
        yb[latent] = ml[0]
        if not latent:
            c0, n0, m0_pad = ml[1:]
    x = ffn(x, mods0, 6, norm_l0[2], ffn2_in_l0, ffn2_out_l0, mix=(ya[False], yb[False], ya[True], yb[True], mix_out_l0))

    x = ffn(x, mods1, 0, norm_l1[0], ffn1_in_l1, ffn1_out_l1)
    yc, yd = {}, {}
    for latent, t, blk0, n_batch, seq_len in streams:
        qc, kc, vc, qd, kd, vd = _proj_odd(x, t, blk0, mods1, latent, norm_l1[1], w_odd, rope_tabs)
        if latent:
            yc[latent] = _na_latent(qc, kc, vc, cache_l1_na_k.reshape(nl * CTX_LEN, 512),
                                    cache_l1_na_v.reshape(nl * CTX_LEN, 512), rpb_flat, nl)
            yd[latent] = _swa_latent(qd, kd, vd, to_t(cache_l1_swa_k), to_t(cache_l1_swa_v), sink_l1, nl, group=4)
        else:
            yc[latent] = _attention(qc, kc, vc, n_batch=n_batch, q_len=seq_len, q_tile=256, group=1, kv_t=True)
            yd[latent] = _attention(qd, kd, vd, n_batch=n_batch, q_len=seq_len, q_tile=256, group=4, kv_t=True, sink=sink_l1)
            kc1t, vc1t, kd1t, vd1t = kc, vc, kd, vd
    y_prompt, y_sample = ffn(x, mods1, 6, norm_l1[2], ffn2_in_l1, ffn2_out_l1,
                             mix=(yc[False], yd[False], yc[True], yd[True], mix_out_l1), final_g=norm_final)

    return (y_prompt.reshape(nb, CTX_LEN, D_MODEL), y_sample.reshape(nl, LAT_LEN, D_MODEL),
            from_t(k0t), from_t(v0t),
            c0, n0, m0_pad[:, :, 0].reshape(nb, 2, N_HEADS),
            from_t(kc1t), from_t(vc1t),
            from_t(kd1t), from_t(vd1t))
```

```python
import functools

import jax
import jax.numpy as jnp
import numpy as np
from jax import lax
from jax.experimental import pallas as pl
from jax.experimental.pallas import tpu as pltpu

F32 = jnp.float32
BF16 = jnp.bfloat16

D_MODEL = 1024
DH = 64
D_FF = 2816
ADA_CHUNKS = 9
GRID_W = 64
LAT_LEN = 1024
CTX_LEN = 256
N_HEADS = 8
NA_KH = 8
NA_KW = 16
SWA_WIN = 128
ROPE_THETA = 10000.0
ROPE_FREQS = DH // 4
ATTN_SCALE = DH ** -0.5
NEG_INF = -1e30
EPS = 1e-6

TOKEN_TILE = 512
MLSTM_T = 256
FFN_CHUNK = 512
VMEM_LIMIT = 56 * 1024 * 1024
FFN_VMEM_LIMIT = 60 * 1024 * 1024
W_TILE = (256, 512)
W_SLOTS = 6


def _params(sem, vmem=VMEM_LIMIT):
    return pltpu.CompilerParams(dimension_semantics=sem, vmem_limit_bytes=vmem)


def _dot(a, b):
    return jnp.dot(a.astype(BF16), b.astype(BF16), preferred_element_type=F32)


def _dot_nt(a, b):
    return lax.dot_general(a.astype(BF16), b.astype(BF16), (((1,), (1,)), ((), ())),
                           preferred_element_type=F32)


def _split3(x):
    hi = x.astype(BF16)
    r1 = x - hi.astype(F32)
    mid = r1.astype(BF16)
    lo = (r1 - mid.astype(F32)).astype(BF16)
    return hi, mid, lo


def _dot_exact_rhs(a_bf16, x):
    hi, mid, lo = _split3(x)
    f = lambda p: jnp.dot(a_bf16, p, preferred_element_type=F32)
    return f(hi) + f(mid) + f(lo)


def _dot_exact_lhs(x, b_bf16):
    hi, mid, lo = _split3(x)
    f = lambda p: jnp.dot(p, b_bf16, preferred_element_type=F32)
    return f(hi) + f(mid) + f(lo)


def _silu(x):
    return x * jax.nn.sigmoid(x)


def _log_sigmoid(x):
    return jnp.minimum(x, 0.0) - jnp.log1p(jnp.exp(-jnp.abs(x)))


def _rms(x, g):
    return x * lax.rsqrt(jnp.mean(x * x, axis=-1, keepdims=True) + EPS) * g


def _head_mean_sq(x, bd_ref):
    n = x.shape[-1]
    sq = x * x
    hi = sq.astype(BF16)
    lo = (sq - hi.astype(F32)).astype(BF16)
    w = min(n, bd_ref.shape[0])
    bd = bd_ref[:w, :w]
    f = lambda p: jnp.dot(p, bd, preferred_element_type=F32)
    parts = [f(hi[:, c:c + w]) + f(lo[:, c:c + w]) for c in range(0, n, w)]
    return parts[0] if len(parts) == 1 else jnp.concatenate(parts, axis=1)


def _rope(x, cos, sin_signed):
    n = x.shape[-1]
    lane = lax.broadcasted_iota(jnp.int32, x.shape, 1)
    first_half = (lane % DH) < (DH // 2)
    partner = jnp.where(first_half, pltpu.roll(x, n - DH // 2, 1), pltpu.roll(x, DH // 2, 1))
    return x * cos[:, :n] + partner * sin_signed[:, :n]


def _adaln_kernel(c_ref, w_ref, b_ref, o_ref):
    s = _silu(c_ref[...])
    o_ref[...] = _dot(s, w_ref[...]) + b_ref[...]


def _adaln(cond8, w, b):
    n = w.shape[1]
    tn = 1536
    return pl.pallas_call(
        _adaln_kernel,
        out_shape=jax.ShapeDtypeStruct((8, n), F32),
        grid=(n // tn,),
        in_specs=[pl.BlockSpec((8, D_MODEL), lambda j: (0, 0)),
                  pl.BlockSpec((D_MODEL, tn), lambda j: (0, j)),
                  pl.BlockSpec((1, tn), lambda j: (0, j))],
        out_specs=pl.BlockSpec((8, tn), lambda j: (0, j)),
        compiler_params=_params(("arbitrary",)),
        name="adaln",
    )(cond8, w, b.reshape(1, n))


def _const_spec(shape):
    nd = len(shape)
    return pl.BlockSpec(shape, lambda i, _n=nd: (0,) * _n, pipeline_mode=pl.Buffered(1))


def _mod_spec(chunk, rowfn):
    return pl.BlockSpec((1, 1, D_MODEL), lambda i: (rowfn(i) * ADA_CHUNKS + chunk, 0, 0))


def _rowfn(latent):
    if latent:
        return lambda i: 1 + (i * TOKEN_TILE) // LAT_LEN
    return lambda i: 0


def _tok_spec(width, blk0=0):
    return pl.BlockSpec((TOKEN_TILE, width), lambda i: (i + blk0, 0))


def _seq_len(latent):
    return LAT_LEN if latent else CTX_LEN


def _transposed_out(t, latent, rows):
    seq = _seq_len(latent)
    shape = jax.ShapeDtypeStruct((t // seq, rows, seq), F32)
    if seq <= TOKEN_TILE:
        per = TOKEN_TILE // seq
        return shape, pl.BlockSpec((per, rows, seq), lambda i: (i, 0, 0))
    per = seq // TOKEN_TILE
    return shape, pl.BlockSpec((1, rows, TOKEN_TILE), lambda i: (i // per, 0, i % per))


def _store_transposed(ref, x):
    n, _, w = ref.shape
    for j in range(n):
        ref[j] = x[j * w:(j + 1) * w].T


def _dot_colsT(w_cols, h):
    return lax.dot_general(w_cols, h, (((0,), (1,)), ((), ())), preferred_element_type=F32)


def _store_proj_cols_transposed(ref, w_cols, h):
    n, _, w = ref.shape
    for j in range(n):
        ref[j] = _dot_colsT(w_cols, h[j * w:(j + 1) * w])


def _modnorm(x, ng_ref, sh_ref, sc_ref):
    return _rms(x, ng_ref[...]) * (1.0 + sc_ref[0]) + sh_ref[0]


def _load_weights_bf16(pairs, stage, sem):
    tr, tc = W_TILE
    tiles = [(src, dst, r, c) for src, dst in pairs
             for r in range(0, src.shape[0], tr) for c in range(0, src.shape[1], tc)]

    def copy(k):
        src, _, r, c = tiles[k]
        return pltpu.make_async_copy(src.at[r:r + tr, c:c + tc], stage.at[k % W_SLOTS], sem.at[k % W_SLOTS])

    for k in range(min(W_SLOTS, len(tiles))):
        copy(k).start()
    for k, (_, dst, r, c) in enumerate(tiles):
        copy(k).wait()
        dst[r:r + tr, c:c + tc] = stage[k % W_SLOTS].astype(BF16)
        if k + W_SLOTS < len(tiles):
            copy(k + W_SLOTS).start()


def _ffn_kernel(*refs, n_ctx, first, has_mix, last):
    it = iter(refs)
    xs = (next(it), next(it)) if first else (next(it),)
    if has_mix:
        yac_ref, ybc_ref, yal_ref, ybl_ref, wo_hbm, g2_ref = (next(it) for _ in range(6))
    ng_ref, sh_ref, sc_ref, g_ref, win_hbm, wout_hbm = (next(it) for _ in range(6))
    if last:
        nf_ref = next(it)
    outs = (next(it), next(it)) if last else (next(it),)
    win_s, wout_s = next(it), next(it)
    if has_mix:
        wo_s = next(it)
    stage, sem = next(it), next(it)

    step = pl.program_id(0)
    is_lat = step >= n_ctx

    @pl.when(step == 0)
    def _():
        pairs = [(win_hbm, win_s), (wout_hbm, wout_s)]
        if has_mix:
            pairs.insert(0, (wo_hbm, wo_s))
        _load_weights_bf16(pairs, stage, sem)

    pick = lambda c_ref, l_ref: jnp.where(is_lat, l_ref[...], c_ref[...])
    x = pick(*xs) if first else xs[0][...]
    if has_mix:
        half = wo_s.shape[0] // 2
        mo = _dot(pick(yac_ref, yal_ref), wo_s[:half, :]) + _dot(pick(ybc_ref, ybl_ref), wo_s[half:, :])
        x = x + g2_ref[0] * mo
    h = _modnorm(x, ng_ref, sh_ref, sc_ref).astype(BF16)
    acc = None
    for lo in range(0, D_FF, FFN_CHUNK):
        hi = min(lo + FFN_CHUNK, D_FF)
        g = _dot(h, win_s[:, lo:hi])
        u = _dot(h, win_s[:, D_FF + lo:D_FF + hi])
        part = _dot(_silu(g) * u, wout_s[lo:hi, :])
        acc = part if acc is None else acc + part
    y = x + (0.5 * g_ref[0]) * acc
    if last:
        y = _rms(y, nf_ref[...])
        oc_ref, ol_ref = outs

        @pl.when(jnp.logical_not(is_lat))
        def _():
            oc_ref[...] = y
            ol_ref[...] = jnp.zeros_like(ol_ref)

        @pl.when(is_lat)
        def _():
            ol_ref[...] = y
    else:
        outs[0][...] = y


def _ffn(x, mods, mod_base, ng, w_in, w_out, *, t_ctx, t_lat, mix=None, final_g=None, tile=TOKEN_TILE):
    n_ctx, n_lat = t_ctx // tile, t_lat // tile
    first, last = isinstance(x, tuple), final_g is not None
    ctx_map = lambda i: (jnp.minimum(i, n_ctx - 1), 0)
    lat_map = lambda i: (jnp.maximum(i - n_ctx, 0), 0)
    rowfn = lambda i: jnp.where(i < n_ctx, 0, 1 + ((i - n_ctx) * tile) // LAT_LEN)
    two = lambda wd: [pl.BlockSpec((tile, wd), ctx_map), pl.BlockSpec((tile, wd), lat_map)]
    one = pl.BlockSpec((tile, D_MODEL), lambda i: (i, 0))
    hbm = pl.BlockSpec(memory_space=pl.ANY)
    args = list(x) if first else [x]
    specs = two(D_MODEL) if first else [one]
    scratch = [pltpu.VMEM(w_in.shape, BF16), pltpu.VMEM(w_out.shape, BF16)]
    if mix is not None:
        yac, ybc, yal, ybl, wo = mix
        args += [yac, ybc, yal, ybl, wo, mods]
        mw = yac.shape[1]
        specs += [pl.BlockSpec((tile, mw), ctx_map)] * 2 + [pl.BlockSpec((tile, mw), lat_map)] * 2
        specs += [hbm, _mod_spec(mod_base - 1, rowfn)]
        scratch.append(pltpu.VMEM(wo.shape, BF16))
    args += [ng.reshape(1, D_MODEL), mods, mods, mods, w_in, w_out]
    specs += [_const_spec((1, D_MODEL)), _mod_spec(mod_base, rowfn), _mod_spec(mod_base + 1, rowfn),
              _mod_spec(mod_base + 2, rowfn), hbm, hbm]
    if last:
        args.append(final_g.reshape(1, D_MODEL))
        specs.append(_const_spec((1, D_MODEL)))
        out_shape = [jax.ShapeDtypeStruct((t_ctx, D_MODEL), F32), jax.ShapeDtypeStruct((t_lat, D_MODEL), F32)]
        out_specs = two(D_MODEL)
    else:
        out_shape = jax.ShapeDtypeStruct((t_ctx + t_lat, D_MODEL), F32)
        out_specs = one
    scratch += [pltpu.VMEM((W_SLOTS,) + W_TILE, F32), pltpu.SemaphoreType.DMA((W_SLOTS,))]
    return pl.pallas_call(
        functools.partial(_ffn_kernel, n_ctx=n_ctx, first=first, has_mix=mix is not None, last=last),
        out_shape=out_shape,
        grid=(n_ctx + n_lat,),
        in_specs=specs,
        out_specs=out_specs,
        scratch_shapes=scratch,
        compiler_params=_params(("arbitrary",), FFN_VMEM_LIMIT),
        name="ffn",
    )(*args)


_EV_QA, _EV_KA, _EV_VA, _EV_QB, _EV_KB, _EV_VB, _EV_OB, _EV_GI, _EV_GF, _EV_END = (
    0, 512, 640, 768, 1280, 1792, 2304, 2816, 2944, 3072)
N_DIRHEAD = 2 * N_HEADS


def _proj_even_kernel(*refs, rope):
    it = iter(refs)
    x_ref, ng_ref, sh_ref, sc_ref, w_ref, gbi_ref, gbf_ref, gbt_ref, qg_ref, kg_ref, bd_ref = (
        next(it) for _ in range(11))
    if rope:
        cos_ref, sin_ref = next(it), next(it)
    qa_ref, kat_ref, vat_ref, qb_ref, kb_ref, vb_ref, ob_ref, gi_ref, gf_ref, gt_ref, kbt_ref = (next(it) for _ in range(11))

    h = _modnorm(x_ref[...], ng_ref, sh_ref, sc_ref).astype(BF16)
    proj = lambda lo, hi: _dot(h, w_ref[:, lo:hi])

    qa = proj(_EV_QA, _EV_KA)
    qa = qa * lax.rsqrt(_head_mean_sq(qa, bd_ref) + EPS) * qg_ref[...]
    ka = proj(_EV_KA, _EV_VA)
    ka = ka * lax.rsqrt(_head_mean_sq(ka, bd_ref) + EPS) * kg_ref[...]
    if rope:
        qa = _rope(qa, cos_ref[...], sin_ref[...])
        ka = _rope(ka, cos_ref[...], sin_ref[...])
    qa_ref[...] = qa * ATTN_SCALE
    _store_transposed(kat_ref, ka)
    _store_proj_cols_transposed(vat_ref, w_ref[:, _EV_VA:_EV_QB], h)
    qb_ref[...] = proj(_EV_QB, _EV_KB)
    kb_ref[...] = proj(_EV_KB, _EV_VB) * ATTN_SCALE
    vb_ref[...] = proj(_EV_VB, _EV_OB)
    ob_ref[...] = proj(_EV_OB, _EV_GI)
    gi_ref[...] = proj(_EV_GI, _EV_GF) + gbi_ref[...]
    gf_ref[...] = proj(_EV_GF, _EV_END) + gbf_ref[...]
    for j in range(TOKEN_TILE // MLSTM_T):
        ht = h[j * MLSTM_T:(j + 1) * MLSTM_T]
        gates_t = [_dot_colsT(w_ref[:, c0:c0 + N_DIRHEAD], ht) for c0 in (_EV_GI, _EV_GF)]
        gt_ref[j] = jnp.concatenate(gates_t, axis=0) + gbt_ref[...]
        kbt_ref[j] = _dot_colsT(w_ref[:, _EV_KB:_EV_VB], ht) * ATTN_SCALE


def _proj_even(x, t, blk0, mods, latent, ng, w, gbi, gbf, gbt, qg, kg, bd, rope_tabs):
    rowfn = _rowfn(latent)
    args = [x, ng.reshape(1, D_MODEL), mods, mods, w, gbi, gbf, gbt, qg, kg, bd]
    specs = [_tok_spec(D_MODEL, blk0), _const_spec((1, D_MODEL)), _mod_spec(3, rowfn), _mod_spec(4, rowfn)]
    specs += [_const_spec(a.shape) for a in args[4:]]
    if latent:
        nblk = LAT_LEN // TOKEN_TILE
        args += list(rope_tabs)
        specs += [pl.BlockSpec((TOKEN_TILE, 512), lambda i: (i % nblk, 0))] * 2
    widths = (512, 512, 512, 512, 512, 128, 128)
    out_shape = [jax.ShapeDtypeStruct((t, wd), F32) for wd in widths]
    out_specs = [_tok_spec(wd) for wd in widths]
    for pos in (1, 2):
        shape, spec = _transposed_out(t, latent, 128)
        out_shape.insert(pos, shape)
        out_specs.insert(pos, spec)
    nj = TOKEN_TILE // MLSTM_T
    for rows in (2 * N_DIRHEAD, 512):
        out_shape.append(jax.ShapeDtypeStruct((t // MLSTM_T, rows, MLSTM_T), F32))
        out_specs.append(pl.BlockSpec((nj, rows, MLSTM_T), lambda i: (i, 0, 0)))
    return pl.pallas_call(
        functools.partial(_proj_even_kernel, rope=latent),
        out_shape=out_shape, grid=(t // TOKEN_TILE,), in_specs=specs, out_specs=out_specs,
        compiler_params=_params(("arbitrary",)), name="proj_even",
    )(*args)


def _proj_odd_kernel(*refs, rope):
    it = iter(refs)
    x_ref, ng_ref, sh_ref, sc_ref, w_ref = (next(it) for _ in range(5))
    if rope:
        cos_ref, sin_ref = next(it), next(it)
    qc_ref, kc_ref, vc_ref, qd_ref, kd_ref, vd_ref = (next(it) for _ in range(6))

    h = _modnorm(x_ref[...], ng_ref, sh_ref, sc_ref).astype(BF16)
    proj = lambda lo, hi: _dot(h, w_ref[:, lo:hi])
    qc_ref[...] = proj(0, 512) * ATTN_SCALE
    if rope:
        kc_ref[...] = proj(512, 1024)
        vc_ref[...] = proj(1024, 1536)
    else:
        _store_proj_cols_transposed(kc_ref, w_ref[:, 512:1024], h)
        _store_proj_cols_transposed(vc_ref, w_ref[:, 1024:1536], h)
    qd = proj(1536, 2048)
    if rope:
        qd = _rope(qd, cos_ref[...], sin_ref[...])
        _store_transposed(kd_ref, _rope(proj(2048, 2176), cos_ref[...], sin_ref[...]))
    else:
        _store_proj_cols_transposed(kd_ref, w_ref[:, 2048:2176], h)
    _store_proj_cols_transposed(vd_ref, w_ref[:, 2176:2304], h)
    qd_ref[...] = qd * ATTN_SCALE


def _proj_odd(x, t, blk0, mods, latent, ng, w, rope_tabs):
    rowfn = _rowfn(latent)
    args = [x, ng.reshape(1, D_MODEL), mods, mods, w]
    specs = [_tok_spec(D_MODEL, blk0), _const_spec((1, D_MODEL)), _mod_spec(3, rowfn), _mod_spec(4, rowfn),
             _const_spec(w.shape)]
    if latent:
        nblk = LAT_LEN // TOKEN_TILE
        args += list(rope_tabs)
        specs += [pl.BlockSpec((TOKEN_TILE, 512), lambda i: (i % nblk, 0))] * 2
    out_shape = [jax.ShapeDtypeStruct((t, 512), F32)] * 4
    out_specs = [_tok_spec(512)] * 4
    if not latent:
        out_shape[1], out_specs[1] = _transposed_out(t, latent, 512)
        out_shape[2], out_specs[2] = out_shape[1], out_specs[1]
    for _ in range(2):
        shape, spec = _transposed_out(t, latent, 128)
        out_shape.append(shape)
        out_specs.append(spec)
    return pl.pallas_call(
        functools.partial(_proj_odd_kernel, rope=latent),
        out_shape=out_shape,
        grid=(t // TOKEN_TILE,), in_specs=specs, out_specs=out_specs,
        compiler_params=_params(("arbitrary",)), name="proj_odd",
    )(*args)


def _attn_kernel(*refs, group, kv_t, has_extra, has_sink, spb):
    it = iter(refs)
    q_ref, k_ref, v_ref = next(it), next(it), next(it)
    if has_extra:
        ke_ref, ve_ref = next(it), next(it)
    if has_sink:
        sink_ref = next(it)
    o_ref = next(it)
    tq = q_ref.shape[0] // spb
    first = lax.broadcasted_iota(jnp.int32, (tq, 128), 1) < DH

    def kv_operands(kt_ref, vt_ref, smp, kv):
        kt = kt_ref[smp, kv * DH:(kv + 1) * DH, :].astype(BF16)
        vt = vt_ref[smp, kv * DH:(kv + 1) * DH, :].astype(BF16)
        z = jnp.zeros_like(kt)
        ones = jnp.ones((128, kt.shape[1]), BF16)
        return ((jnp.concatenate([kt, z], axis=0), jnp.concatenate([z, kt], axis=0)),
                (jnp.concatenate([vt, z, ones], axis=0), jnp.concatenate([z, vt, ones], axis=0)))

    for smp in range(spb):
        qrows = slice(smp * tq, (smp + 1) * tq)
        cache = {}
        for j in range(N_HEADS // 2):
            ps = slice(128 * j, 128 * (j + 1))
            q_pair = q_ref[qrows, ps]
            if kv_t:
                qb = q_pair.astype(BF16)
                ops = []
                for half in range(2):
                    kv = (2 * j + half) // group
                    if kv not in cache:
                        cache[kv] = (kv_operands(k_ref, v_ref, smp, kv),
                                     kv_operands(ke_ref, ve_ref, smp, kv) if has_extra else None)
                    ops.append(cache[kv])
            else:
                keys = k_ref.shape[0] // spb
                krows = slice(smp * keys, (smp + 1) * keys)
                k_pair = k_ref[krows, ps].astype(BF16)
                v_aug = jnp.concatenate([v_ref[krows, ps].astype(BF16), jnp.ones((keys, 128), BF16)], axis=1)
            res, sink_den = [], []
            for half in range(2):
                if kv_t:
                    (k_ops, v_ops), extra = ops[half]
                    s = jnp.dot(qb, k_ops[half], preferred_element_type=F32)
                else:
                    s = _dot_nt(jnp.where(first, q_pair, 0.0) if half == 0 else jnp.where(first, 0.0, q_pair), k_pair)
                m = jnp.max(s, axis=-1, keepdims=True)
                if has_extra:
                    se = jnp.dot(qb, extra[0][half], preferred_element_type=F32)
                    m = jnp.maximum(m, jnp.max(se, axis=-1, keepdims=True))
                if has_sink:
                    m = jnp.maximum(m, sink_ref[2 * j + half])
                    sink_den.append(jnp.exp(sink_ref[2 * j + half] - m))
                p = jnp.exp(s - m)
                r = _dot_nt(p, v_ops[half]) if kv_t else _dot(p, v_aug)
                if has_extra:
                    r = r + _dot_nt(jnp.exp(se - m), extra[1][half])
                res.append(r)
            num = jnp.where(first, res[0][:, :128], res[1][:, :128])
            den = jnp.where(first, res[0][:, 128:], res[1][:, 128:])
            if has_sink:
                den = den + jnp.where(first, sink_den[0], sink_den[1])
            o_ref[qrows, ps] = num / den


def _attention(q, k, v, *, n_batch, q_len, q_tile, group, kv_t, extra=None, sink=None):
    nq = q_len // q_tile
    spb = next(n for n in (4, 2, 1) if n_batch % n == 0) if nq == 1 else 1
    args = [q, k, v]
    specs = [pl.BlockSpec((spb * q_tile, 512), lambda b, j: (b * nq + j, 0))]
    if kv_t:
        kv_spec = lambda a: pl.BlockSpec((spb,) + a.shape[1:], lambda b, j: (b, 0, 0))
        specs += [kv_spec(k), kv_spec(v)]
        if extra is not None:
            args += list(extra)
            specs += [kv_spec(extra[0]), kv_spec(extra[1])]
    else:
        assert extra is None
        keys = k.shape[0] // n_batch
        specs += [pl.BlockSpec((spb * keys, 512), lambda b, j: (b, 0))] * 2
    if sink is not None:
        args.append(sink)
        specs.append(pl.BlockSpec(memory_space=pltpu.SMEM))
    return pl.pallas_call(
        functools.partial(_attn_kernel, group=group, kv_t=kv_t, has_extra=extra is not None,
                          has_sink=sink is not None, spb=spb),
        out_shape=jax.ShapeDtypeStruct(q.shape, F32),
        grid=(n_batch // spb, nq), in_specs=specs,
        out_specs=pl.BlockSpec((spb * q_tile, 512), lambda b, j: (b * nq + j, 0)),
        compiler_params=_params(("arbitrary", "arbitrary")), name="attn",
    )(*args)


LOG2E = 1.4426950408889634


def _cummax_time(x, reverse):
    n = x.shape[0]
    row = lax.broadcasted_iota(jnp.int32, x.shape, 0)
    k = 1
    while k < n:
        if reverse:
            shifted = jnp.where(row < n - k, pltpu.roll(x, n - k, 0), -jnp.inf)
        else:
            shifted = jnp.where(row >= k, pltpu.roll(x, k, 0), -jnp.inf)
        x = jnp.maximum(x, shifted)
        k *= 2
    return x


def _mlstm_kernel(*refs, n_chunks, carry, spb):
    it = iter(refs)
    q_ref, k_ref, kt_ref, v_ref, ob_ref, gi_ref, gf_ref, gt_ref, hg_ref, bd_ref = (next(it) for _ in range(10))
    if carry:
        c0_ref, n0_ref, m0r_ref, m0c_ref = (next(it) for _ in range(4))
    y_ref = next(it)
    if not carry:
        c_ref, n_ref, m_ref = next(it), next(it), next(it)
    hf_s, hb_s = next(it), next(it)
    if carry:
        st_s, mrow_s, mcol_s = next(it), next(it), next(it)
    T = MLSTM_T
    step = pl.program_id(1)

    if carry:
        @pl.when(step == 0)
        def _():
            z = jnp.zeros((DH, DH), F32)
            for d in range(2):
                for j in range(N_HEADS // 2):
                    ca, cb = c0_ref[0, d, 2 * j], c0_ref[0, d, 2 * j + 1]
                    na, nb = n0_ref[0, d, 2 * j][:, :DH], n0_ref[0, d, 2 * j + 1][:, :DH]
                    st_s[d, j, 0:DH, :] = jnp.concatenate([ca, z, na, z], axis=1)
                    st_s[d, j, DH:2 * DH, :] = jnp.concatenate([z, cb, z, nb], axis=1)
            mrow_s[...] = m0r_ref[0]
            mcol_s[...] = m0c_ref[0]

    row = lax.broadcasted_iota(jnp.int32, (T, T), 0)
    col = lax.broadcasted_iota(jnp.int32, (T, T), 1)
    lower = row >= col
    upper = row <= col
    lower_m = jnp.where(lower, 1.0, 0.0).astype(BF16)
    upper_m = jnp.where(upper, 1.0, 0.0).astype(BF16)
    lane = lax.broadcasted_iota(jnp.int32, (T, 128), 1)
    first = lane < DH
    lane_c = lax.broadcasted_iota(jnp.int32, (DH, 256), 1) % 128
    zeros_kt = jnp.zeros((DH, T), BF16)
    ones_v = jnp.ones((T, 128), BF16)

    for smp, d in ((s_, d_) for s_ in range(spb) for d_ in range(2)):
        chunk = (step if d == 0 else n_chunks - 1 - step) if carry else smp
        rows = pl.ds(pl.multiple_of(chunk * T, T), T)
        hd0 = N_HEADS * d
        mask = lower if d == 0 else upper

        gi, gf = gi_ref[rows, :], gf_ref[rows, :]
        lf = _log_sigmoid(gf)
        b_cols = _dot_exact_rhs(lower_m, lf)
        if d == 1:
            b_cols = b_cols[T - 1:T, :] - b_cols + lf
        a_cols = gi - b_cols
        dmax = b_cols + _cummax_time(a_cols, reverse=d == 1)
        if carry:
            inter = b_cols + mrow_s[0:1, :]
            mt = jnp.maximum(inter, dmax)
            w_inter = jnp.exp(inter - mt)
        else:
            mt = jnp.maximum(b_cols, dmax)
        c2 = (b_cols - mt) * LOG2E
        einv = jnp.exp(-mt)

        gt = gt_ref[chunk]
        gi_t = gt[hd0:hd0 + N_HEADS, :]
        gf_t = gt[N_DIRHEAD + hd0:N_DIRHEAD + hd0 + N_HEADS, :]
        lf_t = _log_sigmoid(gf_t)
        b_rows = _dot_exact_lhs(lf_t, upper_m)
        if d == 1:
            b_rows = b_rows[:, T - 1:T] - b_rows + lf_t
        a_rows = gi_t - b_rows
        a2 = a_rows * LOG2E
        b_last = b_rows[:, T - 1:T] if d == 0 else b_rows[:, 0:1]
        dec_rows = b_last + a_rows
        m_new = jnp.max(dec_rows, axis=-1, keepdims=True)
        if carry:
            m_prev = mcol_s[hd0:hd0 + N_HEADS, 0:1]
            m_new = jnp.maximum(m_new, b_last + m_prev)
            wc = jnp.exp(b_last + m_prev - m_new)
        else:
            m_new = jnp.maximum(m_new, b_last)
        ws_rows = jnp.exp(dec_rows - m_new)

        h_s = hf_s if d == 0 else hb_s
        for j in range(N_HEADS // 2):
            ps = slice(128 * j, 128 * (j + 1))
            q_pair = q_ref[rows, ps].astype(BF16)
            v_aug = jnp.concatenate([v_ref[rows, ps].astype(BF16), ones_v], axis=1)
            kt = (kt_ref[chunk, 128 * j:128 * j + DH, :], kt_ref[chunk, 128 * j + DH:128 * (j + 1), :])
            lhs = []
            for half in range(2):
                hd = 2 * j + half
                kt0 = jnp.concatenate([kt[0].astype(BF16), zeros_kt] if half == 0 else [zeros_kt, kt[1].astype(BF16)], axis=0)
                qk = jnp.dot(q_pair, kt0, preferred_element_type=F32)
                e = c2[:, hd0 + hd:hd0 + hd + 1] + a2[hd:hd + 1, :]
                lhs.append((jnp.exp2(jnp.where(mask, e, -jnp.inf)) * qk).astype(BF16))
            for half in range(2):
                lhs.append((kt[half] * ws_rows[2 * j + half:2 * j + half + 1, :]).astype(BF16))
            both = jnp.dot(jnp.concatenate(lhs, axis=0), v_aug, preferred_element_type=F32)
            res = (both[0:T], both[T:2 * T])
            cu = (both[2 * T:2 * T + DH], both[2 * T + DH:2 * T + 2 * DH])
            ra, rb = hd0 + 2 * j, hd0 + 2 * j + 1
            num = jnp.where(first, res[0][:, :128], res[1][:, :128])
            den = jnp.where(first, res[0][:, 128:], res[1][:, 128:])
            if carry:
                old = st_s[d, j]
                qs = _dot(q_pair, old)
                wi = jnp.where(first, w_inter[:, ra:ra + 1], w_inter[:, rb:rb + 1])
                num = num + wi * qs[:, :128]
                den = den + wi * qs[:, 128:]
            floor = jnp.where(first, einv[:, ra:ra + 1], einv[:, rb:rb + 1])
            h_s[rows, ps] = num / jnp.maximum(jnp.abs(den), floor)
            if carry:
                st_s[d, j, 0:DH, :] = jnp.where(lane_c < DH, cu[0], 0.0) + wc[2 * j:2 * j + 1, :] * old[0:DH]
                st_s[d, j, DH:2 * DH, :] = jnp.where(lane_c >= DH, cu[1], 0.0) + wc[2 * j + 1:2 * j + 2, :] * old[DH:2 * DH]
            else:
                c_ref[smp, d, 2 * j] = cu[0][:, 0:DH]
                c_ref[smp, d, 2 * j + 1] = cu[1][:, DH:2 * DH]

        if carry:
            b_last_l = b_cols[T - 1:T, :] if d == 0 else b_cols[0:1, :]
            m_prev_l = mrow_s[0:1, :]
            m_new_l = jnp.maximum(jnp.max(b_last_l + a_cols, axis=0, keepdims=True), b_last_l + m_prev_l)
            lane_r = lax.broadcasted_iota(jnp.int32, (1, 128), 1)
            mine = (lane_r >= hd0) & (lane_r < hd0 + N_HEADS)
            mrow_s[0:1, :] = jnp.where(mine, m_new_l, m_prev_l)
            mcol_s[hd0:hd0 + N_HEADS, :] = jnp.broadcast_to(m_new, (N_HEADS, 128))
        else:
            n_all = _dot(ws_rows, k_ref[rows, :])
            for hd in range(N_HEADS):
                n_ref[smp, d, hd:hd + 1, :] = n_all[hd:hd + 1, hd * DH:(hd + 1) * DH]
            m_ref[smp, hd0:hd0 + N_HEADS, :] = jnp.broadcast_to(m_new, (N_HEADS, 128))

    @pl.when(step == n_chunks - 1)
    def _():
        hsum = hf_s[...] + hb_s[...]
        y = hsum * lax.rsqrt(_head_mean_sq(hsum, bd_ref) + EPS) * hg_ref[...]
        y_ref[...] = jax.nn.sigmoid(ob_ref[...]) * y


def _mlstm(qb, kb, kbt, vb, ob, gi, gf, gt, head_g, bd, *, n_batch, seq_len, state=None):
    n_chunks = seq_len // MLSTM_T
    carry = state is not None
    assert carry or n_chunks == 1
    spb = 1
    seq = lambda wd: pl.BlockSpec((spb * seq_len, wd), lambda b, c: (b, 0))
    per_chunk = lambda r: pl.BlockSpec((spb * n_chunks, r, MLSTM_T), lambda b, c: (b, 0, 0))
    args = [qb, kb, kbt, vb, ob, gi, gf, gt, head_g, bd]
    specs = [seq(512), seq(512), per_chunk(512), seq(512), seq(512), seq(128), seq(128), per_chunk(2 * N_DIRHEAD),
             pl.BlockSpec((1, 512), lambda b, c: (0, 0)), pl.BlockSpec(bd.shape, lambda b, c: (0, 0))]
    out_shape = [jax.ShapeDtypeStruct((n_batch * seq_len, 512), F32)]
    out_specs = [seq(512)]
    scratch = [pltpu.VMEM((spb * seq_len, 512), F32), pltpu.VMEM((spb * seq_len, 512), F32)]
    if carry:
        args += list(state)
        specs += [pl.BlockSpec((1, 2, N_HEADS, DH, DH), lambda b, c: (b, 0, 0, 0, 0)),
                  pl.BlockSpec((1, 2, N_HEADS, DH, 128), lambda b, c: (b, 0, 0, 0, 0)),
                  pl.BlockSpec((1, 8, 128), lambda b, c: (b, 0, 0)),
                  pl.BlockSpec((1, N_DIRHEAD, 128), lambda b, c: (b, 0, 0))]
        scratch += [pltpu.VMEM((2, N_HEADS // 2, 128, 256), F32), pltpu.VMEM((8, 128), F32),
                    pltpu.VMEM((N_DIRHEAD, 128), F32)]
    else:
        out_shape += [jax.ShapeDtypeStruct((n_batch, 2, N_HEADS, DH, DH), F32),
                      jax.ShapeDtypeStruct((n_batch, 2, N_HEADS, DH), F32),
                      jax.ShapeDtypeStruct((n_batch, N_DIRHEAD, 128), F32)]
        out_specs += [pl.BlockSpec((spb, 2, N_HEADS, DH, DH), lambda b, c: (b, 0, 0, 0, 0)),
                      pl.BlockSpec((spb, 2, N_HEADS, DH), lambda b, c: (b, 0, 0, 0)),
                      pl.BlockSpec((spb, N_DIRHEAD, 128), lambda b, c: (b, 0, 0))]
    return pl.pallas_call(
        functools.partial(_mlstm_kernel, n_chunks=n_chunks, carry=carry, spb=spb),
        out_shape=out_shape, grid=(n_batch // spb, n_chunks), in_specs=specs, out_specs=out_specs,
        scratch_shapes=scratch,
        compiler_params=_params(("arbitrary", "arbitrary")), name="mlstm",
    )(*args)


N_ROWS = LAT_LEN // GRID_W
N_DY = 2 * NA_KH - 1
N_DX = 2 * NA_KW - 1


NA_QROWS = 4
NA_KROWS = 12
NA_GROUPS = N_ROWS // NA_QROWS


def _na_kernel(q_ref, k_ref, v_ref, ke_ref, ve_ref, rpb_ref, o_ref, tile_s, slab_s):
    pair = pl.program_id(1)
    qn, kn = NA_QROWS * GRID_W, NA_KROWS * GRID_W

    @pl.when(pl.program_id(0) == 0)
    def _():
        qc = lax.broadcasted_iota(jnp.int32, (GRID_W, GRID_W), 0)
        kc = lax.broadcasted_iota(jnp.int32, (GRID_W, GRID_W), 1)
        start = jnp.clip(qc - NA_KW // 2, 0, GRID_W - NA_KW)
        in_win = (kc >= start) & (kc < start + NA_KW)
        dx = kc - qc + (NA_KW - 1)
        blocked = jnp.full((GRID_W, GRID_W), NEG_INF, F32)
        for half in range(2):
            head = 2 * pair + half

            def build_tile(dy, carry):
                tile = blocked
                for j in range(N_DX):
                    tile = jnp.where(dx == j, rpb_ref[(head * N_DY + dy) * N_DX + j], tile)
                tile_s[half * N_DY + dy] = jnp.where(in_win, tile, NEG_INF)
                return carry

            lax.fori_loop(0, N_DY, build_tile, 0)
            for kind in range(3):
                idx = (pair * 2 + half) * 3 + kind
                for i in range(NA_QROWS):
                    for k in range(NA_KROWS):
                        if kind == 0:
                            ok, dy = k < NA_KH, k - i + NA_KH - 1
                        elif kind == 1:
                            ok, dy = i <= k < i + NA_KH, k - i + NA_KH // 2 - 1
                        else:
                            ok, dy = k >= NA_KROWS - NA_KH, k - i - 1
                        slab_s[idx, i * GRID_W:(i + 1) * GRID_W, k * GRID_W:(k + 1) * GRID_W] = (
                            tile_s[half * N_DY + dy] if ok else blocked)

    first = lax.broadcasted_iota(jnp.int32, (qn, 128), 1) < DH
    ones_k = jnp.ones((kn, 128), BF16)
    ke = ke_ref[...].astype(BF16)
    ve_aug = jnp.concatenate([ve_ref[...].astype(BF16), jnp.ones((ke.shape[0], 128), BF16)], axis=1)

    def group_body(g, carry):
        k0 = jnp.where(g < NA_GROUPS // 2, 0, N_ROWS - NA_KROWS)
        kind = jnp.where(g == 0, 0, jnp.where(g == NA_GROUPS - 1, 2, 1))
        qrows = pl.ds(pl.multiple_of(g * qn, qn), qn)
        keys = pl.ds(pl.multiple_of(k0 * GRID_W, GRID_W), kn)
        q_pair = q_ref[qrows, :]
        k_pair = k_ref[keys, :].astype(BF16)
        v_aug = jnp.concatenate([v_ref[keys, :].astype(BF16), ones_k], axis=1)
        res = []
        for half in range(2):
            qm = jnp.where(first, q_pair, 0.0) if half == 0 else jnp.where(first, 0.0, q_pair)
            s = _dot_nt(qm, k_pair) + slab_s[(pair * 2 + half) * 3 + kind]
            se = _dot_nt(qm, ke)
            m = jnp.maximum(jnp.max(s, axis=-1, keepdims=True), jnp.max(se, axis=-1, keepdims=True))
            res.append(_dot(jnp.exp(s - m), v_aug) + _dot(jnp.exp(se - m), ve_aug))
        num = jnp.where(first, res[0][:, :128], res[1][:, :128])
        den = jnp.where(first, res[0][:, 128:], res[1][:, 128:])
        o_ref[qrows, :] = num / den
        return carry

    lax.fori_loop(0, NA_GROUPS, group_body, 0)


def _na_latent(q, k, v, k_ctx, v_ctx, rpb_flat, n_batch):
    seq = pl.BlockSpec((LAT_LEN, 128), lambda b, j: (b, j))
    ctx = pl.BlockSpec((CTX_LEN, 128), lambda b, j: (b, j))
    return pl.pallas_call(
        _na_kernel,
        out_shape=jax.ShapeDtypeStruct(q.shape, F32),
        grid=(n_batch, N_HEADS // 2),
        in_specs=[seq, seq, seq, ctx, ctx, pl.BlockSpec(memory_space=pltpu.SMEM)],
        out_specs=seq,
        scratch_shapes=[pltpu.VMEM((2 * N_DY, GRID_W, GRID_W), F32),
                        pltpu.VMEM((N_HEADS * 3, NA_QROWS * GRID_W, NA_KROWS * GRID_W), F32)],
        compiler_params=_params(("arbitrary", "arbitrary")), name="na_latent",
    )(q, k, v, k_ctx, v_ctx, rpb_flat)


SWA_QT = 256
SWA_SPAN = SWA_QT + 2 * SWA_WIN


def _swa_kernel(q_ref, kt_ref, vt_ref, ket_ref, vet_ref, sink_ref, o_ref):
    pair = pl.program_id(1)
    first = lax.broadcasted_iota(jnp.int32, (SWA_QT, 128), 1) < DH
    row = lax.broadcasted_iota(jnp.int32, (SWA_QT, SWA_SPAN), 0)
    col = lax.broadcasted_iota(jnp.int32, (SWA_QT, SWA_SPAN), 1)
    z = jnp.zeros((DH, SWA_SPAN), BF16)
    ones = jnp.ones((128, SWA_SPAN), BF16)
    ket, vet = ket_ref[0].astype(BF16), vet_ref[0].astype(BF16)
    ze = jnp.zeros_like(ket)
    ones_e = jnp.ones((128, ket.shape[1]), BF16)
    sinks = (sink_ref[2 * pair], sink_ref[2 * pair + 1])
    for g in range(LAT_LEN // SWA_QT):
        q0 = g * SWA_QT
        lo = min(max(q0 - SWA_WIN, 0), LAT_LEN - SWA_SPAN)
        near = jnp.abs((lo + col) - (q0 + row)) <= SWA_WIN
        qb = q_ref[q0:q0 + SWA_QT, :].astype(BF16)
        kt = kt_ref[0, :, lo:lo + SWA_SPAN].astype(BF16)
        vt = vt_ref[0, :, lo:lo + SWA_SPAN].astype(BF16)
        res, sink_den = [], []
        for half in range(2):
            k_op = jnp.concatenate([kt, z] if half == 0 else [z, kt], axis=0)
            v_op = jnp.concatenate([vt, z, ones] if half == 0 else [z, vt, ones], axis=0)
            ke_op = jnp.concatenate([ket, ze] if half == 0 else [ze, ket], axis=0)
            ve_op = jnp.concatenate([vet, ze, ones_e] if half == 0 else [ze, vet, ones_e], axis=0)
            s = jnp.where(near, jnp.dot(qb, k_op, preferred_element_type=F32), NEG_INF)
            se = jnp.dot(qb, ke_op, preferred_element_type=F32)
            m = jnp.maximum(jnp.max(s, axis=-1, keepdims=True), jnp.max(se, axis=-1, keepdims=True))
            m = jnp.maximum(m, sinks[half])
            sink_den.append(jnp.exp(sinks[half] - m))
            res.append(_dot_nt(jnp.exp(s - m), v_op) + _dot_nt(jnp.exp(se - m), ve_op))
        num = jnp.where(first, res[0][:, :128], res[1][:, :128])
        den = jnp.where(first, res[0][:, 128:], res[1][:, 128:]) + jnp.where(first, sink_den[0], sink_den[1])
        o_ref[q0:q0 + SWA_QT, :] = num / den


def _swa_latent(q, kt, vt, kt_ctx, vt_ctx, sink, n_batch, group):
    ppk = group // 2
    kv_spec = lambda a: pl.BlockSpec((1, DH, a.shape[2]), lambda b, j: (b, j // ppk, 0))
    return pl.pallas_call(
        _swa_kernel,
        out_shape=jax.ShapeDtypeStruct(q.shape, F32),
        grid=(n_batch, N_HEADS // 2),
        in_specs=[pl.BlockSpec((LAT_LEN, 128), lambda b, j: (b, j)),
                  kv_spec(kt), kv_spec(vt), kv_spec(kt_ctx), kv_spec(vt_ctx),
                  pl.BlockSpec(memory_space=pltpu.SMEM)],
        out_specs=pl.BlockSpec((LAT_LEN, 128), lambda b, j: (b, j)),
        compiler_params=_params(("arbitrary", "arbitrary")), name="swa_latent",
    )(q, kt, vt, kt_ctx, vt_ctx, sink)


def _rope_tables():
    t = jnp.arange(LAT_LEN)
    pos = jnp.stack([t // GRID_W, t % GRID_W], axis=-1).astype(F32)
    freqs = ROPE_THETA ** (-jnp.arange(ROPE_FREQS, dtype=F32) / ROPE_FREQS)
    ang = (pos[:, :, None] * freqs).reshape(LAT_LEN, 2 * ROPE_FREQS)
    cos, sin = jnp.cos(ang), jnp.sin(ang)
    cos_t = jnp.tile(jnp.concatenate([cos, cos], axis=-1), (1, N_HEADS))
    sin_t = jnp.tile(jnp.concatenate([-sin, sin], axis=-1), (1, N_HEADS))
    return cos_t, sin_t


def kernel(x_prompt, x_sample, cache_l0_attn_k, cache_l0_attn_v, state_l0_mlstm_C, state_l0_mlstm_n, state_l0_mlstm_m, cache_l1_na_k, cache_l1_na_v, cache_l1_swa_k, cache_l1_swa_v, c, c_ctx, norm_final, ada_w_l0, ada_b_l0, norm_l0, ffn1_in_l0, ffn1_out_l0, ffn2_in_l0, ffn2_out_l0, mix_in_l0, mix_out_l0, qk_norm_l0, gate_bias_l0, head_norm_l0, ada_w_l1, ada_b_l1, norm_l1, ffn1_in_l1, ffn1_out_l1, ffn2_in_l1, ffn2_out_l1, mix_in_l1, mix_out_l1, rpb_l1, sink_l1):
    nb, nl = x_prompt.shape[0], x_sample.shape[0]
    bf = lambda a: a.astype(BF16)

    cond8 = jnp.zeros((8, D_MODEL), F32).at[0].set(c_ctx).at[1:1 + nl].set(c)
    mods0 = _adaln(cond8, ada_w_l0, ada_b_l0).reshape(8 * ADA_CHUNKS, 1, D_MODEL)
    mods1 = _adaln(cond8, ada_w_l1, ada_b_l1).reshape(8 * ADA_CHUNKS, 1, D_MODEL)

    g0 = 2304
    gcols = lambda j: mix_in_l0[:, g0 + 8 * j:g0 + 8 * (j + 1)]
    gpad = jnp.zeros((D_MODEL, 128 - N_DIRHEAD), F32)
    w_gi, w_gf = jnp.concatenate([gcols(0), gcols(2)], axis=1), jnp.concatenate([gcols(1), gcols(3)], axis=1)
    w_even = bf(jnp.concatenate([mix_in_l0[:, :g0], mix_in_l0[:, g0 + 32:], w_gi, gpad, w_gf, gpad], axis=1))
    gb4 = gate_bias_l0.reshape(4, N_HEADS)
    b_gi, b_gf = jnp.concatenate([gb4[0], gb4[2]]), jnp.concatenate([gb4[1], gb4[3]])
    gbi = jnp.zeros((1, 128), F32).at[0, :N_DIRHEAD].set(b_gi)
    gbf = jnp.zeros((1, 128), F32).at[0, :N_DIRHEAD].set(b_gf)
    gbt = jnp.concatenate([b_gi, b_gf]).reshape(2 * N_DIRHEAD, 1)
    qg = jnp.tile(qk_norm_l0[0], N_HEADS).reshape(1, 512)
    kg = jnp.tile(qk_norm_l0[1], 2).reshape(1, 128)
    head_g = head_norm_l0.reshape(1, 512)
    grp = np.arange(256) // DH
    bd = jnp.asarray((grp[:, None] == grp[None, :]).astype(np.float32) / DH, dtype=BF16)
    rope_tabs = _rope_tables()
    w_odd = bf(mix_in_l1)
    rpb_flat = rpb_l1.reshape(-1)
    to_t = lambda a: a.reshape(a.shape[0], a.shape[1], -1).transpose(0, 2, 1)
    from_t = lambda a: a.reshape(a.shape[0], -1, DH, a.shape[2]).transpose(0, 3, 1, 2)

    xp = x_prompt.reshape(nb * CTX_LEN, D_MODEL)
    xs = x_sample.reshape(nl * LAT_LEN, D_MODEL)
    t_ctx, t_lat = xp.shape[0], xs.shape[0]
    streams = ((False, t_ctx, 0, nb, CTX_LEN), (True, t_lat, t_ctx // TOKEN_TILE, nl, LAT_LEN))
    ffn = functools.partial(_ffn, t_ctx=t_ctx, t_lat=t_lat)

    x = ffn((xp, xs), mods0, 0, norm_l0[0], ffn1_in_l0, ffn1_out_l0)
    ya, yb = {}, {}
    for latent, t, blk0, n_batch, seq_len in streams:
        qa, kat, vat, qb, kb, vb, ob, gi, gf, gt, kbt = _proj_even(
            x, t, blk0, mods0, latent, norm_l0[1], w_even, gbi, gbf, gbt, qg, kg, bd, rope_tabs)
        if latent:
            extra = (to_t(cache_l0_attn_k), to_t(cache_l0_attn_v))
            m0 = state_l0_mlstm_m.reshape(nl, N_DIRHEAD)
            m0_lanes = jnp.zeros((nl, 8, 128), F32).at[:, 0, :N_DIRHEAD].set(m0)
            m0_rows = jnp.broadcast_to(m0[:, :, None], (nl, N_DIRHEAD, 128))
            n0_cols = jnp.broadcast_to(state_l0_mlstm_n[..., None], (nl, 2, N_HEADS, DH, 128))
            state = (state_l0_mlstm_C, n0_cols, m0_lanes, m0_rows)
        else:
            extra, state = None, None
            k0t, v0t = kat, vat
        ya[latent] = _attention(qa, kat, vat, n_batch=n_batch, q_len=seq_len, q_tile=256, group=4, kv_t=True, extra=extra)
        ml = _mlstm(qb, kb, kbt, vb, ob, gi, gf, gt, head_g, bd, n_batch=n_batch, seq_len=seq_len, state=state)
        yb[latent] = ml[0]
        if not latent:
            c0, n0, m0_pad = ml[1:]
    x = ffn(x, mods0, 6, norm_l0[2], ffn2_in_l0, ffn2_out_l0, mix=(ya[False], yb[False], ya[True], yb[True], mix_out_l0))

    x = ffn(x, mods1, 0, norm_l1[0], ffn1_in_l1, ffn1_out_l1, tile=1024)
    yc, yd = {}, {}
    for latent, t, blk0, n_batch, seq_len in streams:
        qc, kc, vc, qd, kd, vd = _proj_odd(x, t, blk0, mods1, latent, norm_l1[1], w_odd, rope_tabs)
        if latent:
            yc[latent] = _na_latent(qc, kc, vc, cache_l1_na_k.reshape(nl * CTX_LEN, 512),
                                    cache_l1_na_v.reshape(nl * CTX_LEN, 512), rpb_flat, nl)
            yd[latent] = _swa_latent(qd, kd, vd, to_t(cache_l1_swa_k), to_t(cache_l1_swa_v), sink_l1, nl, group=4)
        else:
            yc[latent] = _attention(qc, kc, vc, n_batch=n_batch, q_len=seq_len, q_tile=256, group=1, kv_t=True)
            yd[latent] = _attention(qd, kd, vd, n_batch=n_batch, q_len=seq_len, q_tile=256, group=4, kv_t=True, sink=sink_l1)
            kc1t, vc1t, kd1t, vd1t = kc, vc, kd, vd
    y_prompt, y_sample = ffn(x, mods1, 6, norm_l1[2], ffn2_in_l1, ffn2_out_l1,
                             mix=(yc[False], yd[False], yc[True], yd[True], mix_out_l1), final_g=norm_final)

    return (y_prompt.reshape(nb, CTX_LEN, D_MODEL), y_sample.reshape(nl, LAT_LEN, D_MODEL),
            from_t(k0t), from_t(v0t),
            c0, n0, m0_pad[:, :, 0].reshape(nb, 2, N_HEADS),
            from_t(kc1t), from_t(vc1t),
            from_t(kd1t), from_t(vd1t))
```

```python
import functools

import jax
import jax.numpy as jnp
import numpy as np
from jax import lax
from jax.experimental import pallas as pl
from jax.experimental.pallas import tpu as pltpu

F32 = jnp.float32
BF16 = jnp.bfloat16

D_MODEL = 1024
DH = 64
D_FF = 2816
ADA_CHUNKS = 9
GRID_W = 64
LAT_LEN = 1024
CTX_LEN = 256
N_HEADS = 8
NA_KH = 8
NA_KW = 16
SWA_WIN = 128
ROPE_THETA = 10000.0
ROPE_FREQS = DH // 4
ATTN_SCALE = DH ** -0.5
NEG_INF = -1e30
EPS = 1e-6

TOKEN_TILE = 512
MLSTM_T = 256
FFN_CHUNK = 512
VMEM_LIMIT = 56 * 1024 * 1024
FFN_VMEM_LIMIT = 60 * 1024 * 1024
W_TILE = (256, 512)
W_SLOTS = 6


def _params(sem, vmem=VMEM_LIMIT):
    return pltpu.CompilerParams(dimension_semantics=sem, vmem_limit_bytes=vmem)


def _dot(a, b):
    return jnp.dot(a.astype(BF16), b.astype(BF16), preferred_element_type=F32)


def _dot_nt(a, b):
    return lax.dot_general(a.astype(BF16), b.astype(BF16), (((1,), (1,)), ((), ())),
                           preferred_element_type=F32)


def _split3(x):
    hi = x.astype(BF16)
    r1 = x - hi.astype(F32)
    mid = r1.astype(BF16)
    lo = (r1 - mid.astype(F32)).astype(BF16)
    return hi, mid, lo


def _dot_exact_rhs(a_bf16, x):
    hi, mid, lo = _split3(x)
    f = lambda p: jnp.dot(a_bf16, p, preferred_element_type=F32)
    return f(hi) + f(mid) + f(lo)


def _dot_exact_lhs(x, b_bf16):
    hi, mid, lo = _split3(x)
    f = lambda p: jnp.dot(p, b_bf16, preferred_element_type=F32)
    return f(hi) + f(mid) + f(lo)


def _silu(x):
    return x * jax.nn.sigmoid(x)


def _log_sigmoid(x):
    return jnp.minimum(x, 0.0) - jnp.log1p(jnp.exp(-jnp.abs(x)))


def _rms(x, g):
    return x * lax.rsqrt(jnp.mean(x * x, axis=-1, keepdims=True) + EPS) * g


def _head_mean_sq(x, bd_ref):
    n = x.shape[-1]
    sq = x * x
    hi = sq.astype(BF16)
    lo = (sq - hi.astype(F32)).astype(BF16)
    w = min(n, bd_ref.shape[0])
    bd = bd_ref[:w, :w]
    f = lambda p: jnp.dot(p, bd, preferred_element_type=F32)
    parts = [f(hi[:, c:c + w]) + f(lo[:, c:c + w]) for c in range(0, n, w)]
    return parts[0] if len(parts) == 1 else jnp.concatenate(parts, axis=1)


def _rope(x, cos, sin_signed):
    n = x.shape[-1]
    lane = lax.broadcasted_iota(jnp.int32, x.shape, 1)
    first_half = (lane % DH) < (DH // 2)
    partner = jnp.where(first_half, pltpu.roll(x, n - DH // 2, 1), pltpu.roll(x, DH // 2, 1))
    spread = lambda tab: tab if n == 128 else jnp.concatenate([tab] * (n // 128), axis=1)
    return x * spread(cos) + partner * spread(sin_signed)


def _adaln_kernel(c_ref, w_ref, b_ref, o_ref):
    s = _silu(c_ref[...])
    o_ref[...] = _dot(s, w_ref[...]) + b_ref[...]


def _adaln(cond8, w, b):
    n = w.shape[1]
    tn = 1536
    return pl.pallas_call(
        _adaln_kernel,
        out_shape=jax.ShapeDtypeStruct((8, n), F32),
        grid=(n // tn,),
        in_specs=[pl.BlockSpec((8, D_MODEL), lambda j: (0, 0)),
                  pl.BlockSpec((D_MODEL, tn), lambda j: (0, j)),
                  pl.BlockSpec((1, tn), lambda j: (0, j))],
        out_specs=pl.BlockSpec((8, tn), lambda j: (0, j)),
        compiler_params=_params(("arbitrary",)),
        name="adaln",
    )(cond8, w, b.reshape(1, n))


def _const_spec(shape):
    nd = len(shape)
    return pl.BlockSpec(shape, lambda i, _n=nd: (0,) * _n, pipeline_mode=pl.Buffered(1))


def _mod_spec(chunk, rowfn):
    return pl.BlockSpec((1, 1, D_MODEL), lambda i: (rowfn(i) * ADA_CHUNKS + chunk, 0, 0))


def _rowfn(latent):
    if latent:
        return lambda i: 1 + (i * TOKEN_TILE) // LAT_LEN
    return lambda i: 0


def _tok_spec(width, blk0=0):
    return pl.BlockSpec((TOKEN_TILE, width), lambda i: (i + blk0, 0))


def _seq_len(latent):
    return LAT_LEN if latent else CTX_LEN


def _transposed_out(t, latent, rows):
    seq = _seq_len(latent)
    shape = jax.ShapeDtypeStruct((t // seq, rows, seq), F32)
    if seq <= TOKEN_TILE:
        per = TOKEN_TILE // seq
        return shape, pl.BlockSpec((per, rows, seq), lambda i: (i, 0, 0))
    per = seq // TOKEN_TILE
    return shape, pl.BlockSpec((1, rows, TOKEN_TILE), lambda i: (i // per, 0, i % per))


def _store_transposed(ref, x):
    n, _, w = ref.shape
    for j in range(n):
        ref[j] = x[j * w:(j + 1) * w].T


def _dot_colsT(w_cols, h):
    return lax.dot_general(w_cols, h, (((0,), (1,)), ((), ())), preferred_element_type=F32)


def _store_proj_cols_transposed(ref, w_cols, h):
    n, _, w = ref.shape
    for j in range(n):
        ref[j] = _dot_colsT(w_cols, h[j * w:(j + 1) * w])


def _modnorm(x, ng_ref, sh_ref, sc_ref):
    return _rms(x, ng_ref[...]) * (1.0 + sc_ref[0]) + sh_ref[0]


def _load_weights_bf16(pairs, stage, sem):
    tr, tc = W_TILE
    tiles = [(src, dst, r, c) for src, dst in pairs
             for r in range(0, src.shape[0], tr) for c in range(0, src.shape[1], tc)]

    def copy(k):
        src, _, r, c = tiles[k]
        return pltpu.make_async_copy(src.at[r:r + tr, c:c + tc], stage.at[k % W_SLOTS], sem.at[k % W_SLOTS])

    for k in range(min(W_SLOTS, len(tiles))):
        copy(k).start()
    for k, (_, dst, r, c) in enumerate(tiles):
        copy(k).wait()
        dst[r:r + tr, c:c + tc] = stage[k % W_SLOTS].astype(BF16)
        if k + W_SLOTS < len(tiles):
            copy(k + W_SLOTS).start()


def _ffn_kernel(*refs, n_ctx, first, has_mix, last):
    it = iter(refs)
    xs = (next(it), next(it)) if first else (next(it),)
    if has_mix:
        yac_ref, ybc_ref, yal_ref, ybl_ref, wo_hbm, g2_ref = (next(it) for _ in range(6))
    ng_ref, sh_ref, sc_ref, g_ref, win_hbm, wout_hbm = (next(it) for _ in range(6))
    if last:
        nf_ref = next(it)
    outs = (next(it), next(it)) if last else (next(it),)
    win_s, wout_s = next(it), next(it)
    if has_mix:
        wo_s = next(it)
    stage, sem = next(it), next(it)

    step = pl.program_id(0)
    is_lat = step >= n_ctx

    @pl.when(step == 0)
    def _():
        pairs = [(win_hbm, win_s), (wout_hbm, wout_s)]
        if has_mix:
            pairs.insert(0, (wo_hbm, wo_s))
        _load_weights_bf16(pairs, stage, sem)

    pick = lambda c_ref, l_ref: jnp.where(is_lat, l_ref[...], c_ref[...])
    x = pick(*xs) if first else xs[0][...]
    if has_mix:
        half = wo_s.shape[0] // 2
        mo = _dot(pick(yac_ref, yal_ref), wo_s[:half, :]) + _dot(pick(ybc_ref, ybl_ref), wo_s[half:, :])
        x = x + g2_ref[0] * mo
    h = _modnorm(x, ng_ref, sh_ref, sc_ref).astype(BF16)
    acc = None
    for lo in range(0, D_FF, FFN_CHUNK):
        hi = min(lo + FFN_CHUNK, D_FF)
        g = _dot(h, win_s[:, lo:hi])
        u = _dot(h, win_s[:, D_FF + lo:D_FF + hi])
        part = _dot(_silu(g) * u, wout_s[lo:hi, :])
        acc = part if acc is None else acc + part
    y = x + (0.5 * g_ref[0]) * acc
    if last:
        y = _rms(y, nf_ref[...])
        oc_ref, ol_ref = outs

        @pl.when(jnp.logical_not(is_lat))
        def _():
            oc_ref[...] = y
            ol_ref[...] = jnp.zeros_like(ol_ref)

        @pl.when(is_lat)
        def _():
            ol_ref[...] = y
    else:
        outs[0][...] = y


def _ffn(x, mods, mod_base, ng, w_in, w_out, *, t_ctx, t_lat, mix=None, final_g=None, tile=TOKEN_TILE):
    n_ctx, n_lat = t_ctx // tile, t_lat // tile
    first, last = isinstance(x, tuple), final_g is not None
    ctx_map = lambda i: (jnp.minimum(i, n_ctx - 1), 0)
    lat_map = lambda i: (jnp.maximum(i - n_ctx, 0), 0)
    rowfn = lambda i: jnp.where(i < n_ctx, 0, 1 + ((i - n_ctx) * tile) // LAT_LEN)
    two = lambda wd: [pl.BlockSpec((tile, wd), ctx_map), pl.BlockSpec((tile, wd), lat_map)]
    one = pl.BlockSpec((tile, D_MODEL), lambda i: (i, 0))
    hbm = pl.BlockSpec(memory_space=pl.ANY)
    args = list(x) if first else [x]
    specs = two(D_MODEL) if first else [one]
    scratch = [pltpu.VMEM(w_in.shape, BF16), pltpu.VMEM(w_out.shape, BF16)]
    if mix is not None:
        yac, ybc, yal, ybl, wo = mix
        args += [yac, ybc, yal, ybl, wo, mods]
        mw = yac.shape[1]
        specs += [pl.BlockSpec((tile, mw), ctx_map)] * 2 + [pl.BlockSpec((tile, mw), lat_map)] * 2
        specs += [hbm, _mod_spec(mod_base - 1, rowfn)]
        scratch.append(pltpu.VMEM(wo.shape, BF16))
    args += [ng.reshape(1, D_MODEL), mods, mods, mods, w_in, w_out]
    specs += [_const_spec((1, D_MODEL)), _mod_spec(mod_base, rowfn), _mod_spec(mod_base + 1, rowfn),
              _mod_spec(mod_base + 2, rowfn), hbm, hbm]
    if last:
        args.append(final_g.reshape(1, D_MODEL))
        specs.append(_const_spec((1, D_MODEL)))
        out_shape = [jax.ShapeDtypeStruct((t_ctx, D_MODEL), F32), jax.ShapeDtypeStruct((t_lat, D_MODEL), F32)]
        out_specs = two(D_MODEL)
    else:
        out_shape = jax.ShapeDtypeStruct((t_ctx + t_lat, D_MODEL), F32)
        out_specs = one
    scratch += [pltpu.VMEM((W_SLOTS,) + W_TILE, F32), pltpu.SemaphoreType.DMA((W_SLOTS,))]
    return pl.pallas_call(
        functools.partial(_ffn_kernel, n_ctx=n_ctx, first=first, has_mix=mix is not None, last=last),
        out_shape=out_shape,
        grid=(n_ctx + n_lat,),
        in_specs=specs,
        out_specs=out_specs,
        scratch_shapes=scratch,
        compiler_params=_params(("arbitrary",), FFN_VMEM_LIMIT),
        name="ffn",
    )(*args)


_EV_QA, _EV_KA, _EV_VA, _EV_QB, _EV_KB, _EV_VB, _EV_OB, _EV_GI, _EV_GF, _EV_END = (
    0, 512, 640, 768, 1280, 1792, 2304, 2816, 2944, 3072)
N_DIRHEAD = 2 * N_HEADS


def _proj_even_kernel(*refs, rope):
    it = iter(refs)
    x_ref, ng_ref, sh_ref, sc_ref, w_ref, gbi_ref, gbf_ref, gbt_ref, qg_ref, kg_ref, bd_ref = (
        next(it) for _ in range(11))
    if rope:
        cos_ref, sin_ref = next(it), next(it)
    qa_ref, kat_ref, vat_ref, qb_ref, kb_ref, vb_ref, ob_ref, gi_ref, gf_ref, gt_ref, kbt_ref = (next(it) for _ in range(11))

    h = _modnorm(x_ref[...], ng_ref, sh_ref, sc_ref).astype(BF16)
    proj = lambda lo, hi: _dot(h, w_ref[:, lo:hi])

    qa = proj(_EV_QA, _EV_KA)
    qa = qa * lax.rsqrt(_head_mean_sq(qa, bd_ref) + EPS) * qg_ref[...]
    ka = proj(_EV_KA, _EV_VA)
    ka = ka * lax.rsqrt(_head_mean_sq(ka, bd_ref) + EPS) * kg_ref[...]
    if rope:
        qa = _rope(qa, cos_ref[...], sin_ref[...])
        ka = _rope(ka, cos_ref[...], sin_ref[...])
    qa_ref[...] = qa * ATTN_SCALE
    _store_transposed(kat_ref, ka)
    _store_proj_cols_transposed(vat_ref, w_ref[:, _EV_VA:_EV_QB], h)
    qb_ref[...] = proj(_EV_QB, _EV_KB)
    kb_ref[...] = proj(_EV_KB, _EV_VB) * ATTN_SCALE
    vb_ref[...] = proj(_EV_VB, _EV_OB)
    ob_ref[...] = proj(_EV_OB, _EV_GI)
    gi_ref[...] = proj(_EV_GI, _EV_GF) + gbi_ref[...]
    gf_ref[...] = proj(_EV_GF, _EV_END) + gbf_ref[...]
    for j in range(TOKEN_TILE // MLSTM_T):
        ht = h[j * MLSTM_T:(j + 1) * MLSTM_T]
        gates_t = [_dot_colsT(w_ref[:, c0:c0 + N_DIRHEAD], ht) for c0 in (_EV_GI, _EV_GF)]
        gt_ref[j] = jnp.concatenate(gates_t, axis=0) + gbt_ref[...]
        kbt_ref[j] = _dot_colsT(w_ref[:, _EV_KB:_EV_VB], ht) * ATTN_SCALE


def _proj_even(x, t, blk0, mods, latent, ng, w, gbi, gbf, gbt, qg, kg, bd, rope_tabs):
    rowfn = _rowfn(latent)
    args = [x, ng.reshape(1, D_MODEL), mods, mods, w, gbi, gbf, gbt, qg, kg, bd]
    specs = [_tok_spec(D_MODEL, blk0), _const_spec((1, D_MODEL)), _mod_spec(3, rowfn), _mod_spec(4, rowfn)]
    specs += [_const_spec(a.shape) for a in args[4:]]
    if latent:
        nblk = LAT_LEN // TOKEN_TILE
        args += list(rope_tabs)
        specs += [pl.BlockSpec((TOKEN_TILE, 128), lambda i: (i % nblk, 0))] * 2
    widths = (512, 512, 512, 512, 512, 128, 128)
    out_shape = [jax.ShapeDtypeStruct((t, wd), F32) for wd in widths]
    out_specs = [_tok_spec(wd) for wd in widths]
    for pos in (1, 2):
        shape, spec = _transposed_out(t, latent, 128)
        out_shape.insert(pos, shape)
        out_specs.insert(pos, spec)
    nj = TOKEN_TILE // MLSTM_T
    for rows in (2 * N_DIRHEAD, 512):
        out_shape.append(jax.ShapeDtypeStruct((t // MLSTM_T, rows, MLSTM_T), F32))
        out_specs.append(pl.BlockSpec((nj, rows, MLSTM_T), lambda i: (i, 0, 0)))
    return pl.pallas_call(
        functools.partial(_proj_even_kernel, rope=latent),
        out_shape=out_shape, grid=(t // TOKEN_TILE,), in_specs=specs, out_specs=out_specs,
        compiler_params=_params(("arbitrary",)), name="proj_even",
    )(*args)


def _proj_odd_kernel(*refs, rope):
    it = iter(refs)
    x_ref, ng_ref, sh_ref, sc_ref, w_ref = (next(it) for _ in range(5))
    if rope:
        cos_ref, sin_ref = next(it), next(it)
    qc_ref, kc_ref, vc_ref, qd_ref, kd_ref, vd_ref = (next(it) for _ in range(6))

    h = _modnorm(x_ref[...], ng_ref, sh_ref, sc_ref).astype(BF16)
    proj = lambda lo, hi: _dot(h, w_ref[:, lo:hi])
    qc_ref[...] = proj(0, 512) * ATTN_SCALE
    if rope:
        kc_ref[...] = proj(512, 1024)
        vc_ref[...] = proj(1024, 1536)
    else:
        _store_proj_cols_transposed(kc_ref, w_ref[:, 512:1024], h)
        _store_proj_cols_transposed(vc_ref, w_ref[:, 1024:1536], h)
    qd = proj(1536, 2048)
    if rope:
        qd = _rope(qd, cos_ref[...], sin_ref[...])
        _store_transposed(kd_ref, _rope(proj(2048, 2176), cos_ref[...], sin_ref[...]))
    else:
        _store_proj_cols_transposed(kd_ref, w_ref[:, 2048:2176], h)
    _store_proj_cols_transposed(vd_ref, w_ref[:, 2176:2304], h)
    qd_ref[...] = qd * ATTN_SCALE


def _proj_odd(x, t, blk0, mods, latent, ng, w, rope_tabs):
    rowfn = _rowfn(latent)
    args = [x, ng.reshape(1, D_MODEL), mods, mods, w]
    specs = [_tok_spec(D_MODEL, blk0), _const_spec((1, D_MODEL)), _mod_spec(3, rowfn), _mod_spec(4, rowfn),
             _const_spec(w.shape)]
    if latent:
        nblk = LAT_LEN // TOKEN_TILE
        args += list(rope_tabs)
        specs += [pl.BlockSpec((TOKEN_TILE, 128), lambda i: (i % nblk, 0))] * 2
    out_shape = [jax.ShapeDtypeStruct((t, 512), F32)] * 4
    out_specs = [_tok_spec(512)] * 4
    if not latent:
        out_shape[1], out_specs[1] = _transposed_out(t, latent, 512)
        out_shape[2], out_specs[2] = out_shape[1], out_specs[1]
    for _ in range(2):
        shape, spec = _transposed_out(t, latent, 128)
        out_shape.append(shape)
        out_specs.append(spec)
    return pl.pallas_call(
        functools.partial(_proj_odd_kernel, rope=latent),
        out_shape=out_shape,
        grid=(t // TOKEN_TILE,), in_specs=specs, out_specs=out_specs,
        compiler_params=_params(("arbitrary",)), name="proj_odd",
    )(*args)


def _attn_kernel(*refs, group, kv_t, has_extra, has_sink, spb):
    it = iter(refs)
    q_ref, k_ref, v_ref = next(it), next(it), next(it)
    if has_extra:
        ke_ref, ve_ref = next(it), next(it)
    if has_sink:
        sink_ref = next(it)
    o_ref = next(it)
    tq = q_ref.shape[0] // spb
    first = lax.broadcasted_iota(jnp.int32, (tq, 128), 1) < DH

    def kv_operands(kt_ref, vt_ref, smp, kv):
        kt = kt_ref[smp, kv * DH:(kv + 1) * DH, :].astype(BF16)
        vt = vt_ref[smp, kv * DH:(kv + 1) * DH, :].astype(BF16)
        z = jnp.zeros_like(kt)
        ones = jnp.ones((128, kt.shape[1]), BF16)
        return ((jnp.concatenate([kt, z], axis=0), jnp.concatenate([z, kt], axis=0)),
                (jnp.concatenate([vt, z, ones], axis=0), jnp.concatenate([z, vt, ones], axis=0)))

    for smp in range(spb):
        qrows = slice(smp * tq, (smp + 1) * tq)
        cache = {}
        for j in range(N_HEADS // 2):
            ps = slice(128 * j, 128 * (j + 1))
            q_pair = q_ref[qrows, ps]
            if kv_t:
                qb = q_pair.astype(BF16)
                ops = []
                for half in range(2):
                    kv = (2 * j + half) // group
                    if kv not in cache:
                        cache[kv] = (kv_operands(k_ref, v_ref, smp, kv),
                                     kv_operands(ke_ref, ve_ref, smp, kv) if has_extra else None)
                    ops.append(cache[kv])
            else:
                keys = k_ref.shape[0] // spb
                krows = slice(smp * keys, (smp + 1) * keys)
                k_pair = k_ref[krows, ps].astype(BF16)
                v_aug = jnp.concatenate([v_ref[krows, ps].astype(BF16), jnp.ones((keys, 128), BF16)], axis=1)
            res, sink_den = [], []
            for half in range(2):
                if kv_t:
                    (k_ops, v_ops), extra = ops[half]
                    s = jnp.dot(qb, k_ops[half], preferred_element_type=F32)
                else:
                    s = _dot_nt(jnp.where(first, q_pair, 0.0) if half == 0 else jnp.where(first, 0.0, q_pair), k_pair)
                m = jnp.max(s, axis=-1, keepdims=True)
                if has_extra:
                    se = jnp.dot(qb, extra[0][half], preferred_element_type=F32)
                    m = jnp.maximum(m, jnp.max(se, axis=-1, keepdims=True))
                if has_sink:
                    m = jnp.maximum(m, sink_ref[2 * j + half])
                    sink_den.append(jnp.exp(sink_ref[2 * j + half] - m))
                p = jnp.exp(s - m)
                r = _dot_nt(p, v_ops[half]) if kv_t else _dot(p, v_aug)
                if has_extra:
                    r = r + _dot_nt(jnp.exp(se - m), extra[1][half])
                res.append(r)
            num = jnp.where(first, res[0][:, :128], res[1][:, :128])
            den = jnp.where(first, res[0][:, 128:], res[1][:, 128:])
            if has_sink:
                den = den + jnp.where(first, sink_den[0], sink_den[1])
            o_ref[qrows, ps] = num / den


def _attention(q, k, v, *, n_batch, q_len, q_tile, group, kv_t, extra=None, sink=None):
    nq = q_len // q_tile
    spb = next(n for n in (4, 2, 1) if n_batch % n == 0) if nq == 1 else 1
    args = [q, k, v]
    specs = [pl.BlockSpec((spb * q_tile, 512), lambda b, j: (b * nq + j, 0))]
    if kv_t:
        kv_spec = lambda a: pl.BlockSpec((spb,) + a.shape[1:], lambda b, j: (b, 0, 0))
        specs += [kv_spec(k), kv_spec(v)]
        if extra is not None:
            args += list(extra)
            specs += [kv_spec(extra[0]), kv_spec(extra[1])]
    else:
        assert extra is None
        keys = k.shape[0] // n_batch
        specs += [pl.BlockSpec((spb * keys, 512), lambda b, j: (b, 0))] * 2
    if sink is not None:
        args.append(sink)
        specs.append(pl.BlockSpec(memory_space=pltpu.SMEM))
    return pl.pallas_call(
        functools.partial(_attn_kernel, group=group, kv_t=kv_t, has_extra=extra is not None,
                          has_sink=sink is not None, spb=spb),
        out_shape=jax.ShapeDtypeStruct(q.shape, F32),
        grid=(n_batch // spb, nq), in_specs=specs,
        out_specs=pl.BlockSpec((spb * q_tile, 512), lambda b, j: (b * nq + j, 0)),
        compiler_params=_params(("arbitrary", "arbitrary")), name="attn",
    )(*args)


LOG2E = 1.4426950408889634


def _cummax_time(x, reverse):
    n = x.shape[0]
    row = lax.broadcasted_iota(jnp.int32, x.shape, 0)
    k = 1
    while k < n:
        if reverse:
            shifted = jnp.where(row < n - k, pltpu.roll(x, n - k, 0), -jnp.inf)
        else:
            shifted = jnp.where(row >= k, pltpu.roll(x, k, 0), -jnp.inf)
        x = jnp.maximum(x, shifted)
        k *= 2
    return x


def _mlstm_kernel(*refs, n_chunks, carry, spb):
    it = iter(refs)
    q_ref, k_ref, kt_ref, v_ref, ob_ref, gi_ref, gf_ref, gt_ref, hg_ref, bd_ref = (next(it) for _ in range(10))
    if carry:
        c0_ref, n0_ref, m0r_ref, m0c_ref = (next(it) for _ in range(4))
    y_ref = next(it)
    if not carry:
        c_ref, n_ref, m_ref = next(it), next(it), next(it)
    hf_s, hb_s = next(it), next(it)
    if carry:
        st_s, mrow_s, mcol_s = next(it), next(it), next(it)
    T = MLSTM_T
    step = pl.program_id(1)

    if carry:
        @pl.when(step == 0)
        def _():
            z = jnp.zeros((DH, DH), F32)
            for d in range(2):
                for j in range(N_HEADS // 2):
                    ca, cb = c0_ref[0, d, 2 * j], c0_ref[0, d, 2 * j + 1]
                    na, nb = n0_ref[0, d, 2 * j][:, :DH], n0_ref[0, d, 2 * j + 1][:, :DH]
                    st_s[d, j, 0:DH, :] = jnp.concatenate([ca, z, na, z], axis=1)
                    st_s[d, j, DH:2 * DH, :] = jnp.concatenate([z, cb, z, nb], axis=1)
            mrow_s[...] = m0r_ref[0]
            mcol_s[...] = m0c_ref[0]

    row = lax.broadcasted_iota(jnp.int32, (T, T), 0)
    col = lax.broadcasted_iota(jnp.int32, (T, T), 1)
    lower = row >= col
    upper = row <= col
    lower_m = jnp.where(lower, 1.0, 0.0).astype(BF16)
    upper_m = jnp.where(upper, 1.0, 0.0).astype(BF16)
    lane = lax.broadcasted_iota(jnp.int32, (T, 128), 1)
    first = lane < DH
    lane_c = lax.broadcasted_iota(jnp.int32, (DH, 256), 1) % 128
    zeros_kt = jnp.zeros((DH, T), BF16)
    ones_v = jnp.ones((T, 128), BF16)

    for smp, d in ((s_, d_) for s_ in range(spb) for d_ in range(2)):
        chunk = (step if d == 0 else n_chunks - 1 - step) if carry else smp
        rows = pl.ds(pl.multiple_of(chunk * T, T), T)
        hd0 = N_HEADS * d
        mask = lower if d == 0 else upper

        if carry or d == 0:
            gi = gi_ref[rows, :]
            lf = _log_sigmoid(gf_ref[rows, :])
            prefix = _dot_exact_rhs(lower_m, lf)
        b_cols = prefix if d == 0 else prefix[T - 1:T, :] - prefix + lf
        a_cols = gi - b_cols
        dmax = b_cols + _cummax_time(a_cols, reverse=d == 1)
        if carry:
            inter = b_cols + mrow_s[0:1, :]
            mt = jnp.maximum(inter, dmax)
            w_inter = jnp.exp(inter - mt)
        else:
            mt = jnp.maximum(b_cols, dmax)
        c2 = (b_cols - mt) * LOG2E
        einv = jnp.exp(-mt)

        gt = gt_ref[chunk]
        gi_t = gt[hd0:hd0 + N_HEADS, :]
        gf_t = gt[N_DIRHEAD + hd0:N_DIRHEAD + hd0 + N_HEADS, :]
        lf_t = _log_sigmoid(gf_t)
        b_rows = _dot_exact_lhs(lf_t, upper_m)
        if d == 1:
            b_rows = b_rows[:, T - 1:T] - b_rows + lf_t
        a_rows = gi_t - b_rows
        a2 = a_rows * LOG2E
        b_last = b_rows[:, T - 1:T] if d == 0 else b_rows[:, 0:1]
        dec_rows = b_last + a_rows
        m_new = jnp.max(dec_rows, axis=-1, keepdims=True)
        if carry:
            m_prev = mcol_s[hd0:hd0 + N_HEADS, 0:1]
            m_new = jnp.maximum(m_new, b_last + m_prev)
            wc = jnp.exp(b_last + m_prev - m_new)
        else:
            m_new = jnp.maximum(m_new, b_last)
        ws_rows = jnp.exp(dec_rows - m_new)

        h_s = hf_s if d == 0 else hb_s
        for j in range(N_HEADS // 2):
            ps = slice(128 * j, 128 * (j + 1))
            q_pair = q_ref[rows, ps].astype(BF16)
            v_aug = jnp.concatenate([v_ref[rows, ps].astype(BF16), ones_v], axis=1)
            kt = (kt_ref[chunk, 128 * j:128 * j + DH, :], kt_ref[chunk, 128 * j + DH:128 * (j + 1), :])
            lhs = []
            for half in range(2):
                hd = 2 * j + half
                kt0 = jnp.concatenate([kt[0].astype(BF16), zeros_kt] if half == 0 else [zeros_kt, kt[1].astype(BF16)], axis=0)
                qk = jnp.dot(q_pair, kt0, preferred_element_type=F32)
                e = c2[:, hd0 + hd:hd0 + hd + 1] + a2[hd:hd + 1, :]
                lhs.append((jnp.exp2(jnp.where(mask, e, -jnp.inf)) * qk).astype(BF16))
            for half in range(2):
                lhs.append((kt[half] * ws_rows[2 * j + half:2 * j + half + 1, :]).astype(BF16))
            both = jnp.dot(jnp.concatenate(lhs, axis=0), v_aug, preferred_element_type=F32)
            res = (both[0:T], both[T:2 * T])
            cu = (both[2 * T:2 * T + DH], both[2 * T + DH:2 * T + 2 * DH])
            ra, rb = hd0 + 2 * j, hd0 + 2 * j + 1
            num = jnp.where(first, res[0][:, :128], res[1][:, :128])
            den = jnp.where(first, res[0][:, 128:], res[1][:, 128:])
            if carry:
                old = st_s[d, j]
                qs = _dot(q_pair, old)
                wi = jnp.where(first, w_inter[:, ra:ra + 1], w_inter[:, rb:rb + 1])
                num = num + wi * qs[:, :128]
                den = den + wi * qs[:, 128:]
            floor = jnp.where(first, einv[:, ra:ra + 1], einv[:, rb:rb + 1])
            h_s[rows, ps] = num / jnp.maximum(jnp.abs(den), floor)
            if carry:
                st_s[d, j, 0:DH, :] = jnp.where(lane_c < DH, cu[0], 0.0) + wc[2 * j:2 * j + 1, :] * old[0:DH]
                st_s[d, j, DH:2 * DH, :] = jnp.where(lane_c >= DH, cu[1], 0.0) + wc[2 * j + 1:2 * j + 2, :] * old[DH:2 * DH]
            else:
                c_ref[smp, d, 2 * j] = cu[0][:, 0:DH]
                c_ref[smp, d, 2 * j + 1] = cu[1][:, DH:2 * DH]

        if carry:
            b_last_l = b_cols[T - 1:T, :] if d == 0 else b_cols[0:1, :]
            m_prev_l = mrow_s[0:1, :]
            m_new_l = jnp.maximum(jnp.max(b_last_l + a_cols, axis=0, keepdims=True), b_last_l + m_prev_l)
            lane_r = lax.broadcasted_iota(jnp.int32, (1, 128), 1)
            mine = (lane_r >= hd0) & (lane_r < hd0 + N_HEADS)
            mrow_s[0:1, :] = jnp.where(mine, m_new_l, m_prev_l)
            mcol_s[hd0:hd0 + N_HEADS, :] = jnp.broadcast_to(m_new, (N_HEADS, 128))
        else:
            n_all = _dot(ws_rows, k_ref[rows, :])
            for hd in range(N_HEADS):
                n_ref[smp, d, hd:hd + 1, :] = n_all[hd:hd + 1, hd * DH:(hd + 1) * DH]
            m_ref[smp, hd0:hd0 + N_HEADS, :] = jnp.broadcast_to(m_new, (N_HEADS, 128))

    @pl.when(step == n_chunks - 1)
    def _():
        hsum = hf_s[...] + hb_s[...]
        y = hsum * lax.rsqrt(_head_mean_sq(hsum, bd_ref) + EPS) * hg_ref[...]
        y_ref[...] = jax.nn.sigmoid(ob_ref[...]) * y


def _mlstm(qb, kb, kbt, vb, ob, gi, gf, gt, head_g, bd, *, n_batch, seq_len, state=None):
    n_chunks = seq_len // MLSTM_T
    carry = state is not None
    assert carry or n_chunks == 1
    spb = 1
    seq = lambda wd: pl.BlockSpec((spb * seq_len, wd), lambda b, c: (b, 0))
    per_chunk = lambda r: pl.BlockSpec((spb * n_chunks, r, MLSTM_T), lambda b, c: (b, 0, 0))
    args = [qb, kb, kbt, vb, ob, gi, gf, gt, head_g, bd]
    specs = [seq(512), seq(512), per_chunk(512), seq(512), seq(512), seq(128), seq(128), per_chunk(2 * N_DIRHEAD),
             pl.BlockSpec((1, 512), lambda b, c: (0, 0)), pl.BlockSpec(bd.shape, lambda b, c: (0, 0))]
    out_shape = [jax.ShapeDtypeStruct((n_batch * seq_len, 512), F32)]
    out_specs = [seq(512)]
    scratch = [pltpu.VMEM((spb * seq_len, 512), F32), pltpu.VMEM((spb * seq_len, 512), F32)]
    if carry:
        args += list(state)
        specs += [pl.BlockSpec((1, 2, N_HEADS, DH, DH), lambda b, c: (b, 0, 0, 0, 0)),
                  pl.BlockSpec((1, 2, N_HEADS, DH, 128), lambda b, c: (b, 0, 0, 0, 0)),
                  pl.BlockSpec((1, 8, 128), lambda b, c: (b, 0, 0)),
                  pl.BlockSpec((1, N_DIRHEAD, 128), lambda b, c: (b, 0, 0))]
        scratch += [pltpu.VMEM((2, N_HEADS // 2, 128, 256), F32), pltpu.VMEM((8, 128), F32),
                    pltpu.VMEM((N_DIRHEAD, 128), F32)]
    else:
        out_shape += [jax.ShapeDtypeStruct((n_batch, 2, N_HEADS, DH, DH), F32),
                      jax.ShapeDtypeStruct((n_batch, 2, N_HEADS, DH), F32),
                      jax.ShapeDtypeStruct((n_batch, N_DIRHEAD, 128), F32)]
        out_specs += [pl.BlockSpec((spb, 2, N_HEADS, DH, DH), lambda b, c: (b, 0, 0, 0, 0)),
                      pl.BlockSpec((spb, 2, N_HEADS, DH), lambda b, c: (b, 0, 0, 0)),
                      pl.BlockSpec((spb, N_DIRHEAD, 128), lambda b, c: (b, 0, 0))]
    return pl.pallas_call(
        functools.partial(_mlstm_kernel, n_chunks=n_chunks, carry=carry, spb=spb),
        out_shape=out_shape, grid=(n_batch // spb, n_chunks), in_specs=specs, out_specs=out_specs,
        scratch_shapes=scratch,
        compiler_params=_params(("arbitrary", "arbitrary")), name="mlstm",
    )(*args)


N_ROWS = LAT_LEN // GRID_W
N_DY = 2 * NA_KH - 1
N_DX = 2 * NA_KW - 1


NA_QROWS = 4
NA_KROWS = 12
NA_GROUPS = N_ROWS // NA_QROWS


def _na_kernel(q_ref, k_ref, v_ref, ke_ref, ve_ref, rpb_ref, o_ref, tile_s, slab_s):
    pair = pl.program_id(1)
    qn, kn = NA_QROWS * GRID_W, NA_KROWS * GRID_W

    @pl.when(pl.program_id(0) == 0)
    def _():
        qc = lax.broadcasted_iota(jnp.int32, (GRID_W, GRID_W), 0)
        kc = lax.broadcasted_iota(jnp.int32, (GRID_W, GRID_W), 1)
        start = jnp.clip(qc - NA_KW // 2, 0, GRID_W - NA_KW)
        in_win = (kc >= start) & (kc < start + NA_KW)
        dx = kc - qc + (NA_KW - 1)
        blocked = jnp.full((GRID_W, GRID_W), NEG_INF, F32)
        for half in range(2):
            head = 2 * pair + half

            def build_tile(dy, carry):
                tile = blocked
                for j in range(N_DX):
                    tile = jnp.where(dx == j, rpb_ref[(head * N_DY + dy) * N_DX + j], tile)
                tile_s[half * N_DY + dy] = jnp.where(in_win, tile, NEG_INF)
                return carry

            lax.fori_loop(0, N_DY, build_tile, 0)
            for kind in range(3):
                idx = (pair * 2 + half) * 3 + kind
                for i in range(NA_QROWS):
                    for k in range(NA_KROWS):
                        if kind == 0:
                            ok, dy = k < NA_KH, k - i + NA_KH - 1
                        elif kind == 1:
                            ok, dy = i <= k < i + NA_KH, k - i + NA_KH // 2 - 1
                        else:
                            ok, dy = k >= NA_KROWS - NA_KH, k - i - 1
                        slab_s[idx, i * GRID_W:(i + 1) * GRID_W, k * GRID_W:(k + 1) * GRID_W] = (
                            tile_s[half * N_DY + dy] if ok else blocked)

    first = lax.broadcasted_iota(jnp.int32, (qn, 128), 1) < DH
    ones_k = jnp.ones((kn, 128), BF16)
    ke = ke_ref[...].astype(BF16)
    ve_aug = jnp.concatenate([ve_ref[...].astype(BF16), jnp.ones((ke.shape[0], 128), BF16)], axis=1)

    def group_body(g, carry):
        k0 = jnp.where(g < NA_GROUPS // 2, 0, N_ROWS - NA_KROWS)
        kind = jnp.where(g == 0, 0, jnp.where(g == NA_GROUPS - 1, 2, 1))
        qrows = pl.ds(pl.multiple_of(g * qn, qn), qn)
        keys = pl.ds(pl.multiple_of(k0 * GRID_W, GRID_W), kn)
        q_pair = q_ref[qrows, :]
        k_pair = k_ref[keys, :].astype(BF16)
        v_aug = jnp.concatenate([v_ref[keys, :].astype(BF16), ones_k], axis=1)
        res = []
        for half in range(2):
            qm = jnp.where(first, q_pair, 0.0) if half == 0 else jnp.where(first, 0.0, q_pair)
            s = _dot_nt(qm, k_pair) + slab_s[(pair * 2 + half) * 3 + kind]
            se = _dot_nt(qm, ke)
            m = jnp.maximum(jnp.max(s, axis=-1, keepdims=True), jnp.max(se, axis=-1, keepdims=True))
            res.append(_dot(jnp.exp(s - m), v_aug) + _dot(jnp.exp(se - m), ve_aug))
        num = jnp.where(first, res[0][:, :128], res[1][:, :128])
        den = jnp.where(first, res[0][:, 128:], res[1][:, 128:])
        o_ref[qrows, :] = num / den
        return carry

    lax.fori_loop(0, NA_GROUPS, group_body, 0, unroll=True)


def _na_latent(q, k, v, k_ctx, v_ctx, rpb_flat, n_batch):
    seq = pl.BlockSpec((LAT_LEN, 128), lambda b, j: (b, j))
    ctx = pl.BlockSpec((CTX_LEN, 128), lambda b, j: (b, j))
    return pl.pallas_call(
        _na_kernel,
        out_shape=jax.ShapeDtypeStruct(q.shape, F32),
        grid=(n_batch, N_HEADS // 2),
        in_specs=[seq, seq, seq, ctx, ctx, pl.BlockSpec(memory_space=pltpu.SMEM)],
        out_specs=seq,
        scratch_shapes=[pltpu.VMEM((2 * N_DY, GRID_W, GRID_W), F32),
                        pltpu.VMEM((N_HEADS * 3, NA_QROWS * GRID_W, NA_KROWS * GRID_W), F32)],
        compiler_params=_params(("arbitrary", "arbitrary")), name="na_latent",
    )(q, k, v, k_ctx, v_ctx, rpb_flat)


SWA_QT = 256
SWA_SPAN = SWA_QT + 2 * SWA_WIN


def _swa_kernel(q_ref, kt_ref, vt_ref, ket_ref, vet_ref, sink_ref, o_ref):
    pair = pl.program_id(1)
    first = lax.broadcasted_iota(jnp.int32, (SWA_QT, 128), 1) < DH
    row = lax.broadcasted_iota(jnp.int32, (SWA_QT, SWA_SPAN), 0)
    col = lax.broadcasted_iota(jnp.int32, (SWA_QT, SWA_SPAN), 1)
    z = jnp.zeros((DH, SWA_SPAN), BF16)
    ones = jnp.ones((128, SWA_SPAN), BF16)
    ket, vet = ket_ref[0].astype(BF16), vet_ref[0].astype(BF16)
    ze = jnp.zeros_like(ket)
    ones_e = jnp.ones((128, ket.shape[1]), BF16)
    sinks = (sink_ref[2 * pair], sink_ref[2 * pair + 1])
    for g in range(LAT_LEN // SWA_QT):
        q0 = g * SWA_QT
        lo = min(max(q0 - SWA_WIN, 0), LAT_LEN - SWA_SPAN)
        near = jnp.abs((lo + col) - (q0 + row)) <= SWA_WIN
        qb = q_ref[q0:q0 + SWA_QT, :].astype(BF16)
        kt = kt_ref[0, :, lo:lo + SWA_SPAN].astype(BF16)
        vt = vt_ref[0, :, lo:lo + SWA_SPAN].astype(BF16)
        res, sink_den = [], []
        for half in range(2):
            k_op = jnp.concatenate([kt, z] if half == 0 else [z, kt], axis=0)
            v_op = jnp.concatenate([vt, z, ones] if half == 0 else [z, vt, ones], axis=0)
            ke_op = jnp.concatenate([ket, ze] if half == 0 else [ze, ket], axis=0)
            ve_op = jnp.concatenate([vet, ze, ones_e] if half == 0 else [ze, vet, ones_e], axis=0)
            s = jnp.where(near, jnp.dot(qb, k_op, preferred_element_type=F32), NEG_INF)
            se = jnp.dot(qb, ke_op, preferred_element_type=F32)
            m = jnp.maximum(jnp.max(s, axis=-1, keepdims=True), jnp.max(se, axis=-1, keepdims=True))
            m = jnp.maximum(m, sinks[half])
            sink_den.append(jnp.exp(sinks[half] - m))
            res.append(_dot_nt(jnp.exp(s - m), v_op) + _dot_nt(jnp.exp(se - m), ve_op))
        num = jnp.where(first, res[0][:, :128], res[1][:, :128])
        den = jnp.where(first, res[0][:, 128:], res[1][:, 128:]) + jnp.where(first, sink_den[0], sink_den[1])
        o_ref[q0:q0 + SWA_QT, :] = num / den


def _swa_latent(q, kt, vt, kt_ctx, vt_ctx, sink, n_batch, group):
    ppk = group // 2
    kv_spec = lambda a: pl.BlockSpec((1, DH, a.shape[2]), lambda b, j: (b, j // ppk, 0))
    return pl.pallas_call(
        _swa_kernel,
        out_shape=jax.ShapeDtypeStruct(q.shape, F32),
        grid=(n_batch, N_HEADS // 2),
        in_specs=[pl.BlockSpec((LAT_LEN, 128), lambda b, j: (b, j)),
                  kv_spec(kt), kv_spec(vt), kv_spec(kt_ctx), kv_spec(vt_ctx),
                  pl.BlockSpec(memory_space=pltpu.SMEM)],
        out_specs=pl.BlockSpec((LAT_LEN, 128), lambda b, j: (b, j)),
        compiler_params=_params(("arbitrary", "arbitrary")), name="swa_latent",
    )(q, kt, vt, kt_ctx, vt_ctx, sink)


def _rope_tables():
    t = np.arange(LAT_LEN)
    pos = np.stack([t // GRID_W, t % GRID_W], axis=-1).astype(np.float32)
    freqs = np.float32(ROPE_THETA) ** (-np.arange(ROPE_FREQS, dtype=np.float32) / np.float32(ROPE_FREQS))
    ang = (pos[:, :, None] * freqs).reshape(LAT_LEN, 2 * ROPE_FREQS).astype(np.float32)
    cos, sin = np.cos(ang), np.sin(ang)
    cos_t = np.tile(np.concatenate([cos, cos], axis=-1), (1, 2))
    sin_t = np.tile(np.concatenate([-sin, sin], axis=-1), (1, 2))
    return jnp.asarray(cos_t, F32), jnp.asarray(sin_t, F32)


def kernel(x_prompt, x_sample, cache_l0_attn_k, cache_l0_attn_v, state_l0_mlstm_C, state_l0_mlstm_n, state_l0_mlstm_m, cache_l1_na_k, cache_l1_na_v, cache_l1_swa_k, cache_l1_swa_v, c, c_ctx, norm_final, ada_w_l0, ada_b_l0, norm_l0, ffn1_in_l0, ffn1_out_l0, ffn2_in_l0, ffn2_out_l0, mix_in_l0, mix_out_l0, qk_norm_l0, gate_bias_l0, head_norm_l0, ada_w_l1, ada_b_l1, norm_l1, ffn1_in_l1, ffn1_out_l1, ffn2_in_l1, ffn2_out_l1, mix_in_l1, mix_out_l1, rpb_l1, sink_l1):
    nb, nl = x_prompt.shape[0], x_sample.shape[0]
    bf = lambda a: a.astype(BF16)

    cond8 = jnp.zeros((8, D_MODEL), F32).at[0].set(c_ctx).at[1:1 + nl].set(c)
    mods0 = _adaln(cond8, ada_w_l0, ada_b_l0).reshape(8 * ADA_CHUNKS, 1, D_MODEL)
    mods1 = _adaln(cond8, ada_w_l1, ada_b_l1).reshape(8 * ADA_CHUNKS, 1, D_MODEL)

    g0 = 2304
    gcols = lambda j: mix_in_l0[:, g0 + 8 * j:g0 + 8 * (j + 1)]
    gpad = jnp.zeros((D_MODEL, 128 - N_DIRHEAD), F32)
    w_gi, w_gf = jnp.concatenate([gcols(0), gcols(2)], axis=1), jnp.concatenate([gcols(1), gcols(3)], axis=1)
    w_even = bf(jnp.concatenate([mix_in_l0[:, :g0], mix_in_l0[:, g0 + 32:], w_gi, gpad, w_gf, gpad], axis=1))
    gb4 = gate_bias_l0.reshape(4, N_HEADS)
    b_gi, b_gf = jnp.concatenate([gb4[0], gb4[2]]), jnp.concatenate([gb4[1], gb4[3]])
    gbi = jnp.zeros((1, 128), F32).at[0, :N_DIRHEAD].set(b_gi)
    gbf = jnp.zeros((1, 128), F32).at[0, :N_DIRHEAD].set(b_gf)
    gbt = jnp.concatenate([b_gi, b_gf]).reshape(2 * N_DIRHEAD, 1)
    qg = jnp.tile(qk_norm_l0[0], N_HEADS).reshape(1, 512)
    kg = jnp.tile(qk_norm_l0[1], 2).reshape(1, 128)
    head_g = head_norm_l0.reshape(1, 512)
    grp = np.arange(256) // DH
    bd = jnp.asarray((grp[:, None] == grp[None, :]).astype(np.float32) / DH, dtype=BF16)
    rope_tabs = _rope_tables()
    w_odd = bf(mix_in_l1)
    rpb_flat = rpb_l1.reshape(-1)
    to_t = lambda a: a.reshape(a.shape[0], a.shape[1], -1).transpose(0, 2, 1)
    from_t = lambda a: a.reshape(a.shape[0], -1, DH, a.shape[2]).transpose(0, 3, 1, 2)

    xp = x_prompt.reshape(nb * CTX_LEN, D_MODEL)
    xs = x_sample.reshape(nl * LAT_LEN, D_MODEL)
    t_ctx, t_lat = xp.shape[0], xs.shape[0]
    streams = ((False, t_ctx, 0, nb, CTX_LEN), (True, t_lat, t_ctx // TOKEN_TILE, nl, LAT_LEN))
    ffn = functools.partial(_ffn, t_ctx=t_ctx, t_lat=t_lat)

    x = ffn((xp, xs), mods0, 0, norm_l0[0], ffn1_in_l0, ffn1_out_l0)
    ya, yb = {}, {}
    for latent, t, blk0, n_batch, seq_len in streams:
        qa, kat, vat, qb, kb, vb, ob, gi, gf, gt, kbt = _proj_even(
            x, t, blk0, mods0, latent, norm_l0[1], w_even, gbi, gbf, gbt, qg, kg, bd, rope_tabs)
        if latent:
            extra = (to_t(cache_l0_attn_k), to_t(cache_l0_attn_v))
            m0 = state_l0_mlstm_m.reshape(nl, N_DIRHEAD)
            m0_lanes = jnp.zeros((nl, 8, 128), F32).at[:, 0, :N_DIRHEAD].set(m0)
            m0_rows = jnp.broadcast_to(m0[:, :, None], (nl, N_DIRHEAD, 128))
            n0_cols = jnp.broadcast_to(state_l0_mlstm_n[..., None], (nl, 2, N_HEADS, DH, 128))
            state = (state_l0_mlstm_C, n0_cols, m0_lanes, m0_rows)
        else:
            extra, state = None, None
            k0t, v0t = kat, vat
        ya[latent] = _attention(qa, kat, vat, n_batch=n_batch, q_len=seq_len, q_tile=256, group=4, kv_t=True, extra=extra)
        ml = _mlstm(qb, kb, kbt, vb, ob, gi, gf, gt, head_g, bd, n_batch=n_batch, seq_len=seq_len, state=state)
        yb[latent] = ml[0]
        if not latent:
            c0, n0, m0_pad = ml[1:]
    x = ffn(x, mods0, 6, norm_l0[2], ffn2_in_l0, ffn2_out_l0, mix=(ya[False], yb[False], ya[True], yb[True], mix_out_l0))

    x = ffn(x, mods1, 0, norm_l1[0], ffn1_in_l1, ffn1_out_l1)
    yc, yd = {}, {}
    for latent, t, blk0, n_batch, seq_len in streams:
        qc, kc, vc, qd, kd, vd = _proj_odd(x, t, blk0, mods1, latent, norm_l1[1], w_odd, rope_tabs)
        if latent:
            yc[latent] = _na_latent(qc, kc, vc, cache_l1_na_k.reshape(nl * CTX_LEN, 512),
                                    cache_l1_na_v.reshape(nl * CTX_LEN, 512), rpb_flat, nl)
            yd[latent] = _swa_latent(qd, kd, vd, to_t(cache_l1_swa_k), to_t(cache_l1_swa_v), sink_l1, nl, group=4)
        else:
            yc[latent] = _attention(qc, kc, vc, n_batch=n_batch, q_len=seq_len, q_tile=256, group=1, kv_t=True)
            yd[latent] = _attention(qd, kd, vd, n_batch=n_batch, q_len=seq_len, q_tile=256, group=4, kv_t=True, sink=sink_l1)
            kc1t, vc1t, kd1t, vd1t = kc, vc, kd, vd
    y_prompt, y_sample = ffn(x, mods1, 6, norm_l1[2], ffn2_in_l1, ffn2_out_l1,
                             mix=(yc[False], yd[False], yc[True], yd[True], mix_out_l1), final_g=norm_final)

    return (y_prompt.reshape(nb, CTX_LEN, D_MODEL), y_sample.reshape(nl, LAT_LEN, D_MODEL),
            from_t(k0t), from_t(v0t),
            c0, n0, m0_pad[:, :, 0].reshape(nb, 2, N_HEADS),
            from_t(kc1t), from_t(vc1t),
            from_t(kd1t), from_t(vd1t))
```

```python
import functools

import jax
import jax.numpy as jnp
import numpy as np
from jax import lax
from jax.experimental import pallas as pl
from jax.experimental.pallas import tpu as pltpu

F32 = jnp.float32
BF16 = jnp.bfloat16

D_MODEL = 1024
DH = 64
D_FF = 2816
ADA_CHUNKS = 9
GRID_W = 64
LAT_LEN = 1024
CTX_LEN = 256
N_HEADS = 8
NA_KH = 8
NA_KW = 16
SWA_WIN = 128
ROPE_THETA = 10000.0
ROPE_FREQS = DH // 4
ATTN_SCALE = DH ** -0.5
NEG_INF = -1e30
EPS = 1e-6

TOKEN_TILE = 512
MLSTM_T = 256
FFN_CHUNK = 512
VMEM_LIMIT = 56 * 1024 * 1024
FFN_VMEM_LIMIT = 60 * 1024 * 1024
W_TILE = (256, 512)
W_SLOTS = 6


def _params(sem, vmem=VMEM_LIMIT):
    return pltpu.CompilerParams(dimension_semantics=sem, vmem_limit_bytes=vmem)


def _dot(a, b):
    return jnp.dot(a.astype(BF16), b.astype(BF16), preferred_element_type=F32)


def _dot_nt(a, b):
    return lax.dot_general(a.astype(BF16), b.astype(BF16), (((1,), (1,)), ((), ())),
                           preferred_element_type=F32)


def _split3(x):
    hi = x.astype(BF16)
    r1 = x - hi.astype(F32)
    mid = r1.astype(BF16)
    lo = (r1 - mid.astype(F32)).astype(BF16)
    return hi, mid, lo


def _dot_exact_rhs(a_bf16, x):
    hi, mid, lo = _split3(x)
    f = lambda p: jnp.dot(a_bf16, p, preferred_element_type=F32)
    return f(hi) + f(mid) + f(lo)


def _dot_exact_lhs(x, b_bf16):
    hi, mid, lo = _split3(x)
    f = lambda p: jnp.dot(p, b_bf16, preferred_element_type=F32)
    return f(hi) + f(mid) + f(lo)


def _silu(x):
    return x * jax.nn.sigmoid(x)


def _log_sigmoid(x):
    return jnp.minimum(x, 0.0) - jnp.log1p(jnp.exp(-jnp.abs(x)))


def _rms(x, g):
    return x * lax.rsqrt(jnp.mean(x * x, axis=-1, keepdims=True) + EPS) * g


def _head_mean_sq(x, bd_ref):
    n = x.shape[-1]
    sq = x * x
    hi = sq.astype(BF16)
    lo = (sq - hi.astype(F32)).astype(BF16)
    w = min(n, bd_ref.shape[0])
    bd = bd_ref[:w, :w]
    f = lambda p: jnp.dot(p, bd, preferred_element_type=F32)
    parts = [f(hi[:, c:c + w]) + f(lo[:, c:c + w]) for c in range(0, n, w)]
    return parts[0] if len(parts) == 1 else jnp.concatenate(parts, axis=1)


def _rope(x, cos, sin_signed):
    n = x.shape[-1]
    lane = lax.broadcasted_iota(jnp.int32, x.shape, 1)
    first_half = (lane % DH) < (DH // 2)
    partner = jnp.where(first_half, pltpu.roll(x, n - DH // 2, 1), pltpu.roll(x, DH // 2, 1))
    spread = lambda tab: tab if n == 128 else jnp.concatenate([tab] * (n // 128), axis=1)
    return x * spread(cos) + partner * spread(sin_signed)


def _adaln_kernel(c_ref, w_ref, b_ref, o_ref):
    s = _silu(c_ref[...])
    o_ref[...] = _dot(s, w_ref[...]) + b_ref[...]


def _adaln(cond8, w, b):
    n = w.shape[1]
    tn = 1536
    return pl.pallas_call(
        _adaln_kernel,
        out_shape=jax.ShapeDtypeStruct((8, n), F32),
        grid=(n // tn,),
        in_specs=[pl.BlockSpec((8, D_MODEL), lambda j: (0, 0)),
                  pl.BlockSpec((D_MODEL, tn), lambda j: (0, j)),
                  pl.BlockSpec((1, tn), lambda j: (0, j))],
        out_specs=pl.BlockSpec((8, tn), lambda j: (0, j)),
        compiler_params=_params(("arbitrary",)),
        name="adaln",
    )(cond8, w, b.reshape(1, n))


def _const_spec(shape):
    nd = len(shape)
    return pl.BlockSpec(shape, lambda i, _n=nd: (0,) * _n, pipeline_mode=pl.Buffered(1))


def _mod_spec(chunk, rowfn):
    return pl.BlockSpec((1, 1, D_MODEL), lambda i: (rowfn(i) * ADA_CHUNKS + chunk, 0, 0))


def _rowfn(latent):
    if latent:
        return lambda i: 1 + (i * TOKEN_TILE) // LAT_LEN
    return lambda i: 0


def _tok_spec(width, blk0=0):
    return pl.BlockSpec((TOKEN_TILE, width), lambda i: (i + blk0, 0))


def _seq_len(latent):
    return LAT_LEN if latent else CTX_LEN


def _transposed_out(t, latent, rows):
    seq = _seq_len(latent)
    shape = jax.ShapeDtypeStruct((t // seq, rows, seq), F32)
    if seq <= TOKEN_TILE:
        per = TOKEN_TILE // seq
        return shape, pl.BlockSpec((per, rows, seq), lambda i: (i, 0, 0))
    per = seq // TOKEN_TILE
    return shape, pl.BlockSpec((1, rows, TOKEN_TILE), lambda i: (i // per, 0, i % per))


def _store_transposed(ref, x):
    n, _, w = ref.shape
    for j in range(n):
        ref[j] = x[j * w:(j + 1) * w].T


def _dot_colsT(w_cols, h):
    return lax.dot_general(w_cols, h, (((0,), (1,)), ((), ())), preferred_element_type=F32)


def _store_proj_cols_transposed(ref, w_cols, h):
    n, _, w = ref.shape
    for j in range(n):
        ref[j] = _dot_colsT(w_cols, h[j * w:(j + 1) * w])


def _modnorm(x, ng_ref, sh_ref, sc_ref):
    return _rms(x, ng_ref[...]) * (1.0 + sc_ref[0]) + sh_ref[0]


def _load_weights_bf16(pairs, stage, sem):
    tr, tc = W_TILE
    tiles = [(src, dst, r, c) for src, dst in pairs
             for r in range(0, src.shape[0], tr) for c in range(0, src.shape[1], tc)]

    def copy(k):
        src, _, r, c = tiles[k]
        return pltpu.make_async_copy(src.at[r:r + tr, c:c + tc], stage.at[k % W_SLOTS], sem.at[k % W_SLOTS])

    for k in range(min(W_SLOTS, len(tiles))):
        copy(k).start()
    for k, (_, dst, r, c) in enumerate(tiles):
        copy(k).wait()
        dst[r:r + tr, c:c + tc] = stage[k % W_SLOTS].astype(BF16)
        if k + W_SLOTS < len(tiles):
            copy(k + W_SLOTS).start()


def _ffn_kernel(*refs, n_ctx, first, has_mix, last):
    it = iter(refs)
    xs = (next(it), next(it)) if first else (next(it),)
    if has_mix:
        yac_ref, ybc_ref, yal_ref, ybl_ref, wo_hbm, g2_ref = (next(it) for _ in range(6))
    ng_ref, sh_ref, sc_ref, g_ref, win_hbm, wout_hbm = (next(it) for _ in range(6))
    if last:
        nf_ref = next(it)
    outs = (next(it), next(it)) if last else (next(it),)
    win_s, wout_s = next(it), next(it)
    if has_mix:
        wo_s = next(it)
    stage, sem = next(it), next(it)

    step = pl.program_id(0)
    is_lat = step >= n_ctx

    @pl.when(step == 0)
    def _():
        pairs = [(win_hbm, win_s), (wout_hbm, wout_s)]
        if has_mix:
            pairs.insert(0, (wo_hbm, wo_s))
        _load_weights_bf16(pairs, stage, sem)

    pick = lambda c_ref, l_ref: jnp.where(is_lat, l_ref[...], c_ref[...])
    x = pick(*xs) if first else xs[0][...]
    if has_mix:
        half = wo_s.shape[0] // 2
        mo = _dot(pick(yac_ref, yal_ref), wo_s[:half, :]) + _dot(pick(ybc_ref, ybl_ref), wo_s[half:, :])
        x = x + g2_ref[0] * mo
    h = _modnorm(x, ng_ref, sh_ref, sc_ref).astype(BF16)
    acc = None
    for lo in range(0, D_FF, FFN_CHUNK):
        hi = min(lo + FFN_CHUNK, D_FF)
        g = _dot(h, win_s[:, lo:hi])
        u = _dot(h, win_s[:, D_FF + lo:D_FF + hi])
        part = _dot(_silu(g) * u, wout_s[lo:hi, :])
        acc = part if acc is None else acc + part
    y = x + (0.5 * g_ref[0]) * acc
    if last:
        y = _rms(y, nf_ref[...])
        oc_ref, ol_ref = outs

        @pl.when(jnp.logical_not(is_lat))
        def _():
            oc_ref[...] = y
            ol_ref[...] = jnp.zeros_like(ol_ref)

        @pl.when(is_lat)
        def _():
            ol_ref[...] = y
    else:
        outs[0][...] = y


def _ffn(x, mods, mod_base, ng, w_in, w_out, *, t_ctx, t_lat, mix=None, final_g=None, tile=TOKEN_TILE):
    n_ctx, n_lat = t_ctx // tile, t_lat // tile
    first, last = isinstance(x, tuple), final_g is not None
    ctx_map = lambda i: (jnp.minimum(i, n_ctx - 1), 0)
    lat_map = lambda i: (jnp.maximum(i - n_ctx, 0), 0)
    rowfn = lambda i: jnp.where(i < n_ctx, 0, 1 + ((i - n_ctx) * tile) // LAT_LEN)
    two = lambda wd: [pl.BlockSpec((tile, wd), ctx_map), pl.BlockSpec((tile, wd), lat_map)]
    one = pl.BlockSpec((tile, D_MODEL), lambda i: (i, 0))
    hbm = pl.BlockSpec(memory_space=pl.ANY)
    args = list(x) if first else [x]
    specs = two(D_MODEL) if first else [one]
    scratch = [pltpu.VMEM(w_in.shape, BF16), pltpu.VMEM(w_out.shape, BF16)]
    if mix is not None:
        yac, ybc, yal, ybl, wo = mix
        args += [yac, ybc, yal, ybl, wo, mods]
        mw = yac.shape[1]
        specs += [pl.BlockSpec((tile, mw), ctx_map)] * 2 + [pl.BlockSpec((tile, mw), lat_map)] * 2
        specs += [hbm, _mod_spec(mod_base - 1, rowfn)]
        scratch.append(pltpu.VMEM(wo.shape, BF16))
    args += [ng.reshape(1, D_MODEL), mods, mods, mods, w_in, w_out]
    specs += [_const_spec((1, D_MODEL)), _mod_spec(mod_base, rowfn), _mod_spec(mod_base + 1, rowfn),
              _mod_spec(mod_base + 2, rowfn), hbm, hbm]
    if last:
        args.append(final_g.reshape(1, D_MODEL))
        specs.append(_const_spec((1, D_MODEL)))
        out_shape = [jax.ShapeDtypeStruct((t_ctx, D_MODEL), F32), jax.ShapeDtypeStruct((t_lat, D_MODEL), F32)]
        out_specs = two(D_MODEL)
    else:
        out_shape = jax.ShapeDtypeStruct((t_ctx + t_lat, D_MODEL), F32)
        out_specs = one
    scratch += [pltpu.VMEM((W_SLOTS,) + W_TILE, F32), pltpu.SemaphoreType.DMA((W_SLOTS,))]
    return pl.pallas_call(
        functools.partial(_ffn_kernel, n_ctx=n_ctx, first=first, has_mix=mix is not None, last=last),
        out_shape=out_shape,
        grid=(n_ctx + n_lat,),
        in_specs=specs,
        out_specs=out_specs,
        scratch_shapes=scratch,
        compiler_params=_params(("arbitrary",), FFN_VMEM_LIMIT),
        name="ffn",
    )(*args)


_EV_QA, _EV_KA, _EV_VA, _EV_QB, _EV_KB, _EV_VB, _EV_OB = 0, 512, 640, 768, 1280, 1792, 2304
N_DIRHEAD = 2 * N_HEADS


def _proj_even_kernel(*refs, rope):
    it = iter(refs)
    x_ref, ng_ref, sh_ref, sc_ref, w_ref, wob_ref, wg_ref, gbi_ref, gbf_ref, gbt_ref, qg_ref, kg_ref, bd_ref = (
        next(it) for _ in range(13))
    if rope:
        cos_ref, sin_ref = next(it), next(it)
    qa_ref, kat_ref, vat_ref, qb_ref, vb_ref, ob_ref, gi_ref, gf_ref, gt_ref, kbt_ref = (next(it) for _ in range(10))

    h = _modnorm(x_ref[...], ng_ref, sh_ref, sc_ref).astype(BF16)
    proj = lambda lo, hi: _dot(h, w_ref[:, lo:hi])

    qa = proj(_EV_QA, _EV_KA)
    qa = qa * lax.rsqrt(_head_mean_sq(qa, bd_ref) + EPS) * qg_ref[...]
    ka = proj(_EV_KA, _EV_VA)
    ka = ka * lax.rsqrt(_head_mean_sq(ka, bd_ref) + EPS) * kg_ref[...]
    if rope:
        qa = _rope(qa, cos_ref[...], sin_ref[...])
        ka = _rope(ka, cos_ref[...], sin_ref[...])
    qa_ref[...] = qa * ATTN_SCALE
    _store_transposed(kat_ref, ka)
    _store_proj_cols_transposed(vat_ref, w_ref[:, _EV_VA:_EV_QB], h)
    qb_ref[...] = proj(_EV_QB, _EV_KB)
    vb_ref[...] = proj(_EV_VB, _EV_OB)
    ob_ref[...] = _dot(h, wob_ref[...])
    gi_ref[...] = _dot(h, wg_ref[:, :128]) + gbi_ref[...]
    gf_ref[...] = _dot(h, wg_ref[:, 128:]) + gbf_ref[...]
    for j in range(TOKEN_TILE // MLSTM_T):
        ht = h[j * MLSTM_T:(j + 1) * MLSTM_T]
        gates_t = [_dot_colsT(wg_ref[:, c0:c0 + N_DIRHEAD], ht) for c0 in (0, 128)]
        gt_ref[j] = jnp.concatenate(gates_t, axis=0) + gbt_ref[...]
        kbt_ref[j] = _dot_colsT(w_ref[:, _EV_KB:_EV_VB], ht) * ATTN_SCALE


def _proj_even(x, t, blk0, mods, latent, ng, w, w_ob, w_g, gbi, gbf, gbt, qg, kg, bd, rope_tabs):
    rowfn = _rowfn(latent)
    args = [x, ng.reshape(1, D_MODEL), mods, mods, w, w_ob, w_g, gbi, gbf, gbt, qg, kg, bd]
    specs = [_tok_spec(D_MODEL, blk0), _const_spec((1, D_MODEL)), _mod_spec(3, rowfn), _mod_spec(4, rowfn)]
    specs += [_const_spec(a.shape) for a in args[4:]]
    if latent:
        nblk = LAT_LEN // TOKEN_TILE
        args += list(rope_tabs)
        specs += [pl.BlockSpec((TOKEN_TILE, 128), lambda i: (i % nblk, 0))] * 2
    widths = (512, 512, 512, 512, 128, 128)
    out_shape = [jax.ShapeDtypeStruct((t, wd), F32) for wd in widths]
    out_specs = [_tok_spec(wd) for wd in widths]
    for pos in (1, 2):
        shape, spec = _transposed_out(t, latent, 128)
        out_shape.insert(pos, shape)
        out_specs.insert(pos, spec)
    nj = TOKEN_TILE // MLSTM_T
    for rows in (2 * N_DIRHEAD, 512):
        out_shape.append(jax.ShapeDtypeStruct((t // MLSTM_T, rows, MLSTM_T), F32))
        out_specs.append(pl.BlockSpec((nj, rows, MLSTM_T), lambda i: (i, 0, 0)))
    return pl.pallas_call(
        functools.partial(_proj_even_kernel, rope=latent),
        out_shape=out_shape, grid=(t // TOKEN_TILE,), in_specs=specs, out_specs=out_specs,
        compiler_params=_params(("arbitrary",)), name="proj_even",
    )(*args)


def _proj_odd_kernel(*refs, rope):
    it = iter(refs)
    x_ref, ng_ref, sh_ref, sc_ref, w_ref = (next(it) for _ in range(5))
    if rope:
        cos_ref, sin_ref = next(it), next(it)
    qc_ref, kc_ref, vc_ref, qd_ref, kd_ref, vd_ref = (next(it) for _ in range(6))

    h = _modnorm(x_ref[...], ng_ref, sh_ref, sc_ref).astype(BF16)
    proj = lambda lo, hi: _dot(h, w_ref[:, lo:hi])
    qc_ref[...] = proj(0, 512) * ATTN_SCALE
    if rope:
        kc_ref[...] = proj(512, 1024)
        vc_ref[...] = proj(1024, 1536)
    else:
        _store_proj_cols_transposed(kc_ref, w_ref[:, 512:1024], h)
        _store_proj_cols_transposed(vc_ref, w_ref[:, 1024:1536], h)
    qd = proj(1536, 2048)
    if rope:
        qd = _rope(qd, cos_ref[...], sin_ref[...])
        _store_transposed(kd_ref, _rope(proj(2048, 2176), cos_ref[...], sin_ref[...]))
    else:
        _store_proj_cols_transposed(kd_ref, w_ref[:, 2048:2176], h)
    _store_proj_cols_transposed(vd_ref, w_ref[:, 2176:2304], h)
    qd_ref[...] = qd * ATTN_SCALE


def _proj_odd(x, t, blk0, mods, latent, ng, w, rope_tabs):
    rowfn = _rowfn(latent)
    args = [x, ng.reshape(1, D_MODEL), mods, mods, w]
    specs = [_tok_spec(D_MODEL, blk0), _const_spec((1, D_MODEL)), _mod_spec(3, rowfn), _mod_spec(4, rowfn),
             _const_spec(w.shape)]
    if latent:
        nblk = LAT_LEN // TOKEN_TILE
        args += list(rope_tabs)
        specs += [pl.BlockSpec((TOKEN_TILE, 128), lambda i: (i % nblk, 0))] * 2
    out_shape = [jax.ShapeDtypeStruct((t, 512), F32)] * 4
    out_specs = [_tok_spec(512)] * 4
    if not latent:
        out_shape[1], out_specs[1] = _transposed_out(t, latent, 512)
        out_shape[2], out_specs[2] = out_shape[1], out_specs[1]
    for _ in range(2):
        shape, spec = _transposed_out(t, latent, 128)
        out_shape.append(shape)
        out_specs.append(spec)
    return pl.pallas_call(
        functools.partial(_proj_odd_kernel, rope=latent),
        out_shape=out_shape,
        grid=(t // TOKEN_TILE,), in_specs=specs, out_specs=out_specs,
        compiler_params=_params(("arbitrary",)), name="proj_odd",
    )(*args)


def _attn_kernel(*refs, group, kv_t, has_extra, has_sink, spb):
    it = iter(refs)
    q_ref, k_ref, v_ref = next(it), next(it), next(it)
    if has_extra:
        ke_ref, ve_ref = next(it), next(it)
    if has_sink:
        sink_ref = next(it)
    o_ref = next(it)
    tq = q_ref.shape[0] // spb
    first = lax.broadcasted_iota(jnp.int32, (tq, 128), 1) < DH

    def kv_operands(kt_ref, vt_ref, smp, kv):
        kt = kt_ref[smp, kv * DH:(kv + 1) * DH, :].astype(BF16)
        vt = vt_ref[smp, kv * DH:(kv + 1) * DH, :].astype(BF16)
        z = jnp.zeros_like(kt)
        ones = jnp.ones((128, kt.shape[1]), BF16)
        return ((jnp.concatenate([kt, z], axis=0), jnp.concatenate([z, kt], axis=0)),
                (jnp.concatenate([vt, z, ones], axis=0), jnp.concatenate([z, vt, ones], axis=0)))

    for smp in range(spb):
        qrows = slice(smp * tq, (smp + 1) * tq)
        cache = {}
        for j in range(N_HEADS // 2):
            ps = slice(128 * j, 128 * (j + 1))
            q_pair = q_ref[qrows, ps]
            if kv_t:
                qb = q_pair.astype(BF16)
                ops = []
                for half in range(2):
                    kv = (2 * j + half) // group
                    if kv not in cache:
                        cache[kv] = (kv_operands(k_ref, v_ref, smp, kv),
                                     kv_operands(ke_ref, ve_ref, smp, kv) if has_extra else None)
                    ops.append(cache[kv])
            else:
                keys = k_ref.shape[0] // spb
                krows = slice(smp * keys, (smp + 1) * keys)
                k_pair = k_ref[krows, ps].astype(BF16)
                v_aug = jnp.concatenate([v_ref[krows, ps].astype(BF16), jnp.ones((keys, 128), BF16)], axis=1)
            res, sink_den = [], []
            for half in range(2):
                if kv_t:
                    (k_ops, v_ops), extra = ops[half]
                    s = jnp.dot(qb, k_ops[half], preferred_element_type=F32)
                else:
                    s = _dot_nt(jnp.where(first, q_pair, 0.0) if half == 0 else jnp.where(first, 0.0, q_pair), k_pair)
                m = jnp.max(s, axis=-1, keepdims=True)
                if has_extra:
                    se = jnp.dot(qb, extra[0][half], preferred_element_type=F32)
                    m = jnp.maximum(m, jnp.max(se, axis=-1, keepdims=True))
                if has_sink:
                    m = jnp.maximum(m, sink_ref[2 * j + half])
                    sink_den.append(jnp.exp(sink_ref[2 * j + half] - m))
                p = jnp.exp(s - m)
                r = _dot_nt(p, v_ops[half]) if kv_t else _dot(p, v_aug)
                if has_extra:
                    r = r + _dot_nt(jnp.exp(se - m), extra[1][half])
                res.append(r)
            num = jnp.where(first, res[0][:, :128], res[1][:, :128])
            den = jnp.where(first, res[0][:, 128:], res[1][:, 128:])
            if has_sink:
                den = den + jnp.where(first, sink_den[0], sink_den[1])
            o_ref[qrows, ps] = num / den


def _attention(q, k, v, *, n_batch, q_len, q_tile, group, kv_t, extra=None, sink=None):
    nq = q_len // q_tile
    spb = next(n for n in (4, 2, 1) if n_batch % n == 0) if nq == 1 else 1
    args = [q, k, v]
    specs = [pl.BlockSpec((spb * q_tile, 512), lambda b, j: (b * nq + j, 0))]
    if kv_t:
        kv_spec = lambda a: pl.BlockSpec((spb,) + a.shape[1:], lambda b, j: (b, 0, 0))
        specs += [kv_spec(k), kv_spec(v)]
        if extra is not None:
            args += list(extra)
            specs += [kv_spec(extra[0]), kv_spec(extra[1])]
    else:
        assert extra is None
        keys = k.shape[0] // n_batch
        specs += [pl.BlockSpec((spb * keys, 512), lambda b, j: (b, 0))] * 2
    if sink is not None:
        args.append(sink)
        specs.append(pl.BlockSpec(memory_space=pltpu.SMEM))
    return pl.pallas_call(
        functools.partial(_attn_kernel, group=group, kv_t=kv_t, has_extra=extra is not None,
                          has_sink=sink is not None, spb=spb),
        out_shape=jax.ShapeDtypeStruct(q.shape, F32),
        grid=(n_batch // spb, nq), in_specs=specs,
        out_specs=pl.BlockSpec((spb * q_tile, 512), lambda b, j: (b * nq + j, 0)),
        compiler_params=_params(("arbitrary", "arbitrary")), name="attn",
    )(*args)


LOG2E = 1.4426950408889634


def _cummax_time(x, reverse):
    n = x.shape[0]
    row = lax.broadcasted_iota(jnp.int32, x.shape, 0)
    k = 1
    while k < n:
        if reverse:
            shifted = jnp.where(row < n - k, pltpu.roll(x, n - k, 0), -jnp.inf)
        else:
            shifted = jnp.where(row >= k, pltpu.roll(x, k, 0), -jnp.inf)
        x = jnp.maximum(x, shifted)
        k *= 2
    return x


def _mlstm_kernel(*refs, n_chunks, carry, spb):
    it = iter(refs)
    q_ref, kt_ref, v_ref, ob_ref, gi_ref, gf_ref, gt_ref, hg_ref, bd_ref = (next(it) for _ in range(9))
    if carry:
        c0_ref, n0_ref, m0r_ref, m0c_ref = (next(it) for _ in range(4))
    y_ref = next(it)
    if not carry:
        c_ref, n_ref, m_ref = next(it), next(it), next(it)
    hf_s, hb_s = next(it), next(it)
    if carry:
        st_s, mrow_s, mcol_s = next(it), next(it), next(it)
    T = MLSTM_T
    step = pl.program_id(1)

    if carry:
        @pl.when(step == 0)
        def _():
            z = jnp.zeros((DH, DH), F32)
            for d in range(2):
                for j in range(N_HEADS // 2):
                    ca, cb = c0_ref[0, d, 2 * j], c0_ref[0, d, 2 * j + 1]
                    na, nb = n0_ref[0, d, 2 * j][:, :DH], n0_ref[0, d, 2 * j + 1][:, :DH]
                    st_s[d, j, 0:DH, :] = jnp.concatenate([ca, z, na, z], axis=1)
                    st_s[d, j, DH:2 * DH, :] = jnp.concatenate([z, cb, z, nb], axis=1)
            mrow_s[...] = m0r_ref[0]
            mcol_s[...] = m0c_ref[0]

    row = lax.broadcasted_iota(jnp.int32, (T, T), 0)
    col = lax.broadcasted_iota(jnp.int32, (T, T), 1)
    lower = row >= col
    upper = row <= col
    lower_m = jnp.where(lower, 1.0, 0.0).astype(BF16)
    upper_m = jnp.where(upper, 1.0, 0.0).astype(BF16)
    lane = lax.broadcasted_iota(jnp.int32, (T, 128), 1)
    first = lane < DH
    lane_c = lax.broadcasted_iota(jnp.int32, (DH, 256), 1) % 128
    zeros_kt = jnp.zeros((DH, T), BF16)
    ones_v = jnp.ones((T, 128), BF16)

    for smp, d in ((s_, d_) for s_ in range(spb) for d_ in range(2)):
        chunk = (step if d == 0 else n_chunks - 1 - step) if carry else smp
        rows = pl.ds(pl.multiple_of(chunk * T, T), T)
        hd0 = N_HEADS * d
        mask = lower if d == 0 else upper

        if carry or d == 0:
            gi = gi_ref[rows, :]
            lf = _log_sigmoid(gf_ref[rows, :])
            prefix = _dot_exact_rhs(lower_m, lf)
        b_cols = prefix if d == 0 else prefix[T - 1:T, :] - prefix + lf
        a_cols = gi - b_cols
        dmax = b_cols + _cummax_time(a_cols, reverse=d == 1)
        if carry:
            inter = b_cols + mrow_s[0:1, :]
            mt = jnp.maximum(inter, dmax)
            w_inter = jnp.exp(inter - mt)
        else:
            mt = jnp.maximum(b_cols, dmax)
        c2 = (b_cols - mt) * LOG2E
        einv = jnp.exp(-mt)

        gt = gt_ref[chunk]
        gi_t = gt[hd0:hd0 + N_HEADS, :]
        gf_t = gt[N_DIRHEAD + hd0:N_DIRHEAD + hd0 + N_HEADS, :]
        lf_t = _log_sigmoid(gf_t)
        b_rows = _dot_exact_lhs(lf_t, upper_m)
        if d == 1:
            b_rows = b_rows[:, T - 1:T] - b_rows + lf_t
        a_rows = gi_t - b_rows
        a2 = a_rows * LOG2E
        b_last = b_rows[:, T - 1:T] if d == 0 else b_rows[:, 0:1]
        dec_rows = b_last + a_rows
        m_new = jnp.max(dec_rows, axis=-1, keepdims=True)
        if carry:
            m_prev = mcol_s[hd0:hd0 + N_HEADS, 0:1]
            m_new = jnp.maximum(m_new, b_last + m_prev)
            wc = jnp.exp(b_last + m_prev - m_new)
        else:
            m_new = jnp.maximum(m_new, b_last)
        ws_rows = jnp.exp(dec_rows - m_new)

        h_s = hf_s if d == 0 else hb_s
        for j in range(N_HEADS // 2):
            ps = slice(128 * j, 128 * (j + 1))
            q_pair = q_ref[rows, ps].astype(BF16)
            v_aug = jnp.concatenate([v_ref[rows, ps].astype(BF16), ones_v], axis=1)
            kt = (kt_ref[chunk, 128 * j:128 * j + DH, :], kt_ref[chunk, 128 * j + DH:128 * (j + 1), :])
            lhs = []
            for half in range(2):
                hd = 2 * j + half
                kt0 = jnp.concatenate([kt[0].astype(BF16), zeros_kt] if half == 0 else [zeros_kt, kt[1].astype(BF16)], axis=0)
                qk = jnp.dot(q_pair, kt0, preferred_element_type=F32)
                e = c2[:, hd0 + hd:hd0 + hd + 1] + a2[hd:hd + 1, :]
                lhs.append((jnp.exp2(jnp.where(mask, e, -jnp.inf)) * qk).astype(BF16))
            for half in range(2):
                lhs.append((kt[half] * ws_rows[2 * j + half:2 * j + half + 1, :]).astype(BF16))
            both = jnp.dot(jnp.concatenate(lhs, axis=0), v_aug, preferred_element_type=F32)
            res = (both[0:T], both[T:2 * T])
            cu = (both[2 * T:2 * T + DH], both[2 * T + DH:2 * T + 2 * DH])
            ra, rb = hd0 + 2 * j, hd0 + 2 * j + 1
            num = jnp.where(first, res[0][:, :128], res[1][:, :128])
            den = jnp.where(first, res[0][:, 128:], res[1][:, 128:])
            if carry:
                old = st_s[d, j]
                qs = _dot(q_pair, old)
                wi = jnp.where(first, w_inter[:, ra:ra + 1], w_inter[:, rb:rb + 1])
                num = num + wi * qs[:, :128]
                den = den + wi * qs[:, 128:]
            floor = jnp.where(first, einv[:, ra:ra + 1], einv[:, rb:rb + 1])
            h_s[rows, ps] = num / jnp.maximum(jnp.abs(den), floor)
            if carry:
                st_s[d, j, 0:DH, :] = jnp.where(lane_c < DH, cu[0], 0.0) + wc[2 * j:2 * j + 1, :] * old[0:DH]
                st_s[d, j, DH:2 * DH, :] = jnp.where(lane_c >= DH, cu[1], 0.0) + wc[2 * j + 1:2 * j + 2, :] * old[DH:2 * DH]
            else:
                c_ref[smp, d, 2 * j] = cu[0][:, 0:DH]
                c_ref[smp, d, 2 * j + 1] = cu[1][:, DH:2 * DH]

        if carry:
            b_last_l = b_cols[T - 1:T, :] if d == 0 else b_cols[0:1, :]
            m_prev_l = mrow_s[0:1, :]
            m_new_l = jnp.maximum(jnp.max(b_last_l + a_cols, axis=0, keepdims=True), b_last_l + m_prev_l)
            lane_r = lax.broadcasted_iota(jnp.int32, (1, 128), 1)
            mine = (lane_r >= hd0) & (lane_r < hd0 + N_HEADS)
            mrow_s[0:1, :] = jnp.where(mine, m_new_l, m_prev_l)
            mcol_s[hd0:hd0 + N_HEADS, :] = jnp.broadcast_to(m_new, (N_HEADS, 128))
        else:
            for hd in range(N_HEADS):
                n_hd = _dot_nt(ws_rows, kt_ref[chunk, hd * DH:(hd + 1) * DH, :])
                n_ref[smp, d, hd:hd + 1, :] = n_hd[hd:hd + 1, :]
            m_ref[smp, hd0:hd0 + N_HEADS, :] = jnp.broadcast_to(m_new, (N_HEADS, 128))

    @pl.when(step == n_chunks - 1)
    def _():
        hsum = hf_s[...] + hb_s[...]
        y = hsum * lax.rsqrt(_head_mean_sq(hsum, bd_ref) + EPS) * hg_ref[...]
        y_ref[...] = jax.nn.sigmoid(ob_ref[...]) * y


def _mlstm(qb, kbt, vb, ob, gi, gf, gt, head_g, bd, *, n_batch, seq_len, state=None):
    n_chunks = seq_len // MLSTM_T
    carry = state is not None
    assert carry or n_chunks == 1
    spb = 1
    seq = lambda wd: pl.BlockSpec((spb * seq_len, wd), lambda b, c: (b, 0))
    per_chunk = lambda r: pl.BlockSpec((spb * n_chunks, r, MLSTM_T), lambda b, c: (b, 0, 0))
    args = [qb, kbt, vb, ob, gi, gf, gt, head_g, bd]
    specs = [seq(512), per_chunk(512), seq(512), seq(512), seq(128), seq(128), per_chunk(2 * N_DIRHEAD),
             pl.BlockSpec((1, 512), lambda b, c: (0, 0)), pl.BlockSpec(bd.shape, lambda b, c: (0, 0))]
    out_shape = [jax.ShapeDtypeStruct((n_batch * seq_len, 512), F32)]
    out_specs = [seq(512)]
    scratch = [pltpu.VMEM((spb * seq_len, 512), F32), pltpu.VMEM((spb * seq_len, 512), F32)]
    if carry:
        args += list(state)
        specs += [pl.BlockSpec((1, 2, N_HEADS, DH, DH), lambda b, c: (b, 0, 0, 0, 0)),
                  pl.BlockSpec((1, 2, N_HEADS, DH, 128), lambda b, c: (b, 0, 0, 0, 0)),
                  pl.BlockSpec((1, 8, 128), lambda b, c: (b, 0, 0)),
                  pl.BlockSpec((1, N_DIRHEAD, 128), lambda b, c: (b, 0, 0))]
        scratch += [pltpu.VMEM((2, N_HEADS // 2, 128, 256), F32), pltpu.VMEM((8, 128), F32),
                    pltpu.VMEM((N_DIRHEAD, 128), F32)]
    else:
        out_shape += [jax.ShapeDtypeStruct((n_batch, 2, N_HEADS, DH, DH), F32),
                      jax.ShapeDtypeStruct((n_batch, 2, N_HEADS, DH), F32),
                      jax.ShapeDtypeStruct((n_batch, N_DIRHEAD, 128), F32)]
        out_specs += [pl.BlockSpec((spb, 2, N_HEADS, DH, DH), lambda b, c: (b, 0, 0, 0, 0)),
                      pl.BlockSpec((spb, 2, N_HEADS, DH), lambda b, c: (b, 0, 0, 0)),
                      pl.BlockSpec((spb, N_DIRHEAD, 128), lambda b, c: (b, 0, 0))]
    return pl.pallas_call(
        functools.partial(_mlstm_kernel, n_chunks=n_chunks, carry=carry, spb=spb),
        out_shape=out_shape, grid=(n_batch // spb, n_chunks), in_specs=specs, out_specs=out_specs,
        scratch_shapes=scratch,
        compiler_params=_params(("arbitrary", "arbitrary")), name="mlstm",
    )(*args)


N_ROWS = LAT_LEN // GRID_W
N_DY = 2 * NA_KH - 1
N_DX = 2 * NA_KW - 1


NA_QROWS = 4
NA_KROWS = 12
NA_GROUPS = N_ROWS // NA_QROWS


def _na_kernel(q_ref, k_ref, v_ref, ke_ref, ve_ref, rpb_ref, o_ref, tile_s, slab_s):
    pair = pl.program_id(1)
    qn, kn = NA_QROWS * GRID_W, NA_KROWS * GRID_W

    @pl.when(pl.program_id(0) == 0)
    def _():
        qc = lax.broadcasted_iota(jnp.int32, (GRID_W, GRID_W), 0)
        kc = lax.broadcasted_iota(jnp.int32, (GRID_W, GRID_W), 1)
        start = jnp.clip(qc - NA_KW // 2, 0, GRID_W - NA_KW)
        in_win = (kc >= start) & (kc < start + NA_KW)
        dx = kc - qc + (NA_KW - 1)
        blocked = jnp.full((GRID_W, GRID_W), NEG_INF, F32)
        for half in range(2):
            head = 2 * pair + half

            def build_tile(dy, carry):
                tile = blocked
                for j in range(N_DX):
                    tile = jnp.where(dx == j, rpb_ref[(head * N_DY + dy) * N_DX + j], tile)
                tile_s[half * N_DY + dy] = jnp.where(in_win, tile, NEG_INF)
                return carry

            lax.fori_loop(0, N_DY, build_tile, 0)
            for kind in range(3):
                idx = (pair * 2 + half) * 3 + kind
                for i in range(NA_QROWS):
                    for k in range(NA_KROWS):
                        if kind == 0:
                            ok, dy = k < NA_KH, k - i + NA_KH - 1
                        elif kind == 1:
                            ok, dy = i <= k < i + NA_KH, k - i + NA_KH // 2 - 1
                        else:
                            ok, dy = k >= NA_KROWS - NA_KH, k - i - 1
                        slab_s[idx, i * GRID_W:(i + 1) * GRID_W, k * GRID_W:(k + 1) * GRID_W] = (
                            tile_s[half * N_DY + dy] if ok else blocked)

    first = lax.broadcasted_iota(jnp.int32, (qn, 128), 1) < DH
    ones_k = jnp.ones((kn, 128), BF16)
    ke = ke_ref[...].astype(BF16)
    ve_aug = jnp.concatenate([ve_ref[...].astype(BF16), jnp.ones((ke.shape[0], 128), BF16)], axis=1)

    def group_body(g, carry):
        k0 = jnp.where(g < NA_GROUPS // 2, 0, N_ROWS - NA_KROWS)
        kind = jnp.where(g == 0, 0, jnp.where(g == NA_GROUPS - 1, 2, 1))
        qrows = pl.ds(pl.multiple_of(g * qn, qn), qn)
        keys = pl.ds(pl.multiple_of(k0 * GRID_W, GRID_W), kn)
        q_pair = q_ref[qrows, :]
        k_pair = k_ref[keys, :].astype(BF16)
        v_aug = jnp.concatenate([v_ref[keys, :].astype(BF16), ones_k], axis=1)
        res = []
        for half in range(2):
            qm = jnp.where(first, q_pair, 0.0) if half == 0 else jnp.where(first, 0.0, q_pair)
            s = _dot_nt(qm, k_pair) + slab_s[(pair * 2 + half) * 3 + kind]
            se = _dot_nt(qm, ke)
            m = jnp.maximum(jnp.max(s, axis=-1, keepdims=True), jnp.max(se, axis=-1, keepdims=True))
            res.append(_dot(jnp.exp(s - m), v_aug) + _dot(jnp.exp(se - m), ve_aug))
        num = jnp.where(first, res[0][:, :128], res[1][:, :128])
        den = jnp.where(first, res[0][:, 128:], res[1][:, 128:])
        o_ref[qrows, :] = num / den
        return carry

    lax.fori_loop(0, NA_GROUPS, group_body, 0, unroll=True)


def _na_latent(q, k, v, k_ctx, v_ctx, rpb_flat, n_batch):
    seq = pl.BlockSpec((LAT_LEN, 128), lambda b, j: (b, j))
    ctx = pl.BlockSpec((CTX_LEN, 128), lambda b, j: (b, j))
    return pl.pallas_call(
        _na_kernel,
        out_shape=jax.ShapeDtypeStruct(q.shape, F32),
        grid=(n_batch, N_HEADS // 2),
        in_specs=[seq, seq, seq, ctx, ctx, pl.BlockSpec(memory_space=pltpu.SMEM)],
        out_specs=seq,
        scratch_shapes=[pltpu.VMEM((2 * N_DY, GRID_W, GRID_W), F32),
                        pltpu.VMEM((N_HEADS * 3, NA_QROWS * GRID_W, NA_KROWS * GRID_W), F32)],
        compiler_params=_params(("arbitrary", "arbitrary")), name="na_latent",
    )(q, k, v, k_ctx, v_ctx, rpb_flat)


SWA_QT = 256
SWA_SPAN = SWA_QT + 2 * SWA_WIN


def _swa_kernel(q_ref, kt_ref, vt_ref, ket_ref, vet_ref, sink_ref, o_ref):
    pair = pl.program_id(1)
    first = lax.broadcasted_iota(jnp.int32, (SWA_QT, 128), 1) < DH
    row = lax.broadcasted_iota(jnp.int32, (SWA_QT, SWA_SPAN), 0)
    col = lax.broadcasted_iota(jnp.int32, (SWA_QT, SWA_SPAN), 1)
    z = jnp.zeros((DH, SWA_SPAN), BF16)
    ones = jnp.ones((128, SWA_SPAN), BF16)
    ket, vet = ket_ref[0].astype(BF16), vet_ref[0].astype(BF16)
    ze = jnp.zeros_like(ket)
    ones_e = jnp.ones((128, ket.shape[1]), BF16)
    sinks = (sink_ref[2 * pair], sink_ref[2 * pair + 1])
    for g in range(LAT_LEN // SWA_QT):
        q0 = g * SWA_QT
        lo = min(max(q0 - SWA_WIN, 0), LAT_LEN - SWA_SPAN)
        near = jnp.abs((lo + col) - (q0 + row)) <= SWA_WIN
        qb = q_ref[q0:q0 + SWA_QT, :].astype(BF16)
        kt = kt_ref[0, :, lo:lo + SWA_SPAN].astype(BF16)
        vt = vt_ref[0, :, lo:lo + SWA_SPAN].astype(BF16)
        res, sink_den = [], []
        for half in range(2):
            k_op = jnp.concatenate([kt, z] if half == 0 else [z, kt], axis=0)
            v_op = jnp.concatenate([vt, z, ones] if half == 0 else [z, vt, ones], axis=0)
            ke_op = jnp.concatenate([ket, ze] if half == 0 else [ze, ket], axis=0)
            ve_op = jnp.concatenate([vet, ze, ones_e] if half == 0 else [ze, vet, ones_e], axis=0)
            s = jnp.where(near, jnp.dot(qb, k_op, preferred_element_type=F32), NEG_INF)
            se = jnp.dot(qb, ke_op, preferred_element_type=F32)
            m = jnp.maximum(jnp.max(s, axis=-1, keepdims=True), jnp.max(se, axis=-1, keepdims=True))
            m = jnp.maximum(m, sinks[half])
            sink_den.append(jnp.exp(sinks[half] - m))
            res.append(_dot_nt(jnp.exp(s - m), v_op) + _dot_nt(jnp.exp(se - m), ve_op))
        num = jnp.where(first, res[0][:, :128], res[1][:, :128])
        den = jnp.where(first, res[0][:, 128:], res[1][:, 128:]) + jnp.where(first, sink_den[0], sink_den[1])
        o_ref[q0:q0 + SWA_QT, :] = num / den


def _swa_latent(q, kt, vt, kt_ctx, vt_ctx, sink, n_batch, group):
    ppk = group // 2
    kv_spec = lambda a: pl.BlockSpec((1, DH, a.shape[2]), lambda b, j: (b, j // ppk, 0))
    return pl.pallas_call(
        _swa_kernel,
        out_shape=jax.ShapeDtypeStruct(q.shape, F32),
        grid=(n_batch, N_HEADS // 2),
        in_specs=[pl.BlockSpec((LAT_LEN, 128), lambda b, j: (b, j)),
                  kv_spec(kt), kv_spec(vt), kv_spec(kt_ctx), kv_spec(vt_ctx),
                  pl.BlockSpec(memory_space=pltpu.SMEM)],
        out_specs=pl.BlockSpec((LAT_LEN, 128), lambda b, j: (b, j)),
        compiler_params=_params(("arbitrary", "arbitrary")), name="swa_latent",
    )(q, kt, vt, kt_ctx, vt_ctx, sink)


def _rope_tables():
    t = np.arange(LAT_LEN)
    pos = np.stack([t // GRID_W, t % GRID_W], axis=-1).astype(np.float32)
    freqs = np.float32(ROPE_THETA) ** (-np.arange(ROPE_FREQS, dtype=np.float32) / np.float32(ROPE_FREQS))
    ang = (pos[:, :, None] * freqs).reshape(LAT_LEN, 2 * ROPE_FREQS).astype(np.float32)
    cos, sin = np.cos(ang), np.sin(ang)
    cos_t = np.tile(np.concatenate([cos, cos], axis=-1), (1, 2))
    sin_t = np.tile(np.concatenate([-sin, sin], axis=-1), (1, 2))
    return jnp.asarray(cos_t, F32), jnp.asarray(sin_t, F32)


def kernel(x_prompt, x_sample, cache_l0_attn_k, cache_l0_attn_v, state_l0_mlstm_C, state_l0_mlstm_n, state_l0_mlstm_m, cache_l1_na_k, cache_l1_na_v, cache_l1_swa_k, cache_l1_swa_v, c, c_ctx, norm_final, ada_w_l0, ada_b_l0, norm_l0, ffn1_in_l0, ffn1_out_l0, ffn2_in_l0, ffn2_out_l0, mix_in_l0, mix_out_l0, qk_norm_l0, gate_bias_l0, head_norm_l0, ada_w_l1, ada_b_l1, norm_l1, ffn1_in_l1, ffn1_out_l1, ffn2_in_l1, ffn2_out_l1, mix_in_l1, mix_out_l1, rpb_l1, sink_l1):
    nb, nl = x_prompt.shape[0], x_sample.shape[0]
    bf = lambda a: a.astype(BF16)

    cond8 = jnp.zeros((8, D_MODEL), F32).at[0].set(c_ctx).at[1:1 + nl].set(c)
    mods0 = _adaln(cond8, ada_w_l0, ada_b_l0).reshape(8 * ADA_CHUNKS, 1, D_MODEL)
    mods1 = _adaln(cond8, ada_w_l1, ada_b_l1).reshape(8 * ADA_CHUNKS, 1, D_MODEL)

    g0 = 2304
    gcols = lambda j: mix_in_l0[:, g0 + 8 * j:g0 + 8 * (j + 1)]
    gpad = jnp.zeros((D_MODEL, 128 - N_DIRHEAD), F32)
    w_gi, w_gf = jnp.concatenate([gcols(0), gcols(2)], axis=1), jnp.concatenate([gcols(1), gcols(3)], axis=1)
    w_even, w_ob = bf(mix_in_l0[:, :g0]), bf(mix_in_l0[:, g0 + 32:])
    w_g = bf(jnp.concatenate([w_gi, gpad, w_gf, gpad], axis=1))
    gb4 = gate_bias_l0.reshape(4, N_HEADS)
    b_gi, b_gf = jnp.concatenate([gb4[0], gb4[2]]), jnp.concatenate([gb4[1], gb4[3]])
    gbi = jnp.zeros((1, 128), F32).at[0, :N_DIRHEAD].set(b_gi)
    gbf = jnp.zeros((1, 128), F32).at[0, :N_DIRHEAD].set(b_gf)
    gbt = jnp.concatenate([b_gi, b_gf]).reshape(2 * N_DIRHEAD, 1)
    qg = jnp.tile(qk_norm_l0[0], N_HEADS).reshape(1, 512)
    kg = jnp.tile(qk_norm_l0[1], 2).reshape(1, 128)
    head_g = head_norm_l0.reshape(1, 512)
    grp = np.arange(256) // DH
    bd = jnp.asarray((grp[:, None] == grp[None, :]).astype(np.float32) / DH, dtype=BF16)
    rope_tabs = _rope_tables()
    w_odd = bf(mix_in_l1)
    rpb_flat = rpb_l1.reshape(-1)
    to_t = lambda a: a.reshape(a.shape[0], a.shape[1], -1).transpose(0, 2, 1)
    from_t = lambda a: a.reshape(a.shape[0], -1, DH, a.shape[2]).transpose(0, 3, 1, 2)

    xp = x_prompt.reshape(nb * CTX_LEN, D_MODEL)
    xs = x_sample.reshape(nl * LAT_LEN, D_MODEL)
    t_ctx, t_lat = xp.shape[0], xs.shape[0]
    streams = ((False, t_ctx, 0, nb, CTX_LEN), (True, t_lat, t_ctx // TOKEN_TILE, nl, LAT_LEN))
    ffn = functools.partial(_ffn, t_ctx=t_ctx, t_lat=t_lat)

    x = ffn((xp, xs), mods0, 0, norm_l0[0], ffn1_in_l0, ffn1_out_l0)
    ya, yb = {}, {}
    for latent, t, blk0, n_batch, seq_len in streams:
        qa, kat, vat, qb, vb, ob, gi, gf, gt, kbt = _proj_even(
            x, t, blk0, mods0, latent, norm_l0[1], w_even, w_ob, w_g, gbi, gbf, gbt, qg, kg, bd, rope_tabs)
        if latent:
            extra = (to_t(cache_l0_attn_k), to_t(cache_l0_attn_v))
            m0 = state_l0_mlstm_m.reshape(nl, N_DIRHEAD)
            m0_lanes = jnp.zeros((nl, 8, 128), F32).at[:, 0, :N_DIRHEAD].set(m0)
            m0_rows = jnp.broadcast_to(m0[:, :, None], (nl, N_DIRHEAD, 128))
            n0_cols = jnp.broadcast_to(state_l0_mlstm_n[..., None], (nl, 2, N_HEADS, DH, 128))
            state = (state_l0_mlstm_C, n0_cols, m0_lanes, m0_rows)
        else:
            extra, state = None, None
            k0t, v0t = kat, vat
        ya[latent] = _attention(qa, kat, vat, n_batch=n_batch, q_len=seq_len, q_tile=256, group=4, kv_t=True, extra=extra)
        ml = _mlstm(qb, kbt, vb, ob, gi, gf, gt, head_g, bd, n_batch=n_batch, seq_len=seq_len, state=state)
        yb[latent] = ml[0]
        if not latent:
            c0, n0, m0_pad = ml[1:]
    x = ffn(x, mods0, 6, norm_l0[2], ffn2_in_l0, ffn2_out_l0, mix=(ya[False], yb[False], ya[True], yb[True], mix_out_l0))

    x = ffn(x, mods1, 0, norm_l1[0], ffn1_in_l1, ffn1_out_l1)
    yc, yd = {}, {}
    for latent, t, blk0, n_batch, seq_len in streams:
        qc, kc, vc, qd, kd, vd = _proj_odd(x, t, blk0, mods1, latent, norm_l1[1], w_odd, rope_tabs)
        if latent:
            yc[latent] = _na_latent(qc, kc, vc, cache_l1_na_k.reshape(nl * CTX_LEN, 512),
                                    cache_l1_na_v.reshape(nl * CTX_LEN, 512), rpb_flat, nl)
            yd[latent] = _swa_latent(qd, kd, vd, to_t(cache_l1_swa_k), to_t(cache_l1_swa_v), sink_l1, nl, group=4)
        else:
            yc[latent] = _attention(qc, kc, vc, n_batch=n_batch, q_len=seq_len, q_tile=256, group=1, kv_t=True)
            yd[latent] = _attention(qd, kd, vd, n_batch=n_batch, q_len=seq_len, q_tile=256, group=4, kv_t=True, sink=sink_l1)
            kc1t, vc1t, kd1t, vd1t = kc, vc, kd, vd
    y_prompt, y_sample = ffn(x, mods1, 6, norm_l1[2], ffn2_in_l1, ffn2_out_l1,
                             mix=(yc[False], yd[False], yc[True], yd[True], mix_out_l1), final_g=norm_final)

    return (y_prompt.reshape(nb, CTX_LEN, D_MODEL), y_sample.reshape(nl, LAT_LEN, D_MODEL),
            from_t(k0t), from_t(v0t),
            c0, n0, m0_pad[:, :, 0].reshape(nb, 2, N_HEADS),
            from_t(kc1t), from_t(vc1t),
            from_t(kd1t), from_t(vd1t))
```

```python
import functools

import jax
import jax.numpy as jnp
import numpy as np
from jax import lax
from jax.experimental import pallas as pl
from jax.experimental.pallas import tpu as pltpu

F32 = jnp.float32
BF16 = jnp.bfloat16

D_MODEL = 1024
DH = 64
D_FF = 2816
ADA_CHUNKS = 9
GRID_W = 64
LAT_LEN = 1024
CTX_LEN = 256
N_HEADS = 8
NA_KH = 8
NA_KW = 16
SWA_WIN = 128
ROPE_THETA = 10000.0
ROPE_FREQS = DH // 4
ATTN_SCALE = DH ** -0.5
NEG_INF = -1e30
EPS = 1e-6

TOKEN_TILE = 512
MLSTM_T = 256
FFN_CHUNK = 512
VMEM_LIMIT = 56 * 1024 * 1024
FFN_VMEM_LIMIT = 60 * 1024 * 1024
W_TILE = (256, 512)
W_SLOTS = 6


def _params(sem, vmem=VMEM_LIMIT):
    return pltpu.CompilerParams(dimension_semantics=sem, vmem_limit_bytes=vmem)


def _dot(a, b):
    return jnp.dot(a.astype(BF16), b.astype(BF16), preferred_element_type=F32)


def _dot_nt(a, b):
    return lax.dot_general(a.astype(BF16), b.astype(BF16), (((1,), (1,)), ((), ())),
                           preferred_element_type=F32)


def _split3(x):
    hi = x.astype(BF16)
    r1 = x - hi.astype(F32)
    mid = r1.astype(BF16)
    lo = (r1 - mid.astype(F32)).astype(BF16)
    return hi, mid, lo


def _dot_exact_rhs(a_bf16, x):
    hi, mid, lo = _split3(x)
    f = lambda p: jnp.dot(a_bf16, p, preferred_element_type=F32)
    return f(hi) + f(mid) + f(lo)


def _dot_exact_lhs(x, b_bf16):
    hi, mid, lo = _split3(x)
    f = lambda p: jnp.dot(p, b_bf16, preferred_element_type=F32)
    return f(hi) + f(mid) + f(lo)


def _silu(x):
    return x * jax.nn.sigmoid(x)


def _log_sigmoid(x):
    return jnp.minimum(x, 0.0) - jnp.log1p(jnp.exp(-jnp.abs(x)))


def _rms(x, g):
    return x * lax.rsqrt(jnp.mean(x * x, axis=-1, keepdims=True) + EPS) * g


def _head_mean_sq(x, bd_ref):
    n = x.shape[-1]
    sq = x * x
    hi = sq.astype(BF16)
    lo = (sq - hi.astype(F32)).astype(BF16)
    w = min(n, bd_ref.shape[0])
    bd = bd_ref[:w, :w]
    f = lambda p: jnp.dot(p, bd, preferred_element_type=F32)
    parts = [f(hi[:, c:c + w]) + f(lo[:, c:c + w]) for c in range(0, n, w)]
    return parts[0] if len(parts) == 1 else jnp.concatenate(parts, axis=1)


def _rope(x, cos, sin_signed):
    n = x.shape[-1]
    lane = lax.broadcasted_iota(jnp.int32, x.shape, 1)
    first_half = (lane % DH) < (DH // 2)
    partner = jnp.where(first_half, pltpu.roll(x, n - DH // 2, 1), pltpu.roll(x, DH // 2, 1))
    spread = lambda tab: tab if n == 128 else jnp.concatenate([tab] * (n // 128), axis=1)
    return x * spread(cos) + partner * spread(sin_signed)


def _adaln_kernel(c_ref, w_ref, b_ref, o_ref):
    s = _silu(c_ref[...])
    o_ref[...] = _dot(s, w_ref[...]) + b_ref[...]


def _adaln(cond8, w, b):
    n = w.shape[1]
    tn = 1536
    return pl.pallas_call(
        _adaln_kernel,
        out_shape=jax.ShapeDtypeStruct((8, n), F32),
        grid=(n // tn,),
        in_specs=[pl.BlockSpec((8, D_MODEL), lambda j: (0, 0)),
                  pl.BlockSpec((D_MODEL, tn), lambda j: (0, j)),
                  pl.BlockSpec((1, tn), lambda j: (0, j))],
        out_specs=pl.BlockSpec((8, tn), lambda j: (0, j)),
        compiler_params=_params(("arbitrary",)),
        name="adaln",
    )(cond8, w, b.reshape(1, n))


def _const_spec(shape):
    nd = len(shape)
    return pl.BlockSpec(shape, lambda i, _n=nd: (0,) * _n, pipeline_mode=pl.Buffered(1))


def _mod_spec(chunk, rowfn):
    return pl.BlockSpec((1, 1, D_MODEL), lambda i: (rowfn(i) * ADA_CHUNKS + chunk, 0, 0))


def _rowfn(latent):
    if latent:
        return lambda i: 1 + (i * TOKEN_TILE) // LAT_LEN
    return lambda i: 0


def _tok_spec(width, blk0=0):
    return pl.BlockSpec((TOKEN_TILE, width), lambda i: (i + blk0, 0))


def _seq_len(latent):
    return LAT_LEN if latent else CTX_LEN


def _transposed_out(t, latent, rows):
    seq = _seq_len(latent)
    shape = jax.ShapeDtypeStruct((t // seq, rows, seq), F32)
    if seq <= TOKEN_TILE:
        per = TOKEN_TILE // seq
        return shape, pl.BlockSpec((per, rows, seq), lambda i: (i, 0, 0))
    per = seq // TOKEN_TILE
    return shape, pl.BlockSpec((1, rows, TOKEN_TILE), lambda i: (i // per, 0, i % per))


def _store_transposed(ref, x):
    n, _, w = ref.shape
    for j in range(n):
        ref[j] = x[j * w:(j + 1) * w].T


def _dot_colsT(w_cols, h):
    return lax.dot_general(w_cols, h, (((0,), (1,)), ((), ())), preferred_element_type=F32)


def _store_proj_cols_transposed(ref, w_cols, h):
    n, _, w = ref.shape
    for j in range(n):
        ref[j] = _dot_colsT(w_cols, h[j * w:(j + 1) * w])


def _modnorm(x, ng_ref, sh_ref, sc_ref):
    return _rms(x, ng_ref[...]) * (1.0 + sc_ref[0]) + sh_ref[0]


def _load_weights_bf16(pairs, stage, sem):
    tr, tc = W_TILE
    tiles = [(src, dst, r, c) for src, dst in pairs
             for r in range(0, src.shape[0], tr) for c in range(0, src.shape[1], tc)]

    def copy(k):
        src, _, r, c = tiles[k]
        return pltpu.make_async_copy(src.at[r:r + tr, c:c + tc], stage.at[k % W_SLOTS], sem.at[k % W_SLOTS])

    for k in range(min(W_SLOTS, len(tiles))):
        copy(k).start()
    for k, (_, dst, r, c) in enumerate(tiles):
        copy(k).wait()
        dst[r:r + tr, c:c + tc] = stage[k % W_SLOTS].astype(BF16)
        if k + W_SLOTS < len(tiles):
            copy(k + W_SLOTS).start()


def _ffn_kernel(*refs, n_ctx, first, has_mix, last):
    it = iter(refs)
    xs = (next(it), next(it)) if first else (next(it),)
    if has_mix:
        yac_ref, ybc_ref, yal_ref, ybl_ref, wo_hbm, g2_ref = (next(it) for _ in range(6))
    ng_ref, sh_ref, sc_ref, g_ref, win_hbm, wout_hbm = (next(it) for _ in range(6))
    if last:
        nf_ref = next(it)
    outs = (next(it), next(it)) if last else (next(it),)
    win_s, wout_s = next(it), next(it)
    if has_mix:
        wo_s = next(it)
    stage, sem = next(it), next(it)

    step = pl.program_id(0)
    is_lat = step >= n_ctx

    @pl.when(step == 0)
    def _():
        pairs = [(win_hbm, win_s), (wout_hbm, wout_s)]
        if has_mix:
            pairs.insert(0, (wo_hbm, wo_s))
        _load_weights_bf16(pairs, stage, sem)

    pick = lambda c_ref, l_ref: jnp.where(is_lat, l_ref[...], c_ref[...])
    x = pick(*xs) if first else xs[0][...]
    if has_mix:
        half = wo_s.shape[0] // 2
        mo = _dot(pick(yac_ref, yal_ref), wo_s[:half, :]) + _dot(pick(ybc_ref, ybl_ref), wo_s[half:, :])
        x = x + g2_ref[0] * mo
    h = _modnorm(x, ng_ref, sh_ref, sc_ref).astype(BF16)
    acc = None
    for lo in range(0, D_FF, FFN_CHUNK):
        hi = min(lo + FFN_CHUNK, D_FF)
        g = _dot(h, win_s[:, lo:hi])
        u = _dot(h, win_s[:, D_FF + lo:D_FF + hi])
        part = _dot(_silu(g) * u, wout_s[lo:hi, :])
        acc = part if acc is None else acc + part
    y = x + (0.5 * g_ref[0]) * acc
    if last:
        y = _rms(y, nf_ref[...])
        oc_ref, ol_ref = outs

        @pl.when(jnp.logical_not(is_lat))
        def _():
            oc_ref[...] = y
            ol_ref[...] = jnp.zeros_like(ol_ref)

        @pl.when(is_lat)
        def _():
            ol_ref[...] = y
    else:
        outs[0][...] = y


def _ffn(x, mods, mod_base, ng, w_in, w_out, *, t_ctx, t_lat, mix=None, final_g=None):
    tile = TOKEN_TILE
    n_ctx, n_lat = t_ctx // tile, t_lat // tile
    first, last = isinstance(x, tuple), final_g is not None
    ctx_map = lambda i: (jnp.minimum(i, n_ctx - 1), 0)
    lat_map = lambda i: (jnp.maximum(i - n_ctx, 0), 0)
    rowfn = lambda i: jnp.where(i < n_ctx, 0, 1 + ((i - n_ctx) * tile) // LAT_LEN)
    two = lambda wd: [pl.BlockSpec((tile, wd), ctx_map), pl.BlockSpec((tile, wd), lat_map)]
    one = pl.BlockSpec((tile, D_MODEL), lambda i: (i, 0))
    hbm = pl.BlockSpec(memory_space=pl.ANY)
    args = list(x) if first else [x]
    specs = two(D_MODEL) if first else [one]
    scratch = [pltpu.VMEM(w_in.shape, BF16), pltpu.VMEM(w_out.shape, BF16)]
    if mix is not None:
        yac, ybc, yal, ybl, wo = mix
        args += [yac, ybc, yal, ybl, wo, mods]
        mw = yac.shape[1]
        specs += [pl.BlockSpec((tile, mw), ctx_map)] * 2 + [pl.BlockSpec((tile, mw), lat_map)] * 2
        specs += [hbm, _mod_spec(mod_base - 1, rowfn)]
        scratch.append(pltpu.VMEM(wo.shape, BF16))
    args += [ng.reshape(1, D_MODEL), mods, mods, mods, w_in, w_out]
    specs += [_const_spec((1, D_MODEL)), _mod_spec(mod_base, rowfn), _mod_spec(mod_base + 1, rowfn),
              _mod_spec(mod_base + 2, rowfn), hbm, hbm]
    if last:
        args.append(final_g.reshape(1, D_MODEL))
        specs.append(_const_spec((1, D_MODEL)))
        out_shape = [jax.ShapeDtypeStruct((t_ctx, D_MODEL), F32), jax.ShapeDtypeStruct((t_lat, D_MODEL), F32)]
        out_specs = two(D_MODEL)
    else:
        out_shape = jax.ShapeDtypeStruct((t_ctx + t_lat, D_MODEL), F32)
        out_specs = one
    scratch += [pltpu.VMEM((W_SLOTS,) + W_TILE, F32), pltpu.SemaphoreType.DMA((W_SLOTS,))]
    return pl.pallas_call(
        functools.partial(_ffn_kernel, n_ctx=n_ctx, first=first, has_mix=mix is not None, last=last),
        out_shape=out_shape,
        grid=(n_ctx + n_lat,),
        in_specs=specs,
        out_specs=out_specs,
        scratch_shapes=scratch,
        compiler_params=_params(("arbitrary",), FFN_VMEM_LIMIT),
        name="ffn",
    )(*args)


_EV_QA, _EV_KA, _EV_VA, _EV_QB, _EV_KB, _EV_VB, _EV_OB = 0, 512, 640, 768, 1280, 1792, 2304
N_DIRHEAD = 2 * N_HEADS


def _proj_even_kernel(*refs, rope):
    it = iter(refs)
    x_ref, ng_ref, sh_ref, sc_ref, w_ref, wob_ref, wg_ref, gbi_ref, gbf_ref, gbt_ref, qg_ref, kg_ref, bd_ref = (
        next(it) for _ in range(13))
    if rope:
        cos_ref, sin_ref = next(it), next(it)
    qa_ref, kat_ref, vat_ref, qb_ref, vb_ref, ob_ref, gi_ref, gf_ref, gt_ref, kbt_ref = (next(it) for _ in range(10))

    h = _modnorm(x_ref[...], ng_ref, sh_ref, sc_ref).astype(BF16)
    proj = lambda lo, hi: _dot(h, w_ref[:, lo:hi])

    qa = proj(_EV_QA, _EV_KA)
    qa = qa * lax.rsqrt(_head_mean_sq(qa, bd_ref) + EPS) * qg_ref[...]
    ka = proj(_EV_KA, _EV_VA)
    ka = ka * lax.rsqrt(_head_mean_sq(ka, bd_ref) + EPS) * kg_ref[...]
    if rope:
        qa = _rope(qa, cos_ref[...], sin_ref[...])
        ka = _rope(ka, cos_ref[...], sin_ref[...])
    qa_ref[...] = qa * ATTN_SCALE
    _store_transposed(kat_ref, ka)
    _store_proj_cols_transposed(vat_ref, w_ref[:, _EV_VA:_EV_QB], h)
    qb_ref[...] = proj(_EV_QB, _EV_KB)
    vb_ref[...] = proj(_EV_VB, _EV_OB)
    ob_ref[...] = _dot(h, wob_ref[...])
    gi_ref[...] = _dot(h, wg_ref[:, :128]) + gbi_ref[...]
    gf_ref[...] = _dot(h, wg_ref[:, 128:]) + gbf_ref[...]
    for j in range(TOKEN_TILE // MLSTM_T):
        ht = h[j * MLSTM_T:(j + 1) * MLSTM_T]
        gates_t = [_dot_colsT(wg_ref[:, c0:c0 + N_DIRHEAD], ht) for c0 in (0, 128)]
        gt_ref[j] = jnp.concatenate(gates_t, axis=0) + gbt_ref[...]
        kbt_ref[j] = _dot_colsT(w_ref[:, _EV_KB:_EV_VB], ht) * ATTN_SCALE


def _proj_even(x, t, blk0, mods, latent, ng, w, w_ob, w_g, gbi, gbf, gbt, qg, kg, bd, rope_tabs):
    rowfn = _rowfn(latent)
    args = [x, ng.reshape(1, D_MODEL), mods, mods, w, w_ob, w_g, gbi, gbf, gbt, qg, kg, bd]
    specs = [_tok_spec(D_MODEL, blk0), _const_spec((1, D_MODEL)), _mod_spec(3, rowfn), _mod_spec(4, rowfn)]
    specs += [_const_spec(a.shape) for a in args[4:]]
    if latent:
        nblk = LAT_LEN // TOKEN_TILE
        args += list(rope_tabs)
        specs += [pl.BlockSpec((TOKEN_TILE, 128), lambda i: (i % nblk, 0))] * 2
    widths = (512, 512, 512, 512, 128, 128)
    out_shape = [jax.ShapeDtypeStruct((t, wd), F32) for wd in widths]
    out_specs = [_tok_spec(wd) for wd in widths]
    for pos in (1, 2):
        shape, spec = _transposed_out(t, latent, 128)
        out_shape.insert(pos, shape)
        out_specs.insert(pos, spec)
    nj = TOKEN_TILE // MLSTM_T
    for rows in (2 * N_DIRHEAD, 512):
        out_shape.append(jax.ShapeDtypeStruct((t // MLSTM_T, rows, MLSTM_T), F32))
        out_specs.append(pl.BlockSpec((nj, rows, MLSTM_T), lambda i: (i, 0, 0)))
    return pl.pallas_call(
        functools.partial(_proj_even_kernel, rope=latent),
        out_shape=out_shape, grid=(t // TOKEN_TILE,), in_specs=specs, out_specs=out_specs,
        compiler_params=_params(("arbitrary",)), name="proj_even",
    )(*args)


def _proj_odd_kernel(*refs, rope):
    it = iter(refs)
    x_ref, ng_ref, sh_ref, sc_ref, w_ref = (next(it) for _ in range(5))
    if rope:
        cos_ref, sin_ref = next(it), next(it)
    qc_ref, kc_ref, vc_ref, qd_ref, kd_ref, vd_ref = (next(it) for _ in range(6))

    h = _modnorm(x_ref[...], ng_ref, sh_ref, sc_ref).astype(BF16)
    proj = lambda lo, hi: _dot(h, w_ref[:, lo:hi])
    qc_ref[...] = proj(0, 512) * ATTN_SCALE
    if rope:
        kc_ref[...] = proj(512, 1024)
        vc_ref[...] = proj(1024, 1536)
    else:
        _store_proj_cols_transposed(kc_ref, w_ref[:, 512:1024], h)
        _store_proj_cols_transposed(vc_ref, w_ref[:, 1024:1536], h)
    qd = proj(1536, 2048)
    if rope:
        qd = _rope(qd, cos_ref[...], sin_ref[...])
        _store_transposed(kd_ref, _rope(proj(2048, 2176), cos_ref[...], sin_ref[...]))
    else:
        _store_proj_cols_transposed(kd_ref, w_ref[:, 2048:2176], h)
    _store_proj_cols_transposed(vd_ref, w_ref[:, 2176:2304], h)
    qd_ref[...] = qd * ATTN_SCALE


def _proj_odd(x, t, blk0, mods, latent, ng, w, rope_tabs):
    rowfn = _rowfn(latent)
    args = [x, ng.reshape(1, D_MODEL), mods, mods, w]
    specs = [_tok_spec(D_MODEL, blk0), _const_spec((1, D_MODEL)), _mod_spec(3, rowfn), _mod_spec(4, rowfn),
             _const_spec(w.shape)]
    if latent:
        nblk = LAT_LEN // TOKEN_TILE
        args += list(rope_tabs)
        specs += [pl.BlockSpec((TOKEN_TILE, 128), lambda i: (i % nblk, 0))] * 2
    out_shape = [jax.ShapeDtypeStruct((t, 512), F32)] * 4
    out_specs = [_tok_spec(512)] * 4
    if not latent:
        out_shape[1], out_specs[1] = _transposed_out(t, latent, 512)
        out_shape[2], out_specs[2] = out_shape[1], out_specs[1]
    for _ in range(2):
        shape, spec = _transposed_out(t, latent, 128)
        out_shape.append(shape)
        out_specs.append(spec)
    return pl.pallas_call(
        functools.partial(_proj_odd_kernel, rope=latent),
        out_shape=out_shape,
        grid=(t // TOKEN_TILE,), in_specs=specs, out_specs=out_specs,
        compiler_params=_params(("arbitrary",)), name="proj_odd",
    )(*args)


def _attn_kernel(*refs, group, has_extra, has_sink, spb):
    it = iter(refs)
    q_ref, k_ref, v_ref = next(it), next(it), next(it)
    if has_extra:
        ke_ref, ve_ref = next(it), next(it)
    if has_sink:
        sink_ref = next(it)
    o_ref = next(it)
    tq = q_ref.shape[0] // spb
    first = lax.broadcasted_iota(jnp.int32, (tq, 128), 1) < DH

    def kv_operands(kt_ref, vt_ref, smp, kv):
        kt = kt_ref[smp, kv * DH:(kv + 1) * DH, :].astype(BF16)
        vt = vt_ref[smp, kv * DH:(kv + 1) * DH, :].astype(BF16)
        z = jnp.zeros_like(kt)
        ones = jnp.ones((128, kt.shape[1]), BF16)
        return ((jnp.concatenate([kt, z], axis=0), jnp.concatenate([z, kt], axis=0)),
                (jnp.concatenate([vt, z, ones], axis=0), jnp.concatenate([z, vt, ones], axis=0)))

    for smp in range(spb):
        qrows = slice(smp * tq, (smp + 1) * tq)
        cache = {}
        for j in range(N_HEADS // 2):
            ps = slice(128 * j, 128 * (j + 1))
            qb = q_ref[qrows, ps].astype(BF16)
            res, sink_den = [], []
            for half in range(2):
                kv = (2 * j + half) // group
                if kv not in cache:
                    cache[kv] = (kv_operands(k_ref, v_ref, smp, kv),
                                 kv_operands(ke_ref, ve_ref, smp, kv) if has_extra else None)
                (k_ops, v_ops), extra = cache[kv]
                s = jnp.dot(qb, k_ops[half], preferred_element_type=F32)
                m = jnp.max(s, axis=-1, keepdims=True)
                if has_extra:
                    se = jnp.dot(qb, extra[0][half], preferred_element_type=F32)
                    m = jnp.maximum(m, jnp.max(se, axis=-1, keepdims=True))
                if has_sink:
                    m = jnp.maximum(m, sink_ref[2 * j + half])
                    sink_den.append(jnp.exp(sink_ref[2 * j + half] - m))
                r = _dot_nt(jnp.exp(s - m), v_ops[half])
                if has_extra:
                    r = r + _dot_nt(jnp.exp(se - m), extra[1][half])
                res.append(r)
            num = jnp.where(first, res[0][:, :128], res[1][:, :128])
            den = jnp.where(first, res[0][:, 128:], res[1][:, 128:])
            if has_sink:
                den = den + jnp.where(first, sink_den[0], sink_den[1])
            o_ref[qrows, ps] = num / den


def _attention(q, k, v, *, n_batch, q_len, q_tile, group, extra=None, sink=None):
    nq = q_len // q_tile
    spb = next(n for n in (8, 4, 2, 1) if n_batch % n == 0) if nq == 1 else 1
    kv_spec = lambda a: pl.BlockSpec((spb,) + a.shape[1:], lambda b, j: (b, 0, 0))
    args = [q, k, v]
    specs = [pl.BlockSpec((spb * q_tile, 512), lambda b, j: (b * nq + j, 0)), kv_spec(k), kv_spec(v)]
    if extra is not None:
        args += list(extra)
        specs += [kv_spec(extra[0]), kv_spec(extra[1])]
    if sink is not None:
        args.append(sink)
        specs.append(pl.BlockSpec(memory_space=pltpu.SMEM))
    return pl.pallas_call(
        functools.partial(_attn_kernel, group=group, has_extra=extra is not None, has_sink=sink is not None, spb=spb),
        out_shape=jax.ShapeDtypeStruct(q.shape, F32),
        grid=(n_batch // spb, nq), in_specs=specs,
        out_specs=pl.BlockSpec((spb * q_tile, 512), lambda b, j: (b * nq + j, 0)),
        compiler_params=_params(("arbitrary", "arbitrary")), name="attn",
    )(*args)


LOG2E = 1.4426950408889634


def _cummax_time(x, reverse):
    n = x.shape[0]
    row = lax.broadcasted_iota(jnp.int32, x.shape, 0)
    k = 1
    while k < n:
        if reverse:
            shifted = jnp.where(row < n - k, pltpu.roll(x, n - k, 0), -jnp.inf)
        else:
            shifted = jnp.where(row >= k, pltpu.roll(x, k, 0), -jnp.inf)
        x = jnp.maximum(x, shifted)
        k *= 2
    return x


def _mlstm_kernel(*refs, n_chunks, carry, spb):
    it = iter(refs)
    q_ref, kt_ref, v_ref, ob_ref, gi_ref, gf_ref, gt_ref, hg_ref, bd_ref = (next(it) for _ in range(9))
    if carry:
        c0_ref, n0_ref, m0r_ref, m0c_ref = (next(it) for _ in range(4))
    y_ref = next(it)
    if not carry:
        c_ref, n_ref, m_ref = next(it), next(it), next(it)
    hf_s, hb_s = next(it), next(it)
    if carry:
        st_s, mrow_s, mcol_s = next(it), next(it), next(it)
    T = MLSTM_T
    step = pl.program_id(1)

    if carry:
        @pl.when(step == 0)
        def _():
            z = jnp.zeros((DH, DH), F32)
            for d in range(2):
                for j in range(N_HEADS // 2):
                    ca, cb = c0_ref[0, d, 2 * j], c0_ref[0, d, 2 * j + 1]
                    na, nb = n0_ref[0, d, 2 * j][:, :DH], n0_ref[0, d, 2 * j + 1][:, :DH]
                    st_s[d, j, 0:DH, :] = jnp.concatenate([ca, z, na, z], axis=1)
                    st_s[d, j, DH:2 * DH, :] = jnp.concatenate([z, cb, z, nb], axis=1)
            mrow_s[...] = m0r_ref[0]
            mcol_s[...] = m0c_ref[0]

    row = lax.broadcasted_iota(jnp.int32, (T, T), 0)
    col = lax.broadcasted_iota(jnp.int32, (T, T), 1)
    lower = row >= col
    upper = row <= col
    lower_m = jnp.where(lower, 1.0, 0.0).astype(BF16)
    upper_m = jnp.where(upper, 1.0, 0.0).astype(BF16)
    lane = lax.broadcasted_iota(jnp.int32, (T, 128), 1)
    first = lane < DH
    lane_c = lax.broadcasted_iota(jnp.int32, (DH, 256), 1) % 128
    zeros_kt = jnp.zeros((DH, T), BF16)
    ones_v = jnp.ones((T, 128), BF16)

    for smp, d in ((s_, d_) for s_ in range(spb) for d_ in range(2)):
        chunk = (step if d == 0 else n_chunks - 1 - step) if carry else smp
        rows = pl.ds(pl.multiple_of(chunk * T, T), T)
        hd0 = N_HEADS * d
        mask = lower if d == 0 else upper

        if carry or d == 0:
            gi = gi_ref[rows, :]
            lf = _log_sigmoid(gf_ref[rows, :])
            prefix = _dot_exact_rhs(lower_m, lf)
        b_cols = prefix if d == 0 else prefix[T - 1:T, :] - prefix + lf
        a_cols = gi - b_cols
        dmax = b_cols + _cummax_time(a_cols, reverse=d == 1)
        if carry:
            inter = b_cols + mrow_s[0:1, :]
            mt = jnp.maximum(inter, dmax)
            w_inter = jnp.exp(inter - mt)
        else:
            mt = jnp.maximum(b_cols, dmax)
        c2 = (b_cols - mt) * LOG2E
        einv = jnp.exp(-mt)

        gt = gt_ref[chunk]
        gi_t = gt[hd0:hd0 + N_HEADS, :]
        gf_t = gt[N_DIRHEAD + hd0:N_DIRHEAD + hd0 + N_HEADS, :]
        lf_t = _log_sigmoid(gf_t)
        b_rows = _dot_exact_lhs(lf_t, upper_m)
        if d == 1:
            b_rows = b_rows[:, T - 1:T] - b_rows + lf_t
        a_rows = gi_t - b_rows
        a2 = a_rows * LOG2E
        b_last = b_rows[:, T - 1:T] if d == 0 else b_rows[:, 0:1]
        dec_rows = b_last + a_rows
        m_new = jnp.max(dec_rows, axis=-1, keepdims=True)
        if carry:
            m_prev = mcol_s[hd0:hd0 + N_HEADS, 0:1]
            m_new = jnp.maximum(m_new, b_last + m_prev)
            wc = jnp.exp(b_last + m_prev - m_new)
        else:
            m_new = jnp.maximum(m_new, b_last)
        ws_rows = jnp.exp(dec_rows - m_new)

        h_s = hf_s if d == 0 else hb_s
        for j in range(N_HEADS // 2):
            ps = slice(128 * j, 128 * (j + 1))
            q_pair = q_ref[rows, ps].astype(BF16)
            v_aug = jnp.concatenate([v_ref[rows, ps].astype(BF16), ones_v], axis=1)
            kt = (kt_ref[chunk, 128 * j:128 * j + DH, :], kt_ref[chunk, 128 * j + DH:128 * (j + 1), :])
            lhs = []
            for half in range(2):
                hd = 2 * j + half
                kt0 = jnp.concatenate([kt[0].astype(BF16), zeros_kt] if half == 0 else [zeros_kt, kt[1].astype(BF16)], axis=0)
                qk = jnp.dot(q_pair, kt0, preferred_element_type=F32)
                e = c2[:, hd0 + hd:hd0 + hd + 1] + a2[hd:hd + 1, :]
                lhs.append((jnp.exp2(jnp.where(mask, e, -jnp.inf)) * qk).astype(BF16))
            for half in range(2):
                lhs.append((kt[half] * ws_rows[2 * j + half:2 * j + half + 1, :]).astype(BF16))
            both = jnp.dot(jnp.concatenate(lhs, axis=0), v_aug, preferred_element_type=F32)
            res = (both[0:T], both[T:2 * T])
            cu = (both[2 * T:2 * T + DH], both[2 * T + DH:2 * T + 2 * DH])
            ra, rb = hd0 + 2 * j, hd0 + 2 * j + 1
            num = jnp.where(first, res[0][:, :128], res[1][:, :128])
            den = jnp.where(first, res[0][:, 128:], res[1][:, 128:])
            if carry:
                old = st_s[d, j]
                qs = _dot(q_pair, old)
                wi = jnp.where(first, w_inter[:, ra:ra + 1], w_inter[:, rb:rb + 1])
                num = num + wi * qs[:, :128]
                den = den + wi * qs[:, 128:]
            floor = jnp.where(first, einv[:, ra:ra + 1], einv[:, rb:rb + 1])
            h_s[rows, ps] = num / jnp.maximum(jnp.abs(den), floor)
            if carry:
                st_s[d, j, 0:DH, :] = jnp.where(lane_c < DH, cu[0], 0.0) + wc[2 * j:2 * j + 1, :] * old[0:DH]
                st_s[d, j, DH:2 * DH, :] = jnp.where(lane_c >= DH, cu[1], 0.0) + wc[2 * j + 1:2 * j + 2, :] * old[DH:2 * DH]
            else:
                c_ref[smp, d, 2 * j] = cu[0][:, 0:DH]
                c_ref[smp, d, 2 * j + 1] = cu[1][:, DH:2 * DH]

        if carry:
            b_last_l = b_cols[T - 1:T, :] if d == 0 else b_cols[0:1, :]
            m_prev_l = mrow_s[0:1, :]
            m_new_l = jnp.maximum(jnp.max(b_last_l + a_cols, axis=0, keepdims=True), b_last_l + m_prev_l)
            lane_r = lax.broadcasted_iota(jnp.int32, (1, 128), 1)
            mine = (lane_r >= hd0) & (lane_r < hd0 + N_HEADS)
            mrow_s[0:1, :] = jnp.where(mine, m_new_l, m_prev_l)
            mcol_s[hd0:hd0 + N_HEADS, :] = jnp.broadcast_to(m_new, (N_HEADS, 128))
        else:
            for hd in range(N_HEADS):
                n_hd = _dot_nt(ws_rows, kt_ref[chunk, hd * DH:(hd + 1) * DH, :])
                n_ref[smp, d, hd:hd + 1, :] = n_hd[hd:hd + 1, :]
            m_ref[smp, hd0:hd0 + N_HEADS, :] = jnp.broadcast_to(m_new, (N_HEADS, 128))

    @pl.when(step == n_chunks - 1)
    def _():
        hsum = hf_s[...] + hb_s[...]
        y = hsum * lax.rsqrt(_head_mean_sq(hsum, bd_ref) + EPS) * hg_ref[...]
        y_ref[...] = jax.nn.sigmoid(ob_ref[...]) * y


def _mlstm(qb, kbt, vb, ob, gi, gf, gt, head_g, bd, *, n_batch, seq_len, state=None):
    n_chunks = seq_len // MLSTM_T
    carry = state is not None
    assert carry or n_chunks == 1
    spb = 1
    seq = lambda wd: pl.BlockSpec((spb * seq_len, wd), lambda b, c: (b, 0))
    per_chunk = lambda r: pl.BlockSpec((spb * n_chunks, r, MLSTM_T), lambda b, c: (b, 0, 0))
    args = [qb, kbt, vb, ob, gi, gf, gt, head_g, bd]
    specs = [seq(512), per_chunk(512), seq(512), seq(512), seq(128), seq(128), per_chunk(2 * N_DIRHEAD),
             pl.BlockSpec((1, 512), lambda b, c: (0, 0)), pl.BlockSpec(bd.shape, lambda b, c: (0, 0))]
    out_shape = [jax.ShapeDtypeStruct((n_batch * seq_len, 512), F32)]
    out_specs = [seq(512)]
    scratch = [pltpu.VMEM((spb * seq_len, 512), F32), pltpu.VMEM((spb * seq_len, 512), F32)]
    if carry:
        args += list(state)
        specs += [pl.BlockSpec((1, 2, N_HEADS, DH, DH), lambda b, c: (b, 0, 0, 0, 0)),
                  pl.BlockSpec((1, 2, N_HEADS, DH, 128), lambda b, c: (b, 0, 0, 0, 0)),
                  pl.BlockSpec((1, 8, 128), lambda b, c: (b, 0, 0)),
                  pl.BlockSpec((1, N_DIRHEAD, 128), lambda b, c: (b, 0, 0))]
        scratch += [pltpu.VMEM((2, N_HEADS // 2, 128, 256), F32), pltpu.VMEM((8, 128), F32),
                    pltpu.VMEM((N_DIRHEAD, 128), F32)]
    else:
        out_shape += [jax.ShapeDtypeStruct((n_batch, 2, N_HEADS, DH, DH), F32),
                      jax.ShapeDtypeStruct((n_batch, 2, N_HEADS, DH), F32),
                      jax.ShapeDtypeStruct((n_batch, N_DIRHEAD, 128), F32)]
        out_specs += [pl.BlockSpec((spb, 2, N_HEADS, DH, DH), lambda b, c: (b, 0, 0, 0, 0)),
                      pl.BlockSpec((spb, 2, N_HEADS, DH), lambda b, c: (b, 0, 0, 0)),
                      pl.BlockSpec((spb, N_DIRHEAD, 128), lambda b, c: (b, 0, 0))]
    return pl.pallas_call(
        functools.partial(_mlstm_kernel, n_chunks=n_chunks, carry=carry, spb=spb),
        out_shape=out_shape, grid=(n_batch // spb, n_chunks), in_specs=specs, out_specs=out_specs,
        scratch_shapes=scratch,
        compiler_params=_params(("arbitrary", "arbitrary")), name="mlstm",
    )(*args)


N_ROWS = LAT_LEN // GRID_W
N_DY = 2 * NA_KH - 1
N_DX = 2 * NA_KW - 1


NA_QROWS = 4
NA_KROWS = 12
NA_GROUPS = N_ROWS // NA_QROWS


def _na_kernel(q_ref, k_ref, v_ref, ke_ref, ve_ref, rpb_ref, o_ref, tile_s, slab_s):
    pair = pl.program_id(1)
    qn, kn = NA_QROWS * GRID_W, NA_KROWS * GRID_W

    @pl.when(pl.program_id(0) == 0)
    def _():
        qc = lax.broadcasted_iota(jnp.int32, (GRID_W, GRID_W), 0)
        kc = lax.broadcasted_iota(jnp.int32, (GRID_W, GRID_W), 1)
        start = jnp.clip(qc - NA_KW // 2, 0, GRID_W - NA_KW)
        in_win = (kc >= start) & (kc < start + NA_KW)
        dx = kc - qc + (NA_KW - 1)
        blocked = jnp.full((GRID_W, GRID_W), NEG_INF, F32)
        for half in range(2):
            head = 2 * pair + half

            def build_tile(dy, carry):
                tile = blocked
                for j in range(N_DX):
                    tile = jnp.where(dx == j, rpb_ref[(head * N_DY + dy) * N_DX + j], tile)
                tile_s[half * N_DY + dy] = jnp.where(in_win, tile, NEG_INF)
                return carry

            lax.fori_loop(0, N_DY, build_tile, 0)
            for kind in range(3):
                idx = (pair * 2 + half) * 3 + kind
                for i in range(NA_QROWS):
                    for k in range(NA_KROWS):
                        if kind == 0:
                            ok, dy = k < NA_KH, k - i + NA_KH - 1
                        elif kind == 1:
                            ok, dy = i <= k < i + NA_KH, k - i + NA_KH // 2 - 1
                        else:
                            ok, dy = k >= NA_KROWS - NA_KH, k - i - 1
                        slab_s[idx, i * GRID_W:(i + 1) * GRID_W, k * GRID_W:(k + 1) * GRID_W] = (
                            tile_s[half * N_DY + dy] if ok else blocked)

    first = lax.broadcasted_iota(jnp.int32, (qn, 128), 1) < DH
    ones_k = jnp.ones((kn, 128), BF16)
    ke = ke_ref[...].astype(BF16)
    ve_aug = jnp.concatenate([ve_ref[...].astype(BF16), jnp.ones((ke.shape[0], 128), BF16)], axis=1)

    def group_body(g, carry):
        k0 = jnp.where(g < NA_GROUPS // 2, 0, N_ROWS - NA_KROWS)
        kind = jnp.where(g == 0, 0, jnp.where(g == NA_GROUPS - 1, 2, 1))
        qrows = pl.ds(pl.multiple_of(g * qn, qn), qn)
        keys = pl.ds(pl.multiple_of(k0 * GRID_W, GRID_W), kn)
        q_pair = q_ref[qrows, :]
        k_pair = k_ref[keys, :].astype(BF16)
        v_aug = jnp.concatenate([v_ref[keys, :].astype(BF16), ones_k], axis=1)
        res = []
        for half in range(2):
            qm = jnp.where(first, q_pair, 0.0) if half == 0 else jnp.where(first, 0.0, q_pair)
            s = _dot_nt(qm, k_pair) + slab_s[(pair * 2 + half) * 3 + kind]
            se = _dot_nt(qm, ke)
            m = jnp.maximum(jnp.max(s, axis=-1, keepdims=True), jnp.max(se, axis=-1, keepdims=True))
            res.append(_dot(jnp.exp(s - m), v_aug) + _dot(jnp.exp(se - m), ve_aug))
        num = jnp.where(first, res[0][:, :128], res[1][:, :128])
        den = jnp.where(first, res[0][:, 128:], res[1][:, 128:])
        o_ref[qrows, :] = num / den
        return carry

    lax.fori_loop(0, NA_GROUPS, group_body, 0, unroll=True)


def _na_latent(q, k, v, k_ctx, v_ctx, rpb_flat, n_batch):
    seq = pl.BlockSpec((LAT_LEN, 128), lambda b, j: (b, j))
    ctx = pl.BlockSpec((CTX_LEN, 128), lambda b, j: (b, j))
    return pl.pallas_call(
        _na_kernel,
        out_shape=jax.ShapeDtypeStruct(q.shape, F32),
        grid=(n_batch, N_HEADS // 2),
        in_specs=[seq, seq, seq, ctx, ctx, pl.BlockSpec(memory_space=pltpu.SMEM)],
        out_specs=seq,
        scratch_shapes=[pltpu.VMEM((2 * N_DY, GRID_W, GRID_W), F32),
                        pltpu.VMEM((N_HEADS * 3, NA_QROWS * GRID_W, NA_KROWS * GRID_W), F32)],
        compiler_params=_params(("arbitrary", "arbitrary")), name="na_latent",
    )(q, k, v, k_ctx, v_ctx, rpb_flat)


SWA_QT = 256
SWA_SPAN = SWA_QT + 2 * SWA_WIN


def _swa_kernel(q_ref, kt_ref, vt_ref, ket_ref, vet_ref, sink_ref, o_ref):
    pair = pl.program_id(1)
    first = lax.broadcasted_iota(jnp.int32, (SWA_QT, 128), 1) < DH
    row = lax.broadcasted_iota(jnp.int32, (SWA_QT, SWA_SPAN), 0)
    col = lax.broadcasted_iota(jnp.int32, (SWA_QT, SWA_SPAN), 1)
    z = jnp.zeros((DH, SWA_SPAN), BF16)
    ones = jnp.ones((128, SWA_SPAN), BF16)
    ket, vet = ket_ref[0].astype(BF16), vet_ref[0].astype(BF16)
    ze = jnp.zeros_like(ket)
    ones_e = jnp.ones((128, ket.shape[1]), BF16)
    sinks = (sink_ref[2 * pair], sink_ref[2 * pair + 1])
    for g in range(LAT_LEN // SWA_QT):
        q0 = g * SWA_QT
        lo = min(max(q0 - SWA_WIN, 0), LAT_LEN - SWA_SPAN)
        near = jnp.abs((lo + col) - (q0 + row)) <= SWA_WIN
        qb = q_ref[q0:q0 + SWA_QT, :].astype(BF16)
        kt = kt_ref[0, :, lo:lo + SWA_SPAN].astype(BF16)
        vt = vt_ref[0, :, lo:lo + SWA_SPAN].astype(BF16)
        res, sink_den = [], []
        for half in range(2):
            k_op = jnp.concatenate([kt, z] if half == 0 else [z, kt], axis=0)
            v_op = jnp.concatenate([vt, z, ones] if half == 0 else [z, vt, ones], axis=0)
            ke_op = jnp.concatenate([ket, ze] if half == 0 else [ze, ket], axis=0)
            ve_op = jnp.concatenate([vet, ze, ones_e] if half == 0 else [ze, vet, ones_e], axis=0)
            s = jnp.where(near, jnp.dot(qb, k_op, preferred_element_type=F32), NEG_INF)
            se = jnp.dot(qb, ke_op, preferred_element_type=F32)
            m = jnp.maximum(jnp.max(s, axis=-1, keepdims=True), jnp.max(se, axis=-1, keepdims=True))
            m = jnp.maximum(m, sinks[half])
            sink_den.append(jnp.exp(sinks[half] - m))
            res.append(_dot_nt(jnp.exp(s - m), v_op) + _dot_nt(jnp.exp(se - m), ve_op))
        num = jnp.where(first, res[0][:, :128], res[1][:, :128])
        den = jnp.where(first, res[0][:, 128:], res[1][:, 128:]) + jnp.where(first, sink_den[0], sink_den[1])
        o_ref[q0:q0 + SWA_QT, :] = num / den


def _swa_latent(q, kt, vt, kt_ctx, vt_ctx, sink, n_batch, group):
    ppk = group // 2
    kv_spec = lambda a: pl.BlockSpec((1, DH, a.shape[2]), lambda b, j: (b, j // ppk, 0))
    return pl.pallas_call(
        _swa_kernel,
        out_shape=jax.ShapeDtypeStruct(q.shape, F32),
        grid=(n_batch, N_HEADS // 2),
        in_specs=[pl.BlockSpec((LAT_LEN, 128), lambda b, j: (b, j)),
                  kv_spec(kt), kv_spec(vt), kv_spec(kt_ctx), kv_spec(vt_ctx),
                  pl.BlockSpec(memory_space=pltpu.SMEM)],
        out_specs=pl.BlockSpec((LAT_LEN, 128), lambda b, j: (b, j)),
        compiler_params=_params(("arbitrary", "arbitrary")), name="swa_latent",
    )(q, kt, vt, kt_ctx, vt_ctx, sink)


def _rope_tables():
    t = np.arange(LAT_LEN)
    pos = np.stack([t // GRID_W, t % GRID_W], axis=-1).astype(np.float32)
    freqs = np.float32(ROPE_THETA) ** (-np.arange(ROPE_FREQS, dtype=np.float32) / np.float32(ROPE_FREQS))
    ang = (pos[:, :, None] * freqs).reshape(LAT_LEN, 2 * ROPE_FREQS).astype(np.float32)
    cos, sin = np.cos(ang), np.sin(ang)
    cos_t = np.tile(np.concatenate([cos, cos], axis=-1), (1, 2))
    sin_t = np.tile(np.concatenate([-sin, sin], axis=-1), (1, 2))
    return jnp.asarray(cos_t, F32), jnp.asarray(sin_t, F32)


def kernel(x_prompt, x_sample, cache_l0_attn_k, cache_l0_attn_v, state_l0_mlstm_C, state_l0_mlstm_n, state_l0_mlstm_m, cache_l1_na_k, cache_l1_na_v, cache_l1_swa_k, cache_l1_swa_v, c, c_ctx, norm_final, ada_w_l0, ada_b_l0, norm_l0, ffn1_in_l0, ffn1_out_l0, ffn2_in_l0, ffn2_out_l0, mix_in_l0, mix_out_l0, qk_norm_l0, gate_bias_l0, head_norm_l0, ada_w_l1, ada_b_l1, norm_l1, ffn1_in_l1, ffn1_out_l1, ffn2_in_l1, ffn2_out_l1, mix_in_l1, mix_out_l1, rpb_l1, sink_l1):
    nb, nl = x_prompt.shape[0], x_sample.shape[0]
    bf = lambda a: a.astype(BF16)

    cond8 = jnp.zeros((8, D_MODEL), F32).at[0].set(c_ctx).at[1:1 + nl].set(c)
    mods0 = _adaln(cond8, ada_w_l0, ada_b_l0).reshape(8 * ADA_CHUNKS, 1, D_MODEL)
    mods1 = _adaln(cond8, ada_w_l1, ada_b_l1).reshape(8 * ADA_CHUNKS, 1, D_MODEL)

    g0 = 2304
    gcols = lambda j: mix_in_l0[:, g0 + 8 * j:g0 + 8 * (j + 1)]
    gpad = jnp.zeros((D_MODEL, 128 - N_DIRHEAD), F32)
    w_gi, w_gf = jnp.concatenate([gcols(0), gcols(2)], axis=1), jnp.concatenate([gcols(1), gcols(3)], axis=1)
    w_even, w_ob = bf(mix_in_l0[:, :g0]), bf(mix_in_l0[:, g0 + 32:])
    w_g = bf(jnp.concatenate([w_gi, gpad, w_gf, gpad], axis=1))
    gb4 = gate_bias_l0.reshape(4, N_HEADS)
    b_gi, b_gf = jnp.concatenate([gb4[0], gb4[2]]), jnp.concatenate([gb4[1], gb4[3]])
    gbi = jnp.zeros((1, 128), F32).at[0, :N_DIRHEAD].set(b_gi)
    gbf = jnp.zeros((1, 128), F32).at[0, :N_DIRHEAD].set(b_gf)
    gbt = jnp.concatenate([b_gi, b_gf]).reshape(2 * N_DIRHEAD, 1)
    qg = jnp.tile(qk_norm_l0[0], N_HEADS).reshape(1, 512)
    kg = jnp.tile(qk_norm_l0[1], 2).reshape(1, 128)
    head_g = head_norm_l0.reshape(1, 512)
    grp = np.arange(256) // DH
    bd = jnp.asarray((grp[:, None] == grp[None, :]).astype(np.float32) / DH, dtype=BF16)
    rope_tabs = _rope_tables()
    w_odd = bf(mix_in_l1)
    rpb_flat = rpb_l1.reshape(-1)
    to_t = lambda a: a.reshape(a.shape[0], a.shape[1], -1).transpose(0, 2, 1)
    from_t = lambda a: a.reshape(a.shape[0], -1, DH, a.shape[2]).transpose(0, 3, 1, 2)

    xp = x_prompt.reshape(nb * CTX_LEN, D_MODEL)
    xs = x_sample.reshape(nl * LAT_LEN, D_MODEL)
    t_ctx, t_lat = xp.shape[0], xs.shape[0]
    streams = ((False, t_ctx, 0, nb, CTX_LEN), (True, t_lat, t_ctx // TOKEN_TILE, nl, LAT_LEN))
    ffn = functools.partial(_ffn, t_ctx=t_ctx, t_lat=t_lat)

    x = ffn((xp, xs), mods0, 0, norm_l0[0], ffn1_in_l0, ffn1_out_l0)
    ya, yb = {}, {}
    for latent, t, blk0, n_batch, seq_len in streams:
        qa, kat, vat, qb, vb, ob, gi, gf, gt, kbt = _proj_even(
            x, t, blk0, mods0, latent, norm_l0[1], w_even, w_ob, w_g, gbi, gbf, gbt, qg, kg, bd, rope_tabs)
        if latent:
            extra = (to_t(cache_l0_attn_k), to_t(cache_l0_attn_v))
            m0 = state_l0_mlstm_m.reshape(nl, N_DIRHEAD)
            m0_lanes = jnp.zeros((nl, 8, 128), F32).at[:, 0, :N_DIRHEAD].set(m0)
            m0_rows = jnp.broadcast_to(m0[:, :, None], (nl, N_DIRHEAD, 128))
            n0_cols = jnp.broadcast_to(state_l0_mlstm_n[..., None], (nl, 2, N_HEADS, DH, 128))
            state = (state_l0_mlstm_C, n0_cols, m0_lanes, m0_rows)
        else:
            extra, state = None, None
            k0t, v0t = kat, vat
        ya[latent] = _attention(qa, kat, vat, n_batch=n_batch, q_len=seq_len, q_tile=256, group=4, extra=extra)
        ml = _mlstm(qb, kbt, vb, ob, gi, gf, gt, head_g, bd, n_batch=n_batch, seq_len=seq_len, state=state)
        yb[latent] = ml[0]
        if not latent:
            c0, n0, m0_pad = ml[1:]
    x = ffn(x, mods0, 6, norm_l0[2], ffn2_in_l0, ffn2_out_l0, mix=(ya[False], yb[False], ya[True], yb[True], mix_out_l0))

    x = ffn(x, mods1, 0, norm_l1[0], ffn1_in_l1, ffn1_out_l1)
    yc, yd = {}, {}
    for latent, t, blk0, n_batch, seq_len in streams:
        qc, kc, vc, qd, kd, vd = _proj_odd(x, t, blk0, mods1, latent, norm_l1[1], w_odd, rope_tabs)
        if latent:
            yc[latent] = _na_latent(qc, kc, vc, cache_l1_na_k.reshape(nl * CTX_LEN, 512),
                                    cache_l1_na_v.reshape(nl * CTX_LEN, 512), rpb_flat, nl)
            yd[latent] = _swa_latent(qd, kd, vd, to_t(cache_l1_swa_k), to_t(cache_l1_swa_v), sink_l1, nl, group=4)
        else:
            yc[latent] = _attention(qc, kc, vc, n_batch=n_batch, q_len=seq_len, q_tile=256, group=1)
            yd[latent] = _attention(qd, kd, vd, n_batch=n_batch, q_len=seq_len, q_tile=256, group=4, sink=sink_l1)
            kc1t, vc1t, kd1t, vd1t = kc, vc, kd, vd
    y_prompt, y_sample = ffn(x, mods1, 6, norm_l1[2], ffn2_in_l1, ffn2_out_l1,
                             mix=(yc[False], yd[False], yc[True], yd[True], mix_out_l1), final_g=norm_final)

    return (y_prompt.reshape(nb, CTX_LEN, D_MODEL), y_sample.reshape(nl, LAT_LEN, D_MODEL),
            from_t(k0t), from_t(v0t),
            c0, n0, m0_pad[:, :, 0].reshape(nb, 2, N_HEADS),
            from_t(kc1t), from_t(vc1t),
            from_t(kd1t), from_t(vd1t))
```

```python
import functools

import jax
import jax.numpy as jnp
import numpy as np
from jax import lax
from jax.experimental import pallas as pl
from jax.experimental.pallas import tpu as pltpu

F32 = jnp.float32
BF16 = jnp.bfloat16

D_MODEL = 1024
DH = 64
D_FF = 2816
ADA_CHUNKS = 9
GRID_W = 64
LAT_LEN = 1024
CTX_LEN = 256
N_HEADS = 8
NA_KH = 8
NA_KW = 16
SWA_WIN = 128
ROPE_THETA = 10000.0
ROPE_FREQS = DH // 4
ATTN_SCALE = DH ** -0.5
NEG_INF = -1e30
EPS = 1e-6

TOKEN_TILE = 512
MLSTM_T = 256
FFN_CHUNK = 512
VMEM_LIMIT = 56 * 1024 * 1024
FFN_VMEM_LIMIT = 60 * 1024 * 1024
W_TILE = (256, 512)
W_SLOTS = 6


def _params(sem, vmem=VMEM_LIMIT):
    return pltpu.CompilerParams(dimension_semantics=sem, vmem_limit_bytes=vmem)


def _dot(a, b):
    return jnp.dot(a.astype(BF16), b.astype(BF16), preferred_element_type=F32)


def _dot_nt(a, b):
    return lax.dot_general(a.astype(BF16), b.astype(BF16), (((1,), (1,)), ((), ())),
                           preferred_element_type=F32)


def _split3(x):
    hi = x.astype(BF16)
    r1 = x - hi.astype(F32)
    mid = r1.astype(BF16)
    lo = (r1 - mid.astype(F32)).astype(BF16)
    return hi, mid, lo


def _dot_exact_rhs(a_bf16, x):
    hi, mid, lo = _split3(x)
    f = lambda p: jnp.dot(a_bf16, p, preferred_element_type=F32)
    return f(hi) + f(mid) + f(lo)


def _dot_exact_lhs(x, b_bf16):
    hi, mid, lo = _split3(x)
    f = lambda p: jnp.dot(p, b_bf16, preferred_element_type=F32)
    return f(hi) + f(mid) + f(lo)


def _silu(x):
    return x * jax.nn.sigmoid(x)


def _log_sigmoid(x):
    return jnp.minimum(x, 0.0) - jnp.log1p(jnp.exp(-jnp.abs(x)))


def _rms(x, g):
    return x * lax.rsqrt(jnp.mean(x * x, axis=-1, keepdims=True) + EPS) * g


def _head_mean_sq(x, bd_ref):
    n = x.shape[-1]
    sq = x * x
    hi = sq.astype(BF16)
    lo = (sq - hi.astype(F32)).astype(BF16)
    w = min(n, bd_ref.shape[0])
    bd = bd_ref[:w, :w]
    f = lambda p: jnp.dot(p, bd, preferred_element_type=F32)
    parts = [f(hi[:, c:c + w]) + f(lo[:, c:c + w]) for c in range(0, n, w)]
    return parts[0] if len(parts) == 1 else jnp.concatenate(parts, axis=1)


def _rope(x, cos, sin_signed):
    n = x.shape[-1]
    lane = lax.broadcasted_iota(jnp.int32, x.shape, 1)
    first_half = (lane % DH) < (DH // 2)
    partner = jnp.where(first_half, pltpu.roll(x, n - DH // 2, 1), pltpu.roll(x, DH // 2, 1))
    spread = lambda tab: tab if n == 128 else jnp.concatenate([tab] * (n // 128), axis=1)
    return x * spread(cos) + partner * spread(sin_signed)


def _adaln_kernel(c_ref, w_ref, b_ref, o_ref):
    s = _silu(c_ref[...])
    o_ref[...] = _dot(s, w_ref[...]) + b_ref[...]


def _adaln(cond8, w, b):
    n = w.shape[1]
    tn = 1536
    return pl.pallas_call(
        _adaln_kernel,
        out_shape=jax.ShapeDtypeStruct((8, n), F32),
        grid=(n // tn,),
        in_specs=[pl.BlockSpec((8, D_MODEL), lambda j: (0, 0)),
                  pl.BlockSpec((D_MODEL, tn), lambda j: (0, j)),
                  pl.BlockSpec((1, tn), lambda j: (0, j))],
        out_specs=pl.BlockSpec((8, tn), lambda j: (0, j)),
        compiler_params=_params(("arbitrary",)),
        name="adaln",
    )(cond8, w, b.reshape(1, n))


def _const_spec(shape):
    nd = len(shape)
    return pl.BlockSpec(shape, lambda i, _n=nd: (0,) * _n, pipeline_mode=pl.Buffered(1))


def _mod_spec(chunk, rowfn):
    return pl.BlockSpec((1, 1, D_MODEL), lambda i: (rowfn(i) * ADA_CHUNKS + chunk, 0, 0))


def _rowfn(latent):
    if latent:
        return lambda i: 1 + (i * TOKEN_TILE) // LAT_LEN
    return lambda i: 0


def _tok_spec(width, blk0=0):
    return pl.BlockSpec((TOKEN_TILE, width), lambda i: (i + blk0, 0))


def _seq_len(latent):
    return LAT_LEN if latent else CTX_LEN


def _transposed_out(t, latent, rows):
    seq = _seq_len(latent)
    shape = jax.ShapeDtypeStruct((t // seq, rows, seq), F32)
    if seq <= TOKEN_TILE:
        per = TOKEN_TILE // seq
        return shape, pl.BlockSpec((per, rows, seq), lambda i: (i, 0, 0))
    per = seq // TOKEN_TILE
    return shape, pl.BlockSpec((1, rows, TOKEN_TILE), lambda i: (i // per, 0, i % per))


def _store_transposed(ref, x):
    n, _, w = ref.shape
    for j in range(n):
        ref[j] = x[j * w:(j + 1) * w].T


def _dot_colsT(w_cols, h):
    return lax.dot_general(w_cols, h, (((0,), (1,)), ((), ())), preferred_element_type=F32)


def _store_proj_cols_transposed(ref, w_cols, h):
    n, _, w = ref.shape
    for j in range(n):
        ref[j] = _dot_colsT(w_cols, h[j * w:(j + 1) * w])


def _modnorm(x, ng_ref, sh_ref, sc_ref):
    return _rms(x, ng_ref[...] * (1.0 + sc_ref[0])) + sh_ref[0]


def _load_weights_bf16(pairs, stage, sem):
    tr, tc = W_TILE
    tiles = [(src, dst, r, c) for src, dst in pairs
             for r in range(0, src.shape[0], tr) for c in range(0, src.shape[1], tc)]

    def copy(k):
        src, _, r, c = tiles[k]
        return pltpu.make_async_copy(src.at[r:r + tr, c:c + tc], stage.at[k % W_SLOTS], sem.at[k % W_SLOTS])

    for k in range(min(W_SLOTS, len(tiles))):
        copy(k).start()
    for k, (_, dst, r, c) in enumerate(tiles):
        copy(k).wait()
        dst[r:r + tr, c:c + tc] = stage[k % W_SLOTS].astype(BF16)
        if k + W_SLOTS < len(tiles):
            copy(k + W_SLOTS).start()


def _ffn_kernel(*refs, n_ctx, first, has_mix, last):
    it = iter(refs)
    xs = (next(it), next(it)) if first else (next(it),)
    if has_mix:
        yac_ref, ybc_ref, yal_ref, ybl_ref, wo_hbm, g2_ref = (next(it) for _ in range(6))
    ng_ref, sh_ref, sc_ref, g_ref, win_hbm, wout_hbm = (next(it) for _ in range(6))
    if last:
        nf_ref = next(it)
    outs = (next(it), next(it)) if last else (next(it),)
    win_s, wout_s = next(it), next(it)
    if has_mix:
        wo_s = next(it)
    stage, sem = next(it), next(it)

    step = pl.program_id(0)
    is_lat = step >= n_ctx

    @pl.when(step == 0)
    def _():
        pairs = [(win_hbm, win_s), (wout_hbm, wout_s)]
        if has_mix:
            pairs.insert(0, (wo_hbm, wo_s))
        _load_weights_bf16(pairs, stage, sem)

    pick = lambda c_ref, l_ref: jnp.where(is_lat, l_ref[...], c_ref[...])
    x = pick(*xs) if first else xs[0][...]
    if has_mix:
        half = wo_s.shape[0] // 2
        mo = _dot(pick(yac_ref, yal_ref), wo_s[:half, :]) + _dot(pick(ybc_ref, ybl_ref), wo_s[half:, :])
        x = x + g2_ref[0] * mo
    h = _modnorm(x, ng_ref, sh_ref, sc_ref).astype(BF16)
    acc = None
    for lo in range(0, D_FF, FFN_CHUNK):
        hi = min(lo + FFN_CHUNK, D_FF)
        g = _dot(h, win_s[:, lo:hi])
        u = _dot(h, win_s[:, D_FF + lo:D_FF + hi])
        part = _dot(_silu(g) * u, wout_s[lo:hi, :])
        acc = part if acc is None else acc + part
    y = x + (0.5 * g_ref[0]) * acc
    if last:
        y = _rms(y, nf_ref[...])
        oc_ref, ol_ref = outs

        @pl.when(jnp.logical_not(is_lat))
        def _():
            oc_ref[...] = y
            ol_ref[...] = jnp.zeros_like(ol_ref)

        @pl.when(is_lat)
        def _():
            ol_ref[...] = y
    else:
        outs[0][...] = y


def _ffn(x, mods, mod_base, ng, w_in, w_out, *, t_ctx, t_lat, mix=None, final_g=None):
    tile = TOKEN_TILE
    n_ctx, n_lat = t_ctx // tile, t_lat // tile
    first, last = isinstance(x, tuple), final_g is not None
    ctx_map = lambda i: (jnp.minimum(i, n_ctx - 1), 0)
    lat_map = lambda i: (jnp.maximum(i - n_ctx, 0), 0)
    rowfn = lambda i: jnp.where(i < n_ctx, 0, 1 + ((i - n_ctx) * tile) // LAT_LEN)
    two = lambda wd: [pl.BlockSpec((tile, wd), ctx_map), pl.BlockSpec((tile, wd), lat_map)]
    one = pl.BlockSpec((tile, D_MODEL), lambda i: (i, 0))
    hbm = pl.BlockSpec(memory_space=pl.ANY)
    args = list(x) if first else [x]
    specs = two(D_MODEL) if first else [one]
    scratch = [pltpu.VMEM(w_in.shape, BF16), pltpu.VMEM(w_out.shape, BF16)]
    if mix is not None:
        yac, ybc, yal, ybl, wo = mix
        args += [yac, ybc, yal, ybl, wo, mods]
        mw = yac.shape[1]
        specs += [pl.BlockSpec((tile, mw), ctx_map)] * 2 + [pl.BlockSpec((tile, mw), lat_map)] * 2
        specs += [hbm, _mod_spec(mod_base - 1, rowfn)]
        scratch.append(pltpu.VMEM(wo.shape, BF16))
    args += [ng.reshape(1, D_MODEL), mods, mods, mods, w_in, w_out]
    specs += [_const_spec((1, D_MODEL)), _mod_spec(mod_base, rowfn), _mod_spec(mod_base + 1, rowfn),
              _mod_spec(mod_base + 2, rowfn), hbm, hbm]
    if last:
        args.append(final_g.reshape(1, D_MODEL))
        specs.append(_const_spec((1, D_MODEL)))
        out_shape = [jax.ShapeDtypeStruct((t_ctx, D_MODEL), F32), jax.ShapeDtypeStruct((t_lat, D_MODEL), F32)]
        out_specs = two(D_MODEL)
    else:
        out_shape = jax.ShapeDtypeStruct((t_ctx + t_lat, D_MODEL), F32)
        out_specs = one
    scratch += [pltpu.VMEM((W_SLOTS,) + W_TILE, F32), pltpu.SemaphoreType.DMA((W_SLOTS,))]
    return pl.pallas_call(
        functools.partial(_ffn_kernel, n_ctx=n_ctx, first=first, has_mix=mix is not None, last=last),
        out_shape=out_shape,
        grid=(n_ctx + n_lat,),
        in_specs=specs,
        out_specs=out_specs,
        scratch_shapes=scratch,
        compiler_params=_params(("arbitrary",), FFN_VMEM_LIMIT),
        name="ffn",
    )(*args)


_EV_QA, _EV_KA, _EV_VA, _EV_QB, _EV_KB, _EV_VB, _EV_OB = 0, 512, 640, 768, 1280, 1792, 2304
N_DIRHEAD = 2 * N_HEADS


def _proj_even_kernel(*refs, rope):
    it = iter(refs)
    x_ref, ng_ref, sh_ref, sc_ref, w_ref, wob_ref, wg_ref, gbi_ref, gbf_ref, gbt_ref, qg_ref, kg_ref, bd_ref = (
        next(it) for _ in range(13))
    if rope:
        cos_ref, sin_ref = next(it), next(it)
    qa_ref, kat_ref, vat_ref, qb_ref, vb_ref, ob_ref, gi_ref, gf_ref, gt_ref, kbt_ref = (next(it) for _ in range(10))

    h = _modnorm(x_ref[...], ng_ref, sh_ref, sc_ref).astype(BF16)
    proj = lambda lo, hi: _dot(h, w_ref[:, lo:hi])

    qa = proj(_EV_QA, _EV_KA)
    qa = qa * lax.rsqrt(_head_mean_sq(qa, bd_ref) + EPS) * qg_ref[...]
    ka = proj(_EV_KA, _EV_VA)
    ka = ka * lax.rsqrt(_head_mean_sq(ka, bd_ref) + EPS) * kg_ref[...]
    if rope:
        qa = _rope(qa, cos_ref[...], sin_ref[...])
        ka = _rope(ka, cos_ref[...], sin_ref[...])
    qa_ref[...] = qa * ATTN_SCALE
    _store_transposed(kat_ref, ka)
    _store_proj_cols_transposed(vat_ref, w_ref[:, _EV_VA:_EV_QB], h)
    qb_ref[...] = proj(_EV_QB, _EV_KB)
    vb_ref[...] = proj(_EV_VB, _EV_OB)
    ob_ref[...] = _dot(h, wob_ref[...])
    gi_ref[...] = _dot(h, wg_ref[:, :128]) + gbi_ref[...]
    gf_ref[...] = _dot(h, wg_ref[:, 128:]) + gbf_ref[...]
    for j in range(TOKEN_TILE // MLSTM_T):
        ht = h[j * MLSTM_T:(j + 1) * MLSTM_T]
        gates_t = [_dot_colsT(wg_ref[:, c0:c0 + N_DIRHEAD], ht) for c0 in (0, 128)]
        gt_ref[j] = jnp.concatenate(gates_t, axis=0) + gbt_ref[...]
        kbt_ref[j] = _dot_colsT(w_ref[:, _EV_KB:_EV_VB], ht) * ATTN_SCALE


def _proj_even(x, t, blk0, mods, latent, ng, w, w_ob, w_g, gbi, gbf, gbt, qg, kg, bd, rope_tabs):
    rowfn = _rowfn(latent)
    args = [x, ng.reshape(1, D_MODEL), mods, mods, w, w_ob, w_g, gbi, gbf, gbt, qg, kg, bd]
    specs = [_tok_spec(D_MODEL, blk0), _const_spec((1, D_MODEL)), _mod_spec(3, rowfn), _mod_spec(4, rowfn)]
    specs += [_const_spec(a.shape) for a in args[4:]]
    if latent:
        nblk = LAT_LEN // TOKEN_TILE
        args += list(rope_tabs)
        specs += [pl.BlockSpec((TOKEN_TILE, 128), lambda i: (i % nblk, 0))] * 2
    widths = (512, 512, 512, 512, 128, 128)
    out_shape = [jax.ShapeDtypeStruct((t, wd), F32) for wd in widths]
    out_specs = [_tok_spec(wd) for wd in widths]
    for pos in (1, 2):
        shape, spec = _transposed_out(t, latent, 128)
        out_shape.insert(pos, shape)
        out_specs.insert(pos, spec)
    nj = TOKEN_TILE // MLSTM_T
    for rows in (2 * N_DIRHEAD, 512):
        out_shape.append(jax.ShapeDtypeStruct((t // MLSTM_T, rows, MLSTM_T), F32))
        out_specs.append(pl.BlockSpec((nj, rows, MLSTM_T), lambda i: (i, 0, 0)))
    return pl.pallas_call(
        functools.partial(_proj_even_kernel, rope=latent),
        out_shape=out_shape, grid=(t // TOKEN_TILE,), in_specs=specs, out_specs=out_specs,
        compiler_params=_params(("arbitrary",)), name="proj_even",
    )(*args)


def _proj_odd_kernel(*refs, rope):
    it = iter(refs)
    x_ref, ng_ref, sh_ref, sc_ref, w_ref = (next(it) for _ in range(5))
    if rope:
        cos_ref, sin_ref = next(it), next(it)
    qc_ref, kc_ref, vc_ref, qd_ref, kd_ref, vd_ref = (next(it) for _ in range(6))

    h = _modnorm(x_ref[...], ng_ref, sh_ref, sc_ref).astype(BF16)
    proj = lambda lo, hi: _dot(h, w_ref[:, lo:hi])
    qc_ref[...] = proj(0, 512) * ATTN_SCALE
    if rope:
        kc_ref[...] = proj(512, 1024)
        vc_ref[...] = proj(1024, 1536)
    else:
        _store_proj_cols_transposed(kc_ref, w_ref[:, 512:1024], h)
        _store_proj_cols_transposed(vc_ref, w_ref[:, 1024:1536], h)
    qd = proj(1536, 2048)
    if rope:
        qd = _rope(qd, cos_ref[...], sin_ref[...])
        _store_transposed(kd_ref, _rope(proj(2048, 2176), cos_ref[...], sin_ref[...]))
    else:
        _store_proj_cols_transposed(kd_ref, w_ref[:, 2048:2176], h)
    _store_proj_cols_transposed(vd_ref, w_ref[:, 2176:2304], h)
    qd_ref[...] = qd * ATTN_SCALE


def _proj_odd(x, t, blk0, mods, latent, ng, w, rope_tabs):
    rowfn = _rowfn(latent)
    args = [x, ng.reshape(1, D_MODEL), mods, mods, w]
    specs = [_tok_spec(D_MODEL, blk0), _const_spec((1, D_MODEL)), _mod_spec(3, rowfn), _mod_spec(4, rowfn),
             _const_spec(w.shape)]
    if latent:
        nblk = LAT_LEN // TOKEN_TILE
        args += list(rope_tabs)
        specs += [pl.BlockSpec((TOKEN_TILE, 128), lambda i: (i % nblk, 0))] * 2
    out_shape = [jax.ShapeDtypeStruct((t, 512), F32)] * 4
    out_specs = [_tok_spec(512)] * 4
    if not latent:
        out_shape[1], out_specs[1] = _transposed_out(t, latent, 512)
        out_shape[2], out_specs[2] = out_shape[1], out_specs[1]
    for _ in range(2):
        shape, spec = _transposed_out(t, latent, 128)
        out_shape.append(shape)
        out_specs.append(spec)
    return pl.pallas_call(
        functools.partial(_proj_odd_kernel, rope=latent),
        out_shape=out_shape,
        grid=(t // TOKEN_TILE,), in_specs=specs, out_specs=out_specs,
        compiler_params=_params(("arbitrary",)), name="proj_odd",
    )(*args)


def _attn_kernel(*refs, group, has_extra, has_sink, spb):
    it = iter(refs)
    q_ref, k_ref, v_ref = next(it), next(it), next(it)
    if has_extra:
        ke_ref, ve_ref = next(it), next(it)
    if has_sink:
        sink_ref = next(it)
    o_ref = next(it)
    tq = q_ref.shape[0] // spb
    first = lax.broadcasted_iota(jnp.int32, (tq, 128), 1) < DH

    def kv_operands(kt_ref, vt_ref, smp, kv):
        kt = kt_ref[smp, kv * DH:(kv + 1) * DH, :].astype(BF16)
        vt = vt_ref[smp, kv * DH:(kv + 1) * DH, :].astype(BF16)
        z = jnp.zeros_like(kt)
        ones = jnp.ones((128, kt.shape[1]), BF16)
        return ((jnp.concatenate([kt, z], axis=0), jnp.concatenate([z, kt], axis=0)),
                (jnp.concatenate([vt, z, ones], axis=0), jnp.concatenate([z, vt, ones], axis=0)))

    for smp in range(spb):
        qrows = slice(smp * tq, (smp + 1) * tq)
        cache = {}
        for j in range(N_HEADS // 2):
            ps = slice(128 * j, 128 * (j + 1))
            qb = q_ref[qrows, ps].astype(BF16)
            res, sink_den = [], []
            for half in range(2):
                kv = (2 * j + half) // group
                if kv not in cache:
                    cache[kv] = (kv_operands(k_ref, v_ref, smp, kv),
                                 kv_operands(ke_ref, ve_ref, smp, kv) if has_extra else None)
                (k_ops, v_ops), extra = cache[kv]
                s = jnp.dot(qb, k_ops[half], preferred_element_type=F32)
                m = jnp.max(s, axis=-1, keepdims=True)
                if has_extra:
                    se = jnp.dot(qb, extra[0][half], preferred_element_type=F32)
                    m = jnp.maximum(m, jnp.max(se, axis=-1, keepdims=True))
                if has_sink:
                    m = jnp.maximum(m, sink_ref[2 * j + half])
                    sink_den.append(jnp.exp(sink_ref[2 * j + half] - m))
                r = _dot_nt(jnp.exp(s - m), v_ops[half])
                if has_extra:
                    r = r + _dot_nt(jnp.exp(se - m), extra[1][half])
                res.append(r)
            num = jnp.where(first, res[0][:, :128], res[1][:, :128])
            den = jnp.where(first, res[0][:, 128:], res[1][:, 128:])
            if has_sink:
                den = den + jnp.where(first, sink_den[0], sink_den[1])
            o_ref[qrows, ps] = num / den


def _attention(q, k, v, *, n_batch, q_len, q_tile, group, extra=None, sink=None):
    nq = q_len // q_tile
    spb = next(n for n in (8, 4, 2, 1) if n_batch % n == 0) if nq == 1 else 1
    kv_spec = lambda a: pl.BlockSpec((spb,) + a.shape[1:], lambda b, j: (b, 0, 0))
    args = [q, k, v]
    specs = [pl.BlockSpec((spb * q_tile, 512), lambda b, j: (b * nq + j, 0)), kv_spec(k), kv_spec(v)]
    if extra is not None:
        args += list(extra)
        specs += [kv_spec(extra[0]), kv_spec(extra[1])]
    if sink is not None:
        args.append(sink)
        specs.append(pl.BlockSpec(memory_space=pltpu.SMEM))
    return pl.pallas_call(
        functools.partial(_attn_kernel, group=group, has_extra=extra is not None, has_sink=sink is not None, spb=spb),
        out_shape=jax.ShapeDtypeStruct(q.shape, F32),
        grid=(n_batch // spb, nq), in_specs=specs,
        out_specs=pl.BlockSpec((spb * q_tile, 512), lambda b, j: (b * nq + j, 0)),
        compiler_params=_params(("arbitrary", "arbitrary")), name="attn",
    )(*args)


LOG2E = 1.4426950408889634


def _cummax_time(x, reverse):
    n = x.shape[0]
    row = lax.broadcasted_iota(jnp.int32, x.shape, 0)
    k = 1
    while k < n:
        if reverse:
            shifted = jnp.where(row < n - k, pltpu.roll(x, n - k, 0), -jnp.inf)
        else:
            shifted = jnp.where(row >= k, pltpu.roll(x, k, 0), -jnp.inf)
        x = jnp.maximum(x, shifted)
        k *= 2
    return x


def _mlstm_kernel(*refs, n_chunks, carry, spb):
    it = iter(refs)
    q_ref, kt_ref, v_ref, ob_ref, gi_ref, gf_ref, gt_ref, hg_ref, bd_ref = (next(it) for _ in range(9))
    if carry:
        c0_ref, n0_ref, m0r_ref, m0c_ref = (next(it) for _ in range(4))
    y_ref = next(it)
    if not carry:
        c_ref, n_ref, m_ref = next(it), next(it), next(it)
    hf_s, hb_s = next(it), next(it)
    if carry:
        st_s, mrow_s, mcol_s = next(it), next(it), next(it)
    T = MLSTM_T
    step = pl.program_id(1)

    if carry:
        @pl.when(step == 0)
        def _():
            z = jnp.zeros((DH, DH), F32)
            for d in range(2):
                for j in range(N_HEADS // 2):
                    ca, cb = c0_ref[0, d, 2 * j], c0_ref[0, d, 2 * j + 1]
                    na, nb = n0_ref[0, d, 2 * j][:, :DH], n0_ref[0, d, 2 * j + 1][:, :DH]
                    st_s[d, j, 0:DH, :] = jnp.concatenate([ca, z, na, z], axis=1)
                    st_s[d, j, DH:2 * DH, :] = jnp.concatenate([z, cb, z, nb], axis=1)
            mrow_s[...] = m0r_ref[0]
            mcol_s[...] = m0c_ref[0]

    row = lax.broadcasted_iota(jnp.int32, (T, T), 0)
    col = lax.broadcasted_iota(jnp.int32, (T, T), 1)
    lower = row >= col
    upper = row <= col
    lower_m = jnp.where(lower, 1.0, 0.0).astype(BF16)
    upper_m = jnp.where(upper, 1.0, 0.0).astype(BF16)
    lane = lax.broadcasted_iota(jnp.int32, (T, 128), 1)
    first = lane < DH
    lane_c = lax.broadcasted_iota(jnp.int32, (DH, 256), 1) % 128
    zeros_kt = jnp.zeros((DH, T), BF16)
    ones_v = jnp.ones((T, 128), BF16)

    for smp, d in ((s_, d_) for s_ in range(spb) for d_ in range(2)):
        chunk = (step if d == 0 else n_chunks - 1 - step) if carry else smp
        rows = pl.ds(pl.multiple_of(chunk * T, T), T)
        hd0 = N_HEADS * d
        mask = lower if d == 0 else upper

        if carry or d == 0:
            gi = gi_ref[rows, :]
            lf = _log_sigmoid(gf_ref[rows, :])
            prefix = _dot_exact_rhs(lower_m, lf)
        b_cols = prefix if d == 0 else prefix[T - 1:T, :] - prefix + lf
        a_cols = gi - b_cols
        dmax = b_cols + _cummax_time(a_cols, reverse=d == 1)
        if carry:
            inter = b_cols + mrow_s[0:1, :]
            mt = jnp.maximum(inter, dmax)
            w_inter = jnp.exp(inter - mt)
        else:
            mt = jnp.maximum(b_cols, dmax)
        c2 = (b_cols - mt) * LOG2E
        einv = jnp.exp(-mt)

        gt = gt_ref[chunk]
        gi_t = gt[hd0:hd0 + N_HEADS, :]
        gf_t = gt[N_DIRHEAD + hd0:N_DIRHEAD + hd0 + N_HEADS, :]
        lf_t = _log_sigmoid(gf_t)
        b_rows = _dot_exact_lhs(lf_t, upper_m)
        if d == 1:
            b_rows = b_rows[:, T - 1:T] - b_rows + lf_t
        a_rows = gi_t - b_rows
        a2 = a_rows * LOG2E
        b_last = b_rows[:, T - 1:T] if d == 0 else b_rows[:, 0:1]
        dec_rows = b_last + a_rows
        m_new = jnp.max(dec_rows, axis=-1, keepdims=True)
        if carry:
            m_prev = mcol_s[hd0:hd0 + N_HEADS, 0:1]
            m_new = jnp.maximum(m_new, b_last + m_prev)
            wc = jnp.exp(b_last + m_prev - m_new)
        else:
            m_new = jnp.maximum(m_new, b_last)
        ws_rows = jnp.exp(dec_rows - m_new)

        h_s = hf_s if d == 0 else hb_s
        for j in range(N_HEADS // 2):
            ps = slice(128 * j, 128 * (j + 1))
            q_pair = q_ref[rows, ps].astype(BF16)
            v_aug = jnp.concatenate([v_ref[rows, ps].astype(BF16), ones_v], axis=1)
            kt = (kt_ref[chunk, 128 * j:128 * j + DH, :], kt_ref[chunk, 128 * j + DH:128 * (j + 1), :])
            lhs = []
            for half in range(2):
                hd = 2 * j + half
                kt0 = jnp.concatenate([kt[0].astype(BF16), zeros_kt] if half == 0 else [zeros_kt, kt[1].astype(BF16)], axis=0)
                qk = jnp.dot(q_pair, kt0, preferred_element_type=F32)
                e = c2[:, hd0 + hd:hd0 + hd + 1] + a2[hd:hd + 1, :]
                lhs.append((jnp.exp2(jnp.where(mask, e, -jnp.inf)) * qk).astype(BF16))
            for half in range(2):
                lhs.append((kt[half] * ws_rows[2 * j + half:2 * j + half + 1, :]).astype(BF16))
            both = jnp.dot(jnp.concatenate(lhs, axis=0), v_aug, preferred_element_type=F32)
            res = (both[0:T], both[T:2 * T])
            cu = (both[2 * T:2 * T + DH], both[2 * T + DH:2 * T + 2 * DH])
            ra, rb = hd0 + 2 * j, hd0 + 2 * j + 1
            num = jnp.where(first, res[0][:, :128], res[1][:, :128])
            den = jnp.where(first, res[0][:, 128:], res[1][:, 128:])
            if carry:
                old = st_s[d, j]
                qs = _dot(q_pair, old)
                wi = jnp.where(first, w_inter[:, ra:ra + 1], w_inter[:, rb:rb + 1])
                num = num + wi * qs[:, :128]
                den = den + wi * qs[:, 128:]
            floor = jnp.where(first, einv[:, ra:ra + 1], einv[:, rb:rb + 1])
            h_s[rows, ps] = num / jnp.maximum(jnp.abs(den), floor)
            if carry:
                st_s[d, j, 0:DH, :] = jnp.where(lane_c < DH, cu[0], 0.0) + wc[2 * j:2 * j + 1, :] * old[0:DH]
                st_s[d, j, DH:2 * DH, :] = jnp.where(lane_c >= DH, cu[1], 0.0) + wc[2 * j + 1:2 * j + 2, :] * old[DH:2 * DH]
            else:
                c_ref[smp, d, 2 * j] = cu[0][:, 0:DH]
                c_ref[smp, d, 2 * j + 1] = cu[1][:, DH:2 * DH]

        if carry:
            b_last_l = b_cols[T - 1:T, :] if d == 0 else b_cols[0:1, :]
            m_prev_l = mrow_s[0:1, :]
            m_new_l = jnp.maximum(jnp.max(b_last_l + a_cols, axis=0, keepdims=True), b_last_l + m_prev_l)
            lane_r = lax.broadcasted_iota(jnp.int32, (1, 128), 1)
            mine = (lane_r >= hd0) & (lane_r < hd0 + N_HEADS)
            mrow_s[0:1, :] = jnp.where(mine, m_new_l, m_prev_l)
            mcol_s[hd0:hd0 + N_HEADS, :] = jnp.broadcast_to(m_new, (N_HEADS, 128))
        else:
            for hd in range(N_HEADS):
                n_hd = _dot_nt(ws_rows, kt_ref[chunk, hd * DH:(hd + 1) * DH, :])
                n_ref[smp, d, hd:hd + 1, :] = n_hd[hd:hd + 1, :]
            m_ref[smp, hd0:hd0 + N_HEADS, :] = jnp.broadcast_to(m_new, (N_HEADS, 128))

    @pl.when(step == n_chunks - 1)
    def _():
        hsum = hf_s[...] + hb_s[...]
        y = hsum * lax.rsqrt(_head_mean_sq(hsum, bd_ref) + EPS) * hg_ref[...]
        y_ref[...] = jax.nn.sigmoid(ob_ref[...]) * y


def _mlstm(qb, kbt, vb, ob, gi, gf, gt, head_g, bd, *, n_batch, seq_len, state=None):
    n_chunks = seq_len // MLSTM_T
    carry = state is not None
    assert carry or n_chunks == 1
    spb = 1
    seq = lambda wd: pl.BlockSpec((spb * seq_len, wd), lambda b, c: (b, 0))
    per_chunk = lambda r: pl.BlockSpec((spb * n_chunks, r, MLSTM_T), lambda b, c: (b, 0, 0))
    args = [qb, kbt, vb, ob, gi, gf, gt, head_g, bd]
    specs = [seq(512), per_chunk(512), seq(512), seq(512), seq(128), seq(128), per_chunk(2 * N_DIRHEAD),
             pl.BlockSpec((1, 512), lambda b, c: (0, 0)), pl.BlockSpec(bd.shape, lambda b, c: (0, 0))]
    out_shape = [jax.ShapeDtypeStruct((n_batch * seq_len, 512), F32)]
    out_specs = [seq(512)]
    scratch = [pltpu.VMEM((spb * seq_len, 512), F32), pltpu.VMEM((spb * seq_len, 512), F32)]
    if carry:
        args += list(state)
        specs += [pl.BlockSpec((1, 2, N_HEADS, DH, DH), lambda b, c: (b, 0, 0, 0, 0)),
                  pl.BlockSpec((1, 2, N_HEADS, DH, 128), lambda b, c: (b, 0, 0, 0, 0)),
                  pl.BlockSpec((1, 8, 128), lambda b, c: (b, 0, 0)),
                  pl.BlockSpec((1, N_DIRHEAD, 128), lambda b, c: (b, 0, 0))]
        scratch += [pltpu.VMEM((2, N_HEADS // 2, 128, 256), F32), pltpu.VMEM((8, 128), F32),
                    pltpu.VMEM((N_DIRHEAD, 128), F32)]
    else:
        out_shape += [jax.ShapeDtypeStruct((n_batch, 2, N_HEADS, DH, DH), F32),
                      jax.ShapeDtypeStruct((n_batch, 2, N_HEADS, DH), F32),
                      jax.ShapeDtypeStruct((n_batch, N_DIRHEAD, 128), F32)]
        out_specs += [pl.BlockSpec((spb, 2, N_HEADS, DH, DH), lambda b, c: (b, 0, 0, 0, 0)),
                      pl.BlockSpec((spb, 2, N_HEADS, DH), lambda b, c: (b, 0, 0, 0)),
                      pl.BlockSpec((spb, N_DIRHEAD, 128), lambda b, c: (b, 0, 0))]
    return pl.pallas_call(
        functools.partial(_mlstm_kernel, n_chunks=n_chunks, carry=carry, spb=spb),
        out_shape=out_shape, grid=(n_batch // spb, n_chunks), in_specs=specs, out_specs=out_specs,
        scratch_shapes=scratch,
        compiler_params=_params(("arbitrary", "arbitrary")), name="mlstm",
    )(*args)


N_ROWS = LAT_LEN // GRID_W
N_DY = 2 * NA_KH - 1
N_DX = 2 * NA_KW - 1


NA_QROWS = 4
NA_KROWS = 12
NA_GROUPS = N_ROWS // NA_QROWS


def _na_kernel(q_ref, k_ref, v_ref, ke_ref, ve_ref, rpb_ref, o_ref, tile_s, slab_s):
    pair = pl.program_id(1)
    qn, kn = NA_QROWS * GRID_W, NA_KROWS * GRID_W

    @pl.when(pl.program_id(0) == 0)
    def _():
        qc = lax.broadcasted_iota(jnp.int32, (GRID_W, GRID_W), 0)
        kc = lax.broadcasted_iota(jnp.int32, (GRID_W, GRID_W), 1)
        start = jnp.clip(qc - NA_KW // 2, 0, GRID_W - NA_KW)
        in_win = (kc >= start) & (kc < start + NA_KW)
        dx = kc - qc + (NA_KW - 1)
        blocked = jnp.full((GRID_W, GRID_W), NEG_INF, F32)
        for half in range(2):
            head = 2 * pair + half

            def build_tile(dy, carry):
                tile = blocked
                for j in range(N_DX):
                    tile = jnp.where(dx == j, rpb_ref[(head * N_DY + dy) * N_DX + j], tile)
                tile_s[half * N_DY + dy] = jnp.where(in_win, tile, NEG_INF)
                return carry

            lax.fori_loop(0, N_DY, build_tile, 0)
            for kind in range(3):
                idx = (pair * 2 + half) * 3 + kind
                for i in range(NA_QROWS):
                    for k in range(NA_KROWS):
                        if kind == 0:
                            ok, dy = k < NA_KH, k - i + NA_KH - 1
                        elif kind == 1:
                            ok, dy = i <= k < i + NA_KH, k - i + NA_KH // 2 - 1
                        else:
                            ok, dy = k >= NA_KROWS - NA_KH, k - i - 1
                        slab_s[idx, i * GRID_W:(i + 1) * GRID_W, k * GRID_W:(k + 1) * GRID_W] = (
                            tile_s[half * N_DY + dy] if ok else blocked)

    first = lax.broadcasted_iota(jnp.int32, (qn, 128), 1) < DH
    ones_k = jnp.ones((kn, 128), BF16)
    ke = ke_ref[...].astype(BF16)
    ve_aug = jnp.concatenate([ve_ref[...].astype(BF16), jnp.ones((ke.shape[0], 128), BF16)], axis=1)

    def group_body(g, carry):
        k0 = jnp.where(g < NA_GROUPS // 2, 0, N_ROWS - NA_KROWS)
        kind = jnp.where(g == 0, 0, jnp.where(g == NA_GROUPS - 1, 2, 1))
        qrows = pl.ds(pl.multiple_of(g * qn, qn), qn)
        keys = pl.ds(pl.multiple_of(k0 * GRID_W, GRID_W), kn)
        q_pair = q_ref[qrows, :]
        k_pair = k_ref[keys, :].astype(BF16)
        v_aug = jnp.concatenate([v_ref[keys, :].astype(BF16), ones_k], axis=1)
        res = []
        for half in range(2):
            qm = jnp.where(first, q_pair, 0.0) if half == 0 else jnp.where(first, 0.0, q_pair)
            s = _dot_nt(qm, k_pair) + slab_s[(pair * 2 + half) * 3 + kind]
            se = _dot_nt(qm, ke)
            m = jnp.maximum(jnp.max(s, axis=-1, keepdims=True), jnp.max(se, axis=-1, keepdims=True))
            res.append(_dot(jnp.exp(s - m), v_aug) + _dot(jnp.exp(se - m), ve_aug))
        num = jnp.where(first, res[0][:, :128], res[1][:, :128])
        den = jnp.where(first, res[0][:, 128:], res[1][:, 128:])
        o_ref[qrows, :] = num / den
        return carry

    lax.fori_loop(0, NA_GROUPS, group_body, 0, unroll=True)


def _na_latent(q, k, v, k_ctx, v_ctx, rpb_flat, n_batch):
    seq = pl.BlockSpec((LAT_LEN, 128), lambda b, j: (b, j))
    ctx = pl.BlockSpec((CTX_LEN, 128), lambda b, j: (b, j))
    return pl.pallas_call(
        _na_kernel,
        out_shape=jax.ShapeDtypeStruct(q.shape, F32),
        grid=(n_batch, N_HEADS // 2),
        in_specs=[seq, seq, seq, ctx, ctx, pl.BlockSpec(memory_space=pltpu.SMEM)],
        out_specs=seq,
        scratch_shapes=[pltpu.VMEM((2 * N_DY, GRID_W, GRID_W), F32),
                        pltpu.VMEM((N_HEADS * 3, NA_QROWS * GRID_W, NA_KROWS * GRID_W), F32)],
        compiler_params=_params(("arbitrary", "arbitrary")), name="na_latent",
    )(q, k, v, k_ctx, v_ctx, rpb_flat)


SWA_QT = 256
SWA_SPAN = SWA_QT + 2 * SWA_WIN


def _swa_kernel(q_ref, kt_ref, vt_ref, ket_ref, vet_ref, sink_ref, o_ref):
    pair = pl.program_id(1)
    first = lax.broadcasted_iota(jnp.int32, (SWA_QT, 128), 1) < DH
    row = lax.broadcasted_iota(jnp.int32, (SWA_QT, SWA_SPAN), 0)
    col = lax.broadcasted_iota(jnp.int32, (SWA_QT, SWA_SPAN), 1)
    z = jnp.zeros((DH, SWA_SPAN), BF16)
    ones = jnp.ones((128, SWA_SPAN), BF16)
    ket, vet = ket_ref[0].astype(BF16), vet_ref[0].astype(BF16)
    ze = jnp.zeros_like(ket)
    ones_e = jnp.ones((128, ket.shape[1]), BF16)
    sinks = (sink_ref[2 * pair], sink_ref[2 * pair + 1])
    for g in range(LAT_LEN // SWA_QT):
        q0 = g * SWA_QT
        lo = min(max(q0 - SWA_WIN, 0), LAT_LEN - SWA_SPAN)
        near = jnp.abs((lo + col) - (q0 + row)) <= SWA_WIN
        qb = q_ref[q0:q0 + SWA_QT, :].astype(BF16)
        kt = kt_ref[0, :, lo:lo + SWA_SPAN].astype(BF16)
        vt = vt_ref[0, :, lo:lo + SWA_SPAN].astype(BF16)
        res, sink_den = [], []
        for half in range(2):
            k_op = jnp.concatenate([kt, z] if half == 0 else [z, kt], axis=0)
            v_op = jnp.concatenate([vt, z, ones] if half == 0 else [z, vt, ones], axis=0)
            ke_op = jnp.concatenate([ket, ze] if half == 0 else [ze, ket], axis=0)
            ve_op = jnp.concatenate([vet, ze, ones_e] if half == 0 else [ze, vet, ones_e], axis=0)
            s = jnp.where(near, jnp.dot(qb, k_op, preferred_element_type=F32), NEG_INF)
            se = jnp.dot(qb, ke_op, preferred_element_type=F32)
            m = jnp.maximum(jnp.max(s, axis=-1, keepdims=True), jnp.max(se, axis=-1, keepdims=True))
            m = jnp.maximum(m, sinks[half])
            sink_den.append(jnp.exp(sinks[half] - m))
            res.append(_dot_nt(jnp.exp(s - m), v_op) + _dot_nt(jnp.exp(se - m), ve_op))
        num = jnp.where(first, res[0][:, :128], res[1][:, :128])
        den = jnp.where(first, res[0][:, 128:], res[1][:, 128:]) + jnp.where(first, sink_den[0], sink_den[1])
        o_ref[q0:q0 + SWA_QT, :] = num / den


def _swa_latent(q, kt, vt, kt_ctx, vt_ctx, sink, n_batch, group):
    ppk = group // 2
    kv_spec = lambda a: pl.BlockSpec((1, DH, a.shape[2]), lambda b, j: (b, j // ppk, 0))
    return pl.pallas_call(
        _swa_kernel,
        out_shape=jax.ShapeDtypeStruct(q.shape, F32),
        grid=(n_batch, N_HEADS // 2),
        in_specs=[pl.BlockSpec((LAT_LEN, 128), lambda b, j: (b, j)),
                  kv_spec(kt), kv_spec(vt), kv_spec(kt_ctx), kv_spec(vt_ctx),
                  pl.BlockSpec(memory_space=pltpu.SMEM)],
        out_specs=pl.BlockSpec((LAT_LEN, 128), lambda b, j: (b, j)),
        compiler_params=_params(("arbitrary", "arbitrary")), name="swa_latent",
    )(q, kt, vt, kt_ctx, vt_ctx, sink)


def _rope_tables():
    t = np.arange(LAT_LEN)
    pos = np.stack([t // GRID_W, t % GRID_W], axis=-1).astype(np.float32)
    freqs = np.float32(ROPE_THETA) ** (-np.arange(ROPE_FREQS, dtype=np.float32) / np.float32(ROPE_FREQS))
    ang = (pos[:, :, None] * freqs).reshape(LAT_LEN, 2 * ROPE_FREQS).astype(np.float32)
    cos, sin = np.cos(ang), np.sin(ang)
    cos_t = np.tile(np.concatenate([cos, cos], axis=-1), (1, 2))
    sin_t = np.tile(np.concatenate([-sin, sin], axis=-1), (1, 2))
    return jnp.asarray(cos_t, F32), jnp.asarray(sin_t, F32)


def kernel(x_prompt, x_sample, cache_l0_attn_k, cache_l0_attn_v, state_l0_mlstm_C, state_l0_mlstm_n, state_l0_mlstm_m, cache_l1_na_k, cache_l1_na_v, cache_l1_swa_k, cache_l1_swa_v, c, c_ctx, norm_final, ada_w_l0, ada_b_l0, norm_l0, ffn1_in_l0, ffn1_out_l0, ffn2_in_l0, ffn2_out_l0, mix_in_l0, mix_out_l0, qk_norm_l0, gate_bias_l0, head_norm_l0, ada_w_l1, ada_b_l1, norm_l1, ffn1_in_l1, ffn1_out_l1, ffn2_in_l1, ffn2_out_l1, mix_in_l1, mix_out_l1, rpb_l1, sink_l1):
    nb, nl = x_prompt.shape[0], x_sample.shape[0]
    bf = lambda a: a.astype(BF16)

    cond8 = jnp.zeros((8, D_MODEL), F32).at[0].set(c_ctx).at[1:1 + nl].set(c)
    mods0 = _adaln(cond8, ada_w_l0, ada_b_l0).reshape(8 * ADA_CHUNKS, 1, D_MODEL)
    mods1 = _adaln(cond8, ada_w_l1, ada_b_l1).reshape(8 * ADA_CHUNKS, 1, D_MODEL)

    g0 = 2304
    gcols = lambda j: mix_in_l0[:, g0 + 8 * j:g0 + 8 * (j + 1)]
    gpad = jnp.zeros((D_MODEL, 128 - N_DIRHEAD), F32)
    w_gi, w_gf = jnp.concatenate([gcols(0), gcols(2)], axis=1), jnp.concatenate([gcols(1), gcols(3)], axis=1)
    w_even, w_ob = bf(mix_in_l0[:, :g0]), bf(mix_in_l0[:, g0 + 32:])
    w_g = bf(jnp.concatenate([w_gi, gpad, w_gf, gpad], axis=1))
    gb4 = gate_bias_l0.reshape(4, N_HEADS)
    b_gi, b_gf = jnp.concatenate([gb4[0], gb4[2]]), jnp.concatenate([gb4[1], gb4[3]])
    gbi = jnp.zeros((1, 128), F32).at[0, :N_DIRHEAD].set(b_gi)
    gbf = jnp.zeros((1, 128), F32).at[0, :N_DIRHEAD].set(b_gf)
    gbt = jnp.concatenate([b_gi, b_gf]).reshape(2 * N_DIRHEAD, 1)
    qg = jnp.tile(qk_norm_l0[0], N_HEADS).reshape(1, 512)
    kg = jnp.tile(qk_norm_l0[1], 2).reshape(1, 128)
    head_g = head_norm_l0.reshape(1, 512)
    grp = np.arange(256) // DH
    bd = jnp.asarray((grp[:, None] == grp[None, :]).astype(np.float32) / DH, dtype=BF16)
    rope_tabs = _rope_tables()
    w_odd = bf(mix_in_l1)
    rpb_flat = rpb_l1.reshape(-1)
    to_t = lambda a: a.reshape(a.shape[0], a.shape[1], -1).transpose(0, 2, 1)
    from_t = lambda a: a.reshape(a.shape[0], -1, DH, a.shape[2]).transpose(0, 3, 1, 2)

    xp = x_prompt.reshape(nb * CTX_LEN, D_MODEL)
    xs = x_sample.reshape(nl * LAT_LEN, D_MODEL)
    t_ctx, t_lat = xp.shape[0], xs.shape[0]
    streams = ((False, t_ctx, 0, nb, CTX_LEN), (True, t_lat, t_ctx // TOKEN_TILE, nl, LAT_LEN))
    ffn = functools.partial(_ffn, t_ctx=t_ctx, t_lat=t_lat)

    x = ffn((xp, xs), mods0, 0, norm_l0[0], ffn1_in_l0, ffn1_out_l0)
    ya, yb = {}, {}
    for latent, t, blk0, n_batch, seq_len in streams:
        qa, kat, vat, qb, vb, ob, gi, gf, gt, kbt = _proj_even(
            x, t, blk0, mods0, latent, norm_l0[1], w_even, w_ob, w_g, gbi, gbf, gbt, qg, kg, bd, rope_tabs)
        if latent:
            extra = (to_t(cache_l0_attn_k), to_t(cache_l0_attn_v))
            m0 = state_l0_mlstm_m.reshape(nl, N_DIRHEAD)
            m0_lanes = jnp.zeros((nl, 8, 128), F32).at[:, 0, :N_DIRHEAD].set(m0)
            m0_rows = jnp.broadcast_to(m0[:, :, None], (nl, N_DIRHEAD, 128))
            n0_cols = jnp.broadcast_to(state_l0_mlstm_n[..., None], (nl, 2, N_HEADS, DH, 128))
            state = (state_l0_mlstm_C, n0_cols, m0_lanes, m0_rows)
        else:
            extra, state = None, None
            k0t, v0t = kat, vat
        ya[latent] = _attention(qa, kat, vat, n_batch=n_batch, q_len=seq_len, q_tile=256, group=4, extra=extra)
        ml = _mlstm(qb, kbt, vb, ob, gi, gf, gt, head_g, bd, n_batch=n_batch, seq_len=seq_len, state=state)
        yb[latent] = ml[0]
        if not latent:
            c0, n0, m0_pad = ml[1:]
    x = ffn(x, mods0, 6, norm_l0[2], ffn2_in_l0, ffn2_out_l0, mix=(ya[False], yb[False], ya[True], yb[True], mix_out_l0))

    x = ffn(x, mods1, 0, norm_l1[0], ffn1_in_l1, ffn1_out_l1)
    yc, yd = {}, {}
    for latent, t, blk0, n_batch, seq_len in streams:
        qc, kc, vc, qd, kd, vd = _proj_odd(x, t, blk0, mods1, latent, norm_l1[1], w_odd, rope_tabs)
        if latent:
            yc[latent] = _na_latent(qc, kc, vc, cache_l1_na_k.reshape(nl * CTX_LEN, 512),
                                    cache_l1_na_v.reshape(nl * CTX_LEN, 512), rpb_flat, nl)
            yd[latent] = _swa_latent(qd, kd, vd, to_t(cache_l1_swa_k), to_t(cache_l1_swa_v), sink_l1, nl, group=4)
        else:
            yc[latent] = _attention(qc, kc, vc, n_batch=n_batch, q_len=seq_len, q_tile=256, group=1)
            yd[latent] = _attention(qd, kd, vd, n_batch=n_batch, q_len=seq_len, q_tile=256, group=4, sink=sink_l1)
            kc1t, vc1t, kd1t, vd1t = kc, vc, kd, vd
    y_prompt, y_sample = ffn(x, mods1, 6, norm_l1[2], ffn2_in_l1, ffn2_out_l1,
                             mix=(yc[False], yd[False], yc[True], yd[True], mix_out_l1), final_g=norm_final)

    return (y_prompt.reshape(nb, CTX_LEN, D_MODEL), y_sample.reshape(nl, LAT_LEN, D_MODEL),
            from_t(k0t), from_t(v0t),
            c0, n0, m0_pad[:, :, 0].reshape(nb, 2, N_HEADS),
            from_t(kc1t), from_t(vc1t),
            from_t(kd1t), from_t(vd1t))
```

```python
import functools

import jax
import jax.numpy as jnp
import numpy as np
from jax import lax
from jax.experimental import pallas as pl
from jax.experimental.pallas import tpu as pltpu

F32 = jnp.float32
BF16 = jnp.bfloat16

D_MODEL = 1024
DH = 64
D_FF = 2816
ADA_CHUNKS = 9
GRID_W = 64
LAT_LEN = 1024
CTX_LEN = 256
N_HEADS = 8
NA_KH = 8
NA_KW = 16
SWA_WIN = 128
ROPE_THETA = 10000.0
ROPE_FREQS = DH // 4
ATTN_SCALE = DH ** -0.5
NEG_INF = -1e30
EPS = 1e-6

TOKEN_TILE = 512
MLSTM_T = 256
FFN_CHUNK = 512
VMEM_LIMIT = 56 * 1024 * 1024
FFN_VMEM_LIMIT = 60 * 1024 * 1024
W_TILE = (256, 512)
W_SLOTS = 6


def _params(sem, vmem=VMEM_LIMIT):
    return pltpu.CompilerParams(dimension_semantics=sem, vmem_limit_bytes=vmem)


def _dot(a, b):
    return jnp.dot(a.astype(BF16), b.astype(BF16), preferred_element_type=F32)


def _dot_nt(a, b):
    return lax.dot_general(a.astype(BF16), b.astype(BF16), (((1,), (1,)), ((), ())),
                           preferred_element_type=F32)


def _split3(x):
    hi = x.astype(BF16)
    r1 = x - hi.astype(F32)
    mid = r1.astype(BF16)
    lo = (r1 - mid.astype(F32)).astype(BF16)
    return hi, mid, lo


def _dot_exact_rhs(a_bf16, x):
    hi, mid, lo = _split3(x)
    f = lambda p: jnp.dot(a_bf16, p, preferred_element_type=F32)
    return f(hi) + f(mid) + f(lo)


def _dot_exact_lhs(x, b_bf16):
    hi, mid, lo = _split3(x)
    f = lambda p: jnp.dot(p, b_bf16, preferred_element_type=F32)
    return f(hi) + f(mid) + f(lo)


def _silu(x):
    return x * jax.nn.sigmoid(x)


def _log_sigmoid(x):
    return jnp.minimum(x, 0.0) - jnp.log1p(jnp.exp(-jnp.abs(x)))


def _rms(x, g):
    return x * lax.rsqrt(jnp.mean(x * x, axis=-1, keepdims=True) + EPS) * g


def _head_mean_sq(x, bd_ref):
    n = x.shape[-1]
    sq = x * x
    hi = sq.astype(BF16)
    lo = (sq - hi.astype(F32)).astype(BF16)
    w = min(n, bd_ref.shape[0])
    bd = bd_ref[:w, :w]
    if 2 * w <= bd_ref.shape[0]:
        return jnp.dot(jnp.concatenate([hi, lo], axis=1), jnp.concatenate([bd, bd], axis=0),
                       preferred_element_type=F32)
    f = lambda p: jnp.dot(p, bd, preferred_element_type=F32)
    parts = [f(hi[:, c:c + w]) + f(lo[:, c:c + w]) for c in range(0, n, w)]
    return parts[0] if len(parts) == 1 else jnp.concatenate(parts, axis=1)


def _rope(x, cos, sin_signed):
    n = x.shape[-1]
    lane = lax.broadcasted_iota(jnp.int32, x.shape, 1)
    first_half = (lane % DH) < (DH // 2)
    partner = jnp.where(first_half, pltpu.roll(x, n - DH // 2, 1), pltpu.roll(x, DH // 2, 1))
    spread = lambda tab: tab if n == 128 else jnp.concatenate([tab] * (n // 128), axis=1)
    return x * spread(cos) + partner * spread(sin_signed)


def _adaln_kernel(c_ref, w_ref, b_ref, o_ref):
    s = _silu(c_ref[...])
    o_ref[...] = _dot(s, w_ref[...]) + b_ref[...]


def _adaln(cond8, w, b):
    n = w.shape[1]
    tn = 1536
    return pl.pallas_call(
        _adaln_kernel,
        out_shape=jax.ShapeDtypeStruct((8, n), F32),
        grid=(n // tn,),
        in_specs=[pl.BlockSpec((8, D_MODEL), lambda j: (0, 0)),
                  pl.BlockSpec((D_MODEL, tn), lambda j: (0, j)),
                  pl.BlockSpec((1, tn), lambda j: (0, j))],
        out_specs=pl.BlockSpec((8, tn), lambda j: (0, j)),
        compiler_params=_params(("arbitrary",)),
        name="adaln",
    )(cond8, w, b.reshape(1, n))


def _const_spec(shape):
    nd = len(shape)
    return pl.BlockSpec(shape, lambda i, _n=nd: (0,) * _n, pipeline_mode=pl.Buffered(1))


def _mod_spec(chunk, rowfn):
    return pl.BlockSpec((1, 1, D_MODEL), lambda i: (rowfn(i) * ADA_CHUNKS + chunk, 0, 0))


def _rowfn(latent):
    if latent:
        return lambda i: 1 + (i * TOKEN_TILE) // LAT_LEN
    return lambda i: 0


def _tok_spec(width, blk0=0):
    return pl.BlockSpec((TOKEN_TILE, width), lambda i: (i + blk0, 0))


def _seq_len(latent):
    return LAT_LEN if latent else CTX_LEN


def _transposed_out(t, latent, rows):
    seq = _seq_len(latent)
    shape = jax.ShapeDtypeStruct((t // seq, rows, seq), F32)
    if seq <= TOKEN_TILE:
        per = TOKEN_TILE // seq
        return shape, pl.BlockSpec((per, rows, seq), lambda i: (i, 0, 0))
    per = seq // TOKEN_TILE
    return shape, pl.BlockSpec((1, rows, TOKEN_TILE), lambda i: (i // per, 0, i % per))


def _store_transposed(ref, x):
    n, _, w = ref.shape
    for j in range(n):
        ref[j] = x[j * w:(j + 1) * w].T


def _dot_colsT(w_cols, h):
    return lax.dot_general(w_cols, h, (((0,), (1,)), ((), ())), preferred_element_type=F32)


def _store_proj_cols_transposed(ref, w_cols, h):
    n, _, w = ref.shape
    for j in range(n):
        ref[j] = _dot_colsT(w_cols, h[j * w:(j + 1) * w])


def _modnorm(x, ng_ref, sh_ref, sc_ref):
    return _rms(x, ng_ref[...] * (1.0 + sc_ref[0])) + sh_ref[0]


def _load_weights_bf16(pairs, stage, sem):
    tr, tc = W_TILE
    tiles = [(src, dst, r, c) for src, dst in pairs
             for r in range(0, src.shape[0], tr) for c in range(0, src.shape[1], tc)]

    def copy(k):
        src, _, r, c = tiles[k]
        return pltpu.make_async_copy(src.at[r:r + tr, c:c + tc], stage.at[k % W_SLOTS], sem.at[k % W_SLOTS])

    for k in range(min(W_SLOTS, len(tiles))):
        copy(k).start()
    for k, (_, dst, r, c) in enumerate(tiles):
        copy(k).wait()
        dst[r:r + tr, c:c + tc] = stage[k % W_SLOTS].astype(BF16)
        if k + W_SLOTS < len(tiles):
            copy(k + W_SLOTS).start()


def _ffn_kernel(*refs, n_ctx, first, has_mix, last):
    it = iter(refs)
    xs = (next(it), next(it)) if first else (next(it),)
    if has_mix:
        yac_ref, ybc_ref, yal_ref, ybl_ref, wo_hbm, g2_ref = (next(it) for _ in range(6))
    ng_ref, sh_ref, sc_ref, g_ref, win_hbm, wout_hbm = (next(it) for _ in range(6))
    if last:
        nf_ref = next(it)
    outs = (next(it), next(it)) if last else (next(it),)
    win_s, wout_s = next(it), next(it)
    if has_mix:
        wo_s = next(it)
    stage, sem = next(it), next(it)

    step = pl.program_id(0)
    is_lat = step >= n_ctx

    @pl.when(step == 0)
    def _():
        pairs = [(win_hbm, win_s), (wout_hbm, wout_s)]
        if has_mix:
            pairs.insert(0, (wo_hbm, wo_s))
        _load_weights_bf16(pairs, stage, sem)

    pick = lambda c_ref, l_ref: jnp.where(is_lat, l_ref[...], c_ref[...])
    x = pick(*xs) if first else xs[0][...]
    if has_mix:
        half = wo_s.shape[0] // 2
        mo = _dot(pick(yac_ref, yal_ref), wo_s[:half, :]) + _dot(pick(ybc_ref, ybl_ref), wo_s[half:, :])
        x = x + g2_ref[0] * mo
    h = _modnorm(x, ng_ref, sh_ref, sc_ref).astype(BF16)
    acc = None
    for lo in range(0, D_FF, FFN_CHUNK):
        hi = min(lo + FFN_CHUNK, D_FF)
        g = _dot(h, win_s[:, lo:hi])
        u = _dot(h, win_s[:, D_FF + lo:D_FF + hi])
        part = _dot(_silu(g) * u, wout_s[lo:hi, :])
        acc = part if acc is None else acc + part
    y = x + (0.5 * g_ref[0]) * acc
    if last:
        y = _rms(y, nf_ref[...])
        oc_ref, ol_ref = outs

        @pl.when(jnp.logical_not(is_lat))
        def _():
            oc_ref[...] = y
            ol_ref[...] = jnp.zeros_like(ol_ref)

        @pl.when(is_lat)
        def _():
            ol_ref[...] = y
    else:
        outs[0][...] = y


def _ffn(x, mods, mod_base, ng, w_in, w_out, *, t_ctx, t_lat, mix=None, final_g=None):
    tile = TOKEN_TILE
    n_ctx, n_lat = t_ctx // tile, t_lat // tile
    first, last = isinstance(x, tuple), final_g is not None
    ctx_map = lambda i: (jnp.minimum(i, n_ctx - 1), 0)
    lat_map = lambda i: (jnp.maximum(i - n_ctx, 0), 0)
    rowfn = lambda i: jnp.where(i < n_ctx, 0, 1 + ((i - n_ctx) * tile) // LAT_LEN)
    two = lambda wd: [pl.BlockSpec((tile, wd), ctx_map), pl.BlockSpec((tile, wd), lat_map)]
    one = pl.BlockSpec((tile, D_MODEL), lambda i: (i, 0))
    hbm = pl.BlockSpec(memory_space=pl.ANY)
    args = list(x) if first else [x]
    specs = two(D_MODEL) if first else [one]
    scratch = [pltpu.VMEM(w_in.shape, BF16), pltpu.VMEM(w_out.shape, BF16)]
    if mix is not None:
        yac, ybc, yal, ybl, wo = mix
        args += [yac, ybc, yal, ybl, wo, mods]
        mw = yac.shape[1]
        specs += [pl.BlockSpec((tile, mw), ctx_map)] * 2 + [pl.BlockSpec((tile, mw), lat_map)] * 2
        specs += [hbm, _mod_spec(mod_base - 1, rowfn)]
        scratch.append(pltpu.VMEM(wo.shape, BF16))
    args += [ng.reshape(1, D_MODEL), mods, mods, mods, w_in, w_out]
    specs += [_const_spec((1, D_MODEL)), _mod_spec(mod_base, rowfn), _mod_spec(mod_base + 1, rowfn),
              _mod_spec(mod_base + 2, rowfn), hbm, hbm]
    if last:
        args.append(final_g.reshape(1, D_MODEL))
        specs.append(_const_spec((1, D_MODEL)))
        out_shape = [jax.ShapeDtypeStruct((t_ctx, D_MODEL), F32), jax.ShapeDtypeStruct((t_lat, D_MODEL), F32)]
        out_specs = two(D_MODEL)
    else:
        out_shape = jax.ShapeDtypeStruct((t_ctx + t_lat, D_MODEL), F32)
        out_specs = one
    scratch += [pltpu.VMEM((W_SLOTS,) + W_TILE, F32), pltpu.SemaphoreType.DMA((W_SLOTS,))]
    return pl.pallas_call(
        functools.partial(_ffn_kernel, n_ctx=n_ctx, first=first, has_mix=mix is not None, last=last),
        out_shape=out_shape,
        grid=(n_ctx + n_lat,),
        in_specs=specs,
        out_specs=out_specs,
        scratch_shapes=scratch,
        compiler_params=_params(("arbitrary",), FFN_VMEM_LIMIT),
        name="ffn",
    )(*args)


_EV_QA, _EV_KA, _EV_VA, _EV_QB, _EV_KB, _EV_VB, _EV_OB = 0, 512, 640, 768, 1280, 1792, 2304
N_DIRHEAD = 2 * N_HEADS


def _proj_even_kernel(*refs, rope):
    it = iter(refs)
    x_ref, ng_ref, sh_ref, sc_ref, w_ref, wob_ref, wg_ref, gbi_ref, gbf_ref, gbt_ref, qg_ref, kg_ref, bd_ref = (
        next(it) for _ in range(13))
    if rope:
        cos_ref, sin_ref = next(it), next(it)
    qa_ref, kat_ref, vat_ref, qb_ref, vb_ref, ob_ref, gi_ref, gf_ref, gt_ref, kbt_ref = (next(it) for _ in range(10))

    h = _modnorm(x_ref[...], ng_ref, sh_ref, sc_ref).astype(BF16)
    proj = lambda lo, hi: _dot(h, w_ref[:, lo:hi])

    qa = proj(_EV_QA, _EV_KA)
    qa = qa * lax.rsqrt(_head_mean_sq(qa, bd_ref) + EPS) * qg_ref[...]
    ka = proj(_EV_KA, _EV_VA)
    ka = ka * lax.rsqrt(_head_mean_sq(ka, bd_ref) + EPS) * kg_ref[...]
    if rope:
        qa = _rope(qa, cos_ref[...], sin_ref[...])
        ka = _rope(ka, cos_ref[...], sin_ref[...])
    qa_ref[...] = qa * ATTN_SCALE
    _store_transposed(kat_ref, ka)
    _store_proj_cols_transposed(vat_ref, w_ref[:, _EV_VA:_EV_QB], h)
    qb_ref[...] = proj(_EV_QB, _EV_KB)
    vb_ref[...] = proj(_EV_VB, _EV_OB)
    ob_ref[...] = _dot(h, wob_ref[...])
    gates = _dot(h, wg_ref[...])
    gi_ref[...] = gates[:, :128] + gbi_ref[...]
    gf_ref[...] = gates[:, 128:] + gbf_ref[...]
    for j in range(TOKEN_TILE // MLSTM_T):
        ht = h[j * MLSTM_T:(j + 1) * MLSTM_T]
        gates_t = [_dot_colsT(wg_ref[:, c0:c0 + N_DIRHEAD], ht) for c0 in (0, 128)]
        gt_ref[j] = jnp.concatenate(gates_t, axis=0) + gbt_ref[...]
        kbt_ref[j] = _dot_colsT(w_ref[:, _EV_KB:_EV_VB], ht) * ATTN_SCALE


def _proj_even(x, t, blk0, mods, latent, ng, w, w_ob, w_g, gbi, gbf, gbt, qg, kg, bd, rope_tabs):
    rowfn = _rowfn(latent)
    args = [x, ng.reshape(1, D_MODEL), mods, mods, w, w_ob, w_g, gbi, gbf, gbt, qg, kg, bd]
    specs = [_tok_spec(D_MODEL, blk0), _const_spec((1, D_MODEL)), _mod_spec(3, rowfn), _mod_spec(4, rowfn)]
    specs += [_const_spec(a.shape) for a in args[4:]]
    if latent:
        nblk = LAT_LEN // TOKEN_TILE
        args += list(rope_tabs)
        specs += [pl.BlockSpec((TOKEN_TILE, 128), lambda i: (i % nblk, 0))] * 2
    widths = (512, 512, 512, 512, 128, 128)
    out_shape = [jax.ShapeDtypeStruct((t, wd), F32) for wd in widths]
    out_specs = [_tok_spec(wd) for wd in widths]
    for pos in (1, 2):
        shape, spec = _transposed_out(t, latent, 128)
        out_shape.insert(pos, shape)
        out_specs.insert(pos, spec)
    nj = TOKEN_TILE // MLSTM_T
    for rows in (2 * N_DIRHEAD, 512):
        out_shape.append(jax.ShapeDtypeStruct((t // MLSTM_T, rows, MLSTM_T), F32))
        out_specs.append(pl.BlockSpec((nj, rows, MLSTM_T), lambda i: (i, 0, 0)))
    return pl.pallas_call(
        functools.partial(_proj_even_kernel, rope=latent),
        out_shape=out_shape, grid=(t // TOKEN_TILE,), in_specs=specs, out_specs=out_specs,
        compiler_params=_params(("arbitrary",)), name="proj_even",
    )(*args)


def _proj_odd_kernel(*refs, rope):
    it = iter(refs)
    x_ref, ng_ref, sh_ref, sc_ref, w_ref = (next(it) for _ in range(5))
    if rope:
        cos_ref, sin_ref = next(it), next(it)
    qc_ref, kc_ref, vc_ref, qd_ref, kd_ref, vd_ref = (next(it) for _ in range(6))

    h = _modnorm(x_ref[...], ng_ref, sh_ref, sc_ref).astype(BF16)
    proj = lambda lo, hi: _dot(h, w_ref[:, lo:hi])
    qc_ref[...] = proj(0, 512) * ATTN_SCALE
    if rope:
        kc_ref[...] = proj(512, 1024)
        vc_ref[...] = proj(1024, 1536)
    else:
        _store_proj_cols_transposed(kc_ref, w_ref[:, 512:1024], h)
        _store_proj_cols_transposed(vc_ref, w_ref[:, 1024:1536], h)
    qd = proj(1536, 2048)
    if rope:
        qd = _rope(qd, cos_ref[...], sin_ref[...])
        _store_transposed(kd_ref, _rope(proj(2048, 2176), cos_ref[...], sin_ref[...]))
    else:
        _store_proj_cols_transposed(kd_ref, w_ref[:, 2048:2176], h)
    _store_proj_cols_transposed(vd_ref, w_ref[:, 2176:2304], h)
    qd_ref[...] = qd * ATTN_SCALE


def _proj_odd(x, t, blk0, mods, latent, ng, w, rope_tabs):
    rowfn = _rowfn(latent)
    args = [x, ng.reshape(1, D_MODEL), mods, mods, w]
    specs = [_tok_spec(D_MODEL, blk0), _const_spec((1, D_MODEL)), _mod_spec(3, rowfn), _mod_spec(4, rowfn),
             _const_spec(w.shape)]
    if latent:
        nblk = LAT_LEN // TOKEN_TILE
        args += list(rope_tabs)
        specs += [pl.BlockSpec((TOKEN_TILE, 128), lambda i: (i % nblk, 0))] * 2
    out_shape = [jax.ShapeDtypeStruct((t, 512), F32)] * 4
    out_specs = [_tok_spec(512)] * 4
    if not latent:
        out_shape[1], out_specs[1] = _transposed_out(t, latent, 512)
        out_shape[2], out_specs[2] = out_shape[1], out_specs[1]
    for _ in range(2):
        shape, spec = _transposed_out(t, latent, 128)
        out_shape.append(shape)
        out_specs.append(spec)
    return pl.pallas_call(
        functools.partial(_proj_odd_kernel, rope=latent),
        out_shape=out_shape,
        grid=(t // TOKEN_TILE,), in_specs=specs, out_specs=out_specs,
        compiler_params=_params(("arbitrary",)), name="proj_odd",
    )(*args)


def _attn_kernel(*refs, group, has_extra, has_sink, spb):
    it = iter(refs)
    q_ref, k_ref, v_ref = next(it), next(it), next(it)
    if has_extra:
        ke_ref, ve_ref = next(it), next(it)
    if has_sink:
        sink_ref = next(it)
    o_ref = next(it)
    tq = q_ref.shape[0] // spb
    first = lax.broadcasted_iota(jnp.int32, (tq, 128), 1) < DH

    def kv_operands(kt_ref, vt_ref, smp, kv):
        kt = kt_ref[smp, kv * DH:(kv + 1) * DH, :].astype(BF16)
        vt = vt_ref[smp, kv * DH:(kv + 1) * DH, :].astype(BF16)
        z = jnp.zeros_like(kt)
        ones = jnp.ones((128, kt.shape[1]), BF16)
        return ((jnp.concatenate([kt, z], axis=0), jnp.concatenate([z, kt], axis=0)),
                (jnp.concatenate([vt, z, ones], axis=0), jnp.concatenate([z, vt, ones], axis=0)))

    for smp in range(spb):
        qrows = slice(smp * tq, (smp + 1) * tq)
        cache = {}
        for j in range(N_HEADS // 2):
            ps = slice(128 * j, 128 * (j + 1))
            qb = q_ref[qrows, ps].astype(BF16)
            res, sink_den = [], []
            for half in range(2):
                kv = (2 * j + half) // group
                if kv not in cache:
                    cache[kv] = (kv_operands(k_ref, v_ref, smp, kv),
                                 kv_operands(ke_ref, ve_ref, smp, kv) if has_extra else None)
                (k_ops, v_ops), extra = cache[kv]
                s = jnp.dot(qb, k_ops[half], preferred_element_type=F32)
                m = jnp.max(s, axis=-1, keepdims=True)
                if has_extra:
                    se = jnp.dot(qb, extra[0][half], preferred_element_type=F32)
                    m = jnp.maximum(m, jnp.max(se, axis=-1, keepdims=True))
                if has_sink:
                    m = jnp.maximum(m, sink_ref[2 * j + half])
                    sink_den.append(jnp.exp(sink_ref[2 * j + half] - m))
                r = _dot_nt(jnp.exp(s - m), v_ops[half])
                if has_extra:
                    r = r + _dot_nt(jnp.exp(se - m), extra[1][half])
                res.append(r)
            num = jnp.where(first, res[0][:, :128], res[1][:, :128])
            den = jnp.where(first, res[0][:, 128:], res[1][:, 128:])
            if has_sink:
                den = den + jnp.where(first, sink_den[0], sink_den[1])
            o_ref[qrows, ps] = num / den


def _attention(q, k, v, *, n_batch, q_len, q_tile, group, extra=None, sink=None):
    nq = q_len // q_tile
    spb = next(n for n in (8, 4, 2, 1) if n_batch % n == 0) if nq == 1 else 1
    kv_spec = lambda a: pl.BlockSpec((spb,) + a.shape[1:], lambda b, j: (b, 0, 0))
    args = [q, k, v]
    specs = [pl.BlockSpec((spb * q_tile, 512), lambda b, j: (b * nq + j, 0)), kv_spec(k), kv_spec(v)]
    if extra is not None:
        args += list(extra)
        specs += [kv_spec(extra[0]), kv_spec(extra[1])]
    if sink is not None:
        args.append(sink)
        specs.append(pl.BlockSpec(memory_space=pltpu.SMEM))
    return pl.pallas_call(
        functools.partial(_attn_kernel, group=group, has_extra=extra is not None, has_sink=sink is not None, spb=spb),
        out_shape=jax.ShapeDtypeStruct(q.shape, F32),
        grid=(n_batch // spb, nq), in_specs=specs,
        out_specs=pl.BlockSpec((spb * q_tile, 512), lambda b, j: (b * nq + j, 0)),
        compiler_params=_params(("arbitrary", "arbitrary")), name="attn",
    )(*args)


LOG2E = 1.4426950408889634


def _cummax_time(x, reverse):
    n = x.shape[0]
    row = lax.broadcasted_iota(jnp.int32, x.shape, 0)
    k = 1
    while k < n:
        if reverse:
            shifted = jnp.where(row < n - k, pltpu.roll(x, n - k, 0), -jnp.inf)
        else:
            shifted = jnp.where(row >= k, pltpu.roll(x, k, 0), -jnp.inf)
        x = jnp.maximum(x, shifted)
        k *= 2
    return x


def _mlstm_kernel(*refs, n_chunks, carry, spb):
    it = iter(refs)
    q_ref, kt_ref, v_ref, ob_ref, gi_ref, gf_ref, gt_ref, hg_ref, bd_ref = (next(it) for _ in range(9))
    if carry:
        c0_ref, n0_ref, m0r_ref, m0c_ref = (next(it) for _ in range(4))
    y_ref = next(it)
    if not carry:
        c_ref, n_ref, m_ref = next(it), next(it), next(it)
    hf_s, hb_s = next(it), next(it)
    if carry:
        st_s, mrow_s, mcol_s = next(it), next(it), next(it)
    T = MLSTM_T
    step = pl.program_id(1)

    if carry:
        @pl.when(step == 0)
        def _():
            z = jnp.zeros((DH, DH), F32)
            for d in range(2):
                for j in range(N_HEADS // 2):
                    ca, cb = c0_ref[0, d, 2 * j], c0_ref[0, d, 2 * j + 1]
                    na, nb = n0_ref[0, d, 2 * j][:, :DH], n0_ref[0, d, 2 * j + 1][:, :DH]
                    st_s[d, j, 0:DH, :] = jnp.concatenate([ca, z, na, z], axis=1)
                    st_s[d, j, DH:2 * DH, :] = jnp.concatenate([z, cb, z, nb], axis=1)
            mrow_s[...] = m0r_ref[0]
            mcol_s[...] = m0c_ref[0]

    row = lax.broadcasted_iota(jnp.int32, (T, T), 0)
    col = lax.broadcasted_iota(jnp.int32, (T, T), 1)
    lower = row >= col
    upper = row <= col
    lower_m = jnp.where(lower, 1.0, 0.0).astype(BF16)
    upper_m = jnp.where(upper, 1.0, 0.0).astype(BF16)
    lane = lax.broadcasted_iota(jnp.int32, (T, 128), 1)
    first = lane < DH
    lane_c = lax.broadcasted_iota(jnp.int32, (DH, 256), 1) % 128
    zeros_kt = jnp.zeros((DH, T), BF16)
    ones_v = jnp.ones((T, 128), BF16)

    for smp, d in ((s_, d_) for s_ in range(spb) for d_ in range(2)):
        chunk = (step if d == 0 else n_chunks - 1 - step) if carry else smp
        rows = pl.ds(pl.multiple_of(chunk * T, T), T)
        hd0 = N_HEADS * d
        mask = lower if d == 0 else upper

        if carry or d == 0:
            gi = gi_ref[rows, :]
            lf = _log_sigmoid(gf_ref[rows, :])
            prefix = _dot_exact_rhs(lower_m, lf)
        b_cols = prefix if d == 0 else prefix[T - 1:T, :] - prefix + lf
        a_cols = gi - b_cols
        dmax = b_cols + _cummax_time(a_cols, reverse=d == 1)
        if carry:
            inter = b_cols + mrow_s[0:1, :]
            mt = jnp.maximum(inter, dmax)
            w_inter = jnp.exp(inter - mt)
        else:
            mt = jnp.maximum(b_cols, dmax)
        c2 = (b_cols - mt) * LOG2E
        einv = jnp.exp(-mt)

        gt = gt_ref[chunk]
        gi_t = gt[hd0:hd0 + N_HEADS, :]
        gf_t = gt[N_DIRHEAD + hd0:N_DIRHEAD + hd0 + N_HEADS, :]
        lf_t = _log_sigmoid(gf_t)
        b_rows = _dot_exact_lhs(lf_t, upper_m)
        if d == 1:
            b_rows = b_rows[:, T - 1:T] - b_rows + lf_t
        a_rows = gi_t - b_rows
        a2 = a_rows * LOG2E
        b_last = b_rows[:, T - 1:T] if d == 0 else b_rows[:, 0:1]
        dec_rows = b_last + a_rows
        m_new = jnp.max(dec_rows, axis=-1, keepdims=True)
        if carry:
            m_prev = mcol_s[hd0:hd0 + N_HEADS, 0:1]
            m_new = jnp.maximum(m_new, b_last + m_prev)
            wc = jnp.exp(b_last + m_prev - m_new)
        else:
            m_new = jnp.maximum(m_new, b_last)
        ws_rows = jnp.exp(dec_rows - m_new)

        h_s = hf_s if d == 0 else hb_s
        for j in range(N_HEADS // 2):
            ps = slice(128 * j, 128 * (j + 1))
            q_pair = q_ref[rows, ps].astype(BF16)
            v_aug = jnp.concatenate([v_ref[rows, ps].astype(BF16), ones_v], axis=1)
            kt = (kt_ref[chunk, 128 * j:128 * j + DH, :], kt_ref[chunk, 128 * j + DH:128 * (j + 1), :])
            lhs = []
            for half in range(2):
                hd = 2 * j + half
                kt0 = jnp.concatenate([kt[0].astype(BF16), zeros_kt] if half == 0 else [zeros_kt, kt[1].astype(BF16)], axis=0)
                qk = jnp.dot(q_pair, kt0, preferred_element_type=F32)
                e = c2[:, hd0 + hd:hd0 + hd + 1] + a2[hd:hd + 1, :]
                lhs.append((jnp.exp2(jnp.where(mask, e, -jnp.inf)) * qk).astype(BF16))
            for half in range(2):
                lhs.append((kt[half] * ws_rows[2 * j + half:2 * j + half + 1, :]).astype(BF16))
            both = jnp.dot(jnp.concatenate(lhs, axis=0), v_aug, preferred_element_type=F32)
            res = (both[0:T], both[T:2 * T])
            cu = (both[2 * T:2 * T + DH], both[2 * T + DH:2 * T + 2 * DH])
            ra, rb = hd0 + 2 * j, hd0 + 2 * j + 1
            num = jnp.where(first, res[0][:, :128], res[1][:, :128])
            den = jnp.where(first, res[0][:, 128:], res[1][:, 128:])
            if carry:
                old = st_s[d, j]
                qs = _dot(q_pair, old)
                wi = jnp.where(first, w_inter[:, ra:ra + 1], w_inter[:, rb:rb + 1])
                num = num + wi * qs[:, :128]
                den = den + wi * qs[:, 128:]
            floor = jnp.where(first, einv[:, ra:ra + 1], einv[:, rb:rb + 1])
            h_s[rows, ps] = num / jnp.maximum(jnp.abs(den), floor)
            if carry:
                st_s[d, j, 0:DH, :] = jnp.where(lane_c < DH, cu[0], 0.0) + wc[2 * j:2 * j + 1, :] * old[0:DH]
                st_s[d, j, DH:2 * DH, :] = jnp.where(lane_c >= DH, cu[1], 0.0) + wc[2 * j + 1:2 * j + 2, :] * old[DH:2 * DH]
            else:
                c_ref[smp, d, 2 * j] = cu[0][:, 0:DH]
                c_ref[smp, d, 2 * j + 1] = cu[1][:, DH:2 * DH]

        if carry:
            b_last_l = b_cols[T - 1:T, :] if d == 0 else b_cols[0:1, :]
            m_prev_l = mrow_s[0:1, :]
            m_new_l = jnp.maximum(jnp.max(b_last_l + a_cols, axis=0, keepdims=True), b_last_l + m_prev_l)
            lane_r = lax.broadcasted_iota(jnp.int32, (1, 128), 1)
            mine = (lane_r >= hd0) & (lane_r < hd0 + N_HEADS)
            mrow_s[0:1, :] = jnp.where(mine, m_new_l, m_prev_l)
            mcol_s[hd0:hd0 + N_HEADS, :] = jnp.broadcast_to(m_new, (N_HEADS, 128))
        else:
            for hd in range(N_HEADS):
                n_hd = _dot_nt(ws_rows, kt_ref[chunk, hd * DH:(hd + 1) * DH, :])
                n_ref[smp, d, hd:hd + 1, :] = n_hd[hd:hd + 1, :]
            m_ref[smp, hd0:hd0 + N_HEADS, :] = jnp.broadcast_to(m_new, (N_HEADS, 128))

    @pl.when(step == n_chunks - 1)
    def _():
        hsum = hf_s[...] + hb_s[...]
        y = hsum * lax.rsqrt(_head_mean_sq(hsum, bd_ref) + EPS) * hg_ref[...]
        y_ref[...] = jax.nn.sigmoid(ob_ref[...]) * y


def _mlstm(qb, kbt, vb, ob, gi, gf, gt, head_g, bd, *, n_batch, seq_len, state=None):
    n_chunks = seq_len // MLSTM_T
    carry = state is not None
    assert carry or n_chunks == 1
    spb = 1
    seq = lambda wd: pl.BlockSpec((spb * seq_len, wd), lambda b, c: (b, 0))
    per_chunk = lambda r: pl.BlockSpec((spb * n_chunks, r, MLSTM_T), lambda b, c: (b, 0, 0))
    args = [qb, kbt, vb, ob, gi, gf, gt, head_g, bd]
    specs = [seq(512), per_chunk(512), seq(512), seq(512), seq(128), seq(128), per_chunk(2 * N_DIRHEAD),
             pl.BlockSpec((1, 512), lambda b, c: (0, 0)), pl.BlockSpec(bd.shape, lambda b, c: (0, 0))]
    out_shape = [jax.ShapeDtypeStruct((n_batch * seq_len, 512), F32)]
    out_specs = [seq(512)]
    scratch = [pltpu.VMEM((spb * seq_len, 512), F32), pltpu.VMEM((spb * seq_len, 512), F32)]
    if carry:
        args += list(state)
        specs += [pl.BlockSpec((1, 2, N_HEADS, DH, DH), lambda b, c: (b, 0, 0, 0, 0)),
                  pl.BlockSpec((1, 2, N_HEADS, DH, 128), lambda b, c: (b, 0, 0, 0, 0)),
                  pl.BlockSpec((1, 8, 128), lambda b, c: (b, 0, 0)),
                  pl.BlockSpec((1, N_DIRHEAD, 128), lambda b, c: (b, 0, 0))]
        scratch += [pltpu.VMEM((2, N_HEADS // 2, 128, 256), F32), pltpu.VMEM((8, 128), F32),
                    pltpu.VMEM((N_DIRHEAD, 128), F32)]
    else:
        out_shape += [jax.ShapeDtypeStruct((n_batch, 2, N_HEADS, DH, DH), F32),
                      jax.ShapeDtypeStruct((n_batch, 2, N_HEADS, DH), F32),
                      jax.ShapeDtypeStruct((n_batch, N_DIRHEAD, 128), F32)]
        out_specs += [pl.BlockSpec((spb, 2, N_HEADS, DH, DH), lambda b, c: (b, 0, 0, 0, 0)),
                      pl.BlockSpec((spb, 2, N_HEADS, DH), lambda b, c: (b, 0, 0, 0)),
                      pl.BlockSpec((spb, N_DIRHEAD, 128), lambda b, c: (b, 0, 0))]
    return pl.pallas_call(
        functools.partial(_mlstm_kernel, n_chunks=n_chunks, carry=carry, spb=spb),
        out_shape=out_shape, grid=(n_batch // spb, n_chunks), in_specs=specs, out_specs=out_specs,
        scratch_shapes=scratch,
        compiler_params=_params(("arbitrary", "arbitrary")), name="mlstm",
    )(*args)


N_ROWS = LAT_LEN // GRID_W
N_DY = 2 * NA_KH - 1
N_DX = 2 * NA_KW - 1


NA_QROWS = 4
NA_KROWS = 12
NA_GROUPS = N_ROWS // NA_QROWS


def _na_kernel(q_ref, k_ref, v_ref, ke_ref, ve_ref, rpb_ref, o_ref, tile_s, slab_s):
    pair = pl.program_id(1)
    qn, kn = NA_QROWS * GRID_W, NA_KROWS * GRID_W

    @pl.when(pl.program_id(0) == 0)
    def _():
        qc = lax.broadcasted_iota(jnp.int32, (GRID_W, GRID_W), 0)
        kc = lax.broadcasted_iota(jnp.int32, (GRID_W, GRID_W), 1)
        start = jnp.clip(qc - NA_KW // 2, 0, GRID_W - NA_KW)
        in_win = (kc >= start) & (kc < start + NA_KW)
        dx = kc - qc + (NA_KW - 1)
        blocked = jnp.full((GRID_W, GRID_W), NEG_INF, F32)
        for half in range(2):
            head = 2 * pair + half

            def build_tile(dy, carry):
                tile = blocked
                for j in range(N_DX):
                    tile = jnp.where(dx == j, rpb_ref[(head * N_DY + dy) * N_DX + j], tile)
                tile_s[half * N_DY + dy] = jnp.where(in_win, tile, NEG_INF)
                return carry

            lax.fori_loop(0, N_DY, build_tile, 0)
            for kind in range(3):
                idx = (pair * 2 + half) * 3 + kind
                for i in range(NA_QROWS):
                    for k in range(NA_KROWS):
                        if kind == 0:
                            ok, dy = k < NA_KH, k - i + NA_KH - 1
                        elif kind == 1:
                            ok, dy = i <= k < i + NA_KH, k - i + NA_KH // 2 - 1
                        else:
                            ok, dy = k >= NA_KROWS - NA_KH, k - i - 1
                        slab_s[idx, i * GRID_W:(i + 1) * GRID_W, k * GRID_W:(k + 1) * GRID_W] = (
                            tile_s[half * N_DY + dy] if ok else blocked)

    first = lax.broadcasted_iota(jnp.int32, (qn, 128), 1) < DH
    ones_k = jnp.ones((kn, 128), BF16)
    ke = ke_ref[...].astype(BF16)
    ve_aug = jnp.concatenate([ve_ref[...].astype(BF16), jnp.ones((ke.shape[0], 128), BF16)], axis=1)

    def group_body(g, carry):
        k0 = jnp.where(g < NA_GROUPS // 2, 0, N_ROWS - NA_KROWS)
        kind = jnp.where(g == 0, 0, jnp.where(g == NA_GROUPS - 1, 2, 1))
        qrows = pl.ds(pl.multiple_of(g * qn, qn), qn)
        keys = pl.ds(pl.multiple_of(k0 * GRID_W, GRID_W), kn)
        q_pair = q_ref[qrows, :]
        k_pair = k_ref[keys, :].astype(BF16)
        v_aug = jnp.concatenate([v_ref[keys, :].astype(BF16), ones_k], axis=1)
        res = []
        for half in range(2):
            qm = jnp.where(first, q_pair, 0.0) if half == 0 else jnp.where(first, 0.0, q_pair)
            s = _dot_nt(qm, k_pair) + slab_s[(pair * 2 + half) * 3 + kind]
            se = _dot_nt(qm, ke)
            m = jnp.maximum(jnp.max(s, axis=-1, keepdims=True), jnp.max(se, axis=-1, keepdims=True))
            res.append(_dot(jnp.exp(s - m), v_aug) + _dot(jnp.exp(se - m), ve_aug))
        num = jnp.where(first, res[0][:, :128], res[1][:, :128])
        den = jnp.where(first, res[0][:, 128:], res[1][:, 128:])
        o_ref[qrows, :] = num / den
        return carry

    lax.fori_loop(0, NA_GROUPS, group_body, 0, unroll=True)


def _na_latent(q, k, v, k_ctx, v_ctx, rpb_flat, n_batch):
    seq = pl.BlockSpec((LAT_LEN, 128), lambda b, j: (b, j))
    ctx = pl.BlockSpec((CTX_LEN, 128), lambda b, j: (b, j))
    return pl.pallas_call(
        _na_kernel,
        out_shape=jax.ShapeDtypeStruct(q.shape, F32),
        grid=(n_batch, N_HEADS // 2),
        in_specs=[seq, seq, seq, ctx, ctx, pl.BlockSpec(memory_space=pltpu.SMEM)],
        out_specs=seq,
        scratch_shapes=[pltpu.VMEM((2 * N_DY, GRID_W, GRID_W), F32),
                        pltpu.VMEM((N_HEADS * 3, NA_QROWS * GRID_W, NA_KROWS * GRID_W), F32)],
        compiler_params=_params(("arbitrary", "arbitrary")), name="na_latent",
    )(q, k, v, k_ctx, v_ctx, rpb_flat)


SWA_QT = 256
SWA_SPAN = SWA_QT + 2 * SWA_WIN


def _swa_kernel(q_ref, kt_ref, vt_ref, ket_ref, vet_ref, sink_ref, o_ref):
    pair = pl.program_id(1)
    first = lax.broadcasted_iota(jnp.int32, (SWA_QT, 128), 1) < DH
    row = lax.broadcasted_iota(jnp.int32, (SWA_QT, SWA_SPAN), 0)
    col = lax.broadcasted_iota(jnp.int32, (SWA_QT, SWA_SPAN), 1)
    z = jnp.zeros((DH, SWA_SPAN), BF16)
    ones = jnp.ones((128, SWA_SPAN), BF16)
    ket, vet = ket_ref[0].astype(BF16), vet_ref[0].astype(BF16)
    ze = jnp.zeros_like(ket)
    ones_e = jnp.ones((128, ket.shape[1]), BF16)
    sinks = (sink_ref[2 * pair], sink_ref[2 * pair + 1])
    for g in range(LAT_LEN // SWA_QT):
        q0 = g * SWA_QT
        lo = min(max(q0 - SWA_WIN, 0), LAT_LEN - SWA_SPAN)
        near = jnp.abs((lo + col) - (q0 + row)) <= SWA_WIN
        qb = q_ref[q0:q0 + SWA_QT, :].astype(BF16)
        kt = kt_ref[0, :, lo:lo + SWA_SPAN].astype(BF16)
        vt = vt_ref[0, :, lo:lo + SWA_SPAN].astype(BF16)
        res, sink_den = [], []
        for half in range(2):
            k_op = jnp.concatenate([kt, z] if half == 0 else [z, kt], axis=0)
            v_op = jnp.concatenate([vt, z, ones] if half == 0 else [z, vt, ones], axis=0)
            ke_op = jnp.concatenate([ket, ze] if half == 0 else [ze, ket], axis=0)
            ve_op = jnp.concatenate([vet, ze, ones_e] if half == 0 else [ze, vet, ones_e], axis=0)
            s = jnp.where(near, jnp.dot(qb, k_op, preferred_element_type=F32), NEG_INF)
            se = jnp.dot(qb, ke_op, preferred_element_type=F32)
            m = jnp.maximum(jnp.max(s, axis=-1, keepdims=True), jnp.max(se, axis=-1, keepdims=True))
            m = jnp.maximum(m, sinks[half])
            sink_den.append(jnp.exp(sinks[half] - m))
            res.append(_dot_nt(jnp.exp(s - m), v_op) + _dot_nt(jnp.exp(se - m), ve_op))
        num = jnp.where(first, res[0][:, :128], res[1][:, :128])
        den = jnp.where(first, res[0][:, 128:], res[1][:, 128:]) + jnp.where(first, sink_den[0], sink_den[1])
        o_ref[q0:q0 + SWA_QT, :] = num / den


def _swa_latent(q, kt, vt, kt_ctx, vt_ctx, sink, n_batch, group):
    ppk = group // 2
    kv_spec = lambda a: pl.BlockSpec((1, DH, a.shape[2]), lambda b, j: (b, j // ppk, 0))
    return pl.pallas_call(
        _swa_kernel,
        out_shape=jax.ShapeDtypeStruct(q.shape, F32),
        grid=(n_batch, N_HEADS // 2),
        in_specs=[pl.BlockSpec((LAT_LEN, 128), lambda b, j: (b, j)),
                  kv_spec(kt), kv_spec(vt), kv_spec(kt_ctx), kv_spec(vt_ctx),
                  pl.BlockSpec(memory_space=pltpu.SMEM)],
        out_specs=pl.BlockSpec((LAT_LEN, 128), lambda b, j: (b, j)),
        compiler_params=_params(("arbitrary", "arbitrary")), name="swa_latent",
    )(q, kt, vt, kt_ctx, vt_ctx, sink)


def _rope_tables():
    t = np.arange(LAT_LEN)
    pos = np.stack([t // GRID_W, t % GRID_W], axis=-1).astype(np.float32)
    freqs = np.float32(ROPE_THETA) ** (-np.arange(ROPE_FREQS, dtype=np.float32) / np.float32(ROPE_FREQS))
    ang = (pos[:, :, None] * freqs).reshape(LAT_LEN, 2 * ROPE_FREQS).astype(np.float32)
    cos, sin = np.cos(ang), np.sin(ang)
    cos_t = np.tile(np.concatenate([cos, cos], axis=-1), (1, 2))
    sin_t = np.tile(np.concatenate([-sin, sin], axis=-1), (1, 2))
    return jnp.asarray(cos_t, F32), jnp.asarray(sin_t, F32)


def kernel(x_prompt, x_sample, cache_l0_attn_k, cache_l0_attn_v, state_l0_mlstm_C, state_l0_mlstm_n, state_l0_mlstm_m, cache_l1_na_k, cache_l1_na_v, cache_l1_swa_k, cache_l1_swa_v, c, c_ctx, norm_final, ada_w_l0, ada_b_l0, norm_l0, ffn1_in_l0, ffn1_out_l0, ffn2_in_l0, ffn2_out_l0, mix_in_l0, mix_out_l0, qk_norm_l0, gate_bias_l0, head_norm_l0, ada_w_l1, ada_b_l1, norm_l1, ffn1_in_l1, ffn1_out_l1, ffn2_in_l1, ffn2_out_l1, mix_in_l1, mix_out_l1, rpb_l1, sink_l1):
    nb, nl = x_prompt.shape[0], x_sample.shape[0]
    bf = lambda a: a.astype(BF16)

    cond8 = jnp.zeros((8, D_MODEL), F32).at[0].set(c_ctx).at[1:1 + nl].set(c)
    mods0 = _adaln(cond8, ada_w_l0, ada_b_l0).reshape(8 * ADA_CHUNKS, 1, D_MODEL)
    mods1 = _adaln(cond8, ada_w_l1, ada_b_l1).reshape(8 * ADA_CHUNKS, 1, D_MODEL)

    g0 = 2304
    gcols = lambda j: mix_in_l0[:, g0 + 8 * j:g0 + 8 * (j + 1)]
    gpad = jnp.zeros((D_MODEL, 128 - N_DIRHEAD), F32)
    w_gi, w_gf = jnp.concatenate([gcols(0), gcols(2)], axis=1), jnp.concatenate([gcols(1), gcols(3)], axis=1)
    w_even, w_ob = bf(mix_in_l0[:, :g0]), bf(mix_in_l0[:, g0 + 32:])
    w_g = bf(jnp.concatenate([w_gi, gpad, w_gf, gpad], axis=1))
    gb4 = gate_bias_l0.reshape(4, N_HEADS)
    b_gi, b_gf = jnp.concatenate([gb4[0], gb4[2]]), jnp.concatenate([gb4[1], gb4[3]])
    gbi = jnp.zeros((1, 128), F32).at[0, :N_DIRHEAD].set(b_gi)
    gbf = jnp.zeros((1, 128), F32).at[0, :N_DIRHEAD].set(b_gf)
    gbt = jnp.concatenate([b_gi, b_gf]).reshape(2 * N_DIRHEAD, 1)
    qg = jnp.tile(qk_norm_l0[0], N_HEADS).reshape(1, 512)
    kg = jnp.tile(qk_norm_l0[1], 2).reshape(1, 128)
    head_g = head_norm_l0.reshape(1, 512)
    grp = np.arange(256) // DH
    bd = jnp.asarray((grp[:, None] == grp[None, :]).astype(np.float32) / DH, dtype=BF16)
    rope_tabs = _rope_tables()
    w_odd = bf(mix_in_l1)
    rpb_flat = rpb_l1.reshape(-1)
    to_t = lambda a: a.reshape(a.shape[0], a.shape[1], -1).transpose(0, 2, 1)
    from_t = lambda a: a.reshape(a.shape[0], -1, DH, a.shape[2]).transpose(0, 3, 1, 2)

    xp = x_prompt.reshape(nb * CTX_LEN, D_MODEL)
    xs = x_sample.reshape(nl * LAT_LEN, D_MODEL)
    t_ctx, t_lat = xp.shape[0], xs.shape[0]
    streams = ((False, t_ctx, 0, nb, CTX_LEN), (True, t_lat, t_ctx // TOKEN_TILE, nl, LAT_LEN))
    ffn = functools.partial(_ffn, t_ctx=t_ctx, t_lat=t_lat)

    x = ffn((xp, xs), mods0, 0, norm_l0[0], ffn1_in_l0, ffn1_out_l0)
    ya, yb = {}, {}
    for latent, t, blk0, n_batch, seq_len in streams:
        qa, kat, vat, qb, vb, ob, gi, gf, gt, kbt = _proj_even(
            x, t, blk0, mods0, latent, norm_l0[1], w_even, w_ob, w_g, gbi, gbf, gbt, qg, kg, bd, rope_tabs)
        if latent:
            extra = (to_t(cache_l0_attn_k), to_t(cache_l0_attn_v))
            m0 = state_l0_mlstm_m.reshape(nl, N_DIRHEAD)
            m0_lanes = jnp.zeros((nl, 8, 128), F32).at[:, 0, :N_DIRHEAD].set(m0)
            m0_rows = jnp.broadcast_to(m0[:, :, None], (nl, N_DIRHEAD, 128))
            n0_cols = jnp.broadcast_to(state_l0_mlstm_n[..., None], (nl, 2, N_HEADS, DH, 128))
            state = (state_l0_mlstm_C, n0_cols, m0_lanes, m0_rows)
        else:
            extra, state = None, None
            k0t, v0t = kat, vat
        ya[latent] = _attention(qa, kat, vat, n_batch=n_batch, q_len=seq_len, q_tile=256, group=4, extra=extra)
        ml = _mlstm(qb, kbt, vb, ob, gi, gf, gt, head_g, bd, n_batch=n_batch, seq_len=seq_len, state=state)
        yb[latent] = ml[0]
        if not latent:
            c0, n0, m0_pad = ml[1:]
    x = ffn(x, mods0, 6, norm_l0[2], ffn2_in_l0, ffn2_out_l0, mix=(ya[False], yb[False], ya[True], yb[True], mix_out_l0))

    x = ffn(x, mods1, 0, norm_l1[0], ffn1_in_l1, ffn1_out_l1)
    yc, yd = {}, {}
    for latent, t, blk0, n_batch, seq_len in streams:
        qc, kc, vc, qd, kd, vd = _proj_odd(x, t, blk0, mods1, latent, norm_l1[1], w_odd, rope_tabs)
        if latent:
            yc[latent] = _na_latent(qc, kc, vc, cache_l1_na_k.reshape(nl * CTX_LEN, 512),
                                    cache_l1_na_v.reshape(nl * CTX_LEN, 512), rpb_flat, nl)
            yd[latent] = _swa_latent(qd, kd, vd, to_t(cache_l1_swa_k), to_t(cache_l1_swa_v), sink_l1, nl, group=4)
        else:
            yc[latent] = _attention(qc, kc, vc, n_batch=n_batch, q_len=seq_len, q_tile=256, group=1)
            yd[latent] = _attention(qd, kd, vd, n_batch=n_batch, q_len=seq_len, q_tile=256, group=4, sink=sink_l1)
            kc1t, vc1t, kd1t, vd1t = kc, vc, kd, vd
    y_prompt, y_sample = ffn(x, mods1, 6, norm_l1[2], ffn2_in_l1, ffn2_out_l1,
                             mix=(yc[False], yd[False], yc[True], yd[True], mix_out_l1), final_g=norm_final)

    return (y_prompt.reshape(nb, CTX_LEN, D_MODEL), y_sample.reshape(nl, LAT_LEN, D_MODEL),
            from_t(k0t), from_t(v0t),
            c0, n0, m0_pad[:, :, 0].reshape(nb, 2, N_HEADS),
            from_t(kc1t), from_t(vc1t),
            from_t(kd1t), from_t(vd1t))
```

```python
import functools

import jax
import jax.numpy as jnp
import numpy as np
from jax import lax
from jax.experimental import pallas as pl
from jax.experimental.pallas import tpu as pltpu

F32 = jnp.float32
BF16 = jnp.bfloat16

D_MODEL = 1024
DH = 64
D_FF = 2816
ADA_CHUNKS = 9
GRID_W = 64
LAT_LEN = 1024
CTX_LEN = 256
N_HEADS = 8
NA_KH = 8
NA_KW = 16
SWA_WIN = 128
ROPE_THETA = 10000.0
ROPE_FREQS = DH // 4
ATTN_SCALE = DH ** -0.5
NEG_INF = -1e30
EPS = 1e-6

TOKEN_TILE = 512
MLSTM_T = 256
FFN_CHUNK = 512
VMEM_LIMIT = 56 * 1024 * 1024
FFN_VMEM_LIMIT = 60 * 1024 * 1024
W_TILE = (256, 512)
W_SLOTS = 6


def _params(sem, vmem=VMEM_LIMIT):
    return pltpu.CompilerParams(dimension_semantics=sem, vmem_limit_bytes=vmem)


def _dot(a, b):
    return jnp.dot(a.astype(BF16), b.astype(BF16), preferred_element_type=F32)


def _dot_nt(a, b):
    return lax.dot_general(a.astype(BF16), b.astype(BF16), (((1,), (1,)), ((), ())),
                           preferred_element_type=F32)


def _split3(x):
    hi = x.astype(BF16)
    r1 = x - hi.astype(F32)
    mid = r1.astype(BF16)
    lo = (r1 - mid.astype(F32)).astype(BF16)
    return hi, mid, lo


def _dot_exact_rhs(a_bf16, x):
    hi, mid, lo = _split3(x)
    f = lambda p: jnp.dot(a_bf16, p, preferred_element_type=F32)
    return f(hi) + f(mid) + f(lo)


def _dot_exact_lhs(x, b_bf16):
    hi, mid, lo = _split3(x)
    f = lambda p: jnp.dot(p, b_bf16, preferred_element_type=F32)
    return f(hi) + f(mid) + f(lo)


def _silu(x):
    return x * jax.nn.sigmoid(x)


def _log_sigmoid(x):
    return jnp.minimum(x, 0.0) - jnp.log1p(jnp.exp(-jnp.abs(x)))


def _rms(x, g):
    return x * lax.rsqrt(jnp.mean(x * x, axis=-1, keepdims=True) + EPS) * g


def _head_mean_sq(x, bd_ref):
    n = x.shape[-1]
    sq = x * x
    hi = sq.astype(BF16)
    lo = (sq - hi.astype(F32)).astype(BF16)
    w = min(n, bd_ref.shape[0])
    bd = bd_ref[:w, :w]
    if 2 * w <= bd_ref.shape[0]:
        return jnp.dot(jnp.concatenate([hi, lo], axis=1), jnp.concatenate([bd, bd], axis=0),
                       preferred_element_type=F32)
    f = lambda p: jnp.dot(p, bd, preferred_element_type=F32)
    parts = [f(hi[:, c:c + w]) + f(lo[:, c:c + w]) for c in range(0, n, w)]
    return parts[0] if len(parts) == 1 else jnp.concatenate(parts, axis=1)


def _rope(x, cos, sin_signed):
    n = x.shape[-1]
    lane = lax.broadcasted_iota(jnp.int32, x.shape, 1)
    first_half = (lane % DH) < (DH // 2)
    partner = jnp.where(first_half, pltpu.roll(x, n - DH // 2, 1), pltpu.roll(x, DH // 2, 1))
    spread = lambda tab: tab if n == 128 else jnp.concatenate([tab] * (n // 128), axis=1)
    return x * spread(cos) + partner * spread(sin_signed)


def _adaln_kernel(c_ref, w_ref, b_ref, o_ref):
    s = _silu(c_ref[...])
    o_ref[...] = _dot(s, w_ref[...]) + b_ref[...]


def _adaln(cond8, w, b):
    n = w.shape[1]
    tn = 1536
    return pl.pallas_call(
        _adaln_kernel,
        out_shape=jax.ShapeDtypeStruct((8, n), F32),
        grid=(n // tn,),
        in_specs=[pl.BlockSpec((8, D_MODEL), lambda j: (0, 0)),
                  pl.BlockSpec((D_MODEL, tn), lambda j: (0, j)),
                  pl.BlockSpec((1, tn), lambda j: (0, j))],
        out_specs=pl.BlockSpec((8, tn), lambda j: (0, j)),
        compiler_params=_params(("arbitrary",)),
        name="adaln",
    )(cond8, w, b.reshape(1, n))


def _const_spec(shape):
    nd = len(shape)
    return pl.BlockSpec(shape, lambda i, _n=nd: (0,) * _n, pipeline_mode=pl.Buffered(1))


def _mod_spec(chunk, rowfn):
    return pl.BlockSpec((1, 1, D_MODEL), lambda i: (rowfn(i) * ADA_CHUNKS + chunk, 0, 0))


def _rowfn(latent):
    if latent:
        return lambda i: 1 + (i * TOKEN_TILE) // LAT_LEN
    return lambda i: 0


def _tok_spec(width, blk0=0):
    return pl.BlockSpec((TOKEN_TILE, width), lambda i: (i + blk0, 0))


def _seq_len(latent):
    return LAT_LEN if latent else CTX_LEN


def _transposed_out(t, latent, rows):
    seq = _seq_len(latent)
    shape = jax.ShapeDtypeStruct((t // seq, rows, seq), F32)
    if seq <= TOKEN_TILE:
        per = TOKEN_TILE // seq
        return shape, pl.BlockSpec((per, rows, seq), lambda i: (i, 0, 0))
    per = seq // TOKEN_TILE
    return shape, pl.BlockSpec((1, rows, TOKEN_TILE), lambda i: (i // per, 0, i % per))


def _store_transposed(ref, x):
    n, _, w = ref.shape
    for j in range(n):
        ref[j] = x[j * w:(j + 1) * w].T


def _dot_colsT(w_cols, h):
    return lax.dot_general(w_cols, h, (((0,), (1,)), ((), ())), preferred_element_type=F32)


def _store_proj_cols_transposed(refs, w_cols, h):
    n, _, w = refs[0].shape
    for j in range(n):
        out_t = _dot_colsT(w_cols, h[j * w:(j + 1) * w])
        r0 = 0
        for ref in refs:
            ref[j] = out_t[r0:r0 + ref.shape[1]]
            r0 += ref.shape[1]


def _modnorm(x, ng_ref, sh_ref, sc_ref):
    return _rms(x, ng_ref[...] * (1.0 + sc_ref[0])) + sh_ref[0]


def _load_weights_bf16(pairs, stage, sem):
    tr, tc = W_TILE
    tiles = [(src, dst, r, c) for src, dst in pairs
             for r in range(0, src.shape[0], tr) for c in range(0, src.shape[1], tc)]

    def copy(k):
        src, _, r, c = tiles[k]
        return pltpu.make_async_copy(src.at[r:r + tr, c:c + tc], stage.at[k % W_SLOTS], sem.at[k % W_SLOTS])

    for k in range(min(W_SLOTS, len(tiles))):
        copy(k).start()
    for k, (_, dst, r, c) in enumerate(tiles):
        copy(k).wait()
        dst[r:r + tr, c:c + tc] = stage[k % W_SLOTS].astype(BF16)
        if k + W_SLOTS < len(tiles):
            copy(k + W_SLOTS).start()


def _ffn_kernel(*refs, n_ctx, first, has_mix, last):
    it = iter(refs)
    xs = (next(it), next(it)) if first else (next(it),)
    if has_mix:
        yac_ref, ybc_ref, yal_ref, ybl_ref, wo_hbm, g2_ref = (next(it) for _ in range(6))
    ng_ref, sh_ref, sc_ref, g_ref, win_hbm, wout_hbm = (next(it) for _ in range(6))
    if last:
        nf_ref = next(it)
    outs = (next(it), next(it)) if last else (next(it),)
    win_s, wout_s = next(it), next(it)
    if has_mix:
        wo_s = next(it)
    stage, sem = next(it), next(it)

    step = pl.program_id(0)
    is_lat = step >= n_ctx

    @pl.when(step == 0)
    def _():
        pairs = [(win_hbm, win_s), (wout_hbm, wout_s)]
        if has_mix:
            pairs.insert(0, (wo_hbm, wo_s))
        _load_weights_bf16(pairs, stage, sem)

    pick = lambda c_ref, l_ref: jnp.where(is_lat, l_ref[...], c_ref[...])
    x = pick(*xs) if first else xs[0][...]
    if has_mix:
        half = wo_s.shape[0] // 2
        mo = _dot(pick(yac_ref, yal_ref), wo_s[:half, :]) + _dot(pick(ybc_ref, ybl_ref), wo_s[half:, :])
        x = x + g2_ref[0] * mo
    h = _modnorm(x, ng_ref, sh_ref, sc_ref).astype(BF16)
    acc = None
    for lo in range(0, D_FF, FFN_CHUNK):
        hi = min(lo + FFN_CHUNK, D_FF)
        g = _dot(h, win_s[:, lo:hi])
        u = _dot(h, win_s[:, D_FF + lo:D_FF + hi])
        part = _dot(_silu(g) * u, wout_s[lo:hi, :])
        acc = part if acc is None else acc + part
    y = x + (0.5 * g_ref[0]) * acc
    if last:
        y = _rms(y, nf_ref[...])
        oc_ref, ol_ref = outs

        @pl.when(jnp.logical_not(is_lat))
        def _():
            oc_ref[...] = y
            ol_ref[...] = jnp.zeros_like(ol_ref)

        @pl.when(is_lat)
        def _():
            ol_ref[...] = y
    else:
        outs[0][...] = y


def _ffn(x, mods, mod_base, ng, w_in, w_out, *, t_ctx, t_lat, mix=None, final_g=None):
    tile = TOKEN_TILE
    n_ctx, n_lat = t_ctx // tile, t_lat // tile
    first, last = isinstance(x, tuple), final_g is not None
    ctx_map = lambda i: (jnp.minimum(i, n_ctx - 1), 0)
    lat_map = lambda i: (jnp.maximum(i - n_ctx, 0), 0)
    rowfn = lambda i: jnp.where(i < n_ctx, 0, 1 + ((i - n_ctx) * tile) // LAT_LEN)
    two = lambda wd: [pl.BlockSpec((tile, wd), ctx_map), pl.BlockSpec((tile, wd), lat_map)]
    one = pl.BlockSpec((tile, D_MODEL), lambda i: (i, 0))
    hbm = pl.BlockSpec(memory_space=pl.ANY)
    args = list(x) if first else [x]
    specs = two(D_MODEL) if first else [one]
    scratch = [pltpu.VMEM(w_in.shape, BF16), pltpu.VMEM(w_out.shape, BF16)]
    if mix is not None:
        yac, ybc, yal, ybl, wo = mix
        args += [yac, ybc, yal, ybl, wo, mods]
        mw = yac.shape[1]
        specs += [pl.BlockSpec((tile, mw), ctx_map)] * 2 + [pl.BlockSpec((tile, mw), lat_map)] * 2
        specs += [hbm, _mod_spec(mod_base - 1, rowfn)]
        scratch.append(pltpu.VMEM(wo.shape, BF16))
    args += [ng.reshape(1, D_MODEL), mods, mods, mods, w_in, w_out]
    specs += [_const_spec((1, D_MODEL)), _mod_spec(mod_base, rowfn), _mod_spec(mod_base + 1, rowfn),
              _mod_spec(mod_base + 2, rowfn), hbm, hbm]
    if last:
        args.append(final_g.reshape(1, D_MODEL))
        specs.append(_const_spec((1, D_MODEL)))
        out_shape = [jax.ShapeDtypeStruct((t_ctx, D_MODEL), F32), jax.ShapeDtypeStruct((t_lat, D_MODEL), F32)]
        out_specs = two(D_MODEL)
    else:
        out_shape = jax.ShapeDtypeStruct((t_ctx + t_lat, D_MODEL), F32)
        out_specs = one
    scratch += [pltpu.VMEM((W_SLOTS,) + W_TILE, F32), pltpu.SemaphoreType.DMA((W_SLOTS,))]
    return pl.pallas_call(
        functools.partial(_ffn_kernel, n_ctx=n_ctx, first=first, has_mix=mix is not None, last=last),
        out_shape=out_shape,
        grid=(n_ctx + n_lat,),
        in_specs=specs,
        out_specs=out_specs,
        scratch_shapes=scratch,
        compiler_params=_params(("arbitrary",), FFN_VMEM_LIMIT),
        name="ffn",
    )(*args)


_EV_QA, _EV_KA, _EV_VA, _EV_QB, _EV_KB, _EV_VB, _EV_OB = 0, 512, 640, 768, 1280, 1792, 2304
N_DIRHEAD = 2 * N_HEADS


def _proj_even_kernel(*refs, rope):
    it = iter(refs)
    x_ref, ng_ref, sh_ref, sc_ref, w_ref, wob_ref, wg_ref, gbi_ref, gbf_ref, gbt_ref, qg_ref, kg_ref, bd_ref = (
        next(it) for _ in range(13))
    if rope:
        cos_ref, sin_ref = next(it), next(it)
    qa_ref, kat_ref, vat_ref, qb_ref, vb_ref, ob_ref, gi_ref, gf_ref, gt_ref, kbt_ref = (next(it) for _ in range(10))

    h = _modnorm(x_ref[...], ng_ref, sh_ref, sc_ref).astype(BF16)
    proj = lambda lo, hi: _dot(h, w_ref[:, lo:hi])

    qa = proj(_EV_QA, _EV_KA)
    qa = qa * lax.rsqrt(_head_mean_sq(qa, bd_ref) + EPS) * qg_ref[...]
    ka = proj(_EV_KA, _EV_VA)
    ka = ka * lax.rsqrt(_head_mean_sq(ka, bd_ref) + EPS) * kg_ref[...]
    if rope:
        qa = _rope(qa, cos_ref[...], sin_ref[...])
        ka = _rope(ka, cos_ref[...], sin_ref[...])
    qa_ref[...] = qa * ATTN_SCALE
    _store_transposed(kat_ref, ka)
    _store_proj_cols_transposed((vat_ref,), w_ref[:, _EV_VA:_EV_QB], h)
    qb_ref[...] = proj(_EV_QB, _EV_KB)
    vb_ref[...] = proj(_EV_VB, _EV_OB)
    ob_ref[...] = _dot(h, wob_ref[...])
    gates = _dot(h, wg_ref[...])
    gi_ref[...] = gates[:, :128] + gbi_ref[...]
    gf_ref[...] = gates[:, 128:] + gbf_ref[...]
    for j in range(TOKEN_TILE // MLSTM_T):
        ht = h[j * MLSTM_T:(j + 1) * MLSTM_T]
        gates_t = [_dot_colsT(wg_ref[:, c0:c0 + N_DIRHEAD], ht) for c0 in (0, 128)]
        gt_ref[j] = jnp.concatenate(gates_t, axis=0) + gbt_ref[...]
        kbt_ref[j] = _dot_colsT(w_ref[:, _EV_KB:_EV_VB], ht) * ATTN_SCALE


def _proj_even(x, t, blk0, mods, latent, ng, w, w_ob, w_g, gbi, gbf, gbt, qg, kg, bd, rope_tabs):
    rowfn = _rowfn(latent)
    args = [x, ng.reshape(1, D_MODEL), mods, mods, w, w_ob, w_g, gbi, gbf, gbt, qg, kg, bd]
    specs = [_tok_spec(D_MODEL, blk0), _const_spec((1, D_MODEL)), _mod_spec(3, rowfn), _mod_spec(4, rowfn)]
    specs += [_const_spec(a.shape) for a in args[4:]]
    if latent:
        nblk = LAT_LEN // TOKEN_TILE
        args += list(rope_tabs)
        specs += [pl.BlockSpec((TOKEN_TILE, 128), lambda i: (i % nblk, 0))] * 2
    widths = (512, 512, 512, 512, 128, 128)
    out_shape = [jax.ShapeDtypeStruct((t, wd), F32) for wd in widths]
    out_specs = [_tok_spec(wd) for wd in widths]
    for pos in (1, 2):
        shape, spec = _transposed_out(t, latent, 128)
        out_shape.insert(pos, shape)
        out_specs.insert(pos, spec)
    nj = TOKEN_TILE // MLSTM_T
    for rows in (2 * N_DIRHEAD, 512):
        out_shape.append(jax.ShapeDtypeStruct((t // MLSTM_T, rows, MLSTM_T), F32))
        out_specs.append(pl.BlockSpec((nj, rows, MLSTM_T), lambda i: (i, 0, 0)))
    return pl.pallas_call(
        functools.partial(_proj_even_kernel, rope=latent),
        out_shape=out_shape, grid=(t // TOKEN_TILE,), in_specs=specs, out_specs=out_specs,
        compiler_params=_params(("arbitrary",)), name="proj_even",
    )(*args)


def _proj_odd_kernel(*refs, rope):
    it = iter(refs)
    x_ref, ng_ref, sh_ref, sc_ref, w_ref = (next(it) for _ in range(5))
    if rope:
        cos_ref, sin_ref = next(it), next(it)
    qc_ref, kc_ref, vc_ref, qd_ref, kd_ref, vd_ref = (next(it) for _ in range(6))

    h = _modnorm(x_ref[...], ng_ref, sh_ref, sc_ref).astype(BF16)
    proj = lambda lo, hi: _dot(h, w_ref[:, lo:hi])
    qc_ref[...] = proj(0, 512) * ATTN_SCALE
    if rope:
        kc_ref[...] = proj(512, 1024)
        vc_ref[...] = proj(1024, 1536)
    else:
        _store_proj_cols_transposed((kc_ref, vc_ref), w_ref[:, 512:1536], h)
    qd = proj(1536, 2048)
    if rope:
        qd = _rope(qd, cos_ref[...], sin_ref[...])
        _store_transposed(kd_ref, _rope(proj(2048, 2176), cos_ref[...], sin_ref[...]))
        _store_proj_cols_transposed((vd_ref,), w_ref[:, 2176:2304], h)
    else:
        _store_proj_cols_transposed((kd_ref, vd_ref), w_ref[:, 2048:2304], h)
    qd_ref[...] = qd * ATTN_SCALE


def _proj_odd(x, t, blk0, mods, latent, ng, w, rope_tabs):
    rowfn = _rowfn(latent)
    args = [x, ng.reshape(1, D_MODEL), mods, mods, w]
    specs = [_tok_spec(D_MODEL, blk0), _const_spec((1, D_MODEL)), _mod_spec(3, rowfn), _mod_spec(4, rowfn),
             _const_spec(w.shape)]
    if latent:
        nblk = LAT_LEN // TOKEN_TILE
        args += list(rope_tabs)
        specs += [pl.BlockSpec((TOKEN_TILE, 128), lambda i: (i % nblk, 0))] * 2
    out_shape = [jax.ShapeDtypeStruct((t, 512), F32)] * 4
    out_specs = [_tok_spec(512)] * 4
    if not latent:
        out_shape[1], out_specs[1] = _transposed_out(t, latent, 512)
        out_shape[2], out_specs[2] = out_shape[1], out_specs[1]
    for _ in range(2):
        shape, spec = _transposed_out(t, latent, 128)
        out_shape.append(shape)
        out_specs.append(spec)
    return pl.pallas_call(
        functools.partial(_proj_odd_kernel, rope=latent),
        out_shape=out_shape,
        grid=(t // TOKEN_TILE,), in_specs=specs, out_specs=out_specs,
        compiler_params=_params(("arbitrary",)), name="proj_odd",
    )(*args)


def _attn_kernel(*refs, group, has_extra, has_sink, spb):
    it = iter(refs)
    q_ref, k_ref, v_ref = next(it), next(it), next(it)
    if has_extra:
        ke_ref, ve_ref = next(it), next(it)
    if has_sink:
        sink_ref = next(it)
    o_ref = next(it)
    tq = q_ref.shape[0] // spb
    first = lax.broadcasted_iota(jnp.int32, (tq, 128), 1) < DH

    def kv_operands(kt_ref, vt_ref, smp, kv):
        kt = kt_ref[smp, kv * DH:(kv + 1) * DH, :].astype(BF16)
        vt = vt_ref[smp, kv * DH:(kv + 1) * DH, :].astype(BF16)
        z = jnp.zeros_like(kt)
        ones = jnp.ones((128, kt.shape[1]), BF16)
        return ((jnp.concatenate([kt, z], axis=0), jnp.concatenate([z, kt], axis=0)),
                (jnp.concatenate([vt, z, ones], axis=0), jnp.concatenate([z, vt, ones], axis=0)))

    for smp in range(spb):
        qrows = slice(smp * tq, (smp + 1) * tq)
        cache = {}
        for j in range(N_HEADS // 2):
            ps = slice(128 * j, 128 * (j + 1))
            qb = q_ref[qrows, ps].astype(BF16)
            res, sink_den = [], []
            for half in range(2):
                kv = (2 * j + half) // group
                if kv not in cache:
                    cache[kv] = (kv_operands(k_ref, v_ref, smp, kv),
                                 kv_operands(ke_ref, ve_ref, smp, kv) if has_extra else None)
                (k_ops, v_ops), extra = cache[kv]
                s = jnp.dot(qb, k_ops[half], preferred_element_type=F32)
                m = jnp.max(s, axis=-1, keepdims=True)
                if has_extra:
                    se = jnp.dot(qb, extra[0][half], preferred_element_type=F32)
                    m = jnp.maximum(m, jnp.max(se, axis=-1, keepdims=True))
                if has_sink:
                    m = jnp.maximum(m, sink_ref[2 * j + half])
                    sink_den.append(jnp.exp(sink_ref[2 * j + half] - m))
                r = _dot_nt(jnp.exp(s - m), v_ops[half])
                if has_extra:
                    r = r + _dot_nt(jnp.exp(se - m), extra[1][half])
                res.append(r)
            num = jnp.where(first, res[0][:, :128], res[1][:, :128])
            den = jnp.where(first, res[0][:, 128:], res[1][:, 128:])
            if has_sink:
                den = den + jnp.where(first, sink_den[0], sink_den[1])
            o_ref[qrows, ps] = num / den


def _attention(q, k, v, *, n_batch, q_len, q_tile, group, extra=None, sink=None):
    nq = q_len // q_tile
    spb = next(n for n in (8, 4, 2, 1) if n_batch % n == 0) if nq == 1 else 1
    kv_spec = lambda a: pl.BlockSpec((spb,) + a.shape[1:], lambda b, j: (b, 0, 0))
    args = [q, k, v]
    specs = [pl.BlockSpec((spb * q_tile, 512), lambda b, j: (b * nq + j, 0)), kv_spec(k), kv_spec(v)]
    if extra is not None:
        args += list(extra)
        specs += [kv_spec(extra[0]), kv_spec(extra[1])]
    if sink is not None:
        args.append(sink)
        specs.append(pl.BlockSpec(memory_space=pltpu.SMEM))
    return pl.pallas_call(
        functools.partial(_attn_kernel, group=group, has_extra=extra is not None, has_sink=sink is not None, spb=spb),
        out_shape=jax.ShapeDtypeStruct(q.shape, F32),
        grid=(n_batch // spb, nq), in_specs=specs,
        out_specs=pl.BlockSpec((spb * q_tile, 512), lambda b, j: (b * nq + j, 0)),
        compiler_params=_params(("arbitrary", "arbitrary")), name="attn",
    )(*args)


LOG2E = 1.4426950408889634


def _cummax_time(x, reverse):
    n = x.shape[0]
    row = lax.broadcasted_iota(jnp.int32, x.shape, 0)
    k = 1
    while k < n:
        if reverse:
            shifted = jnp.where(row < n - k, pltpu.roll(x, n - k, 0), -jnp.inf)
        else:
            shifted = jnp.where(row >= k, pltpu.roll(x, k, 0), -jnp.inf)
        x = jnp.maximum(x, shifted)
        k *= 2
    return x


def _mlstm_kernel(*refs, n_chunks, carry, spb):
    it = iter(refs)
    q_ref, kt_ref, v_ref, ob_ref, gi_ref, gf_ref, gt_ref, hg_ref, bd_ref = (next(it) for _ in range(9))
    if carry:
        c0_ref, n0_ref, m0r_ref, m0c_ref = (next(it) for _ in range(4))
    y_ref = next(it)
    if not carry:
        c_ref, n_ref, m_ref = next(it), next(it), next(it)
    hf_s, hb_s = next(it), next(it)
    if carry:
        st_s, mrow_s, mcol_s = next(it), next(it), next(it)
    T = MLSTM_T
    step = pl.program_id(1)

    if carry:
        @pl.when(step == 0)
        def _():
            z = jnp.zeros((DH, DH), F32)
            for d in range(2):
                for j in range(N_HEADS // 2):
                    ca, cb = c0_ref[0, d, 2 * j], c0_ref[0, d, 2 * j + 1]
                    na, nb = n0_ref[0, d, 2 * j][:, :DH], n0_ref[0, d, 2 * j + 1][:, :DH]
                    st_s[d, j, 0:DH, :] = jnp.concatenate([ca, z, na, z], axis=1)
                    st_s[d, j, DH:2 * DH, :] = jnp.concatenate([z, cb, z, nb], axis=1)
            mrow_s[...] = m0r_ref[0]
            mcol_s[...] = m0c_ref[0]

    row = lax.broadcasted_iota(jnp.int32, (T, T), 0)
    col = lax.broadcasted_iota(jnp.int32, (T, T), 1)
    lower = row >= col
    upper = row <= col
    lower_m = jnp.where(lower, 1.0, 0.0).astype(BF16)
    upper_m = jnp.where(upper, 1.0, 0.0).astype(BF16)
    lane = lax.broadcasted_iota(jnp.int32, (T, 128), 1)
    first = lane < DH
    lane_c = lax.broadcasted_iota(jnp.int32, (DH, 256), 1) % 128
    zeros_kt = jnp.zeros((DH, T), BF16)
    ones_v = jnp.ones((T, 128), BF16)

    for smp, d in ((s_, d_) for s_ in range(spb) for d_ in range(2)):
        chunk = (step if d == 0 else n_chunks - 1 - step) if carry else smp
        rows = pl.ds(pl.multiple_of(chunk * T, T), T)
        hd0 = N_HEADS * d
        mask = lower if d == 0 else upper

        if carry or d == 0:
            gi = gi_ref[rows, :]
            lf = _log_sigmoid(gf_ref[rows, :])
            prefix = _dot_exact_rhs(lower_m, lf)
        b_cols = prefix if d == 0 else prefix[T - 1:T, :] - prefix + lf
        a_cols = gi - b_cols
        dmax = b_cols + _cummax_time(a_cols, reverse=d == 1)
        if carry:
            inter = b_cols + mrow_s[0:1, :]
            mt = jnp.maximum(inter, dmax)
            w_inter = jnp.exp(inter - mt)
        else:
            mt = jnp.maximum(b_cols, dmax)
        c2 = (b_cols - mt) * LOG2E
        einv = jnp.exp(-mt)

        gt = gt_ref[chunk]
        gi_t = gt[hd0:hd0 + N_HEADS, :]
        gf_t = gt[N_DIRHEAD + hd0:N_DIRHEAD + hd0 + N_HEADS, :]
        lf_t = _log_sigmoid(gf_t)
        b_rows = _dot_exact_lhs(lf_t, upper_m)
        if d == 1:
            b_rows = b_rows[:, T - 1:T] - b_rows + lf_t
        a_rows = gi_t - b_rows
        a2 = a_rows * LOG2E
        b_last = b_rows[:, T - 1:T] if d == 0 else b_rows[:, 0:1]
        dec_rows = b_last + a_rows
        m_new = jnp.max(dec_rows, axis=-1, keepdims=True)
        if carry:
            m_prev = mcol_s[hd0:hd0 + N_HEADS, 0:1]
            m_new = jnp.maximum(m_new, b_last + m_prev)
            wc = jnp.exp(b_last + m_prev - m_new)
        else:
            m_new = jnp.maximum(m_new, b_last)
        ws_rows = jnp.exp(dec_rows - m_new)

        h_s = hf_s if d == 0 else hb_s
        for j in range(N_HEADS // 2):
            ps = slice(128 * j, 128 * (j + 1))
            q_pair = q_ref[rows, ps].astype(BF16)
            v_aug = jnp.concatenate([v_ref[rows, ps].astype(BF16), ones_v], axis=1)
            kt = (kt_ref[chunk, 128 * j:128 * j + DH, :], kt_ref[chunk, 128 * j + DH:128 * (j + 1), :])
            lhs = []
            for half in range(2):
                hd = 2 * j + half
                kt0 = jnp.concatenate([kt[0].astype(BF16), zeros_kt] if half == 0 else [zeros_kt, kt[1].astype(BF16)], axis=0)
                qk = jnp.dot(q_pair, kt0, preferred_element_type=F32)
                e = c2[:, hd0 + hd:hd0 + hd + 1] + a2[hd:hd + 1, :]
                lhs.append((jnp.exp2(jnp.where(mask, e, -jnp.inf)) * qk).astype(BF16))
            for half in range(2):
                lhs.append((kt[half] * ws_rows[2 * j + half:2 * j + half + 1, :]).astype(BF16))
            both = jnp.dot(jnp.concatenate(lhs, axis=0), v_aug, preferred_element_type=F32)
            res = (both[0:T], both[T:2 * T])
            cu = (both[2 * T:2 * T + DH], both[2 * T + DH:2 * T + 2 * DH])
            ra, rb = hd0 + 2 * j, hd0 + 2 * j + 1
            num = jnp.where(first, res[0][:, :128], res[1][:, :128])
            den = jnp.where(first, res[0][:, 128:], res[1][:, 128:])
            if carry:
                old = st_s[d, j]
                qs = _dot(q_pair, old)
                wi = jnp.where(first, w_inter[:, ra:ra + 1], w_inter[:, rb:rb + 1])
                num = num + wi * qs[:, :128]
                den = den + wi * qs[:, 128:]
            floor = jnp.where(first, einv[:, ra:ra + 1], einv[:, rb:rb + 1])
            h_s[rows, ps] = num / jnp.maximum(jnp.abs(den), floor)
            if carry:
                st_s[d, j, 0:DH, :] = jnp.where(lane_c < DH, cu[0], 0.0) + wc[2 * j:2 * j + 1, :] * old[0:DH]
                st_s[d, j, DH:2 * DH, :] = jnp.where(lane_c >= DH, cu[1], 0.0) + wc[2 * j + 1:2 * j + 2, :] * old[DH:2 * DH]
            else:
                c_ref[smp, d, 2 * j] = cu[0][:, 0:DH]
                c_ref[smp, d, 2 * j + 1] = cu[1][:, DH:2 * DH]

        if carry:
            b_last_l = b_cols[T - 1:T, :] if d == 0 else b_cols[0:1, :]
            m_prev_l = mrow_s[0:1, :]
            m_new_l = jnp.maximum(jnp.max(b_last_l + a_cols, axis=0, keepdims=True), b_last_l + m_prev_l)
            lane_r = lax.broadcasted_iota(jnp.int32, (1, 128), 1)
            mine = (lane_r >= hd0) & (lane_r < hd0 + N_HEADS)
            mrow_s[0:1, :] = jnp.where(mine, m_new_l, m_prev_l)
            mcol_s[hd0:hd0 + N_HEADS, :] = jnp.broadcast_to(m_new, (N_HEADS, 128))
        else:
            for hd in range(N_HEADS):
                n_hd = _dot_nt(ws_rows, kt_ref[chunk, hd * DH:(hd + 1) * DH, :])
                n_ref[smp, d, hd:hd + 1, :] = n_hd[hd:hd + 1, :]
            m_ref[smp, hd0:hd0 + N_HEADS, :] = jnp.broadcast_to(m_new, (N_HEADS, 128))

    @pl.when(step == n_chunks - 1)
    def _():
        hsum = hf_s[...] + hb_s[...]
        y = hsum * lax.rsqrt(_head_mean_sq(hsum, bd_ref) + EPS) * hg_ref[...]
        y_ref[...] = jax.nn.sigmoid(ob_ref[...]) * y


def _mlstm(qb, kbt, vb, ob, gi, gf, gt, head_g, bd, *, n_batch, seq_len, state=None):
    n_chunks = seq_len // MLSTM_T
    carry = state is not None
    assert carry or n_chunks == 1
    spb = 1
    seq = lambda wd: pl.BlockSpec((spb * seq_len, wd), lambda b, c: (b, 0))
    per_chunk = lambda r: pl.BlockSpec((spb * n_chunks, r, MLSTM_T), lambda b, c: (b, 0, 0))
    args = [qb, kbt, vb, ob, gi, gf, gt, head_g, bd]
    specs = [seq(512), per_chunk(512), seq(512), seq(512), seq(128), seq(128), per_chunk(2 * N_DIRHEAD),
             pl.BlockSpec((1, 512), lambda b, c: (0, 0)), pl.BlockSpec(bd.shape, lambda b, c: (0, 0))]
    out_shape = [jax.ShapeDtypeStruct((n_batch * seq_len, 512), F32)]
    out_specs = [seq(512)]
    scratch = [pltpu.VMEM((spb * seq_len, 512), F32), pltpu.VMEM((spb * seq_len, 512), F32)]
    if carry:
        args += list(state)
        specs += [pl.BlockSpec((1, 2, N_HEADS, DH, DH), lambda b, c: (b, 0, 0, 0, 0)),
                  pl.BlockSpec((1, 2, N_HEADS, DH, 128), lambda b, c: (b, 0, 0, 0, 0)),
                  pl.BlockSpec((1, 8, 128), lambda b, c: (b, 0, 0)),
                  pl.BlockSpec((1, N_DIRHEAD, 128), lambda b, c: (b, 0, 0))]
        scratch += [pltpu.VMEM((2, N_HEADS // 2, 128, 256), F32), pltpu.VMEM((8, 128), F32),
                    pltpu.VMEM((N_DIRHEAD, 128), F32)]
    else:
        out_shape += [jax.ShapeDtypeStruct((n_batch, 2, N_HEADS, DH, DH), F32),
                      jax.ShapeDtypeStruct((n_batch, 2, N_HEADS, DH), F32),
                      jax.ShapeDtypeStruct((n_batch, N_DIRHEAD, 128), F32)]
        out_specs += [pl.BlockSpec((spb, 2, N_HEADS, DH, DH), lambda b, c: (b, 0, 0, 0, 0)),
                      pl.BlockSpec((spb, 2, N_HEADS, DH), lambda b, c: (b, 0, 0, 0)),
                      pl.BlockSpec((spb, N_DIRHEAD, 128), lambda b, c: (b, 0, 0))]
    return pl.pallas_call(
        functools.partial(_mlstm_kernel, n_chunks=n_chunks, carry=carry, spb=spb),
        out_shape=out_shape, grid=(n_batch // spb, n_chunks), in_specs=specs, out_specs=out_specs,
        scratch_shapes=scratch,
        compiler_params=_params(("arbitrary", "arbitrary")), name="mlstm",
    )(*args)


N_ROWS = LAT_LEN // GRID_W
N_DY = 2 * NA_KH - 1
N_DX = 2 * NA_KW - 1


NA_QROWS = 4
NA_KROWS = 12
NA_GROUPS = N_ROWS // NA_QROWS


def _na_kernel(q_ref, k_ref, v_ref, ke_ref, ve_ref, rpb_ref, o_ref, tile_s, slab_s):
    pair = pl.program_id(1)
    qn, kn = NA_QROWS * GRID_W, NA_KROWS * GRID_W

    @pl.when(pl.program_id(0) == 0)
    def _():
        qc = lax.broadcasted_iota(jnp.int32, (GRID_W, GRID_W), 0)
        kc = lax.broadcasted_iota(jnp.int32, (GRID_W, GRID_W), 1)
        start = jnp.clip(qc - NA_KW // 2, 0, GRID_W - NA_KW)
        in_win = (kc >= start) & (kc < start + NA_KW)
        dx = kc - qc + (NA_KW - 1)
        blocked = jnp.full((GRID_W, GRID_W), NEG_INF, F32)
        for half in range(2):
            head = 2 * pair + half

            def build_tile(dy, carry):
                tile = blocked
                for j in range(N_DX):
                    tile = jnp.where(dx == j, rpb_ref[(head * N_DY + dy) * N_DX + j], tile)
                tile_s[half * N_DY + dy] = jnp.where(in_win, tile, NEG_INF)
                return carry

            lax.fori_loop(0, N_DY, build_tile, 0)
            for kind in range(3):
                idx = (pair * 2 + half) * 3 + kind
                for i in range(NA_QROWS):
                    for k in range(NA_KROWS):
                        if kind == 0:
                            ok, dy = k < NA_KH, k - i + NA_KH - 1
                        elif kind == 1:
                            ok, dy = i <= k < i + NA_KH, k - i + NA_KH // 2 - 1
                        else:
                            ok, dy = k >= NA_KROWS - NA_KH, k - i - 1
                        slab_s[idx, i * GRID_W:(i + 1) * GRID_W, k * GRID_W:(k + 1) * GRID_W] = (
                            tile_s[half * N_DY + dy] if ok else blocked)

    first = lax.broadcasted_iota(jnp.int32, (qn, 128), 1) < DH
    ones_k = jnp.ones((kn, 128), BF16)
    ke = ke_ref[...].astype(BF16)
    ve_aug = jnp.concatenate([ve_ref[...].astype(BF16), jnp.ones((ke.shape[0], 128), BF16)], axis=1)

    def group_body(g, carry):
        k0 = jnp.where(g < NA_GROUPS // 2, 0, N_ROWS - NA_KROWS)
        kind = jnp.where(g == 0, 0, jnp.where(g == NA_GROUPS - 1, 2, 1))
        qrows = pl.ds(pl.multiple_of(g * qn, qn), qn)
        keys = pl.ds(pl.multiple_of(k0 * GRID_W, GRID_W), kn)
        q_pair = q_ref[qrows, :]
        k_pair = k_ref[keys, :].astype(BF16)
        v_aug = jnp.concatenate([v_ref[keys, :].astype(BF16), ones_k], axis=1)
        res = []
        for half in range(2):
            qm = jnp.where(first, q_pair, 0.0) if half == 0 else jnp.where(first, 0.0, q_pair)
            s = _dot_nt(qm, k_pair) + slab_s[(pair * 2 + half) * 3 + kind]
            se = _dot_nt(qm, ke)
            m = jnp.maximum(jnp.max(s, axis=-1, keepdims=True), jnp.max(se, axis=-1, keepdims=True))
            res.append(_dot(jnp.exp(s - m), v_aug) + _dot(jnp.exp(se - m), ve_aug))
        num = jnp.where(first, res[0][:, :128], res[1][:, :128])
        den = jnp.where(first, res[0][:, 128:], res[1][:, 128:])
        o_ref[qrows, :] = num / den
        return carry

    lax.fori_loop(0, NA_GROUPS, group_body, 0, unroll=True)


def _na_latent(q, k, v, k_ctx, v_ctx, rpb_flat, n_batch):
    seq = pl.BlockSpec((LAT_LEN, 128), lambda b, j: (b, j))
    ctx = pl.BlockSpec((CTX_LEN, 128), lambda b, j: (b, j))
    return pl.pallas_call(
        _na_kernel,
        out_shape=jax.ShapeDtypeStruct(q.shape, F32),
        grid=(n_batch, N_HEADS // 2),
        in_specs=[seq, seq, seq, ctx, ctx, pl.BlockSpec(memory_space=pltpu.SMEM)],
        out_specs=seq,
        scratch_shapes=[pltpu.VMEM((2 * N_DY, GRID_W, GRID_W), F32),
                        pltpu.VMEM((N_HEADS * 3, NA_QROWS * GRID_W, NA_KROWS * GRID_W), F32)],
        compiler_params=_params(("arbitrary", "arbitrary")), name="na_latent",
    )(q, k, v, k_ctx, v_ctx, rpb_flat)


SWA_QT = 256
SWA_SPAN = SWA_QT + 2 * SWA_WIN


def _swa_kernel(q_ref, kt_ref, vt_ref, ket_ref, vet_ref, sink_ref, o_ref):
    pair = pl.program_id(1)
    first = lax.broadcasted_iota(jnp.int32, (SWA_QT, 128), 1) < DH
    row = lax.broadcasted_iota(jnp.int32, (SWA_QT, SWA_SPAN), 0)
    col = lax.broadcasted_iota(jnp.int32, (SWA_QT, SWA_SPAN), 1)
    z = jnp.zeros((DH, SWA_SPAN), BF16)
    ones = jnp.ones((128, SWA_SPAN), BF16)
    ket, vet = ket_ref[0].astype(BF16), vet_ref[0].astype(BF16)
    ze = jnp.zeros_like(ket)
    ones_e = jnp.ones((128, ket.shape[1]), BF16)
    sinks = (sink_ref[2 * pair], sink_ref[2 * pair + 1])
    for g in range(LAT_LEN // SWA_QT):
        q0 = g * SWA_QT
        lo = min(max(q0 - SWA_WIN, 0), LAT_LEN - SWA_SPAN)
        near = jnp.abs((lo + col) - (q0 + row)) <= SWA_WIN
        qb = q_ref[q0:q0 + SWA_QT, :].astype(BF16)
        kt = kt_ref[0, :, lo:lo + SWA_SPAN].astype(BF16)
        vt = vt_ref[0, :, lo:lo + SWA_SPAN].astype(BF16)
        res, sink_den = [], []
        for half in range(2):
            k_op = jnp.concatenate([kt, z] if half == 0 else [z, kt], axis=0)
            v_op = jnp.concatenate([vt, z, ones] if half == 0 else [z, vt, ones], axis=0)
            ke_op = jnp.concatenate([ket, ze] if half == 0 else [ze, ket], axis=0)
            ve_op = jnp.concatenate([vet, ze, ones_e] if half == 0 else [ze, vet, ones_e], axis=0)
            s = jnp.where(near, jnp.dot(qb, k_op, preferred_element_type=F32), NEG_INF)
            se = jnp.dot(qb, ke_op, preferred_element_type=F32)
            m = jnp.maximum(jnp.max(s, axis=-1, keepdims=True), jnp.max(se, axis=-1, keepdims=True))
            m = jnp.maximum(m, sinks[half])
            sink_den.append(jnp.exp(sinks[half] - m))
            res.append(_dot_nt(jnp.exp(s - m), v_op) + _dot_nt(jnp.exp(se - m), ve_op))
        num = jnp.where(first, res[0][:, :128], res[1][:, :128])
        den = jnp.where(first, res[0][:, 128:], res[1][:, 128:]) + jnp.where(first, sink_den[0], sink_den[1])
        o_ref[q0:q0 + SWA_QT, :] = num / den


def _swa_latent(q, kt, vt, kt_ctx, vt_ctx, sink, n_batch, group):
    ppk = group // 2
    kv_spec = lambda a: pl.BlockSpec((1, DH, a.shape[2]), lambda b, j: (b, j // ppk, 0))
    return pl.pallas_call(
        _swa_kernel,
        out_shape=jax.ShapeDtypeStruct(q.shape, F32),
        grid=(n_batch, N_HEADS // 2),
        in_specs=[pl.BlockSpec((LAT_LEN, 128), lambda b, j: (b, j)),
                  kv_spec(kt), kv_spec(vt), kv_spec(kt_ctx), kv_spec(vt_ctx),
                  pl.BlockSpec(memory_space=pltpu.SMEM)],
        out_specs=pl.BlockSpec((LAT_LEN, 128), lambda b, j: (b, j)),
        compiler_params=_params(("arbitrary", "arbitrary")), name="swa_latent",
    )(q, kt, vt, kt_ctx, vt_ctx, sink)


def _rope_tables():
    t = np.arange(LAT_LEN)
    pos = np.stack([t // GRID_W, t % GRID_W], axis=-1).astype(np.float32)
    freqs = np.float32(ROPE_THETA) ** (-np.arange(ROPE_FREQS, dtype=np.float32) / np.float32(ROPE_FREQS))
    ang = (pos[:, :, None] * freqs).reshape(LAT_LEN, 2 * ROPE_FREQS).astype(np.float32)
    cos, sin = np.cos(ang), np.sin(ang)
    cos_t = np.tile(np.concatenate([cos, cos], axis=-1), (1, 2))
    sin_t = np.tile(np.concatenate([-sin, sin], axis=-1), (1, 2))
    return jnp.asarray(cos_t, F32), jnp.asarray(sin_t, F32)


def kernel(x_prompt, x_sample, cache_l0_attn_k, cache_l0_attn_v, state_l0_mlstm_C, state_l0_mlstm_n, state_l0_mlstm_m, cache_l1_na_k, cache_l1_na_v, cache_l1_swa_k, cache_l1_swa_v, c, c_ctx, norm_final, ada_w_l0, ada_b_l0, norm_l0, ffn1_in_l0, ffn1_out_l0, ffn2_in_l0, ffn2_out_l0, mix_in_l0, mix_out_l0, qk_norm_l0, gate_bias_l0, head_norm_l0, ada_w_l1, ada_b_l1, norm_l1, ffn1_in_l1, ffn1_out_l1, ffn2_in_l1, ffn2_out_l1, mix_in_l1, mix_out_l1, rpb_l1, sink_l1):
    nb, nl = x_prompt.shape[0], x_sample.shape[0]
    bf = lambda a: a.astype(BF16)

    cond8 = jnp.zeros((8, D_MODEL), F32).at[0].set(c_ctx).at[1:1 + nl].set(c)
    mods0 = _adaln(cond8, ada_w_l0, ada_b_l0).reshape(8 * ADA_CHUNKS, 1, D_MODEL)
    mods1 = _adaln(cond8, ada_w_l1, ada_b_l1).reshape(8 * ADA_CHUNKS, 1, D_MODEL)

    g0 = 2304
    gcols = lambda j: mix_in_l0[:, g0 + 8 * j:g0 + 8 * (j + 1)]
    gpad = jnp.zeros((D_MODEL, 128 - N_DIRHEAD), F32)
    w_gi, w_gf = jnp.concatenate([gcols(0), gcols(2)], axis=1), jnp.concatenate([gcols(1), gcols(3)], axis=1)
    w_even, w_ob = bf(mix_in_l0[:, :g0]), bf(mix_in_l0[:, g0 + 32:])
    w_g = bf(jnp.concatenate([w_gi, gpad, w_gf, gpad], axis=1))
    gb4 = gate_bias_l0.reshape(4, N_HEADS)
    b_gi, b_gf = jnp.concatenate([gb4[0], gb4[2]]), jnp.concatenate([gb4[1], gb4[3]])
    gbi = jnp.zeros((1, 128), F32).at[0, :N_DIRHEAD].set(b_gi)
    gbf = jnp.zeros((1, 128), F32).at[0, :N_DIRHEAD].set(b_gf)
    gbt = jnp.concatenate([b_gi, b_gf]).reshape(2 * N_DIRHEAD, 1)
    qg = jnp.tile(qk_norm_l0[0], N_HEADS).reshape(1, 512)
    kg = jnp.tile(qk_norm_l0[1], 2).reshape(1, 128)
    head_g = head_norm_l0.reshape(1, 512)
    grp = np.arange(256) // DH
    bd = jnp.asarray((grp[:, None] == grp[None, :]).astype(np.float32) / DH, dtype=BF16)
    rope_tabs = _rope_tables()
    w_odd = bf(mix_in_l1)
    rpb_flat = rpb_l1.reshape(-1)
    to_t = lambda a: a.reshape(a.shape[0], a.shape[1], -1).transpose(0, 2, 1)
    from_t = lambda a: a.reshape(a.shape[0], -1, DH, a.shape[2]).transpose(0, 3, 1, 2)

    xp = x_prompt.reshape(nb * CTX_LEN, D_MODEL)
    xs = x_sample.reshape(nl * LAT_LEN, D_MODEL)
    t_ctx, t_lat = xp.shape[0], xs.shape[0]
    streams = ((False, t_ctx, 0, nb, CTX_LEN), (True, t_lat, t_ctx // TOKEN_TILE, nl, LAT_LEN))
    ffn = functools.partial(_ffn, t_ctx=t_ctx, t_lat=t_lat)

    x = ffn((xp, xs), mods0, 0, norm_l0[0], ffn1_in_l0, ffn1_out_l0)
    ya, yb = {}, {}
    for latent, t, blk0, n_batch, seq_len in streams:
        qa, kat, vat, qb, vb, ob, gi, gf, gt, kbt = _proj_even(
            x, t, blk0, mods0, latent, norm_l0[1], w_even, w_ob, w_g, gbi, gbf, gbt, qg, kg, bd, rope_tabs)
        if latent:
            extra = (to_t(cache_l0_attn_k), to_t(cache_l0_attn_v))
            m0 = state_l0_mlstm_m.reshape(nl, N_DIRHEAD)
            m0_lanes = jnp.zeros((nl, 8, 128), F32).at[:, 0, :N_DIRHEAD].set(m0)
            m0_rows = jnp.broadcast_to(m0[:, :, None], (nl, N_DIRHEAD, 128))
            n0_cols = jnp.broadcast_to(state_l0_mlstm_n[..., None], (nl, 2, N_HEADS, DH, 128))
            state = (state_l0_mlstm_C, n0_cols, m0_lanes, m0_rows)
        else:
            extra, state = None, None
            k0t, v0t = kat, vat
        ya[latent] = _attention(qa, kat, vat, n_batch=n_batch, q_len=seq_len, q_tile=256, group=4, extra=extra)
        ml = _mlstm(qb, kbt, vb, ob, gi, gf, gt, head_g, bd, n_batch=n_batch, seq_len=seq_len, state=state)
        yb[latent] = ml[0]
        if not latent:
            c0, n0, m0_pad = ml[1:]
    x = ffn(x, mods0, 6, norm_l0[2], ffn2_in_l0, ffn2_out_l0, mix=(ya[False], yb[False], ya[True], yb[True], mix_out_l0))

    x = ffn(x, mods1, 0, norm_l1[0], ffn1_in_l1, ffn1_out_l1)
    yc, yd = {}, {}
    for latent, t, blk0, n_batch, seq_len in streams:
        qc, kc, vc, qd, kd, vd = _proj_odd(x, t, blk0, mods1, latent, norm_l1[1], w_odd, rope_tabs)
        if latent:
            yc[latent] = _na_latent(qc, kc, vc, cache_l1_na_k.reshape(nl * CTX_LEN, 512),
                                    cache_l1_na_v.reshape(nl * CTX_LEN, 512), rpb_flat, nl)
            yd[latent] = _swa_latent(qd, kd, vd, to_t(cache_l1_swa_k), to_t(cache_l1_swa_v), sink_l1, nl, group=4)
        else:
            yc[latent] = _attention(qc, kc, vc, n_batch=n_batch, q_len=seq_len, q_tile=256, group=1)
            yd[latent] = _attention(qd, kd, vd, n_batch=n_batch, q_len=seq_len, q_tile=256, group=4, sink=sink_l1)
            kc1t, vc1t, kd1t, vd1t = kc, vc, kd, vd
    y_prompt, y_sample = ffn(x, mods1, 6, norm_l1[2], ffn2_in_l1, ffn2_out_l1,
                             mix=(yc[False], yd[False], yc[True], yd[True], mix_out_l1), final_g=norm_final)

    return (y_prompt.reshape(nb, CTX_LEN, D_MODEL), y_sample.reshape(nl, LAT_LEN, D_MODEL),
            from_t(k0t), from_t(v0t),
            c0, n0, m0_pad[:, :, 0].reshape(nb, 2, N_HEADS),
            from_t(kc1t), from_t(vc1t),
            from_t(kd1t), from_t(vd1t))
```

```python
import functools

import jax
import jax.numpy as jnp
import numpy as np
from jax import lax
from jax.experimental import pallas as pl
from jax.experimental.pallas import tpu as pltpu

F32 = jnp.float32
BF16 = jnp.bfloat16

D_MODEL = 1024
DH = 64
D_FF = 2816
ADA_CHUNKS = 9
GRID_W = 64
LAT_LEN = 1024
CTX_LEN = 256
N_HEADS = 8
NA_KH = 8
NA_KW = 16
SWA_WIN = 128
ROPE_THETA = 10000.0
ROPE_FREQS = DH // 4
ATTN_SCALE = DH ** -0.5
NEG_INF = -1e30
EPS = 1e-6

TOKEN_TILE = 512
MLSTM_T = 256
FFN_CHUNK = 256
VMEM_LIMIT = 56 * 1024 * 1024
FFN_VMEM_LIMIT = 60 * 1024 * 1024
W_TILE = (256, 512)
W_SLOTS = 6


def _params(sem, vmem=VMEM_LIMIT):
    return pltpu.CompilerParams(dimension_semantics=sem, vmem_limit_bytes=vmem)


def _dot(a, b):
    return jnp.dot(a.astype(BF16), b.astype(BF16), preferred_element_type=F32)


def _dot_nt(a, b):
    return lax.dot_general(a.astype(BF16), b.astype(BF16), (((1,), (1,)), ((), ())),
                           preferred_element_type=F32)


def _split3(x):
    hi = x.astype(BF16)
    r1 = x - hi.astype(F32)
    mid = r1.astype(BF16)
    lo = (r1 - mid.astype(F32)).astype(BF16)
    return hi, mid, lo


def _dot_exact_rhs(a_bf16, x):
    hi, mid, lo = _split3(x)
    f = lambda p: jnp.dot(a_bf16, p, preferred_element_type=F32)
    return f(hi) + f(mid) + f(lo)


def _dot_exact_lhs(x, b_bf16):
    hi, mid, lo = _split3(x)
    f = lambda p: jnp.dot(p, b_bf16, preferred_element_type=F32)
    return f(hi) + f(mid) + f(lo)


def _silu(x):
    return x * jax.nn.sigmoid(x)


def _log_sigmoid(x):
    return jnp.minimum(x, 0.0) - jnp.log1p(jnp.exp(-jnp.abs(x)))


def _rms(x, g):
    return x * lax.rsqrt(jnp.mean(x * x, axis=-1, keepdims=True) + EPS) * g


def _head_mean_sq(x, bd_ref):
    n = x.shape[-1]
    sq = x * x
    hi = sq.astype(BF16)
    lo = (sq - hi.astype(F32)).astype(BF16)
    w = min(n, bd_ref.shape[0])
    bd = bd_ref[:w, :w]
    if 2 * w <= bd_ref.shape[0]:
        return jnp.dot(jnp.concatenate([hi, lo], axis=1), jnp.concatenate([bd, bd], axis=0),
                       preferred_element_type=F32)
    f = lambda p: jnp.dot(p, bd, preferred_element_type=F32)
    parts = [f(hi[:, c:c + w]) + f(lo[:, c:c + w]) for c in range(0, n, w)]
    return parts[0] if len(parts) == 1 else jnp.concatenate(parts, axis=1)


def _rope(x, cos, sin_signed):
    n = x.shape[-1]
    lane = lax.broadcasted_iota(jnp.int32, x.shape, 1)
    first_half = (lane % DH) < (DH // 2)
    partner = jnp.where(first_half, pltpu.roll(x, n - DH // 2, 1), pltpu.roll(x, DH // 2, 1))
    spread = lambda tab: tab if n == 128 else jnp.concatenate([tab] * (n // 128), axis=1)
    return x * spread(cos) + partner * spread(sin_signed)


def _adaln_kernel(c_ref, w_ref, b_ref, o_ref):
    s = _silu(c_ref[...])
    o_ref[...] = _dot(s, w_ref[...]) + b_ref[...]


def _adaln(cond8, w, b):
    n = w.shape[1]
    tn = 1536
    return pl.pallas_call(
        _adaln_kernel,
        out_shape=jax.ShapeDtypeStruct((8, n), F32),
        grid=(n // tn,),
        in_specs=[pl.BlockSpec((8, D_MODEL), lambda j: (0, 0)),
                  pl.BlockSpec((D_MODEL, tn), lambda j: (0, j)),
                  pl.BlockSpec((1, tn), lambda j: (0, j))],
        out_specs=pl.BlockSpec((8, tn), lambda j: (0, j)),
        compiler_params=_params(("arbitrary",)),
        name="adaln",
    )(cond8, w, b.reshape(1, n))


def _const_spec(shape):
    nd = len(shape)
    return pl.BlockSpec(shape, lambda i, _n=nd: (0,) * _n, pipeline_mode=pl.Buffered(1))


def _mod_spec(chunk, rowfn):
    return pl.BlockSpec((1, 1, D_MODEL), lambda i: (rowfn(i) * ADA_CHUNKS + chunk, 0, 0))


def _rowfn(latent):
    if latent:
        return lambda i: 1 + (i * TOKEN_TILE) // LAT_LEN
    return lambda i: 0


def _tok_spec(width, blk0=0):
    return pl.BlockSpec((TOKEN_TILE, width), lambda i: (i + blk0, 0))


def _seq_len(latent):
    return LAT_LEN if latent else CTX_LEN


def _transposed_out(t, latent, rows):
    seq = _seq_len(latent)
    shape = jax.ShapeDtypeStruct((t // seq, rows, seq), F32)
    if seq <= TOKEN_TILE:
        per = TOKEN_TILE // seq
        return shape, pl.BlockSpec((per, rows, seq), lambda i: (i, 0, 0))
    per = seq // TOKEN_TILE
    return shape, pl.BlockSpec((1, rows, TOKEN_TILE), lambda i: (i // per, 0, i % per))


def _store_transposed(ref, x):
    n, _, w = ref.shape
    for j in range(n):
        ref[j] = x[j * w:(j + 1) * w].T


def _dot_colsT(w_cols, h):
    return lax.dot_general(w_cols, h, (((0,), (1,)), ((), ())), preferred_element_type=F32)


def _store_proj_cols_transposed(refs, w_cols, h):
    n, _, w = refs[0].shape
    for j in range(n):
        out_t = _dot_colsT(w_cols, h[j * w:(j + 1) * w])
        r0 = 0
        for ref in refs:
            ref[j] = out_t[r0:r0 + ref.shape[1]]
            r0 += ref.shape[1]


def _modnorm(x, ng_ref, sh_ref, sc_ref):
    return _rms(x, ng_ref[...] * (1.0 + sc_ref[0])) + sh_ref[0]


def _load_weights_bf16(pairs, stage, sem):
    tr, tc = W_TILE
    tiles = [(src, dst, r, c) for src, dst in pairs
             for r in range(0, src.shape[0], tr) for c in range(0, src.shape[1], tc)]

    def copy(k):
        src, _, r, c = tiles[k]
        return pltpu.make_async_copy(src.at[r:r + tr, c:c + tc], stage.at[k % W_SLOTS], sem.at[k % W_SLOTS])

    for k in range(min(W_SLOTS, len(tiles))):
        copy(k).start()
    for k, (_, dst, r, c) in enumerate(tiles):
        copy(k).wait()
        dst[r:r + tr, c:c + tc] = stage[k % W_SLOTS].astype(BF16)
        if k + W_SLOTS < len(tiles):
            copy(k + W_SLOTS).start()


def _ffn_kernel(*refs, n_ctx, first, has_mix, last):
    it = iter(refs)
    xs = (next(it), next(it)) if first else (next(it),)
    if has_mix:
        yac_ref, ybc_ref, yal_ref, ybl_ref, wo_hbm, g2_ref = (next(it) for _ in range(6))
    ng_ref, sh_ref, sc_ref, g_ref, win_hbm, wout_hbm = (next(it) for _ in range(6))
    if last:
        nf_ref = next(it)
    outs = (next(it), next(it)) if last else (next(it),)
    win_s, wout_s = next(it), next(it)
    if has_mix:
        wo_s = next(it)
    stage, sem = next(it), next(it)

    step = pl.program_id(0)
    is_lat = step >= n_ctx

    @pl.when(step == 0)
    def _():
        pairs = [(win_hbm, win_s), (wout_hbm, wout_s)]
        if has_mix:
            pairs.insert(0, (wo_hbm, wo_s))
        _load_weights_bf16(pairs, stage, sem)

    pick = lambda c_ref, l_ref: jnp.where(is_lat, l_ref[...], c_ref[...])
    x = pick(*xs) if first else xs[0][...]
    if has_mix:
        half = wo_s.shape[0] // 2
        mo = _dot(pick(yac_ref, yal_ref), wo_s[:half, :]) + _dot(pick(ybc_ref, ybl_ref), wo_s[half:, :])
        x = x + g2_ref[0] * mo
    h = _modnorm(x, ng_ref, sh_ref, sc_ref).astype(BF16)
    acc = None
    for lo in range(0, D_FF, FFN_CHUNK):
        hi = min(lo + FFN_CHUNK, D_FF)
        g = _dot(h, win_s[:, lo:hi])
        u = _dot(h, win_s[:, D_FF + lo:D_FF + hi])
        part = _dot(_silu(g) * u, wout_s[lo:hi, :])
        acc = part if acc is None else acc + part
    y = x + (0.5 * g_ref[0]) * acc
    if last:
        y = _rms(y, nf_ref[...])
        oc_ref, ol_ref = outs

        @pl.when(jnp.logical_not(is_lat))
        def _():
            oc_ref[...] = y
            ol_ref[...] = jnp.zeros_like(ol_ref)

        @pl.when(is_lat)
        def _():
            ol_ref[...] = y
    else:
        outs[0][...] = y


def _ffn(x, mods, mod_base, ng, w_in, w_out, *, t_ctx, t_lat, mix=None, final_g=None):
    tile = TOKEN_TILE
    n_ctx, n_lat = t_ctx // tile, t_lat // tile
    first, last = isinstance(x, tuple), final_g is not None
    ctx_map = lambda i: (jnp.minimum(i, n_ctx - 1), 0)
    lat_map = lambda i: (jnp.maximum(i - n_ctx, 0), 0)
    rowfn = lambda i: jnp.where(i < n_ctx, 0, 1 + ((i - n_ctx) * tile) // LAT_LEN)
    two = lambda wd: [pl.BlockSpec((tile, wd), ctx_map), pl.BlockSpec((tile, wd), lat_map)]
    one = pl.BlockSpec((tile, D_MODEL), lambda i: (i, 0))
    hbm = pl.BlockSpec(memory_space=pl.ANY)
    args = list(x) if first else [x]
    specs = two(D_MODEL) if first else [one]
    scratch = [pltpu.VMEM(w_in.shape, BF16), pltpu.VMEM(w_out.shape, BF16)]
    if mix is not None:
        yac, ybc, yal, ybl, wo = mix
        args += [yac, ybc, yal, ybl, wo, mods]
        mw = yac.shape[1]
        specs += [pl.BlockSpec((tile, mw), ctx_map)] * 2 + [pl.BlockSpec((tile, mw), lat_map)] * 2
        specs += [hbm, _mod_spec(mod_base - 1, rowfn)]
        scratch.append(pltpu.VMEM(wo.shape, BF16))
    args += [ng.reshape(1, D_MODEL), mods, mods, mods, w_in, w_out]
    specs += [_const_spec((1, D_MODEL)), _mod_spec(mod_base, rowfn), _mod_spec(mod_base + 1, rowfn),
              _mod_spec(mod_base + 2, rowfn), hbm, hbm]
    if last:
        args.append(final_g.reshape(1, D_MODEL))
        specs.append(_const_spec((1, D_MODEL)))
        out_shape = [jax.ShapeDtypeStruct((t_ctx, D_MODEL), F32), jax.ShapeDtypeStruct((t_lat, D_MODEL), F32)]
        out_specs = two(D_MODEL)
    else:
        out_shape = jax.ShapeDtypeStruct((t_ctx + t_lat, D_MODEL), F32)
        out_specs = one
    scratch += [pltpu.VMEM((W_SLOTS,) + W_TILE, F32), pltpu.SemaphoreType.DMA((W_SLOTS,))]
    return pl.pallas_call(
        functools.partial(_ffn_kernel, n_ctx=n_ctx, first=first, has_mix=mix is not None, last=last),
        out_shape=out_shape,
        grid=(n_ctx + n_lat,),
        in_specs=specs,
        out_specs=out_specs,
        scratch_shapes=scratch,
        compiler_params=_params(("arbitrary",), FFN_VMEM_LIMIT),
        name="ffn",
    )(*args)


_EV_QA, _EV_KA, _EV_VA, _EV_QB, _EV_KB, _EV_VB, _EV_OB = 0, 512, 640, 768, 1280, 1792, 2304
N_DIRHEAD = 2 * N_HEADS


def _proj_even_kernel(*refs, rope):
    it = iter(refs)
    x_ref, ng_ref, sh_ref, sc_ref, w_ref, wob_ref, wg_ref, gbi_ref, gbf_ref, gbt_ref, qg_ref, kg_ref, bd_ref = (
        next(it) for _ in range(13))
    if rope:
        cos_ref, sin_ref = next(it), next(it)
    qa_ref, kat_ref, vat_ref, qb_ref, vb_ref, ob_ref, gi_ref, gf_ref, gt_ref, kbt_ref = (next(it) for _ in range(10))

    h = _modnorm(x_ref[...], ng_ref, sh_ref, sc_ref).astype(BF16)
    proj = lambda lo, hi: _dot(h, w_ref[:, lo:hi])

    qa = proj(_EV_QA, _EV_KA)
    qa = qa * lax.rsqrt(_head_mean_sq(qa, bd_ref) + EPS) * qg_ref[...]
    ka = proj(_EV_KA, _EV_VA)
    ka = ka * lax.rsqrt(_head_mean_sq(ka, bd_ref) + EPS) * kg_ref[...]
    if rope:
        qa = _rope(qa, cos_ref[...], sin_ref[...])
        ka = _rope(ka, cos_ref[...], sin_ref[...])
    qa_ref[...] = qa * ATTN_SCALE
    _store_transposed(kat_ref, ka)
    _store_proj_cols_transposed((vat_ref,), w_ref[:, _EV_VA:_EV_QB], h)
    qb_ref[...] = proj(_EV_QB, _EV_KB)
    vb_ref[...] = proj(_EV_VB, _EV_OB)
    ob_ref[...] = _dot(h, wob_ref[...])
    gates = _dot(h, wg_ref[...])
    gi_ref[...] = gates[:, :128] + gbi_ref[...]
    gf_ref[...] = gates[:, 128:] + gbf_ref[...]
    for j in range(TOKEN_TILE // MLSTM_T):
        ht = h[j * MLSTM_T:(j + 1) * MLSTM_T]
        gates_t = [_dot_colsT(wg_ref[:, c0:c0 + N_DIRHEAD], ht) for c0 in (0, 128)]
        gt_ref[j] = jnp.concatenate(gates_t, axis=0) + gbt_ref[...]
        kbt_ref[j] = _dot_colsT(w_ref[:, _EV_KB:_EV_VB], ht) * ATTN_SCALE


def _proj_even(x, t, blk0, mods, latent, ng, w, w_ob, w_g, gbi, gbf, gbt, qg, kg, bd, rope_tabs):
    rowfn = _rowfn(latent)
    args = [x, ng.reshape(1, D_MODEL), mods, mods, w, w_ob, w_g, gbi, gbf, gbt, qg, kg, bd]
    specs = [_tok_spec(D_MODEL, blk0), _const_spec((1, D_MODEL)), _mod_spec(3, rowfn), _mod_spec(4, rowfn)]
    specs += [_const_spec(a.shape) for a in args[4:]]
    if latent:
        nblk = LAT_LEN // TOKEN_TILE
        args += list(rope_tabs)
        specs += [pl.BlockSpec((TOKEN_TILE, 128), lambda i: (i % nblk, 0))] * 2
    widths = (512, 512, 512, 512, 128, 128)
    out_shape = [jax.ShapeDtypeStruct((t, wd), F32) for wd in widths]
    out_specs = [_tok_spec(wd) for wd in widths]
    for pos in (1, 2):
        shape, spec = _transposed_out(t, latent, 128)
        out_shape.insert(pos, shape)
        out_specs.insert(pos, spec)
    nj = TOKEN_TILE // MLSTM_T
    for rows in (2 * N_DIRHEAD, 512):
        out_shape.append(jax.ShapeDtypeStruct((t // MLSTM_T, rows, MLSTM_T), F32))
        out_specs.append(pl.BlockSpec((nj, rows, MLSTM_T), lambda i: (i, 0, 0)))
    return pl.pallas_call(
        functools.partial(_proj_even_kernel, rope=latent),
        out_shape=out_shape, grid=(t // TOKEN_TILE,), in_specs=specs, out_specs=out_specs,
        compiler_params=_params(("arbitrary",)), name="proj_even",
    )(*args)


def _proj_odd_kernel(*refs, rope):
    it = iter(refs)
    x_ref, ng_ref, sh_ref, sc_ref, w_ref = (next(it) for _ in range(5))
    if rope:
        cos_ref, sin_ref = next(it), next(it)
    qc_ref, kc_ref, vc_ref, qd_ref, kd_ref, vd_ref = (next(it) for _ in range(6))

    h = _modnorm(x_ref[...], ng_ref, sh_ref, sc_ref).astype(BF16)
    proj = lambda lo, hi: _dot(h, w_ref[:, lo:hi])
    qc_ref[...] = proj(0, 512) * ATTN_SCALE
    if rope:
        kc_ref[...] = proj(512, 1024)
        vc_ref[...] = proj(1024, 1536)
    else:
        _store_proj_cols_transposed((kc_ref, vc_ref), w_ref[:, 512:1536], h)
    qd = proj(1536, 2048)
    if rope:
        qd = _rope(qd, cos_ref[...], sin_ref[...])
        _store_transposed(kd_ref, _rope(proj(2048, 2176), cos_ref[...], sin_ref[...]))
        _store_proj_cols_transposed((vd_ref,), w_ref[:, 2176:2304], h)
    else:
        _store_proj_cols_transposed((kd_ref, vd_ref), w_ref[:, 2048:2304], h)
    qd_ref[...] = qd * ATTN_SCALE


def _proj_odd(x, t, blk0, mods, latent, ng, w, rope_tabs):
    rowfn = _rowfn(latent)
    args = [x, ng.reshape(1, D_MODEL), mods, mods, w]
    specs = [_tok_spec(D_MODEL, blk0), _const_spec((1, D_MODEL)), _mod_spec(3, rowfn), _mod_spec(4, rowfn),
             _const_spec(w.shape)]
    if latent:
        nblk = LAT_LEN // TOKEN_TILE
        args += list(rope_tabs)
        specs += [pl.BlockSpec((TOKEN_TILE, 128), lambda i: (i % nblk, 0))] * 2
    out_shape = [jax.ShapeDtypeStruct((t, 512), F32)] * 4
    out_specs = [_tok_spec(512)] * 4
    if not latent:
        out_shape[1], out_specs[1] = _transposed_out(t, latent, 512)
        out_shape[2], out_specs[2] = out_shape[1], out_specs[1]
    for _ in range(2):
        shape, spec = _transposed_out(t, latent, 128)
        out_shape.append(shape)
        out_specs.append(spec)
    return pl.pallas_call(
        functools.partial(_proj_odd_kernel, rope=latent),
        out_shape=out_shape,
        grid=(t // TOKEN_TILE,), in_specs=specs, out_specs=out_specs,
        compiler_params=_params(("arbitrary",)), name="proj_odd",
    )(*args)


def _attn_kernel(*refs, group, has_extra, has_sink, spb):
    it = iter(refs)
    q_ref, k_ref, v_ref = next(it), next(it), next(it)
    if has_extra:
        ke_ref, ve_ref = next(it), next(it)
    if has_sink:
        sink_ref = next(it)
    o_ref = next(it)
    tq = q_ref.shape[0] // spb
    first = lax.broadcasted_iota(jnp.int32, (tq, 128), 1) < DH

    def kv_operands(kt_ref, vt_ref, smp, kv):
        kt = kt_ref[smp, kv * DH:(kv + 1) * DH, :].astype(BF16)
        vt = vt_ref[smp, kv * DH:(kv + 1) * DH, :].astype(BF16)
        z = jnp.zeros_like(kt)
        ones = jnp.ones((128, kt.shape[1]), BF16)
        return ((jnp.concatenate([kt, z], axis=0), jnp.concatenate([z, kt], axis=0)),
                (jnp.concatenate([vt, z, ones], axis=0), jnp.concatenate([z, vt, ones], axis=0)))

    for smp in range(spb):
        qrows = slice(smp * tq, (smp + 1) * tq)
        cache = {}
        for j in range(N_HEADS // 2):
            ps = slice(128 * j, 128 * (j + 1))
            qb = q_ref[qrows, ps].astype(BF16)
            res, sink_den = [], []
            for half in range(2):
                kv = (2 * j + half) // group
                if kv not in cache:
                    cache[kv] = (kv_operands(k_ref, v_ref, smp, kv),
                                 kv_operands(ke_ref, ve_ref, smp, kv) if has_extra else None)
                (k_ops, v_ops), extra = cache[kv]
                s = jnp.dot(qb, k_ops[half], preferred_element_type=F32)
                m = jnp.max(s, axis=-1, keepdims=True)
                if has_extra:
                    se = jnp.dot(qb, extra[0][half], preferred_element_type=F32)
                    m = jnp.maximum(m, jnp.max(se, axis=-1, keepdims=True))
                if has_sink:
                    m = jnp.maximum(m, sink_ref[2 * j + half])
                    sink_den.append(jnp.exp(sink_ref[2 * j + half] - m))
                r = _dot_nt(jnp.exp(s - m), v_ops[half])
                if has_extra:
                    r = r + _dot_nt(jnp.exp(se - m), extra[1][half])
                res.append(r)
            num = jnp.where(first, res[0][:, :128], res[1][:, :128])
            den = jnp.where(first, res[0][:, 128:], res[1][:, 128:])
            if has_sink:
                den = den + jnp.where(first, sink_den[0], sink_den[1])
            o_ref[qrows, ps] = num / den


def _attention(q, k, v, *, n_batch, q_len, q_tile, group, extra=None, sink=None):
    nq = q_len // q_tile
    spb = next(n for n in (8, 4, 2, 1) if n_batch % n == 0) if nq == 1 else 1
    kv_spec = lambda a: pl.BlockSpec((spb,) + a.shape[1:], lambda b, j: (b, 0, 0))
    args = [q, k, v]
    specs = [pl.BlockSpec((spb * q_tile, 512), lambda b, j: (b * nq + j, 0)), kv_spec(k), kv_spec(v)]
    if extra is not None:
        args += list(extra)
        specs += [kv_spec(extra[0]), kv_spec(extra[1])]
    if sink is not None:
        args.append(sink)
        specs.append(pl.BlockSpec(memory_space=pltpu.SMEM))
    return pl.pallas_call(
        functools.partial(_attn_kernel, group=group, has_extra=extra is not None, has_sink=sink is not None, spb=spb),
        out_shape=jax.ShapeDtypeStruct(q.shape, F32),
        grid=(n_batch // spb, nq), in_specs=specs,
        out_specs=pl.BlockSpec((spb * q_tile, 512), lambda b, j: (b * nq + j, 0)),
        compiler_params=_params(("arbitrary", "arbitrary")), name="attn",
    )(*args)


LOG2E = 1.4426950408889634


def _cummax_time(x, reverse):
    n = x.shape[0]
    row = lax.broadcasted_iota(jnp.int32, x.shape, 0)
    k = 1
    while k < n:
        if reverse:
            shifted = jnp.where(row < n - k, pltpu.roll(x, n - k, 0), -jnp.inf)
        else:
            shifted = jnp.where(row >= k, pltpu.roll(x, k, 0), -jnp.inf)
        x = jnp.maximum(x, shifted)
        k *= 2
    return x


def _mlstm_kernel(*refs, n_chunks, carry, spb):
    it = iter(refs)
    q_ref, kt_ref, v_ref, ob_ref, gi_ref, gf_ref, gt_ref, hg_ref, bd_ref = (next(it) for _ in range(9))
    if carry:
        c0_ref, n0_ref, m0r_ref, m0c_ref = (next(it) for _ in range(4))
    y_ref = next(it)
    if not carry:
        c_ref, n_ref, m_ref = next(it), next(it), next(it)
    hf_s, hb_s = next(it), next(it)
    if carry:
        st_s, mrow_s, mcol_s = next(it), next(it), next(it)
    T = MLSTM_T
    step = pl.program_id(1)

    if carry:
        @pl.when(step == 0)
        def _():
            z = jnp.zeros((DH, DH), F32)
            for d in range(2):
                for j in range(N_HEADS // 2):
                    ca, cb = c0_ref[0, d, 2 * j], c0_ref[0, d, 2 * j + 1]
                    na, nb = n0_ref[0, d, 2 * j][:, :DH], n0_ref[0, d, 2 * j + 1][:, :DH]
                    st_s[d, j, 0:DH, :] = jnp.concatenate([ca, z, na, z], axis=1)
                    st_s[d, j, DH:2 * DH, :] = jnp.concatenate([z, cb, z, nb], axis=1)
            mrow_s[...] = m0r_ref[0]
            mcol_s[...] = m0c_ref[0]

    row = lax.broadcasted_iota(jnp.int32, (T, T), 0)
    col = lax.broadcasted_iota(jnp.int32, (T, T), 1)
    lower = row >= col
    upper = row <= col
    lower_m = jnp.where(lower, 1.0, 0.0).astype(BF16)
    upper_m = jnp.where(upper, 1.0, 0.0).astype(BF16)
    lane = lax.broadcasted_iota(jnp.int32, (T, 128), 1)
    first = lane < DH
    lane_c = lax.broadcasted_iota(jnp.int32, (DH, 256), 1) % 128
    zeros_kt = jnp.zeros((DH, T), BF16)
    ones_v = jnp.ones((T, 128), BF16)

    for smp, d in ((s_, d_) for s_ in range(spb) for d_ in range(2)):
        chunk = (step if d == 0 else n_chunks - 1 - step) if carry else smp
        rows = pl.ds(pl.multiple_of(chunk * T, T), T)
        hd0 = N_HEADS * d
        mask = lower if d == 0 else upper

        if carry or d == 0:
            gi = gi_ref[rows, :]
            lf = _log_sigmoid(gf_ref[rows, :])
            prefix = _dot_exact_rhs(lower_m, lf)
        b_cols = prefix if d == 0 else prefix[T - 1:T, :] - prefix + lf
        a_cols = gi - b_cols
        dmax = b_cols + _cummax_time(a_cols, reverse=d == 1)
        if carry:
            inter = b_cols + mrow_s[0:1, :]
            mt = jnp.maximum(inter, dmax)
            w_inter = jnp.exp(inter - mt)
        else:
            mt = jnp.maximum(b_cols, dmax)
        c2 = (b_cols - mt) * LOG2E
        einv = jnp.exp(-mt)

        gt = gt_ref[chunk]
        gi_t = gt[hd0:hd0 + N_HEADS, :]
        gf_t = gt[N_DIRHEAD + hd0:N_DIRHEAD + hd0 + N_HEADS, :]
        lf_t = _log_sigmoid(gf_t)
        b_rows = _dot_exact_lhs(lf_t, upper_m)
        if d == 1:
            b_rows = b_rows[:, T - 1:T] - b_rows + lf_t
        a_rows = gi_t - b_rows
        a2 = a_rows * LOG2E
        b_last = b_rows[:, T - 1:T] if d == 0 else b_rows[:, 0:1]
        dec_rows = b_last + a_rows
        m_new = jnp.max(dec_rows, axis=-1, keepdims=True)
        if carry:
            m_prev = mcol_s[hd0:hd0 + N_HEADS, 0:1]
            m_new = jnp.maximum(m_new, b_last + m_prev)
            wc = jnp.exp(b_last + m_prev - m_new)
        else:
            m_new = jnp.maximum(m_new, b_last)
        ws_rows = jnp.exp(dec_rows - m_new)

        h_s = hf_s if d == 0 else hb_s
        for j in range(N_HEADS // 2):
            ps = slice(128 * j, 128 * (j + 1))
            q_pair = q_ref[rows, ps].astype(BF16)
            v_aug = jnp.concatenate([v_ref[rows, ps].astype(BF16), ones_v], axis=1)
            kt = (kt_ref[chunk, 128 * j:128 * j + DH, :], kt_ref[chunk, 128 * j + DH:128 * (j + 1), :])
            lhs = []
            for half in range(2):
                hd = 2 * j + half
                kt0 = jnp.concatenate([kt[0].astype(BF16), zeros_kt] if half == 0 else [zeros_kt, kt[1].astype(BF16)], axis=0)
                qk = jnp.dot(q_pair, kt0, preferred_element_type=F32)
                e = c2[:, hd0 + hd:hd0 + hd + 1] + a2[hd:hd + 1, :]
                lhs.append((jnp.exp2(jnp.where(mask, e, -jnp.inf)) * qk).astype(BF16))
            for half in range(2):
                lhs.append((kt[half] * ws_rows[2 * j + half:2 * j + half + 1, :]).astype(BF16))
            both = jnp.dot(jnp.concatenate(lhs, axis=0), v_aug, preferred_element_type=F32)
            res = (both[0:T], both[T:2 * T])
            cu = (both[2 * T:2 * T + DH], both[2 * T + DH:2 * T + 2 * DH])
            ra, rb = hd0 + 2 * j, hd0 + 2 * j + 1
            num = jnp.where(first, res[0][:, :128], res[1][:, :128])
            den = jnp.where(first, res[0][:, 128:], res[1][:, 128:])
            if carry:
                old = st_s[d, j]
                qs = _dot(q_pair, old)
                wi = jnp.where(first, w_inter[:, ra:ra + 1], w_inter[:, rb:rb + 1])
                num = num + wi * qs[:, :128]
                den = den + wi * qs[:, 128:]
            floor = jnp.where(first, einv[:, ra:ra + 1], einv[:, rb:rb + 1])
            h_s[rows, ps] = num / jnp.maximum(jnp.abs(den), floor)
            if carry:
                st_s[d, j, 0:DH, :] = jnp.where(lane_c < DH, cu[0], 0.0) + wc[2 * j:2 * j + 1, :] * old[0:DH]
                st_s[d, j, DH:2 * DH, :] = jnp.where(lane_c >= DH, cu[1], 0.0) + wc[2 * j + 1:2 * j + 2, :] * old[DH:2 * DH]
            else:
                c_ref[smp, d, 2 * j] = cu[0][:, 0:DH]
                c_ref[smp, d, 2 * j + 1] = cu[1][:, DH:2 * DH]

        if carry:
            b_last_l = b_cols[T - 1:T, :] if d == 0 else b_cols[0:1, :]
            m_prev_l = mrow_s[0:1, :]
            m_new_l = jnp.maximum(jnp.max(b_last_l + a_cols, axis=0, keepdims=True), b_last_l + m_prev_l)
            lane_r = lax.broadcasted_iota(jnp.int32, (1, 128), 1)
            mine = (lane_r >= hd0) & (lane_r < hd0 + N_HEADS)
            mrow_s[0:1, :] = jnp.where(mine, m_new_l, m_prev_l)
            mcol_s[hd0:hd0 + N_HEADS, :] = jnp.broadcast_to(m_new, (N_HEADS, 128))
        else:
            for hd in range(N_HEADS):
                n_hd = _dot_nt(ws_rows, kt_ref[chunk, hd * DH:(hd + 1) * DH, :])
                n_ref[smp, d, hd:hd + 1, :] = n_hd[hd:hd + 1, :]
            m_ref[smp, hd0:hd0 + N_HEADS, :] = jnp.broadcast_to(m_new, (N_HEADS, 128))

    @pl.when(step == n_chunks - 1)
    def _():
        hsum = hf_s[...] + hb_s[...]
        y = hsum * lax.rsqrt(_head_mean_sq(hsum, bd_ref) + EPS) * hg_ref[...]
        y_ref[...] = jax.nn.sigmoid(ob_ref[...]) * y


def _mlstm(qb, kbt, vb, ob, gi, gf, gt, head_g, bd, *, n_batch, seq_len, state=None):
    n_chunks = seq_len // MLSTM_T
    carry = state is not None
    assert carry or n_chunks == 1
    spb = 1
    seq = lambda wd: pl.BlockSpec((spb * seq_len, wd), lambda b, c: (b, 0))
    per_chunk = lambda r: pl.BlockSpec((spb * n_chunks, r, MLSTM_T), lambda b, c: (b, 0, 0))
    args = [qb, kbt, vb, ob, gi, gf, gt, head_g, bd]
    specs = [seq(512), per_chunk(512), seq(512), seq(512), seq(128), seq(128), per_chunk(2 * N_DIRHEAD),
             pl.BlockSpec((1, 512), lambda b, c: (0, 0)), pl.BlockSpec(bd.shape, lambda b, c: (0, 0))]
    out_shape = [jax.ShapeDtypeStruct((n_batch * seq_len, 512), F32)]
    out_specs = [seq(512)]
    scratch = [pltpu.VMEM((spb * seq_len, 512), F32), pltpu.VMEM((spb * seq_len, 512), F32)]
    if carry:
        args += list(state)
        specs += [pl.BlockSpec((1, 2, N_HEADS, DH, DH), lambda b, c: (b, 0, 0, 0, 0)),
                  pl.BlockSpec((1, 2, N_HEADS, DH, 128), lambda b, c: (b, 0, 0, 0, 0)),
                  pl.BlockSpec((1, 8, 128), lambda b, c: (b, 0, 0)),
                  pl.BlockSpec((1, N_DIRHEAD, 128), lambda b, c: (b, 0, 0))]
        scratch += [pltpu.VMEM((2, N_HEADS // 2, 128, 256), F32), pltpu.VMEM((8, 128), F32),
                    pltpu.VMEM((N_DIRHEAD, 128), F32)]
    else:
        out_shape += [jax.ShapeDtypeStruct((n_batch, 2, N_HEADS, DH, DH), F32),
                      jax.ShapeDtypeStruct((n_batch, 2, N_HEADS, DH), F32),
                      jax.ShapeDtypeStruct((n_batch, N_DIRHEAD, 128), F32)]
        out_specs += [pl.BlockSpec((spb, 2, N_HEADS, DH, DH), lambda b, c: (b, 0, 0, 0, 0)),
                      pl.BlockSpec((spb, 2, N_HEADS, DH), lambda b, c: (b, 0, 0, 0)),
                      pl.BlockSpec((spb, N_DIRHEAD, 128), lambda b, c: (b, 0, 0))]
    return pl.pallas_call(
        functools.partial(_mlstm_kernel, n_chunks=n_chunks, carry=carry, spb=spb),
        out_shape=out_shape, grid=(n_batch // spb, n_chunks), in_specs=specs, out_specs=out_specs,
        scratch_shapes=scratch,
        compiler_params=_params(("arbitrary", "arbitrary")), name="mlstm",
    )(*args)


N_ROWS = LAT_LEN // GRID_W
N_DY = 2 * NA_KH - 1
N_DX = 2 * NA_KW - 1


NA_QROWS = 4
NA_KROWS = 12
NA_GROUPS = N_ROWS // NA_QROWS


def _na_kernel(q_ref, k_ref, v_ref, ke_ref, ve_ref, rpb_ref, o_ref, tile_s, slab_s):
    pair = pl.program_id(1)
    qn, kn = NA_QROWS * GRID_W, NA_KROWS * GRID_W

    @pl.when(pl.program_id(0) == 0)
    def _():
        qc = lax.broadcasted_iota(jnp.int32, (GRID_W, GRID_W), 0)
        kc = lax.broadcasted_iota(jnp.int32, (GRID_W, GRID_W), 1)
        start = jnp.clip(qc - NA_KW // 2, 0, GRID_W - NA_KW)
        in_win = (kc >= start) & (kc < start + NA_KW)
        dx = kc - qc + (NA_KW - 1)
        blocked = jnp.full((GRID_W, GRID_W), NEG_INF, F32)
        for half in range(2):
            head = 2 * pair + half

            def build_tile(dy, carry):
                tile = blocked
                for j in range(N_DX):
                    tile = jnp.where(dx == j, rpb_ref[(head * N_DY + dy) * N_DX + j], tile)
                tile_s[half * N_DY + dy] = jnp.where(in_win, tile, NEG_INF)
                return carry

            lax.fori_loop(0, N_DY, build_tile, 0)
            for kind in range(3):
                idx = (pair * 2 + half) * 3 + kind
                for i in range(NA_QROWS):
                    for k in range(NA_KROWS):
                        if kind == 0:
                            ok, dy = k < NA_KH, k - i + NA_KH - 1
                        elif kind == 1:
                            ok, dy = i <= k < i + NA_KH, k - i + NA_KH // 2 - 1
                        else:
                            ok, dy = k >= NA_KROWS - NA_KH, k - i - 1
                        slab_s[idx, i * GRID_W:(i + 1) * GRID_W, k * GRID_W:(k + 1) * GRID_W] = (
                            tile_s[half * N_DY + dy] if ok else blocked)

    first = lax.broadcasted_iota(jnp.int32, (qn, 128), 1) < DH
    ones_k = jnp.ones((kn, 128), BF16)
    ke = ke_ref[...].astype(BF16)
    ve_aug = jnp.concatenate([ve_ref[...].astype(BF16), jnp.ones((ke.shape[0], 128), BF16)], axis=1)

    def group_body(g, carry):
        k0 = jnp.where(g < NA_GROUPS // 2, 0, N_ROWS - NA_KROWS)
        kind = jnp.where(g == 0, 0, jnp.where(g == NA_GROUPS - 1, 2, 1))
        qrows = pl.ds(pl.multiple_of(g * qn, qn), qn)
        keys = pl.ds(pl.multiple_of(k0 * GRID_W, GRID_W), kn)
        q_pair = q_ref[qrows, :]
        k_pair = k_ref[keys, :].astype(BF16)
        v_aug = jnp.concatenate([v_ref[keys, :].astype(BF16), ones_k], axis=1)
        res = []
        for half in range(2):
            qm = jnp.where(first, q_pair, 0.0) if half == 0 else jnp.where(first, 0.0, q_pair)
            s = _dot_nt(qm, k_pair) + slab_s[(pair * 2 + half) * 3 + kind]
            se = _dot_nt(qm, ke)
            m = jnp.maximum(jnp.max(s, axis=-1, keepdims=True), jnp.max(se, axis=-1, keepdims=True))
            res.append(_dot(jnp.exp(s - m), v_aug) + _dot(jnp.exp(se - m), ve_aug))
        num = jnp.where(first, res[0][:, :128], res[1][:, :128])
        den = jnp.where(first, res[0][:, 128:], res[1][:, 128:])
        o_ref[qrows, :] = num / den
        return carry

    lax.fori_loop(0, NA_GROUPS, group_body, 0, unroll=True)


def _na_latent(q, k, v, k_ctx, v_ctx, rpb_flat, n_batch):
    seq = pl.BlockSpec((LAT_LEN, 128), lambda b, j: (b, j))
    ctx = pl.BlockSpec((CTX_LEN, 128), lambda b, j: (b, j))
    return pl.pallas_call(
        _na_kernel,
        out_shape=jax.ShapeDtypeStruct(q.shape, F32),
        grid=(n_batch, N_HEADS // 2),
        in_specs=[seq, seq, seq, ctx, ctx, pl.BlockSpec(memory_space=pltpu.SMEM)],
        out_specs=seq,
        scratch_shapes=[pltpu.VMEM((2 * N_DY, GRID_W, GRID_W), F32),
                        pltpu.VMEM((N_HEADS * 3, NA_QROWS * GRID_W, NA_KROWS * GRID_W), F32)],
        compiler_params=_params(("arbitrary", "arbitrary")), name="na_latent",
    )(q, k, v, k_ctx, v_ctx, rpb_flat)


SWA_QT = 256
SWA_SPAN = SWA_QT + 2 * SWA_WIN


def _swa_kernel(q_ref, kt_ref, vt_ref, ket_ref, vet_ref, sink_ref, o_ref):
    pair = pl.program_id(1)
    first = lax.broadcasted_iota(jnp.int32, (SWA_QT, 128), 1) < DH
    row = lax.broadcasted_iota(jnp.int32, (SWA_QT, SWA_SPAN), 0)
    col = lax.broadcasted_iota(jnp.int32, (SWA_QT, SWA_SPAN), 1)
    z = jnp.zeros((DH, SWA_SPAN), BF16)
    ones = jnp.ones((128, SWA_SPAN), BF16)
    ket, vet = ket_ref[0].astype(BF16), vet_ref[0].astype(BF16)
    ze = jnp.zeros_like(ket)
    ones_e = jnp.ones((128, ket.shape[1]), BF16)
    sinks = (sink_ref[2 * pair], sink_ref[2 * pair + 1])
    for g in range(LAT_LEN // SWA_QT):
        q0 = g * SWA_QT
        lo = min(max(q0 - SWA_WIN, 0), LAT_LEN - SWA_SPAN)
        near = jnp.abs((lo + col) - (q0 + row)) <= SWA_WIN
        qb = q_ref[q0:q0 + SWA_QT, :].astype(BF16)
        kt = kt_ref[0, :, lo:lo + SWA_SPAN].astype(BF16)
        vt = vt_ref[0, :, lo:lo + SWA_SPAN].astype(BF16)
        res, sink_den = [], []
        for half in range(2):
            k_op = jnp.concatenate([kt, z] if half == 0 else [z, kt], axis=0)
            v_op = jnp.concatenate([vt, z, ones] if half == 0 else [z, vt, ones], axis=0)
            ke_op = jnp.concatenate([ket, ze] if half == 0 else [ze, ket], axis=0)
            ve_op = jnp.concatenate([vet, ze, ones_e] if half == 0 else [ze, vet, ones_e], axis=0)
            s = jnp.where(near, jnp.dot(qb, k_op, preferred_element_type=F32), NEG_INF)
            se = jnp.dot(qb, ke_op, preferred_element_type=F32)
            m = jnp.maximum(jnp.max(s, axis=-1, keepdims=True), jnp.max(se, axis=-1, keepdims=True))
            m = jnp.maximum(m, sinks[half])
            sink_den.append(jnp.exp(sinks[half] - m))
            res.append(_dot_nt(jnp.exp(s - m), v_op) + _dot_nt(jnp.exp(se - m), ve_op))
        num = jnp.where(first, res[0][:, :128], res[1][:, :128])
        den = jnp.where(first, res[0][:, 128:], res[1][:, 128:]) + jnp.where(first, sink_den[0], sink_den[1])
        o_ref[q0:q0 + SWA_QT, :] = num / den


def _swa_latent(q, kt, vt, kt_ctx, vt_ctx, sink, n_batch, group):
    ppk = group // 2
    kv_spec = lambda a: pl.BlockSpec((1, DH, a.shape[2]), lambda b, j: (b, j // ppk, 0))
    return pl.pallas_call(
        _swa_kernel,
        out_shape=jax.ShapeDtypeStruct(q.shape, F32),
        grid=(n_batch, N_HEADS // 2),
        in_specs=[pl.BlockSpec((LAT_LEN, 128), lambda b, j: (b, j)),
                  kv_spec(kt), kv_spec(vt), kv_spec(kt_ctx), kv_spec(vt_ctx),
                  pl.BlockSpec(memory_space=pltpu.SMEM)],
        out_specs=pl.BlockSpec((LAT_LEN, 128), lambda b, j: (b, j)),
        compiler_params=_params(("arbitrary", "arbitrary")), name="swa_latent",
    )(q, kt, vt, kt_ctx, vt_ctx, sink)


def _rope_tables():
    t = np.arange(LAT_LEN)
    pos = np.stack([t // GRID_W, t % GRID_W], axis=-1).astype(np.float32)
    freqs = np.float32(ROPE_THETA) ** (-np.arange(ROPE_FREQS, dtype=np.float32) / np.float32(ROPE_FREQS))
    ang = (pos[:, :, None] * freqs).reshape(LAT_LEN, 2 * ROPE_FREQS).astype(np.float32)
    cos, sin = np.cos(ang), np.sin(ang)
    cos_t = np.tile(np.concatenate([cos, cos], axis=-1), (1, 2))
    sin_t = np.tile(np.concatenate([-sin, sin], axis=-1), (1, 2))
    return jnp.asarray(cos_t, F32), jnp.asarray(sin_t, F32)


def kernel(x_prompt, x_sample, cache_l0_attn_k, cache_l0_attn_v, state_l0_mlstm_C, state_l0_mlstm_n, state_l0_mlstm_m, cache_l1_na_k, cache_l1_na_v, cache_l1_swa_k, cache_l1_swa_v, c, c_ctx, norm_final, ada_w_l0, ada_b_l0, norm_l0, ffn1_in_l0, ffn1_out_l0, ffn2_in_l0, ffn2_out_l0, mix_in_l0, mix_out_l0, qk_norm_l0, gate_bias_l0, head_norm_l0, ada_w_l1, ada_b_l1, norm_l1, ffn1_in_l1, ffn1_out_l1, ffn2_in_l1, ffn2_out_l1, mix_in_l1, mix_out_l1, rpb_l1, sink_l1):
    nb, nl = x_prompt.shape[0], x_sample.shape[0]
    bf = lambda a: a.astype(BF16)

    cond8 = jnp.zeros((8, D_MODEL), F32).at[0].set(c_ctx).at[1:1 + nl].set(c)
    mods0 = _adaln(cond8, ada_w_l0, ada_b_l0).reshape(8 * ADA_CHUNKS, 1, D_MODEL)
    mods1 = _adaln(cond8, ada_w_l1, ada_b_l1).reshape(8 * ADA_CHUNKS, 1, D_MODEL)

    g0 = 2304
    gcols = lambda j: mix_in_l0[:, g0 + 8 * j:g0 + 8 * (j + 1)]
    gpad = jnp.zeros((D_MODEL, 128 - N_DIRHEAD), F32)
    w_gi, w_gf = jnp.concatenate([gcols(0), gcols(2)], axis=1), jnp.concatenate([gcols(1), gcols(3)], axis=1)
    w_even, w_ob = bf(mix_in_l0[:, :g0]), bf(mix_in_l0[:, g0 + 32:])
    w_g = bf(jnp.concatenate([w_gi, gpad, w_gf, gpad], axis=1))
    gb4 = gate_bias_l0.reshape(4, N_HEADS)
    b_gi, b_gf = jnp.concatenate([gb4[0], gb4[2]]), jnp.concatenate([gb4[1], gb4[3]])
    gbi = jnp.zeros((1, 128), F32).at[0, :N_DIRHEAD].set(b_gi)
    gbf = jnp.zeros((1, 128), F32).at[0, :N_DIRHEAD].set(b_gf)
    gbt = jnp.concatenate([b_gi, b_gf]).reshape(2 * N_DIRHEAD, 1)
    qg = jnp.tile(qk_norm_l0[0], N_HEADS).reshape(1, 512)
    kg = jnp.tile(qk_norm_l0[1], 2).reshape(1, 128)
    head_g = head_norm_l0.reshape(1, 512)
    grp = np.arange(256) // DH
    bd = jnp.asarray((grp[:, None] == grp[None, :]).astype(np.float32) / DH, dtype=BF16)
    rope_tabs = _rope_tables()
    w_odd = bf(mix_in_l1)
    rpb_flat = rpb_l1.reshape(-1)
    to_t = lambda a: a.reshape(a.shape[0], a.shape[1], -1).transpose(0, 2, 1)
    from_t = lambda a: a.reshape(a.shape[0], -1, DH, a.shape[2]).transpose(0, 3, 1, 2)

    xp = x_prompt.reshape(nb * CTX_LEN, D_MODEL)
    xs = x_sample.reshape(nl * LAT_LEN, D_MODEL)
    t_ctx, t_lat = xp.shape[0], xs.shape[0]
    streams = ((False, t_ctx, 0, nb, CTX_LEN), (True, t_lat, t_ctx // TOKEN_TILE, nl, LAT_LEN))
    ffn = functools.partial(_ffn, t_ctx=t_ctx, t_lat=t_lat)

    x = ffn((xp, xs), mods0, 0, norm_l0[0], ffn1_in_l0, ffn1_out_l0)
    ya, yb = {}, {}
    for latent, t, blk0, n_batch, seq_len in streams:
        qa, kat, vat, qb, vb, ob, gi, gf, gt, kbt = _proj_even(
            x, t, blk0, mods0, latent, norm_l0[1], w_even, w_ob, w_g, gbi, gbf, gbt, qg, kg, bd, rope_tabs)
        if latent:
            extra = (to_t(cache_l0_attn_k), to_t(cache_l0_attn_v))
            m0 = state_l0_mlstm_m.reshape(nl, N_DIRHEAD)
            m0_lanes = jnp.zeros((nl, 8, 128), F32).at[:, 0, :N_DIRHEAD].set(m0)
            m0_rows = jnp.broadcast_to(m0[:, :, None], (nl, N_DIRHEAD, 128))
            n0_cols = jnp.broadcast_to(state_l0_mlstm_n[..., None], (nl, 2, N_HEADS, DH, 128))
            state = (state_l0_mlstm_C, n0_cols, m0_lanes, m0_rows)
        else:
            extra, state = None, None
            k0t, v0t = kat, vat
        ya[latent] = _attention(qa, kat, vat, n_batch=n_batch, q_len=seq_len, q_tile=256, group=4, extra=extra)
        ml = _mlstm(qb, kbt, vb, ob, gi, gf, gt, head_g, bd, n_batch=n_batch, seq_len=seq_len, state=state)
        yb[latent] = ml[0]
        if not latent:
            c0, n0, m0_pad = ml[1:]
    x = ffn(x, mods0, 6, norm_l0[2], ffn2_in_l0, ffn2_out_l0, mix=(ya[False], yb[False], ya[True], yb[True], mix_out_l0))

    x = ffn(x, mods1, 0, norm_l1[0], ffn1_in_l1, ffn1_out_l1)
    yc, yd = {}, {}
    for latent, t, blk0, n_batch, seq_len in streams:
        qc, kc, vc, qd, kd, vd = _proj_odd(x, t, blk0, mods1, latent, norm_l1[1], w_odd, rope_tabs)
        if latent:
            yc[latent] = _na_latent(qc, kc, vc, cache_l1_na_k.reshape(nl * CTX_LEN, 512),
                                    cache_l1_na_v.reshape(nl * CTX_LEN, 512), rpb_flat, nl)
            yd[latent] = _swa_latent(qd, kd, vd, to_t(cache_l1_swa_k), to_t(cache_l1_swa_v), sink_l1, nl, group=4)
        else:
            yc[latent] = _attention(qc, kc, vc, n_batch=n_batch, q_len=seq_len, q_tile=256, group=1)
            yd[latent] = _attention(qd, kd, vd, n_batch=n_batch, q_len=seq_len, q_tile=256, group=4, sink=sink_l1)
            kc1t, vc1t, kd1t, vd1t = kc, vc, kd, vd
    y_prompt, y_sample = ffn(x, mods1, 6, norm_l1[2], ffn2_in_l1, ffn2_out_l1,
                             mix=(yc[False], yd[False], yc[True], yd[True], mix_out_l1), final_g=norm_final)

    return (y_prompt.reshape(nb, CTX_LEN, D_MODEL), y_sample.reshape(nl, LAT_LEN, D_MODEL),
            from_t(k0t), from_t(v0t),
            c0, n0, m0_pad[:, :, 0].reshape(nb, 2, N_HEADS),
            from_t(kc1t), from_t(vc1t),
            from_t(kd1t), from_t(vd1t))
```

```python
import functools

import jax
import jax.numpy as jnp
import numpy as np
from jax import lax
from jax.experimental import pallas as pl
from jax.experimental.pallas import tpu as pltpu

F32 = jnp.float32
BF16 = jnp.bfloat16

D_MODEL = 1024
DH = 64
D_FF = 2816
ADA_CHUNKS = 9
GRID_W = 64
LAT_LEN = 1024
CTX_LEN = 256
N_HEADS = 8
NA_KH = 8
NA_KW = 16
SWA_WIN = 128
ROPE_THETA = 10000.0
ROPE_FREQS = DH // 4
ATTN_SCALE = DH ** -0.5
NEG_INF = -1e30
EPS = 1e-6

TOKEN_TILE = 512
MLSTM_T = 256
FFN_CHUNK = 256
VMEM_LIMIT = 56 * 1024 * 1024
FFN_VMEM_LIMIT = 60 * 1024 * 1024
W_TILE = (256, 512)
W_SLOTS = 6


def _params(sem, vmem=VMEM_LIMIT):
    return pltpu.CompilerParams(dimension_semantics=sem, vmem_limit_bytes=vmem)


def _dot(a, b):
    return jnp.dot(a.astype(BF16), b.astype(BF16), preferred_element_type=F32)


def _dot_nt(a, b):
    return lax.dot_general(a.astype(BF16), b.astype(BF16), (((1,), (1,)), ((), ())),
                           preferred_element_type=F32)


def _split3(x):
    hi = x.astype(BF16)
    r1 = x - hi.astype(F32)
    mid = r1.astype(BF16)
    lo = (r1 - mid.astype(F32)).astype(BF16)
    return hi, mid, lo


def _dot_exact_rhs(a_bf16, x):
    hi, mid, lo = _split3(x)
    f = lambda p: jnp.dot(a_bf16, p, preferred_element_type=F32)
    return f(hi) + f(mid) + f(lo)


def _dot_exact_lhs(x, b_bf16):
    hi, mid, lo = _split3(x)
    f = lambda p: jnp.dot(p, b_bf16, preferred_element_type=F32)
    return f(hi) + f(mid) + f(lo)


def _silu(x):
    return x * jax.nn.sigmoid(x)


def _log_sigmoid(x):
    return jnp.minimum(x, 0.0) - jnp.log1p(jnp.exp(-jnp.abs(x)))


def _rms(x, g):
    return x * lax.rsqrt(jnp.mean(x * x, axis=-1, keepdims=True) + EPS) * g


def _head_mean_sq(x, bd_ref):
    n = x.shape[-1]
    sq = x * x
    hi = sq.astype(BF16)
    lo = (sq - hi.astype(F32)).astype(BF16)
    w = min(n, bd_ref.shape[0])
    bd = bd_ref[:w, :w]
    if 2 * w <= bd_ref.shape[0]:
        return jnp.dot(jnp.concatenate([hi, lo], axis=1), jnp.concatenate([bd, bd], axis=0),
                       preferred_element_type=F32)
    f = lambda p: jnp.dot(p, bd, preferred_element_type=F32)
    parts = [f(hi[:, c:c + w]) + f(lo[:, c:c + w]) for c in range(0, n, w)]
    return parts[0] if len(parts) == 1 else jnp.concatenate(parts, axis=1)


def _rope(x, cos, sin_signed):
    n = x.shape[-1]
    lane = lax.broadcasted_iota(jnp.int32, x.shape, 1)
    first_half = (lane % DH) < (DH // 2)
    partner = jnp.where(first_half, pltpu.roll(x, n - DH // 2, 1), pltpu.roll(x, DH // 2, 1))
    spread = lambda tab: tab if n == 128 else jnp.concatenate([tab] * (n // 128), axis=1)
    return x * spread(cos) + partner * spread(sin_signed)


def _adaln_kernel(c_ref, w0_ref, b0_ref, w1_ref, b1_ref, o_ref, *, steps):
    s = _silu(c_ref[...])
    j = pl.program_id(0)

    @pl.when(j < steps)
    def _():
        o_ref[0] = _dot(s, w0_ref[...]) + b0_ref[...]

    @pl.when(j >= steps)
    def _():
        o_ref[0] = _dot(s, w1_ref[...]) + b1_ref[...]


def _adaln(cond8, w0, b0, w1, b1):
    n = w0.shape[1]
    tn = 1536
    steps = n // tn
    first = lambda j: (0, jnp.minimum(j, steps - 1))
    second = lambda j: (0, jnp.maximum(j - steps, 0))
    return pl.pallas_call(
        functools.partial(_adaln_kernel, steps=steps),
        out_shape=jax.ShapeDtypeStruct((2, 8, n), F32),
        grid=(2 * steps,),
        in_specs=[pl.BlockSpec((8, D_MODEL), lambda j: (0, 0)),
                  pl.BlockSpec((D_MODEL, tn), first), pl.BlockSpec((1, tn), first),
                  pl.BlockSpec((D_MODEL, tn), second), pl.BlockSpec((1, tn), second)],
        out_specs=pl.BlockSpec((1, 8, tn), lambda j: (j // steps, 0, j % steps)),
        compiler_params=_params(("arbitrary",)),
        name="adaln",
    )(cond8, w0, b0.reshape(1, n), w1, b1.reshape(1, n))


def _const_spec(shape):
    nd = len(shape)
    return pl.BlockSpec(shape, lambda i, _n=nd: (0,) * _n, pipeline_mode=pl.Buffered(1))


def _mod_spec(chunk, rowfn):
    return pl.BlockSpec((1, 1, D_MODEL), lambda i: (rowfn(i) * ADA_CHUNKS + chunk, 0, 0))


def _rowfn(latent):
    if latent:
        return lambda i: 1 + (i * TOKEN_TILE) // LAT_LEN
    return lambda i: 0


def _tok_spec(width, blk0=0):
    return pl.BlockSpec((TOKEN_TILE, width), lambda i: (i + blk0, 0))


def _seq_len(latent):
    return LAT_LEN if latent else CTX_LEN


def _transposed_out(t, latent, rows):
    seq = _seq_len(latent)
    shape = jax.ShapeDtypeStruct((t // seq, rows, seq), F32)
    if seq <= TOKEN_TILE:
        per = TOKEN_TILE // seq
        return shape, pl.BlockSpec((per, rows, seq), lambda i: (i, 0, 0))
    per = seq // TOKEN_TILE
    return shape, pl.BlockSpec((1, rows, TOKEN_TILE), lambda i: (i // per, 0, i % per))


def _store_transposed(ref, x):
    n, _, w = ref.shape
    for j in range(n):
        ref[j] = x[j * w:(j + 1) * w].T


def _dot_colsT(w_cols, h):
    return lax.dot_general(w_cols, h, (((0,), (1,)), ((), ())), preferred_element_type=F32)


def _store_proj_cols_transposed(refs, w_cols, h):
    n, _, w = refs[0].shape
    for j in range(n):
        out_t = _dot_colsT(w_cols, h[j * w:(j + 1) * w])
        r0 = 0
        for ref in refs:
            ref[j] = out_t[r0:r0 + ref.shape[1]]
            r0 += ref.shape[1]


def _modnorm(x, ng_ref, sh_ref, sc_ref):
    return _rms(x, ng_ref[...] * (1.0 + sc_ref[0])) + sh_ref[0]


def _load_weights_bf16(pairs, stage, sem):
    tr, tc = W_TILE
    tiles = [(src, dst, r, c) for src, dst in pairs
             for r in range(0, src.shape[0], tr) for c in range(0, src.shape[1], tc)]

    def copy(k):
        src, _, r, c = tiles[k]
        return pltpu.make_async_copy(src.at[r:r + tr, c:c + tc], stage.at[k % W_SLOTS], sem.at[k % W_SLOTS])

    for k in range(min(W_SLOTS, len(tiles))):
        copy(k).start()
    for k, (_, dst, r, c) in enumerate(tiles):
        copy(k).wait()
        dst[r:r + tr, c:c + tc] = stage[k % W_SLOTS].astype(BF16)
        if k + W_SLOTS < len(tiles):
            copy(k + W_SLOTS).start()


def _ffn_kernel(*refs, n_ctx, first, has_mix, last):
    it = iter(refs)
    xs = (next(it), next(it)) if first else (next(it),)
    if has_mix:
        yac_ref, ybc_ref, yal_ref, ybl_ref, wo_hbm, g2_ref = (next(it) for _ in range(6))
    ng_ref, sh_ref, sc_ref, g_ref, win_hbm, wout_hbm = (next(it) for _ in range(6))
    if last:
        nf_ref = next(it)
    outs = (next(it), next(it)) if last else (next(it),)
    win_s, wout_s = next(it), next(it)
    if has_mix:
        wo_s = next(it)
    stage, sem = next(it), next(it)

    step = pl.program_id(0)
    is_lat = step >= n_ctx

    @pl.when(step == 0)
    def _():
        pairs = [(win_hbm, win_s), (wout_hbm, wout_s)]
        if has_mix:
            pairs.insert(0, (wo_hbm, wo_s))
        _load_weights_bf16(pairs, stage, sem)

    pick = lambda c_ref, l_ref: jnp.where(is_lat, l_ref[...], c_ref[...])
    x = pick(*xs) if first else xs[0][...]
    if has_mix:
        half = wo_s.shape[0] // 2
        mo = _dot(pick(yac_ref, yal_ref), wo_s[:half, :]) + _dot(pick(ybc_ref, ybl_ref), wo_s[half:, :])
        x = x + g2_ref[0] * mo
    h = _modnorm(x, ng_ref, sh_ref, sc_ref).astype(BF16)
    acc = None
    for lo in range(0, D_FF, FFN_CHUNK):
        hi = min(lo + FFN_CHUNK, D_FF)
        g = _dot(h, win_s[:, lo:hi])
        u = _dot(h, win_s[:, D_FF + lo:D_FF + hi])
        part = _dot(_silu(g) * u, wout_s[lo:hi, :])
        acc = part if acc is None else acc + part
    y = x + (0.5 * g_ref[0]) * acc
    if last:
        y = _rms(y, nf_ref[...])
        oc_ref, ol_ref = outs

        @pl.when(jnp.logical_not(is_lat))
        def _():
            oc_ref[...] = y
            ol_ref[...] = jnp.zeros_like(ol_ref)

        @pl.when(is_lat)
        def _():
            ol_ref[...] = y
    else:
        outs[0][...] = y


def _ffn(x, mods, mod_base, ng, w_in, w_out, *, t_ctx, t_lat, mix=None, final_g=None):
    tile = TOKEN_TILE
    n_ctx, n_lat = t_ctx // tile, t_lat // tile
    first, last = isinstance(x, tuple), final_g is not None
    ctx_map = lambda i: (jnp.minimum(i, n_ctx - 1), 0)
    lat_map = lambda i: (jnp.maximum(i - n_ctx, 0), 0)
    rowfn = lambda i: jnp.where(i < n_ctx, 0, 1 + ((i - n_ctx) * tile) // LAT_LEN)
    two = lambda wd: [pl.BlockSpec((tile, wd), ctx_map), pl.BlockSpec((tile, wd), lat_map)]
    one = pl.BlockSpec((tile, D_MODEL), lambda i: (i, 0))
    hbm = pl.BlockSpec(memory_space=pl.ANY)
    args = list(x) if first else [x]
    specs = two(D_MODEL) if first else [one]
    scratch = [pltpu.VMEM(w_in.shape, BF16), pltpu.VMEM(w_out.shape, BF16)]
    if mix is not None:
        yac, ybc, yal, ybl, wo = mix
        args += [yac, ybc, yal, ybl, wo, mods]
        mw = yac.shape[1]
        specs += [pl.BlockSpec((tile, mw), ctx_map)] * 2 + [pl.BlockSpec((tile, mw), lat_map)] * 2
        specs += [hbm, _mod_spec(mod_base - 1, rowfn)]
        scratch.append(pltpu.VMEM(wo.shape, BF16))
    args += [ng.reshape(1, D_MODEL), mods, mods, mods, w_in, w_out]
    specs += [_const_spec((1, D_MODEL)), _mod_spec(mod_base, rowfn), _mod_spec(mod_base + 1, rowfn),
              _mod_spec(mod_base + 2, rowfn), hbm, hbm]
    if last:
        args.append(final_g.reshape(1, D_MODEL))
        specs.append(_const_spec((1, D_MODEL)))
        out_shape = [jax.ShapeDtypeStruct((t_ctx, D_MODEL), F32), jax.ShapeDtypeStruct((t_lat, D_MODEL), F32)]
        out_specs = two(D_MODEL)
    else:
        out_shape = jax.ShapeDtypeStruct((t_ctx + t_lat, D_MODEL), F32)
        out_specs = one
    scratch += [pltpu.VMEM((W_SLOTS,) + W_TILE, F32), pltpu.SemaphoreType.DMA((W_SLOTS,))]
    return pl.pallas_call(
        functools.partial(_ffn_kernel, n_ctx=n_ctx, first=first, has_mix=mix is not None, last=last),
        out_shape=out_shape,
        grid=(n_ctx + n_lat,),
        in_specs=specs,
        out_specs=out_specs,
        scratch_shapes=scratch,
        compiler_params=_params(("arbitrary",), FFN_VMEM_LIMIT),
        name="ffn",
    )(*args)


_EV_QA, _EV_KA, _EV_VA, _EV_QB, _EV_KB, _EV_VB, _EV_OB = 0, 512, 640, 768, 1280, 1792, 2304
N_DIRHEAD = 2 * N_HEADS


def _proj_even_kernel(*refs, rope):
    it = iter(refs)
    x_ref, ng_ref, sh_ref, sc_ref, w_ref, wob_ref, wg_ref, gbi_ref, gbf_ref, gbt_ref, qg_ref, kg_ref, bd_ref = (
        next(it) for _ in range(13))
    if rope:
        cos_ref, sin_ref = next(it), next(it)
    qa_ref, kat_ref, vat_ref, qb_ref, vb_ref, ob_ref, gi_ref, gf_ref, gt_ref, kbt_ref = (next(it) for _ in range(10))

    h = _modnorm(x_ref[...], ng_ref, sh_ref, sc_ref).astype(BF16)
    proj = lambda lo, hi: _dot(h, w_ref[:, lo:hi])

    qa = proj(_EV_QA, _EV_KA)
    qa = qa * lax.rsqrt(_head_mean_sq(qa, bd_ref) + EPS) * qg_ref[...]
    ka = proj(_EV_KA, _EV_VA)
    ka = ka * lax.rsqrt(_head_mean_sq(ka, bd_ref) + EPS) * kg_ref[...]
    if rope:
        qa = _rope(qa, cos_ref[...], sin_ref[...])
        ka = _rope(ka, cos_ref[...], sin_ref[...])
    qa_ref[...] = qa * ATTN_SCALE
    _store_transposed(kat_ref, ka)
    _store_proj_cols_transposed((vat_ref,), w_ref[:, _EV_VA:_EV_QB], h)
    qb_ref[...] = proj(_EV_QB, _EV_KB)
    vb_ref[...] = proj(_EV_VB, _EV_OB)
    ob_ref[...] = _dot(h, wob_ref[...])
    gates = _dot(h, wg_ref[...])
    gi_ref[...] = gates[:, :128] + gbi_ref[...]
    gf_ref[...] = gates[:, 128:] + gbf_ref[...]
    for j in range(TOKEN_TILE // MLSTM_T):
        ht = h[j * MLSTM_T:(j + 1) * MLSTM_T]
        gates_t = [_dot_colsT(wg_ref[:, c0:c0 + N_DIRHEAD], ht) for c0 in (0, 128)]
        gt_ref[j] = jnp.concatenate(gates_t, axis=0) + gbt_ref[...]
        kbt_ref[j] = _dot_colsT(w_ref[:, _EV_KB:_EV_VB], ht) * ATTN_SCALE


def _proj_even(x, t, blk0, mods, latent, ng, w, w_ob, w_g, gbi, gbf, gbt, qg, kg, bd, rope_tabs):
    rowfn = _rowfn(latent)
    args = [x, ng.reshape(1, D_MODEL), mods, mods, w, w_ob, w_g, gbi, gbf, gbt, qg, kg, bd]
    specs = [_tok_spec(D_MODEL, blk0), _const_spec((1, D_MODEL)), _mod_spec(3, rowfn), _mod_spec(4, rowfn)]
    specs += [_const_spec(a.shape) for a in args[4:]]
    if latent:
        nblk = LAT_LEN // TOKEN_TILE
        args += list(rope_tabs)
        specs += [pl.BlockSpec((TOKEN_TILE, 128), lambda i: (i % nblk, 0))] * 2
    widths = (512, 512, 512, 512, 128, 128)
    out_shape = [jax.ShapeDtypeStruct((t, wd), F32) for wd in widths]
    out_specs = [_tok_spec(wd) for wd in widths]
    for pos in (1, 2):
        shape, spec = _transposed_out(t, latent, 128)
        out_shape.insert(pos, shape)
        out_specs.insert(pos, spec)
    nj = TOKEN_TILE // MLSTM_T
    for rows in (2 * N_DIRHEAD, 512):
        out_shape.append(jax.ShapeDtypeStruct((t // MLSTM_T, rows, MLSTM_T), F32))
        out_specs.append(pl.BlockSpec((nj, rows, MLSTM_T), lambda i: (i, 0, 0)))
    return pl.pallas_call(
        functools.partial(_proj_even_kernel, rope=latent),
        out_shape=out_shape, grid=(t // TOKEN_TILE,), in_specs=specs, out_specs=out_specs,
        compiler_params=_params(("arbitrary",)), name="proj_even",
    )(*args)


def _proj_odd_kernel(*refs, rope):
    it = iter(refs)
    x_ref, ng_ref, sh_ref, sc_ref, w_ref = (next(it) for _ in range(5))
    if rope:
        cos_ref, sin_ref = next(it), next(it)
    qc_ref, kc_ref, vc_ref, qd_ref, kd_ref, vd_ref = (next(it) for _ in range(6))

    h = _modnorm(x_ref[...], ng_ref, sh_ref, sc_ref).astype(BF16)
    proj = lambda lo, hi: _dot(h, w_ref[:, lo:hi])
    qc_ref[...] = proj(0, 512) * ATTN_SCALE
    if rope:
        kc_ref[...] = proj(512, 1024)
        vc_ref[...] = proj(1024, 1536)
    else:
        _store_proj_cols_transposed((kc_ref, vc_ref), w_ref[:, 512:1536], h)
    qd = proj(1536, 2048)
    if rope:
        qd = _rope(qd, cos_ref[...], sin_ref[...])
        _store_transposed(kd_ref, _rope(proj(2048, 2176), cos_ref[...], sin_ref[...]))
        _store_proj_cols_transposed((vd_ref,), w_ref[:, 2176:2304], h)
    else:
        _store_proj_cols_transposed((kd_ref, vd_ref), w_ref[:, 2048:2304], h)
    qd_ref[...] = qd * ATTN_SCALE


def _proj_odd(x, t, blk0, mods, latent, ng, w, rope_tabs):
    rowfn = _rowfn(latent)
    args = [x, ng.reshape(1, D_MODEL), mods, mods, w]
    specs = [_tok_spec(D_MODEL, blk0), _const_spec((1, D_MODEL)), _mod_spec(3, rowfn), _mod_spec(4, rowfn),
             _const_spec(w.shape)]
    if latent:
        nblk = LAT_LEN // TOKEN_TILE
        args += list(rope_tabs)
        specs += [pl.BlockSpec((TOKEN_TILE, 128), lambda i: (i % nblk, 0))] * 2
    out_shape = [jax.ShapeDtypeStruct((t, 512), F32)] * 4
    out_specs = [_tok_spec(512)] * 4
    if not latent:
        out_shape[1], out_specs[1] = _transposed_out(t, latent, 512)
        out_shape[2], out_specs[2] = out_shape[1], out_specs[1]
    for _ in range(2):
        shape, spec = _transposed_out(t, latent, 128)
        out_shape.append(shape)
        out_specs.append(spec)
    return pl.pallas_call(
        functools.partial(_proj_odd_kernel, rope=latent),
        out_shape=out_shape,
        grid=(t // TOKEN_TILE,), in_specs=specs, out_specs=out_specs,
        compiler_params=_params(("arbitrary",)), name="proj_odd",
    )(*args)


def _attn_kernel(*refs, group, has_extra, has_sink, spb):
    it = iter(refs)
    q_ref, k_ref, v_ref = next(it), next(it), next(it)
    if has_extra:
        ke_ref, ve_ref = next(it), next(it)
    if has_sink:
        sink_ref = next(it)
    o_ref = next(it)
    tq = q_ref.shape[0] // spb
    first = lax.broadcasted_iota(jnp.int32, (tq, 128), 1) < DH

    def kv_operands(kt_ref, vt_ref, smp, kv):
        kt = kt_ref[smp, kv * DH:(kv + 1) * DH, :].astype(BF16)
        vt = vt_ref[smp, kv * DH:(kv + 1) * DH, :].astype(BF16)
        z = jnp.zeros_like(kt)
        ones = jnp.ones((128, kt.shape[1]), BF16)
        return ((jnp.concatenate([kt, z], axis=0), jnp.concatenate([z, kt], axis=0)),
                (jnp.concatenate([vt, z, ones], axis=0), jnp.concatenate([z, vt, ones], axis=0)))

    for smp in range(spb):
        qrows = slice(smp * tq, (smp + 1) * tq)
        cache = {}
        for j in range(N_HEADS // 2):
            ps = slice(128 * j, 128 * (j + 1))
            qb = q_ref[qrows, ps].astype(BF16)
            res, sink_den = [], []
            for half in range(2):
                kv = (2 * j + half) // group
                if kv not in cache:
                    cache[kv] = (kv_operands(k_ref, v_ref, smp, kv),
                                 kv_operands(ke_ref, ve_ref, smp, kv) if has_extra else None)
                (k_ops, v_ops), extra = cache[kv]
                s = jnp.dot(qb, k_ops[half], preferred_element_type=F32)
                m = jnp.max(s, axis=-1, keepdims=True)
                if has_extra:
                    se = jnp.dot(qb, extra[0][half], preferred_element_type=F32)
                    m = jnp.maximum(m, jnp.max(se, axis=-1, keepdims=True))
                if has_sink:
                    m = jnp.maximum(m, sink_ref[2 * j + half])
                    sink_den.append(jnp.exp(sink_ref[2 * j + half] - m))
                r = _dot_nt(jnp.exp(s - m), v_ops[half])
                if has_extra:
                    r = r + _dot_nt(jnp.exp(se - m), extra[1][half])
                res.append(r)
            num = jnp.where(first, res[0][:, :128], res[1][:, :128])
            den = jnp.where(first, res[0][:, 128:], res[1][:, 128:])
            if has_sink:
                den = den + jnp.where(first, sink_den[0], sink_den[1])
            o_ref[qrows, ps] = num / den


def _attention(q, k, v, *, n_batch, q_len, q_tile, group, extra=None, sink=None):
    nq = q_len // q_tile
    spb = next(n for n in (8, 4, 2, 1) if n_batch % n == 0) if nq == 1 else 1
    kv_spec = lambda a: pl.BlockSpec((spb,) + a.shape[1:], lambda b, j: (b, 0, 0))
    args = [q, k, v]
    specs = [pl.BlockSpec((spb * q_tile, 512), lambda b, j: (b * nq + j, 0)), kv_spec(k), kv_spec(v)]
    if extra is not None:
        args += list(extra)
        specs += [kv_spec(extra[0]), kv_spec(extra[1])]
    if sink is not None:
        args.append(sink)
        specs.append(pl.BlockSpec(memory_space=pltpu.SMEM))
    return pl.pallas_call(
        functools.partial(_attn_kernel, group=group, has_extra=extra is not None, has_sink=sink is not None, spb=spb),
        out_shape=jax.ShapeDtypeStruct(q.shape, F32),
        grid=(n_batch // spb, nq), in_specs=specs,
        out_specs=pl.BlockSpec((spb * q_tile, 512), lambda b, j: (b * nq + j, 0)),
        compiler_params=_params(("arbitrary", "arbitrary")), name="attn",
    )(*args)


LOG2E = 1.4426950408889634


def _cummax_time(x, reverse):
    n = x.shape[0]
    row = lax.broadcasted_iota(jnp.int32, x.shape, 0)
    k = 1
    while k < n:
        if reverse:
            shifted = jnp.where(row < n - k, pltpu.roll(x, n - k, 0), -jnp.inf)
        else:
            shifted = jnp.where(row >= k, pltpu.roll(x, k, 0), -jnp.inf)
        x = jnp.maximum(x, shifted)
        k *= 2
    return x


def _mlstm_kernel(*refs, n_chunks, carry, spb):
    it = iter(refs)
    q_ref, kt_ref, v_ref, ob_ref, gi_ref, gf_ref, gt_ref, hg_ref, bd_ref = (next(it) for _ in range(9))
    if carry:
        c0_ref, n0_ref, m0r_ref, m0c_ref = (next(it) for _ in range(4))
    y_ref = next(it)
    if not carry:
        c_ref, n_ref, m_ref = next(it), next(it), next(it)
    hf_s, hb_s = next(it), next(it)
    if carry:
        st_s, mrow_s, mcol_s = next(it), next(it), next(it)
    T = MLSTM_T
    step = pl.program_id(1)

    if carry:
        @pl.when(step == 0)
        def _():
            z = jnp.zeros((DH, DH), F32)
            for d in range(2):
                for j in range(N_HEADS // 2):
                    ca, cb = c0_ref[0, d, 2 * j], c0_ref[0, d, 2 * j + 1]
                    na, nb = n0_ref[0, d, 2 * j][:, :DH], n0_ref[0, d, 2 * j + 1][:, :DH]
                    st_s[d, j, 0:DH, :] = jnp.concatenate([ca, z, na, z], axis=1)
                    st_s[d, j, DH:2 * DH, :] = jnp.concatenate([z, cb, z, nb], axis=1)
            mrow_s[...] = m0r_ref[0]
            mcol_s[...] = m0c_ref[0]

    row = lax.broadcasted_iota(jnp.int32, (T, T), 0)
    col = lax.broadcasted_iota(jnp.int32, (T, T), 1)
    lower = row >= col
    upper = row <= col
    lower_m = jnp.where(lower, 1.0, 0.0).astype(BF16)
    upper_m = jnp.where(upper, 1.0, 0.0).astype(BF16)
    lane = lax.broadcasted_iota(jnp.int32, (T, 128), 1)
    first = lane < DH
    lane_c = lax.broadcasted_iota(jnp.int32, (DH, 256), 1) % 128
    zeros_kt = jnp.zeros((DH, T), BF16)
    ones_v = jnp.ones((T, 128), BF16)

    for smp, d in ((s_, d_) for s_ in range(spb) for d_ in range(2)):
        chunk = (step if d == 0 else n_chunks - 1 - step) if carry else smp
        rows = pl.ds(pl.multiple_of(chunk * T, T), T)
        hd0 = N_HEADS * d
        mask = lower if d == 0 else upper

        if carry or d == 0:
            gi = gi_ref[rows, :]
            lf = _log_sigmoid(gf_ref[rows, :])
            prefix = _dot_exact_rhs(lower_m, lf)
        b_cols = prefix if d == 0 else prefix[T - 1:T, :] - prefix + lf
        a_cols = gi - b_cols
        dmax = b_cols + _cummax_time(a_cols, reverse=d == 1)
        if carry:
            inter = b_cols + mrow_s[0:1, :]
            mt = jnp.maximum(inter, dmax)
            w_inter = jnp.exp(inter - mt)
        else:
            mt = jnp.maximum(b_cols, dmax)
        c2 = (b_cols - mt) * LOG2E
        einv = jnp.exp(-mt)

        gt = gt_ref[chunk]
        gi_t = gt[hd0:hd0 + N_HEADS, :]
        gf_t = gt[N_DIRHEAD + hd0:N_DIRHEAD + hd0 + N_HEADS, :]
        lf_t = _log_sigmoid(gf_t)
        b_rows = _dot_exact_lhs(lf_t, upper_m)
        if d == 1:
            b_rows = b_rows[:, T - 1:T] - b_rows + lf_t
        a_rows = gi_t - b_rows
        a2 = a_rows * LOG2E
        b_last = b_rows[:, T - 1:T] if d == 0 else b_rows[:, 0:1]
        dec_rows = b_last + a_rows
        m_new = jnp.max(dec_rows, axis=-1, keepdims=True)
        if carry:
            m_prev = mcol_s[hd0:hd0 + N_HEADS, 0:1]
            m_new = jnp.maximum(m_new, b_last + m_prev)
            wc = jnp.exp(b_last + m_prev - m_new)
        else:
            m_new = jnp.maximum(m_new, b_last)
        ws_rows = jnp.exp(dec_rows - m_new)

        h_s = hf_s if d == 0 else hb_s
        for j in range(N_HEADS // 2):
            ps = slice(128 * j, 128 * (j + 1))
            q_pair = q_ref[rows, ps].astype(BF16)
            v_aug = jnp.concatenate([v_ref[rows, ps].astype(BF16), ones_v], axis=1)
            kt = (kt_ref[chunk, 128 * j:128 * j + DH, :], kt_ref[chunk, 128 * j + DH:128 * (j + 1), :])
            lhs = []
            for half in range(2):
                hd = 2 * j + half
                kt0 = jnp.concatenate([kt[0].astype(BF16), zeros_kt] if half == 0 else [zeros_kt, kt[1].astype(BF16)], axis=0)
                qk = jnp.dot(q_pair, kt0, preferred_element_type=F32)
                e = c2[:, hd0 + hd:hd0 + hd + 1] + a2[hd:hd + 1, :]
                lhs.append((jnp.exp2(jnp.where(mask, e, -jnp.inf)) * qk).astype(BF16))
            for half in range(2):
                lhs.append((kt[half] * ws_rows[2 * j + half:2 * j + half + 1, :]).astype(BF16))
            both = jnp.dot(jnp.concatenate(lhs, axis=0), v_aug, preferred_element_type=F32)
            res = (both[0:T], both[T:2 * T])
            cu = (both[2 * T:2 * T + DH], both[2 * T + DH:2 * T + 2 * DH])
            ra, rb = hd0 + 2 * j, hd0 + 2 * j + 1
            num = jnp.where(first, res[0][:, :128], res[1][:, :128])
            den = jnp.where(first, res[0][:, 128:], res[1][:, 128:])
            if carry:
                old = st_s[d, j]
                qs = _dot(q_pair, old)
                wi = jnp.where(first, w_inter[:, ra:ra + 1], w_inter[:, rb:rb + 1])
                num = num + wi * qs[:, :128]
                den = den + wi * qs[:, 128:]
            floor = jnp.where(first, einv[:, ra:ra + 1], einv[:, rb:rb + 1])
            h_s[rows, ps] = num / jnp.maximum(jnp.abs(den), floor)
            if carry:
                st_s[d, j, 0:DH, :] = jnp.where(lane_c < DH, cu[0], 0.0) + wc[2 * j:2 * j + 1, :] * old[0:DH]
                st_s[d, j, DH:2 * DH, :] = jnp.where(lane_c >= DH, cu[1], 0.0) + wc[2 * j + 1:2 * j + 2, :] * old[DH:2 * DH]
            else:
                c_ref[smp, d, 2 * j] = cu[0][:, 0:DH]
                c_ref[smp, d, 2 * j + 1] = cu[1][:, DH:2 * DH]

        if carry:
            b_last_l = b_cols[T - 1:T, :] if d == 0 else b_cols[0:1, :]
            m_prev_l = mrow_s[0:1, :]
            m_new_l = jnp.maximum(jnp.max(b_last_l + a_cols, axis=0, keepdims=True), b_last_l + m_prev_l)
            lane_r = lax.broadcasted_iota(jnp.int32, (1, 128), 1)
            mine = (lane_r >= hd0) & (lane_r < hd0 + N_HEADS)
            mrow_s[0:1, :] = jnp.where(mine, m_new_l, m_prev_l)
            mcol_s[hd0:hd0 + N_HEADS, :] = jnp.broadcast_to(m_new, (N_HEADS, 128))
        else:
            for hd in range(N_HEADS):
                n_hd = _dot_nt(ws_rows, kt_ref[chunk, hd * DH:(hd + 1) * DH, :])
                n_ref[smp, d, hd:hd + 1, :] = n_hd[hd:hd + 1, :]
            m_ref[smp, hd0:hd0 + N_HEADS, :] = jnp.broadcast_to(m_new, (N_HEADS, 128))

    @pl.when(step == n_chunks - 1)
    def _():
        hsum = hf_s[...] + hb_s[...]
        y = hsum * lax.rsqrt(_head_mean_sq(hsum, bd_ref) + EPS) * hg_ref[...]
        y_ref[...] = jax.nn.sigmoid(ob_ref[...]) * y


def _mlstm(qb, kbt, vb, ob, gi, gf, gt, head_g, bd, *, n_batch, seq_len, state=None):
    n_chunks = seq_len // MLSTM_T
    carry = state is not None
    assert carry or n_chunks == 1
    spb = 1
    seq = lambda wd: pl.BlockSpec((spb * seq_len, wd), lambda b, c: (b, 0))
    per_chunk = lambda r: pl.BlockSpec((spb * n_chunks, r, MLSTM_T), lambda b, c: (b, 0, 0))
    args = [qb, kbt, vb, ob, gi, gf, gt, head_g, bd]
    specs = [seq(512), per_chunk(512), seq(512), seq(512), seq(128), seq(128), per_chunk(2 * N_DIRHEAD),
             pl.BlockSpec((1, 512), lambda b, c: (0, 0)), pl.BlockSpec(bd.shape, lambda b, c: (0, 0))]
    out_shape = [jax.ShapeDtypeStruct((n_batch * seq_len, 512), F32)]
    out_specs = [seq(512)]
    scratch = [pltpu.VMEM((spb * seq_len, 512), F32), pltpu.VMEM((spb * seq_len, 512), F32)]
    if carry:
        args += list(state)
        specs += [pl.BlockSpec((1, 2, N_HEADS, DH, DH), lambda b, c: (b, 0, 0, 0, 0)),
                  pl.BlockSpec((1, 2, N_HEADS, DH, 128), lambda b, c: (b, 0, 0, 0, 0)),
                  pl.BlockSpec((1, 8, 128), lambda b, c: (b, 0, 0)),
                  pl.BlockSpec((1, N_DIRHEAD, 128), lambda b, c: (b, 0, 0))]
        scratch += [pltpu.VMEM((2, N_HEADS // 2, 128, 256), F32), pltpu.VMEM((8, 128), F32),
                    pltpu.VMEM((N_DIRHEAD, 128), F32)]
    else:
        out_shape += [jax.ShapeDtypeStruct((n_batch, 2, N_HEADS, DH, DH), F32),
                      jax.ShapeDtypeStruct((n_batch, 2, N_HEADS, DH), F32),
                      jax.ShapeDtypeStruct((n_batch, N_DIRHEAD, 128), F32)]
        out_specs += [pl.BlockSpec((spb, 2, N_HEADS, DH, DH), lambda b, c: (b, 0, 0, 0, 0)),
                      pl.BlockSpec((spb, 2, N_HEADS, DH), lambda b, c: (b, 0, 0, 0)),
                      pl.BlockSpec((spb, N_DIRHEAD, 128), lambda b, c: (b, 0, 0))]
    return pl.pallas_call(
        functools.partial(_mlstm_kernel, n_chunks=n_chunks, carry=carry, spb=spb),
        out_shape=out_shape, grid=(n_batch // spb, n_chunks), in_specs=specs, out_specs=out_specs,
        scratch_shapes=scratch,
        compiler_params=_params(("arbitrary", "arbitrary")), name="mlstm",
    )(*args)


N_ROWS = LAT_LEN // GRID_W
N_DY = 2 * NA_KH - 1
N_DX = 2 * NA_KW - 1


NA_QROWS = 4
NA_KROWS = 12
NA_GROUPS = N_ROWS // NA_QROWS


def _na_kernel(q_ref, k_ref, v_ref, ke_ref, ve_ref, rpb_ref, o_ref, tile_s, slab_s):
    pair = pl.program_id(1)
    qn, kn = NA_QROWS * GRID_W, NA_KROWS * GRID_W

    @pl.when(pl.program_id(0) == 0)
    def _():
        qc = lax.broadcasted_iota(jnp.int32, (GRID_W, GRID_W), 0)
        kc = lax.broadcasted_iota(jnp.int32, (GRID_W, GRID_W), 1)
        start = jnp.clip(qc - NA_KW // 2, 0, GRID_W - NA_KW)
        in_win = (kc >= start) & (kc < start + NA_KW)
        dx = kc - qc + (NA_KW - 1)
        blocked = jnp.full((GRID_W, GRID_W), NEG_INF, F32)
        for half in range(2):
            head = 2 * pair + half

            def build_tile(dy, carry):
                tile = blocked
                for j in range(N_DX):
                    tile = jnp.where(dx == j, rpb_ref[(head * N_DY + dy) * N_DX + j], tile)
                tile_s[half * N_DY + dy] = jnp.where(in_win, tile, NEG_INF)
                return carry

            lax.fori_loop(0, N_DY, build_tile, 0)
            for kind in range(3):
                idx = (pair * 2 + half) * 3 + kind
                for i in range(NA_QROWS):
                    for k in range(NA_KROWS):
                        if kind == 0:
                            ok, dy = k < NA_KH, k - i + NA_KH - 1
                        elif kind == 1:
                            ok, dy = i <= k < i + NA_KH, k - i + NA_KH // 2 - 1
                        else:
                            ok, dy = k >= NA_KROWS - NA_KH, k - i - 1
                        slab_s[idx, i * GRID_W:(i + 1) * GRID_W, k * GRID_W:(k + 1) * GRID_W] = (
                            tile_s[half * N_DY + dy] if ok else blocked)

    first = lax.broadcasted_iota(jnp.int32, (qn, 128), 1) < DH
    ones_k = jnp.ones((kn, 128), BF16)
    ke = ke_ref[...].astype(BF16)
    ve_aug = jnp.concatenate([ve_ref[...].astype(BF16), jnp.ones((ke.shape[0], 128), BF16)], axis=1)

    def group_body(g, carry):
        k0 = jnp.where(g < NA_GROUPS // 2, 0, N_ROWS - NA_KROWS)
        kind = jnp.where(g == 0, 0, jnp.where(g == NA_GROUPS - 1, 2, 1))
        qrows = pl.ds(pl.multiple_of(g * qn, qn), qn)
        keys = pl.ds(pl.multiple_of(k0 * GRID_W, GRID_W), kn)
        q_pair = q_ref[qrows, :]
        k_pair = k_ref[keys, :].astype(BF16)
        v_aug = jnp.concatenate([v_ref[keys, :].astype(BF16), ones_k], axis=1)
        res = []
        for half in range(2):
            qm = jnp.where(first, q_pair, 0.0) if half == 0 else jnp.where(first, 0.0, q_pair)
            s = _dot_nt(qm, k_pair) + slab_s[(pair * 2 + half) * 3 + kind]
            se = _dot_nt(qm, ke)
            m = jnp.maximum(jnp.max(s, axis=-1, keepdims=True), jnp.max(se, axis=-1, keepdims=True))
            res.append(_dot(jnp.exp(s - m), v_aug) + _dot(jnp.exp(se - m), ve_aug))
        num = jnp.where(first, res[0][:, :128], res[1][:, :128])
        den = jnp.where(first, res[0][:, 128:], res[1][:, 128:])
        o_ref[qrows, :] = num / den
        return carry

    lax.fori_loop(0, NA_GROUPS, group_body, 0, unroll=True)


def _na_latent(q, k, v, k_ctx, v_ctx, rpb_flat, n_batch):
    seq = pl.BlockSpec((LAT_LEN, 128), lambda b, j: (b, j))
    ctx = pl.BlockSpec((CTX_LEN, 128), lambda b, j: (b, j))
    return pl.pallas_call(
        _na_kernel,
        out_shape=jax.ShapeDtypeStruct(q.shape, F32),
        grid=(n_batch, N_HEADS // 2),
        in_specs=[seq, seq, seq, ctx, ctx, pl.BlockSpec(memory_space=pltpu.SMEM)],
        out_specs=seq,
        scratch_shapes=[pltpu.VMEM((2 * N_DY, GRID_W, GRID_W), F32),
                        pltpu.VMEM((N_HEADS * 3, NA_QROWS * GRID_W, NA_KROWS * GRID_W), F32)],
        compiler_params=_params(("arbitrary", "arbitrary")), name="na_latent",
    )(q, k, v, k_ctx, v_ctx, rpb_flat)


SWA_QT = 256
SWA_SPAN = SWA_QT + 2 * SWA_WIN


def _swa_kernel(q_ref, kt_ref, vt_ref, ket_ref, vet_ref, sink_ref, o_ref):
    pair = pl.program_id(1)
    first = lax.broadcasted_iota(jnp.int32, (SWA_QT, 128), 1) < DH
    row = lax.broadcasted_iota(jnp.int32, (SWA_QT, SWA_SPAN), 0)
    col = lax.broadcasted_iota(jnp.int32, (SWA_QT, SWA_SPAN), 1)
    z = jnp.zeros((DH, SWA_SPAN), BF16)
    ones = jnp.ones((128, SWA_SPAN), BF16)
    ket, vet = ket_ref[0].astype(BF16), vet_ref[0].astype(BF16)
    ze = jnp.zeros_like(ket)
    ones_e = jnp.ones((128, ket.shape[1]), BF16)
    sinks = (sink_ref[2 * pair], sink_ref[2 * pair + 1])
    for g in range(LAT_LEN // SWA_QT):
        q0 = g * SWA_QT
        lo = min(max(q0 - SWA_WIN, 0), LAT_LEN - SWA_SPAN)
        near = jnp.abs((lo + col) - (q0 + row)) <= SWA_WIN
        qb = q_ref[q0:q0 + SWA_QT, :].astype(BF16)
        kt = kt_ref[0, :, lo:lo + SWA_SPAN].astype(BF16)
        vt = vt_ref[0, :, lo:lo + SWA_SPAN].astype(BF16)
        res, sink_den = [], []
        for half in range(2):
            k_op = jnp.concatenate([kt, z] if half == 0 else [z, kt], axis=0)
            v_op = jnp.concatenate([vt, z, ones] if half == 0 else [z, vt, ones], axis=0)
            ke_op = jnp.concatenate([ket, ze] if half == 0 else [ze, ket], axis=0)
            ve_op = jnp.concatenate([vet, ze, ones_e] if half == 0 else [ze, vet, ones_e], axis=0)
            s = jnp.where(near, jnp.dot(qb, k_op, preferred_element_type=F32), NEG_INF)
            se = jnp.dot(qb, ke_op, preferred_element_type=F32)
            m = jnp.maximum(jnp.max(s, axis=-1, keepdims=True), jnp.max(se, axis=-1, keepdims=True))
            m = jnp.maximum(m, sinks[half])
            sink_den.append(jnp.exp(sinks[half] - m))
            res.append(_dot_nt(jnp.exp(s - m), v_op) + _dot_nt(jnp.exp(se - m), ve_op))
        num = jnp.where(first, res[0][:, :128], res[1][:, :128])
        den = jnp.where(first, res[0][:, 128:], res[1][:, 128:]) + jnp.where(first, sink_den[0], sink_den[1])
        o_ref[q0:q0 + SWA_QT, :] = num / den


def _swa_latent(q, kt, vt, kt_ctx, vt_ctx, sink, n_batch, group):
    ppk = group // 2
    kv_spec = lambda a: pl.BlockSpec((1, DH, a.shape[2]), lambda b, j: (b, j // ppk, 0))
    return pl.pallas_call(
        _swa_kernel,
        out_shape=jax.ShapeDtypeStruct(q.shape, F32),
        grid=(n_batch, N_HEADS // 2),
        in_specs=[pl.BlockSpec((LAT_LEN, 128), lambda b, j: (b, j)),
                  kv_spec(kt), kv_spec(vt), kv_spec(kt_ctx), kv_spec(vt_ctx),
                  pl.BlockSpec(memory_space=pltpu.SMEM)],
        out_specs=pl.BlockSpec((LAT_LEN, 128), lambda b, j: (b, j)),
        compiler_params=_params(("arbitrary", "arbitrary")), name="swa_latent",
    )(q, kt, vt, kt_ctx, vt_ctx, sink)


def _rope_tables():
    t = np.arange(LAT_LEN)
    pos = np.stack([t // GRID_W, t % GRID_W], axis=-1).astype(np.float32)
    freqs = np.float32(ROPE_THETA) ** (-np.arange(ROPE_FREQS, dtype=np.float32) / np.float32(ROPE_FREQS))
    ang = (pos[:, :, None] * freqs).reshape(LAT_LEN, 2 * ROPE_FREQS).astype(np.float32)
    cos, sin = np.cos(ang), np.sin(ang)
    cos_t = np.tile(np.concatenate([cos, cos], axis=-1), (1, 2))
    sin_t = np.tile(np.concatenate([-sin, sin], axis=-1), (1, 2))
    return jnp.asarray(cos_t, F32), jnp.asarray(sin_t, F32)


def kernel(x_prompt, x_sample, cache_l0_attn_k, cache_l0_attn_v, state_l0_mlstm_C, state_l0_mlstm_n, state_l0_mlstm_m, cache_l1_na_k, cache_l1_na_v, cache_l1_swa_k, cache_l1_swa_v, c, c_ctx, norm_final, ada_w_l0, ada_b_l0, norm_l0, ffn1_in_l0, ffn1_out_l0, ffn2_in_l0, ffn2_out_l0, mix_in_l0, mix_out_l0, qk_norm_l0, gate_bias_l0, head_norm_l0, ada_w_l1, ada_b_l1, norm_l1, ffn1_in_l1, ffn1_out_l1, ffn2_in_l1, ffn2_out_l1, mix_in_l1, mix_out_l1, rpb_l1, sink_l1):
    nb, nl = x_prompt.shape[0], x_sample.shape[0]
    bf = lambda a: a.astype(BF16)

    cond8 = jnp.zeros((8, D_MODEL), F32).at[0].set(c_ctx).at[1:1 + nl].set(c)
    mods = _adaln(cond8, ada_w_l0, ada_b_l0, ada_w_l1, ada_b_l1)
    mods0 = mods[0].reshape(8 * ADA_CHUNKS, 1, D_MODEL)
    mods1 = mods[1].reshape(8 * ADA_CHUNKS, 1, D_MODEL)

    g0 = 2304
    gcols = lambda j: mix_in_l0[:, g0 + 8 * j:g0 + 8 * (j + 1)]
    gpad = jnp.zeros((D_MODEL, 128 - N_DIRHEAD), F32)
    w_gi, w_gf = jnp.concatenate([gcols(0), gcols(2)], axis=1), jnp.concatenate([gcols(1), gcols(3)], axis=1)
    w_even, w_ob = bf(mix_in_l0[:, :g0]), bf(mix_in_l0[:, g0 + 32:])
    w_g = bf(jnp.concatenate([w_gi, gpad, w_gf, gpad], axis=1))
    gb4 = gate_bias_l0.reshape(4, N_HEADS)
    b_gi, b_gf = jnp.concatenate([gb4[0], gb4[2]]), jnp.concatenate([gb4[1], gb4[3]])
    gbi = jnp.zeros((1, 128), F32).at[0, :N_DIRHEAD].set(b_gi)
    gbf = jnp.zeros((1, 128), F32).at[0, :N_DIRHEAD].set(b_gf)
    gbt = jnp.concatenate([b_gi, b_gf]).reshape(2 * N_DIRHEAD, 1)
    qg = jnp.tile(qk_norm_l0[0], N_HEADS).reshape(1, 512)
    kg = jnp.tile(qk_norm_l0[1], 2).reshape(1, 128)
    head_g = head_norm_l0.reshape(1, 512)
    grp = np.arange(256) // DH
    bd = jnp.asarray((grp[:, None] == grp[None, :]).astype(np.float32) / DH, dtype=BF16)
    rope_tabs = _rope_tables()
    w_odd = bf(mix_in_l1)
    rpb_flat = rpb_l1.reshape(-1)
    to_t = lambda a: a.reshape(a.shape[0], a.shape[1], -1).transpose(0, 2, 1)
    from_t = lambda a: a.reshape(a.shape[0], -1, DH, a.shape[2]).transpose(0, 3, 1, 2)

    xp = x_prompt.reshape(nb * CTX_LEN, D_MODEL)
    xs = x_sample.reshape(nl * LAT_LEN, D_MODEL)
    t_ctx, t_lat = xp.shape[0], xs.shape[0]
    streams = ((False, t_ctx, 0, nb, CTX_LEN), (True, t_lat, t_ctx // TOKEN_TILE, nl, LAT_LEN))
    ffn = functools.partial(_ffn, t_ctx=t_ctx, t_lat=t_lat)

    x = ffn((xp, xs), mods0, 0, norm_l0[0], ffn1_in_l0, ffn1_out_l0)
    ya, yb = {}, {}
    for latent, t, blk0, n_batch, seq_len in streams:
        qa, kat, vat, qb, vb, ob, gi, gf, gt, kbt = _proj_even(
            x, t, blk0, mods0, latent, norm_l0[1], w_even, w_ob, w_g, gbi, gbf, gbt, qg, kg, bd, rope_tabs)
        if latent:
            extra = (to_t(cache_l0_attn_k), to_t(cache_l0_attn_v))
            m0 = state_l0_mlstm_m.reshape(nl, N_DIRHEAD)
            m0_lanes = jnp.zeros((nl, 8, 128), F32).at[:, 0, :N_DIRHEAD].set(m0)
            m0_rows = jnp.broadcast_to(m0[:, :, None], (nl, N_DIRHEAD, 128))
            n0_cols = jnp.broadcast_to(state_l0_mlstm_n[..., None], (nl, 2, N_HEADS, DH, 128))
            state = (state_l0_mlstm_C, n0_cols, m0_lanes, m0_rows)
        else:
            extra, state = None, None
            k0t, v0t = kat, vat
        ya[latent] = _attention(qa, kat, vat, n_batch=n_batch, q_len=seq_len, q_tile=256, group=4, extra=extra)
        ml = _mlstm(qb, kbt, vb, ob, gi, gf, gt, head_g, bd, n_batch=n_batch, seq_len=seq_len, state=state)
        yb[latent] = ml[0]
        if not latent:
            c0, n0, m0_pad = ml[1:]
    x = ffn(x, mods0, 6, norm_l0[2], ffn2_in_l0, ffn2_out_l0, mix=(ya[False], yb[False], ya[True], yb[True], mix_out_l0))

    x = ffn(x, mods1, 0, norm_l1[0], ffn1_in_l1, ffn1_out_l1)
    yc, yd = {}, {}
    for latent, t, blk0, n_batch, seq_len in streams:
        qc, kc, vc, qd, kd, vd = _proj_odd(x, t, blk0, mods1, latent, norm_l1[1], w_odd, rope_tabs)
        if latent:
            yc[latent] = _na_latent(qc, kc, vc, cache_l1_na_k.reshape(nl * CTX_LEN, 512),
                                    cache_l1_na_v.reshape(nl * CTX_LEN, 512), rpb_flat, nl)
            yd[latent] = _swa_latent(qd, kd, vd, to_t(cache_l1_swa_k), to_t(cache_l1_swa_v), sink_l1, nl, group=4)
        else:
            yc[latent] = _attention(qc, kc, vc, n_batch=n_batch, q_len=seq_len, q_tile=256, group=1)
            yd[latent] = _attention(qd, kd, vd, n_batch=n_batch, q_len=seq_len, q_tile=256, group=4, sink=sink_l1)
            kc1t, vc1t, kd1t, vd1t = kc, vc, kd, vd
    y_prompt, y_sample = ffn(x, mods1, 6, norm_l1[2], ffn2_in_l1, ffn2_out_l1,
                             mix=(yc[False], yd[False], yc[True], yd[True], mix_out_l1), final_g=norm_final)

    return (y_prompt.reshape(nb, CTX_LEN, D_MODEL), y_sample.reshape(nl, LAT_LEN, D_MODEL),
            from_t(k0t), from_t(v0t),
            c0, n0, m0_pad[:, :, 0].reshape(nb, 2, N_HEADS),
            from_t(kc1t), from_t(vc1t),
            from_t(kd1t), from_t(vd1t))
```

```python
import functools

import jax
import jax.numpy as jnp
import numpy as np
from jax import lax
from jax.experimental import pallas as pl
from jax.experimental.pallas import tpu as pltpu

F32 = jnp.float32
BF16 = jnp.bfloat16

D_MODEL = 1024
DH = 64
D_FF = 2816
ADA_CHUNKS = 9
GRID_W = 64
LAT_LEN = 1024
CTX_LEN = 256
N_HEADS = 8
NA_KH = 8
NA_KW = 16
SWA_WIN = 128
ROPE_THETA = 10000.0
ROPE_FREQS = DH // 4
ATTN_SCALE = DH ** -0.5
NEG_INF = -1e30
EPS = 1e-6

TOKEN_TILE = 512
MLSTM_T = 256
FFN_CHUNK = 256
VMEM_LIMIT = 56 * 1024 * 1024
FFN_VMEM_LIMIT = 60 * 1024 * 1024
W_TILE = (256, 512)
W_SLOTS = 6


def _params(sem, vmem=VMEM_LIMIT):
    return pltpu.CompilerParams(dimension_semantics=sem, vmem_limit_bytes=vmem)


def _dot(a, b):
    return jnp.dot(a.astype(BF16), b.astype(BF16), preferred_element_type=F32)


def _dot_nt(a, b):
    return lax.dot_general(a.astype(BF16), b.astype(BF16), (((1,), (1,)), ((), ())),
                           preferred_element_type=F32)


def _split3(x):
    hi = x.astype(BF16)
    r1 = x - hi.astype(F32)
    mid = r1.astype(BF16)
    lo = (r1 - mid.astype(F32)).astype(BF16)
    return hi, mid, lo


def _dot_exact_rhs(a_bf16, x):
    hi, mid, lo = _split3(x)
    f = lambda p: jnp.dot(a_bf16, p, preferred_element_type=F32)
    return f(hi) + f(mid) + f(lo)


def _dot_exact_lhs(x, b_bf16):
    hi, mid, lo = _split3(x)
    f = lambda p: jnp.dot(p, b_bf16, preferred_element_type=F32)
    return f(hi) + f(mid) + f(lo)


def _silu(x):
    return x * jax.nn.sigmoid(x)


def _log_sigmoid(x):
    return jnp.minimum(x, 0.0) - jnp.log1p(jnp.exp(-jnp.abs(x)))


def _rms(x, g):
    return x * lax.rsqrt(jnp.mean(x * x, axis=-1, keepdims=True) + EPS) * g


def _head_mean_sq(x, bd_ref):
    n = x.shape[-1]
    sq = x * x
    hi = sq.astype(BF16)
    lo = (sq - hi.astype(F32)).astype(BF16)
    w = min(n, bd_ref.shape[0])
    bd = bd_ref[:w, :w]
    if 2 * w <= bd_ref.shape[0]:
        return jnp.dot(jnp.concatenate([hi, lo], axis=1), jnp.concatenate([bd, bd], axis=0),
                       preferred_element_type=F32)
    f = lambda p: jnp.dot(p, bd, preferred_element_type=F32)
    parts = [f(hi[:, c:c + w]) + f(lo[:, c:c + w]) for c in range(0, n, w)]
    return parts[0] if len(parts) == 1 else jnp.concatenate(parts, axis=1)


def _rope(x, cos, sin_signed):
    n = x.shape[-1]
    lane = lax.broadcasted_iota(jnp.int32, x.shape, 1)
    first_half = (lane % DH) < (DH // 2)
    partner = jnp.where(first_half, pltpu.roll(x, n - DH // 2, 1), pltpu.roll(x, DH // 2, 1))
    spread = lambda tab: tab if n == 128 else jnp.concatenate([tab] * (n // 128), axis=1)
    return x * spread(cos) + partner * spread(sin_signed)


def _adaln_kernel(c_ref, w0_ref, b0_ref, w1_ref, b1_ref, o_ref, *, steps):
    s = _silu(c_ref[...])
    j = pl.program_id(0)

    @pl.when(j < steps)
    def _():
        o_ref[0] = _dot(s, w0_ref[...]) + b0_ref[...]

    @pl.when(j >= steps)
    def _():
        o_ref[0] = _dot(s, w1_ref[...]) + b1_ref[...]


def _adaln(cond8, w0, b0, w1, b1):
    n = w0.shape[1]
    tn = 1536
    steps = n // tn
    first = lambda j: (0, jnp.minimum(j, steps - 1))
    second = lambda j: (0, jnp.maximum(j - steps, 0))
    return pl.pallas_call(
        functools.partial(_adaln_kernel, steps=steps),
        out_shape=jax.ShapeDtypeStruct((2, 8, n), F32),
        grid=(2 * steps,),
        in_specs=[pl.BlockSpec((8, D_MODEL), lambda j: (0, 0)),
                  pl.BlockSpec((D_MODEL, tn), first), pl.BlockSpec((1, tn), first),
                  pl.BlockSpec((D_MODEL, tn), second), pl.BlockSpec((1, tn), second)],
        out_specs=pl.BlockSpec((1, 8, tn), lambda j: (j // steps, 0, j % steps)),
        compiler_params=_params(("arbitrary",)),
        name="adaln",
    )(cond8, w0, b0.reshape(1, n), w1, b1.reshape(1, n))


def _const_spec(shape):
    nd = len(shape)
    return pl.BlockSpec(shape, lambda i, _n=nd: (0,) * _n, pipeline_mode=pl.Buffered(1))


def _mod_spec(chunk, rowfn):
    return pl.BlockSpec((1, 1, D_MODEL), lambda i: (rowfn(i) * ADA_CHUNKS + chunk, 0, 0))


def _rowfn(latent):
    if latent:
        return lambda i: 1 + (i * TOKEN_TILE) // LAT_LEN
    return lambda i: 0


def _tok_spec(width, blk0=0):
    return pl.BlockSpec((TOKEN_TILE, width), lambda i: (i + blk0, 0))


def _seq_len(latent):
    return LAT_LEN if latent else CTX_LEN


def _transposed_out(t, latent, rows):
    seq = _seq_len(latent)
    shape = jax.ShapeDtypeStruct((t // seq, rows, seq), F32)
    if seq <= TOKEN_TILE:
        per = TOKEN_TILE // seq
        return shape, pl.BlockSpec((per, rows, seq), lambda i: (i, 0, 0))
    per = seq // TOKEN_TILE
    return shape, pl.BlockSpec((1, rows, TOKEN_TILE), lambda i: (i // per, 0, i % per))


def _store_transposed(ref, x):
    n, _, w = ref.shape
    for j in range(n):
        ref[j] = x[j * w:(j + 1) * w].T


def _dot_colsT(w_cols, h):
    return lax.dot_general(w_cols, h, (((0,), (1,)), ((), ())), preferred_element_type=F32)


def _store_proj_cols_transposed(refs, w_cols, h):
    n, _, w = refs[0].shape
    for j in range(n):
        out_t = _dot_colsT(w_cols, h[j * w:(j + 1) * w])
        r0 = 0
        for ref in refs:
            ref[j] = out_t[r0:r0 + ref.shape[1]]
            r0 += ref.shape[1]


def _modnorm(x, ng_ref, sh_ref, sc_ref):
    return _rms(x, ng_ref[...] * (1.0 + sc_ref[0])) + sh_ref[0]


def _load_weights_bf16(pairs, stage, sem):
    tr, tc = W_TILE
    tiles = [(src, dst, r, c) for src, dst in pairs
             for r in range(0, src.shape[0], tr) for c in range(0, src.shape[1], tc)]

    def copy(k):
        src, _, r, c = tiles[k]
        return pltpu.make_async_copy(src.at[r:r + tr, c:c + tc], stage.at[k % W_SLOTS], sem.at[k % W_SLOTS])

    for k in range(min(W_SLOTS, len(tiles))):
        copy(k).start()
    for k, (_, dst, r, c) in enumerate(tiles):
        copy(k).wait()
        dst[r:r + tr, c:c + tc] = stage[k % W_SLOTS].astype(BF16)
        if k + W_SLOTS < len(tiles):
            copy(k + W_SLOTS).start()


def _ffn_kernel(*refs, n_ctx, first, has_mix, last):
    it = iter(refs)
    xs = (next(it), next(it)) if first else (next(it),)
    if has_mix:
        yac_ref, ybc_ref, yal_ref, ybl_ref, wo_hbm, g2_ref = (next(it) for _ in range(6))
    ng_ref, sh_ref, sc_ref, g_ref, win_hbm, wout_hbm = (next(it) for _ in range(6))
    if last:
        nf_ref = next(it)
    outs = (next(it), next(it)) if last else (next(it),)
    win_s, wout_s = next(it), next(it)
    if has_mix:
        wo_s = next(it)
    stage, sem = next(it), next(it)

    step = pl.program_id(0)
    is_lat = step >= n_ctx

    @pl.when(step == 0)
    def _():
        pairs = [(win_hbm, win_s), (wout_hbm, wout_s)]
        if has_mix:
            pairs.insert(0, (wo_hbm, wo_s))
        _load_weights_bf16(pairs, stage, sem)

    pick = lambda c_ref, l_ref: jnp.where(is_lat, l_ref[...], c_ref[...])
    x = pick(*xs) if first else xs[0][...]
    if has_mix:
        half = wo_s.shape[0] // 2
        mo = _dot(pick(yac_ref, yal_ref), wo_s[:half, :]) + _dot(pick(ybc_ref, ybl_ref), wo_s[half:, :])
        x = x + g2_ref[0] * mo
    h = _modnorm(x, ng_ref, sh_ref, sc_ref).astype(BF16)
    acc = None
    for lo in range(0, D_FF, FFN_CHUNK):
        hi = min(lo + FFN_CHUNK, D_FF)
        g = _dot(h, win_s[:, lo:hi])
        u = _dot(h, win_s[:, D_FF + lo:D_FF + hi])
        part = _dot(_silu(g) * u, wout_s[lo:hi, :])
        acc = part if acc is None else acc + part
    y = x + (0.5 * g_ref[0]) * acc
    if last:
        y = _rms(y, nf_ref[...])
        oc_ref, ol_ref = outs

        @pl.when(jnp.logical_not(is_lat))
        def _():
            oc_ref[...] = y
            ol_ref[...] = jnp.zeros_like(ol_ref)

        @pl.when(is_lat)
        def _():
            ol_ref[...] = y
    else:
        outs[0][...] = y


def _ffn(x, mods, mod_base, ng, w_in, w_out, *, t_ctx, t_lat, mix=None, final_g=None):
    tile = TOKEN_TILE
    n_ctx, n_lat = t_ctx // tile, t_lat // tile
    first, last = isinstance(x, tuple), final_g is not None
    ctx_map = lambda i: (jnp.minimum(i, n_ctx - 1), 0)
    lat_map = lambda i: (jnp.maximum(i - n_ctx, 0), 0)
    rowfn = lambda i: jnp.where(i < n_ctx, 0, 1 + ((i - n_ctx) * tile) // LAT_LEN)
    two = lambda wd: [pl.BlockSpec((tile, wd), ctx_map), pl.BlockSpec((tile, wd), lat_map)]
    one = pl.BlockSpec((tile, D_MODEL), lambda i: (i, 0))
    hbm = pl.BlockSpec(memory_space=pl.ANY)
    args = list(x) if first else [x]
    specs = two(D_MODEL) if first else [one]
    scratch = [pltpu.VMEM(w_in.shape, BF16), pltpu.VMEM(w_out.shape, BF16)]
    if mix is not None:
        yac, ybc, yal, ybl, wo = mix
        args += [yac, ybc, yal, ybl, wo, mods]
        mw = yac.shape[1]
        specs += [pl.BlockSpec((tile, mw), ctx_map)] * 2 + [pl.BlockSpec((tile, mw), lat_map)] * 2
        specs += [hbm, _mod_spec(mod_base - 1, rowfn)]
        scratch.append(pltpu.VMEM(wo.shape, BF16))
    args += [ng.reshape(1, D_MODEL), mods, mods, mods, w_in, w_out]
    specs += [_const_spec((1, D_MODEL)), _mod_spec(mod_base, rowfn), _mod_spec(mod_base + 1, rowfn),
              _mod_spec(mod_base + 2, rowfn), hbm, hbm]
    if last:
        args.append(final_g.reshape(1, D_MODEL))
        specs.append(_const_spec((1, D_MODEL)))
        out_shape = [jax.ShapeDtypeStruct((t_ctx, D_MODEL), F32), jax.ShapeDtypeStruct((t_lat, D_MODEL), F32)]
        out_specs = two(D_MODEL)
    else:
        out_shape = jax.ShapeDtypeStruct((t_ctx + t_lat, D_MODEL), F32)
        out_specs = one
    scratch += [pltpu.VMEM((W_SLOTS,) + W_TILE, F32), pltpu.SemaphoreType.DMA((W_SLOTS,))]
    return pl.pallas_call(
        functools.partial(_ffn_kernel, n_ctx=n_ctx, first=first, has_mix=mix is not None, last=last),
        out_shape=out_shape,
        grid=(n_ctx + n_lat,),
        in_specs=specs,
        out_specs=out_specs,
        scratch_shapes=scratch,
        compiler_params=_params(("arbitrary",), FFN_VMEM_LIMIT),
        name="ffn",
    )(*args)


_EV_QA, _EV_KA, _EV_VA, _EV_QB, _EV_KB, _EV_VB, _EV_OB = 0, 512, 640, 768, 1280, 1792, 2304
N_DIRHEAD = 2 * N_HEADS


def _proj_even_kernel(*refs, rope):
    it = iter(refs)
    x_ref, ng_ref, sh_ref, sc_ref, w_ref, wob_ref, wg_ref, gbi_ref, gbf_ref, gbt_ref, qg_ref, kg_ref, bd_ref = (
        next(it) for _ in range(13))
    if rope:
        cos_ref, sin_ref = next(it), next(it)
    qa_ref, kat_ref, vat_ref, qb_ref, vb_ref, ob_ref, gi_ref, gf_ref, gt_ref, kbt_ref = (next(it) for _ in range(10))

    h = _modnorm(x_ref[...], ng_ref, sh_ref, sc_ref).astype(BF16)
    proj = lambda lo, hi: _dot(h, w_ref[:, lo:hi])

    qa = proj(_EV_QA, _EV_KA)
    qa = qa * lax.rsqrt(_head_mean_sq(qa, bd_ref) + EPS) * qg_ref[...]
    ka = proj(_EV_KA, _EV_VA)
    ka = ka * lax.rsqrt(_head_mean_sq(ka, bd_ref) + EPS) * kg_ref[...]
    if rope:
        qa = _rope(qa, cos_ref[...], sin_ref[...])
        ka = _rope(ka, cos_ref[...], sin_ref[...])
    qa_ref[...] = qa * ATTN_SCALE
    _store_transposed(kat_ref, ka)
    _store_proj_cols_transposed((vat_ref,), w_ref[:, _EV_VA:_EV_QB], h)
    qb_ref[...] = proj(_EV_QB, _EV_KB)
    vb_ref[...] = proj(_EV_VB, _EV_OB)
    ob_ref[...] = _dot(h, wob_ref[...])
    gates = _dot(h, wg_ref[...])
    gi_ref[...] = gates[:, :128] + gbi_ref[...]
    gf_ref[...] = gates[:, 128:] + gbf_ref[...]
    for j in range(TOKEN_TILE // MLSTM_T):
        ht = h[j * MLSTM_T:(j + 1) * MLSTM_T]
        gates_t = [_dot_colsT(wg_ref[:, c0:c0 + N_DIRHEAD], ht) for c0 in (0, 128)]
        gt_ref[j] = jnp.concatenate(gates_t, axis=0) + gbt_ref[...]
        kbt_ref[j] = _dot_colsT(w_ref[:, _EV_KB:_EV_VB], ht) * ATTN_SCALE


def _proj_even(x, t, blk0, mods, latent, ng, w, w_ob, w_g, gbi, gbf, gbt, qg, kg, bd, rope_tabs):
    rowfn = _rowfn(latent)
    args = [x, ng.reshape(1, D_MODEL), mods, mods, w, w_ob, w_g, gbi, gbf, gbt, qg, kg, bd]
    specs = [_tok_spec(D_MODEL, blk0), _const_spec((1, D_MODEL)), _mod_spec(3, rowfn), _mod_spec(4, rowfn)]
    specs += [_const_spec(a.shape) for a in args[4:]]
    if latent:
        nblk = LAT_LEN // TOKEN_TILE
        args += list(rope_tabs)
        specs += [pl.BlockSpec((TOKEN_TILE, 128), lambda i: (i % nblk, 0))] * 2
    widths = (512, 512, 512, 512, 128, 128)
    out_shape = [jax.ShapeDtypeStruct((t, wd), F32) for wd in widths]
    out_specs = [_tok_spec(wd) for wd in widths]
    for pos in (1, 2):
        shape, spec = _transposed_out(t, latent, 128)
        out_shape.insert(pos, shape)
        out_specs.insert(pos, spec)
    nj = TOKEN_TILE // MLSTM_T
    for rows in (2 * N_DIRHEAD, 512):
        out_shape.append(jax.ShapeDtypeStruct((t // MLSTM_T, rows, MLSTM_T), F32))
        out_specs.append(pl.BlockSpec((nj, rows, MLSTM_T), lambda i: (i, 0, 0)))
    return pl.pallas_call(
        functools.partial(_proj_even_kernel, rope=latent),
        out_shape=out_shape, grid=(t // TOKEN_TILE,), in_specs=specs, out_specs=out_specs,
        compiler_params=_params(("arbitrary",)), name="proj_even",
    )(*args)


def _proj_odd_kernel(*refs, rope):
    it = iter(refs)
    x_ref, ng_ref, sh_ref, sc_ref, w_ref = (next(it) for _ in range(5))
    if rope:
        cos_ref, sin_ref = next(it), next(it)
    qc_ref, kc_ref, vc_ref, qd_ref, kd_ref, vd_ref = (next(it) for _ in range(6))

    h = _modnorm(x_ref[...], ng_ref, sh_ref, sc_ref).astype(BF16)
    proj = lambda lo, hi: _dot(h, w_ref[:, lo:hi])
    qc_ref[...] = proj(0, 512) * ATTN_SCALE
    if rope:
        kc_ref[...] = proj(512, 1024)
        vc_ref[...] = proj(1024, 1536)
    else:
        _store_proj_cols_transposed((kc_ref, vc_ref), w_ref[:, 512:1536], h)
    qd = proj(1536, 2048)
    if rope:
        qd = _rope(qd, cos_ref[...], sin_ref[...])
        _store_transposed(kd_ref, _rope(proj(2048, 2176), cos_ref[...], sin_ref[...]))
        _store_proj_cols_transposed((vd_ref,), w_ref[:, 2176:2304], h)
    else:
        _store_proj_cols_transposed((kd_ref, vd_ref), w_ref[:, 2048:2304], h)
    qd_ref[...] = qd * ATTN_SCALE


def _proj_odd(x, t, blk0, mods, latent, ng, w, rope_tabs):
    rowfn = _rowfn(latent)
    args = [x, ng.reshape(1, D_MODEL), mods, mods, w]
    specs = [_tok_spec(D_MODEL, blk0), _const_spec((1, D_MODEL)), _mod_spec(3, rowfn), _mod_spec(4, rowfn),
             _const_spec(w.shape)]
    if latent:
        nblk = LAT_LEN // TOKEN_TILE
        args += list(rope_tabs)
        specs += [pl.BlockSpec((TOKEN_TILE, 128), lambda i: (i % nblk, 0))] * 2
    out_shape = [jax.ShapeDtypeStruct((t, 512), F32)] * 4
    out_specs = [_tok_spec(512)] * 4
    if not latent:
        out_shape[1], out_specs[1] = _transposed_out(t, latent, 512)
        out_shape[2], out_specs[2] = out_shape[1], out_specs[1]
    for _ in range(2):
        shape, spec = _transposed_out(t, latent, 128)
        out_shape.append(shape)
        out_specs.append(spec)
    return pl.pallas_call(
        functools.partial(_proj_odd_kernel, rope=latent),
        out_shape=out_shape,
        grid=(t // TOKEN_TILE,), in_specs=specs, out_specs=out_specs,
        compiler_params=_params(("arbitrary",)), name="proj_odd",
    )(*args)


def _attn_kernel(*refs, group, has_extra, has_sink, spb):
    it = iter(refs)
    q_ref, k_ref, v_ref = next(it), next(it), next(it)
    if has_extra:
        ke_ref, ve_ref = next(it), next(it)
    if has_sink:
        sink_ref = next(it)
    o_ref = next(it)
    tq = q_ref.shape[0] // spb
    first = lax.broadcasted_iota(jnp.int32, (tq, 128), 1) < DH

    def kv_operands(kt_ref, vt_ref, smp, kv):
        kt = kt_ref[smp, kv * DH:(kv + 1) * DH, :].astype(BF16)
        vt = vt_ref[smp, kv * DH:(kv + 1) * DH, :].astype(BF16)
        z = jnp.zeros_like(kt)
        ones = jnp.ones((128, kt.shape[1]), BF16)
        return ((jnp.concatenate([kt, z], axis=0), jnp.concatenate([z, kt], axis=0)),
                (jnp.concatenate([vt, z, ones], axis=0), jnp.concatenate([z, vt, ones], axis=0)))

    for smp in range(spb):
        qrows = slice(smp * tq, (smp + 1) * tq)
        cache = {}
        for j in range(N_HEADS // 2):
            ps = slice(128 * j, 128 * (j + 1))
            qb = q_ref[qrows, ps].astype(BF16)
            res, sink_den = [], []
            for half in range(2):
                kv = (2 * j + half) // group
                if kv not in cache:
                    cache[kv] = (kv_operands(k_ref, v_ref, smp, kv),
                                 kv_operands(ke_ref, ve_ref, smp, kv) if has_extra else None)
                (k_ops, v_ops), extra = cache[kv]
                s = jnp.dot(qb, k_ops[half], preferred_element_type=F32)
                m = jnp.max(s, axis=-1, keepdims=True)
                if has_extra:
                    se = jnp.dot(qb, extra[0][half], preferred_element_type=F32)
                    m = jnp.maximum(m, jnp.max(se, axis=-1, keepdims=True))
                if has_sink:
                    m = jnp.maximum(m, sink_ref[2 * j + half])
                    sink_den.append(jnp.exp(sink_ref[2 * j + half] - m))
                r = _dot_nt(jnp.exp(s - m), v_ops[half])
                if has_extra:
                    r = r + _dot_nt(jnp.exp(se - m), extra[1][half])
                res.append(r)
            num = jnp.where(first, res[0][:, :128], res[1][:, :128])
            den = jnp.where(first, res[0][:, 128:], res[1][:, 128:])
            if has_sink:
                den = den + jnp.where(first, sink_den[0], sink_den[1])
            o_ref[qrows, ps] = num / den


def _attn_two_kernel(qc_ref, kc_ref, vc_ref, qd_ref, kd_ref, vd_ref, sink_ref, oc_ref, od_ref, *, spb):
    _attn_kernel(qc_ref, kc_ref, vc_ref, oc_ref, group=1, has_extra=False, has_sink=False, spb=spb)
    _attn_kernel(qd_ref, kd_ref, vd_ref, sink_ref, od_ref, group=4, has_extra=False, has_sink=True, spb=spb)


def _attention_two(qc, kc, vc, qd, kd, vd, sink, *, n_batch, q_len):
    spb = 4
    q_spec = pl.BlockSpec((spb * q_len, 512), lambda b: (b, 0))
    kv_spec = lambda a: pl.BlockSpec((spb,) + a.shape[1:], lambda b: (b, 0, 0))
    return pl.pallas_call(
        functools.partial(_attn_two_kernel, spb=spb),
        out_shape=[jax.ShapeDtypeStruct(qc.shape, F32), jax.ShapeDtypeStruct(qd.shape, F32)],
        grid=(n_batch // spb,),
        in_specs=[q_spec, kv_spec(kc), kv_spec(vc), q_spec, kv_spec(kd), kv_spec(vd),
                  pl.BlockSpec(memory_space=pltpu.SMEM)],
        out_specs=[q_spec, q_spec],
        compiler_params=_params(("arbitrary",)), name="attn_two",
    )(qc, kc, vc, qd, kd, vd, sink)


def _attention(q, k, v, *, n_batch, q_len, q_tile, group, extra=None, sink=None):
    nq = q_len // q_tile
    spb = next(n for n in (8, 4, 2, 1) if n_batch % n == 0) if nq == 1 else 1
    kv_spec = lambda a: pl.BlockSpec((spb,) + a.shape[1:], lambda b, j: (b, 0, 0))
    args = [q, k, v]
    specs = [pl.BlockSpec((spb * q_tile, 512), lambda b, j: (b * nq + j, 0)), kv_spec(k), kv_spec(v)]
    if extra is not None:
        args += list(extra)
        specs += [kv_spec(extra[0]), kv_spec(extra[1])]
    if sink is not None:
        args.append(sink)
        specs.append(pl.BlockSpec(memory_space=pltpu.SMEM))
    return pl.pallas_call(
        functools.partial(_attn_kernel, group=group, has_extra=extra is not None, has_sink=sink is not None, spb=spb),
        out_shape=jax.ShapeDtypeStruct(q.shape, F32),
        grid=(n_batch // spb, nq), in_specs=specs,
        out_specs=pl.BlockSpec((spb * q_tile, 512), lambda b, j: (b * nq + j, 0)),
        compiler_params=_params(("arbitrary", "arbitrary")), name="attn",
    )(*args)


LOG2E = 1.4426950408889634


def _cummax_time(x, reverse):
    n = x.shape[0]
    row = lax.broadcasted_iota(jnp.int32, x.shape, 0)
    k = 1
    while k < n:
        if reverse:
            shifted = jnp.where(row < n - k, pltpu.roll(x, n - k, 0), -jnp.inf)
        else:
            shifted = jnp.where(row >= k, pltpu.roll(x, k, 0), -jnp.inf)
        x = jnp.maximum(x, shifted)
        k *= 2
    return x


def _mlstm_kernel(*refs, n_chunks, carry, spb):
    it = iter(refs)
    q_ref, kt_ref, v_ref, ob_ref, gi_ref, gf_ref, gt_ref, hg_ref, bd_ref = (next(it) for _ in range(9))
    if carry:
        c0_ref, n0_ref, m0r_ref, m0c_ref = (next(it) for _ in range(4))
    y_ref = next(it)
    if not carry:
        c_ref, n_ref, m_ref = next(it), next(it), next(it)
    hf_s, hb_s = next(it), next(it)
    if carry:
        st_s, mrow_s, mcol_s = next(it), next(it), next(it)
    T = MLSTM_T
    step = pl.program_id(1)

    if carry:
        @pl.when(step == 0)
        def _():
            z = jnp.zeros((DH, DH), F32)
            for d in range(2):
                for j in range(N_HEADS // 2):
                    ca, cb = c0_ref[0, d, 2 * j], c0_ref[0, d, 2 * j + 1]
                    na, nb = n0_ref[0, d, 2 * j][:, :DH], n0_ref[0, d, 2 * j + 1][:, :DH]
                    st_s[d, j, 0:DH, :] = jnp.concatenate([ca, z, na, z], axis=1)
                    st_s[d, j, DH:2 * DH, :] = jnp.concatenate([z, cb, z, nb], axis=1)
            mrow_s[...] = m0r_ref[0]
            mcol_s[...] = m0c_ref[0]

    row = lax.broadcasted_iota(jnp.int32, (T, T), 0)
    col = lax.broadcasted_iota(jnp.int32, (T, T), 1)
    lower = row >= col
    upper = row <= col
    lower_m = jnp.where(lower, 1.0, 0.0).astype(BF16)
    upper_m = jnp.where(upper, 1.0, 0.0).astype(BF16)
    lane = lax.broadcasted_iota(jnp.int32, (T, 128), 1)
    first = lane < DH
    lane_c = lax.broadcasted_iota(jnp.int32, (DH, 256), 1) % 128
    zeros_kt = jnp.zeros((DH, T), BF16)
    ones_v = jnp.ones((T, 128), BF16)

    for smp, d in ((s_, d_) for s_ in range(spb) for d_ in range(2)):
        chunk = (step if d == 0 else n_chunks - 1 - step) if carry else smp
        rows = pl.ds(pl.multiple_of(chunk * T, T), T)
        hd0 = N_HEADS * d
        mask = lower if d == 0 else upper

        if carry or d == 0:
            gi = gi_ref[rows, :]
            lf = _log_sigmoid(gf_ref[rows, :])
            prefix = _dot_exact_rhs(lower_m, lf)
        b_cols = prefix if d == 0 else prefix[T - 1:T, :] - prefix + lf
        a_cols = gi - b_cols
        dmax = b_cols + _cummax_time(a_cols, reverse=d == 1)
        if carry:
            inter = b_cols + mrow_s[0:1, :]
            mt = jnp.maximum(inter, dmax)
            w_inter = jnp.exp(inter - mt)
        else:
            mt = jnp.maximum(b_cols, dmax)
        c2 = (b_cols - mt) * LOG2E
        einv = jnp.exp(-mt)

        gt = gt_ref[chunk]
        gi_t = gt[hd0:hd0 + N_HEADS, :]
        gf_t = gt[N_DIRHEAD + hd0:N_DIRHEAD + hd0 + N_HEADS, :]
        lf_t = _log_sigmoid(gf_t)
        b_rows = _dot_exact_lhs(lf_t, upper_m)
        if d == 1:
            b_rows = b_rows[:, T - 1:T] - b_rows + lf_t
        a_rows = gi_t - b_rows
        a2 = a_rows * LOG2E
        b_last = b_rows[:, T - 1:T] if d == 0 else b_rows[:, 0:1]
        dec_rows = b_last + a_rows
        m_new = jnp.max(dec_rows, axis=-1, keepdims=True)
        if carry:
            m_prev = mcol_s[hd0:hd0 + N_HEADS, 0:1]
            m_new = jnp.maximum(m_new, b_last + m_prev)
            wc = jnp.exp(b_last + m_prev - m_new)
        else:
            m_new = jnp.maximum(m_new, b_last)
        ws_rows = jnp.exp(dec_rows - m_new)

        h_s = hf_s if d == 0 else hb_s
        for j in range(N_HEADS // 2):
            ps = slice(128 * j, 128 * (j + 1))
            q_pair = q_ref[rows, ps].astype(BF16)
            v_aug = jnp.concatenate([v_ref[rows, ps].astype(BF16), ones_v], axis=1)
            kt = (kt_ref[chunk, 128 * j:128 * j + DH, :], kt_ref[chunk, 128 * j + DH:128 * (j + 1), :])
            lhs = []
            for half in range(2):
                hd = 2 * j + half
                kt0 = jnp.concatenate([kt[0].astype(BF16), zeros_kt] if half == 0 else [zeros_kt, kt[1].astype(BF16)], axis=0)
                qk = jnp.dot(q_pair, kt0, preferred_element_type=F32)
                e = c2[:, hd0 + hd:hd0 + hd + 1] + a2[hd:hd + 1, :]
                lhs.append((jnp.exp2(jnp.where(mask, e, -jnp.inf)) * qk).astype(BF16))
            for half in range(2):
                lhs.append((kt[half] * ws_rows[2 * j + half:2 * j + half + 1, :]).astype(BF16))
            both = jnp.dot(jnp.concatenate(lhs, axis=0), v_aug, preferred_element_type=F32)
            res = (both[0:T], both[T:2 * T])
            cu = (both[2 * T:2 * T + DH], both[2 * T + DH:2 * T + 2 * DH])
            ra, rb = hd0 + 2 * j, hd0 + 2 * j + 1
            num = jnp.where(first, res[0][:, :128], res[1][:, :128])
            den = jnp.where(first, res[0][:, 128:], res[1][:, 128:])
            if carry:
                old = st_s[d, j]
                qs = _dot(q_pair, old)
                wi = jnp.where(first, w_inter[:, ra:ra + 1], w_inter[:, rb:rb + 1])
                num = num + wi * qs[:, :128]
                den = den + wi * qs[:, 128:]
            floor = jnp.where(first, einv[:, ra:ra + 1], einv[:, rb:rb + 1])
            h_s[rows, ps] = num / jnp.maximum(jnp.abs(den), floor)
            if carry:
                st_s[d, j, 0:DH, :] = jnp.where(lane_c < DH, cu[0], 0.0) + wc[2 * j:2 * j + 1, :] * old[0:DH]
                st_s[d, j, DH:2 * DH, :] = jnp.where(lane_c >= DH, cu[1], 0.0) + wc[2 * j + 1:2 * j + 2, :] * old[DH:2 * DH]
            else:
                c_ref[smp, d, 2 * j] = cu[0][:, 0:DH]
                c_ref[smp, d, 2 * j + 1] = cu[1][:, DH:2 * DH]

        if carry:
            b_last_l = b_cols[T - 1:T, :] if d == 0 else b_cols[0:1, :]
            m_prev_l = mrow_s[0:1, :]
            m_new_l = jnp.maximum(jnp.max(b_last_l + a_cols, axis=0, keepdims=True), b_last_l + m_prev_l)
            lane_r = lax.broadcasted_iota(jnp.int32, (1, 128), 1)
            mine = (lane_r >= hd0) & (lane_r < hd0 + N_HEADS)
            mrow_s[0:1, :] = jnp.where(mine, m_new_l, m_prev_l)
            mcol_s[hd0:hd0 + N_HEADS, :] = jnp.broadcast_to(m_new, (N_HEADS, 128))
        else:
            for hd in range(N_HEADS):
                n_hd = _dot_nt(ws_rows, kt_ref[chunk, hd * DH:(hd + 1) * DH, :])
                n_ref[smp, d, hd:hd + 1, :] = n_hd[hd:hd + 1, :]
            m_ref[smp, hd0:hd0 + N_HEADS, :] = jnp.broadcast_to(m_new, (N_HEADS, 128))

    @pl.when(step == n_chunks - 1)
    def _():
        hsum = hf_s[...] + hb_s[...]
        y = hsum * lax.rsqrt(_head_mean_sq(hsum, bd_ref) + EPS) * hg_ref[...]
        y_ref[...] = jax.nn.sigmoid(ob_ref[...]) * y


def _mlstm(qb, kbt, vb, ob, gi, gf, gt, head_g, bd, *, n_batch, seq_len, state=None):
    n_chunks = seq_len // MLSTM_T
    carry = state is not None
    assert carry or n_chunks == 1
    spb = 1
    seq = lambda wd: pl.BlockSpec((spb * seq_len, wd), lambda b, c: (b, 0))
    per_chunk = lambda r: pl.BlockSpec((spb * n_chunks, r, MLSTM_T), lambda b, c: (b, 0, 0))
    args = [qb, kbt, vb, ob, gi, gf, gt, head_g, bd]
    specs = [seq(512), per_chunk(512), seq(512), seq(512), seq(128), seq(128), per_chunk(2 * N_DIRHEAD),
             pl.BlockSpec((1, 512), lambda b, c: (0, 0)), pl.BlockSpec(bd.shape, lambda b, c: (0, 0))]
    out_shape = [jax.ShapeDtypeStruct((n_batch * seq_len, 512), F32)]
    out_specs = [seq(512)]
    scratch = [pltpu.VMEM((spb * seq_len, 512), F32), pltpu.VMEM((spb * seq_len, 512), F32)]
    if carry:
        args += list(state)
        specs += [pl.BlockSpec((1, 2, N_HEADS, DH, DH), lambda b, c: (b, 0, 0, 0, 0)),
                  pl.BlockSpec((1, 2, N_HEADS, DH, 128), lambda b, c: (b, 0, 0, 0, 0)),
                  pl.BlockSpec((1, 8, 128), lambda b, c: (b, 0, 0)),
                  pl.BlockSpec((1, N_DIRHEAD, 128), lambda b, c: (b, 0, 0))]
        scratch += [pltpu.VMEM((2, N_HEADS // 2, 128, 256), F32), pltpu.VMEM((8, 128), F32),
                    pltpu.VMEM((N_DIRHEAD, 128), F32)]
    else:
        out_shape += [jax.ShapeDtypeStruct((n_batch, 2, N_HEADS, DH, DH), F32),
                      jax.ShapeDtypeStruct((n_batch, 2, N_HEADS, DH), F32),
                      jax.ShapeDtypeStruct((n_batch, N_DIRHEAD, 128), F32)]
        out_specs += [pl.BlockSpec((spb, 2, N_HEADS, DH, DH), lambda b, c: (b, 0, 0, 0, 0)),
                      pl.BlockSpec((spb, 2, N_HEADS, DH), lambda b, c: (b, 0, 0, 0)),
                      pl.BlockSpec((spb, N_DIRHEAD, 128), lambda b, c: (b, 0, 0))]
    return pl.pallas_call(
        functools.partial(_mlstm_kernel, n_chunks=n_chunks, carry=carry, spb=spb),
        out_shape=out_shape, grid=(n_batch // spb, n_chunks), in_specs=specs, out_specs=out_specs,
        scratch_shapes=scratch,
        compiler_params=_params(("arbitrary", "arbitrary")), name="mlstm",
    )(*args)


N_ROWS = LAT_LEN // GRID_W
N_DY = 2 * NA_KH - 1
N_DX = 2 * NA_KW - 1


NA_QROWS = 4
NA_KROWS = 12
NA_GROUPS = N_ROWS // NA_QROWS


def _na_kernel(q_ref, k_ref, v_ref, ke_ref, ve_ref, rpb_ref, o_ref, tile_s, slab_s):
    pair = pl.program_id(1)
    qn, kn = NA_QROWS * GRID_W, NA_KROWS * GRID_W

    @pl.when(pl.program_id(0) == 0)
    def _():
        qc = lax.broadcasted_iota(jnp.int32, (GRID_W, GRID_W), 0)
        kc = lax.broadcasted_iota(jnp.int32, (GRID_W, GRID_W), 1)
        start = jnp.clip(qc - NA_KW // 2, 0, GRID_W - NA_KW)
        in_win = (kc >= start) & (kc < start + NA_KW)
        dx = kc - qc + (NA_KW - 1)
        blocked = jnp.full((GRID_W, GRID_W), NEG_INF, F32)
        for half in range(2):
            head = 2 * pair + half

            def build_tile(dy, carry):
                tile = blocked
                for j in range(N_DX):
                    tile = jnp.where(dx == j, rpb_ref[(head * N_DY + dy) * N_DX + j], tile)
                tile_s[half * N_DY + dy] = jnp.where(in_win, tile, NEG_INF)
                return carry

            lax.fori_loop(0, N_DY, build_tile, 0)
            for kind in range(3):
                idx = (pair * 2 + half) * 3 + kind
                for i in range(NA_QROWS):
                    for k in range(NA_KROWS):
                        if kind == 0:
                            ok, dy = k < NA_KH, k - i + NA_KH - 1
                        elif kind == 1:
                            ok, dy = i <= k < i + NA_KH, k - i + NA_KH // 2 - 1
                        else:
                            ok, dy = k >= NA_KROWS - NA_KH, k - i - 1
                        slab_s[idx, i * GRID_W:(i + 1) * GRID_W, k * GRID_W:(k + 1) * GRID_W] = (
                            tile_s[half * N_DY + dy] if ok else blocked)

    first = lax.broadcasted_iota(jnp.int32, (qn, 128), 1) < DH
    ones_k = jnp.ones((kn, 128), BF16)
    ke = ke_ref[...].astype(BF16)
    ve_aug = jnp.concatenate([ve_ref[...].astype(BF16), jnp.ones((ke.shape[0], 128), BF16)], axis=1)

    def group_body(g, carry):
        k0 = jnp.where(g < NA_GROUPS // 2, 0, N_ROWS - NA_KROWS)
        kind = jnp.where(g == 0, 0, jnp.where(g == NA_GROUPS - 1, 2, 1))
        qrows = pl.ds(pl.multiple_of(g * qn, qn), qn)
        keys = pl.ds(pl.multiple_of(k0 * GRID_W, GRID_W), kn)
        q_pair = q_ref[qrows, :]
        k_pair = k_ref[keys, :].astype(BF16)
        v_aug = jnp.concatenate([v_ref[keys, :].astype(BF16), ones_k], axis=1)
        res = []
        for half in range(2):
            qm = jnp.where(first, q_pair, 0.0) if half == 0 else jnp.where(first, 0.0, q_pair)
            s = _dot_nt(qm, k_pair) + slab_s[(pair * 2 + half) * 3 + kind]
            se = _dot_nt(qm, ke)
            m = jnp.maximum(jnp.max(s, axis=-1, keepdims=True), jnp.max(se, axis=-1, keepdims=True))
            res.append(_dot(jnp.exp(s - m), v_aug) + _dot(jnp.exp(se - m), ve_aug))
        num = jnp.where(first, res[0][:, :128], res[1][:, :128])
        den = jnp.where(first, res[0][:, 128:], res[1][:, 128:])
        o_ref[qrows, :] = num / den
        return carry

    lax.fori_loop(0, NA_GROUPS, group_body, 0, unroll=True)


def _na_latent(q, k, v, k_ctx, v_ctx, rpb_flat, n_batch):
    seq = pl.BlockSpec((LAT_LEN, 128), lambda b, j: (b, j))
    ctx = pl.BlockSpec((CTX_LEN, 128), lambda b, j: (b, j))
    return pl.pallas_call(
        _na_kernel,
        out_shape=jax.ShapeDtypeStruct(q.shape, F32),
        grid=(n_batch, N_HEADS // 2),
        in_specs=[seq, seq, seq, ctx, ctx, pl.BlockSpec(memory_space=pltpu.SMEM)],
        out_specs=seq,
        scratch_shapes=[pltpu.VMEM((2 * N_DY, GRID_W, GRID_W), F32),
                        pltpu.VMEM((N_HEADS * 3, NA_QROWS * GRID_W, NA_KROWS * GRID_W), F32)],
        compiler_params=_params(("arbitrary", "arbitrary")), name="na_latent",
    )(q, k, v, k_ctx, v_ctx, rpb_flat)


SWA_QT = 256
SWA_SPAN = SWA_QT + 2 * SWA_WIN


def _swa_kernel(q_ref, kt_ref, vt_ref, ket_ref, vet_ref, sink_ref, o_ref):
    pair = pl.program_id(1)
    first = lax.broadcasted_iota(jnp.int32, (SWA_QT, 128), 1) < DH
    row = lax.broadcasted_iota(jnp.int32, (SWA_QT, SWA_SPAN), 0)
    col = lax.broadcasted_iota(jnp.int32, (SWA_QT, SWA_SPAN), 1)
    z = jnp.zeros((DH, SWA_SPAN), BF16)
    ones = jnp.ones((128, SWA_SPAN), BF16)
    ket, vet = ket_ref[0].astype(BF16), vet_ref[0].astype(BF16)
    ze = jnp.zeros_like(ket)
    ones_e = jnp.ones((128, ket.shape[1]), BF16)
    sinks = (sink_ref[2 * pair], sink_ref[2 * pair + 1])
    for g in range(LAT_LEN // SWA_QT):
        q0 = g * SWA_QT
        lo = min(max(q0 - SWA_WIN, 0), LAT_LEN - SWA_SPAN)
        near = jnp.abs((lo + col) - (q0 + row)) <= SWA_WIN
        qb = q_ref[q0:q0 + SWA_QT, :].astype(BF16)
        kt = kt_ref[0, :, lo:lo + SWA_SPAN].astype(BF16)
        vt = vt_ref[0, :, lo:lo + SWA_SPAN].astype(BF16)
        res, sink_den = [], []
        for half in range(2):
            k_op = jnp.concatenate([kt, z] if half == 0 else [z, kt], axis=0)
            v_op = jnp.concatenate([vt, z, ones] if half == 0 else [z, vt, ones], axis=0)
            ke_op = jnp.concatenate([ket, ze] if half == 0 else [ze, ket], axis=0)
            ve_op = jnp.concatenate([vet, ze, ones_e] if half == 0 else [ze, vet, ones_e], axis=0)
            s = jnp.where(near, jnp.dot(qb, k_op, preferred_element_type=F32), NEG_INF)
            se = jnp.dot(qb, ke_op, preferred_element_type=F32)
            m = jnp.maximum(jnp.max(s, axis=-1, keepdims=True), jnp.max(se, axis=-1, keepdims=True))
            m = jnp.maximum(m, sinks[half])
            sink_den.append(jnp.exp(sinks[half] - m))
            res.append(_dot_nt(jnp.exp(s - m), v_op) + _dot_nt(jnp.exp(se - m), ve_op))
        num = jnp.where(first, res[0][:, :128], res[1][:, :128])
        den = jnp.where(first, res[0][:, 128:], res[1][:, 128:]) + jnp.where(first, sink_den[0], sink_den[1])
        o_ref[q0:q0 + SWA_QT, :] = num / den


def _swa_latent(q, kt, vt, kt_ctx, vt_ctx, sink, n_batch, group):
    ppk = group // 2
    kv_spec = lambda a: pl.BlockSpec((1, DH, a.shape[2]), lambda b, j: (b, j // ppk, 0))
    return pl.pallas_call(
        _swa_kernel,
        out_shape=jax.ShapeDtypeStruct(q.shape, F32),
        grid=(n_batch, N_HEADS // 2),
        in_specs=[pl.BlockSpec((LAT_LEN, 128), lambda b, j: (b, j)),
                  kv_spec(kt), kv_spec(vt), kv_spec(kt_ctx), kv_spec(vt_ctx),
                  pl.BlockSpec(memory_space=pltpu.SMEM)],
        out_specs=pl.BlockSpec((LAT_LEN, 128), lambda b, j: (b, j)),
        compiler_params=_params(("arbitrary", "arbitrary")), name="swa_latent",
    )(q, kt, vt, kt_ctx, vt_ctx, sink)


def _rope_tables():
    t = np.arange(LAT_LEN)
    pos = np.stack([t // GRID_W, t % GRID_W], axis=-1).astype(np.float32)
    freqs = np.float32(ROPE_THETA) ** (-np.arange(ROPE_FREQS, dtype=np.float32) / np.float32(ROPE_FREQS))
    ang = (pos[:, :, None] * freqs).reshape(LAT_LEN, 2 * ROPE_FREQS).astype(np.float32)
    cos, sin = np.cos(ang), np.sin(ang)
    cos_t = np.tile(np.concatenate([cos, cos], axis=-1), (1, 2))
    sin_t = np.tile(np.concatenate([-sin, sin], axis=-1), (1, 2))
    return jnp.asarray(cos_t, F32), jnp.asarray(sin_t, F32)


def kernel(x_prompt, x_sample, cache_l0_attn_k, cache_l0_attn_v, state_l0_mlstm_C, state_l0_mlstm_n, state_l0_mlstm_m, cache_l1_na_k, cache_l1_na_v, cache_l1_swa_k, cache_l1_swa_v, c, c_ctx, norm_final, ada_w_l0, ada_b_l0, norm_l0, ffn1_in_l0, ffn1_out_l0, ffn2_in_l0, ffn2_out_l0, mix_in_l0, mix_out_l0, qk_norm_l0, gate_bias_l0, head_norm_l0, ada_w_l1, ada_b_l1, norm_l1, ffn1_in_l1, ffn1_out_l1, ffn2_in_l1, ffn2_out_l1, mix_in_l1, mix_out_l1, rpb_l1, sink_l1):
    nb, nl = x_prompt.shape[0], x_sample.shape[0]
    bf = lambda a: a.astype(BF16)

    cond8 = jnp.zeros((8, D_MODEL), F32).at[0].set(c_ctx).at[1:1 + nl].set(c)
    mods = _adaln(cond8, ada_w_l0, ada_b_l0, ada_w_l1, ada_b_l1)
    mods0 = mods[0].reshape(8 * ADA_CHUNKS, 1, D_MODEL)
    mods1 = mods[1].reshape(8 * ADA_CHUNKS, 1, D_MODEL)

    g0 = 2304
    gcols = lambda j: mix_in_l0[:, g0 + 8 * j:g0 + 8 * (j + 1)]
    gpad = jnp.zeros((D_MODEL, 128 - N_DIRHEAD), F32)
    w_gi, w_gf = jnp.concatenate([gcols(0), gcols(2)], axis=1), jnp.concatenate([gcols(1), gcols(3)], axis=1)
    w_even, w_ob = bf(mix_in_l0[:, :g0]), bf(mix_in_l0[:, g0 + 32:])
    w_g = bf(jnp.concatenate([w_gi, gpad, w_gf, gpad], axis=1))
    gb4 = gate_bias_l0.reshape(4, N_HEADS)
    b_gi, b_gf = jnp.concatenate([gb4[0], gb4[2]]), jnp.concatenate([gb4[1], gb4[3]])
    gbi = jnp.zeros((1, 128), F32).at[0, :N_DIRHEAD].set(b_gi)
    gbf = jnp.zeros((1, 128), F32).at[0, :N_DIRHEAD].set(b_gf)
    gbt = jnp.concatenate([b_gi, b_gf]).reshape(2 * N_DIRHEAD, 1)
    qg = jnp.tile(qk_norm_l0[0], N_HEADS).reshape(1, 512)
    kg = jnp.tile(qk_norm_l0[1], 2).reshape(1, 128)
    head_g = head_norm_l0.reshape(1, 512)
    grp = np.arange(256) // DH
    bd = jnp.asarray((grp[:, None] == grp[None, :]).astype(np.float32) / DH, dtype=BF16)
    rope_tabs = _rope_tables()
    w_odd = bf(mix_in_l1)
    rpb_flat = rpb_l1.reshape(-1)
    to_t = lambda a: a.reshape(a.shape[0], a.shape[1], -1).transpose(0, 2, 1)
    from_t = lambda a: a.reshape(a.shape[0], -1, DH, a.shape[2]).transpose(0, 3, 1, 2)

    xp = x_prompt.reshape(nb * CTX_LEN, D_MODEL)
    xs = x_sample.reshape(nl * LAT_LEN, D_MODEL)
    t_ctx, t_lat = xp.shape[0], xs.shape[0]
    streams = ((False, t_ctx, 0, nb, CTX_LEN), (True, t_lat, t_ctx // TOKEN_TILE, nl, LAT_LEN))
    ffn = functools.partial(_ffn, t_ctx=t_ctx, t_lat=t_lat)

    x = ffn((xp, xs), mods0, 0, norm_l0[0], ffn1_in_l0, ffn1_out_l0)
    ya, yb = {}, {}
    for latent, t, blk0, n_batch, seq_len in streams:
        qa, kat, vat, qb, vb, ob, gi, gf, gt, kbt = _proj_even(
            x, t, blk0, mods0, latent, norm_l0[1], w_even, w_ob, w_g, gbi, gbf, gbt, qg, kg, bd, rope_tabs)
        if latent:
            extra = (to_t(cache_l0_attn_k), to_t(cache_l0_attn_v))
            m0 = state_l0_mlstm_m.reshape(nl, N_DIRHEAD)
            m0_lanes = jnp.zeros((nl, 8, 128), F32).at[:, 0, :N_DIRHEAD].set(m0)
            m0_rows = jnp.broadcast_to(m0[:, :, None], (nl, N_DIRHEAD, 128))
            n0_cols = jnp.broadcast_to(state_l0_mlstm_n[..., None], (nl, 2, N_HEADS, DH, 128))
            state = (state_l0_mlstm_C, n0_cols, m0_lanes, m0_rows)
        else:
            extra, state = None, None
            k0t, v0t = kat, vat
        ya[latent] = _attention(qa, kat, vat, n_batch=n_batch, q_len=seq_len, q_tile=256, group=4, extra=extra)
        ml = _mlstm(qb, kbt, vb, ob, gi, gf, gt, head_g, bd, n_batch=n_batch, seq_len=seq_len, state=state)
        yb[latent] = ml[0]
        if not latent:
            c0, n0, m0_pad = ml[1:]
    x = ffn(x, mods0, 6, norm_l0[2], ffn2_in_l0, ffn2_out_l0, mix=(ya[False], yb[False], ya[True], yb[True], mix_out_l0))

    x = ffn(x, mods1, 0, norm_l1[0], ffn1_in_l1, ffn1_out_l1)
    yc, yd = {}, {}
    for latent, t, blk0, n_batch, seq_len in streams:
        qc, kc, vc, qd, kd, vd = _proj_odd(x, t, blk0, mods1, latent, norm_l1[1], w_odd, rope_tabs)
        if latent:
            yc[latent] = _na_latent(qc, kc, vc, cache_l1_na_k.reshape(nl * CTX_LEN, 512),
                                    cache_l1_na_v.reshape(nl * CTX_LEN, 512), rpb_flat, nl)
            yd[latent] = _swa_latent(qd, kd, vd, to_t(cache_l1_swa_k), to_t(cache_l1_swa_v), sink_l1, nl, group=4)
        else:
            yc[latent], yd[latent] = _attention_two(qc, kc, vc, qd, kd, vd, sink_l1, n_batch=n_batch, q_len=seq_len)
            kc1t, vc1t, kd1t, vd1t = kc, vc, kd, vd
    y_prompt, y_sample = ffn(x, mods1, 6, norm_l1[2], ffn2_in_l1, ffn2_out_l1,
                             mix=(yc[False], yd[False], yc[True], yd[True], mix_out_l1), final_g=norm_final)

    return (y_prompt.reshape(nb, CTX_LEN, D_MODEL), y_sample.reshape(nl, LAT_LEN, D_MODEL),
            from_t(k0t), from_t(v0t),
            c0, n0, m0_pad[:, :, 0].reshape(nb, 2, N_HEADS),
            from_t(kc1t), from_t(vc1t),
            from_t(kd1t), from_t(vd1t))
```
